```python
import math
import jax, jax.numpy as jnp
from jax import lax
import numpy as np

D_MODEL = 1024
BATCH = 8
SEQ = 8192
DEPTH = 2

N_META = 16
N_A = DEPTH // 2
N_B = DEPTH - N_A
N_HEADS = 8
HEAD_DIM = D_MODEL // N_HEADS
D_FF = 2816
CONV_WIDTH = 3
BLOCK = 128
PAD = (-N_META) % BLOCK
LN_EPS = 1e-5
DEEPNORM_ALPHA = (2 * DEPTH) ** 0.25
DEEPNORM_BETA = (8 * DEPTH) ** -0.25
NEG_INF = -1e30

kernel_name = "yoco_shortconv_fox_macaron_deepnorm"


def layer_norm(x, g, b):
    xf = x.astype(jnp.float32)
    mu = jnp.mean(xf, axis=-1, keepdims=True)
    var = jnp.mean(jnp.square(xf - mu), axis=-1, keepdims=True)
    y = (xf - mu) * lax.rsqrt(var + LN_EPS) * g.astype(jnp.float32) + b.astype(jnp.float32)
    return y.astype(x.dtype)


def swiglu(x, wg, wu, wd):
    return (jax.nn.silu(x @ wg) * (x @ wu)) @ wd


def short_conv(x, w_in, w_conv, w_out):
    bgate, cgate, val = jnp.split(x @ w_in, 3, axis=-1)
    u = cgate * val
    y = lax.conv_general_dilated(
        u, w_conv[:, None, :].astype(u.dtype),
        window_strides=(1,), padding=[(CONV_WIDTH - 1, 0)],
        dimension_numbers=("NWC", "WIO", "NWC"),
        feature_group_count=D_MODEL)
    return (bgate * y) @ w_out


def shared_kv(h, kv_w, f_bias):
    bsz, L, _ = h.shape
    kvf = h @ kv_w
    k = kvf[..., :D_MODEL].reshape(bsz, L, N_HEADS, HEAD_DIM)
    v = kvf[..., D_MODEL:2 * D_MODEL].reshape(bsz, L, N_HEADS, HEAD_DIM)
    f_logit = (kvf[..., 2 * D_MODEL:] + f_bias).astype(jnp.float32)
    log_f = jax.nn.log_sigmoid(f_logit)
    pad4 = ((0, 0), (PAD, 0), (0, 0), (0, 0))
    k = jnp.pad(k, pad4).transpose(0, 2, 1, 3)
    v = jnp.pad(v, pad4).transpose(0, 2, 1, 3)
    log_f = jnp.pad(log_f, ((0, 0), (PAD, 0), (0, 0)))
    c = jnp.cumsum(log_f, axis=1).transpose(0, 2, 1)
    return k, v, c


def forgetting_attention(h, w_q, w_o, k, v, c):
    bsz, L, _ = h.shape
    Lp = L + PAD
    n_blocks = Lp // BLOCK
    scale = 1.0 / math.sqrt(HEAD_DIM)
    q = (h @ w_q).reshape(bsz, L, N_HEADS, HEAD_DIM)
    q = jnp.pad(q, ((0, 0), (PAD, 0), (0, 0), (0, 0))).transpose(0, 2, 1, 3)
    k_pos = jnp.arange(Lp)

    def one_block(i):
        start = i * BLOCK
        qb = lax.dynamic_slice_in_dim(q, start, BLOCK, axis=2)
        cq = lax.dynamic_slice_in_dim(c, start, BLOCK, axis=2)
        s = jnp.einsum("bhqd,bhkd->bhqk", qb, k).astype(jnp.float32) * scale
        s = s + cq[..., :, None] - c[..., None, :]
        q_pos = start + jnp.arange(BLOCK)
        mask = (k_pos[None, :] <= q_pos[:, None]) & (k_pos[None, :] >= PAD)
        s = jnp.where(mask, s, NEG_INF)
        p = jax.nn.softmax(s, axis=-1)
        return jnp.einsum("bhqk,bhkd->bhqd", p.astype(v.dtype), v)

    o = lax.map(one_block, jnp.arange(n_blocks))
    o = o.transpose(1, 0, 3, 2, 4).reshape(bsz, Lp, D_MODEL)[:, PAD:]
    return o @ w_o


def _fwd_setup_inputs(seed: int = 0) -> dict:
    key = jax.random.key(seed)
    ks = jax.random.split(key, 20)
    f32 = jnp.float32

    def nrm(k, shape, scale):
        return jax.random.normal(k, shape, f32) * scale

    d_s = D_MODEL ** -0.5
    f_s = D_FF ** -0.5
    x = nrm(ks[0], (BATCH, SEQ, D_MODEL), 1.0)
    meta = nrm(ks[1], (N_META, D_MODEL), 1.0)
    ffn1_wg = nrm(ks[2], (DEPTH, D_MODEL, D_FF), d_s)
    ffn1_wu = nrm(ks[3], (DEPTH, D_MODEL, D_FF), d_s)
    ffn1_wd = nrm(ks[4], (DEPTH, D_FF, D_MODEL), f_s * DEEPNORM_BETA)
    ffn2_wg = nrm(ks[5], (DEPTH, D_MODEL, D_FF), d_s)
    ffn2_wu = nrm(ks[6], (DEPTH, D_MODEL, D_FF), d_s)
    ffn2_wd = nrm(ks[7], (DEPTH, D_FF, D_MODEL), f_s * DEEPNORM_BETA)
    ln_gain = 1.0 + nrm(ks[8], (DEPTH, 3, D_MODEL), 0.02)
    ln_bias = nrm(ks[9], (DEPTH, 3, D_MODEL), 0.02)
    conv_w_in = nrm(ks[10], (N_A, D_MODEL, 3 * D_MODEL), d_s)
    conv_w = nrm(ks[11], (N_A, CONV_WIDTH, D_MODEL), CONV_WIDTH ** -0.5)
    conv_w_out = nrm(ks[12], (N_A, D_MODEL, D_MODEL), d_s * DEEPNORM_BETA)
    w_k = nrm(ks[13], (D_MODEL, D_MODEL), d_s)
    w_v = nrm(ks[14], (D_MODEL, D_MODEL), d_s * DEEPNORM_BETA)
    w_f = nrm(ks[15], (D_MODEL, N_HEADS), d_s * 0.5)
    kv_w = jnp.concatenate([w_k, w_v, w_f], axis=1)
    f_bias = 3.0 + nrm(ks[16], (N_HEADS,), 0.5)
    attn_w_q = nrm(ks[17], (N_B, D_MODEL, D_MODEL), d_s)
    attn_w_o = nrm(ks[18], (N_B, D_MODEL, D_MODEL), d_s * DEEPNORM_BETA)
    return {"x": x, "meta": meta,
            "ffn1_wg": ffn1_wg, "ffn1_wu": ffn1_wu, "ffn1_wd": ffn1_wd,
            "ffn2_wg": ffn2_wg, "ffn2_wu": ffn2_wu, "ffn2_wd": ffn2_wd,
            "ln_gain": ln_gain, "ln_bias": ln_bias,
            "conv_w_in": conv_w_in, "conv_w": conv_w, "conv_w_out": conv_w_out,
            "kv_w": kv_w, "f_bias": f_bias,
            "attn_w_q": attn_w_q, "attn_w_o": attn_w_o}


def _fwd_reference(x, meta, ffn1_wg, ffn1_wu, ffn1_wd, ffn2_wg, ffn2_wu, ffn2_wd,
              ln_gain, ln_bias, conv_w_in, conv_w, conv_w_out, kv_w, f_bias,
              attn_w_q, attn_w_o):
    bsz = x.shape[0]
    h = jnp.concatenate(
        [jnp.broadcast_to(meta.astype(x.dtype)[None], (bsz, N_META, D_MODEL)), x], axis=1)
    k_sh = v_sh = c_sh = None
    for l in range(DEPTH):
        h = layer_norm(DEEPNORM_ALPHA * h + 0.5 * swiglu(h, ffn1_wg[l], ffn1_wu[l], ffn1_wd[l]),
                       ln_gain[l, 0], ln_bias[l, 0])
        if l < N_A:
            mix = short_conv(h, conv_w_in[l], conv_w[l], conv_w_out[l])
        else:
            j = l - N_A
            mix = forgetting_attention(h, attn_w_q[j], attn_w_o[j], k_sh, v_sh, c_sh)
        h = layer_norm(DEEPNORM_ALPHA * h + mix, ln_gain[l, 1], ln_bias[l, 1])
        h = layer_norm(DEEPNORM_ALPHA * h + 0.5 * swiglu(h, ffn2_wg[l], ffn2_wu[l], ffn2_wd[l]),
                       ln_gain[l, 2], ln_bias[l, 2])
        if l == N_A - 1:
            k_sh, v_sh, c_sh = shared_kv(h, kv_w, f_bias)
    return h[:, N_META:]


import jax as _jax
import jax.numpy as _jnp

TWIN_FORMAT = 'train_step'
FWD_PARAMS = ['x', 'meta', 'ffn1_wg', 'ffn1_wu', 'ffn1_wd', 'ffn2_wg', 'ffn2_wu', 'ffn2_wd', 'ln_gain', 'ln_bias', 'conv_w_in', 'conv_w', 'conv_w_out', 'kv_w', 'f_bias', 'attn_w_q', 'attn_w_o']
TWIN_WEIGHTS = ['meta', 'ffn1_wg', 'ffn1_wu', 'ffn1_wd', 'ffn2_wg', 'ffn2_wu', 'ffn2_wd', 'ln_gain', 'ln_bias', 'conv_w_in', 'conv_w', 'conv_w_out', 'kv_w', 'f_bias', 'attn_w_q', 'attn_w_o']
TWIN_DIFF_INPUT = 'x'
TWIN_INPUTS = ['x', 'meta', 'ffn1_wg', 'ffn1_wu', 'ffn1_wd', 'ffn2_wg', 'ffn2_wu', 'ffn2_wd', 'ln_gain', 'ln_bias', 'conv_w_in', 'conv_w', 'conv_w_out', 'kv_w', 'f_bias', 'attn_w_q', 'attn_w_o', 'loss_target', 'm_meta', 'm_ffn1_wg', 'm_ffn1_wu', 'm_ffn1_wd', 'm_ffn2_wg', 'm_ffn2_wu', 'm_ffn2_wd', 'm_ln_gain', 'm_ln_bias', 'm_conv_w_in', 'm_conv_w', 'm_conv_w_out', 'm_kv_w', 'm_f_bias', 'm_attn_w_q', 'm_attn_w_o', 'v_meta', 'v_ffn1_wg', 'v_ffn1_wu', 'v_ffn1_wd', 'v_ffn2_wg', 'v_ffn2_wu', 'v_ffn2_wd', 'v_ln_gain', 'v_ln_bias', 'v_conv_w_in', 'v_conv_w', 'v_conv_w_out', 'v_kv_w', 'v_f_bias', 'v_attn_w_q', 'v_attn_w_o']
TWIN_OUTPUTS = ['loss', 'grad_x', 'grad_meta', 'grad_ffn1_wg', 'grad_ffn1_wu', 'grad_ffn1_wd', 'grad_ffn2_wg', 'grad_ffn2_wu', 'grad_ffn2_wd', 'grad_ln_gain', 'grad_ln_bias', 'grad_conv_w_in', 'grad_conv_w', 'grad_conv_w_out', 'grad_kv_w', 'grad_f_bias', 'grad_attn_w_q', 'grad_attn_w_o', 'delta_meta', 'delta_ffn1_wg', 'delta_ffn1_wu', 'delta_ffn1_wd', 'delta_ffn2_wg', 'delta_ffn2_wu', 'delta_ffn2_wd', 'delta_ln_gain', 'delta_ln_bias', 'delta_conv_w_in', 'delta_conv_w', 'delta_conv_w_out', 'delta_kv_w', 'delta_f_bias', 'delta_attn_w_q', 'delta_attn_w_o', 'new_m_meta', 'new_m_ffn1_wg', 'new_m_ffn1_wu', 'new_m_ffn1_wd', 'new_m_ffn2_wg', 'new_m_ffn2_wu', 'new_m_ffn2_wd', 'new_m_ln_gain', 'new_m_ln_bias', 'new_m_conv_w_in', 'new_m_conv_w', 'new_m_conv_w_out', 'new_m_kv_w', 'new_m_f_bias', 'new_m_attn_w_q', 'new_m_attn_w_o', 'new_v_meta', 'new_v_ffn1_wg', 'new_v_ffn1_wu', 'new_v_ffn1_wd', 'new_v_ffn2_wg', 'new_v_ffn2_wu', 'new_v_ffn2_wd', 'new_v_ln_gain', 'new_v_ln_bias', 'new_v_conv_w_in', 'new_v_conv_w', 'new_v_conv_w_out', 'new_v_kv_w', 'new_v_f_bias', 'new_v_attn_w_q', 'new_v_attn_w_o']
TWIN_LEAF_KINDS = {'loss': 'loss', 'grad_x': 'grad_x', 'grad_meta': 'grad_w', 'grad_ffn1_wg': 'grad_w', 'grad_ffn1_wu': 'grad_w', 'grad_ffn1_wd': 'grad_w', 'grad_ffn2_wg': 'grad_w', 'grad_ffn2_wu': 'grad_w', 'grad_ffn2_wd': 'grad_w', 'grad_ln_gain': 'grad_w', 'grad_ln_bias': 'grad_w', 'grad_conv_w_in': 'grad_w', 'grad_conv_w': 'grad_w', 'grad_conv_w_out': 'grad_w', 'grad_kv_w': 'grad_w', 'grad_f_bias': 'grad_w', 'grad_attn_w_q': 'grad_w', 'grad_attn_w_o': 'grad_w', 'delta_meta': 'delta_w', 'delta_ffn1_wg': 'delta_w', 'delta_ffn1_wu': 'delta_w', 'delta_ffn1_wd': 'delta_w', 'delta_ffn2_wg': 'delta_w', 'delta_ffn2_wu': 'delta_w', 'delta_ffn2_wd': 'delta_w', 'delta_ln_gain': 'delta_w', 'delta_ln_bias': 'delta_w', 'delta_conv_w_in': 'delta_w', 'delta_conv_w': 'delta_w', 'delta_conv_w_out': 'delta_w', 'delta_kv_w': 'delta_w', 'delta_f_bias': 'delta_w', 'delta_attn_w_q': 'delta_w', 'delta_attn_w_o': 'delta_w', 'new_m_meta': 'new_m', 'new_m_ffn1_wg': 'new_m', 'new_m_ffn1_wu': 'new_m', 'new_m_ffn1_wd': 'new_m', 'new_m_ffn2_wg': 'new_m', 'new_m_ffn2_wu': 'new_m', 'new_m_ffn2_wd': 'new_m', 'new_m_ln_gain': 'new_m', 'new_m_ln_bias': 'new_m', 'new_m_conv_w_in': 'new_m', 'new_m_conv_w': 'new_m', 'new_m_conv_w_out': 'new_m', 'new_m_kv_w': 'new_m', 'new_m_f_bias': 'new_m', 'new_m_attn_w_q': 'new_m', 'new_m_attn_w_o': 'new_m', 'new_v_meta': 'new_v', 'new_v_ffn1_wg': 'new_v', 'new_v_ffn1_wu': 'new_v', 'new_v_ffn1_wd': 'new_v', 'new_v_ffn2_wg': 'new_v', 'new_v_ffn2_wu': 'new_v', 'new_v_ffn2_wd': 'new_v', 'new_v_ln_gain': 'new_v', 'new_v_ln_bias': 'new_v', 'new_v_conv_w_in': 'new_v', 'new_v_conv_w': 'new_v', 'new_v_conv_w_out': 'new_v', 'new_v_kv_w': 'new_v', 'new_v_f_bias': 'new_v', 'new_v_attn_w_q': 'new_v', 'new_v_attn_w_o': 'new_v'}


def _forward(args):
    return _fwd_reference(*[args[k] for k in FWD_PARAMS])


def _output_shape():
    def fwd():
        inp = _fwd_setup_inputs(0)
        return _fwd_reference(*[inp[k] for k in FWD_PARAMS])
    out = _jax.eval_shape(fwd)
    return out.shape, out.dtype

N_MICROBATCH = 1
ADAM_LR = 0.001
ADAM_B1 = 0.9
ADAM_B2 = 0.999
ADAM_EPS = 1e-08
ADAM_WD = 0.01
ADAM_STEP = 10
PER_EXAMPLE_BATCH_AXIS = {'x': 0, 'loss_target': 0}
SHARED_INPUTS = []
_WEIGHT_DTYPES = {'meta': _jnp.float32, 'ffn1_wg': _jnp.float32, 'ffn1_wu': _jnp.float32, 'ffn1_wd': _jnp.float32, 'ffn2_wg': _jnp.float32, 'ffn2_wu': _jnp.float32, 'ffn2_wd': _jnp.float32, 'ln_gain': _jnp.float32, 'ln_bias': _jnp.float32, 'conv_w_in': _jnp.float32, 'conv_w': _jnp.float32, 'conv_w_out': _jnp.float32, 'kv_w': _jnp.float32, 'f_bias': _jnp.float32, 'attn_w_q': _jnp.float32, 'attn_w_o': _jnp.float32}
MOMENT_SCALE = {'meta': 1.931412e-03, 'ffn1_wg': 1.743039e-02, 'ffn1_wu': 1.687885e-02, 'ffn1_wd': 5.600281e-02, 'ffn2_wg': 1.646940e-02, 'ffn2_wu': 1.596226e-02, 'ffn2_wd': 5.306398e-02, 'ln_gain': 2.625985e+01, 'ln_bias': 1.416781e+00, 'conv_w_in': 8.396252e-02, 'conv_w': 8.496935e-02, 'conv_w_out': 1.676336e-01, 'kv_w': 1.967252e-02, 'f_bias': 1.744529e-01, 'attn_w_q': 1.084443e-02, 'attn_w_o': 2.562877e-02}


def _to_microbatches(a, axis):
    t = _jnp.moveaxis(a, axis, 0)
    t = t.reshape((N_MICROBATCH, t.shape[0] // N_MICROBATCH) + t.shape[1:])
    return _jnp.moveaxis(t, 1, axis + 1)


def setup_inputs(seed: int = 0) -> dict:
    inp = _fwd_setup_inputs(seed)
    key = _jax.random.fold_in(_jax.random.key(seed), 7919)
    shape, _ = _output_shape()
    out = dict(inp)
    out["loss_target"] = _jax.random.normal(_jax.random.fold_in(key, 0), shape, _jnp.float32)
    for i, name in enumerate(TWIN_WEIGHTS):
        w = inp[name].astype(_jnp.float32)
        if MOMENT_SCALE is None:
            s = _jnp.sqrt(_jnp.mean(_jnp.square(w)) + 1e-30)
        else:
            s = MOMENT_SCALE[name]
        km, kv = _jax.random.split(_jax.random.fold_in(key, i + 1))
        out[name] = w
        out["m_" + name] = s * _jax.random.normal(km, w.shape, _jnp.float32)
        out["v_" + name] = (s * s) * _jax.random.uniform(kv, w.shape, _jnp.float32, 0.5, 1.5)
    if N_MICROBATCH > 1:
        for name, axis in PER_EXAMPLE_BATCH_AXIS.items():
            out[name] = _to_microbatches(out[name], axis)
    return {'x': out['x'], 'meta': out['meta'], 'ffn1_wg': out['ffn1_wg'], 'ffn1_wu': out['ffn1_wu'], 'ffn1_wd': out['ffn1_wd'], 'ffn2_wg': out['ffn2_wg'], 'ffn2_wu': out['ffn2_wu'], 'ffn2_wd': out['ffn2_wd'], 'ln_gain': out['ln_gain'], 'ln_bias': out['ln_bias'], 'conv_w_in': out['conv_w_in'], 'conv_w': out['conv_w'], 'conv_w_out': out['conv_w_out'], 'kv_w': out['kv_w'], 'f_bias': out['f_bias'], 'attn_w_q': out['attn_w_q'], 'attn_w_o': out['attn_w_o'], 'loss_target': out['loss_target'], 'm_meta': out['m_meta'], 'm_ffn1_wg': out['m_ffn1_wg'], 'm_ffn1_wu': out['m_ffn1_wu'], 'm_ffn1_wd': out['m_ffn1_wd'], 'm_ffn2_wg': out['m_ffn2_wg'], 'm_ffn2_wu': out['m_ffn2_wu'], 'm_ffn2_wd': out['m_ffn2_wd'], 'm_ln_gain': out['m_ln_gain'], 'm_ln_bias': out['m_ln_bias'], 'm_conv_w_in': out['m_conv_w_in'], 'm_conv_w': out['m_conv_w'], 'm_conv_w_out': out['m_conv_w_out'], 'm_kv_w': out['m_kv_w'], 'm_f_bias': out['m_f_bias'], 'm_attn_w_q': out['m_attn_w_q'], 'm_attn_w_o': out['m_attn_w_o'], 'v_meta': out['v_meta'], 'v_ffn1_wg': out['v_ffn1_wg'], 'v_ffn1_wu': out['v_ffn1_wu'], 'v_ffn1_wd': out['v_ffn1_wd'], 'v_ffn2_wg': out['v_ffn2_wg'], 'v_ffn2_wu': out['v_ffn2_wu'], 'v_ffn2_wd': out['v_ffn2_wd'], 'v_ln_gain': out['v_ln_gain'], 'v_ln_bias': out['v_ln_bias'], 'v_conv_w_in': out['v_conv_w_in'], 'v_conv_w': out['v_conv_w'], 'v_conv_w_out': out['v_conv_w_out'], 'v_kv_w': out['v_kv_w'], 'v_f_bias': out['v_f_bias'], 'v_attn_w_q': out['v_attn_w_q'], 'v_attn_w_o': out['v_attn_w_o']}


def _loss(weights, diff, rest, loss_target):
    with _jax.named_scope("forward"):
        args = {**rest, TWIN_DIFF_INPUT: diff, **{k: w.astype(_WEIGHT_DTYPES[k]) for k, w in weights.items()}}
        y = _forward(args)
    with _jax.named_scope("loss_head"):
        err = _jnp.square(y.astype(_jnp.float32) - loss_target)
        return 0.5 * _jnp.sum(_jnp.mean(err, axis=-1)) if err.ndim else 0.5 * err


def _adamw(w, g, m, v):
    m = ADAM_B1 * m + (1.0 - ADAM_B1) * g
    v = ADAM_B2 * v + (1.0 - ADAM_B2) * _jnp.square(g)
    m_hat = m / (1.0 - ADAM_B1 ** ADAM_STEP)
    v_hat = v / (1.0 - ADAM_B2 ** ADAM_STEP)
    delta = -ADAM_LR * (m_hat / (_jnp.sqrt(v_hat) + ADAM_EPS) + ADAM_WD * w)
    return delta, m, v


def reference(x, meta, ffn1_wg, ffn1_wu, ffn1_wd, ffn2_wg, ffn2_wu, ffn2_wd, ln_gain, ln_bias, conv_w_in, conv_w, conv_w_out, kv_w, f_bias, attn_w_q, attn_w_o, loss_target, m_meta, m_ffn1_wg, m_ffn1_wu, m_ffn1_wd, m_ffn2_wg, m_ffn2_wu, m_ffn2_wd, m_ln_gain, m_ln_bias, m_conv_w_in, m_conv_w, m_conv_w_out, m_kv_w, m_f_bias, m_attn_w_q, m_attn_w_o, v_meta, v_ffn1_wg, v_ffn1_wu, v_ffn1_wd, v_ffn2_wg, v_ffn2_wu, v_ffn2_wd, v_ln_gain, v_ln_bias, v_conv_w_in, v_conv_w, v_conv_w_out, v_kv_w, v_f_bias, v_attn_w_q, v_attn_w_o):
    given = dict(x=x, meta=meta, ffn1_wg=ffn1_wg, ffn1_wu=ffn1_wu, ffn1_wd=ffn1_wd, ffn2_wg=ffn2_wg, ffn2_wu=ffn2_wu, ffn2_wd=ffn2_wd, ln_gain=ln_gain, ln_bias=ln_bias, conv_w_in=conv_w_in, conv_w=conv_w, conv_w_out=conv_w_out, kv_w=kv_w, f_bias=f_bias, attn_w_q=attn_w_q, attn_w_o=attn_w_o, loss_target=loss_target, m_meta=m_meta, m_ffn1_wg=m_ffn1_wg, m_ffn1_wu=m_ffn1_wu, m_ffn1_wd=m_ffn1_wd, m_ffn2_wg=m_ffn2_wg, m_ffn2_wu=m_ffn2_wu, m_ffn2_wd=m_ffn2_wd, m_ln_gain=m_ln_gain, m_ln_bias=m_ln_bias, m_conv_w_in=m_conv_w_in, m_conv_w=m_conv_w, m_conv_w_out=m_conv_w_out, m_kv_w=m_kv_w, m_f_bias=m_f_bias, m_attn_w_q=m_attn_w_q, m_attn_w_o=m_attn_w_o, v_meta=v_meta, v_ffn1_wg=v_ffn1_wg, v_ffn1_wu=v_ffn1_wu, v_ffn1_wd=v_ffn1_wd, v_ffn2_wg=v_ffn2_wg, v_ffn2_wu=v_ffn2_wu, v_ffn2_wd=v_ffn2_wd, v_ln_gain=v_ln_gain, v_ln_bias=v_ln_bias, v_conv_w_in=v_conv_w_in, v_conv_w=v_conv_w, v_conv_w_out=v_conv_w_out, v_kv_w=v_kv_w, v_f_bias=v_f_bias, v_attn_w_q=v_attn_w_q, v_attn_w_o=v_attn_w_o)
    weights = {n: given[n] for n in TWIN_WEIGHTS}
    shared = {n: given[n] for n in SHARED_INPUTS}
    per_example = {n: given[n] for n in ['x']}
    grad_fn = _jax.value_and_grad(_loss, argnums=(0, 1))

    def one_microbatch(ex, loss_target):
        ex = dict(ex)
        diff = ex.pop(TWIN_DIFF_INPUT)
        return grad_fn(weights, diff, {**shared, **ex}, loss_target)

    if N_MICROBATCH == 1:
        loss, (grad_w, grad_x) = one_microbatch(per_example, given["loss_target"])
    else:
        def body(carry, xs):
            loss_sum, grad_sum = carry
            l_k, (gw_k, gx_k) = one_microbatch(xs[0], xs[1])
            with _jax.named_scope("update"):
                return (loss_sum + l_k, _jax.tree.map(_jnp.add, grad_sum, gw_k)), gx_k

        init = (_jnp.zeros((), _jnp.float32), _jax.tree.map(_jnp.zeros_like, weights))
        (loss, grad_w), grad_x = _jax.lax.scan(body, init, (per_example, given["loss_target"]))
    with _jax.named_scope("update"):
        delta_w, new_m, new_v = {}, {}, {}
        for n in TWIN_WEIGHTS:
            delta_w[n], new_m[n], new_v[n] = _adamw(weights[n], grad_w[n], given["m_" + n], given["v_" + n])
    return (loss, grad_x, *[grad_w[n] for n in TWIN_WEIGHTS], *[delta_w[n] for n in TWIN_WEIGHTS],
            *[new_m[n] for n in TWIN_WEIGHTS], *[new_v[n] for n in TWIN_WEIGHTS])
```

```python
import functools
import math

import jax
import jax.numpy as jnp
from jax import lax
from jax.experimental import pallas as pl
from jax.experimental.pallas import tpu as pltpu

f32 = jnp.float32
bf16 = jnp.bfloat16

N_META = 16
PAD = 112
HEAD_DIM = 128
DEPTH = 2
LN_EPS = 1e-5
ALPHA = (2 * DEPTH) ** 0.25
NEG_INF = -1e30
N_CHIPS = 4
SMALL_ROWS = 48
LANES = 128

ADAM_LR = 0.001
ADAM_B1 = 0.9
ADAM_B2 = 0.999
ADAM_EPS = 1e-08
ADAM_WD = 0.01
ADAM_STEP = 10

VMEM_LIMIT_BYTES = 56 * 1024 * 1024
MESH = pl.DeviceIdType.MESH

NT_DIMS = (((1,), (1,)), ((), ()))
TN_DIMS = (((0,), (0,)), ((), ()))


def _tile(n, target, mult):
    best = None
    for d in range(mult, min(n, target) + 1, mult):
        if n % d == 0:
            best = d
    assert best is not None, (n, target, mult)
    return best


def _params(*sem):
    return pltpu.CompilerParams(dimension_semantics=sem, vmem_limit_bytes=VMEM_LIMIT_BYTES)


def _embed(meta_pad, x, name):
    seq, d = x.shape
    t = seq + LANES

    def body(m_ref, x_ref, h_ref, hb_ref):
        first = pl.program_id(0) == 0
        v = jnp.where(first, m_ref[...], x_ref[...])
        h_ref[...] = v
        hb_ref[...] = v.astype(bf16)

    return pl.pallas_call(
        body, name=name, grid=(t // LANES,),
        in_specs=[pl.BlockSpec((LANES, d), lambda i: (0, 0)),
                  pl.BlockSpec((LANES, d), lambda i: (jnp.maximum(i - 1, 0), 0))],
        out_specs=[pl.BlockSpec((LANES, d), lambda i: (i, 0)),
                   pl.BlockSpec((LANES, d), lambda i: (i, 0))],
        out_shape=[jax.ShapeDtypeStruct((t, d), f32), jax.ShapeDtypeStruct((t, d), bf16)],
        compiler_params=_params("parallel"),
    )(meta_pad, x)


def _nn_matmul(x, w, out_dtype, name):
    t, k = x.shape
    s_n, _, n = w.shape
    assert s_n == 1 or n % LANES == 0
    tm = _tile(t, 640, 16)

    def body(x_ref, w_ref, o_ref):
        o_ref[...] = jnp.dot(x_ref[...].astype(bf16), w_ref[...],
                             preferred_element_type=f32).astype(o_ref.dtype)

    return pl.pallas_call(
        body, name=name, grid=(s_n, t // tm),
        in_specs=[pl.BlockSpec((tm, k), lambda s, i: (i, 0)),
                  pl.BlockSpec((None, k, n), lambda s, i: (s, 0, 0))],
        out_specs=pl.BlockSpec((tm, n), lambda s, i: (i, s)),
        out_shape=jax.ShapeDtypeStruct((t, s_n * n), out_dtype),
        compiler_params=_params("parallel", "parallel"),
    )(x, w)


def _ffn_up(hb, wg, wu, layer, name):
    t, d = hb.shape
    s_n, _, _, n = wg.shape
    tm = _tile(t, 640, 16)

    def body(x_ref, wg_ref, wu_ref, a_ref, b_ref, s_ref):
        x = x_ref[...]
        a = jnp.dot(x, wg_ref[...], preferred_element_type=f32)
        b = jnp.dot(x, wu_ref[...], preferred_element_type=f32)
        a_ref[...] = a
        b_ref[...] = b
        s_ref[...] = (a * jax.nn.sigmoid(a) * b).astype(bf16)

    wspec = pl.BlockSpec((None, None, d, n), lambda s, i: (s, layer, 0, 0))
    ospec = pl.BlockSpec((None, tm, n), lambda s, i: (s, i, 0))
    return pl.pallas_call(
        body, name=name, grid=(s_n, t // tm),
        in_specs=[pl.BlockSpec((tm, d), lambda s, i: (i, 0)), wspec, wspec],
        out_specs=[ospec, ospec, ospec],
        out_shape=[jax.ShapeDtypeStruct((s_n, t, n), f32), jax.ShapeDtypeStruct((s_n, t, n), f32),
                   jax.ShapeDtypeStruct((s_n, t, n), bf16)],
        compiler_params=_params("parallel", "parallel"),
    )(hb, wg, wu)


def _down_ln(x, w, layer, hprev, gain, bias, beta, name):
    s_n, t, k = x.shape
    d = w.shape[-1]
    tm = _tile(t, 640, 16)

    def body(x_ref, w_ref, h_ref, g_ref, b_ref, r_out, h_out, hb_out, acc):
        s = pl.program_id(1)

        @pl.when(s == 0)
        def _():
            acc[...] = jnp.zeros_like(acc)

        acc[...] += jnp.dot(x_ref[...], w_ref[...], preferred_element_type=f32)

        @pl.when(s == s_n - 1)
        def _():
            r = ALPHA * h_ref[...] + beta * acc[...]
            mu = jnp.mean(r, axis=-1, keepdims=True)
            xc = r - mu
            var = jnp.mean(xc * xc, axis=-1, keepdims=True)
            y = xc * lax.rsqrt(var + LN_EPS) * g_ref[...] + b_ref[...]
            r_out[...] = r
            h_out[...] = y
            hb_out[...] = y.astype(bf16)

    row = pl.BlockSpec((tm, d), lambda i, s: (i, 0))
    vec = pl.BlockSpec((1, d), lambda i, s: (0, 0))
    return pl.pallas_call(
        body, name=name, grid=(t // tm, s_n),
        in_specs=[pl.BlockSpec((None, tm, k), lambda i, s: (s, i, 0)),
                  pl.BlockSpec((None, None, k, d), lambda i, s: (s, layer, 0, 0)),
                  row, vec, vec],
        out_specs=[row, row, row],
        out_shape=[jax.ShapeDtypeStruct((t, d), f32), jax.ShapeDtypeStruct((t, d), f32),
                   jax.ShapeDtypeStruct((t, d), bf16)],
        scratch_shapes=[pltpu.VMEM((tm, d), f32)],
        compiler_params=_params("parallel", "arbitrary"),
    )(x, w, hprev, gain, bias)


def _conv_fwd(p, conv_w, name):
    t, d3 = p.shape
    d = d3 // 3
    tm = _tile(t, 320, 8)
    hb = tm // 8

    def body(p_ref, prev_ref, w_ref, z_ref):
        i = pl.program_id(0)
        rows = i * tm - 8 + lax.broadcasted_iota(jnp.int32, (tm + 8, 1), 0)
        cg = jnp.concatenate([prev_ref[:, d:2 * d], p_ref[:, d:2 * d]], axis=0)
        val = jnp.concatenate([prev_ref[:, 2 * d:], p_ref[:, 2 * d:]], axis=0)
        u = jnp.where(rows >= PAD, cg * val, 0.0)
        y = (w_ref[2:3, :] * u + w_ref[1:2, :] * pltpu.roll(u, 1, 0)
             + w_ref[0:1, :] * pltpu.roll(u, 2, 0))
        z_ref[...] = (p_ref[:, :d] * y[8:]).astype(bf16)

    return pl.pallas_call(
        body, name=name, grid=(t // tm,),
        in_specs=[pl.BlockSpec((tm, d3), lambda i: (i, 0)),
                  pl.BlockSpec((8, d3), lambda i: (jnp.maximum(i * hb - 1, 0), 0)),
                  pl.BlockSpec((3, d), lambda i: (0, 0))],
        out_specs=pl.BlockSpec((tm, d), lambda i: (i, 0)),
        out_shape=jax.ShapeDtypeStruct((t, d), bf16),
        compiler_params=_params("parallel"),
    )(p, p, conv_w)


def _row_scan(x, name, fbias=None):
    t, n = x.shape
    blk = LANES
    gate = fbias is not None

    def body(*refs):
        if gate:
            x_ref, fb_ref, o_ref, carry = refs
        else:
            x_ref, o_ref, carry = refs
        i = pl.program_id(0)

        @pl.when(i == 0)
        def _():
            carry[...] = jnp.zeros_like(carry)

        v = x_ref[...]
        r = lax.broadcasted_iota(jnp.int32, (blk, n), 0)
        if gate:
            v = v + fb_ref[...]
            v = jnp.minimum(v, 0.0) - jnp.log1p(jnp.exp(-jnp.abs(v)))
            v = jnp.where(i * blk + r >= PAD, v, 0.0)
        sh = 1
        while sh < blk:
            v = v + jnp.where(r >= sh, pltpu.roll(v, sh, 0), 0.0)
            sh *= 2
        v = v + carry[...]
        o_ref[...] = v
        carry[...] = o_ref[blk - 1:blk, :]

    in_specs = [pl.BlockSpec((blk, n), lambda i: (i, 0))]
    args = [x]
    if gate:
        in_specs.append(pl.BlockSpec((1, n), lambda i: (0, 0)))
        args.append(fbias)
    return pl.pallas_call(
        body, name=name, grid=(t // blk,),
        in_specs=in_specs,
        out_specs=pl.BlockSpec((blk, n), lambda i: (i, 0)),
        out_shape=jax.ShapeDtypeStruct((t, n), f32),
        scratch_shapes=[pltpu.VMEM((1, n), f32)],
        compiler_params=_params("arbitrary"),
    )(*args)


def _attn_fwd(q, k, v, ccol, crow, name):
    t, d = q.shape
    n_heads = d // HEAD_DIM
    bk = crow.shape[-1]
    bq = bk
    scale = 1.0 / math.sqrt(HEAD_DIM)

    def body(q_ref, k_ref, v_ref, cc_ref, cr_ref, o_ref, o32_ref, lse_ref):
        i = pl.program_id(1)
        qb = q_ref[...]
        cq = cc_ref[...]
        row = i * bq + lax.broadcasted_iota(jnp.int32, (bq, bk), 0)
        col = lax.broadcasted_iota(jnp.int32, (bq, bk), 1)

        def step(j, carry):
            m, l, acc = carry
            off = pl.multiple_of(j * bk, bk)
            kj = k_ref[pl.ds(off, bk), :]
            vj = v_ref[pl.ds(off, bk), :]
            s = lax.dot_general(qb, kj, NT_DIMS, preferred_element_type=f32) * scale
            s = s + cq - cr_ref[j]
            s = jnp.where(col + j * bk <= row, s, NEG_INF)
            m_new = jnp.maximum(m, jnp.max(s, axis=-1, keepdims=True))
            a = jnp.exp(m - m_new)
            pr = jnp.exp(s - m_new)
            l = a * l + jnp.sum(pr, axis=-1, keepdims=True)
            acc = a * acc + jnp.dot(pr.astype(bf16), vj, preferred_element_type=f32)
            return m_new, l, acc

        init = (jnp.full((bq, 1), NEG_INF, f32), jnp.zeros((bq, 1), f32),
                jnp.zeros((bq, HEAD_DIM), f32))
        m, l, acc = lax.fori_loop(0, i + 1, step, init)
        out = acc / l
        o_ref[...] = out.astype(bf16)
        o32_ref[...] = out
        lse_ref[...] = m + jnp.log(l)

    head_rows = pl.BlockSpec((t, HEAD_DIM), lambda h, i: (0, h))
    qblk = pl.BlockSpec((bq, HEAD_DIM), lambda h, i: (i, h))
    return pl.pallas_call(
        body, name=name, grid=(n_heads, t // bq),
        in_specs=[qblk, head_rows, head_rows,
                  pl.BlockSpec((None, bq, 1), lambda h, i: (h, i, 0)),
                  pl.BlockSpec((None, t // bk, 1, bk), lambda h, i: (h, 0, 0, 0))],
        out_specs=[qblk, qblk, pl.BlockSpec((None, bq, 1), lambda h, i: (h, i, 0))],
        out_shape=[jax.ShapeDtypeStruct((t, d), bf16), jax.ShapeDtypeStruct((t, d), f32),
                   jax.ShapeDtypeStruct((n_heads, t, 1), f32)],
        compiler_params=_params("parallel", "parallel"),
    )(q, k, v, ccol, crow)


def _loss_head(h, target, name):
    t, d = h.shape

    def body(h_ref, t_ref, dy_ref, loss_ref):
        i = pl.program_id(0)

        @pl.when(i == 0)
        def _():
            loss_ref[...] = jnp.zeros_like(loss_ref)

        diff = jnp.where(i >= 1, h_ref[...] - t_ref[...], 0.0)
        dy_ref[...] = diff * (1.0 / d)
        loss_ref[...] += jnp.sum(diff * diff)

    return pl.pallas_call(
        body, name=name, grid=(t // LANES,),
        in_specs=[pl.BlockSpec((LANES, d), lambda i: (i, 0)),
                  pl.BlockSpec((LANES, d), lambda i: (jnp.maximum(i - 1, 0), 0))],
        out_specs=[pl.BlockSpec((LANES, d), lambda i: (i, 0)),
                   pl.BlockSpec((1, LANES), lambda i: (0, 0))],
        out_shape=[jax.ShapeDtypeStruct((t, d), f32), jax.ShapeDtypeStruct((1, LANES), f32)],
        compiler_params=_params("arbitrary"),
    )(h, target)


def _ln_bwd(dh, r, gain, name):
    t, d = r.shape
    tm = _tile(t, 640, 16)

    def body(dh_ref, r_ref, g_ref, dr_ref, drb_ref, dg_ref, db_ref):
        @pl.when(pl.program_id(0) == 0)
        def _():
            dg_ref[...] = jnp.zeros_like(dg_ref)
            db_ref[...] = jnp.zeros_like(db_ref)

        rr = r_ref[...]
        mu = jnp.mean(rr, axis=-1, keepdims=True)
        xc = rr - mu
        var = jnp.mean(xc * xc, axis=-1, keepdims=True)
        rstd = lax.rsqrt(var + LN_EPS)
        xhat = xc * rstd
        dy = dh_ref[...]
        dxh = dy * g_ref[...]
        m1 = jnp.mean(dxh, axis=-1, keepdims=True)
        m2 = jnp.mean(dxh * xhat, axis=-1, keepdims=True)
        dr = rstd * (dxh - m1 - xhat * m2)
        dr_ref[...] = dr
        drb_ref[...] = dr.astype(bf16)
        dg_ref[...] += jnp.sum(dy * xhat, axis=0, keepdims=True)
        db_ref[...] += jnp.sum(dy, axis=0, keepdims=True)

    row = pl.BlockSpec((tm, d), lambda i: (i, 0))
    vec = pl.BlockSpec((1, d), lambda i: (0, 0))
    return pl.pallas_call(
        body, name=name, grid=(t // tm,),
        in_specs=[row, row, vec],
        out_specs=[row, row, vec, vec],
        out_shape=[jax.ShapeDtypeStruct((t, d), f32), jax.ShapeDtypeStruct((t, d), bf16),
                   jax.ShapeDtypeStruct((1, d), f32), jax.ShapeDtypeStruct((1, d), f32)],
        compiler_params=_params("arbitrary"),
    )(dh, r, gain)


def _ffn_bwd_act(drb, wd, layer, a, b, name):
    t, d = drb.shape
    s_n, _, n = a.shape
    tm = _tile(t, 640, 16)

    def body(dr_ref, w_ref, a_ref, b_ref, da_ref, db_ref):
        ds = 0.5 * lax.dot_general(dr_ref[...], w_ref[...], NT_DIMS, preferred_element_type=f32)
        av = a_ref[...]
        sig = jax.nn.sigmoid(av)
        da_ref[...] = (ds * b_ref[...] * (sig * (1.0 + av * (1.0 - sig)))).astype(bf16)
        db_ref[...] = (ds * (av * sig)).astype(bf16)

    act = pl.BlockSpec((None, tm, n), lambda s, i: (s, i, 0))
    return pl.pallas_call(
        body, name=name, grid=(s_n, t // tm),
        in_specs=[pl.BlockSpec((tm, d), lambda s, i: (i, 0)),
                  pl.BlockSpec((None, None, n, d), lambda s, i: (s, layer, 0, 0)), act, act],
        out_specs=[act, act],
        out_shape=[jax.ShapeDtypeStruct((s_n, t, n), bf16), jax.ShapeDtypeStruct((s_n, t, n), bf16)],
        compiler_params=_params("parallel", "parallel"),
    )(drb, wd, a, b)


def _act_spec(mode, tt, k, t_first):
    def fix(fn):
        return (lambda i, s: fn(s, i)) if t_first else fn
    if mode == "shared":
        return pl.BlockSpec((tt, k), fix(lambda s, i: (i, 0)))
    if mode == "cols":
        return pl.BlockSpec((tt, k), fix(lambda s, i: (i, s)))
    assert mode == "stack"
    return pl.BlockSpec((None, tt, k), fix(lambda s, i: (s, i, 0)))


def _act_width(arr, mode, s_n):
    return arr.shape[-1] // s_n if mode == "cols" else arr.shape[-1]


def _tn_matmul(x, xmode, ys, ymodes, s_n, scale, name, layers=None, layer=0, into=None):
    t = x.shape[-2]
    kx = _act_width(x, xmode, s_n)
    kys = [_act_width(y, m, s_n) for y, m in zip(ys, ymodes)]
    tt = _tile(t, 640, 16)
    n_t = t // tt
    n_y = len(ys)
    n_alias = 0 if into is None else n_y

    def body(*refs):
        x_ref = refs[0]
        y_refs = refs[1:1 + n_y]
        o_refs = refs[1 + n_y + n_alias:]
        i = pl.program_id(1)
        xv = x_ref[...].astype(bf16)
        for y_ref, o_ref in zip(y_refs, o_refs):
            part = lax.dot_general(xv, y_ref[...].astype(bf16), TN_DIMS, preferred_element_type=f32)

            @pl.when(i == 0)
            def _():
                o_ref[...] = part

            @pl.when(i > 0)
            def _():
                o_ref[...] += part

            if scale != 1.0:
                @pl.when(i == n_t - 1)
                def _():
                    o_ref[...] = o_ref[...] * scale

    in_specs = [_act_spec(xmode, tt, kx, False)]
    in_specs += [_act_spec(m, tt, ky, False) for m, ky in zip(ymodes, kys)]
    if layers is None:
        out_specs = [pl.BlockSpec((None, kx, ky), lambda s, i: (s, 0, 0)) for ky in kys]
        out_shape = [jax.ShapeDtypeStruct((s_n, kx, ky), f32) for ky in kys]
    else:
        out_specs = [pl.BlockSpec((None, None, kx, ky), lambda s, i: (s, layer, 0, 0)) for ky in kys]
        out_shape = [jax.ShapeDtypeStruct((s_n, layers, kx, ky), f32) for ky in kys]
    args = [x, *ys]
    aliases = {}
    if into is not None:
        in_specs += [pl.BlockSpec(memory_space=pl.ANY)] * n_y
        args += list(into)
        aliases = {1 + n_y + j: j for j in range(n_y)}
    return pl.pallas_call(
        body, name=name, grid=(s_n, n_t),
        in_specs=in_specs, out_specs=out_specs, out_shape=out_shape,
        input_output_aliases=aliases,
        compiler_params=_params("parallel", "arbitrary"),
    )(*args)


def _nt_sum(pairs, base, base_scale, s_n, out_dtype, name):
    t = pairs[0][0].shape[-2]
    d = pairs[0][2].shape[-2]
    tm = _tile(t, 640, 16)
    n_p = len(pairs)
    has_base = base is not None

    def body(*refs):
        dy_refs = refs[0:2 * n_p:2]
        w_refs = refs[1:2 * n_p:2]
        rest = refs[2 * n_p:]
        base_ref = rest[0] if has_base else None
        o_ref, acc = rest[-2], rest[-1]
        s = pl.program_id(1)

        @pl.when(s == 0)
        def _():
            acc[...] = jnp.zeros_like(acc)

        tot = None
        for dy_ref, w_ref in zip(dy_refs, w_refs):
            part = lax.dot_general(dy_ref[...].astype(bf16), w_ref[...], NT_DIMS,
                                   preferred_element_type=f32)
            tot = part if tot is None else tot + part
        acc[...] += tot

        @pl.when(s == s_n - 1)
        def _():
            res = acc[...]
            if has_base:
                res = base_scale * base_ref[...] + res
            o_ref[...] = res.astype(o_ref.dtype)

    in_specs, args = [], []
    for dy, mode, w, layer in pairs:
        k = _act_width(dy, mode, s_n)
        in_specs.append(_act_spec(mode, tm, k, True))
        if layer is None:
            in_specs.append(pl.BlockSpec((None, d, k), lambda i, s: (s, 0, 0)))
        else:
            in_specs.append(pl.BlockSpec((None, None, d, k),
                                         functools.partial(lambda i, s, l: (s, l, 0, 0), l=layer)))
        args += [dy, w]
    row = pl.BlockSpec((tm, d), lambda i, s: (i, 0))
    if has_base:
        in_specs.append(row)
        args.append(base)
    return pl.pallas_call(
        body, name=name, grid=(t // tm, s_n),
        in_specs=in_specs, out_specs=row,
        out_shape=jax.ShapeDtypeStruct((t, d), out_dtype),
        scratch_shapes=[pltpu.VMEM((tm, d), f32)],
        compiler_params=_params("parallel", "arbitrary"),
    )(*args)


def _conv_bwd(dz, p, conv_w, name):
    t, d3 = p.shape
    d = d3 // 3
    tm = _tile(t, 320, 8)
    hb = tm // 8
    last8 = t // 8 - 1
    n_ext = tm + 8

    def body(dz_ref, dzn_ref, p_ref, pp_ref, pn_ref, w_ref, dp_ref, dw_ref):
        i = pl.program_id(0)

        @pl.when(i == 0)
        def _():
            dw_ref[...] = jnp.zeros_like(dw_ref)

        w0, w1, w2 = w_ref[0:1, :], w_ref[1:2, :], w_ref[2:3, :]
        rows_u = i * tm - 8 + lax.broadcasted_iota(jnp.int32, (n_ext, 1), 0)
        cg = jnp.concatenate([pp_ref[:, d:2 * d], p_ref[:, d:2 * d]], axis=0)
        val = jnp.concatenate([pp_ref[:, 2 * d:], p_ref[:, 2 * d:]], axis=0)
        u = jnp.where(rows_u >= PAD, cg * val, 0.0)
        u1 = pltpu.roll(u, 1, 0)
        u2 = pltpu.roll(u, 2, 0)
        y = (w2 * u + w1 * u1 + w0 * u2)[8:]
        dzv = dz_ref[...]
        bg = p_ref[:, :d]
        rows_n = (i + 1) * tm + lax.broadcasted_iota(jnp.int32, (8, 1), 0)
        dy_main = dzv * bg
        dy_next = jnp.where(rows_n < t, dzn_ref[...] * pn_ref[:, :d], 0.0)
        dye = jnp.concatenate([dy_main, dy_next], axis=0)
        du = (w2 * dye + w1 * pltpu.roll(dye, n_ext - 1, 0)
              + w0 * pltpu.roll(dye, n_ext - 2, 0))[:tm]
        du = jnp.where(rows_u[8:] >= PAD, du, 0.0)
        dp_ref[:, :d] = (dzv * y).astype(bf16)
        dp_ref[:, d:2 * d] = (du * val[8:]).astype(bf16)
        dp_ref[:, 2 * d:] = (du * cg[8:]).astype(bf16)
        dw_ref[0:1, :] += jnp.sum(dy_main * u2[8:], axis=0, keepdims=True)
        dw_ref[1:2, :] += jnp.sum(dy_main * u1[8:], axis=0, keepdims=True)
        dw_ref[2:3, :] += jnp.sum(dy_main * u[8:], axis=0, keepdims=True)

    nxt = lambda i: (jnp.minimum((i + 1) * hb, last8), 0)
    return pl.pallas_call(
        body, name=name, grid=(t // tm,),
        in_specs=[pl.BlockSpec((tm, d), lambda i: (i, 0)),
                  pl.BlockSpec((8, d), nxt),
                  pl.BlockSpec((tm, d3), lambda i: (i, 0)),
                  pl.BlockSpec((8, d3), lambda i: (jnp.maximum(i * hb - 1, 0), 0)),
                  pl.BlockSpec((8, d3), nxt),
                  pl.BlockSpec((3, d), lambda i: (0, 0))],
        out_specs=[pl.BlockSpec((tm, d3), lambda i: (i, 0)),
                   pl.BlockSpec((3, d), lambda i: (0, 0))],
        out_shape=[jax.ShapeDtypeStruct((t, d3), bf16), jax.ShapeDtypeStruct((3, d), f32)],
        compiler_params=_params("arbitrary"),
    )(dz, dz, p, p, p, conv_w)


def _attn_stats(o, do, lse, ccol, name):
    t, d = o.shape
    n_heads = d // HEAD_DIM
    tm = _tile(t, 640, 16)

    def body(o_ref, do_ref, lse_ref, cc_ref, cl_ref, delta_ref):
        cl_ref[...] = cc_ref[...] - lse_ref[...]
        delta_ref[...] = jnp.sum(o_ref[...].astype(f32) * do_ref[...].astype(f32), axis=-1,
                                 keepdims=True)

    blk = pl.BlockSpec((tm, HEAD_DIM), lambda h, i: (i, h))
    col = pl.BlockSpec((None, tm, 1), lambda h, i: (h, i, 0))
    shp = jax.ShapeDtypeStruct((n_heads, t, 1), f32)
    return pl.pallas_call(
        body, name=name, grid=(n_heads, t // tm),
        in_specs=[blk, blk, col, col],
        out_specs=[col, col], out_shape=[shp, shp],
        compiler_params=_params("parallel", "parallel"),
    )(o, do, lse, ccol)


def _attn_bwd(q, k, v, do, cl, delta, crow, name):
    t, d = q.shape
    n_heads = d // HEAD_DIM
    bk = crow.shape[-1]
    bq = bk
    n_q = t // bq
    scale = 1.0 / math.sqrt(HEAD_DIM)

    def body(q_ref, do_ref, cl_ref, dl_ref, k_ref, v_ref, cr_ref,
             dq_ref, dr_ref, dk_ref, dv_ref, dc_ref):
        j = pl.program_id(1)

        @pl.when(j == 0)
        def _():
            dq_ref[...] = jnp.zeros_like(dq_ref)
            dr_ref[...] = jnp.zeros_like(dr_ref)

        kj = k_ref[...]
        vj = v_ref[...]
        ck = cr_ref[...]
        row0 = lax.broadcasted_iota(jnp.int32, (bq, bk), 0)
        col = j * bk + lax.broadcasted_iota(jnp.int32, (bq, bk), 1)

        def step(i, carry):
            dk_acc, dv_acc, dc_acc = carry
            off = pl.multiple_of(i * bq, bq)
            qi = q_ref[pl.ds(off, bq), :]
            doi = do_ref[pl.ds(off, bq), :]
            s = lax.dot_general(qi, kj, NT_DIMS, preferred_element_type=f32) * scale
            s = s + cl_ref[pl.ds(off, bq), :] - ck
            s = jnp.where(col <= row0 + i * bq, s, NEG_INF)
            pr = jnp.exp(s)
            dv_acc = dv_acc + lax.dot_general(pr.astype(bf16), doi, TN_DIMS,
                                              preferred_element_type=f32)
            dp = lax.dot_general(doi, vj, NT_DIMS, preferred_element_type=f32)
            ds = pr * (dp - dl_ref[pl.ds(off, bq), :])
            dc_acc = dc_acc - jnp.sum(ds, axis=0, keepdims=True)
            dr_ref[pl.ds(off, bq), :] += jnp.sum(ds, axis=1, keepdims=True)
            dsb = ds.astype(bf16)
            dk_acc = dk_acc + lax.dot_general(dsb, qi, TN_DIMS, preferred_element_type=f32)
            dq_ref[pl.ds(off, bq), :] += scale * jnp.dot(dsb, kj, preferred_element_type=f32)
            return dk_acc, dv_acc, dc_acc

        init = (jnp.zeros((bk, HEAD_DIM), f32), jnp.zeros((bk, HEAD_DIM), f32),
                jnp.zeros((1, bk), f32))
        dk_acc, dv_acc, dc_acc = lax.fori_loop(j, n_q, step, init)
        dk_ref[...] = (dk_acc * scale).astype(bf16)
        dv_ref[...] = dv_acc.astype(bf16)
        dc_ref[...] = dc_acc

    head_rows = pl.BlockSpec((t, HEAD_DIM), lambda h, j: (0, h))
    head_col = pl.BlockSpec((None, t, 1), lambda h, j: (h, 0, 0))
    head_col_in = pl.BlockSpec((None, t, 1), lambda h, j: (h, 0, 0), pipeline_mode=pl.Buffered(1))
    kblk = pl.BlockSpec((bk, HEAD_DIM), lambda h, j: (j, h))
    crow_blk = pl.BlockSpec((None, None, 1, bk), lambda h, j: (h, j, 0, 0))
    return pl.pallas_call(
        body, name=name, grid=(n_heads, t // bk),
        in_specs=[head_rows, head_rows, head_col_in, head_col_in, kblk, kblk, crow_blk],
        out_specs=[head_rows, head_col, kblk, kblk, crow_blk],
        out_shape=[jax.ShapeDtypeStruct((t, d), f32), jax.ShapeDtypeStruct((n_heads, t, 1), f32),
                   jax.ShapeDtypeStruct((t, d), bf16), jax.ShapeDtypeStruct((t, d), bf16),
                   jax.ShapeDtypeStruct((n_heads, t // bk, 1, bk), f32)],
        compiler_params=_params("parallel", "arbitrary"),
    )(q, do, cl, delta, k, v, crow)


def _fgate_bwd(dlogf, flog, fbias, name):
    t, n = flog.shape

    def body(dl_ref, fl_ref, fb_ref, o_ref, sum_ref):
        i = pl.program_id(0)

        @pl.when(i == 0)
        def _():
            sum_ref[...] = jnp.zeros_like(sum_ref)

        r = i * LANES + lax.broadcasted_iota(jnp.int32, (LANES, n), 0)
        g = dl_ref[...] * jax.nn.sigmoid(-(fl_ref[...] + fb_ref[...]))
        g = jnp.where(r >= PAD, g, 0.0)
        o_ref[...] = g
        sum_ref[...] += jnp.sum(g, axis=0, keepdims=True)

    blk = pl.BlockSpec((LANES, n), lambda i: (i, 0))
    vec = pl.BlockSpec((1, n), lambda i: (0, 0))
    return pl.pallas_call(
        body, name=name, grid=(t // LANES,),
        in_specs=[blk, blk, vec], out_specs=[blk, vec],
        out_shape=[jax.ShapeDtypeStruct((t, n), f32), jax.ShapeDtypeStruct((1, n), f32)],
        compiler_params=_params("arbitrary"),
    )(dlogf, flog, fbias)


def _place():
    cx, cy, c = lax.axis_index("x"), lax.axis_index("y"), lax.axis_index("c")
    chips = [(1 - cx, cy), (cx, 1 - cy), (1 - cx, 1 - cy)]
    return cx, cy, c, chips


def _hbm_specs(n):
    return [pl.BlockSpec(memory_space=pl.ANY)] * n


def _all_gather(shards, name):
    n_t = len(shards)

    def body(*refs):
        xs, outs = refs[:n_t], refs[n_t:2 * n_t]
        send, recv, loc = refs[2 * n_t:]
        cx, cy, c, chips = _place()
        me = 2 * cx + cy
        sib = (cx, cy, 1 - c)

        def copy(t, sem, src, dst, to):
            return pltpu.make_async_remote_copy(src_ref=src, dst_ref=dst, send_sem=send.at[6 * t + sem],
                                                recv_sem=recv.at[6 * t + sem], device_id=to,
                                                device_id_type=MESH)

        local = [pltpu.make_async_copy(xs[t], outs[t].at[me], loc.at[t]) for t in range(n_t)]
        for cp in local:
            cp.start()
        sent = []
        for t in range(n_t):
            for k, chip in enumerate(chips):
                cp = copy(t, k, xs[t].at[c], outs[t].at[me, c], (*chip, c))
                cp.start()
                sent.append(cp)
        for t in range(n_t):
            for k, (px, py) in enumerate(chips):
                slot = outs[t].at[2 * px + py, c]
                copy(t, k, slot, slot, sib).wait_recv()
                cp = copy(t, 3 + k, slot, slot, sib)
                cp.start()
                sent.append(cp)
        for t in range(n_t):
            for k, (px, py) in enumerate(chips):
                slot = outs[t].at[2 * px + py, 1 - c]
                copy(t, 3 + k, slot, slot, sib).wait_recv()
        for cp in sent:
            cp.wait_send()
        for cp in local:
            cp.wait()

    return pl.pallas_call(
        body, name=name,
        in_specs=_hbm_specs(n_t), out_specs=_hbm_specs(n_t),
        out_shape=[jax.ShapeDtypeStruct((N_CHIPS, *x.shape), x.dtype) for x in shards],
        scratch_shapes=[pltpu.SemaphoreType.DMA((6 * n_t,)), pltpu.SemaphoreType.DMA((6 * n_t,)),
                        pltpu.SemaphoreType.DMA((n_t,))],
    )(*shards)


def _pair_exchange(grads, name):
    n_t = len(grads)

    def body(*refs):
        gs, outs = refs[:n_t], refs[n_t:2 * n_t]
        send, recv = refs[2 * n_t:]
        cx, cy, c, chips = _place()
        sib = (cx, cy, 1 - c)
        slots = [2 * cx + cy] + [2 * px + py for px, py in chips]
        copies = []
        for t in range(n_t):
            for k, slot in enumerate(slots):
                cp = pltpu.make_async_remote_copy(
                    src_ref=gs[t].at[slot, 1 - c], dst_ref=outs[t].at[k],
                    send_sem=send.at[4 * t + k], recv_sem=recv.at[4 * t + k],
                    device_id=sib, device_id_type=MESH)
                cp.start()
                copies.append(cp)
        for cp in copies:
            cp.wait()

    return pl.pallas_call(
        body, name=name,
        in_specs=_hbm_specs(n_t), out_specs=_hbm_specs(n_t),
        out_shape=[jax.ShapeDtypeStruct((N_CHIPS, *g.shape[2:]), f32) for g in grads],
        scratch_shapes=[pltpu.SemaphoreType.DMA((4 * n_t,)), pltpu.SemaphoreType.DMA((4 * n_t,))],
    )(*grads)


def _chip_exchange(sends, name):
    n_t = len(sends)

    def body(*refs):
        xs, outs = refs[:n_t], refs[n_t:2 * n_t]
        send, recv = refs[2 * n_t:]
        cx, cy, c, chips = _place()
        copies = []
        for t in range(n_t):
            for k, chip in enumerate(chips):
                cp = pltpu.make_async_remote_copy(
                    src_ref=xs[t].at[k], dst_ref=outs[t].at[k],
                    send_sem=send.at[3 * t + k], recv_sem=recv.at[3 * t + k],
                    device_id=(*chip, c), device_id_type=MESH)
                cp.start()
                copies.append(cp)
        for cp in copies:
            cp.wait()

    return pl.pallas_call(
        body, name=name,
        in_specs=_hbm_specs(n_t), out_specs=_hbm_specs(n_t),
        out_shape=[jax.ShapeDtypeStruct(x.shape, x.dtype) for x in sends],
        scratch_shapes=[pltpu.SemaphoreType.DMA((3 * n_t,)), pltpu.SemaphoreType.DMA((3 * n_t,))],
    )(*sends)


def _pair_share(halves, name):
    n_t = len(halves)

    def body(*refs):
        xs, outs = refs[:n_t], refs[n_t:2 * n_t]
        send, recv, loc = refs[2 * n_t:]
        cx, cy, c, _ = _place()
        sib = (cx, cy, 1 - c)
        copies = []
        for t in range(n_t):
            lc = pltpu.make_async_copy(xs[t], outs[t].at[c], loc.at[t])
            lc.start()
            copies.append(lc)
            cp = pltpu.make_async_remote_copy(
                src_ref=xs[t], dst_ref=outs[t].at[c], send_sem=send.at[t], recv_sem=recv.at[t],
                device_id=sib, device_id_type=MESH)
            cp.start()
            copies.append(cp)
        for t in range(n_t):
            copies[2 * t].wait()
            copies[2 * t + 1].wait_send()
            pltpu.make_async_remote_copy(
                src_ref=xs[t], dst_ref=outs[t].at[1 - c], send_sem=send.at[t], recv_sem=recv.at[t],
                device_id=sib, device_id_type=MESH).wait_recv()

    return pl.pallas_call(
        body, name=name,
        in_specs=_hbm_specs(n_t), out_specs=_hbm_specs(n_t),
        out_shape=[jax.ShapeDtypeStruct((2, *x.shape), x.dtype) for x in halves],
        scratch_shapes=[pltpu.SemaphoreType.DMA((n_t,)), pltpu.SemaphoreType.DMA((n_t,)),
                        pltpu.SemaphoreType.DMA((n_t,))],
    )(*halves)


def _rs_prepare(g, pair, idx, name):
    _, _, r, c_n = g.shape
    tr = _tile(r, 256, 8)

    def body(idx_ref, g_ref, p_ref, o_ref):
        o_ref[...] = (g_ref[...] + p_ref[...]).astype(bf16)

    grid_spec = pltpu.PrefetchScalarGridSpec(
        num_scalar_prefetch=1, grid=(3, r // tr),
        in_specs=[pl.BlockSpec((None, None, tr, c_n), lambda k, i, idx: (idx[k + 1], idx[4], i, 0)),
                  pl.BlockSpec((None, tr, c_n), lambda k, i, idx: (k + 1, i, 0))],
        out_specs=pl.BlockSpec((None, tr, c_n), lambda k, i, idx: (k, i, 0)))
    return pl.pallas_call(
        body, name=name, grid_spec=grid_spec,
        out_shape=jax.ShapeDtypeStruct((3, r, c_n), bf16),
        compiler_params=_params("parallel", "parallel"),
    )(idx, g, pair)


def _rs_finish(g, pair, recv, idx, name):
    _, _, r, c_n = g.shape
    tr = _tile(r, 256, 8)

    def body(idx_ref, g_ref, p_ref, r0_ref, r1_ref, r2_ref, o_ref):
        acc = g_ref[...] + p_ref[...]
        acc = acc + r0_ref[...].astype(f32)
        acc = acc + r1_ref[...].astype(f32)
        acc = acc + r2_ref[...].astype(f32)
        o_ref[...] = acc

    def rspec(k):
        return pl.BlockSpec((None, tr, c_n), functools.partial(lambda i, idx, kk: (kk, i, 0), kk=k))

    grid_spec = pltpu.PrefetchScalarGridSpec(
        num_scalar_prefetch=1, grid=(r // tr,),
        in_specs=[pl.BlockSpec((None, None, tr, c_n), lambda i, idx: (idx[0], idx[4], i, 0)),
                  rspec(0), rspec(0), rspec(1), rspec(2)],
        out_specs=pl.BlockSpec((tr, c_n), lambda i, idx: (i, 0)))
    return pl.pallas_call(
        body, name=name, grid_spec=grid_spec,
        out_shape=jax.ShapeDtypeStruct((r, c_n), f32),
        compiler_params=_params("parallel"),
    )(idx, g, pair, recv, recv, recv)


def _adamw(w, g, m, v, name):
    r, c_n = w.shape
    tr = _tile(r, 256, 8) if r % 8 == 0 else r

    def body(w_ref, g_ref, m_ref, v_ref, d_ref, mo_ref, vo_ref):
        gv = g_ref[...]
        mn = ADAM_B1 * m_ref[...] + (1.0 - ADAM_B1) * gv
        vn = ADAM_B2 * v_ref[...] + (1.0 - ADAM_B2) * (gv * gv)
        m_hat = mn / (1.0 - ADAM_B1 ** ADAM_STEP)
        v_hat = vn / (1.0 - ADAM_B2 ** ADAM_STEP)
        d_ref[...] = -ADAM_LR * (m_hat / (jnp.sqrt(v_hat) + ADAM_EPS) + ADAM_WD * w_ref[...])
        mo_ref[...] = mn
        vo_ref[...] = vn

    blk = pl.BlockSpec((tr, c_n), lambda i: (i, 0))
    shp = jax.ShapeDtypeStruct((r, c_n), f32)
    return pl.pallas_call(
        body, name=name, grid=(r // tr,),
        in_specs=[blk] * 4, out_specs=[blk] * 3, out_shape=[shp] * 3,
        compiler_params=_params("parallel"),
    )(w, g, m, v)


def _halves(a):
    flat = a.reshape(-1, a.shape[-1])
    return flat.reshape(2, flat.shape[0] // 2, flat.shape[-1])


def kernel(x, meta, ffn1_wg, ffn1_wu, ffn1_wd, ffn2_wg, ffn2_wu, ffn2_wd, ln_gain, ln_bias, conv_w_in, conv_w, conv_w_out, kv_w, f_bias, attn_w_q, attn_w_o, loss_target, m_meta, m_ffn1_wg, m_ffn1_wu, m_ffn1_wd, m_ffn2_wg, m_ffn2_wu, m_ffn2_wd, m_ln_gain, m_ln_bias, m_conv_w_in, m_conv_w, m_conv_w_out, m_kv_w, m_f_bias, m_attn_w_q, m_attn_w_o, v_meta, v_ffn1_wg, v_ffn1_wu, v_ffn1_wd, v_ffn2_wg, v_ffn2_wu, v_ffn2_wd, v_ln_gain, v_ln_bias, v_conv_w_in, v_conv_w, v_conv_w_out, v_kv_w, v_f_bias, v_attn_w_q, v_attn_w_o):
    seq, d = x.shape[1], x.shape[2]
    t = PAD + N_META + seq
    n_heads = d // HEAD_DIM
    dq = d // N_CHIPS
    n_kv = kv_w.shape[1]
    x2 = x[0]
    target = loss_target[0]

    def rows8(a):
        return jnp.pad(a, ((0, 8 - a.shape[0]), (0, 0)))

    def small_pack(mt, g, b, cw, fb):
        fb_row = jnp.pad(fb, (0, mt.shape[1] - n_heads))[None]
        return jnp.concatenate([mt, rows8(g.reshape(6, -1)), rows8(b.reshape(6, -1)),
                                rows8(cw.reshape(3, -1)), rows8(fb_row)], axis=0)

    w_small = small_pack(meta, ln_gain, ln_bias, conv_w, f_bias)

    shards = [ffn1_wg.astype(bf16), ffn1_wu.astype(bf16), ffn1_wd.astype(bf16),
              ffn2_wg.astype(bf16), ffn2_wu.astype(bf16), ffn2_wd.astype(bf16),
              _halves(conv_w_in.astype(bf16)), _halves(conv_w_out.astype(bf16)),
              _halves(kv_w.astype(bf16)), _halves(attn_w_q.astype(bf16)),
              _halves(attn_w_o.astype(bf16)), _halves(w_small)]
    gw = _all_gather(shards, "all_gather_weights")
    wg1, wu1, wd1, wg2, wu2, wd2 = gw[:6]
    n_in = conv_w_in.shape[-1]
    w_in = gw[6].reshape(N_CHIPS, d, n_in)
    w_out = gw[7].reshape(1, 1, d, d)
    w_q = gw[9].reshape(1, d, d)
    w_o = gw[10].reshape(1, 1, d, d)
    kv_full = gw[8].reshape(N_CHIPS, d, n_kv).transpose(1, 0, 2).reshape(d, N_CHIPS * n_kv)
    w_k = kv_full[:, :d][None]
    w_v = kv_full[:, d:2 * d][None]
    w_f = jnp.pad(kv_full[:, 2 * d:], ((0, 0), (0, LANES - n_heads)))[None]
    small = gw[11].reshape(N_CHIPS, SMALL_ROWS, dq).transpose(1, 0, 2).reshape(SMALL_ROWS, d)
    meta_full = small[:N_META]
    gains = small[16:22].reshape(DEPTH, 3, 1, d)
    biases = small[24:30].reshape(DEPTH, 3, 1, d)
    conv_w_full = small[32:35]
    fb_pad = jnp.pad(f_bias, (0, LANES - n_heads))[None]

    meta_pad = jnp.concatenate([jnp.zeros((PAD, d), f32), meta_full], axis=0)
    h0, h0b = _embed(meta_pad, x2, "embed")
    a1, b1, s1 = _ffn_up(h0b, wg1, wu1, 0, "ffn_up")
    r1, h1, h1b = _down_ln(s1, wd1, 0, h0, gains[0, 0], biases[0, 0], 0.5, "ffn_down_ln")
    p = _nn_matmul(h1b, w_in, f32, "conv_in")
    z = _conv_fwd(p, conv_w_full, "conv_fwd")
    r2, h2, h2b = _down_ln(z[None], w_out, 0, h1, gains[0, 1], biases[0, 1], 1.0, "mix_out_ln")
    a2, b2, s2 = _ffn_up(h2b, wg2, wu2, 0, "ffn_up")
    r3, h3, h3b = _down_ln(s2, wd2, 0, h2, gains[0, 2], biases[0, 2], 0.5, "ffn_down_ln")
    kk = _nn_matmul(h3b, w_k, bf16, "proj_bf16")
    vv = _nn_matmul(h3b, w_v, bf16, "proj_bf16")
    flog = _nn_matmul(h3b, w_f, f32, "proj_gate")
    cum = _row_scan(flog, "gate_cumsum", fb_pad)
    bk = _tile(t, 640, LANES)
    c_ht = cum[:, :n_heads].T
    ccol = c_ht[:, :, None]
    crow = jnp.where(jnp.arange(t)[None, :] < PAD, 1e30, c_ht).reshape(n_heads, t // bk, 1, bk)
    a3, b3, s3 = _ffn_up(h3b, wg1, wu1, 1, "ffn_up")
    r4, h4, h4b = _down_ln(s3, wd1, 1, h3, gains[1, 0], biases[1, 0], 0.5, "ffn_down_ln")
    q = _nn_matmul(h4b, w_q, bf16, "proj_bf16")
    o, o32, lse = _attn_fwd(q, kk, vv, ccol, crow, "attn_fwd")
    r5, h5, h5b = _down_ln(o[None], w_o, 0, h4, gains[1, 1], biases[1, 1], 1.0, "mix_out_ln")
    a4, b4, s4 = _ffn_up(h5b, wg2, wu2, 1, "ffn_up")
    r6, h6, _ = _down_ln(s4, wd2, 1, h5, gains[1, 2], biases[1, 2], 0.5, "ffn_down_ln")
    dy, sq = _loss_head(h6, target, "loss_head")
    loss_part = 0.5 * sq[0, 0] / d

    def ffn_bwd(dr, drb, hb_in, a, b, s, wg, wu, wd, layer, into):
        da, db = _ffn_bwd_act(drb, wd, layer, a, b, "ffn_bwd_act")
        into_d = None if into is None else [into[2]]
        into_gu = None if into is None else [into[0], into[1]]
        (dwd,) = _tn_matmul(s, "stack", [drb], ["shared"], N_CHIPS, 0.5, "ffn_dwd",
                            layers=DEPTH, layer=layer, into=into_d)
        dwg, dwu = _tn_matmul(hb_in, "shared", [da, db], ["stack", "stack"], N_CHIPS, 1.0, "ffn_dwgu",
                              layers=DEPTH, layer=layer, into=into_gu)
        dh = _nt_sum([(da, "stack", wg, layer), (db, "stack", wu, layer)], dr, ALPHA, N_CHIPS, f32,
                     "ffn_dx")
        return dh, (dwg, dwu, dwd)

    dr6, dr6b, dg12, db12 = _ln_bwd(dy, r6, gains[1, 2], "ln_bwd")
    dh5, gf2 = ffn_bwd(dr6, dr6b, h5b, a4, b4, s4, wg2, wu2, wd2, 1, None)
    dr5, dr5b, dg11, db11 = _ln_bwd(dh5, r5, gains[1, 1], "ln_bwd")
    (dwo,) = _tn_matmul(o, "shared", [dr5b], ["shared"], 1, 1.0, "sq_dw")
    do = _nt_sum([(dr5b, "cols", w_o[0], None)], None, 1.0, 1, bf16, "sq_dx_bf16")
    cl, delta = _attn_stats(o32, do, lse, ccol, "attn_stats")
    dq_att, dc_q, dk, dv, dc_k = _attn_bwd(q, kk, vv, do, cl, delta, crow, "attn_bwd")
    dc = dc_q[:, :, 0] + dc_k.reshape(n_heads, t)
    (dwq,) = _tn_matmul(h4b, "shared", [dq_att], ["shared"], 1, 1.0, "sq_dw")
    dh4 = _nt_sum([(dq_att, "cols", w_q, None)], dr5, ALPHA, 1, f32, "sq_dx_res")
    dr4, dr4b, dg10, db10 = _ln_bwd(dh4, r4, gains[1, 0], "ln_bwd")
    dh3a, gf1 = ffn_bwd(dr4, dr4b, h3b, a3, b3, s3, wg1, wu1, wd1, 1, None)
    dc_t = jnp.pad(dc.T, ((0, 0), (0, LANES - n_heads)))
    dlogf = _row_scan(dc_t[::-1], "rev_cumsum")[::-1]
    dfl, dfb_cols = _fgate_bwd(dlogf, flog, fb_pad, "gate_bwd")
    dwk, dwv, dwf = _tn_matmul(h3b, "shared", [dk, dv, dfl], ["shared"] * 3, 1, 1.0, "kv_dw")
    dh3 = _nt_sum([(dk, "cols", w_k, None), (dv, "cols", w_v, None), (dfl, "cols", w_f, None)],
                  dh3a, 1.0, 1, f32, "kv_dx")
    dr3, dr3b, dg02, db02 = _ln_bwd(dh3, r3, gains[0, 2], "ln_bwd")
    dh2, gf2 = ffn_bwd(dr3, dr3b, h2b, a2, b2, s2, wg2, wu2, wd2, 0, gf2)
    dr2, dr2b, dg01, db01 = _ln_bwd(dh2, r2, gains[0, 1], "ln_bwd")
    (dwout,) = _tn_matmul(z, "shared", [dr2b], ["shared"], 1, 1.0, "sq_dw")
    dz = _nt_sum([(dr2b, "cols", w_out[0], None)], None, 1.0, 1, f32, "sq_dx_f32")
    dp, dconv_w = _conv_bwd(dz, p, conv_w_full, "conv_bwd")
    (dwin,) = _tn_matmul(h1b, "shared", [dp], ["cols"], N_CHIPS, 1.0, "conv_dwin")
    dh1 = _nt_sum([(dp, "cols", w_in, None)], dr2, ALPHA, N_CHIPS, f32, "conv_dx")
    dr1, dr1b, dg00, db00 = _ln_bwd(dh1, r1, gains[0, 0], "ln_bwd")
    dh0, gf1 = ffn_bwd(dr1, dr1b, h0b, a1, b1, s1, wg1, wu1, wd1, 0, gf1)
    grad_x = dh0[PAD + N_META:][None]
    dmeta = dh0[PAD:PAD + N_META]

    def by_chip(full, rows):
        n = full.shape[1] // N_CHIPS
        return full.reshape(rows, N_CHIPS, n).transpose(1, 0, 2).reshape(N_CHIPS, 2, rows // 2, n)

    dkv_full = jnp.concatenate([dwk[0], dwv[0], dwf[0][:, :n_heads]], axis=1)
    d_small = jnp.concatenate(
        [dmeta, rows8(jnp.concatenate([dg00, dg01, dg02, dg10, dg11, dg12], axis=0)),
         rows8(jnp.concatenate([db00, db01, db02, db10, db11, db12], axis=0)),
         rows8(dconv_w), jnp.zeros((8, d), f32)], axis=0)
    grads = [gf1[0], gf1[1], gf1[2], gf2[0], gf2[1], gf2[2],
             dwin.reshape(N_CHIPS, 2, d // 2, n_in),
             dwout.reshape(N_CHIPS, 2, dq // 2, d),
             by_chip(dkv_full, d),
             dwq.reshape(N_CHIPS, 2, dq // 2, d),
             dwo.reshape(N_CHIPS, 2, dq // 2, d),
             by_chip(d_small, SMALL_ROWS)]
    cx, cy, c = lax.axis_index("x"), lax.axis_index("y"), lax.axis_index("c")
    idx = jnp.stack([2 * cx + cy, 2 * (1 - cx) + cy, 2 * cx + (1 - cy), 2 * (1 - cx) + (1 - cy), c]
                    ).astype(jnp.int32)
    pair = _pair_exchange(grads, "grad_pair_exchange")
    sends = [_rs_prepare(g, pr, idx, "grad_prepare") for g, pr in zip(grads, pair)]
    recvs = _chip_exchange(sends, "grad_chip_exchange")
    halves = [_rs_finish(g, pr, rc, idx, "grad_finish") for g, pr, rc in zip(grads, pair, recvs)]
    red = _pair_share(halves, "grad_pair_share")

    tail = jnp.zeros((LANES,), f32).at[:n_heads].set(dfb_cols[0, :n_heads]).at[n_heads].set(loss_part)
    tail = lax.psum(tail, ("x", "y", "c"))
    loss = tail[n_heads]
    g_fb = tail[:n_heads]

    params = [ffn1_wg, ffn1_wu, ffn1_wd, ffn2_wg, ffn2_wu, ffn2_wd, conv_w_in, conv_w_out, kv_w,
              attn_w_q, attn_w_o]
    ms = [m_ffn1_wg, m_ffn1_wu, m_ffn1_wd, m_ffn2_wg, m_ffn2_wu, m_ffn2_wd, m_conv_w_in, m_conv_w_out,
          m_kv_w, m_attn_w_q, m_attn_w_o]
    vs = [v_ffn1_wg, v_ffn1_wu, v_ffn1_wd, v_ffn2_wg, v_ffn2_wu, v_ffn2_wd, v_conv_w_in, v_conv_w_out,
          v_kv_w, v_attn_w_q, v_attn_w_o]
    out_g, out_d, out_m, out_v = [], [], [], []
    for w, g, m, v in zip(params, red[:11], ms, vs):
        w2 = w.reshape(-1, w.shape[-1])
        g2 = g.reshape(w2.shape)
        dl, mn, vn = _adamw(w2, g2, m.reshape(w2.shape), v.reshape(w2.shape), "adamw")
        out_g.append(g2.reshape(w.shape))
        out_d.append(dl.reshape(w.shape))
        out_m.append(mn.reshape(w.shape))
        out_v.append(vn.reshape(w.shape))
    g_small = red[11].reshape(SMALL_ROWS, dq)
    g_small = jnp.concatenate([g_small[:40], rows8(jnp.pad(g_fb, (0, dq - n_heads))[None])], axis=0)
    m_small = small_pack(m_meta, m_ln_gain, m_ln_bias, m_conv_w, m_f_bias)
    v_small = small_pack(v_meta, v_ln_gain, v_ln_bias, v_conv_w, v_f_bias)
    d_s, m_s, v_s = _adamw(w_small, g_small, m_small, v_small, "adamw_small")

    def unpack(pk):
        return (pk[:16], pk[16:22].reshape(DEPTH, 3, dq), pk[24:30].reshape(DEPTH, 3, dq),
                pk[32:35].reshape(1, 3, dq), pk[40, :n_heads])

    def order(big, pk):
        mt, g, b, cw, fb = unpack(pk)
        (f1g, f1u, f1d, f2g, f2u, f2d, cin, cout, kvw, aq, ao) = big
        return [mt, f1g, f1u, f1d, f2g, f2u, f2d, g, b, cin, cw, cout, kvw, fb, aq, ao]

    return (loss, grad_x, *order(out_g, g_small), *order(out_d, d_s), *order(out_m, m_s),
            *order(out_v, v_s))
```

```python
import functools
import math

import jax
import jax.numpy as jnp
from jax import lax
from jax.experimental import pallas as pl
from jax.experimental.pallas import tpu as pltpu

f32 = jnp.float32
bf16 = jnp.bfloat16

N_META = 16
PAD = 112
HEAD_DIM = 128
DEPTH = 2
LN_EPS = 1e-5
ALPHA = (2 * DEPTH) ** 0.25
NEG_INF = -1e30
N_CHIPS = 4
SMALL_ROWS = 48
LANES = 128

ADAM_LR = 0.001
ADAM_B1 = 0.9
ADAM_B2 = 0.999
ADAM_EPS = 1e-08
ADAM_WD = 0.01
ADAM_STEP = 10

VMEM_LIMIT_BYTES = 56 * 1024 * 1024
MESH = pl.DeviceIdType.MESH

NT_DIMS = (((1,), (1,)), ((), ()))
TN_DIMS = (((0,), (0,)), ((), ()))


def _tile(n, target, mult):
    best = None
    for d in range(mult, min(n, target) + 1, mult):
        if n % d == 0:
            best = d
    assert best is not None, (n, target, mult)
    return best


def _params(*sem):
    return pltpu.CompilerParams(dimension_semantics=sem, vmem_limit_bytes=VMEM_LIMIT_BYTES)


def _embed(meta_pad, x, name):
    seq, d = x.shape
    t = seq + LANES

    def body(m_ref, x_ref, h_ref, hb_ref):
        first = pl.program_id(0) == 0
        v = jnp.where(first, m_ref[...], x_ref[...])
        h_ref[...] = v
        hb_ref[...] = v.astype(bf16)

    return pl.pallas_call(
        body, name=name, grid=(t // LANES,),
        in_specs=[pl.BlockSpec((LANES, d), lambda i: (0, 0)),
                  pl.BlockSpec((LANES, d), lambda i: (jnp.maximum(i - 1, 0), 0))],
        out_specs=[pl.BlockSpec((LANES, d), lambda i: (i, 0)),
                   pl.BlockSpec((LANES, d), lambda i: (i, 0))],
        out_shape=[jax.ShapeDtypeStruct((t, d), f32), jax.ShapeDtypeStruct((t, d), bf16)],
        compiler_params=_params("parallel"),
    )(meta_pad, x)


def _nn_matmul(x, w, out_dtype, name):
    t, k = x.shape
    s_n, _, n = w.shape
    assert s_n == 1 or n % LANES == 0
    tm = _tile(t, 640, 16)

    def body(x_ref, w_ref, o_ref):
        o_ref[...] = jnp.dot(x_ref[...].astype(bf16), w_ref[...],
                             preferred_element_type=f32).astype(o_ref.dtype)

    return pl.pallas_call(
        body, name=name, grid=(s_n, t // tm),
        in_specs=[pl.BlockSpec((tm, k), lambda s, i: (i, 0)),
                  pl.BlockSpec((None, k, n), lambda s, i: (s, 0, 0))],
        out_specs=pl.BlockSpec((tm, n), lambda s, i: (i, s)),
        out_shape=jax.ShapeDtypeStruct((t, s_n * n), out_dtype),
        compiler_params=_params("parallel", "parallel"),
    )(x, w)


def _ffn_up(hb, wg, wu, layer, name):
    t, d = hb.shape
    s_n, _, _, n = wg.shape
    tm = _tile(t, 640, 16)

    def body(x_ref, wg_ref, wu_ref, a_ref, b_ref, s_ref):
        x = x_ref[...]
        a = jnp.dot(x, wg_ref[...], preferred_element_type=f32)
        b = jnp.dot(x, wu_ref[...], preferred_element_type=f32)
        a_ref[...] = a
        b_ref[...] = b
        s_ref[...] = (a * jax.nn.sigmoid(a) * b).astype(bf16)

    wspec = pl.BlockSpec((None, None, d, n), lambda s, i: (s, layer, 0, 0))
    ospec = pl.BlockSpec((None, tm, n), lambda s, i: (s, i, 0))
    return pl.pallas_call(
        body, name=name, grid=(s_n, t // tm),
        in_specs=[pl.BlockSpec((tm, d), lambda s, i: (i, 0)), wspec, wspec],
        out_specs=[ospec, ospec, ospec],
        out_shape=[jax.ShapeDtypeStruct((s_n, t, n), f32), jax.ShapeDtypeStruct((s_n, t, n), f32),
                   jax.ShapeDtypeStruct((s_n, t, n), bf16)],
        compiler_params=_params("parallel", "parallel"),
    )(hb, wg, wu)


def _down_ln(x, w, layer, hprev, gain, bias, beta, name):
    s_n, t, k = x.shape
    d = w.shape[-1]
    tm = _tile(t, 640, 16)

    def body(x_ref, w_ref, h_ref, g_ref, b_ref, r_out, h_out, hb_out, acc):
        s = pl.program_id(1)

        @pl.when(s == 0)
        def _():
            acc[...] = jnp.zeros_like(acc)

        acc[...] += jnp.dot(x_ref[...], w_ref[...], preferred_element_type=f32)

        @pl.when(s == s_n - 1)
        def _():
            r = ALPHA * h_ref[...] + beta * acc[...]
            mu = jnp.mean(r, axis=-1, keepdims=True)
            xc = r - mu
            var = jnp.mean(xc * xc, axis=-1, keepdims=True)
            y = xc * lax.rsqrt(var + LN_EPS) * g_ref[...] + b_ref[...]
            r_out[...] = r
            h_out[...] = y
            hb_out[...] = y.astype(bf16)

    row = pl.BlockSpec((tm, d), lambda i, s: (i, 0))
    vec = pl.BlockSpec((1, d), lambda i, s: (0, 0))
    return pl.pallas_call(
        body, name=name, grid=(t // tm, s_n),
        in_specs=[pl.BlockSpec((None, tm, k), lambda i, s: (s, i, 0)),
                  pl.BlockSpec((None, None, k, d), lambda i, s: (s, layer, 0, 0)),
                  row, vec, vec],
        out_specs=[row, row, row],
        out_shape=[jax.ShapeDtypeStruct((t, d), f32), jax.ShapeDtypeStruct((t, d), f32),
                   jax.ShapeDtypeStruct((t, d), bf16)],
        scratch_shapes=[pltpu.VMEM((tm, d), f32)],
        compiler_params=_params("parallel", "arbitrary"),
    )(x, w, hprev, gain, bias)


def _conv_fwd(p, conv_w, name):
    t, d3 = p.shape
    d = d3 // 3
    tm = _tile(t, 320, 8)
    hb = tm // 8

    def body(p_ref, prev_ref, w_ref, z_ref):
        i = pl.program_id(0)
        rows = i * tm - 8 + lax.broadcasted_iota(jnp.int32, (tm + 8, 1), 0)
        cg = jnp.concatenate([prev_ref[:, d:2 * d], p_ref[:, d:2 * d]], axis=0)
        val = jnp.concatenate([prev_ref[:, 2 * d:], p_ref[:, 2 * d:]], axis=0)
        u = jnp.where(rows >= PAD, cg * val, 0.0)
        y = (w_ref[2:3, :] * u + w_ref[1:2, :] * pltpu.roll(u, 1, 0)
             + w_ref[0:1, :] * pltpu.roll(u, 2, 0))
        z_ref[...] = (p_ref[:, :d] * y[8:]).astype(bf16)

    return pl.pallas_call(
        body, name=name, grid=(t // tm,),
        in_specs=[pl.BlockSpec((tm, d3), lambda i: (i, 0)),
                  pl.BlockSpec((8, d3), lambda i: (jnp.maximum(i * hb - 1, 0), 0)),
                  pl.BlockSpec((3, d), lambda i: (0, 0))],
        out_specs=pl.BlockSpec((tm, d), lambda i: (i, 0)),
        out_shape=jax.ShapeDtypeStruct((t, d), bf16),
        compiler_params=_params("parallel"),
    )(p, p, conv_w)


def _row_scan(x, name, fbias=None):
    t, n = x.shape
    blk = LANES
    gate = fbias is not None

    def body(*refs):
        if gate:
            x_ref, fb_ref, o_ref, carry = refs
        else:
            x_ref, o_ref, carry = refs
        i = pl.program_id(0)

        @pl.when(i == 0)
        def _():
            carry[...] = jnp.zeros_like(carry)

        v = x_ref[...]
        r = lax.broadcasted_iota(jnp.int32, (blk, n), 0)
        if gate:
            v = v + fb_ref[...]
            v = jnp.minimum(v, 0.0) - jnp.log1p(jnp.exp(-jnp.abs(v)))
            v = jnp.where(i * blk + r >= PAD, v, 0.0)
        sh = 1
        while sh < blk:
            v = v + jnp.where(r >= sh, pltpu.roll(v, sh, 0), 0.0)
            sh *= 2
        v = v + carry[...]
        o_ref[...] = v
        carry[...] = o_ref[blk - 1:blk, :]

    in_specs = [pl.BlockSpec((blk, n), lambda i: (i, 0))]
    args = [x]
    if gate:
        in_specs.append(pl.BlockSpec((1, n), lambda i: (0, 0)))
        args.append(fbias)
    return pl.pallas_call(
        body, name=name, grid=(t // blk,),
        in_specs=in_specs,
        out_specs=pl.BlockSpec((blk, n), lambda i: (i, 0)),
        out_shape=jax.ShapeDtypeStruct((t, n), f32),
        scratch_shapes=[pltpu.VMEM((1, n), f32)],
        compiler_params=_params("arbitrary"),
    )(*args)


def _attn_fwd(q, k, v, ccol, crow, name):
    t, d = q.shape
    n_heads = d // HEAD_DIM
    bk = crow.shape[-1]
    bq = bk
    scale = 1.0 / math.sqrt(HEAD_DIM)

    def body(q_ref, k_ref, v_ref, cc_ref, cr_ref, o_ref, o32_ref, lse_ref):
        i = pl.program_id(1)
        qb = q_ref[...]
        cq = cc_ref[...]
        row = i * bq + lax.broadcasted_iota(jnp.int32, (bq, bk), 0)
        col = lax.broadcasted_iota(jnp.int32, (bq, bk), 1)

        def step(j, carry):
            m, l, acc = carry
            off = pl.multiple_of(j * bk, bk)
            kj = k_ref[pl.ds(off, bk), :]
            vj = v_ref[pl.ds(off, bk), :]
            s = lax.dot_general(qb, kj, NT_DIMS, preferred_element_type=f32) * scale
            s = s + cq - cr_ref[j]
            s = jnp.where(col + j * bk <= row, s, NEG_INF)
            m_new = jnp.maximum(m, jnp.max(s, axis=-1, keepdims=True))
            a = jnp.exp(m - m_new)
            pr = jnp.exp(s - m_new)
            l = a * l + jnp.sum(pr, axis=-1, keepdims=True)
            acc = a * acc + jnp.dot(pr.astype(bf16), vj, preferred_element_type=f32)
            return m_new, l, acc

        init = (jnp.full((bq, 1), NEG_INF, f32), jnp.zeros((bq, 1), f32),
                jnp.zeros((bq, HEAD_DIM), f32))
        m, l, acc = lax.fori_loop(0, i + 1, step, init)
        out = acc / l
        o_ref[...] = out.astype(bf16)
        o32_ref[...] = out
        lse_ref[...] = m + jnp.log(l)

    head_rows = pl.BlockSpec((t, HEAD_DIM), lambda h, i: (0, h))
    qblk = pl.BlockSpec((bq, HEAD_DIM), lambda h, i: (i, h))
    return pl.pallas_call(
        body, name=name, grid=(n_heads, t // bq),
        in_specs=[qblk, head_rows, head_rows,
                  pl.BlockSpec((None, bq, 1), lambda h, i: (h, i, 0)),
                  pl.BlockSpec((None, t // bk, 1, bk), lambda h, i: (h, 0, 0, 0))],
        out_specs=[qblk, qblk, pl.BlockSpec((None, bq, 1), lambda h, i: (h, i, 0))],
        out_shape=[jax.ShapeDtypeStruct((t, d), bf16), jax.ShapeDtypeStruct((t, d), f32),
                   jax.ShapeDtypeStruct((n_heads, t, 1), f32)],
        compiler_params=_params("parallel", "parallel"),
    )(q, k, v, ccol, crow)


def _loss_head(h, target, name):
    t, d = h.shape

    def body(h_ref, t_ref, dy_ref, loss_ref):
        i = pl.program_id(0)

        @pl.when(i == 0)
        def _():
            loss_ref[...] = jnp.zeros_like(loss_ref)

        diff = jnp.where(i >= 1, h_ref[...] - t_ref[...], 0.0)
        dy_ref[...] = diff * (1.0 / d)
        loss_ref[...] += jnp.sum(diff * diff)

    return pl.pallas_call(
        body, name=name, grid=(t // LANES,),
        in_specs=[pl.BlockSpec((LANES, d), lambda i: (i, 0)),
                  pl.BlockSpec((LANES, d), lambda i: (jnp.maximum(i - 1, 0), 0))],
        out_specs=[pl.BlockSpec((LANES, d), lambda i: (i, 0)),
                   pl.BlockSpec((1, LANES), lambda i: (0, 0))],
        out_shape=[jax.ShapeDtypeStruct((t, d), f32), jax.ShapeDtypeStruct((1, LANES), f32)],
        compiler_params=_params("arbitrary"),
    )(h, target)


def _ln_bwd(dh, r, gain, name):
    t, d = r.shape
    tm = _tile(t, 640, 16)

    def body(dh_ref, r_ref, g_ref, dr_ref, drb_ref, dg_ref, db_ref):
        @pl.when(pl.program_id(0) == 0)
        def _():
            dg_ref[...] = jnp.zeros_like(dg_ref)
            db_ref[...] = jnp.zeros_like(db_ref)

        rr = r_ref[...]
        mu = jnp.mean(rr, axis=-1, keepdims=True)
        xc = rr - mu
        var = jnp.mean(xc * xc, axis=-1, keepdims=True)
        rstd = lax.rsqrt(var + LN_EPS)
        xhat = xc * rstd
        dy = dh_ref[...]
        dxh = dy * g_ref[...]
        m1 = jnp.mean(dxh, axis=-1, keepdims=True)
        m2 = jnp.mean(dxh * xhat, axis=-1, keepdims=True)
        dr = rstd * (dxh - m1 - xhat * m2)
        dr_ref[...] = dr
        drb_ref[...] = dr.astype(bf16)
        dg_ref[...] += jnp.sum(dy * xhat, axis=0, keepdims=True)
        db_ref[...] += jnp.sum(dy, axis=0, keepdims=True)

    row = pl.BlockSpec((tm, d), lambda i: (i, 0))
    vec = pl.BlockSpec((1, d), lambda i: (0, 0))
    return pl.pallas_call(
        body, name=name, grid=(t // tm,),
        in_specs=[row, row, vec],
        out_specs=[row, row, vec, vec],
        out_shape=[jax.ShapeDtypeStruct((t, d), f32), jax.ShapeDtypeStruct((t, d), bf16),
                   jax.ShapeDtypeStruct((1, d), f32), jax.ShapeDtypeStruct((1, d), f32)],
        compiler_params=_params("arbitrary"),
    )(dh, r, gain)


def _ffn_bwd_act(drb, wd, layer, a, b, name):
    t, d = drb.shape
    s_n, _, n = a.shape
    tm = _tile(t, 640, 16)

    def body(dr_ref, w_ref, a_ref, b_ref, da_ref, db_ref):
        ds = 0.5 * lax.dot_general(dr_ref[...], w_ref[...], NT_DIMS, preferred_element_type=f32)
        av = a_ref[...]
        sig = jax.nn.sigmoid(av)
        da_ref[...] = (ds * b_ref[...] * (sig * (1.0 + av * (1.0 - sig)))).astype(bf16)
        db_ref[...] = (ds * (av * sig)).astype(bf16)

    act = pl.BlockSpec((None, tm, n), lambda s, i: (s, i, 0))
    return pl.pallas_call(
        body, name=name, grid=(s_n, t // tm),
        in_specs=[pl.BlockSpec((tm, d), lambda s, i: (i, 0)),
                  pl.BlockSpec((None, None, n, d), lambda s, i: (s, layer, 0, 0)), act, act],
        out_specs=[act, act],
        out_shape=[jax.ShapeDtypeStruct((s_n, t, n), bf16), jax.ShapeDtypeStruct((s_n, t, n), bf16)],
        compiler_params=_params("parallel", "parallel"),
    )(drb, wd, a, b)


def _act_spec(mode, tt, k, t_first):
    def fix(fn):
        return (lambda i, s: fn(s, i)) if t_first else fn
    if mode == "shared":
        return pl.BlockSpec((tt, k), fix(lambda s, i: (i, 0)))
    if mode == "cols":
        return pl.BlockSpec((tt, k), fix(lambda s, i: (i, s)))
    assert mode == "stack"
    return pl.BlockSpec((None, tt, k), fix(lambda s, i: (s, i, 0)))


def _act_width(arr, mode, s_n):
    return arr.shape[-1] // s_n if mode == "cols" else arr.shape[-1]


def _tn_matmul(x, xmode, ys, ymodes, s_n, scale, name, layers=None, layer=0, into=None):
    t = x.shape[-2]
    kx = _act_width(x, xmode, s_n)
    kys = [_act_width(y, m, s_n) for y, m in zip(ys, ymodes)]
    tt = _tile(t, 640, 16)
    n_t = t // tt
    n_y = len(ys)
    n_alias = 0 if into is None else n_y

    def body(*refs):
        x_ref = refs[0]
        y_refs = refs[1:1 + n_y]
        o_refs = refs[1 + n_y + n_alias:]
        i = pl.program_id(1)
        xv = x_ref[...].astype(bf16)
        for y_ref, o_ref in zip(y_refs, o_refs):
            part = lax.dot_general(xv, y_ref[...].astype(bf16), TN_DIMS, preferred_element_type=f32)

            @pl.when(i == 0)
            def _():
                o_ref[...] = part

            @pl.when(i > 0)
            def _():
                o_ref[...] += part

            if scale != 1.0:
                @pl.when(i == n_t - 1)
                def _():
                    o_ref[...] = o_ref[...] * scale

    in_specs = [_act_spec(xmode, tt, kx, False)]
    in_specs += [_act_spec(m, tt, ky, False) for m, ky in zip(ymodes, kys)]
    if layers is None:
        out_specs = [pl.BlockSpec((None, kx, ky), lambda s, i: (s, 0, 0)) for ky in kys]
        out_shape = [jax.ShapeDtypeStruct((s_n, kx, ky), f32) for ky in kys]
    else:
        out_specs = [pl.BlockSpec((None, None, kx, ky), lambda s, i: (s, layer, 0, 0)) for ky in kys]
        out_shape = [jax.ShapeDtypeStruct((s_n, layers, kx, ky), f32) for ky in kys]
    args = [x, *ys]
    aliases = {}
    if into is not None:
        in_specs += [pl.BlockSpec(memory_space=pl.ANY)] * n_y
        args += list(into)
        aliases = {1 + n_y + j: j for j in range(n_y)}
    return pl.pallas_call(
        body, name=name, grid=(s_n, n_t),
        in_specs=in_specs, out_specs=out_specs, out_shape=out_shape,
        input_output_aliases=aliases,
        compiler_params=_params("parallel", "arbitrary"),
    )(*args)


def _nt_sum(pairs, base, base_scale, s_n, out_dtype, name):
    t = pairs[0][0].shape[-2]
    d = pairs[0][2].shape[-2]
    tm = _tile(t, 640, 16)
    n_p = len(pairs)
    has_base = base is not None

    def body(*refs):
        dy_refs = refs[0:2 * n_p:2]
        w_refs = refs[1:2 * n_p:2]
        rest = refs[2 * n_p:]
        base_ref = rest[0] if has_base else None
        o_ref, acc = rest[-2], rest[-1]
        s = pl.program_id(1)

        @pl.when(s == 0)
        def _():
            acc[...] = jnp.zeros_like(acc)

        tot = None
        for dy_ref, w_ref in zip(dy_refs, w_refs):
            part = lax.dot_general(dy_ref[...].astype(bf16), w_ref[...], NT_DIMS,
                                   preferred_element_type=f32)
            tot = part if tot is None else tot + part
        acc[...] += tot

        @pl.when(s == s_n - 1)
        def _():
            res = acc[...]
            if has_base:
                res = base_scale * base_ref[...] + res
            o_ref[...] = res.astype(o_ref.dtype)

    in_specs, args = [], []
    for dy, mode, w, layer in pairs:
        k = _act_width(dy, mode, s_n)
        in_specs.append(_act_spec(mode, tm, k, True))
        if layer is None:
            in_specs.append(pl.BlockSpec((None, d, k), lambda i, s: (s, 0, 0)))
        else:
            in_specs.append(pl.BlockSpec((None, None, d, k),
                                         functools.partial(lambda i, s, l: (s, l, 0, 0), l=layer)))
        args += [dy, w]
    row = pl.BlockSpec((tm, d), lambda i, s: (i, 0))
    if has_base:
        in_specs.append(row)
        args.append(base)
    return pl.pallas_call(
        body, name=name, grid=(t // tm, s_n),
        in_specs=in_specs, out_specs=row,
        out_shape=jax.ShapeDtypeStruct((t, d), out_dtype),
        scratch_shapes=[pltpu.VMEM((tm, d), f32)],
        compiler_params=_params("parallel", "arbitrary"),
    )(*args)


def _conv_bwd(dz, p, conv_w, name):
    t, d3 = p.shape
    d = d3 // 3
    tm = _tile(t, 320, 8)
    hb = tm // 8
    last8 = t // 8 - 1
    n_ext = tm + 8

    def body(dz_ref, dzn_ref, p_ref, pp_ref, pn_ref, w_ref, dp_ref, dw_ref):
        i = pl.program_id(0)

        @pl.when(i == 0)
        def _():
            dw_ref[...] = jnp.zeros_like(dw_ref)

        w0, w1, w2 = w_ref[0:1, :], w_ref[1:2, :], w_ref[2:3, :]
        rows_u = i * tm - 8 + lax.broadcasted_iota(jnp.int32, (n_ext, 1), 0)
        cg = jnp.concatenate([pp_ref[:, d:2 * d], p_ref[:, d:2 * d]], axis=0)
        val = jnp.concatenate([pp_ref[:, 2 * d:], p_ref[:, 2 * d:]], axis=0)
        u = jnp.where(rows_u >= PAD, cg * val, 0.0)
        u1 = pltpu.roll(u, 1, 0)
        u2 = pltpu.roll(u, 2, 0)
        y = (w2 * u + w1 * u1 + w0 * u2)[8:]
        dzv = dz_ref[...]
        bg = p_ref[:, :d]
        rows_n = (i + 1) * tm + lax.broadcasted_iota(jnp.int32, (8, 1), 0)
        dy_main = dzv * bg
        dy_next = jnp.where(rows_n < t, dzn_ref[...] * pn_ref[:, :d], 0.0)
        dye = jnp.concatenate([dy_main, dy_next], axis=0)
        du = (w2 * dye + w1 * pltpu.roll(dye, n_ext - 1, 0)
              + w0 * pltpu.roll(dye, n_ext - 2, 0))[:tm]
        du = jnp.where(rows_u[8:] >= PAD, du, 0.0)
        dp_ref[:, :d] = (dzv * y).astype(bf16)
        dp_ref[:, d:2 * d] = (du * val[8:]).astype(bf16)
        dp_ref[:, 2 * d:] = (du * cg[8:]).astype(bf16)
        dw_ref[0:1, :] += jnp.sum(dy_main * u2[8:], axis=0, keepdims=True)
        dw_ref[1:2, :] += jnp.sum(dy_main * u1[8:], axis=0, keepdims=True)
        dw_ref[2:3, :] += jnp.sum(dy_main * u[8:], axis=0, keepdims=True)

    nxt = lambda i: (jnp.minimum((i + 1) * hb, last8), 0)
    return pl.pallas_call(
        body, name=name, grid=(t // tm,),
        in_specs=[pl.BlockSpec((tm, d), lambda i: (i, 0)),
                  pl.BlockSpec((8, d), nxt),
                  pl.BlockSpec((tm, d3), lambda i: (i, 0)),
                  pl.BlockSpec((8, d3), lambda i: (jnp.maximum(i * hb - 1, 0), 0)),
                  pl.BlockSpec((8, d3), nxt),
                  pl.BlockSpec((3, d), lambda i: (0, 0))],
        out_specs=[pl.BlockSpec((tm, d3), lambda i: (i, 0)),
                   pl.BlockSpec((3, d), lambda i: (0, 0))],
        out_shape=[jax.ShapeDtypeStruct((t, d3), bf16), jax.ShapeDtypeStruct((3, d), f32)],
        compiler_params=_params("arbitrary"),
    )(dz, dz, p, p, p, conv_w)


def _attn_stats(o, do, lse, ccol, name):
    t, d = o.shape
    n_heads = d // HEAD_DIM
    tm = _tile(t, 640, 16)

    def body(o_ref, do_ref, lse_ref, cc_ref, cl_ref, delta_ref):
        cl_ref[...] = cc_ref[...] - lse_ref[...]
        delta_ref[...] = jnp.sum(o_ref[...].astype(f32) * do_ref[...].astype(f32), axis=-1,
                                 keepdims=True)

    blk = pl.BlockSpec((tm, HEAD_DIM), lambda h, i: (i, h))
    col = pl.BlockSpec((None, tm, 1), lambda h, i: (h, i, 0))
    shp = jax.ShapeDtypeStruct((n_heads, t, 1), f32)
    return pl.pallas_call(
        body, name=name, grid=(n_heads, t // tm),
        in_specs=[blk, blk, col, col],
        out_specs=[col, col], out_shape=[shp, shp],
        compiler_params=_params("parallel", "parallel"),
    )(o, do, lse, ccol)


def _attn_bwd(q, k, v, do, cl, delta, crow, name):
    t, d = q.shape
    n_heads = d // HEAD_DIM
    bk = crow.shape[-1]
    bq = bk
    n_q = t // bq
    scale = 1.0 / math.sqrt(HEAD_DIM)

    def body(q_ref, do_ref, cl_ref, dl_ref, k_ref, v_ref, cr_ref,
             dq_ref, dr_ref, dk_ref, dv_ref, dc_ref):
        j = pl.program_id(1)

        @pl.when(j == 0)
        def _():
            dq_ref[...] = jnp.zeros_like(dq_ref)
            dr_ref[...] = jnp.zeros_like(dr_ref)

        kj = k_ref[...]
        vj = v_ref[...]
        ck = cr_ref[...]
        row0 = lax.broadcasted_iota(jnp.int32, (bq, bk), 0)
        col = j * bk + lax.broadcasted_iota(jnp.int32, (bq, bk), 1)

        def step(i, carry):
            dk_acc, dv_acc, dc_acc = carry
            off = pl.multiple_of(i * bq, bq)
            qi = q_ref[pl.ds(off, bq), :]
            doi = do_ref[pl.ds(off, bq), :]
            s = lax.dot_general(qi, kj, NT_DIMS, preferred_element_type=f32) * scale
            s = s + cl_ref[pl.ds(off, bq), :] - ck
            s = jnp.where(col <= row0 + i * bq, s, NEG_INF)
            pr = jnp.exp(s)
            dv_acc = dv_acc + lax.dot_general(pr.astype(bf16), doi, TN_DIMS,
                                              preferred_element_type=f32)
            dp = lax.dot_general(doi, vj, NT_DIMS, preferred_element_type=f32)
            ds = pr * (dp - dl_ref[pl.ds(off, bq), :])
            dc_acc = dc_acc - jnp.sum(ds, axis=0, keepdims=True)
            dr_ref[pl.ds(off, bq), :] += jnp.sum(ds, axis=1, keepdims=True)
            dsb = ds.astype(bf16)
            dk_acc = dk_acc + lax.dot_general(dsb, qi, TN_DIMS, preferred_element_type=f32)
            dq_ref[pl.ds(off, bq), :] += scale * jnp.dot(dsb, kj, preferred_element_type=f32)
            return dk_acc, dv_acc, dc_acc

        init = (jnp.zeros((bk, HEAD_DIM), f32), jnp.zeros((bk, HEAD_DIM), f32),
                jnp.zeros((1, bk), f32))
        dk_acc, dv_acc, dc_acc = lax.fori_loop(j, n_q, step, init)
        dk_ref[...] = (dk_acc * scale).astype(bf16)
        dv_ref[...] = dv_acc.astype(bf16)
        dc_ref[...] = dc_acc

    head_rows = pl.BlockSpec((t, HEAD_DIM), lambda h, j: (0, h))
    head_col = pl.BlockSpec((None, t, 1), lambda h, j: (h, 0, 0))
    head_col_in = pl.BlockSpec((None, t, 1), lambda h, j: (h, 0, 0), pipeline_mode=pl.Buffered(1))
    kblk = pl.BlockSpec((bk, HEAD_DIM), lambda h, j: (j, h))
    crow_blk = pl.BlockSpec((None, None, 1, bk), lambda h, j: (h, j, 0, 0))
    return pl.pallas_call(
        body, name=name, grid=(n_heads, t // bk),
        in_specs=[head_rows, head_rows, head_col_in, head_col_in, kblk, kblk, crow_blk],
        out_specs=[head_rows, head_col, kblk, kblk, crow_blk],
        out_shape=[jax.ShapeDtypeStruct((t, d), f32), jax.ShapeDtypeStruct((n_heads, t, 1), f32),
                   jax.ShapeDtypeStruct((t, d), bf16), jax.ShapeDtypeStruct((t, d), bf16),
                   jax.ShapeDtypeStruct((n_heads, t // bk, 1, bk), f32)],
        compiler_params=_params("parallel", "arbitrary"),
    )(q, do, cl, delta, k, v, crow)


def _fgate_bwd(dlogf, flog, fbias, name):
    t, n = flog.shape

    def body(dl_ref, fl_ref, fb_ref, o_ref, sum_ref):
        i = pl.program_id(0)

        @pl.when(i == 0)
        def _():
            sum_ref[...] = jnp.zeros_like(sum_ref)

        r = i * LANES + lax.broadcasted_iota(jnp.int32, (LANES, n), 0)
        g = dl_ref[...] * jax.nn.sigmoid(-(fl_ref[...] + fb_ref[...]))
        g = jnp.where(r >= PAD, g, 0.0)
        o_ref[...] = g
        sum_ref[...] += jnp.sum(g, axis=0, keepdims=True)

    blk = pl.BlockSpec((LANES, n), lambda i: (i, 0))
    vec = pl.BlockSpec((1, n), lambda i: (0, 0))
    return pl.pallas_call(
        body, name=name, grid=(t // LANES,),
        in_specs=[blk, blk, vec], out_specs=[blk, vec],
        out_shape=[jax.ShapeDtypeStruct((t, n), f32), jax.ShapeDtypeStruct((1, n), f32)],
        compiler_params=_params("arbitrary"),
    )(dlogf, flog, fbias)


def _place():
    cx, cy, c = lax.axis_index("x"), lax.axis_index("y"), lax.axis_index("c")
    chips = [(1 - cx, cy), (cx, 1 - cy), (1 - cx, 1 - cy)]
    return cx, cy, c, chips


def _hbm_specs(n):
    return [pl.BlockSpec(memory_space=pl.ANY)] * n


def _place_shard(w, idx, name):
    _, r, c_n = w.shape
    out_dtype = bf16 if r * c_n > 2 ** 16 else w.dtype
    tr = _tile(r, 512, 16) if r % 16 == 0 else r

    def body(idx_ref, w_ref, o_ref):
        o_ref[...] = w_ref[...].astype(out_dtype)

    grid_spec = pltpu.PrefetchScalarGridSpec(
        num_scalar_prefetch=1, grid=(2, r // tr),
        in_specs=[pl.BlockSpec((None, tr, c_n), lambda h, i, idx: (h, i, 0))],
        out_specs=pl.BlockSpec((None, None, tr, c_n), lambda h, i, idx: (idx[0], h, i, 0)))
    return pl.pallas_call(
        body, name=name, grid_spec=grid_spec,
        out_shape=jax.ShapeDtypeStruct((N_CHIPS, 2, r, c_n), out_dtype),
        compiler_params=_params("parallel", "parallel"),
    )(idx, w)


def _all_gather(bufs, name):
    n_t = len(bufs)

    def body(*refs):
        outs = refs[n_t:2 * n_t]
        send, recv = refs[2 * n_t:]
        cx, cy, c, chips = _place()
        me = 2 * cx + cy
        sib = (cx, cy, 1 - c)

        def copy(t, sem, slot, to):
            return pltpu.make_async_remote_copy(src_ref=slot, dst_ref=slot, send_sem=send.at[6 * t + sem],
                                                recv_sem=recv.at[6 * t + sem], device_id=to,
                                                device_id_type=MESH)

        sent = []
        for t in range(n_t):
            for k, chip in enumerate(chips):
                cp = copy(t, k, outs[t].at[me, c], (*chip, c))
                cp.start()
                sent.append(cp)
        for t in range(n_t):
            for k, (px, py) in enumerate(chips):
                slot = outs[t].at[2 * px + py, c]
                copy(t, k, slot, sib).wait_recv()
                cp = copy(t, 3 + k, slot, sib)
                cp.start()
                sent.append(cp)
        for t in range(n_t):
            for k, (px, py) in enumerate(chips):
                copy(t, 3 + k, outs[t].at[2 * px + py, 1 - c], sib).wait_recv()
        for cp in sent:
            cp.wait_send()

    return pl.pallas_call(
        body, name=name,
        in_specs=_hbm_specs(n_t), out_specs=_hbm_specs(n_t),
        out_shape=[jax.ShapeDtypeStruct(x.shape, x.dtype) for x in bufs],
        input_output_aliases={t: t for t in range(n_t)},
        scratch_shapes=[pltpu.SemaphoreType.DMA((6 * n_t,)), pltpu.SemaphoreType.DMA((6 * n_t,))],
    )(*bufs)


def _pair_exchange(grads, name):
    n_t = len(grads)

    def body(*refs):
        gs, outs = refs[:n_t], refs[n_t:2 * n_t]
        send, recv = refs[2 * n_t:]
        cx, cy, c, chips = _place()
        sib = (cx, cy, 1 - c)
        slots = [2 * cx + cy] + [2 * px + py for px, py in chips]
        copies = []
        for t in range(n_t):
            for k, slot in enumerate(slots):
                cp = pltpu.make_async_remote_copy(
                    src_ref=gs[t].at[slot, 1 - c], dst_ref=outs[t].at[k],
                    send_sem=send.at[4 * t + k], recv_sem=recv.at[4 * t + k],
                    device_id=sib, device_id_type=MESH)
                cp.start()
                copies.append(cp)
        for cp in copies:
            cp.wait()

    return pl.pallas_call(
        body, name=name,
        in_specs=_hbm_specs(n_t), out_specs=_hbm_specs(n_t),
        out_shape=[jax.ShapeDtypeStruct((N_CHIPS, *g.shape[2:]), f32) for g in grads],
        scratch_shapes=[pltpu.SemaphoreType.DMA((4 * n_t,)), pltpu.SemaphoreType.DMA((4 * n_t,))],
    )(*grads)


def _chip_exchange(sends, name):
    n_t = len(sends)

    def body(*refs):
        xs, outs = refs[:n_t], refs[n_t:2 * n_t]
        send, recv = refs[2 * n_t:]
        cx, cy, c, chips = _place()
        copies = []
        for t in range(n_t):
            for k, chip in enumerate(chips):
                cp = pltpu.make_async_remote_copy(
                    src_ref=xs[t].at[k], dst_ref=outs[t].at[k],
                    send_sem=send.at[3 * t + k], recv_sem=recv.at[3 * t + k],
                    device_id=(*chip, c), device_id_type=MESH)
                cp.start()
                copies.append(cp)
        for cp in copies:
            cp.wait()

    return pl.pallas_call(
        body, name=name,
        in_specs=_hbm_specs(n_t), out_specs=_hbm_specs(n_t),
        out_shape=[jax.ShapeDtypeStruct(x.shape, x.dtype) for x in sends],
        scratch_shapes=[pltpu.SemaphoreType.DMA((3 * n_t,)), pltpu.SemaphoreType.DMA((3 * n_t,))],
    )(*sends)


def _pair_share(bufs, name):
    n_t = len(bufs)

    def body(*refs):
        outs = refs[n_t:2 * n_t]
        send, recv = refs[2 * n_t:]
        cx, cy, c, _ = _place()
        sib = (cx, cy, 1 - c)
        copies = []
        for t in range(n_t):
            cp = pltpu.make_async_remote_copy(
                src_ref=outs[t].at[c], dst_ref=outs[t].at[c], send_sem=send.at[t], recv_sem=recv.at[t],
                device_id=sib, device_id_type=MESH)
            cp.start()
            copies.append(cp)
        for t in range(n_t):
            copies[t].wait_send()
            pltpu.make_async_remote_copy(
                src_ref=outs[t].at[1 - c], dst_ref=outs[t].at[1 - c], send_sem=send.at[t],
                recv_sem=recv.at[t], device_id=sib, device_id_type=MESH).wait_recv()

    return pl.pallas_call(
        body, name=name,
        in_specs=_hbm_specs(n_t), out_specs=_hbm_specs(n_t),
        out_shape=[jax.ShapeDtypeStruct(x.shape, x.dtype) for x in bufs],
        input_output_aliases={t: t for t in range(n_t)},
        scratch_shapes=[pltpu.SemaphoreType.DMA((n_t,)), pltpu.SemaphoreType.DMA((n_t,))],
    )(*bufs)


def _rs_prepare(g, pair, idx, name):
    _, _, r, c_n = g.shape
    tr = _tile(r, 256, 8)

    def body(idx_ref, g_ref, p_ref, o_ref):
        o_ref[...] = (g_ref[...] + p_ref[...]).astype(bf16)

    grid_spec = pltpu.PrefetchScalarGridSpec(
        num_scalar_prefetch=1, grid=(3, r // tr),
        in_specs=[pl.BlockSpec((None, None, tr, c_n), lambda k, i, idx: (idx[k + 1], idx[4], i, 0)),
                  pl.BlockSpec((None, tr, c_n), lambda k, i, idx: (k + 1, i, 0))],
        out_specs=pl.BlockSpec((None, tr, c_n), lambda k, i, idx: (k, i, 0)))
    return pl.pallas_call(
        body, name=name, grid_spec=grid_spec,
        out_shape=jax.ShapeDtypeStruct((3, r, c_n), bf16),
        compiler_params=_params("parallel", "parallel"),
    )(idx, g, pair)


def _rs_finish(g, pair, recv, idx, name):
    _, _, r, c_n = g.shape
    tr = _tile(r, 256, 8)

    def body(idx_ref, g_ref, p_ref, r0_ref, r1_ref, r2_ref, o_ref):
        acc = g_ref[...] + p_ref[...]
        acc = acc + r0_ref[...].astype(f32)
        acc = acc + r1_ref[...].astype(f32)
        acc = acc + r2_ref[...].astype(f32)
        o_ref[...] = acc

    def rspec(k):
        return pl.BlockSpec((None, tr, c_n), functools.partial(lambda i, idx, kk: (kk, i, 0), kk=k))

    grid_spec = pltpu.PrefetchScalarGridSpec(
        num_scalar_prefetch=1, grid=(r // tr,),
        in_specs=[pl.BlockSpec((None, None, tr, c_n), lambda i, idx: (idx[0], idx[4], i, 0)),
                  rspec(0), rspec(0), rspec(1), rspec(2)],
        out_specs=pl.BlockSpec((None, tr, c_n), lambda i, idx: (idx[4], i, 0)))
    return pl.pallas_call(
        body, name=name, grid_spec=grid_spec,
        out_shape=jax.ShapeDtypeStruct((2, r, c_n), f32),
        compiler_params=_params("parallel"),
    )(idx, g, pair, recv, recv, recv)


def _adamw(w, g, m, v, name):
    r, c_n = w.shape
    tr = _tile(r, 256, 8) if r % 8 == 0 else r

    def body(w_ref, g_ref, m_ref, v_ref, d_ref, mo_ref, vo_ref):
        gv = g_ref[...]
        mn = ADAM_B1 * m_ref[...] + (1.0 - ADAM_B1) * gv
        vn = ADAM_B2 * v_ref[...] + (1.0 - ADAM_B2) * (gv * gv)
        m_hat = mn / (1.0 - ADAM_B1 ** ADAM_STEP)
        v_hat = vn / (1.0 - ADAM_B2 ** ADAM_STEP)
        d_ref[...] = -ADAM_LR * (m_hat / (jnp.sqrt(v_hat) + ADAM_EPS) + ADAM_WD * w_ref[...])
        mo_ref[...] = mn
        vo_ref[...] = vn

    blk = pl.BlockSpec((tr, c_n), lambda i: (i, 0))
    shp = jax.ShapeDtypeStruct((r, c_n), f32)
    return pl.pallas_call(
        body, name=name, grid=(r // tr,),
        in_specs=[blk] * 4, out_specs=[blk] * 3, out_shape=[shp] * 3,
        compiler_params=_params("parallel"),
    )(w, g, m, v)


def _halves(a):
    flat = a.reshape(-1, a.shape[-1])
    return flat.reshape(2, flat.shape[0] // 2, flat.shape[-1])


def kernel(x, meta, ffn1_wg, ffn1_wu, ffn1_wd, ffn2_wg, ffn2_wu, ffn2_wd, ln_gain, ln_bias, conv_w_in, conv_w, conv_w_out, kv_w, f_bias, attn_w_q, attn_w_o, loss_target, m_meta, m_ffn1_wg, m_ffn1_wu, m_ffn1_wd, m_ffn2_wg, m_ffn2_wu, m_ffn2_wd, m_ln_gain, m_ln_bias, m_conv_w_in, m_conv_w, m_conv_w_out, m_kv_w, m_f_bias, m_attn_w_q, m_attn_w_o, v_meta, v_ffn1_wg, v_ffn1_wu, v_ffn1_wd, v_ffn2_wg, v_ffn2_wu, v_ffn2_wd, v_ln_gain, v_ln_bias, v_conv_w_in, v_conv_w, v_conv_w_out, v_kv_w, v_f_bias, v_attn_w_q, v_attn_w_o):
    seq, d = x.shape[1], x.shape[2]
    t = PAD + N_META + seq
    n_heads = d // HEAD_DIM
    dq = d // N_CHIPS
    n_kv = kv_w.shape[1]
    x2 = x[0]
    target = loss_target[0]

    def rows8(a):
        return jnp.pad(a, ((0, 8 - a.shape[0]), (0, 0)))

    def small_pack(mt, g, b, cw, fb):
        fb_row = jnp.pad(fb, (0, mt.shape[1] - n_heads))[None]
        return jnp.concatenate([mt, rows8(g.reshape(6, -1)), rows8(b.reshape(6, -1)),
                                rows8(cw.reshape(3, -1)), rows8(fb_row)], axis=0)

    w_small = small_pack(meta, ln_gain, ln_bias, conv_w, f_bias)

    cx, cy, c = lax.axis_index("x"), lax.axis_index("y"), lax.axis_index("c")
    idx = jnp.stack([2 * cx + cy, 2 * (1 - cx) + cy, 2 * cx + (1 - cy), 2 * (1 - cx) + (1 - cy), c]
                    ).astype(jnp.int32)

    shards = [ffn1_wg, ffn1_wu, ffn1_wd, ffn2_wg, ffn2_wu, ffn2_wd,
              _halves(conv_w_in), _halves(conv_w_out), _halves(kv_w), _halves(attn_w_q),
              _halves(attn_w_o), _halves(w_small)]
    gw = _all_gather([_place_shard(w, idx, "place_shard") for w in shards], "all_gather_weights")
    wg1, wu1, wd1, wg2, wu2, wd2 = gw[:6]
    n_in = conv_w_in.shape[-1]
    w_in = gw[6].reshape(N_CHIPS, d, n_in)
    w_out = gw[7].reshape(1, 1, d, d)
    w_q = gw[9].reshape(1, d, d)
    w_o = gw[10].reshape(1, 1, d, d)
    kv_full = gw[8].reshape(N_CHIPS, d, n_kv).transpose(1, 0, 2).reshape(d, N_CHIPS * n_kv)
    w_k = kv_full[:, :d][None]
    w_v = kv_full[:, d:2 * d][None]
    w_f = jnp.pad(kv_full[:, 2 * d:], ((0, 0), (0, LANES - n_heads)))[None]
    small = gw[11].reshape(N_CHIPS, SMALL_ROWS, dq).transpose(1, 0, 2).reshape(SMALL_ROWS, d)
    meta_full = small[:N_META]
    gains = small[16:22].reshape(DEPTH, 3, 1, d)
    biases = small[24:30].reshape(DEPTH, 3, 1, d)
    conv_w_full = small[32:35]
    fb_pad = jnp.pad(f_bias, (0, LANES - n_heads))[None]

    meta_pad = jnp.concatenate([jnp.zeros((PAD, d), f32), meta_full], axis=0)
    h0, h0b = _embed(meta_pad, x2, "embed")
    a1, b1, s1 = _ffn_up(h0b, wg1, wu1, 0, "ffn_up")
    r1, h1, h1b = _down_ln(s1, wd1, 0, h0, gains[0, 0], biases[0, 0], 0.5, "ffn_down_ln")
    p = _nn_matmul(h1b, w_in, f32, "conv_in")
    z = _conv_fwd(p, conv_w_full, "conv_fwd")
    r2, h2, h2b = _down_ln(z[None], w_out, 0, h1, gains[0, 1], biases[0, 1], 1.0, "mix_out_ln")
    a2, b2, s2 = _ffn_up(h2b, wg2, wu2, 0, "ffn_up")
    r3, h3, h3b = _down_ln(s2, wd2, 0, h2, gains[0, 2], biases[0, 2], 0.5, "ffn_down_ln")
    kk = _nn_matmul(h3b, w_k, bf16, "proj_bf16")
    vv = _nn_matmul(h3b, w_v, bf16, "proj_bf16")
    flog = _nn_matmul(h3b, w_f, f32, "proj_gate")
    cum = _row_scan(flog, "gate_cumsum", fb_pad)
    bk = _tile(t, 640, LANES)
    c_ht = cum[:, :n_heads].T
    ccol = c_ht[:, :, None]
    crow = jnp.where(jnp.arange(t)[None, :] < PAD, 1e30, c_ht).reshape(n_heads, t // bk, 1, bk)
    a3, b3, s3 = _ffn_up(h3b, wg1, wu1, 1, "ffn_up")
    r4, h4, h4b = _down_ln(s3, wd1, 1, h3, gains[1, 0], biases[1, 0], 0.5, "ffn_down_ln")
    q = _nn_matmul(h4b, w_q, bf16, "proj_bf16")
    o, o32, lse = _attn_fwd(q, kk, vv, ccol, crow, "attn_fwd")
    r5, h5, h5b = _down_ln(o[None], w_o, 0, h4, gains[1, 1], biases[1, 1], 1.0, "mix_out_ln")
    a4, b4, s4 = _ffn_up(h5b, wg2, wu2, 1, "ffn_up")
    r6, h6, _ = _down_ln(s4, wd2, 1, h5, gains[1, 2], biases[1, 2], 0.5, "ffn_down_ln")
    dy, sq = _loss_head(h6, target, "loss_head")
    loss_part = 0.5 * sq[0, 0] / d

    def ffn_bwd(dr, drb, hb_in, a, b, s, wg, wu, wd, layer, into):
        da, db = _ffn_bwd_act(drb, wd, layer, a, b, "ffn_bwd_act")
        into_d = None if into is None else [into[2]]
        into_gu = None if into is None else [into[0], into[1]]
        (dwd,) = _tn_matmul(s, "stack", [drb], ["shared"], N_CHIPS, 0.5, "ffn_dwd",
                            layers=DEPTH, layer=layer, into=into_d)
        dwg, dwu = _tn_matmul(hb_in, "shared", [da, db], ["stack", "stack"], N_CHIPS, 1.0, "ffn_dwgu",
                              layers=DEPTH, layer=layer, into=into_gu)
        dh = _nt_sum([(da, "stack", wg, layer), (db, "stack", wu, layer)], dr, ALPHA, N_CHIPS, f32,
                     "ffn_dx")
        return dh, (dwg, dwu, dwd)

    dr6, dr6b, dg12, db12 = _ln_bwd(dy, r6, gains[1, 2], "ln_bwd")
    dh5, gf2 = ffn_bwd(dr6, dr6b, h5b, a4, b4, s4, wg2, wu2, wd2, 1, None)
    dr5, dr5b, dg11, db11 = _ln_bwd(dh5, r5, gains[1, 1], "ln_bwd")
    (dwo,) = _tn_matmul(o, "shared", [dr5b], ["shared"], 1, 1.0, "sq_dw")
    do = _nt_sum([(dr5b, "cols", w_o[0], None)], None, 1.0, 1, bf16, "sq_dx_bf16")
    cl, delta = _attn_stats(o32, do, lse, ccol, "attn_stats")
    dq_att, dc_q, dk, dv, dc_k = _attn_bwd(q, kk, vv, do, cl, delta, crow, "attn_bwd")
    dc = dc_q[:, :, 0] + dc_k.reshape(n_heads, t)
    (dwq,) = _tn_matmul(h4b, "shared", [dq_att], ["shared"], 1, 1.0, "sq_dw")
    dh4 = _nt_sum([(dq_att, "cols", w_q, None)], dr5, ALPHA, 1, f32, "sq_dx_res")
    dr4, dr4b, dg10, db10 = _ln_bwd(dh4, r4, gains[1, 0], "ln_bwd")
    dh3a, gf1 = ffn_bwd(dr4, dr4b, h3b, a3, b3, s3, wg1, wu1, wd1, 1, None)
    dc_t = jnp.pad(dc.T, ((0, 0), (0, LANES - n_heads)))
    dlogf = _row_scan(dc_t[::-1], "rev_cumsum")[::-1]
    dfl, dfb_cols = _fgate_bwd(dlogf, flog, fb_pad, "gate_bwd")
    dwk, dwv, dwf = _tn_matmul(h3b, "shared", [dk, dv, dfl], ["shared"] * 3, 1, 1.0, "kv_dw")
    dh3 = _nt_sum([(dk, "cols", w_k, None), (dv, "cols", w_v, None), (dfl, "cols", w_f, None)],
                  dh3a, 1.0, 1, f32, "kv_dx")
    dr3, dr3b, dg02, db02 = _ln_bwd(dh3, r3, gains[0, 2], "ln_bwd")
    dh2, gf2 = ffn_bwd(dr3, dr3b, h2b, a2, b2, s2, wg2, wu2, wd2, 0, gf2)
    dr2, dr2b, dg01, db01 = _ln_bwd(dh2, r2, gains[0, 1], "ln_bwd")
    (dwout,) = _tn_matmul(z, "shared", [dr2b], ["shared"], 1, 1.0, "sq_dw")
    dz = _nt_sum([(dr2b, "cols", w_out[0], None)], None, 1.0, 1, f32, "sq_dx_f32")
    dp, dconv_w = _conv_bwd(dz, p, conv_w_full, "conv_bwd")
    (dwin,) = _tn_matmul(h1b, "shared", [dp], ["cols"], N_CHIPS, 1.0, "conv_dwin")
    dh1 = _nt_sum([(dp, "cols", w_in, None)], dr2, ALPHA, N_CHIPS, f32, "conv_dx")
    dr1, dr1b, dg00, db00 = _ln_bwd(dh1, r1, gains[0, 0], "ln_bwd")
    dh0, gf1 = ffn_bwd(dr1, dr1b, h0b, a1, b1, s1, wg1, wu1, wd1, 0, gf1)
    grad_x = dh0[PAD + N_META:][None]
    dmeta = dh0[PAD:PAD + N_META]

    def by_chip(full, rows):
        n = full.shape[1] // N_CHIPS
        return full.reshape(rows, N_CHIPS, n).transpose(1, 0, 2).reshape(N_CHIPS, 2, rows // 2, n)

    dkv_full = jnp.concatenate([dwk[0], dwv[0], dwf[0][:, :n_heads]], axis=1)
    d_small = jnp.concatenate(
        [dmeta, rows8(jnp.concatenate([dg00, dg01, dg02, dg10, dg11, dg12], axis=0)),
         rows8(jnp.concatenate([db00, db01, db02, db10, db11, db12], axis=0)),
         rows8(dconv_w), jnp.zeros((8, d), f32)], axis=0)
    grads = [gf1[0], gf1[1], gf1[2], gf2[0], gf2[1], gf2[2],
             dwin.reshape(N_CHIPS, 2, d // 2, n_in),
             dwout.reshape(N_CHIPS, 2, dq // 2, d),
             by_chip(dkv_full, d),
             dwq.reshape(N_CHIPS, 2, dq // 2, d),
             dwo.reshape(N_CHIPS, 2, dq // 2, d),
             by_chip(d_small, SMALL_ROWS)]
    pair = _pair_exchange(grads, "grad_pair_exchange")
    sends = [_rs_prepare(g, pr, idx, "grad_prepare") for g, pr in zip(grads, pair)]
    recvs = _chip_exchange(sends, "grad_chip_exchange")
    halves = [_rs_finish(g, pr, rc, idx, "grad_finish") for g, pr, rc in zip(grads, pair, recvs)]
    red = _pair_share(halves, "grad_pair_share")

    tail = jnp.zeros((LANES,), f32).at[:n_heads].set(dfb_cols[0, :n_heads]).at[n_heads].set(loss_part)
    tail = lax.psum(tail, ("x", "y", "c"))
    loss = tail[n_heads]
    g_fb = tail[:n_heads]

    params = [ffn1_wg, ffn1_wu, ffn1_wd, ffn2_wg, ffn2_wu, ffn2_wd, conv_w_in, conv_w_out, kv_w,
              attn_w_q, attn_w_o]
    ms = [m_ffn1_wg, m_ffn1_wu, m_ffn1_wd, m_ffn2_wg, m_ffn2_wu, m_ffn2_wd, m_conv_w_in, m_conv_w_out,
          m_kv_w, m_attn_w_q, m_attn_w_o]
    vs = [v_ffn1_wg, v_ffn1_wu, v_ffn1_wd, v_ffn2_wg, v_ffn2_wu, v_ffn2_wd, v_conv_w_in, v_conv_w_out,
          v_kv_w, v_attn_w_q, v_attn_w_o]
    out_g, out_d, out_m, out_v = [], [], [], []
    for w, g, m, v in zip(params, red[:11], ms, vs):
        w2 = w.reshape(-1, w.shape[-1])
        g2 = g.reshape(w2.shape)
        dl, mn, vn = _adamw(w2, g2, m.reshape(w2.shape), v.reshape(w2.shape), "adamw")
        out_g.append(g2.reshape(w.shape))
        out_d.append(dl.reshape(w.shape))
        out_m.append(mn.reshape(w.shape))
        out_v.append(vn.reshape(w.shape))
    g_small = red[11].reshape(SMALL_ROWS, dq)
    g_small = jnp.concatenate([g_small[:40], rows8(jnp.pad(g_fb, (0, dq - n_heads))[None])], axis=0)
    m_small = small_pack(m_meta, m_ln_gain, m_ln_bias, m_conv_w, m_f_bias)
    v_small = small_pack(v_meta, v_ln_gain, v_ln_bias, v_conv_w, v_f_bias)
    d_s, m_s, v_s = _adamw(w_small, g_small, m_small, v_small, "adamw_small")

    def unpack(pk):
        return (pk[:16], pk[16:22].reshape(DEPTH, 3, dq), pk[24:30].reshape(DEPTH, 3, dq),
                pk[32:35].reshape(1, 3, dq), pk[40, :n_heads])

    def order(big, pk):
        mt, g, b, cw, fb = unpack(pk)
        (f1g, f1u, f1d, f2g, f2u, f2d, cin, cout, kvw, aq, ao) = big
        return [mt, f1g, f1u, f1d, f2g, f2u, f2d, g, b, cin, cw, cout, kvw, fb, aq, ao]

    return (loss, grad_x, *order(out_g, g_small), *order(out_d, d_s), *order(out_m, m_s),
            *order(out_v, v_s))
```

```python
import functools
import math

import jax
import jax.numpy as jnp
from jax import lax
from jax.experimental import pallas as pl
from jax.experimental.pallas import tpu as pltpu

f32 = jnp.float32
bf16 = jnp.bfloat16

N_META = 16
PAD = 112
HEAD_DIM = 128
DEPTH = 2
LN_EPS = 1e-5
ALPHA = (2 * DEPTH) ** 0.25
NEG_INF = -1e30
N_CHIPS = 4
SMALL_ROWS = 48
LANES = 128

ADAM_LR = 0.001
ADAM_B1 = 0.9
ADAM_B2 = 0.999
ADAM_EPS = 1e-08
ADAM_WD = 0.01
ADAM_STEP = 10

VMEM_LIMIT_BYTES = 56 * 1024 * 1024
MESH = pl.DeviceIdType.MESH

NT_DIMS = (((1,), (1,)), ((), ()))
TN_DIMS = (((0,), (0,)), ((), ()))


def _tile(n, target, mult):
    best = None
    for d in range(mult, min(n, target) + 1, mult):
        if n % d == 0:
            best = d
    assert best is not None, (n, target, mult)
    return best


def _params(*sem):
    return pltpu.CompilerParams(dimension_semantics=sem, vmem_limit_bytes=VMEM_LIMIT_BYTES)


def _embed(meta_pad, x, name):
    seq, d = x.shape
    t = seq + LANES

    def body(m_ref, x_ref, h_ref, hb_ref):
        first = pl.program_id(0) == 0
        v = jnp.where(first, m_ref[...], x_ref[...])
        h_ref[...] = v
        hb_ref[...] = v.astype(bf16)

    return pl.pallas_call(
        body, name=name, grid=(t // LANES,),
        in_specs=[pl.BlockSpec((LANES, d), lambda i: (0, 0)),
                  pl.BlockSpec((LANES, d), lambda i: (jnp.maximum(i - 1, 0), 0))],
        out_specs=[pl.BlockSpec((LANES, d), lambda i: (i, 0)),
                   pl.BlockSpec((LANES, d), lambda i: (i, 0))],
        out_shape=[jax.ShapeDtypeStruct((t, d), f32), jax.ShapeDtypeStruct((t, d), bf16)],
        compiler_params=_params("parallel"),
    )(meta_pad, x)


def _nn_matmul(x, w, out_dtype, name):
    t, k = x.shape
    s_n, _, n = w.shape
    assert s_n == 1 or n % LANES == 0
    tm = _tile(t, 640, 16)

    def body(x_ref, w_ref, o_ref):
        o_ref[...] = jnp.dot(x_ref[...].astype(bf16), w_ref[...],
                             preferred_element_type=f32).astype(o_ref.dtype)

    return pl.pallas_call(
        body, name=name, grid=(s_n, t // tm),
        in_specs=[pl.BlockSpec((tm, k), lambda s, i: (i, 0)),
                  pl.BlockSpec((None, k, n), lambda s, i: (s, 0, 0))],
        out_specs=pl.BlockSpec((tm, n), lambda s, i: (i, s)),
        out_shape=jax.ShapeDtypeStruct((t, s_n * n), out_dtype),
        compiler_params=_params("parallel", "parallel"),
    )(x, w)


def _ffn_up(hb, wg, wu, layer, name):
    t, d = hb.shape
    s_n, _, _, n = wg.shape
    tm = _tile(t, 640, 16)

    def body(x_ref, wg_ref, wu_ref, a_ref, b_ref, s_ref):
        x = x_ref[...]
        a = jnp.dot(x, wg_ref[...], preferred_element_type=f32)
        b = jnp.dot(x, wu_ref[...], preferred_element_type=f32)
        a_ref[...] = a
        b_ref[...] = b
        s_ref[...] = (a * jax.nn.sigmoid(a) * b).astype(bf16)

    wspec = pl.BlockSpec((None, None, d, n), lambda s, i: (s, layer, 0, 0))
    ospec = pl.BlockSpec((None, tm, n), lambda s, i: (s, i, 0))
    return pl.pallas_call(
        body, name=name, grid=(s_n, t // tm),
        in_specs=[pl.BlockSpec((tm, d), lambda s, i: (i, 0)), wspec, wspec],
        out_specs=[ospec, ospec, ospec],
        out_shape=[jax.ShapeDtypeStruct((s_n, t, n), f32), jax.ShapeDtypeStruct((s_n, t, n), f32),
                   jax.ShapeDtypeStruct((s_n, t, n), bf16)],
        compiler_params=_params("parallel", "parallel"),
    )(hb, wg, wu)


def _down_ln(x, w, layer, hprev, gain, bias, beta, name):
    s_n, t, k = x.shape
    d = w.shape[-1]
    tm = _tile(t, 640, 16)

    def body(x_ref, w_ref, h_ref, g_ref, b_ref, r_out, h_out, hb_out, acc):
        s = pl.program_id(1)

        @pl.when(s == 0)
        def _():
            acc[...] = jnp.zeros_like(acc)

        acc[...] += jnp.dot(x_ref[...], w_ref[...], preferred_element_type=f32)

        @pl.when(s == s_n - 1)
        def _():
            r = ALPHA * h_ref[...] + beta * acc[...]
            mu = jnp.mean(r, axis=-1, keepdims=True)
            xc = r - mu
            var = jnp.mean(xc * xc, axis=-1, keepdims=True)
            y = xc * lax.rsqrt(var + LN_EPS) * g_ref[...] + b_ref[...]
            r_out[...] = r
            h_out[...] = y
            hb_out[...] = y.astype(bf16)

    row = pl.BlockSpec((tm, d), lambda i, s: (i, 0))
    vec = pl.BlockSpec((1, d), lambda i, s: (0, 0))
    return pl.pallas_call(
        body, name=name, grid=(t // tm, s_n),
        in_specs=[pl.BlockSpec((None, tm, k), lambda i, s: (s, i, 0)),
                  pl.BlockSpec((None, None, k, d), lambda i, s: (s, layer, 0, 0)),
                  row, vec, vec],
        out_specs=[row, row, row],
        out_shape=[jax.ShapeDtypeStruct((t, d), f32), jax.ShapeDtypeStruct((t, d), f32),
                   jax.ShapeDtypeStruct((t, d), bf16)],
        scratch_shapes=[pltpu.VMEM((tm, d), f32)],
        compiler_params=_params("parallel", "arbitrary"),
    )(x, w, hprev, gain, bias)


def _conv_fwd(p, conv_w, name):
    t, d3 = p.shape
    d = d3 // 3
    tm = _tile(t, 320, 8)
    hb = tm // 8

    def body(p_ref, prev_ref, w_ref, z_ref):
        i = pl.program_id(0)
        rows = i * tm - 8 + lax.broadcasted_iota(jnp.int32, (tm + 8, 1), 0)
        cg = jnp.concatenate([prev_ref[:, d:2 * d], p_ref[:, d:2 * d]], axis=0)
        val = jnp.concatenate([prev_ref[:, 2 * d:], p_ref[:, 2 * d:]], axis=0)
        u = jnp.where(rows >= PAD, cg * val, 0.0)
        y = (w_ref[2:3, :] * u + w_ref[1:2, :] * pltpu.roll(u, 1, 0)
             + w_ref[0:1, :] * pltpu.roll(u, 2, 0))
        z_ref[...] = (p_ref[:, :d] * y[8:]).astype(bf16)

    return pl.pallas_call(
        body, name=name, grid=(t // tm,),
        in_specs=[pl.BlockSpec((tm, d3), lambda i: (i, 0)),
                  pl.BlockSpec((8, d3), lambda i: (jnp.maximum(i * hb - 1, 0), 0)),
                  pl.BlockSpec((3, d), lambda i: (0, 0))],
        out_specs=pl.BlockSpec((tm, d), lambda i: (i, 0)),
        out_shape=jax.ShapeDtypeStruct((t, d), bf16),
        compiler_params=_params("parallel"),
    )(p, p, conv_w)


def _row_scan(x, name, fbias=None):
    t, n = x.shape
    blk = LANES
    gate = fbias is not None

    def body(*refs):
        if gate:
            x_ref, fb_ref, o_ref, carry = refs
        else:
            x_ref, o_ref, carry = refs
        i = pl.program_id(0)

        @pl.when(i == 0)
        def _():
            carry[...] = jnp.zeros_like(carry)

        v = x_ref[...]
        r = lax.broadcasted_iota(jnp.int32, (blk, n), 0)
        if gate:
            v = v + fb_ref[...]
            v = jnp.minimum(v, 0.0) - jnp.log1p(jnp.exp(-jnp.abs(v)))
            v = jnp.where(i * blk + r >= PAD, v, 0.0)
        sh = 1
        while sh < blk:
            v = v + jnp.where(r >= sh, pltpu.roll(v, sh, 0), 0.0)
            sh *= 2
        v = v + carry[...]
        o_ref[...] = v
        carry[...] = o_ref[blk - 1:blk, :]

    in_specs = [pl.BlockSpec((blk, n), lambda i: (i, 0))]
    args = [x]
    if gate:
        in_specs.append(pl.BlockSpec((1, n), lambda i: (0, 0)))
        args.append(fbias)
    return pl.pallas_call(
        body, name=name, grid=(t // blk,),
        in_specs=in_specs,
        out_specs=pl.BlockSpec((blk, n), lambda i: (i, 0)),
        out_shape=jax.ShapeDtypeStruct((t, n), f32),
        scratch_shapes=[pltpu.VMEM((1, n), f32)],
        compiler_params=_params("arbitrary"),
    )(*args)


STRIP = 16


def _lanes(x, n):
    return jnp.concatenate([x] * (n // LANES), axis=1)


def _attn_fwd(q, k, v, cq_rep, ck_rows, name):
    t, d = q.shape
    n_heads = d // HEAD_DIM
    bk = ck_rows.shape[-1]
    bq = bk
    scale = 1.0 / math.sqrt(HEAD_DIM)

    def lane_fold(x, op):
        out = x[:, :LANES]
        for c0 in range(LANES, bk, LANES):
            out = op(out, x[:, c0:c0 + LANES])
        return out

    def body(q_ref, k_ref, v_ref, cq_ref, ck_ref, o_ref, o32_ref, lse_ref,
             s_scr, p_scr, m_scr, l_scr, red_scr, acc_scr):
        i = pl.program_id(1)
        m_scr[...] = jnp.full_like(m_scr, NEG_INF)
        l_scr[...] = jnp.zeros_like(l_scr)
        acc_scr[...] = jnp.zeros_like(acc_scr)
        qb = q_ref[...]
        ahead = (lax.broadcasted_iota(jnp.int32, (STRIP, bk), 1)
                 - lax.broadcasted_iota(jnp.int32, (STRIP, bk), 0))

        def tile(j, diagonal):
            k0 = pl.multiple_of(j * bk, bk)
            s_scr[...] = lax.dot_general(qb, k_ref[pl.ds(k0, bk), :], NT_DIMS,
                                         preferred_element_type=f32)
            ck = ck_ref[j]
            for r in range(0, bq, STRIP):
                rows = slice(r, r + STRIP)
                s = s_scr[rows, :] * scale + _lanes(cq_ref[rows, :], bk) - ck
                if diagonal:
                    s = jnp.where(ahead <= r, s, NEG_INF)
                s_scr[rows, :] = s
                red_scr[rows, :] = lane_fold(s, jnp.maximum)
            m_old = m_scr[...]
            m_new = jnp.maximum(m_old, jnp.broadcast_to(
                jnp.max(red_scr[...], axis=1, keepdims=True), (bq, LANES)))
            a = jnp.exp(m_old - m_new)
            m_scr[...] = m_new
            for r in range(0, bq, STRIP):
                rows = slice(r, r + STRIP)
                pr = jnp.exp(s_scr[rows, :] - _lanes(m_scr[rows, :], bk))
                red_scr[rows, :] = lane_fold(pr, jnp.add)
                p_scr[rows, :] = pr.astype(bf16)
            l_scr[...] = a * l_scr[...] + jnp.broadcast_to(
                jnp.sum(red_scr[...], axis=1, keepdims=True), (bq, LANES))
            acc_scr[...] = a * acc_scr[...] + jnp.dot(
                p_scr[...], v_ref[pl.ds(k0, bk), :], preferred_element_type=f32)

        def full_tile(j, carry):
            tile(j, False)
            return carry

        lax.fori_loop(0, i, full_tile, 0)
        tile(i, True)
        out = acc_scr[...] / l_scr[...]
        o_ref[...] = out.astype(bf16)
        o32_ref[...] = out
        lse_ref[...] = m_scr[...] + jnp.log(l_scr[...])

    qblk = pl.BlockSpec((bq, HEAD_DIM), lambda h, i: (i, h))
    head_rows = pl.BlockSpec((t, HEAD_DIM), lambda h, i: (0, h))
    rep = pl.BlockSpec((None, bq, LANES), lambda h, i: (h, i, 0))
    col = pltpu.VMEM((bq, LANES), f32)
    return pl.pallas_call(
        body, name=name, grid=(n_heads, t // bq),
        in_specs=[qblk, head_rows, head_rows, rep,
                  pl.BlockSpec((None, t // bk, 1, bk), lambda h, i: (h, 0, 0, 0))],
        out_specs=[qblk, qblk, rep],
        out_shape=[jax.ShapeDtypeStruct((t, d), bf16), jax.ShapeDtypeStruct((t, d), f32),
                   jax.ShapeDtypeStruct((n_heads, t, LANES), f32)],
        scratch_shapes=[pltpu.VMEM((bq, bk), f32), pltpu.VMEM((bq, bk), bf16), col, col, col,
                        pltpu.VMEM((bq, HEAD_DIM), f32)],
        compiler_params=_params("parallel", "parallel"),
    )(q, k, v, cq_rep, ck_rows)


def _loss_head(h, target, name):
    t, d = h.shape

    def body(h_ref, t_ref, dy_ref, loss_ref):
        i = pl.program_id(0)

        @pl.when(i == 0)
        def _():
            loss_ref[...] = jnp.zeros_like(loss_ref)

        diff = jnp.where(i >= 1, h_ref[...] - t_ref[...], 0.0)
        dy_ref[...] = diff * (1.0 / d)
        loss_ref[...] += jnp.sum(diff * diff)

    return pl.pallas_call(
        body, name=name, grid=(t // LANES,),
        in_specs=[pl.BlockSpec((LANES, d), lambda i: (i, 0)),
                  pl.BlockSpec((LANES, d), lambda i: (jnp.maximum(i - 1, 0), 0))],
        out_specs=[pl.BlockSpec((LANES, d), lambda i: (i, 0)),
                   pl.BlockSpec((1, LANES), lambda i: (0, 0))],
        out_shape=[jax.ShapeDtypeStruct((t, d), f32), jax.ShapeDtypeStruct((1, LANES), f32)],
        compiler_params=_params("arbitrary"),
    )(h, target)


def _ln_bwd(dh, r, gain, name):
    t, d = r.shape
    tm = _tile(t, 640, 16)

    def body(dh_ref, r_ref, g_ref, dr_ref, drb_ref, dg_ref, db_ref):
        @pl.when(pl.program_id(0) == 0)
        def _():
            dg_ref[...] = jnp.zeros_like(dg_ref)
            db_ref[...] = jnp.zeros_like(db_ref)

        rr = r_ref[...]
        mu = jnp.mean(rr, axis=-1, keepdims=True)
        xc = rr - mu
        var = jnp.mean(xc * xc, axis=-1, keepdims=True)
        rstd = lax.rsqrt(var + LN_EPS)
        xhat = xc * rstd
        dy = dh_ref[...]
        dxh = dy * g_ref[...]
        m1 = jnp.mean(dxh, axis=-1, keepdims=True)
        m2 = jnp.mean(dxh * xhat, axis=-1, keepdims=True)
        dr = rstd * (dxh - m1 - xhat * m2)
        dr_ref[...] = dr
        drb_ref[...] = dr.astype(bf16)
        dg_ref[...] += jnp.sum(dy * xhat, axis=0, keepdims=True)
        db_ref[...] += jnp.sum(dy, axis=0, keepdims=True)

    row = pl.BlockSpec((tm, d), lambda i: (i, 0))
    vec = pl.BlockSpec((1, d), lambda i: (0, 0))
    return pl.pallas_call(
        body, name=name, grid=(t // tm,),
        in_specs=[row, row, vec],
        out_specs=[row, row, vec, vec],
        out_shape=[jax.ShapeDtypeStruct((t, d), f32), jax.ShapeDtypeStruct((t, d), bf16),
                   jax.ShapeDtypeStruct((1, d), f32), jax.ShapeDtypeStruct((1, d), f32)],
        compiler_params=_params("arbitrary"),
    )(dh, r, gain)


def _ffn_bwd_act(drb, wd, layer, a, b, name):
    t, d = drb.shape
    s_n, _, n = a.shape
    tm = _tile(t, 640, 16)

    def body(dr_ref, w_ref, a_ref, b_ref, da_ref, db_ref):
        ds = 0.5 * lax.dot_general(dr_ref[...], w_ref[...], NT_DIMS, preferred_element_type=f32)
        av = a_ref[...]
        sig = jax.nn.sigmoid(av)
        da_ref[...] = (ds * b_ref[...] * (sig * (1.0 + av * (1.0 - sig)))).astype(bf16)
        db_ref[...] = (ds * (av * sig)).astype(bf16)

    act = pl.BlockSpec((None, tm, n), lambda s, i: (s, i, 0))
    return pl.pallas_call(
        body, name=name, grid=(s_n, t // tm),
        in_specs=[pl.BlockSpec((tm, d), lambda s, i: (i, 0)),
                  pl.BlockSpec((None, None, n, d), lambda s, i: (s, layer, 0, 0)), act, act],
        out_specs=[act, act],
        out_shape=[jax.ShapeDtypeStruct((s_n, t, n), bf16), jax.ShapeDtypeStruct((s_n, t, n), bf16)],
        compiler_params=_params("parallel", "parallel"),
    )(drb, wd, a, b)


def _act_spec(mode, tt, k, t_first):
    def fix(fn):
        return (lambda i, s: fn(s, i)) if t_first else fn
    if mode == "shared":
        return pl.BlockSpec((tt, k), fix(lambda s, i: (i, 0)))
    if mode == "cols":
        return pl.BlockSpec((tt, k), fix(lambda s, i: (i, s)))
    assert mode == "stack"
    return pl.BlockSpec((None, tt, k), fix(lambda s, i: (s, i, 0)))


def _act_width(arr, mode, s_n):
    return arr.shape[-1] // s_n if mode == "cols" else arr.shape[-1]


def _tn_matmul(x, xmode, ys, ymodes, s_n, scale, name, layers=None, layer=0, into=None):
    t = x.shape[-2]
    kx = _act_width(x, xmode, s_n)
    kys = [_act_width(y, m, s_n) for y, m in zip(ys, ymodes)]
    tt = _tile(t, 640, 16)
    n_t = t // tt
    n_y = len(ys)
    n_alias = 0 if into is None else n_y

    def body(*refs):
        x_ref = refs[0]
        y_refs = refs[1:1 + n_y]
        o_refs = refs[1 + n_y + n_alias:]
        i = pl.program_id(1)
        xv = x_ref[...].astype(bf16)
        for y_ref, o_ref in zip(y_refs, o_refs):
            part = lax.dot_general(xv, y_ref[...].astype(bf16), TN_DIMS, preferred_element_type=f32)

            @pl.when(i == 0)
            def _():
                o_ref[...] = part

            @pl.when(i > 0)
            def _():
                o_ref[...] += part

            if scale != 1.0:
                @pl.when(i == n_t - 1)
                def _():
                    o_ref[...] = o_ref[...] * scale

    in_specs = [_act_spec(xmode, tt, kx, False)]
    in_specs += [_act_spec(m, tt, ky, False) for m, ky in zip(ymodes, kys)]
    if layers is None:
        out_specs = [pl.BlockSpec((None, kx, ky), lambda s, i: (s, 0, 0)) for ky in kys]
        out_shape = [jax.ShapeDtypeStruct((s_n, kx, ky), f32) for ky in kys]
    else:
        out_specs = [pl.BlockSpec((None, None, kx, ky), lambda s, i: (s, layer, 0, 0)) for ky in kys]
        out_shape = [jax.ShapeDtypeStruct((s_n, layers, kx, ky), f32) for ky in kys]
    args = [x, *ys]
    aliases = {}
    if into is not None:
        in_specs += [pl.BlockSpec(memory_space=pl.ANY)] * n_y
        args += list(into)
        aliases = {1 + n_y + j: j for j in range(n_y)}
    return pl.pallas_call(
        body, name=name, grid=(s_n, n_t),
        in_specs=in_specs, out_specs=out_specs, out_shape=out_shape,
        input_output_aliases=aliases,
        compiler_params=_params("parallel", "arbitrary"),
    )(*args)


def _nt_sum(pairs, base, base_scale, s_n, out_dtype, name):
    t = pairs[0][0].shape[-2]
    d = pairs[0][2].shape[-2]
    tm = _tile(t, 640, 16)
    n_p = len(pairs)
    has_base = base is not None

    def body(*refs):
        dy_refs = refs[0:2 * n_p:2]
        w_refs = refs[1:2 * n_p:2]
        rest = refs[2 * n_p:]
        base_ref = rest[0] if has_base else None
        o_ref, acc = rest[-2], rest[-1]
        s = pl.program_id(1)

        @pl.when(s == 0)
        def _():
            acc[...] = jnp.zeros_like(acc)

        tot = None
        for dy_ref, w_ref in zip(dy_refs, w_refs):
            part = lax.dot_general(dy_ref[...].astype(bf16), w_ref[...], NT_DIMS,
                                   preferred_element_type=f32)
            tot = part if tot is None else tot + part
        acc[...] += tot

        @pl.when(s == s_n - 1)
        def _():
            res = acc[...]
            if has_base:
                res = base_scale * base_ref[...] + res
            o_ref[...] = res.astype(o_ref.dtype)

    in_specs, args = [], []
    for dy, mode, w, layer in pairs:
        k = _act_width(dy, mode, s_n)
        in_specs.append(_act_spec(mode, tm, k, True))
        if layer is None:
            in_specs.append(pl.BlockSpec((None, d, k), lambda i, s: (s, 0, 0)))
        else:
            in_specs.append(pl.BlockSpec((None, None, d, k),
                                         functools.partial(lambda i, s, l: (s, l, 0, 0), l=layer)))
        args += [dy, w]
    row = pl.BlockSpec((tm, d), lambda i, s: (i, 0))
    if has_base:
        in_specs.append(row)
        args.append(base)
    return pl.pallas_call(
        body, name=name, grid=(t // tm, s_n),
        in_specs=in_specs, out_specs=row,
        out_shape=jax.ShapeDtypeStruct((t, d), out_dtype),
        scratch_shapes=[pltpu.VMEM((tm, d), f32)],
        compiler_params=_params("parallel", "arbitrary"),
    )(*args)


def _conv_bwd(dz, p, conv_w, name):
    t, d3 = p.shape
    d = d3 // 3
    tm = _tile(t, 320, 8)
    hb = tm // 8
    last8 = t // 8 - 1
    n_ext = tm + 8

    def body(dz_ref, dzn_ref, p_ref, pp_ref, pn_ref, w_ref, dp_ref, dw_ref):
        i = pl.program_id(0)

        @pl.when(i == 0)
        def _():
            dw_ref[...] = jnp.zeros_like(dw_ref)

        w0, w1, w2 = w_ref[0:1, :], w_ref[1:2, :], w_ref[2:3, :]
        rows_u = i * tm - 8 + lax.broadcasted_iota(jnp.int32, (n_ext, 1), 0)
        cg = jnp.concatenate([pp_ref[:, d:2 * d], p_ref[:, d:2 * d]], axis=0)
        val = jnp.concatenate([pp_ref[:, 2 * d:], p_ref[:, 2 * d:]], axis=0)
        u = jnp.where(rows_u >= PAD, cg * val, 0.0)
        u1 = pltpu.roll(u, 1, 0)
        u2 = pltpu.roll(u, 2, 0)
        y = (w2 * u + w1 * u1 + w0 * u2)[8:]
        dzv = dz_ref[...]
        bg = p_ref[:, :d]
        rows_n = (i + 1) * tm + lax.broadcasted_iota(jnp.int32, (8, 1), 0)
        dy_main = dzv * bg
        dy_next = jnp.where(rows_n < t, dzn_ref[...] * pn_ref[:, :d], 0.0)
        dye = jnp.concatenate([dy_main, dy_next], axis=0)
        du = (w2 * dye + w1 * pltpu.roll(dye, n_ext - 1, 0)
              + w0 * pltpu.roll(dye, n_ext - 2, 0))[:tm]
        du = jnp.where(rows_u[8:] >= PAD, du, 0.0)
        dp_ref[:, :d] = (dzv * y).astype(bf16)
        dp_ref[:, d:2 * d] = (du * val[8:]).astype(bf16)
        dp_ref[:, 2 * d:] = (du * cg[8:]).astype(bf16)
        dw_ref[0:1, :] += jnp.sum(dy_main * u2[8:], axis=0, keepdims=True)
        dw_ref[1:2, :] += jnp.sum(dy_main * u1[8:], axis=0, keepdims=True)
        dw_ref[2:3, :] += jnp.sum(dy_main * u[8:], axis=0, keepdims=True)

    nxt = lambda i: (jnp.minimum((i + 1) * hb, last8), 0)
    return pl.pallas_call(
        body, name=name, grid=(t // tm,),
        in_specs=[pl.BlockSpec((tm, d), lambda i: (i, 0)),
                  pl.BlockSpec((8, d), nxt),
                  pl.BlockSpec((tm, d3), lambda i: (i, 0)),
                  pl.BlockSpec((8, d3), lambda i: (jnp.maximum(i * hb - 1, 0), 0)),
                  pl.BlockSpec((8, d3), nxt),
                  pl.BlockSpec((3, d), lambda i: (0, 0))],
        out_specs=[pl.BlockSpec((tm, d3), lambda i: (i, 0)),
                   pl.BlockSpec((3, d), lambda i: (0, 0))],
        out_shape=[jax.ShapeDtypeStruct((t, d3), bf16), jax.ShapeDtypeStruct((3, d), f32)],
        compiler_params=_params("arbitrary"),
    )(dz, dz, p, p, p, conv_w)


def _attn_stats(o, do, lse_rep, cq_rep, bq, name):
    t, d = o.shape
    n_heads = d // HEAD_DIM

    def body(o_ref, do_ref, lse_ref, cq_ref, cl_ref, delta_ref):
        for c0 in range(0, bq, LANES):
            rows = slice(c0, c0 + LANES)
            cl_ref[:, rows] = (cq_ref[rows, :] - lse_ref[rows, :]).T[0:1, :]
            prod = o_ref[rows, :] * do_ref[rows, :].astype(f32)
            delta_ref[:, rows] = jnp.sum(prod.T, axis=0, keepdims=True)

    qblk = pl.BlockSpec((bq, HEAD_DIM), lambda h, i: (i, h))
    rep = pl.BlockSpec((None, bq, LANES), lambda h, i: (h, i, 0))
    row = pl.BlockSpec((None, None, 1, bq), lambda h, i: (h, i, 0, 0))
    shp = jax.ShapeDtypeStruct((n_heads, t // bq, 1, bq), f32)
    return pl.pallas_call(
        body, name=name, grid=(n_heads, t // bq),
        in_specs=[qblk, qblk, rep, rep],
        out_specs=[row, row], out_shape=[shp, shp],
        compiler_params=_params("parallel", "parallel"),
    )(o, do, lse_rep, cq_rep)


def _attn_bwd(q, k, v, kt, do, ckey, cl_rows, delta_rows, name):
    t, d = q.shape
    n_heads = d // HEAD_DIM
    bk = kt.shape[-1]
    bq = bk
    n_kv = t // bk
    n_q = t // bq
    scale = 1.0 / math.sqrt(HEAD_DIM)

    def body(q_ref, do_ref, cl_ref, dl_ref, k_ref, v_ref, kt_ref, ck_ref,
             dq_ref, dcq_ref, dk_ref, dv_ref, dck_ref,
             st_scr, dp_scr, p_scr, ds_scr, dqt, dk_acc, dv_acc, dck_acc):
        j = pl.program_id(1)

        @pl.when(j == 0)
        def _():
            dqt[...] = jnp.zeros_like(dqt)
            dcq_ref[...] = jnp.zeros_like(dcq_ref)

        dk_acc[...] = jnp.zeros_like(dk_acc)
        dv_acc[...] = jnp.zeros_like(dv_acc)
        dck_acc[...] = jnp.zeros_like(dck_acc)
        kb = k_ref[...]
        vb = v_ref[...]
        behind = (lax.broadcasted_iota(jnp.int32, (STRIP, bq), 1)
                  - lax.broadcasted_iota(jnp.int32, (STRIP, bq), 0))

        def tile(i, diagonal):
            r0 = pl.multiple_of(i * bq, bq)
            qi = q_ref[pl.ds(r0, bq), :]
            doi = do_ref[pl.ds(r0, bq), :]
            st_scr[...] = lax.dot_general(kb, qi, NT_DIMS, preferred_element_type=f32)
            dp_scr[...] = lax.dot_general(vb, doi, NT_DIMS, preferred_element_type=f32)
            cl = cl_ref[i]
            dl = dl_ref[i]
            over_keys = jnp.zeros((STRIP, bq), f32)
            for r in range(0, bk, STRIP):
                keys = slice(r, r + STRIP)
                st = st_scr[keys, :] * scale + cl - _lanes(ck_ref[keys, :], bq)
                if diagonal:
                    st = jnp.where(behind >= r, st, NEG_INF)
                pr = jnp.exp(st)
                ds = pr * (dp_scr[keys, :] - dl)
                over_keys = over_keys + ds
                dck_acc[keys, :] -= jnp.sum(ds, axis=1, keepdims=True)
                p_scr[keys, :] = pr.astype(bf16)
                ds_scr[keys, :] = ds.astype(bf16)
            dcq_ref[i] += jnp.sum(over_keys, axis=0, keepdims=True)
            dv_acc[...] += jnp.dot(p_scr[...], doi, preferred_element_type=f32)
            dk_acc[...] += jnp.dot(ds_scr[...], qi, preferred_element_type=f32)
            dqt[i] += jnp.dot(kt_ref[...], ds_scr[...], preferred_element_type=f32)

        def full_tile(i, carry):
            tile(i, False)
            return carry

        tile(j, True)
        lax.fori_loop(j + 1, n_q, full_tile, 0)
        dk_ref[...] = (dk_acc[...] * scale).astype(bf16)
        dv_ref[...] = dv_acc[...].astype(bf16)
        for c0 in range(0, bk, LANES):
            keys = slice(c0, c0 + LANES)
            dck_ref[:, keys] = jnp.broadcast_to(dck_acc[keys, :], (LANES, LANES)).T[0:1, :]

        @pl.when(j == n_kv - 1)
        def _():
            def emit(i, carry):
                r0 = pl.multiple_of(i * bq, bq)
                dq_ref[pl.ds(r0, bq), :] = dqt[i].T * scale
                return carry
            lax.fori_loop(0, n_q, emit, 0)

    head_rows = pl.BlockSpec((t, HEAD_DIM), lambda h, j: (0, h))
    head_stat = pl.BlockSpec((None, n_q, 1, bq), lambda h, j: (h, 0, 0, 0))
    kblk = pl.BlockSpec((bk, HEAD_DIM), lambda h, j: (j, h))
    return pl.pallas_call(
        body, name=name, grid=(n_heads, n_kv),
        in_specs=[head_rows, head_rows, head_stat, head_stat, kblk, kblk,
                  pl.BlockSpec((None, None, HEAD_DIM, bk), lambda h, j: (h, j, 0, 0)),
                  pl.BlockSpec((None, bk, LANES), lambda h, j: (h, j, 0))],
        out_specs=[head_rows, head_stat, kblk, kblk,
                   pl.BlockSpec((None, None, 1, bk), lambda h, j: (h, j, 0, 0))],
        out_shape=[jax.ShapeDtypeStruct((t, d), f32),
                   jax.ShapeDtypeStruct((n_heads, n_q, 1, bq), f32),
                   jax.ShapeDtypeStruct((t, d), bf16), jax.ShapeDtypeStruct((t, d), bf16),
                   jax.ShapeDtypeStruct((n_heads, n_kv, 1, bk), f32)],
        scratch_shapes=[pltpu.VMEM((bk, bq), f32), pltpu.VMEM((bk, bq), f32),
                        pltpu.VMEM((bk, bq), bf16), pltpu.VMEM((bk, bq), bf16),
                        pltpu.VMEM((n_q, HEAD_DIM, bq), f32),
                        pltpu.VMEM((bk, HEAD_DIM), f32), pltpu.VMEM((bk, HEAD_DIM), f32),
                        pltpu.VMEM((bk, 1), f32)],
        compiler_params=_params("parallel", "arbitrary"),
    )(q, do, cl_rows, delta_rows, k, v, kt, ckey)


def _fgate_bwd(dlogf, flog, fbias, name):
    t, n = flog.shape

    def body(dl_ref, fl_ref, fb_ref, o_ref, sum_ref):
        i = pl.program_id(0)

        @pl.when(i == 0)
        def _():
            sum_ref[...] = jnp.zeros_like(sum_ref)

        r = i * LANES + lax.broadcasted_iota(jnp.int32, (LANES, n), 0)
        g = dl_ref[...] * jax.nn.sigmoid(-(fl_ref[...] + fb_ref[...]))
        g = jnp.where(r >= PAD, g, 0.0)
        o_ref[...] = g
        sum_ref[...] += jnp.sum(g, axis=0, keepdims=True)

    blk = pl.BlockSpec((LANES, n), lambda i: (i, 0))
    vec = pl.BlockSpec((1, n), lambda i: (0, 0))
    return pl.pallas_call(
        body, name=name, grid=(t // LANES,),
        in_specs=[blk, blk, vec], out_specs=[blk, vec],
        out_shape=[jax.ShapeDtypeStruct((t, n), f32), jax.ShapeDtypeStruct((1, n), f32)],
        compiler_params=_params("arbitrary"),
    )(dlogf, flog, fbias)


def _place():
    cx, cy, c = lax.axis_index("x"), lax.axis_index("y"), lax.axis_index("c")
    chips = [(1 - cx, cy), (cx, 1 - cy), (1 - cx, 1 - cy)]
    return cx, cy, c, chips


def _hbm_specs(n):
    return [pl.BlockSpec(memory_space=pl.ANY)] * n


def _place_shard(w, idx, name):
    _, r, c_n = w.shape
    out_dtype = bf16 if r * c_n > 2 ** 16 else w.dtype
    tr = _tile(r, 512, 16) if r % 16 == 0 else r

    def body(idx_ref, w_ref, o_ref):
        o_ref[...] = w_ref[...].astype(out_dtype)

    grid_spec = pltpu.PrefetchScalarGridSpec(
        num_scalar_prefetch=1, grid=(2, r // tr),
        in_specs=[pl.BlockSpec((None, tr, c_n), lambda h, i, idx: (h, i, 0))],
        out_specs=pl.BlockSpec((None, None, tr, c_n), lambda h, i, idx: (idx[0], h, i, 0)))
    return pl.pallas_call(
        body, name=name, grid_spec=grid_spec,
        out_shape=jax.ShapeDtypeStruct((N_CHIPS, 2, r, c_n), out_dtype),
        compiler_params=_params("parallel", "parallel"),
    )(idx, w)


def _all_gather(bufs, name):
    n_t = len(bufs)

    def body(*refs):
        outs = refs[n_t:2 * n_t]
        send, recv = refs[2 * n_t:]
        cx, cy, c, chips = _place()
        me = 2 * cx + cy
        sib = (cx, cy, 1 - c)

        def copy(t, sem, slot, to):
            return pltpu.make_async_remote_copy(src_ref=slot, dst_ref=slot, send_sem=send.at[6 * t + sem],
                                                recv_sem=recv.at[6 * t + sem], device_id=to,
                                                device_id_type=MESH)

        sent = []
        for t in range(n_t):
            for k, chip in enumerate(chips):
                cp = copy(t, k, outs[t].at[me, c], (*chip, c))
                cp.start()
                sent.append(cp)
        for t in range(n_t):
            for k, (px, py) in enumerate(chips):
                slot = outs[t].at[2 * px + py, c]
                copy(t, k, slot, sib).wait_recv()
                cp = copy(t, 3 + k, slot, sib)
                cp.start()
                sent.append(cp)
        for t in range(n_t):
            for k, (px, py) in enumerate(chips):
                copy(t, 3 + k, outs[t].at[2 * px + py, 1 - c], sib).wait_recv()
        for cp in sent:
            cp.wait_send()

    return pl.pallas_call(
        body, name=name,
        in_specs=_hbm_specs(n_t), out_specs=_hbm_specs(n_t),
        out_shape=[jax.ShapeDtypeStruct(x.shape, x.dtype) for x in bufs],
        input_output_aliases={t: t for t in range(n_t)},
        scratch_shapes=[pltpu.SemaphoreType.DMA((6 * n_t,)), pltpu.SemaphoreType.DMA((6 * n_t,))],
    )(*bufs)


def _pair_exchange(grads, name):
    n_t = len(grads)

    def body(*refs):
        gs, outs = refs[:n_t], refs[n_t:2 * n_t]
        send, recv = refs[2 * n_t:]
        cx, cy, c, chips = _place()
        sib = (cx, cy, 1 - c)
        slots = [2 * cx + cy] + [2 * px + py for px, py in chips]
        copies = []
        for t in range(n_t):
            for k, slot in enumerate(slots):
                cp = pltpu.make_async_remote_copy(
                    src_ref=gs[t].at[slot, 1 - c], dst_ref=outs[t].at[k],
                    send_sem=send.at[4 * t + k], recv_sem=recv.at[4 * t + k],
                    device_id=sib, device_id_type=MESH)
                cp.start()
                copies.append(cp)
        for cp in copies:
            cp.wait()

    return pl.pallas_call(
        body, name=name,
        in_specs=_hbm_specs(n_t), out_specs=_hbm_specs(n_t),
        out_shape=[jax.ShapeDtypeStruct((N_CHIPS, *g.shape[2:]), f32) for g in grads],
        scratch_shapes=[pltpu.SemaphoreType.DMA((4 * n_t,)), pltpu.SemaphoreType.DMA((4 * n_t,))],
    )(*grads)


def _chip_exchange(sends, name):
    n_t = len(sends)

    def body(*refs):
        xs, outs = refs[:n_t], refs[n_t:2 * n_t]
        send, recv = refs[2 * n_t:]
        cx, cy, c, chips = _place()
        copies = []
        for t in range(n_t):
            for k, chip in enumerate(chips):
                cp = pltpu.make_async_remote_copy(
                    src_ref=xs[t].at[k], dst_ref=outs[t].at[k],
                    send_sem=send.at[3 * t + k], recv_sem=recv.at[3 * t + k],
                    device_id=(*chip, c), device_id_type=MESH)
                cp.start()
                copies.append(cp)
        for cp in copies:
            cp.wait()

    return pl.pallas_call(
        body, name=name,
        in_specs=_hbm_specs(n_t), out_specs=_hbm_specs(n_t),
        out_shape=[jax.ShapeDtypeStruct(x.shape, x.dtype) for x in sends],
        scratch_shapes=[pltpu.SemaphoreType.DMA((3 * n_t,)), pltpu.SemaphoreType.DMA((3 * n_t,))],
    )(*sends)


def _pair_share(bufs, name):
    n_t = len(bufs)

    def body(*refs):
        outs = refs[n_t:2 * n_t]
        send, recv = refs[2 * n_t:]
        cx, cy, c, _ = _place()
        sib = (cx, cy, 1 - c)
        copies = []
        for t in range(n_t):
            cp = pltpu.make_async_remote_copy(
                src_ref=outs[t].at[c], dst_ref=outs[t].at[c], send_sem=send.at[t], recv_sem=recv.at[t],
                device_id=sib, device_id_type=MESH)
            cp.start()
            copies.append(cp)
        for t in range(n_t):
            copies[t].wait_send()
            pltpu.make_async_remote_copy(
                src_ref=outs[t].at[1 - c], dst_ref=outs[t].at[1 - c], send_sem=send.at[t],
                recv_sem=recv.at[t], device_id=sib, device_id_type=MESH).wait_recv()

    return pl.pallas_call(
        body, name=name,
        in_specs=_hbm_specs(n_t), out_specs=_hbm_specs(n_t),
        out_shape=[jax.ShapeDtypeStruct(x.shape, x.dtype) for x in bufs],
        input_output_aliases={t: t for t in range(n_t)},
        scratch_shapes=[pltpu.SemaphoreType.DMA((n_t,)), pltpu.SemaphoreType.DMA((n_t,))],
    )(*bufs)


def _rs_prepare(g, pair, idx, name):
    _, _, r, c_n = g.shape
    tr = _tile(r, 256, 8)

    def body(idx_ref, g_ref, p_ref, o_ref):
        o_ref[...] = (g_ref[...] + p_ref[...]).astype(bf16)

    grid_spec = pltpu.PrefetchScalarGridSpec(
        num_scalar_prefetch=1, grid=(3, r // tr),
        in_specs=[pl.BlockSpec((None, None, tr, c_n), lambda k, i, idx: (idx[k + 1], idx[4], i, 0)),
                  pl.BlockSpec((None, tr, c_n), lambda k, i, idx: (k + 1, i, 0))],
        out_specs=pl.BlockSpec((None, tr, c_n), lambda k, i, idx: (k, i, 0)))
    return pl.pallas_call(
        body, name=name, grid_spec=grid_spec,
        out_shape=jax.ShapeDtypeStruct((3, r, c_n), bf16),
        compiler_params=_params("parallel", "parallel"),
    )(idx, g, pair)


def _rs_finish(g, pair, recv, idx, name):
    _, _, r, c_n = g.shape
    tr = _tile(r, 256, 8)

    def body(idx_ref, g_ref, p_ref, r0_ref, r1_ref, r2_ref, o_ref):
        acc = g_ref[...] + p_ref[...]
        acc = acc + r0_ref[...].astype(f32)
        acc = acc + r1_ref[...].astype(f32)
        acc = acc + r2_ref[...].astype(f32)
        o_ref[...] = acc

    def rspec(k):
        return pl.BlockSpec((None, tr, c_n), functools.partial(lambda i, idx, kk: (kk, i, 0), kk=k))

    grid_spec = pltpu.PrefetchScalarGridSpec(
        num_scalar_prefetch=1, grid=(r // tr,),
        in_specs=[pl.BlockSpec((None, None, tr, c_n), lambda i, idx: (idx[0], idx[4], i, 0)),
                  rspec(0), rspec(0), rspec(1), rspec(2)],
        out_specs=pl.BlockSpec((None, tr, c_n), lambda i, idx: (idx[4], i, 0)))
    return pl.pallas_call(
        body, name=name, grid_spec=grid_spec,
        out_shape=jax.ShapeDtypeStruct((2, r, c_n), f32),
        compiler_params=_params("parallel"),
    )(idx, g, pair, recv, recv, recv)


def _adamw(w, g, m, v, name):
    r, c_n = w.shape
    tr = _tile(r, 256, 8) if r % 8 == 0 else r

    def body(w_ref, g_ref, m_ref, v_ref, d_ref, mo_ref, vo_ref):
        gv = g_ref[...]
        mn = ADAM_B1 * m_ref[...] + (1.0 - ADAM_B1) * gv
        vn = ADAM_B2 * v_ref[...] + (1.0 - ADAM_B2) * (gv * gv)
        m_hat = mn / (1.0 - ADAM_B1 ** ADAM_STEP)
        v_hat = vn / (1.0 - ADAM_B2 ** ADAM_STEP)
        d_ref[...] = -ADAM_LR * (m_hat / (jnp.sqrt(v_hat) + ADAM_EPS) + ADAM_WD * w_ref[...])
        mo_ref[...] = mn
        vo_ref[...] = vn

    blk = pl.BlockSpec((tr, c_n), lambda i: (i, 0))
    shp = jax.ShapeDtypeStruct((r, c_n), f32)
    return pl.pallas_call(
        body, name=name, grid=(r // tr,),
        in_specs=[blk] * 4, out_specs=[blk] * 3, out_shape=[shp] * 3,
        compiler_params=_params("parallel"),
    )(w, g, m, v)


def _halves(a):
    flat = a.reshape(-1, a.shape[-1])
    return flat.reshape(2, flat.shape[0] // 2, flat.shape[-1])


def kernel(x, meta, ffn1_wg, ffn1_wu, ffn1_wd, ffn2_wg, ffn2_wu, ffn2_wd, ln_gain, ln_bias, conv_w_in, conv_w, conv_w_out, kv_w, f_bias, attn_w_q, attn_w_o, loss_target, m_meta, m_ffn1_wg, m_ffn1_wu, m_ffn1_wd, m_ffn2_wg, m_ffn2_wu, m_ffn2_wd, m_ln_gain, m_ln_bias, m_conv_w_in, m_conv_w, m_conv_w_out, m_kv_w, m_f_bias, m_attn_w_q, m_attn_w_o, v_meta, v_ffn1_wg, v_ffn1_wu, v_ffn1_wd, v_ffn2_wg, v_ffn2_wu, v_ffn2_wd, v_ln_gain, v_ln_bias, v_conv_w_in, v_conv_w, v_conv_w_out, v_kv_w, v_f_bias, v_attn_w_q, v_attn_w_o):
    seq, d = x.shape[1], x.shape[2]
    t = PAD + N_META + seq
    n_heads = d // HEAD_DIM
    dq = d // N_CHIPS
    n_kv = kv_w.shape[1]
    x2 = x[0]
    target = loss_target[0]

    def rows8(a):
        return jnp.pad(a, ((0, 8 - a.shape[0]), (0, 0)))

    def small_pack(mt, g, b, cw, fb):
        fb_row = jnp.pad(fb, (0, mt.shape[1] - n_heads))[None]
        return jnp.concatenate([mt, rows8(g.reshape(6, -1)), rows8(b.reshape(6, -1)),
                                rows8(cw.reshape(3, -1)), rows8(fb_row)], axis=0)

    w_small = small_pack(meta, ln_gain, ln_bias, conv_w, f_bias)

    cx, cy, c = lax.axis_index("x"), lax.axis_index("y"), lax.axis_index("c")
    idx = jnp.stack([2 * cx + cy, 2 * (1 - cx) + cy, 2 * cx + (1 - cy), 2 * (1 - cx) + (1 - cy), c]
                    ).astype(jnp.int32)

    shards = [ffn1_wg, ffn1_wu, ffn1_wd, ffn2_wg, ffn2_wu, ffn2_wd,
              _halves(conv_w_in), _halves(conv_w_out), _halves(kv_w), _halves(attn_w_q),
              _halves(attn_w_o), _halves(w_small)]
    gw = _all_gather([_place_shard(w, idx, "place_shard") for w in shards], "all_gather_weights")
    wg1, wu1, wd1, wg2, wu2, wd2 = gw[:6]
    n_in = conv_w_in.shape[-1]
    w_in = gw[6].reshape(N_CHIPS, d, n_in)
    w_out = gw[7].reshape(1, 1, d, d)
    w_q = gw[9].reshape(1, d, d)
    w_o = gw[10].reshape(1, 1, d, d)
    kv_full = gw[8].reshape(N_CHIPS, d, n_kv).transpose(1, 0, 2).reshape(d, N_CHIPS * n_kv)
    w_k = kv_full[:, :d][None]
    w_v = kv_full[:, d:2 * d][None]
    w_f = jnp.pad(kv_full[:, 2 * d:], ((0, 0), (0, LANES - n_heads)))[None]
    small = gw[11].reshape(N_CHIPS, SMALL_ROWS, dq).transpose(1, 0, 2).reshape(SMALL_ROWS, d)
    meta_full = small[:N_META]
    gains = small[16:22].reshape(DEPTH, 3, 1, d)
    biases = small[24:30].reshape(DEPTH, 3, 1, d)
    conv_w_full = small[32:35]
    fb_pad = jnp.pad(f_bias, (0, LANES - n_heads))[None]

    meta_pad = jnp.concatenate([jnp.zeros((PAD, d), f32), meta_full], axis=0)
    h0, h0b = _embed(meta_pad, x2, "embed")
    a1, b1, s1 = _ffn_up(h0b, wg1, wu1, 0, "ffn_up")
    r1, h1, h1b = _down_ln(s1, wd1, 0, h0, gains[0, 0], biases[0, 0], 0.5, "ffn_down_ln")
    p = _nn_matmul(h1b, w_in, f32, "conv_in")
    z = _conv_fwd(p, conv_w_full, "conv_fwd")
    r2, h2, h2b = _down_ln(z[None], w_out, 0, h1, gains[0, 1], biases[0, 1], 1.0, "mix_out_ln")
    a2, b2, s2 = _ffn_up(h2b, wg2, wu2, 0, "ffn_up")
    r3, h3, h3b = _down_ln(s2, wd2, 0, h2, gains[0, 2], biases[0, 2], 0.5, "ffn_down_ln")
    kk = _nn_matmul(h3b, w_k, bf16, "proj_bf16")
    vv = _nn_matmul(h3b, w_v, bf16, "proj_bf16")
    flog = _nn_matmul(h3b, w_f, f32, "proj_gate")
    cum = _row_scan(flog, "gate_cumsum", fb_pad)
    bk = _tile(t, 640, LANES)
    c_ht = cum[:, :n_heads].T
    c_keys = jnp.where(jnp.arange(t)[None, :] < PAD, 1e30, c_ht)
    cq_rep = jnp.broadcast_to(c_ht[:, :, None], (n_heads, t, LANES))
    ck_rep = jnp.broadcast_to(c_keys[:, :, None], (n_heads, t, LANES))
    ck_rows = c_keys.reshape(n_heads, t // bk, 1, bk)
    a3, b3, s3 = _ffn_up(h3b, wg1, wu1, 1, "ffn_up")
    r4, h4, h4b = _down_ln(s3, wd1, 1, h3, gains[1, 0], biases[1, 0], 0.5, "ffn_down_ln")
    q = _nn_matmul(h4b, w_q, bf16, "proj_bf16")
    o, o32, lse_rep = _attn_fwd(q, kk, vv, cq_rep, ck_rows, "attn_fwd")
    r5, h5, h5b = _down_ln(o[None], w_o, 0, h4, gains[1, 1], biases[1, 1], 1.0, "mix_out_ln")
    a4, b4, s4 = _ffn_up(h5b, wg2, wu2, 1, "ffn_up")
    r6, h6, _ = _down_ln(s4, wd2, 1, h5, gains[1, 2], biases[1, 2], 0.5, "ffn_down_ln")
    dy, sq = _loss_head(h6, target, "loss_head")
    loss_part = 0.5 * sq[0, 0] / d

    def ffn_bwd(dr, drb, hb_in, a, b, s, wg, wu, wd, layer, into):
        da, db = _ffn_bwd_act(drb, wd, layer, a, b, "ffn_bwd_act")
        into_d = None if into is None else [into[2]]
        into_gu = None if into is None else [into[0], into[1]]
        (dwd,) = _tn_matmul(s, "stack", [drb], ["shared"], N_CHIPS, 0.5, "ffn_dwd",
                            layers=DEPTH, layer=layer, into=into_d)
        dwg, dwu = _tn_matmul(hb_in, "shared", [da, db], ["stack", "stack"], N_CHIPS, 1.0, "ffn_dwgu",
                              layers=DEPTH, layer=layer, into=into_gu)
        dh = _nt_sum([(da, "stack", wg, layer), (db, "stack", wu, layer)], dr, ALPHA, N_CHIPS, f32,
                     "ffn_dx")
        return dh, (dwg, dwu, dwd)

    dr6, dr6b, dg12, db12 = _ln_bwd(dy, r6, gains[1, 2], "ln_bwd")
    dh5, gf2 = ffn_bwd(dr6, dr6b, h5b, a4, b4, s4, wg2, wu2, wd2, 1, None)
    dr5, dr5b, dg11, db11 = _ln_bwd(dh5, r5, gains[1, 1], "ln_bwd")
    (dwo,) = _tn_matmul(o, "shared", [dr5b], ["shared"], 1, 1.0, "sq_dw")
    do = _nt_sum([(dr5b, "cols", w_o[0], None)], None, 1.0, 1, bf16, "sq_dx_bf16")
    cl, delta = _attn_stats(o32, do, lse_rep, cq_rep, bk, "attn_stats")
    kt = kk.reshape(t // bk, bk, n_heads, HEAD_DIM).transpose(2, 0, 3, 1)
    dq_att, dc_q, dk, dv, dc_k = _attn_bwd(q, kk, vv, kt, do, ck_rep, cl, delta, "attn_bwd")
    dc = (dc_q + dc_k).reshape(n_heads, t)
    (dwq,) = _tn_matmul(h4b, "shared", [dq_att], ["shared"], 1, 1.0, "sq_dw")
    dh4 = _nt_sum([(dq_att, "cols", w_q, None)], dr5, ALPHA, 1, f32, "sq_dx_res")
    dr4, dr4b, dg10, db10 = _ln_bwd(dh4, r4, gains[1, 0], "ln_bwd")
    dh3a, gf1 = ffn_bwd(dr4, dr4b, h3b, a3, b3, s3, wg1, wu1, wd1, 1, None)
    dc_t = jnp.pad(dc.T, ((0, 0), (0, LANES - n_heads)))
    dlogf = _row_scan(dc_t[::-1], "rev_cumsum")[::-1]
    dfl, dfb_cols = _fgate_bwd(dlogf, flog, fb_pad, "gate_bwd")
    dwk, dwv, dwf = _tn_matmul(h3b, "shared", [dk, dv, dfl], ["shared"] * 3, 1, 1.0, "kv_dw")
    dh3 = _nt_sum([(dk, "cols", w_k, None), (dv, "cols", w_v, None), (dfl, "cols", w_f, None)],
                  dh3a, 1.0, 1, f32, "kv_dx")
    dr3, dr3b, dg02, db02 = _ln_bwd(dh3, r3, gains[0, 2], "ln_bwd")
    dh2, gf2 = ffn_bwd(dr3, dr3b, h2b, a2, b2, s2, wg2, wu2, wd2, 0, gf2)
    dr2, dr2b, dg01, db01 = _ln_bwd(dh2, r2, gains[0, 1], "ln_bwd")
    (dwout,) = _tn_matmul(z, "shared", [dr2b], ["shared"], 1, 1.0, "sq_dw")
    dz = _nt_sum([(dr2b, "cols", w_out[0], None)], None, 1.0, 1, f32, "sq_dx_f32")
    dp, dconv_w = _conv_bwd(dz, p, conv_w_full, "conv_bwd")
    (dwin,) = _tn_matmul(h1b, "shared", [dp], ["cols"], N_CHIPS, 1.0, "conv_dwin")
    dh1 = _nt_sum([(dp, "cols", w_in, None)], dr2, ALPHA, N_CHIPS, f32, "conv_dx")
    dr1, dr1b, dg00, db00 = _ln_bwd(dh1, r1, gains[0, 0], "ln_bwd")
    dh0, gf1 = ffn_bwd(dr1, dr1b, h0b, a1, b1, s1, wg1, wu1, wd1, 0, gf1)
    grad_x = dh0[PAD + N_META:][None]
    dmeta = dh0[PAD:PAD + N_META]

    def by_chip(full, rows):
        n = full.shape[1] // N_CHIPS
        return full.reshape(rows, N_CHIPS, n).transpose(1, 0, 2).reshape(N_CHIPS, 2, rows // 2, n)

    dkv_full = jnp.concatenate([dwk[0], dwv[0], dwf[0][:, :n_heads]], axis=1)
    d_small = jnp.concatenate(
        [dmeta, rows8(jnp.concatenate([dg00, dg01, dg02, dg10, dg11, dg12], axis=0)),
         rows8(jnp.concatenate([db00, db01, db02, db10, db11, db12], axis=0)),
         rows8(dconv_w), jnp.zeros((8, d), f32)], axis=0)
    grads = [gf1[0], gf1[1], gf1[2], gf2[0], gf2[1], gf2[2],
             dwin.reshape(N_CHIPS, 2, d // 2, n_in),
             dwout.reshape(N_CHIPS, 2, dq // 2, d),
             by_chip(dkv_full, d),
             dwq.reshape(N_CHIPS, 2, dq // 2, d),
             dwo.reshape(N_CHIPS, 2, dq // 2, d),
             by_chip(d_small, SMALL_ROWS)]
    pair = _pair_exchange(grads, "grad_pair_exchange")
    sends = [_rs_prepare(g, pr, idx, "grad_prepare") for g, pr in zip(grads, pair)]
    recvs = _chip_exchange(sends, "grad_chip_exchange")
    halves = [_rs_finish(g, pr, rc, idx, "grad_finish") for g, pr, rc in zip(grads, pair, recvs)]
    red = _pair_share(halves, "grad_pair_share")

    tail = jnp.zeros((LANES,), f32).at[:n_heads].set(dfb_cols[0, :n_heads]).at[n_heads].set(loss_part)
    tail = lax.psum(tail, ("x", "y", "c"))
    loss = tail[n_heads]
    g_fb = tail[:n_heads]

    params = [ffn1_wg, ffn1_wu, ffn1_wd, ffn2_wg, ffn2_wu, ffn2_wd, conv_w_in, conv_w_out, kv_w,
              attn_w_q, attn_w_o]
    ms = [m_ffn1_wg, m_ffn1_wu, m_ffn1_wd, m_ffn2_wg, m_ffn2_wu, m_ffn2_wd, m_conv_w_in, m_conv_w_out,
          m_kv_w, m_attn_w_q, m_attn_w_o]
    vs = [v_ffn1_wg, v_ffn1_wu, v_ffn1_wd, v_ffn2_wg, v_ffn2_wu, v_ffn2_wd, v_conv_w_in, v_conv_w_out,
          v_kv_w, v_attn_w_q, v_attn_w_o]
    out_g, out_d, out_m, out_v = [], [], [], []
    for w, g, m, v in zip(params, red[:11], ms, vs):
        w2 = w.reshape(-1, w.shape[-1])
        g2 = g.reshape(w2.shape)
        dl, mn, vn = _adamw(w2, g2, m.reshape(w2.shape), v.reshape(w2.shape), "adamw")
        out_g.append(g2.reshape(w.shape))
        out_d.append(dl.reshape(w.shape))
        out_m.append(mn.reshape(w.shape))
        out_v.append(vn.reshape(w.shape))
    g_small = red[11].reshape(SMALL_ROWS, dq)
    g_small = jnp.concatenate([g_small[:40], rows8(jnp.pad(g_fb, (0, dq - n_heads))[None])], axis=0)
    m_small = small_pack(m_meta, m_ln_gain, m_ln_bias, m_conv_w, m_f_bias)
    v_small = small_pack(v_meta, v_ln_gain, v_ln_bias, v_conv_w, v_f_bias)
    d_s, m_s, v_s = _adamw(w_small, g_small, m_small, v_small, "adamw_small")

    def unpack(pk):
        return (pk[:16], pk[16:22].reshape(DEPTH, 3, dq), pk[24:30].reshape(DEPTH, 3, dq),
                pk[32:35].reshape(1, 3, dq), pk[40, :n_heads])

    def order(big, pk):
        mt, g, b, cw, fb = unpack(pk)
        (f1g, f1u, f1d, f2g, f2u, f2d, cin, cout, kvw, aq, ao) = big
        return [mt, f1g, f1u, f1d, f2g, f2u, f2d, g, b, cin, cw, cout, kvw, fb, aq, ao]

    return (loss, grad_x, *order(out_g, g_small), *order(out_d, d_s), *order(out_m, m_s),
            *order(out_v, v_s))
```

```python
import functools
import math

import jax
import jax.numpy as jnp
from jax import lax
from jax.experimental import pallas as pl
from jax.experimental.pallas import tpu as pltpu

f32 = jnp.float32
bf16 = jnp.bfloat16

N_META = 16
PAD = 112
HEAD_DIM = 128
DEPTH = 2
LN_EPS = 1e-5
ALPHA = (2 * DEPTH) ** 0.25
NEG_INF = -1e30
N_CHIPS = 4
SMALL_ROWS = 48
LANES = 128

ADAM_LR = 0.001
ADAM_B1 = 0.9
ADAM_B2 = 0.999
ADAM_EPS = 1e-08
ADAM_WD = 0.01
ADAM_STEP = 10

VMEM_LIMIT_BYTES = 56 * 1024 * 1024
MESH = pl.DeviceIdType.MESH

NT_DIMS = (((1,), (1,)), ((), ()))
TN_DIMS = (((0,), (0,)), ((), ()))


def _tile(n, target, mult):
    best = None
    for d in range(mult, min(n, target) + 1, mult):
        if n % d == 0:
            best = d
    assert best is not None, (n, target, mult)
    return best


def _params(*sem):
    return pltpu.CompilerParams(dimension_semantics=sem, vmem_limit_bytes=VMEM_LIMIT_BYTES)


class _Place:
    def __init__(self):
        self.cx, self.cy, self.c = lax.axis_index("x"), lax.axis_index("y"), lax.axis_index("c")
        self.chips = [(1 - self.cx, self.cy), (self.cx, 1 - self.cy), (1 - self.cx, 1 - self.cy)]
        self.me = 2 * self.cx + self.cy
        self.slots = [self.me] + [2 * px + py for px, py in self.chips]
        self.sib = (self.cx, self.cy, 1 - self.c)


class _Copies:
    def __init__(self):
        self.srcs, self.bufs, self.news = {}, {}, {}
        self.plans = []
        self.out_bufs, self.out_news = {}, {}

    def add(self, plan, n_copies, srcs=None, bufs=None, news=None):
        for have, more in ((self.srcs, srcs), (self.bufs, bufs), (self.news, news)):
            for key, val in (more or {}).items():
                assert key not in have or have[key] is val, key
                have[key] = val
        self.plans.append((plan, n_copies))

    def empty(self):
        return not self.plans

    def count(self):
        return sum(n for _, n in self.plans)

    def copies(self, src_refs, buf_refs, new_refs, send, recv):
        place = _Place()
        srcs = dict(zip(self.srcs, src_refs))
        bufs = dict(zip(self.bufs, buf_refs))
        news = dict(zip(self.news, new_refs))
        out = []
        for plan, n_copies in self.plans:
            triples = plan(srcs, bufs, news, place)
            assert len(triples) == n_copies
            for src, dst, dev in triples:
                n = len(out)
                out.append(pltpu.make_async_remote_copy(
                    src_ref=src, dst_ref=dst, send_sem=send.at[n], recv_sem=recv.at[n],
                    device_id=dev, device_id_type=MESH))
        return out

    def land(self, results):
        n_b = len(self.bufs)
        self.out_bufs = dict(zip(self.bufs, results[:n_b]))
        self.out_news = dict(zip(self.news, results[n_b:]))


def _pallas(comm, body, *, name, grid, in_specs, out_specs, out_shape, compiler_params,
            scratch_shapes=(), input_output_aliases=None):
    aliases = dict(input_output_aliases or {})
    if comm is None or comm.empty():
        return pl.pallas_call(body, name=name, grid=grid, in_specs=in_specs, out_specs=out_specs,
                              out_shape=out_shape, scratch_shapes=list(scratch_shapes),
                              input_output_aliases=aliases, compiler_params=compiler_params)
    single = not isinstance(out_shape, (list, tuple))
    out_shapes = [out_shape] if single else list(out_shape)
    out_specs_l = [out_specs] if single else list(out_specs)
    n_in, n_out, n_scr = len(in_specs), len(out_shapes), len(scratch_shapes)
    n_s, n_b, n_n = len(comm.srcs), len(comm.bufs), len(comm.news)
    n_copies = comm.count()

    def wrapped(*refs):
        ins = refs[:n_in]
        src_refs = refs[n_in:n_in + n_s]
        o0 = n_in + n_s + n_b
        outs = refs[o0:o0 + n_out]
        buf_refs = refs[o0 + n_out:o0 + n_out + n_b]
        new_refs = refs[o0 + n_out + n_b:o0 + n_out + n_b + n_n]
        rest = refs[o0 + n_out + n_b + n_n:]
        scratch, (send, recv) = rest[:n_scr], rest[n_scr:]
        ids = [pl.program_id(a) for a in range(len(grid))]
        first = functools.reduce(jnp.logical_and, [i == 0 for i in ids])
        last = functools.reduce(jnp.logical_and, [i == g - 1 for i, g in zip(ids, grid)])

        @pl.when(first)
        def _():
            for cp in comm.copies(src_refs, buf_refs, new_refs, send, recv):
                cp.start()

        body(*ins, *outs, *scratch)

        @pl.when(last)
        def _():
            for cp in comm.copies(src_refs, buf_refs, new_refs, send, recv):
                cp.wait()

    hbm = pl.BlockSpec(memory_space=pl.ANY)
    for j in range(n_b):
        aliases[n_in + n_s + j] = n_out + j
    call = pl.pallas_call(
        wrapped, name=name, grid=grid,
        in_specs=[*in_specs, *([hbm] * (n_s + n_b))],
        out_specs=[*out_specs_l, *([hbm] * (n_b + n_n))],
        out_shape=[*out_shapes,
                   *[jax.ShapeDtypeStruct(a.shape, a.dtype) for a in comm.bufs.values()],
                   *comm.news.values()],
        scratch_shapes=[*scratch_shapes, pltpu.SemaphoreType.DMA((n_copies,)),
                        pltpu.SemaphoreType.DMA((n_copies,))],
        input_output_aliases=aliases, compiler_params=compiler_params)

    def run(*args):
        res = call(*args, *comm.srcs.values(), *comm.bufs.values())
        comm.land(res[n_out:])
        return res[0] if single else res[:n_out]

    return run


def _copy_call(comm, name):
    n_s, n_b, n_n = len(comm.srcs), len(comm.bufs), len(comm.news)
    n_copies = comm.count()

    def body(*refs):
        src_refs = refs[:n_s]
        buf_refs = refs[n_s + n_b:n_s + 2 * n_b]
        new_refs = refs[n_s + 2 * n_b:n_s + 2 * n_b + n_n]
        send, recv = refs[n_s + 2 * n_b + n_n:]
        copies = comm.copies(src_refs, buf_refs, new_refs, send, recv)
        for cp in copies:
            cp.start()
        for cp in copies:
            cp.wait()

    hbm = pl.BlockSpec(memory_space=pl.ANY)
    res = pl.pallas_call(
        body, name=name,
        in_specs=[hbm] * (n_s + n_b), out_specs=[hbm] * (n_b + n_n),
        out_shape=[*[jax.ShapeDtypeStruct(a.shape, a.dtype) for a in comm.bufs.values()],
                   *comm.news.values()],
        input_output_aliases={n_s + j: j for j in range(n_b)},
        scratch_shapes=[pltpu.SemaphoreType.DMA((n_copies,)), pltpu.SemaphoreType.DMA((n_copies,))],
    )(*comm.srcs.values(), *comm.bufs.values())
    comm.land(res)


def _embed(meta_pad, x, name, comm=None):
    seq, d = x.shape
    t = seq + LANES

    def body(m_ref, x_ref, h_ref, hb_ref):
        first = pl.program_id(0) == 0
        v = jnp.where(first, m_ref[...], x_ref[...])
        h_ref[...] = v
        hb_ref[...] = v.astype(bf16)

    return _pallas(
        comm, body, name=name, grid=(t // LANES,),
        in_specs=[pl.BlockSpec((LANES, d), lambda i: (0, 0)),
                  pl.BlockSpec((LANES, d), lambda i: (jnp.maximum(i - 1, 0), 0))],
        out_specs=[pl.BlockSpec((LANES, d), lambda i: (i, 0)),
                   pl.BlockSpec((LANES, d), lambda i: (i, 0))],
        out_shape=[jax.ShapeDtypeStruct((t, d), f32), jax.ShapeDtypeStruct((t, d), bf16)],
        compiler_params=_params("parallel"),
    )(meta_pad, x)


def _nn_matmul(x, w, out_dtype, name, comm=None):
    t, k = x.shape
    s_n, _, n = w.shape
    assert s_n == 1 or n % LANES == 0
    tm = _tile(t, 640, 16)

    def body(x_ref, w_ref, o_ref):
        o_ref[...] = jnp.dot(x_ref[...].astype(bf16), w_ref[...],
                             preferred_element_type=f32).astype(o_ref.dtype)

    return _pallas(
        comm, body, name=name, grid=(s_n, t // tm),
        in_specs=[pl.BlockSpec((tm, k), lambda s, i: (i, 0)),
                  pl.BlockSpec((None, k, n), lambda s, i: (s, 0, 0))],
        out_specs=pl.BlockSpec((tm, n), lambda s, i: (i, s)),
        out_shape=jax.ShapeDtypeStruct((t, s_n * n), out_dtype),
        compiler_params=_params("parallel", "parallel"),
    )(x, w)


def _ffn_up(hb, wg, wu, layer, name, comm=None):
    t, d = hb.shape
    s_n, _, _, n = wg.shape
    tm = _tile(t, 640, 16)

    def body(x_ref, wg_ref, wu_ref, a_ref, b_ref, s_ref):
        x = x_ref[...]
        a = jnp.dot(x, wg_ref[...], preferred_element_type=f32)
        b = jnp.dot(x, wu_ref[...], preferred_element_type=f32)
        a_ref[...] = a
        b_ref[...] = b
        s_ref[...] = (a * jax.nn.sigmoid(a) * b).astype(bf16)

    wspec = pl.BlockSpec((None, None, d, n), lambda s, i: (s, layer, 0, 0))
    ospec = pl.BlockSpec((None, tm, n), lambda s, i: (s, i, 0))
    return _pallas(
        comm, body, name=name, grid=(s_n, t // tm),
        in_specs=[pl.BlockSpec((tm, d), lambda s, i: (i, 0)), wspec, wspec],
        out_specs=[ospec, ospec, ospec],
        out_shape=[jax.ShapeDtypeStruct((s_n, t, n), f32), jax.ShapeDtypeStruct((s_n, t, n), f32),
                   jax.ShapeDtypeStruct((s_n, t, n), bf16)],
        compiler_params=_params("parallel", "parallel"),
    )(hb, wg, wu)


def _down_ln(x, w, layer, hprev, gain, bias, beta, name, comm=None):
    s_n, t, k = x.shape
    d = w.shape[-1]
    tm = _tile(t, 640, 16)

    def body(x_ref, w_ref, h_ref, g_ref, b_ref, r_out, h_out, hb_out, acc):
        s = pl.program_id(1)

        @pl.when(s == 0)
        def _():
            acc[...] = jnp.zeros_like(acc)

        acc[...] += jnp.dot(x_ref[...], w_ref[...], preferred_element_type=f32)

        @pl.when(s == s_n - 1)
        def _():
            r = ALPHA * h_ref[...] + beta * acc[...]
            mu = jnp.mean(r, axis=-1, keepdims=True)
            xc = r - mu
            var = jnp.mean(xc * xc, axis=-1, keepdims=True)
            y = xc * lax.rsqrt(var + LN_EPS) * g_ref[...] + b_ref[...]
            r_out[...] = r
            h_out[...] = y
            hb_out[...] = y.astype(bf16)

    row = pl.BlockSpec((tm, d), lambda i, s: (i, 0))
    vec = pl.BlockSpec((1, d), lambda i, s: (0, 0))
    return _pallas(
        comm, body, name=name, grid=(t // tm, s_n),
        in_specs=[pl.BlockSpec((None, tm, k), lambda i, s: (s, i, 0)),
                  pl.BlockSpec((None, None, k, d), lambda i, s: (s, layer, 0, 0)),
                  row, vec, vec],
        out_specs=[row, row, row],
        out_shape=[jax.ShapeDtypeStruct((t, d), f32), jax.ShapeDtypeStruct((t, d), f32),
                   jax.ShapeDtypeStruct((t, d), bf16)],
        scratch_shapes=[pltpu.VMEM((tm, d), f32)],
        compiler_params=_params("parallel", "arbitrary"),
    )(x, w, hprev, gain, bias)


def _conv_fwd(p, conv_w, name):
    t, d3 = p.shape
    d = d3 // 3
    tm = _tile(t, 320, 8)
    hb = tm // 8

    def body(p_ref, prev_ref, w_ref, z_ref):
        i = pl.program_id(0)
        rows = i * tm - 8 + lax.broadcasted_iota(jnp.int32, (tm + 8, 1), 0)
        cg = jnp.concatenate([prev_ref[:, d:2 * d], p_ref[:, d:2 * d]], axis=0)
        val = jnp.concatenate([prev_ref[:, 2 * d:], p_ref[:, 2 * d:]], axis=0)
        u = jnp.where(rows >= PAD, cg * val, 0.0)
        y = (w_ref[2:3, :] * u + w_ref[1:2, :] * pltpu.roll(u, 1, 0)
             + w_ref[0:1, :] * pltpu.roll(u, 2, 0))
        z_ref[...] = (p_ref[:, :d] * y[8:]).astype(bf16)

    return pl.pallas_call(
        body, name=name, grid=(t // tm,),
        in_specs=[pl.BlockSpec((tm, d3), lambda i: (i, 0)),
                  pl.BlockSpec((8, d3), lambda i: (jnp.maximum(i * hb - 1, 0), 0)),
                  pl.BlockSpec((3, d), lambda i: (0, 0))],
        out_specs=pl.BlockSpec((tm, d), lambda i: (i, 0)),
        out_shape=jax.ShapeDtypeStruct((t, d), bf16),
        compiler_params=_params("parallel"),
    )(p, p, conv_w)


def _row_scan(x, name, fbias=None):
    t, n = x.shape
    blk = LANES
    gate = fbias is not None

    def body(*refs):
        if gate:
            x_ref, fb_ref, o_ref, carry = refs
        else:
            x_ref, o_ref, carry = refs
        i = pl.program_id(0)

        @pl.when(i == 0)
        def _():
            carry[...] = jnp.zeros_like(carry)

        v = x_ref[...]
        r = lax.broadcasted_iota(jnp.int32, (blk, n), 0)
        if gate:
            v = v + fb_ref[...]
            v = jnp.minimum(v, 0.0) - jnp.log1p(jnp.exp(-jnp.abs(v)))
            v = jnp.where(i * blk + r >= PAD, v, 0.0)
        sh = 1
        while sh < blk:
            v = v + jnp.where(r >= sh, pltpu.roll(v, sh, 0), 0.0)
            sh *= 2
        v = v + carry[...]
        o_ref[...] = v
        carry[...] = o_ref[blk - 1:blk, :]

    in_specs = [pl.BlockSpec((blk, n), lambda i: (i, 0))]
    args = [x]
    if gate:
        in_specs.append(pl.BlockSpec((1, n), lambda i: (0, 0)))
        args.append(fbias)
    return pl.pallas_call(
        body, name=name, grid=(t // blk,),
        in_specs=in_specs,
        out_specs=pl.BlockSpec((blk, n), lambda i: (i, 0)),
        out_shape=jax.ShapeDtypeStruct((t, n), f32),
        scratch_shapes=[pltpu.VMEM((1, n), f32)],
        compiler_params=_params("arbitrary"),
    )(*args)


STRIP = 16


def _lanes(x, n):
    return jnp.concatenate([x] * (n // LANES), axis=1)


def _attn_fwd(q, k, v, cq_rep, ck_rows, name):
    t, d = q.shape
    n_heads = d // HEAD_DIM
    bk = ck_rows.shape[-1]
    bq = bk
    scale = 1.0 / math.sqrt(HEAD_DIM)

    def lane_fold(x, op):
        out = x[:, :LANES]
        for c0 in range(LANES, bk, LANES):
            out = op(out, x[:, c0:c0 + LANES])
        return out

    def body(q_ref, k_ref, v_ref, cq_ref, ck_ref, o_ref, o32_ref, lse_ref,
             s_scr, p_scr, m_scr, l_scr, red_scr, acc_scr):
        i = pl.program_id(1)
        m_scr[...] = jnp.full_like(m_scr, NEG_INF)
        l_scr[...] = jnp.zeros_like(l_scr)
        acc_scr[...] = jnp.zeros_like(acc_scr)
        qb = q_ref[...]
        ahead = (lax.broadcasted_iota(jnp.int32, (STRIP, bk), 1)
                 - lax.broadcasted_iota(jnp.int32, (STRIP, bk), 0))

        def tile(j, diagonal):
            k0 = pl.multiple_of(j * bk, bk)
            s_scr[...] = lax.dot_general(qb, k_ref[pl.ds(k0, bk), :], NT_DIMS,
                                         preferred_element_type=f32)
            ck = ck_ref[j]
            for r in range(0, bq, STRIP):
                rows = slice(r, r + STRIP)
                s = s_scr[rows, :] * scale + _lanes(cq_ref[rows, :], bk) - ck
                if diagonal:
                    s = jnp.where(ahead <= r, s, NEG_INF)
                s_scr[rows, :] = s
                red_scr[rows, :] = lane_fold(s, jnp.maximum)
            m_old = m_scr[...]
            m_new = jnp.maximum(m_old, jnp.broadcast_to(
                jnp.max(red_scr[...], axis=1, keepdims=True), (bq, LANES)))
            a = jnp.exp(m_old - m_new)
            m_scr[...] = m_new
            for r in range(0, bq, STRIP):
                rows = slice(r, r + STRIP)
                pr = jnp.exp(s_scr[rows, :] - _lanes(m_scr[rows, :], bk))
                red_scr[rows, :] = lane_fold(pr, jnp.add)
                p_scr[rows, :] = pr.astype(bf16)
            l_scr[...] = a * l_scr[...] + jnp.broadcast_to(
                jnp.sum(red_scr[...], axis=1, keepdims=True), (bq, LANES))
            acc_scr[...] = a * acc_scr[...] + jnp.dot(
                p_scr[...], v_ref[pl.ds(k0, bk), :], preferred_element_type=f32)

        def full_tile(j, carry):
            tile(j, False)
            return carry

        lax.fori_loop(0, i, full_tile, 0)
        tile(i, True)
        out = acc_scr[...] / l_scr[...]
        o_ref[...] = out.astype(bf16)
        o32_ref[...] = out
        lse_ref[...] = m_scr[...] + jnp.log(l_scr[...])

    qblk = pl.BlockSpec((bq, HEAD_DIM), lambda h, i: (i, h))
    head_rows = pl.BlockSpec((t, HEAD_DIM), lambda h, i: (0, h))
    rep = pl.BlockSpec((None, bq, LANES), lambda h, i: (h, i, 0))
    col = pltpu.VMEM((bq, LANES), f32)
    return pl.pallas_call(
        body, name=name, grid=(n_heads, t // bq),
        in_specs=[qblk, head_rows, head_rows, rep,
                  pl.BlockSpec((None, t // bk, 1, bk), lambda h, i: (h, 0, 0, 0))],
        out_specs=[qblk, qblk, rep],
        out_shape=[jax.ShapeDtypeStruct((t, d), bf16), jax.ShapeDtypeStruct((t, d), f32),
                   jax.ShapeDtypeStruct((n_heads, t, LANES), f32)],
        scratch_shapes=[pltpu.VMEM((bq, bk), f32), pltpu.VMEM((bq, bk), bf16), col, col, col,
                        pltpu.VMEM((bq, HEAD_DIM), f32)],
        compiler_params=_params("parallel", "parallel"),
    )(q, k, v, cq_rep, ck_rows)


def _loss_head(h, target, name):
    t, d = h.shape

    def body(h_ref, t_ref, dy_ref, loss_ref):
        i = pl.program_id(0)

        @pl.when(i == 0)
        def _():
            loss_ref[...] = jnp.zeros_like(loss_ref)

        diff = jnp.where(i >= 1, h_ref[...] - t_ref[...], 0.0)
        dy_ref[...] = diff * (1.0 / d)
        loss_ref[...] += jnp.sum(diff * diff)

    return pl.pallas_call(
        body, name=name, grid=(t // LANES,),
        in_specs=[pl.BlockSpec((LANES, d), lambda i: (i, 0)),
                  pl.BlockSpec((LANES, d), lambda i: (jnp.maximum(i - 1, 0), 0))],
        out_specs=[pl.BlockSpec((LANES, d), lambda i: (i, 0)),
                   pl.BlockSpec((1, LANES), lambda i: (0, 0))],
        out_shape=[jax.ShapeDtypeStruct((t, d), f32), jax.ShapeDtypeStruct((1, LANES), f32)],
        compiler_params=_params("arbitrary"),
    )(h, target)


def _ln_bwd(dh, r, gain, name):
    t, d = r.shape
    tm = _tile(t, 640, 16)

    def body(dh_ref, r_ref, g_ref, dr_ref, drb_ref, dg_ref, db_ref):
        @pl.when(pl.program_id(0) == 0)
        def _():
            dg_ref[...] = jnp.zeros_like(dg_ref)
            db_ref[...] = jnp.zeros_like(db_ref)

        rr = r_ref[...]
        mu = jnp.mean(rr, axis=-1, keepdims=True)
        xc = rr - mu
        var = jnp.mean(xc * xc, axis=-1, keepdims=True)
        rstd = lax.rsqrt(var + LN_EPS)
        xhat = xc * rstd
        dy = dh_ref[...]
        dxh = dy * g_ref[...]
        m1 = jnp.mean(dxh, axis=-1, keepdims=True)
        m2 = jnp.mean(dxh * xhat, axis=-1, keepdims=True)
        dr = rstd * (dxh - m1 - xhat * m2)
        dr_ref[...] = dr
        drb_ref[...] = dr.astype(bf16)
        dg_ref[...] += jnp.sum(dy * xhat, axis=0, keepdims=True)
        db_ref[...] += jnp.sum(dy, axis=0, keepdims=True)

    row = pl.BlockSpec((tm, d), lambda i: (i, 0))
    vec = pl.BlockSpec((1, d), lambda i: (0, 0))
    return pl.pallas_call(
        body, name=name, grid=(t // tm,),
        in_specs=[row, row, vec],
        out_specs=[row, row, vec, vec],
        out_shape=[jax.ShapeDtypeStruct((t, d), f32), jax.ShapeDtypeStruct((t, d), bf16),
                   jax.ShapeDtypeStruct((1, d), f32), jax.ShapeDtypeStruct((1, d), f32)],
        compiler_params=_params("arbitrary"),
    )(dh, r, gain)


def _ffn_bwd_act(drb, wd, layer, a, b, name, comm=None):
    t, d = drb.shape
    s_n, _, n = a.shape
    tm = _tile(t, 640, 16)

    def body(dr_ref, w_ref, a_ref, b_ref, da_ref, db_ref):
        ds = 0.5 * lax.dot_general(dr_ref[...], w_ref[...], NT_DIMS, preferred_element_type=f32)
        av = a_ref[...]
        sig = jax.nn.sigmoid(av)
        da_ref[...] = (ds * b_ref[...] * (sig * (1.0 + av * (1.0 - sig)))).astype(bf16)
        db_ref[...] = (ds * (av * sig)).astype(bf16)

    act = pl.BlockSpec((None, tm, n), lambda s, i: (s, i, 0))
    return _pallas(
        comm, body, name=name, grid=(s_n, t // tm),
        in_specs=[pl.BlockSpec((tm, d), lambda s, i: (i, 0)),
                  pl.BlockSpec((None, None, n, d), lambda s, i: (s, layer, 0, 0)), act, act],
        out_specs=[act, act],
        out_shape=[jax.ShapeDtypeStruct((s_n, t, n), bf16), jax.ShapeDtypeStruct((s_n, t, n), bf16)],
        compiler_params=_params("parallel", "parallel"),
    )(drb, wd, a, b)


def _act_spec(mode, tt, k, t_first):
    def fix(fn):
        return (lambda i, s: fn(s, i)) if t_first else fn
    if mode == "shared":
        return pl.BlockSpec((tt, k), fix(lambda s, i: (i, 0)))
    if mode == "cols":
        return pl.BlockSpec((tt, k), fix(lambda s, i: (i, s)))
    assert mode == "stack"
    return pl.BlockSpec((None, tt, k), fix(lambda s, i: (s, i, 0)))


def _act_width(arr, mode, s_n):
    return arr.shape[-1] // s_n if mode == "cols" else arr.shape[-1]


def _tn_matmul(x, xmode, ys, ymodes, s_n, scale, name, comm=None):
    t = x.shape[-2]
    kx = _act_width(x, xmode, s_n)
    kys = [_act_width(y, m, s_n) for y, m in zip(ys, ymodes)]
    tt = _tile(t, 2080, 16)
    n_t = t // tt
    n_y = len(ys)

    def body(*refs):
        x_ref = refs[0]
        y_refs = refs[1:1 + n_y]
        o_refs = refs[1 + n_y:]
        i = pl.program_id(1)
        xv = x_ref[...].astype(bf16)
        for y_ref, o_ref in zip(y_refs, o_refs):
            part = lax.dot_general(xv, y_ref[...].astype(bf16), TN_DIMS, preferred_element_type=f32)

            @pl.when(i == 0)
            def _():
                o_ref[...] = part

            @pl.when(i > 0)
            def _():
                o_ref[...] += part

            if scale != 1.0:
                @pl.when(i == n_t - 1)
                def _():
                    o_ref[...] = o_ref[...] * scale

    in_specs = [_act_spec(xmode, tt, kx, False)]
    in_specs += [_act_spec(m, tt, ky, False) for m, ky in zip(ymodes, kys)]
    return _pallas(
        comm, body, name=name, grid=(s_n, n_t),
        in_specs=in_specs,
        out_specs=[pl.BlockSpec((None, kx, ky), lambda s, i: (s, 0, 0)) for ky in kys],
        out_shape=[jax.ShapeDtypeStruct((s_n, kx, ky), f32) for ky in kys],
        compiler_params=_params("parallel", "arbitrary"),
    )(x, *ys)


def _nt_sum(pairs, base, base_scale, s_n, out_dtype, name, comm=None):
    t = pairs[0][0].shape[-2]
    d = pairs[0][2].shape[-2]
    tm = _tile(t, 640, 16)
    n_p = len(pairs)
    has_base = base is not None

    def body(*refs):
        dy_refs = refs[0:2 * n_p:2]
        w_refs = refs[1:2 * n_p:2]
        rest = refs[2 * n_p:]
        base_ref = rest[0] if has_base else None
        o_ref, acc = rest[-2], rest[-1]
        s = pl.program_id(1)

        @pl.when(s == 0)
        def _():
            acc[...] = jnp.zeros_like(acc)

        tot = None
        for dy_ref, w_ref in zip(dy_refs, w_refs):
            part = lax.dot_general(dy_ref[...].astype(bf16), w_ref[...], NT_DIMS,
                                   preferred_element_type=f32)
            tot = part if tot is None else tot + part
        acc[...] += tot

        @pl.when(s == s_n - 1)
        def _():
            res = acc[...]
            if has_base:
                res = base_scale * base_ref[...] + res
            o_ref[...] = res.astype(o_ref.dtype)

    in_specs, args = [], []
    for dy, mode, w, layer in pairs:
        k = _act_width(dy, mode, s_n)
        in_specs.append(_act_spec(mode, tm, k, True))
        if layer is None:
            in_specs.append(pl.BlockSpec((None, d, k), lambda i, s: (s, 0, 0)))
        else:
            in_specs.append(pl.BlockSpec((None, None, d, k),
                                         functools.partial(lambda i, s, l: (s, l, 0, 0), l=layer)))
        args += [dy, w]
    row = pl.BlockSpec((tm, d), lambda i, s: (i, 0))
    if has_base:
        in_specs.append(row)
        args.append(base)
    return _pallas(
        comm, body, name=name, grid=(t // tm, s_n),
        in_specs=in_specs, out_specs=row,
        out_shape=jax.ShapeDtypeStruct((t, d), out_dtype),
        scratch_shapes=[pltpu.VMEM((tm, d), f32)],
        compiler_params=_params("parallel", "arbitrary"),
    )(*args)


def _conv_bwd(dz, p, conv_w, name, comm=None):
    t, d3 = p.shape
    d = d3 // 3
    tm = _tile(t, 320, 8)
    hb = tm // 8
    last8 = t // 8 - 1
    n_ext = tm + 8

    def body(dz_ref, dzn_ref, p_ref, pp_ref, pn_ref, w_ref, dp_ref, dw_ref):
        i = pl.program_id(0)

        @pl.when(i == 0)
        def _():
            dw_ref[...] = jnp.zeros_like(dw_ref)

        w0, w1, w2 = w_ref[0:1, :], w_ref[1:2, :], w_ref[2:3, :]
        rows_u = i * tm - 8 + lax.broadcasted_iota(jnp.int32, (n_ext, 1), 0)
        cg = jnp.concatenate([pp_ref[:, d:2 * d], p_ref[:, d:2 * d]], axis=0)
        val = jnp.concatenate([pp_ref[:, 2 * d:], p_ref[:, 2 * d:]], axis=0)
        u = jnp.where(rows_u >= PAD, cg * val, 0.0)
        u1 = pltpu.roll(u, 1, 0)
        u2 = pltpu.roll(u, 2, 0)
        y = (w2 * u + w1 * u1 + w0 * u2)[8:]
        dzv = dz_ref[...]
        bg = p_ref[:, :d]
        rows_n = (i + 1) * tm + lax.broadcasted_iota(jnp.int32, (8, 1), 0)
        dy_main = dzv * bg
        dy_next = jnp.where(rows_n < t, dzn_ref[...] * pn_ref[:, :d], 0.0)
        dye = jnp.concatenate([dy_main, dy_next], axis=0)
        du = (w2 * dye + w1 * pltpu.roll(dye, n_ext - 1, 0)
              + w0 * pltpu.roll(dye, n_ext - 2, 0))[:tm]
        du = jnp.where(rows_u[8:] >= PAD, du, 0.0)
        dp_ref[:, :d] = (dzv * y).astype(bf16)
        dp_ref[:, d:2 * d] = (du * val[8:]).astype(bf16)
        dp_ref[:, 2 * d:] = (du * cg[8:]).astype(bf16)
        dw_ref[0:1, :] += jnp.sum(dy_main * u2[8:], axis=0, keepdims=True)
        dw_ref[1:2, :] += jnp.sum(dy_main * u1[8:], axis=0, keepdims=True)
        dw_ref[2:3, :] += jnp.sum(dy_main * u[8:], axis=0, keepdims=True)

    nxt = lambda i: (jnp.minimum((i + 1) * hb, last8), 0)
    return _pallas(
        comm, body, name=name, grid=(t // tm,),
        in_specs=[pl.BlockSpec((tm, d), lambda i: (i, 0)),
                  pl.BlockSpec((8, d), nxt),
                  pl.BlockSpec((tm, d3), lambda i: (i, 0)),
                  pl.BlockSpec((8, d3), lambda i: (jnp.maximum(i * hb - 1, 0), 0)),
                  pl.BlockSpec((8, d3), nxt),
                  pl.BlockSpec((3, d), lambda i: (0, 0))],
        out_specs=[pl.BlockSpec((tm, d3), lambda i: (i, 0)),
                   pl.BlockSpec((3, d), lambda i: (0, 0))],
        out_shape=[jax.ShapeDtypeStruct((t, d3), bf16), jax.ShapeDtypeStruct((3, d), f32)],
        compiler_params=_params("arbitrary"),
    )(dz, dz, p, p, p, conv_w)


def _attn_stats(o, do, lse_rep, cq_rep, bq, name):
    t, d = o.shape
    n_heads = d // HEAD_DIM

    def body(o_ref, do_ref, lse_ref, cq_ref, cl_ref, delta_ref):
        for c0 in range(0, bq, LANES):
            rows = slice(c0, c0 + LANES)
            cl_ref[:, rows] = (cq_ref[rows, :] - lse_ref[rows, :]).T[0:1, :]
            prod = o_ref[rows, :] * do_ref[rows, :].astype(f32)
            delta_ref[:, rows] = jnp.sum(prod.T, axis=0, keepdims=True)

    qblk = pl.BlockSpec((bq, HEAD_DIM), lambda h, i: (i, h))
    rep = pl.BlockSpec((None, bq, LANES), lambda h, i: (h, i, 0))
    row = pl.BlockSpec((None, None, 1, bq), lambda h, i: (h, i, 0, 0))
    shp = jax.ShapeDtypeStruct((n_heads, t // bq, 1, bq), f32)
    return pl.pallas_call(
        body, name=name, grid=(n_heads, t // bq),
        in_specs=[qblk, qblk, rep, rep],
        out_specs=[row, row], out_shape=[shp, shp],
        compiler_params=_params("parallel", "parallel"),
    )(o, do, lse_rep, cq_rep)


def _attn_bwd(q, k, v, kt, do, ckey, cl_rows, delta_rows, name, comm=None):
    t, d = q.shape
    n_heads = d // HEAD_DIM
    bk = kt.shape[-1]
    bq = bk
    n_kv = t // bk
    n_q = t // bq
    scale = 1.0 / math.sqrt(HEAD_DIM)

    def body(q_ref, do_ref, cl_ref, dl_ref, k_ref, v_ref, kt_ref, ck_ref,
             dq_ref, dcq_ref, dk_ref, dv_ref, dck_ref,
             st_scr, dp_scr, p_scr, ds_scr, dqt, dk_acc, dv_acc, dck_acc):
        j = pl.program_id(1)

        @pl.when(j == 0)
        def _():
            dqt[...] = jnp.zeros_like(dqt)
            dcq_ref[...] = jnp.zeros_like(dcq_ref)

        dk_acc[...] = jnp.zeros_like(dk_acc)
        dv_acc[...] = jnp.zeros_like(dv_acc)
        dck_acc[...] = jnp.zeros_like(dck_acc)
        kb = k_ref[...]
        vb = v_ref[...]
        behind = (lax.broadcasted_iota(jnp.int32, (STRIP, bq), 1)
                  - lax.broadcasted_iota(jnp.int32, (STRIP, bq), 0))

        def tile(i, diagonal):
            r0 = pl.multiple_of(i * bq, bq)
            qi = q_ref[pl.ds(r0, bq), :]
            doi = do_ref[pl.ds(r0, bq), :]
            st_scr[...] = lax.dot_general(kb, qi, NT_DIMS, preferred_element_type=f32)
            dp_scr[...] = lax.dot_general(vb, doi, NT_DIMS, preferred_element_type=f32)
            cl = cl_ref[i]
            dl = dl_ref[i]
            over_keys = jnp.zeros((STRIP, bq), f32)
            for r in range(0, bk, STRIP):
                keys = slice(r, r + STRIP)
                st = st_scr[keys, :] * scale + cl - _lanes(ck_ref[keys, :], bq)
                if diagonal:
                    st = jnp.where(behind >= r, st, NEG_INF)
                pr = jnp.exp(st)
                ds = pr * (dp_scr[keys, :] - dl)
                over_keys = over_keys + ds
                dck_acc[keys, :] -= jnp.sum(ds, axis=1, keepdims=True)
                p_scr[keys, :] = pr.astype(bf16)
                ds_scr[keys, :] = ds.astype(bf16)
            dcq_ref[i] += jnp.sum(over_keys, axis=0, keepdims=True)
            dv_acc[...] += jnp.dot(p_scr[...], doi, preferred_element_type=f32)
            dk_acc[...] += jnp.dot(ds_scr[...], qi, preferred_element_type=f32)
            dqt[i] += jnp.dot(kt_ref[...], ds_scr[...], preferred_element_type=f32)

        def full_tile(i, carry):
            tile(i, False)
            return carry

        tile(j, True)
        lax.fori_loop(j + 1, n_q, full_tile, 0)
        dk_ref[...] = (dk_acc[...] * scale).astype(bf16)
        dv_ref[...] = dv_acc[...].astype(bf16)
        for c0 in range(0, bk, LANES):
            keys = slice(c0, c0 + LANES)
            dck_ref[:, keys] = jnp.broadcast_to(dck_acc[keys, :], (LANES, LANES)).T[0:1, :]

        @pl.when(j == n_kv - 1)
        def _():
            def emit(i, carry):
                r0 = pl.multiple_of(i * bq, bq)
                dq_ref[pl.ds(r0, bq), :] = dqt[i].T * scale
                return carry
            lax.fori_loop(0, n_q, emit, 0)

    head_rows = pl.BlockSpec((t, HEAD_DIM), lambda h, j: (0, h))
    head_stat = pl.BlockSpec((None, n_q, 1, bq), lambda h, j: (h, 0, 0, 0))
    kblk = pl.BlockSpec((bk, HEAD_DIM), lambda h, j: (j, h))
    return _pallas(
        comm, body, name=name, grid=(n_heads, n_kv),
        in_specs=[head_rows, head_rows, head_stat, head_stat, kblk, kblk,
                  pl.BlockSpec((None, None, HEAD_DIM, bk), lambda h, j: (h, j, 0, 0)),
                  pl.BlockSpec((None, bk, LANES), lambda h, j: (h, j, 0))],
        out_specs=[head_rows, head_stat, kblk, kblk,
                   pl.BlockSpec((None, None, 1, bk), lambda h, j: (h, j, 0, 0))],
        out_shape=[jax.ShapeDtypeStruct((t, d), f32),
                   jax.ShapeDtypeStruct((n_heads, n_q, 1, bq), f32),
                   jax.ShapeDtypeStruct((t, d), bf16), jax.ShapeDtypeStruct((t, d), bf16),
                   jax.ShapeDtypeStruct((n_heads, n_kv, 1, bk), f32)],
        scratch_shapes=[pltpu.VMEM((bk, bq), f32), pltpu.VMEM((bk, bq), f32),
                        pltpu.VMEM((bk, bq), bf16), pltpu.VMEM((bk, bq), bf16),
                        pltpu.VMEM((n_q, HEAD_DIM, bq), f32),
                        pltpu.VMEM((bk, HEAD_DIM), f32), pltpu.VMEM((bk, HEAD_DIM), f32),
                        pltpu.VMEM((bk, 1), f32)],
        compiler_params=_params("parallel", "arbitrary"),
    )(q, do, cl_rows, delta_rows, k, v, kt, ckey)


def _fgate_bwd(dlogf, flog, fbias, name):
    t, n = flog.shape

    def body(dl_ref, fl_ref, fb_ref, o_ref, sum_ref):
        i = pl.program_id(0)

        @pl.when(i == 0)
        def _():
            sum_ref[...] = jnp.zeros_like(sum_ref)

        r = i * LANES + lax.broadcasted_iota(jnp.int32, (LANES, n), 0)
        g = dl_ref[...] * jax.nn.sigmoid(-(fl_ref[...] + fb_ref[...]))
        g = jnp.where(r >= PAD, g, 0.0)
        o_ref[...] = g
        sum_ref[...] += jnp.sum(g, axis=0, keepdims=True)

    blk = pl.BlockSpec((LANES, n), lambda i: (i, 0))
    vec = pl.BlockSpec((1, n), lambda i: (0, 0))
    return pl.pallas_call(
        body, name=name, grid=(t // LANES,),
        in_specs=[blk, blk, vec], out_specs=[blk, vec],
        out_shape=[jax.ShapeDtypeStruct((t, n), f32), jax.ShapeDtypeStruct((1, n), f32)],
        compiler_params=_params("arbitrary"),
    )(dlogf, flog, fbias)


def _place_shard(w, idx, name):
    n_l, r, c_n = w.shape
    out_dtype = bf16 if r * c_n > 2 ** 16 else w.dtype
    tr = _tile(r, 512, 16) if r % 16 == 0 else r

    def body(idx_ref, w_ref, o_ref):
        o_ref[...] = w_ref[...].astype(out_dtype)

    grid_spec = pltpu.PrefetchScalarGridSpec(
        num_scalar_prefetch=1, grid=(n_l, r // tr),
        in_specs=[pl.BlockSpec((None, tr, c_n), lambda l, i, idx: (l, i, 0))],
        out_specs=pl.BlockSpec((None, None, tr, c_n), lambda l, i, idx: (idx[0], l, i, 0)))
    return pl.pallas_call(
        body, name=name, grid_spec=grid_spec,
        out_shape=jax.ShapeDtypeStruct((N_CHIPS, n_l, r, c_n), out_dtype),
        compiler_params=_params("parallel", "parallel"),
    )(idx, w)


def _plan_gather_ici(items):
    def plan(srcs, bufs, news, p):
        out = []
        for name, layer, r2 in items:
            mine = bufs[name].at[p.me, layer, pl.ds(p.c * r2, r2)]
            out += [(mine, mine, (*chip, p.c)) for chip in p.chips]
        return out
    return plan, 3 * len(items)


def _plan_gather_d2d(items):
    def plan(srcs, bufs, news, p):
        out = []
        for name, layer, r2 in items:
            for px, py in p.chips:
                landed = bufs[name].at[2 * px + py, layer, pl.ds(p.c * r2, r2)]
                out.append((landed, landed, p.sib))
        return out
    return plan, 3 * len(items)


def _plan_pair_exchange(names, r2s):
    def plan(srcs, bufs, news, p):
        out = []
        for name, r2 in zip(names, r2s):
            for k, slot in enumerate(p.slots):
                out.append((srcs["G_" + name].at[slot, pl.ds((1 - p.c) * r2, r2)],
                            news["PAIR_" + name].at[k], p.sib))
        return out
    return plan, 4 * len(names)


def _plan_chip_exchange(names):
    def plan(srcs, bufs, news, p):
        out = []
        for name in names:
            for k, chip in enumerate(p.chips):
                out.append((srcs["SEND_" + name].at[k], news["RECV_" + name].at[k], (*chip, p.c)))
        return out
    return plan, 3 * len(names)


def _plan_pair_share(items):
    def plan(srcs, bufs, news, p):
        out = []
        for name, layer, r2 in items:
            mine = bufs["RED_" + name].at[layer, pl.ds(p.c * r2, r2)]
            out.append((mine, mine, p.sib))
        return out
    return plan, len(items)


def _rs_prepare(g, pair, idx, name):
    _, r2, c_n = pair.shape
    tr = _tile(r2, 256, 8)
    nb = r2 // tr

    def body(idx_ref, g_ref, p_ref, o_ref):
        o_ref[...] = (g_ref[...] + p_ref[...]).astype(bf16)

    grid_spec = pltpu.PrefetchScalarGridSpec(
        num_scalar_prefetch=1, grid=(3, nb),
        in_specs=[pl.BlockSpec((None, tr, c_n), lambda k, i, idx: (idx[k + 1], idx[4] * nb + i, 0)),
                  pl.BlockSpec((None, tr, c_n), lambda k, i, idx: (k + 1, i, 0))],
        out_specs=pl.BlockSpec((None, tr, c_n), lambda k, i, idx: (k, i, 0)))
    return pl.pallas_call(
        body, name=name, grid_spec=grid_spec,
        out_shape=jax.ShapeDtypeStruct((3, r2, c_n), bf16),
        compiler_params=_params("parallel", "parallel"),
    )(idx, g, pair)


def _rs_finish(g, pair, recv, idx, layer, n_layers, into, name):
    _, r2, c_n = pair.shape
    tr = _tile(r2, 256, 8)
    nb = r2 // tr

    def body(*refs):
        g_ref, p_ref, r0_ref, r1_ref, r2_ref = refs[1:6]
        o_ref = refs[-1]
        acc = g_ref[...] + p_ref[...]
        acc = acc + r0_ref[...].astype(f32)
        acc = acc + r1_ref[...].astype(f32)
        acc = acc + r2_ref[...].astype(f32)
        o_ref[...] = acc

    def rspec(k):
        return pl.BlockSpec((None, tr, c_n), functools.partial(lambda i, idx, kk: (kk, i, 0), kk=k))

    in_specs = [pl.BlockSpec((None, tr, c_n), lambda i, idx: (idx[0], idx[4] * nb + i, 0)),
                rspec(0), rspec(0), rspec(1), rspec(2)]
    args = [idx, g, pair, recv, recv, recv]
    aliases = {}
    if into is not None:
        in_specs.append(pl.BlockSpec(memory_space=pl.ANY))
        args.append(into)
        aliases = {6: 0}
    grid_spec = pltpu.PrefetchScalarGridSpec(
        num_scalar_prefetch=1, grid=(nb,), in_specs=in_specs,
        out_specs=pl.BlockSpec((None, tr, c_n), lambda i, idx: (layer, idx[4] * nb + i, 0)))
    return pl.pallas_call(
        body, name=name, grid_spec=grid_spec,
        out_shape=jax.ShapeDtypeStruct((n_layers, 2 * r2, c_n), f32),
        input_output_aliases=aliases,
        compiler_params=_params("parallel"),
    )(*args)


def _adamw(w, g, m, v, layer, into, name):
    n_l, r, c_n = w.shape
    tr = _tile(r, 256, 8)

    def body(*refs):
        w_ref, g_ref, m_ref, v_ref = refs[:4]
        d_ref, mo_ref, vo_ref = refs[-3:]
        gv = g_ref[...]
        mn = ADAM_B1 * m_ref[...] + (1.0 - ADAM_B1) * gv
        vn = ADAM_B2 * v_ref[...] + (1.0 - ADAM_B2) * (gv * gv)
        m_hat = mn / (1.0 - ADAM_B1 ** ADAM_STEP)
        v_hat = vn / (1.0 - ADAM_B2 ** ADAM_STEP)
        d_ref[...] = -ADAM_LR * (m_hat / (jnp.sqrt(v_hat) + ADAM_EPS) + ADAM_WD * w_ref[...])
        mo_ref[...] = mn
        vo_ref[...] = vn

    blk = pl.BlockSpec((None, tr, c_n), lambda i: (layer, i, 0))
    shp = jax.ShapeDtypeStruct((n_l, r, c_n), f32)
    in_specs = [blk] * 4
    args = [w, g, m, v]
    aliases = {}
    if into is not None:
        in_specs = in_specs + [pl.BlockSpec(memory_space=pl.ANY)] * 3
        args += list(into)
        aliases = {4: 0, 5: 1, 6: 2}
    return pl.pallas_call(
        body, name=name, grid=(r // tr,),
        in_specs=in_specs, out_specs=[blk] * 3, out_shape=[shp] * 3,
        input_output_aliases=aliases,
        compiler_params=_params("parallel"),
    )(*args)


def kernel(x, meta, ffn1_wg, ffn1_wu, ffn1_wd, ffn2_wg, ffn2_wu, ffn2_wd, ln_gain, ln_bias, conv_w_in, conv_w, conv_w_out, kv_w, f_bias, attn_w_q, attn_w_o, loss_target, m_meta, m_ffn1_wg, m_ffn1_wu, m_ffn1_wd, m_ffn2_wg, m_ffn2_wu, m_ffn2_wd, m_ln_gain, m_ln_bias, m_conv_w_in, m_conv_w, m_conv_w_out, m_kv_w, m_f_bias, m_attn_w_q, m_attn_w_o, v_meta, v_ffn1_wg, v_ffn1_wu, v_ffn1_wd, v_ffn2_wg, v_ffn2_wu, v_ffn2_wd, v_ln_gain, v_ln_bias, v_conv_w_in, v_conv_w, v_conv_w_out, v_kv_w, v_f_bias, v_attn_w_q, v_attn_w_o):
    seq, d = x.shape[1], x.shape[2]
    t = PAD + N_META + seq
    n_heads = d // HEAD_DIM
    dq = d // N_CHIPS
    n_kv = kv_w.shape[1]
    x2 = x[0]
    target = loss_target[0]

    def rows8(a):
        return jnp.pad(a, ((0, 8 - a.shape[0]), (0, 0)))

    def small_pack(mt, g, b, cw, fb):
        fb_row = jnp.pad(fb, (0, mt.shape[1] - n_heads))[None]
        return jnp.concatenate([mt, rows8(g.reshape(6, -1)), rows8(b.reshape(6, -1)),
                                rows8(cw.reshape(3, -1)), rows8(fb_row)], axis=0)

    w_small = small_pack(meta, ln_gain, ln_bias, conv_w, f_bias)

    cx, cy, c = lax.axis_index("x"), lax.axis_index("y"), lax.axis_index("c")
    idx = jnp.stack([2 * cx + cy, 2 * (1 - cx) + cy, 2 * cx + (1 - cy), 2 * (1 - cx) + (1 - cy), c]
                    ).astype(jnp.int32)

    w3 = {"wg1": ffn1_wg, "wu1": ffn1_wu, "wd1": ffn1_wd, "wg2": ffn2_wg, "wu2": ffn2_wu,
          "wd2": ffn2_wd, "win": conv_w_in, "wout": conv_w_out, "kv": kv_w[None], "wq": attn_w_q,
          "wo": attn_w_o, "small": w_small[None]}
    buf = {n: _place_shard(w, idx, "place_shard") for n, w in w3.items()}

    def split(item):
        name, layer = item.split(".")
        return name, int(layer)

    def gather_stage(planner, items):
        triples = [(n, l, buf[n].shape[2] // 2) for n, l in map(split, items)]
        plan, n_copies = planner(triples)
        return dict(plan=plan, n=n_copies, bufs={n: buf[n] for n, _, _ in triples})

    def ici(items):
        return gather_stage(_plan_gather_ici, items)

    def d2d(items):
        return gather_stage(_plan_gather_d2d, items)

    def pair_exchange(items):
        r2s = [buf["G_" + it].shape[1] // 2 for it in items]
        plan, n_copies = _plan_pair_exchange(items, r2s)
        news = {"PAIR_" + it: jax.ShapeDtypeStruct((N_CHIPS, r2, buf["G_" + it].shape[2]), f32)
                for it, r2 in zip(items, r2s)}
        return dict(plan=plan, n=n_copies, srcs={"G_" + it: buf["G_" + it] for it in items}, news=news)

    def chip_exchange(items):
        plan, n_copies = _plan_chip_exchange(items)
        srcs = {"SEND_" + it: buf["SEND_" + it] for it in items}
        news = {"RECV_" + it: jax.ShapeDtypeStruct(s.shape, s.dtype)
                for it, s in ((it, buf["SEND_" + it]) for it in items)}
        return dict(plan=plan, n=n_copies, srcs=srcs, news=news)

    def pair_share(items):
        triples = [(n, l, buf["RED_" + n].shape[1] // 2) for n, l in map(split, items)]
        plan, n_copies = _plan_pair_share(triples)
        return dict(plan=plan, n=n_copies, bufs={"RED_" + n: buf["RED_" + n] for n, _, _ in triples})

    def run(fn, *args, stages=(), name=None):
        comm = _Copies()
        for st in (stages() if callable(stages) else stages):
            comm.add(st["plan"], st["n"], srcs=st.get("srcs"), bufs=st.get("bufs"), news=st.get("news"))
        out = _copy_call(comm, name) if fn is None else fn(*args, comm=comm)
        buf.update(comm.out_bufs)
        buf.update(comm.out_news)
        return out

    first = ["wg1.0", "wu1.0", "wd1.0", "small.0"]
    run(None, stages=[ici(first)], name="gather_first_ici")
    run(None, stages=[d2d(first)], name="gather_first_d2d")
    small = buf["small"].reshape(N_CHIPS, SMALL_ROWS, dq).transpose(1, 0, 2).reshape(SMALL_ROWS, d)
    meta_full = small[:N_META]
    gains = small[16:22].reshape(DEPTH, 3, 1, d)
    biases = small[24:30].reshape(DEPTH, 3, 1, d)
    conv_w_full = small[32:35]
    fb_pad = jnp.pad(f_bias, (0, LANES - n_heads))[None]
    conv_w8 = ["win.0", "wout.0"]
    ffn2_l0 = ["wg2.0", "wu2.0", "wd2.0", "kv.0"]
    attn_ffn2_l1 = ["wq.0", "wo.0", "wg2.1", "wu2.1", "wd2.1"]
    ffn1_l1 = ["wg1.1", "wu1.1", "wd1.1"]

    meta_pad = jnp.concatenate([jnp.zeros((PAD, d), f32), meta_full], axis=0)
    h0, h0b = run(_embed, meta_pad, x2, "embed", stages=[ici(conv_w8)])
    a1, b1, s1 = run(_ffn_up, h0b, buf["wg1"], buf["wu1"], 0, "ffn_up",
                     stages=lambda: [d2d(conv_w8), ici(ffn2_l0)])
    r1, h1, h1b = run(_down_ln, s1, buf["wd1"], 0, h0, gains[0, 0], biases[0, 0], 0.5, "ffn_down_ln",
                      stages=lambda: [d2d(ffn2_l0), ici(attn_ffn2_l1)])
    n_in = conv_w_in.shape[-1]
    w_in = buf["win"].reshape(N_CHIPS, d, n_in)
    w_out = buf["wout"].reshape(1, 1, d, d)
    p = run(_nn_matmul, h1b, w_in, f32, "conv_in", stages=lambda: [d2d(attn_ffn2_l1), ici(ffn1_l1)])
    z = _conv_fwd(p, conv_w_full, "conv_fwd")
    r2, h2, h2b = run(_down_ln, z[None], w_out, 0, h1, gains[0, 1], biases[0, 1], 1.0, "mix_out_ln",
                      stages=lambda: [d2d(ffn1_l1)])
    wg1, wu1, wd1, wg2, wu2, wd2 = (buf[n] for n in ("wg1", "wu1", "wd1", "wg2", "wu2", "wd2"))
    w_q = buf["wq"].reshape(1, d, d)
    w_o = buf["wo"].reshape(1, 1, d, d)
    kv_full = buf["kv"].reshape(N_CHIPS, d, n_kv).transpose(1, 0, 2).reshape(d, N_CHIPS * n_kv)
    w_k = kv_full[:, :d][None]
    w_v = kv_full[:, d:2 * d][None]
    w_f = jnp.pad(kv_full[:, 2 * d:], ((0, 0), (0, LANES - n_heads)))[None]
    a2, b2, s2 = _ffn_up(h2b, wg2, wu2, 0, "ffn_up")
    r3, h3, h3b = _down_ln(s2, wd2, 0, h2, gains[0, 2], biases[0, 2], 0.5, "ffn_down_ln")
    kk = _nn_matmul(h3b, w_k, bf16, "proj_bf16")
    vv = _nn_matmul(h3b, w_v, bf16, "proj_bf16")
    flog = _nn_matmul(h3b, w_f, f32, "proj_gate")
    cum = _row_scan(flog, "gate_cumsum", fb_pad)
    bk = _tile(t, 640, LANES)
    c_ht = cum[:, :n_heads].T
    c_keys = jnp.where(jnp.arange(t)[None, :] < PAD, 1e30, c_ht)
    cq_rep = jnp.broadcast_to(c_ht[:, :, None], (n_heads, t, LANES))
    ck_rep = jnp.broadcast_to(c_keys[:, :, None], (n_heads, t, LANES))
    ck_rows = c_keys.reshape(n_heads, t // bk, 1, bk)
    a3, b3, s3 = _ffn_up(h3b, wg1, wu1, 1, "ffn_up")
    r4, h4, h4b = _down_ln(s3, wd1, 1, h3, gains[1, 0], biases[1, 0], 0.5, "ffn_down_ln")
    q = _nn_matmul(h4b, w_q, bf16, "proj_bf16")
    o, o32, lse_rep = _attn_fwd(q, kk, vv, cq_rep, ck_rows, "attn_fwd")
    r5, h5, h5b = _down_ln(o[None], w_o, 0, h4, gains[1, 1], biases[1, 1], 1.0, "mix_out_ln")
    a4, b4, s4 = _ffn_up(h5b, wg2, wu2, 1, "ffn_up")
    r6, h6, _ = _down_ln(s4, wd2, 1, h5, gains[1, 2], biases[1, 2], 0.5, "ffn_down_ln")
    dy, sq = _loss_head(h6, target, "loss_head")
    loss_part = 0.5 * sq[0, 0] / d

    m_small = small_pack(m_meta, m_ln_gain, m_ln_bias, m_conv_w, m_f_bias)
    v_small = small_pack(v_meta, v_ln_gain, v_ln_bias, v_conv_w, v_f_bias)
    m3 = {"wg1": m_ffn1_wg, "wu1": m_ffn1_wu, "wd1": m_ffn1_wd, "wg2": m_ffn2_wg, "wu2": m_ffn2_wu,
          "wd2": m_ffn2_wd, "win": m_conv_w_in, "wout": m_conv_w_out, "kv": m_kv_w[None],
          "wq": m_attn_w_q, "wo": m_attn_w_o, "small": m_small[None]}
    v3 = {"wg1": v_ffn1_wg, "wu1": v_ffn1_wu, "wd1": v_ffn1_wd, "wg2": v_ffn2_wg, "wu2": v_ffn2_wu,
          "wd2": v_ffn2_wd, "win": v_conv_w_in, "wout": v_conv_w_out, "kv": v_kv_w[None],
          "wq": v_attn_w_q, "wo": v_attn_w_o, "small": v_small[None]}
    stepped = {}

    def prepare(items):
        for it in items:
            buf["SEND_" + it] = _rs_prepare(buf["G_" + it], buf["PAIR_" + it], idx, "grad_prepare")

    def finish(items):
        for it in items:
            n, l = split(it)
            buf["RED_" + n] = _rs_finish(buf["G_" + it], buf["PAIR_" + it], buf["RECV_" + it], idx, l,
                                         w3[n].shape[0], buf.get("RED_" + n), "grad_finish")

    def adam(items, grads=None):
        for it in items:
            n, l = split(it)
            g = buf["RED_" + n] if grads is None else grads[n]
            stepped[n] = _adamw(w3[n], g, m3[n], v3[n], l, stepped.get(n), "adamw")

    def ffn_bwd(dr, drb, hb_in, a, b, s, f, layer, on_act=(), after_act=None, on_dwd=(),
                after_dwd=None, on_dx=()):
        da, db = run(_ffn_bwd_act, drb, buf["wd" + f], layer, a, b, "ffn_bwd_act", stages=on_act)
        if after_act is not None:
            after_act()
        (buf[f"G_wd{f}.{layer}"],) = run(_tn_matmul, s, "stack", [drb], ["shared"], N_CHIPS, 0.5,
                                         "ffn_dwd", stages=on_dwd)
        if after_dwd is not None:
            after_dwd()
        buf[f"G_wg{f}.{layer}"], buf[f"G_wu{f}.{layer}"] = _tn_matmul(
            hb_in, "shared", [da, db], ["stack", "stack"], N_CHIPS, 1.0, "ffn_dwgu")
        return run(_nt_sum, [(da, "stack", buf["wg" + f], layer), (db, "stack", buf["wu" + f], layer)],
                   dr, ALPHA, N_CHIPS, f32, "ffn_dx", stages=on_dx)

    ffn2_1 = ["wg2.1", "wu2.1", "wd2.1"]
    ffn1_1 = ["wg1.1", "wu1.1", "wd1.1"]
    ffn2_0 = ["wg2.0", "wu2.0", "wd2.0"]
    ffn1_0 = ["wg1.0", "wu1.0", "wd1.0"]
    conv_items = ["wout.0", "win.0"]

    dr6, dr6b, dg12, db12 = _ln_bwd(dy, r6, gains[1, 2], "ln_bwd")
    dh5 = ffn_bwd(dr6, dr6b, h5b, a4, b4, s4, "2", 1, on_dx=lambda: [pair_exchange(ffn2_1)])
    prepare(ffn2_1)
    dr5, dr5b, dg11, db11 = _ln_bwd(dh5, r5, gains[1, 1], "ln_bwd")
    (dwo,) = _tn_matmul(o, "shared", [dr5b], ["shared"], 1, 1.0, "sq_dw")
    buf["G_wo.0"] = dwo.reshape(N_CHIPS, dq, d)
    do = run(_nt_sum, [(dr5b, "cols", w_o[0], None)], None, 1.0, 1, bf16, "sq_dx_bf16",
             stages=lambda: [pair_exchange(["wo.0"])])
    prepare(["wo.0"])
    cl, delta = _attn_stats(o32, do, lse_rep, cq_rep, bk, "attn_stats")
    kt = kk.reshape(t // bk, bk, n_heads, HEAD_DIM).transpose(2, 0, 3, 1)
    dq_att, dc_q, dk, dv, dc_k = run(_attn_bwd, q, kk, vv, kt, do, ck_rep, cl, delta, "attn_bwd",
                                     stages=lambda: [chip_exchange(ffn2_1 + ["wo.0"])])
    finish(ffn2_1 + ["wo.0"])
    dc = (dc_q + dc_k).reshape(n_heads, t)
    (dwq,) = run(_tn_matmul, h4b, "shared", [dq_att], ["shared"], 1, 1.0, "sq_dw",
                 stages=lambda: [pair_share(ffn2_1 + ["wo.0"])])
    buf["G_wq.0"] = dwq.reshape(N_CHIPS, dq, d)
    adam(ffn2_1 + ["wo.0"])
    dh4 = run(_nt_sum, [(dq_att, "cols", w_q, None)], dr5, ALPHA, 1, f32, "sq_dx_res",
              stages=lambda: [pair_exchange(["wq.0"])])
    prepare(["wq.0"])
    dr4, dr4b, dg10, db10 = _ln_bwd(dh4, r4, gains[1, 0], "ln_bwd")
    dh3a = ffn_bwd(dr4, dr4b, h3b, a3, b3, s3, "1", 1,
                   on_act=lambda: [chip_exchange(["wq.0"])],
                   on_dx=lambda: [pair_exchange(ffn1_1)])
    prepare(ffn1_1)
    finish(["wq.0"])
    dc_t = jnp.pad(dc.T, ((0, 0), (0, LANES - n_heads)))
    dlogf = _row_scan(dc_t[::-1], "rev_cumsum")[::-1]
    dfl, dfb_cols = _fgate_bwd(dlogf, flog, fb_pad, "gate_bwd")
    dwk, dwv, dwf = run(_tn_matmul, h3b, "shared", [dk, dv, dfl], ["shared"] * 3, 1, 1.0, "kv_dw",
                        stages=lambda: [chip_exchange(ffn1_1), pair_share(["wq.0"])])
    adam(["wq.0"])

    def by_chip(full):
        rows = full.shape[0]
        return full.reshape(rows, N_CHIPS, full.shape[1] // N_CHIPS).transpose(1, 0, 2)

    buf["G_kv.0"] = by_chip(jnp.concatenate([dwk[0], dwv[0], dwf[0][:, :n_heads]], axis=1))
    dh3 = run(_nt_sum, [(dk, "cols", w_k, None), (dv, "cols", w_v, None), (dfl, "cols", w_f, None)],
              dh3a, 1.0, 1, f32, "kv_dx", stages=lambda: [pair_exchange(["kv.0"])])
    prepare(["kv.0"])
    finish(ffn1_1)
    dr3, dr3b, dg02, db02 = _ln_bwd(dh3, r3, gains[0, 2], "ln_bwd")
    dh2 = ffn_bwd(dr3, dr3b, h2b, a2, b2, s2, "2", 0,
                  on_act=lambda: [chip_exchange(["kv.0"]), pair_share(ffn1_1)],
                  after_act=lambda: (adam(ffn1_1), finish(["kv.0"])),
                  on_dwd=lambda: [pair_share(["kv.0"])],
                  after_dwd=lambda: adam(["kv.0"]),
                  on_dx=lambda: [pair_exchange(ffn2_0)])
    prepare(ffn2_0)
    dr2, dr2b, dg01, db01 = _ln_bwd(dh2, r2, gains[0, 1], "ln_bwd")
    (dwout,) = _tn_matmul(z, "shared", [dr2b], ["shared"], 1, 1.0, "sq_dw")
    buf["G_wout.0"] = dwout.reshape(N_CHIPS, dq, d)
    dz = _nt_sum([(dr2b, "cols", w_out[0], None)], None, 1.0, 1, f32, "sq_dx_f32")
    dp, dconv_w = run(_conv_bwd, dz, p, conv_w_full, "conv_bwd",
                      stages=lambda: [chip_exchange(ffn2_0)])
    (buf["G_win.0"],) = _tn_matmul(h1b, "shared", [dp], ["cols"], N_CHIPS, 1.0, "conv_dwin")
    finish(ffn2_0)
    dh1 = run(_nt_sum, [(dp, "cols", w_in, None)], dr2, ALPHA, N_CHIPS, f32, "conv_dx",
              stages=lambda: [pair_exchange(conv_items), pair_share(ffn2_0)])
    prepare(conv_items)
    adam(ffn2_0)
    dr1, dr1b, dg00, db00 = _ln_bwd(dh1, r1, gains[0, 0], "ln_bwd")
    dh0 = ffn_bwd(dr1, dr1b, h0b, a1, b1, s1, "1", 0,
                  on_act=lambda: [chip_exchange(conv_items)],
                  after_act=lambda: finish(conv_items),
                  on_dwd=lambda: [pair_share(conv_items)],
                  after_dwd=lambda: adam(conv_items),
                  on_dx=lambda: [pair_exchange(ffn1_0)])
    prepare(ffn1_0)
    grad_x = dh0[PAD + N_META:][None]
    dmeta = dh0[PAD:PAD + N_META]
    buf["G_small.0"] = by_chip(jnp.concatenate(
        [dmeta, rows8(jnp.concatenate([dg00, dg01, dg02, dg10, dg11, dg12], axis=0)),
         rows8(jnp.concatenate([db00, db01, db02, db10, db11, db12], axis=0)),
         rows8(dconv_w), jnp.zeros((8, d), f32)], axis=0))
    run(None, stages=lambda: [chip_exchange(ffn1_0), pair_exchange(["small.0"])], name="grad_tail_1")
    prepare(["small.0"])
    finish(ffn1_0)
    run(None, stages=lambda: [chip_exchange(["small.0"]), pair_share(ffn1_0)], name="grad_tail_2")
    finish(["small.0"])
    adam(ffn1_0)
    run(None, stages=lambda: [pair_share(["small.0"])], name="grad_tail_3")

    tail = jnp.zeros((LANES,), f32).at[:n_heads].set(dfb_cols[0, :n_heads]).at[n_heads].set(loss_part)
    tail = lax.psum(tail, ("x", "y", "c"))
    loss = tail[n_heads]
    g_fb = tail[:n_heads]
    g_small = jnp.concatenate([buf["RED_small"][0, :40],
                               rows8(jnp.pad(g_fb, (0, dq - n_heads))[None])], axis=0)
    adam(["small.0"], grads={"small": g_small[None]})

    def unpack(pk):
        return (pk[:16], pk[16:22].reshape(DEPTH, 3, dq), pk[24:30].reshape(DEPTH, 3, dq),
                pk[32:35].reshape(1, 3, dq), pk[40, :n_heads])

    def order(pick):
        mt, g, b, cw, fb = unpack(pick("small")[0])
        big = {n: pick(n).reshape(w3[n].shape[-3:] if n != "kv" else kv_w.shape) for n in w3
               if n != "small"}
        return [mt, big["wg1"], big["wu1"], big["wd1"], big["wg2"], big["wu2"], big["wd2"], g, b,
                big["win"], cw, big["wout"], big["kv"], fb, big["wq"], big["wo"]]

    grads_out = dict({n: buf["RED_" + n] for n in w3}, small=g_small[None])
    return (loss, grad_x, *order(lambda n: grads_out[n]), *order(lambda n: stepped[n][0]),
            *order(lambda n: stepped[n][1]), *order(lambda n: stepped[n][2]))
```

```python
import functools
import math

import jax
import jax.numpy as jnp
from jax import lax
from jax.experimental import pallas as pl
from jax.experimental.pallas import tpu as pltpu

f32 = jnp.float32
bf16 = jnp.bfloat16

N_META = 16
PAD = 112
HEAD_DIM = 128
DEPTH = 2
LN_EPS = 1e-5
ALPHA = (2 * DEPTH) ** 0.25
NEG_INF = -1e30
N_CHIPS = 4
SMALL_ROWS = 48
LANES = 128

ADAM_LR = 0.001
ADAM_B1 = 0.9
ADAM_B2 = 0.999
ADAM_EPS = 1e-08
ADAM_WD = 0.01
ADAM_STEP = 10

VMEM_LIMIT_BYTES = 56 * 1024 * 1024
ROWS_WIDE = 1664
ROWS_ACC = 1040
STRIP = 16
MESH = pl.DeviceIdType.MESH

NT_DIMS = (((1,), (1,)), ((), ()))
TN_DIMS = (((0,), (0,)), ((), ()))


def _tile(n, target, mult):
    best = None
    for d in range(mult, min(n, target) + 1, mult):
        if n % d == 0:
            best = d
    assert best is not None, (n, target, mult)
    return best


def _params(*sem):
    return pltpu.CompilerParams(dimension_semantics=sem, vmem_limit_bytes=VMEM_LIMIT_BYTES)


class _Place:
    def __init__(self):
        self.cx, self.cy, self.c = lax.axis_index("x"), lax.axis_index("y"), lax.axis_index("c")
        self.chips = [(1 - self.cx, self.cy), (self.cx, 1 - self.cy), (1 - self.cx, 1 - self.cy)]
        self.me = 2 * self.cx + self.cy
        self.slots = [self.me] + [2 * px + py for px, py in self.chips]
        self.sib = (self.cx, self.cy, 1 - self.c)


class _Copies:
    def __init__(self):
        self.srcs, self.bufs, self.news = {}, {}, {}
        self.plans = []
        self.out_bufs, self.out_news = {}, {}

    def add(self, plan, n_copies, srcs=None, bufs=None, news=None):
        for have, more in ((self.srcs, srcs), (self.bufs, bufs), (self.news, news)):
            for key, val in (more or {}).items():
                assert key not in have or have[key] is val, key
                have[key] = val
        self.plans.append((plan, n_copies))

    def empty(self):
        return not self.plans

    def count(self):
        return sum(n for _, n in self.plans)

    def copies(self, src_refs, buf_refs, new_refs, send, recv):
        place = _Place()
        srcs = dict(zip(self.srcs, src_refs))
        bufs = dict(zip(self.bufs, buf_refs))
        news = dict(zip(self.news, new_refs))
        out = []
        for plan, n_copies in self.plans:
            triples = plan(srcs, bufs, news, place)
            assert len(triples) == n_copies
            for src, dst, dev in triples:
                n = len(out)
                out.append(pltpu.make_async_remote_copy(
                    src_ref=src, dst_ref=dst, send_sem=send.at[n], recv_sem=recv.at[n],
                    device_id=dev, device_id_type=MESH))
        return out

    def land(self, results):
        n_b = len(self.bufs)
        self.out_bufs = dict(zip(self.bufs, results[:n_b]))
        self.out_news = dict(zip(self.news, results[n_b:]))


def _pallas(comm, body, *, name, grid, in_specs, out_specs, out_shape, compiler_params,
            scratch_shapes=(), input_output_aliases=None):
    aliases = dict(input_output_aliases or {})
    if comm is None or comm.empty():
        return pl.pallas_call(body, name=name, grid=grid, in_specs=in_specs, out_specs=out_specs,
                              out_shape=out_shape, scratch_shapes=list(scratch_shapes),
                              input_output_aliases=aliases, compiler_params=compiler_params)
    single = not isinstance(out_shape, (list, tuple))
    out_shapes = [out_shape] if single else list(out_shape)
    out_specs_l = [out_specs] if single else list(out_specs)
    n_in, n_out, n_scr = len(in_specs), len(out_shapes), len(scratch_shapes)
    n_s, n_b, n_n = len(comm.srcs), len(comm.bufs), len(comm.news)
    n_copies = comm.count()

    def wrapped(*refs):
        ins = refs[:n_in]
        src_refs = refs[n_in:n_in + n_s]
        o0 = n_in + n_s + n_b
        outs = refs[o0:o0 + n_out]
        buf_refs = refs[o0 + n_out:o0 + n_out + n_b]
        new_refs = refs[o0 + n_out + n_b:o0 + n_out + n_b + n_n]
        rest = refs[o0 + n_out + n_b + n_n:]
        scratch, (send, recv) = rest[:n_scr], rest[n_scr:]
        ids = [pl.program_id(a) for a in range(len(grid))]
        first = functools.reduce(jnp.logical_and, [i == 0 for i in ids])
        last = functools.reduce(jnp.logical_and, [i == g - 1 for i, g in zip(ids, grid)])

        @pl.when(first)
        def _():
            for cp in comm.copies(src_refs, buf_refs, new_refs, send, recv):
                cp.start()

        body(*ins, *outs, *scratch)

        @pl.when(last)
        def _():
            for cp in comm.copies(src_refs, buf_refs, new_refs, send, recv):
                cp.wait()

    hbm = pl.BlockSpec(memory_space=pl.ANY)
    for j in range(n_b):
        aliases[n_in + n_s + j] = n_out + j
    call = pl.pallas_call(
        wrapped, name=name, grid=grid,
        in_specs=[*in_specs, *([hbm] * (n_s + n_b))],
        out_specs=[*out_specs_l, *([hbm] * (n_b + n_n))],
        out_shape=[*out_shapes,
                   *[jax.ShapeDtypeStruct(a.shape, a.dtype) for a in comm.bufs.values()],
                   *comm.news.values()],
        scratch_shapes=[*scratch_shapes, pltpu.SemaphoreType.DMA((n_copies,)),
                        pltpu.SemaphoreType.DMA((n_copies,))],
        input_output_aliases=aliases, compiler_params=compiler_params)

    def run(*args):
        res = call(*args, *comm.srcs.values(), *comm.bufs.values())
        comm.land(res[n_out:])
        return res[0] if single else res[:n_out]

    return run


def _copy_call(comm, name):
    n_s, n_b, n_n = len(comm.srcs), len(comm.bufs), len(comm.news)
    n_copies = comm.count()

    def body(*refs):
        src_refs = refs[:n_s]
        buf_refs = refs[n_s + n_b:n_s + 2 * n_b]
        new_refs = refs[n_s + 2 * n_b:n_s + 2 * n_b + n_n]
        send, recv = refs[n_s + 2 * n_b + n_n:]
        copies = comm.copies(src_refs, buf_refs, new_refs, send, recv)
        for cp in copies:
            cp.start()
        for cp in copies:
            cp.wait()

    hbm = pl.BlockSpec(memory_space=pl.ANY)
    res = pl.pallas_call(
        body, name=name,
        in_specs=[hbm] * (n_s + n_b), out_specs=[hbm] * (n_b + n_n),
        out_shape=[*[jax.ShapeDtypeStruct(a.shape, a.dtype) for a in comm.bufs.values()],
                   *comm.news.values()],
        input_output_aliases={n_s + j: j for j in range(n_b)},
        scratch_shapes=[pltpu.SemaphoreType.DMA((n_copies,)), pltpu.SemaphoreType.DMA((n_copies,))],
    )(*comm.srcs.values(), *comm.bufs.values())
    comm.land(res)


def _embed(meta_pad, x, name, comm=None):
    seq, d = x.shape
    t = seq + LANES

    def body(m_ref, x_ref, h_ref, hb_ref):
        first = pl.program_id(0) == 0
        v = jnp.where(first, m_ref[...], x_ref[...])
        h_ref[...] = v
        hb_ref[...] = v.astype(bf16)

    return _pallas(
        comm, body, name=name, grid=(t // LANES,),
        in_specs=[pl.BlockSpec((LANES, d), lambda i: (0, 0)),
                  pl.BlockSpec((LANES, d), lambda i: (jnp.maximum(i - 1, 0), 0))],
        out_specs=[pl.BlockSpec((LANES, d), lambda i: (i, 0)),
                   pl.BlockSpec((LANES, d), lambda i: (i, 0))],
        out_shape=[jax.ShapeDtypeStruct((t, d), f32), jax.ShapeDtypeStruct((t, d), bf16)],
        compiler_params=_params("parallel"),
    )(meta_pad, x)


def _nn_matmul(x, w, out_dtype, name, comm=None):
    t, k = x.shape
    s_n, _, n = w.shape
    assert s_n == 1 or n % LANES == 0
    tm = _tile(t, ROWS_WIDE, 16)

    def body(x_ref, w_ref, o_ref):
        o_ref[...] = jnp.dot(x_ref[...].astype(bf16), w_ref[...],
                             preferred_element_type=f32).astype(o_ref.dtype)

    return _pallas(
        comm, body, name=name, grid=(s_n, t // tm),
        in_specs=[pl.BlockSpec((tm, k), lambda s, i: (i, 0)),
                  pl.BlockSpec((None, k, n), lambda s, i: (s, 0, 0))],
        out_specs=pl.BlockSpec((tm, n), lambda s, i: (i, s)),
        out_shape=jax.ShapeDtypeStruct((t, s_n * n), out_dtype),
        compiler_params=_params("parallel", "parallel"),
    )(x, w)


def _ffn_up(hb, wg, wu, layer, name, comm=None):
    t, d = hb.shape
    s_n, _, _, n = wg.shape
    tm = _tile(t, ROWS_WIDE, 16)

    def body(x_ref, wg_ref, wu_ref, a_ref, b_ref, s_ref):
        x = x_ref[...]
        a = jnp.dot(x, wg_ref[...], preferred_element_type=f32)
        b = jnp.dot(x, wu_ref[...], preferred_element_type=f32)
        a_ref[...] = a.astype(bf16)
        b_ref[...] = b.astype(bf16)
        s_ref[...] = (a * jax.nn.sigmoid(a) * b).astype(bf16)

    wspec = pl.BlockSpec((None, None, d, n), lambda s, i: (s, layer, 0, 0))
    ospec = pl.BlockSpec((None, tm, n), lambda s, i: (s, i, 0))
    return _pallas(
        comm, body, name=name, grid=(s_n, t // tm),
        in_specs=[pl.BlockSpec((tm, d), lambda s, i: (i, 0)), wspec, wspec],
        out_specs=[ospec, ospec, ospec],
        out_shape=[jax.ShapeDtypeStruct((s_n, t, n), bf16)] * 3,
        compiler_params=_params("parallel", "parallel"),
    )(hb, wg, wu)


def _down_ln(x, w, layer, hprev, gain, bias, beta, name, comm=None):
    s_n, t, k = x.shape
    d = w.shape[-1]
    tm = _tile(t, ROWS_ACC, 16)

    def body(x_ref, w_ref, h_ref, g_ref, b_ref, r_out, h_out, hb_out, acc):
        s = pl.program_id(1)

        @pl.when(s == 0)
        def _():
            acc[...] = jnp.zeros_like(acc)

        acc[...] += jnp.dot(x_ref[...], w_ref[...], preferred_element_type=f32)

        @pl.when(s == s_n - 1)
        def _():
            r = ALPHA * h_ref[...] + beta * acc[...]
            mu = jnp.mean(r, axis=-1, keepdims=True)
            xc = r - mu
            var = jnp.mean(xc * xc, axis=-1, keepdims=True)
            y = xc * lax.rsqrt(var + LN_EPS) * g_ref[...] + b_ref[...]
            r_out[...] = r
            h_out[...] = y
            hb_out[...] = y.astype(bf16)

    row = pl.BlockSpec((tm, d), lambda i, s: (i, 0))
    vec = pl.BlockSpec((1, d), lambda i, s: (0, 0))
    return _pallas(
        comm, body, name=name, grid=(t // tm, s_n),
        in_specs=[pl.BlockSpec((None, tm, k), lambda i, s: (s, i, 0)),
                  pl.BlockSpec((None, None, k, d), lambda i, s: (s, layer, 0, 0)),
                  row, vec, vec],
        out_specs=[row, row, row],
        out_shape=[jax.ShapeDtypeStruct((t, d), f32), jax.ShapeDtypeStruct((t, d), f32),
                   jax.ShapeDtypeStruct((t, d), bf16)],
        scratch_shapes=[pltpu.VMEM((tm, d), f32)],
        compiler_params=_params("parallel", "arbitrary"),
    )(x, w, hprev, gain, bias)


def _conv_fwd(p, conv_w, name):
    t, d3 = p.shape
    d = d3 // 3
    tm = _tile(t, 320, 8)
    hb = tm // 8

    def body(p_ref, prev_ref, w_ref, z_ref):
        i = pl.program_id(0)
        rows = i * tm - 8 + lax.broadcasted_iota(jnp.int32, (tm + 8, 1), 0)
        cg = jnp.concatenate([prev_ref[:, d:2 * d], p_ref[:, d:2 * d]], axis=0)
        val = jnp.concatenate([prev_ref[:, 2 * d:], p_ref[:, 2 * d:]], axis=0)
        u = jnp.where(rows >= PAD, cg * val, 0.0)
        y = (w_ref[2:3, :] * u + w_ref[1:2, :] * pltpu.roll(u, 1, 0)
             + w_ref[0:1, :] * pltpu.roll(u, 2, 0))
        z_ref[...] = (p_ref[:, :d] * y[8:]).astype(bf16)

    return pl.pallas_call(
        body, name=name, grid=(t // tm,),
        in_specs=[pl.BlockSpec((tm, d3), lambda i: (i, 0)),
                  pl.BlockSpec((8, d3), lambda i: (jnp.maximum(i * hb - 1, 0), 0)),
                  pl.BlockSpec((3, d), lambda i: (0, 0))],
        out_specs=pl.BlockSpec((tm, d), lambda i: (i, 0)),
        out_shape=jax.ShapeDtypeStruct((t, d), bf16),
        compiler_params=_params("parallel"),
    )(p, p, conv_w)


def _row_scan(x, name, fbias=None):
    t, n = x.shape
    blk = LANES
    gate = fbias is not None

    def body(*refs):
        if gate:
            x_ref, fb_ref, o_ref, carry = refs
        else:
            x_ref, o_ref, carry = refs
        i = pl.program_id(0)

        @pl.when(i == 0)
        def _():
            carry[...] = jnp.zeros_like(carry)

        v = x_ref[...]
        r = lax.broadcasted_iota(jnp.int32, (blk, n), 0)
        if gate:
            v = v + fb_ref[...]
            v = jnp.minimum(v, 0.0) - jnp.log1p(jnp.exp(-jnp.abs(v)))
            v = jnp.where(i * blk + r >= PAD, v, 0.0)
        sh = 1
        while sh < blk:
            v = v + jnp.where(r >= sh, pltpu.roll(v, sh, 0), 0.0)
            sh *= 2
        v = v + carry[...]
        o_ref[...] = v
        carry[...] = o_ref[blk - 1:blk, :]

    in_specs = [pl.BlockSpec((blk, n), lambda i: (i, 0))]
    args = [x]
    if gate:
        in_specs.append(pl.BlockSpec((1, n), lambda i: (0, 0)))
        args.append(fbias)
    return pl.pallas_call(
        body, name=name, grid=(t // blk,),
        in_specs=in_specs,
        out_specs=pl.BlockSpec((blk, n), lambda i: (i, 0)),
        out_shape=jax.ShapeDtypeStruct((t, n), f32),
        scratch_shapes=[pltpu.VMEM((1, n), f32)],
        compiler_params=_params("arbitrary"),
    )(*args)


def _lanes(x, n):
    return jnp.concatenate([x] * (n // LANES), axis=1)


def _attn_fwd(q, k, v, cq_rep, ck_rows, name):
    t, d = q.shape
    n_heads = d // HEAD_DIM
    bk = ck_rows.shape[-1]
    bq = bk
    scale = 1.0 / math.sqrt(HEAD_DIM)

    def lane_fold(x, op):
        out = x[:, :LANES]
        for c0 in range(LANES, bk, LANES):
            out = op(out, x[:, c0:c0 + LANES])
        return out

    def body(q_ref, k_ref, v_ref, cq_ref, ck_ref, o_ref, o32_ref, lse_ref,
             s_scr, p_scr, m_scr, l_scr, red_scr, acc_scr):
        i = pl.program_id(1)
        m_scr[...] = jnp.full_like(m_scr, NEG_INF)
        l_scr[...] = jnp.zeros_like(l_scr)
        acc_scr[...] = jnp.zeros_like(acc_scr)
        qb = q_ref[...]
        ahead = (lax.broadcasted_iota(jnp.int32, (STRIP, bk), 1)
                 - lax.broadcasted_iota(jnp.int32, (STRIP, bk), 0))

        def tile(j, diagonal):
            k0 = pl.multiple_of(j * bk, bk)
            s_scr[...] = lax.dot_general(qb, k_ref[pl.ds(k0, bk), :], NT_DIMS,
                                         preferred_element_type=f32)
            ck = ck_ref[j]
            for r in range(0, bq, STRIP):
                rows = slice(r, r + STRIP)
                s = s_scr[rows, :] * scale + _lanes(cq_ref[rows, :], bk) - ck
                if diagonal:
                    s = jnp.where(ahead <= r, s, NEG_INF)
                s_scr[rows, :] = s
                red_scr[rows, :] = lane_fold(s, jnp.maximum)
            m_old = m_scr[...]
            m_new = jnp.maximum(m_old, jnp.broadcast_to(
                jnp.max(red_scr[...], axis=1, keepdims=True), (bq, LANES)))
            a = jnp.exp(m_old - m_new)
            m_scr[...] = m_new
            for r in range(0, bq, STRIP):
                rows = slice(r, r + STRIP)
                pr = jnp.exp(s_scr[rows, :] - _lanes(m_scr[rows, :], bk))
                red_scr[rows, :] = lane_fold(pr, jnp.add)
                p_scr[rows, :] = pr.astype(bf16)
            l_scr[...] = a * l_scr[...] + jnp.broadcast_to(
                jnp.sum(red_scr[...], axis=1, keepdims=True), (bq, LANES))
            acc_scr[...] = a * acc_scr[...] + jnp.dot(
                p_scr[...], v_ref[pl.ds(k0, bk), :], preferred_element_type=f32)

        def full_tile(j, carry):
            tile(j, False)
            return carry

        lax.fori_loop(0, i, full_tile, 0)
        tile(i, True)
        out = acc_scr[...] / l_scr[...]
        o_ref[...] = out.astype(bf16)
        o32_ref[...] = out
        lse_ref[...] = m_scr[...] + jnp.log(l_scr[...])

    qblk = pl.BlockSpec((bq, HEAD_DIM), lambda h, i: (i, h))
    head_rows = pl.BlockSpec((t, HEAD_DIM), lambda h, i: (0, h))
    rep = pl.BlockSpec((None, bq, LANES), lambda h, i: (h, i, 0))
    col = pltpu.VMEM((bq, LANES), f32)
    return pl.pallas_call(
        body, name=name, grid=(n_heads, t // bq),
        in_specs=[qblk, head_rows, head_rows, rep,
                  pl.BlockSpec((None, t // bk, 1, bk), lambda h, i: (h, 0, 0, 0))],
        out_specs=[qblk, qblk, rep],
        out_shape=[jax.ShapeDtypeStruct((t, d), bf16), jax.ShapeDtypeStruct((t, d), f32),
                   jax.ShapeDtypeStruct((n_heads, t, LANES), f32)],
        scratch_shapes=[pltpu.VMEM((bq, bk), f32), pltpu.VMEM((bq, bk), bf16), col, col, col,
                        pltpu.VMEM((bq, HEAD_DIM), f32)],
        compiler_params=_params("parallel", "parallel"),
    )(q, k, v, cq_rep, ck_rows)


def _loss_head(h, target, name):
    t, d = h.shape

    def body(h_ref, t_ref, dy_ref, loss_ref):
        i = pl.program_id(0)

        @pl.when(i == 0)
        def _():
            loss_ref[...] = jnp.zeros_like(loss_ref)

        diff = jnp.where(i >= 1, h_ref[...] - t_ref[...], 0.0)
        dy_ref[...] = diff * (1.0 / d)
        loss_ref[...] += jnp.sum(diff * diff)

    return pl.pallas_call(
        body, name=name, grid=(t // LANES,),
        in_specs=[pl.BlockSpec((LANES, d), lambda i: (i, 0)),
                  pl.BlockSpec((LANES, d), lambda i: (jnp.maximum(i - 1, 0), 0))],
        out_specs=[pl.BlockSpec((LANES, d), lambda i: (i, 0)),
                   pl.BlockSpec((1, LANES), lambda i: (0, 0))],
        out_shape=[jax.ShapeDtypeStruct((t, d), f32), jax.ShapeDtypeStruct((1, LANES), f32)],
        compiler_params=_params("arbitrary"),
    )(h, target)


def _ln_bwd(dh, r, gain, name):
    t, d = r.shape
    tm = _tile(t, 640, 16)

    def body(dh_ref, r_ref, g_ref, dr_ref, drb_ref, dg_ref, db_ref):
        @pl.when(pl.program_id(0) == 0)
        def _():
            dg_ref[...] = jnp.zeros_like(dg_ref)
            db_ref[...] = jnp.zeros_like(db_ref)

        rr = r_ref[...]
        mu = jnp.mean(rr, axis=-1, keepdims=True)
        xc = rr - mu
        var = jnp.mean(xc * xc, axis=-1, keepdims=True)
        rstd = lax.rsqrt(var + LN_EPS)
        xhat = xc * rstd
        dy = dh_ref[...]
        dxh = dy * g_ref[...]
        m1 = jnp.mean(dxh, axis=-1, keepdims=True)
        m2 = jnp.mean(dxh * xhat, axis=-1, keepdims=True)
        dr = rstd * (dxh - m1 - xhat * m2)
        dr_ref[...] = dr
        drb_ref[...] = dr.astype(bf16)
        dg_ref[...] += jnp.sum(dy * xhat, axis=0, keepdims=True)
        db_ref[...] += jnp.sum(dy, axis=0, keepdims=True)

    row = pl.BlockSpec((tm, d), lambda i: (i, 0))
    vec = pl.BlockSpec((1, d), lambda i: (0, 0))
    return pl.pallas_call(
        body, name=name, grid=(t // tm,),
        in_specs=[row, row, vec],
        out_specs=[row, row, vec, vec],
        out_shape=[jax.ShapeDtypeStruct((t, d), f32), jax.ShapeDtypeStruct((t, d), bf16),
                   jax.ShapeDtypeStruct((1, d), f32), jax.ShapeDtypeStruct((1, d), f32)],
        compiler_params=_params("arbitrary"),
    )(dh, r, gain)


def _ffn_bwd_act(drb, wd, layer, a, b, name, comm=None):
    t, d = drb.shape
    s_n, _, n = a.shape
    tm = _tile(t, ROWS_WIDE, 16)

    def body(dr_ref, w_ref, a_ref, b_ref, da_ref, db_ref):
        ds = 0.5 * lax.dot_general(dr_ref[...], w_ref[...], NT_DIMS, preferred_element_type=f32)
        av = a_ref[...].astype(f32)
        sig = jax.nn.sigmoid(av)
        da_ref[...] = (ds * b_ref[...].astype(f32) * (sig * (1.0 + av * (1.0 - sig)))).astype(bf16)
        db_ref[...] = (ds * (av * sig)).astype(bf16)

    act = pl.BlockSpec((None, tm, n), lambda s, i: (s, i, 0))
    return _pallas(
        comm, body, name=name, grid=(s_n, t // tm),
        in_specs=[pl.BlockSpec((tm, d), lambda s, i: (i, 0)),
                  pl.BlockSpec((None, None, n, d), lambda s, i: (s, layer, 0, 0)), act, act],
        out_specs=[act, act],
        out_shape=[jax.ShapeDtypeStruct((s_n, t, n), bf16), jax.ShapeDtypeStruct((s_n, t, n), bf16)],
        compiler_params=_params("parallel", "parallel"),
    )(drb, wd, a, b)


def _act_spec(mode, tt, k, t_first):
    def fix(fn):
        return (lambda i, s: fn(s, i)) if t_first else fn
    if mode == "shared":
        return pl.BlockSpec((tt, k), fix(lambda s, i: (i, 0)))
    if mode == "cols":
        return pl.BlockSpec((tt, k), fix(lambda s, i: (i, s)))
    assert mode == "stack"
    return pl.BlockSpec((None, tt, k), fix(lambda s, i: (s, i, 0)))


def _act_width(arr, mode, s_n):
    return arr.shape[-1] // s_n if mode == "cols" else arr.shape[-1]


def _tn_matmul(x, xmode, ys, ymodes, s_n, scale, name, comm=None):
    t = x.shape[-2]
    kx = _act_width(x, xmode, s_n)
    kys = [_act_width(y, m, s_n) for y, m in zip(ys, ymodes)]
    tt = _tile(t, 2080, 16)
    n_t = t // tt
    n_y = len(ys)

    def body(*refs):
        x_ref = refs[0]
        y_refs = refs[1:1 + n_y]
        o_refs = refs[1 + n_y:]
        i = pl.program_id(1)
        xv = x_ref[...].astype(bf16)
        for y_ref, o_ref in zip(y_refs, o_refs):
            part = lax.dot_general(xv, y_ref[...].astype(bf16), TN_DIMS, preferred_element_type=f32)

            @pl.when(i == 0)
            def _():
                o_ref[...] = part

            @pl.when(i > 0)
            def _():
                o_ref[...] += part

            if scale != 1.0:
                @pl.when(i == n_t - 1)
                def _():
                    o_ref[...] = o_ref[...] * scale

    in_specs = [_act_spec(xmode, tt, kx, False)]
    in_specs += [_act_spec(m, tt, ky, False) for m, ky in zip(ymodes, kys)]
    return _pallas(
        comm, body, name=name, grid=(s_n, n_t),
        in_specs=in_specs,
        out_specs=[pl.BlockSpec((None, kx, ky), lambda s, i: (s, 0, 0)) for ky in kys],
        out_shape=[jax.ShapeDtypeStruct((s_n, kx, ky), f32) for ky in kys],
        compiler_params=_params("parallel", "arbitrary"),
    )(x, *ys)


def _nt_sum(pairs, base, base_scale, s_n, out_dtype, name, comm=None):
    t = pairs[0][0].shape[-2]
    d = pairs[0][2].shape[-2]
    tm = _tile(t, ROWS_ACC, 16)
    n_p = len(pairs)
    has_base = base is not None

    def body(*refs):
        dy_refs = refs[0:2 * n_p:2]
        w_refs = refs[1:2 * n_p:2]
        rest = refs[2 * n_p:]
        base_ref = rest[0] if has_base else None
        o_ref, acc = rest[-2], rest[-1]
        s = pl.program_id(1)

        @pl.when(s == 0)
        def _():
            acc[...] = jnp.zeros_like(acc)

        tot = None
        for dy_ref, w_ref in zip(dy_refs, w_refs):
            part = lax.dot_general(dy_ref[...].astype(bf16), w_ref[...], NT_DIMS,
                                   preferred_element_type=f32)
            tot = part if tot is None else tot + part
        acc[...] += tot

        @pl.when(s == s_n - 1)
        def _():
            res = acc[...]
            if has_base:
                res = base_scale * base_ref[...] + res
            o_ref[...] = res.astype(o_ref.dtype)

    in_specs, args = [], []
    for dy, mode, w, layer in pairs:
        k = _act_width(dy, mode, s_n)
        in_specs.append(_act_spec(mode, tm, k, True))
        if layer is None:
            in_specs.append(pl.BlockSpec((None, d, k), lambda i, s: (s, 0, 0)))
        else:
            in_specs.append(pl.BlockSpec((None, None, d, k),
                                         functools.partial(lambda i, s, l: (s, l, 0, 0), l=layer)))
        args += [dy, w]
    row = pl.BlockSpec((tm, d), lambda i, s: (i, 0))
    if has_base:
        in_specs.append(row)
        args.append(base)
    return _pallas(
        comm, body, name=name, grid=(t // tm, s_n),
        in_specs=in_specs, out_specs=row,
        out_shape=jax.ShapeDtypeStruct((t, d), out_dtype),
        scratch_shapes=[pltpu.VMEM((tm, d), f32)],
        compiler_params=_params("parallel", "arbitrary"),
    )(*args)


def _conv_bwd(dz, p, conv_w, name, comm=None):
    t, d3 = p.shape
    d = d3 // 3
    tm = _tile(t, 320, 8)
    hb = tm // 8
    last8 = t // 8 - 1
    n_ext = tm + 8

    def body(dz_ref, dzn_ref, p_ref, pp_ref, pn_ref, w_ref, dp_ref, dw_ref):
        i = pl.program_id(0)

        @pl.when(i == 0)
        def _():
            dw_ref[...] = jnp.zeros_like(dw_ref)

        w0, w1, w2 = w_ref[0:1, :], w_ref[1:2, :], w_ref[2:3, :]
        rows_u = i * tm - 8 + lax.broadcasted_iota(jnp.int32, (n_ext, 1), 0)
        cg = jnp.concatenate([pp_ref[:, d:2 * d], p_ref[:, d:2 * d]], axis=0)
        val = jnp.concatenate([pp_ref[:, 2 * d:], p_ref[:, 2 * d:]], axis=0)
        u = jnp.where(rows_u >= PAD, cg * val, 0.0)
        u1 = pltpu.roll(u, 1, 0)
        u2 = pltpu.roll(u, 2, 0)
        y = (w2 * u + w1 * u1 + w0 * u2)[8:]
        dzv = dz_ref[...]
        bg = p_ref[:, :d]
        rows_n = (i + 1) * tm + lax.broadcasted_iota(jnp.int32, (8, 1), 0)
        dy_main = dzv * bg
        dy_next = jnp.where(rows_n < t, dzn_ref[...] * pn_ref[:, :d], 0.0)
        dye = jnp.concatenate([dy_main, dy_next], axis=0)
        du = (w2 * dye + w1 * pltpu.roll(dye, n_ext - 1, 0)
              + w0 * pltpu.roll(dye, n_ext - 2, 0))[:tm]
        du = jnp.where(rows_u[8:] >= PAD, du, 0.0)
        dp_ref[:, :d] = (dzv * y).astype(bf16)
        dp_ref[:, d:2 * d] = (du * val[8:]).astype(bf16)
        dp_ref[:, 2 * d:] = (du * cg[8:]).astype(bf16)
        dw_ref[0:1, :] += jnp.sum(dy_main * u2[8:], axis=0, keepdims=True)
        dw_ref[1:2, :] += jnp.sum(dy_main * u1[8:], axis=0, keepdims=True)
        dw_ref[2:3, :] += jnp.sum(dy_main * u[8:], axis=0, keepdims=True)

    nxt = lambda i: (jnp.minimum((i + 1) * hb, last8), 0)
    return _pallas(
        comm, body, name=name, grid=(t // tm,),
        in_specs=[pl.BlockSpec((tm, d), lambda i: (i, 0)),
                  pl.BlockSpec((8, d), nxt),
                  pl.BlockSpec((tm, d3), lambda i: (i, 0)),
                  pl.BlockSpec((8, d3), lambda i: (jnp.maximum(i * hb - 1, 0), 0)),
                  pl.BlockSpec((8, d3), nxt),
                  pl.BlockSpec((3, d), lambda i: (0, 0))],
        out_specs=[pl.BlockSpec((tm, d3), lambda i: (i, 0)),
                   pl.BlockSpec((3, d), lambda i: (0, 0))],
        out_shape=[jax.ShapeDtypeStruct((t, d3), bf16), jax.ShapeDtypeStruct((3, d), f32)],
        compiler_params=_params("arbitrary"),
    )(dz, dz, p, p, p, conv_w)


def _attn_stats(o, do, lse_rep, cq_rep, bq, name):
    t, d = o.shape
    n_heads = d // HEAD_DIM

    def body(o_ref, do_ref, lse_ref, cq_ref, cl_ref, delta_ref):
        for c0 in range(0, bq, LANES):
            rows = slice(c0, c0 + LANES)
            cl_ref[:, rows] = (cq_ref[rows, :] - lse_ref[rows, :]).T[0:1, :]
            prod = o_ref[rows, :] * do_ref[rows, :].astype(f32)
            delta_ref[:, rows] = jnp.sum(prod.T, axis=0, keepdims=True)

    qblk = pl.BlockSpec((bq, HEAD_DIM), lambda h, i: (i, h))
    rep = pl.BlockSpec((None, bq, LANES), lambda h, i: (h, i, 0))
    row = pl.BlockSpec((None, None, 1, bq), lambda h, i: (h, i, 0, 0))
    shp = jax.ShapeDtypeStruct((n_heads, t // bq, 1, bq), f32)
    return pl.pallas_call(
        body, name=name, grid=(n_heads, t // bq),
        in_specs=[qblk, qblk, rep, rep],
        out_specs=[row, row], out_shape=[shp, shp],
        compiler_params=_params("parallel", "parallel"),
    )(o, do, lse_rep, cq_rep)


def _attn_bwd(q, k, v, kt, do, ckey, cl_rows, delta_rows, name, comm=None):
    t, d = q.shape
    n_heads = d // HEAD_DIM
    bk = kt.shape[-1]
    bq = bk
    n_kv = t // bk
    n_q = t // bq
    scale = 1.0 / math.sqrt(HEAD_DIM)

    def body(q_ref, do_ref, cl_ref, dl_ref, k_ref, v_ref, kt_ref, ck_ref,
             dq_ref, dcq_ref, dk_ref, dv_ref, dck_ref,
             st_scr, dp_scr, p_scr, ds_scr, dqt, dk_acc, dv_acc, dck_acc):
        j = pl.program_id(1)

        @pl.when(j == 0)
        def _():
            dqt[...] = jnp.zeros_like(dqt)
            dcq_ref[...] = jnp.zeros_like(dcq_ref)

        dk_acc[...] = jnp.zeros_like(dk_acc)
        dv_acc[...] = jnp.zeros_like(dv_acc)
        dck_acc[...] = jnp.zeros_like(dck_acc)
        kb = k_ref[...]
        vb = v_ref[...]
        behind = (lax.broadcasted_iota(jnp.int32, (STRIP, bq), 1)
                  - lax.broadcasted_iota(jnp.int32, (STRIP, bq), 0))

        def tile(i, diagonal):
            r0 = pl.multiple_of(i * bq, bq)
            qi = q_ref[pl.ds(r0, bq), :]
            doi = do_ref[pl.ds(r0, bq), :]
            st_scr[...] = lax.dot_general(kb, qi, NT_DIMS, preferred_element_type=f32)
            dp_scr[...] = lax.dot_general(vb, doi, NT_DIMS, preferred_element_type=f32)
            cl = cl_ref[i]
            dl = dl_ref[i]
            over_keys = jnp.zeros((STRIP, bq), f32)
            for r in range(0, bk, STRIP):
                keys = slice(r, r + STRIP)
                st = st_scr[keys, :] * scale + cl - _lanes(ck_ref[keys, :], bq)
                if diagonal:
                    st = jnp.where(behind >= r, st, NEG_INF)
                pr = jnp.exp(st)
                ds = pr * (dp_scr[keys, :] - dl)
                over_keys = over_keys + ds
                dck_acc[keys, :] -= jnp.sum(ds, axis=1, keepdims=True)
                p_scr[keys, :] = pr.astype(bf16)
                ds_scr[keys, :] = ds.astype(bf16)
            dcq_ref[i] += jnp.sum(over_keys, axis=0, keepdims=True)
            dv_acc[...] += jnp.dot(p_scr[...], doi, preferred_element_type=f32)
            dk_acc[...] += jnp.dot(ds_scr[...], qi, preferred_element_type=f32)
            dqt[i] += jnp.dot(kt_ref[...], ds_scr[...], preferred_element_type=f32)

        def full_tile(i, carry):
            tile(i, False)
            return carry

        tile(j, True)
        lax.fori_loop(j + 1, n_q, full_tile, 0)
        dk_ref[...] = (dk_acc[...] * scale).astype(bf16)
        dv_ref[...] = dv_acc[...].astype(bf16)
        for c0 in range(0, bk, LANES):
            keys = slice(c0, c0 + LANES)
            dck_ref[:, keys] = jnp.broadcast_to(dck_acc[keys, :], (LANES, LANES)).T[0:1, :]

        @pl.when(j == n_kv - 1)
        def _():
            def emit(i, carry):
                r0 = pl.multiple_of(i * bq, bq)
                dq_ref[pl.ds(r0, bq), :] = dqt[i].T * scale
                return carry
            lax.fori_loop(0, n_q, emit, 0)

    head_rows = pl.BlockSpec((t, HEAD_DIM), lambda h, j: (0, h))
    head_stat = pl.BlockSpec((None, n_q, 1, bq), lambda h, j: (h, 0, 0, 0))
    kblk = pl.BlockSpec((bk, HEAD_DIM), lambda h, j: (j, h))
    return _pallas(
        comm, body, name=name, grid=(n_heads, n_kv),
        in_specs=[head_rows, head_rows, head_stat, head_stat, kblk, kblk,
                  pl.BlockSpec((None, None, HEAD_DIM, bk), lambda h, j: (h, j, 0, 0)),
                  pl.BlockSpec((None, bk, LANES), lambda h, j: (h, j, 0))],
        out_specs=[head_rows, head_stat, kblk, kblk,
                   pl.BlockSpec((None, None, 1, bk), lambda h, j: (h, j, 0, 0))],
        out_shape=[jax.ShapeDtypeStruct((t, d), f32),
                   jax.ShapeDtypeStruct((n_heads, n_q, 1, bq), f32),
                   jax.ShapeDtypeStruct((t, d), bf16), jax.ShapeDtypeStruct((t, d), bf16),
                   jax.ShapeDtypeStruct((n_heads, n_kv, 1, bk), f32)],
        scratch_shapes=[pltpu.VMEM((bk, bq), f32), pltpu.VMEM((bk, bq), f32),
                        pltpu.VMEM((bk, bq), bf16), pltpu.VMEM((bk, bq), bf16),
                        pltpu.VMEM((n_q, HEAD_DIM, bq), f32),
                        pltpu.VMEM((bk, HEAD_DIM), f32), pltpu.VMEM((bk, HEAD_DIM), f32),
                        pltpu.VMEM((bk, 1), f32)],
        compiler_params=_params("parallel", "arbitrary"),
    )(q, do, cl_rows, delta_rows, k, v, kt, ckey)


def _fgate_bwd(dlogf, flog, fbias, name):
    t, n = flog.shape

    def body(dl_ref, fl_ref, fb_ref, o_ref, sum_ref):
        i = pl.program_id(0)

        @pl.when(i == 0)
        def _():
            sum_ref[...] = jnp.zeros_like(sum_ref)

        r = i * LANES + lax.broadcasted_iota(jnp.int32, (LANES, n), 0)
        g = dl_ref[...] * jax.nn.sigmoid(-(fl_ref[...] + fb_ref[...]))
        g = jnp.where(r >= PAD, g, 0.0)
        o_ref[...] = g
        sum_ref[...] += jnp.sum(g, axis=0, keepdims=True)

    blk = pl.BlockSpec((LANES, n), lambda i: (i, 0))
    vec = pl.BlockSpec((1, n), lambda i: (0, 0))
    return pl.pallas_call(
        body, name=name, grid=(t // LANES,),
        in_specs=[blk, blk, vec], out_specs=[blk, vec],
        out_shape=[jax.ShapeDtypeStruct((t, n), f32), jax.ShapeDtypeStruct((1, n), f32)],
        compiler_params=_params("arbitrary"),
    )(dlogf, flog, fbias)


def _place_shard(w, idx, name):
    n_l, r, c_n = w.shape
    out_dtype = bf16 if r * c_n > 2 ** 16 else w.dtype
    tr = _tile(r, 512, 16) if r % 16 == 0 else r

    def body(idx_ref, w_ref, o_ref):
        o_ref[...] = w_ref[...].astype(out_dtype)

    grid_spec = pltpu.PrefetchScalarGridSpec(
        num_scalar_prefetch=1, grid=(n_l, r // tr),
        in_specs=[pl.BlockSpec((None, tr, c_n), lambda l, i, idx: (l, i, 0))],
        out_specs=pl.BlockSpec((None, None, tr, c_n), lambda l, i, idx: (idx[0], l, i, 0)))
    return pl.pallas_call(
        body, name=name, grid_spec=grid_spec,
        out_shape=jax.ShapeDtypeStruct((N_CHIPS, n_l, r, c_n), out_dtype),
        compiler_params=_params("parallel", "parallel"),
    )(idx, w)


def _plan_gather_ici(items):
    def plan(srcs, bufs, news, p):
        out = []
        for name, layer, r2 in items:
            mine = bufs[name].at[p.me, layer, pl.ds(p.c * r2, r2)]
            out += [(mine, mine, (*chip, p.c)) for chip in p.chips]
        return out
    return plan, 3 * len(items)


def _plan_gather_d2d(items):
    def plan(srcs, bufs, news, p):
        out = []
        for name, layer, r2 in items:
            for px, py in p.chips:
                landed = bufs[name].at[2 * px + py, layer, pl.ds(p.c * r2, r2)]
                out.append((landed, landed, p.sib))
        return out
    return plan, 3 * len(items)


def _plan_pair_exchange(names, r2s):
    def plan(srcs, bufs, news, p):
        out = []
        for name, r2 in zip(names, r2s):
            for k, slot in enumerate(p.slots):
                out.append((srcs["G_" + name].at[slot, pl.ds((1 - p.c) * r2, r2)],
                            news["PAIR_" + name].at[k], p.sib))
        return out
    return plan, 4 * len(names)


def _plan_chip_exchange(names):
    def plan(srcs, bufs, news, p):
        out = []
        for name in names:
            for k, chip in enumerate(p.chips):
                out.append((srcs["SEND_" + name].at[k], news["RECV_" + name].at[k], (*chip, p.c)))
        return out
    return plan, 3 * len(names)


def _plan_pair_share(items):
    def plan(srcs, bufs, news, p):
        out = []
        for name, layer, r2 in items:
            mine = bufs["RED_" + name].at[layer, pl.ds(p.c * r2, r2)]
            out.append((mine, mine, p.sib))
        return out
    return plan, len(items)


def _rs_prepare(g, pair, idx, name):
    _, r2, c_n = pair.shape
    tr = _tile(r2, 256, 8)
    nb = r2 // tr

    def body(idx_ref, g_ref, p_ref, o_ref):
        o_ref[...] = (g_ref[...] + p_ref[...]).astype(bf16)

    grid_spec = pltpu.PrefetchScalarGridSpec(
        num_scalar_prefetch=1, grid=(3, nb),
        in_specs=[pl.BlockSpec((None, tr, c_n), lambda k, i, idx: (idx[k + 1], idx[4] * nb + i, 0)),
                  pl.BlockSpec((None, tr, c_n), lambda k, i, idx: (k + 1, i, 0))],
        out_specs=pl.BlockSpec((None, tr, c_n), lambda k, i, idx: (k, i, 0)))
    return pl.pallas_call(
        body, name=name, grid_spec=grid_spec,
        out_shape=jax.ShapeDtypeStruct((3, r2, c_n), bf16),
        compiler_params=_params("parallel", "parallel"),
    )(idx, g, pair)


def _rs_finish(g, pair, recv, idx, layer, n_layers, into, name):
    _, r2, c_n = pair.shape
    tr = _tile(r2, 256, 8)
    nb = r2 // tr

    def body(*refs):
        g_ref, p_ref, r0_ref, r1_ref, r2_ref = refs[1:6]
        o_ref = refs[-1]
        acc = g_ref[...] + p_ref[...]
        acc = acc + r0_ref[...].astype(f32)
        acc = acc + r1_ref[...].astype(f32)
        acc = acc + r2_ref[...].astype(f32)
        o_ref[...] = acc

    def rspec(k):
        return pl.BlockSpec((None, tr, c_n), functools.partial(lambda i, idx, kk: (kk, i, 0), kk=k))

    in_specs = [pl.BlockSpec((None, tr, c_n), lambda i, idx: (idx[0], idx[4] * nb + i, 0)),
                rspec(0), rspec(0), rspec(1), rspec(2)]
    args = [idx, g, pair, recv, recv, recv]
    aliases = {}
    if into is not None:
        in_specs.append(pl.BlockSpec(memory_space=pl.ANY))
        args.append(into)
        aliases = {6: 0}
    grid_spec = pltpu.PrefetchScalarGridSpec(
        num_scalar_prefetch=1, grid=(nb,), in_specs=in_specs,
        out_specs=pl.BlockSpec((None, tr, c_n), lambda i, idx: (layer, idx[4] * nb + i, 0)))
    return pl.pallas_call(
        body, name=name, grid_spec=grid_spec,
        out_shape=jax.ShapeDtypeStruct((n_layers, 2 * r2, c_n), f32),
        input_output_aliases=aliases,
        compiler_params=_params("parallel"),
    )(*args)


def _adamw(w, g, m, v, layer, into, name):
    n_l, r, c_n = w.shape
    tr = _tile(r, 256, 8)

    def body(*refs):
        w_ref, g_ref, m_ref, v_ref = refs[:4]
        d_ref, mo_ref, vo_ref = refs[-3:]
        gv = g_ref[...]
        mn = ADAM_B1 * m_ref[...] + (1.0 - ADAM_B1) * gv
        vn = ADAM_B2 * v_ref[...] + (1.0 - ADAM_B2) * (gv * gv)
        m_hat = mn / (1.0 - ADAM_B1 ** ADAM_STEP)
        v_hat = vn / (1.0 - ADAM_B2 ** ADAM_STEP)
        d_ref[...] = -ADAM_LR * (m_hat / (jnp.sqrt(v_hat) + ADAM_EPS) + ADAM_WD * w_ref[...])
        mo_ref[...] = mn
        vo_ref[...] = vn

    blk = pl.BlockSpec((None, tr, c_n), lambda i: (layer, i, 0))
    shp = jax.ShapeDtypeStruct((n_l, r, c_n), f32)
    in_specs = [blk] * 4
    args = [w, g, m, v]
    aliases = {}
    if into is not None:
        in_specs = in_specs + [pl.BlockSpec(memory_space=pl.ANY)] * 3
        args += list(into)
        aliases = {4: 0, 5: 1, 6: 2}
    return pl.pallas_call(
        body, name=name, grid=(r // tr,),
        in_specs=in_specs, out_specs=[blk] * 3, out_shape=[shp] * 3,
        input_output_aliases=aliases,
        compiler_params=_params("parallel"),
    )(*args)


def kernel(x, meta, ffn1_wg, ffn1_wu, ffn1_wd, ffn2_wg, ffn2_wu, ffn2_wd, ln_gain, ln_bias, conv_w_in, conv_w, conv_w_out, kv_w, f_bias, attn_w_q, attn_w_o, loss_target, m_meta, m_ffn1_wg, m_ffn1_wu, m_ffn1_wd, m_ffn2_wg, m_ffn2_wu, m_ffn2_wd, m_ln_gain, m_ln_bias, m_conv_w_in, m_conv_w, m_conv_w_out, m_kv_w, m_f_bias, m_attn_w_q, m_attn_w_o, v_meta, v_ffn1_wg, v_ffn1_wu, v_ffn1_wd, v_ffn2_wg, v_ffn2_wu, v_ffn2_wd, v_ln_gain, v_ln_bias, v_conv_w_in, v_conv_w, v_conv_w_out, v_kv_w, v_f_bias, v_attn_w_q, v_attn_w_o):
    seq, d = x.shape[1], x.shape[2]
    t = PAD + N_META + seq
    n_heads = d // HEAD_DIM
    dq = d // N_CHIPS
    n_kv = kv_w.shape[1]
    x2 = x[0]
    target = loss_target[0]

    def rows8(a):
        return jnp.pad(a, ((0, 8 - a.shape[0]), (0, 0)))

    def small_pack(mt, g, b, cw, fb):
        fb_row = jnp.pad(fb, (0, mt.shape[1] - n_heads))[None]
        return jnp.concatenate([mt, rows8(g.reshape(6, -1)), rows8(b.reshape(6, -1)),
                                rows8(cw.reshape(3, -1)), rows8(fb_row)], axis=0)

    w_small = small_pack(meta, ln_gain, ln_bias, conv_w, f_bias)

    cx, cy, c = lax.axis_index("x"), lax.axis_index("y"), lax.axis_index("c")
    idx = jnp.stack([2 * cx + cy, 2 * (1 - cx) + cy, 2 * cx + (1 - cy), 2 * (1 - cx) + (1 - cy), c]
                    ).astype(jnp.int32)

    w3 = {"wg1": ffn1_wg, "wu1": ffn1_wu, "wd1": ffn1_wd, "wg2": ffn2_wg, "wu2": ffn2_wu,
          "wd2": ffn2_wd, "win": conv_w_in, "wout": conv_w_out, "kv": kv_w[None], "wq": attn_w_q,
          "wo": attn_w_o, "small": w_small[None]}
    buf = {n: _place_shard(w, idx, "place_shard") for n, w in w3.items()}

    def split(item):
        name, layer = item.split(".")
        return name, int(layer)

    def gather_stage(planner, items):
        triples = [(n, l, buf[n].shape[2] // 2) for n, l in map(split, items)]
        plan, n_copies = planner(triples)
        return dict(plan=plan, n=n_copies, bufs={n: buf[n] for n, _, _ in triples})

    def ici(items):
        return gather_stage(_plan_gather_ici, items)

    def d2d(items):
        return gather_stage(_plan_gather_d2d, items)

    def pair_exchange(items):
        r2s = [buf["G_" + it].shape[1] // 2 for it in items]
        plan, n_copies = _plan_pair_exchange(items, r2s)
        news = {"PAIR_" + it: jax.ShapeDtypeStruct((N_CHIPS, r2, buf["G_" + it].shape[2]), f32)
                for it, r2 in zip(items, r2s)}
        return dict(plan=plan, n=n_copies, srcs={"G_" + it: buf["G_" + it] for it in items}, news=news)

    def chip_exchange(items):
        plan, n_copies = _plan_chip_exchange(items)
        srcs = {"SEND_" + it: buf["SEND_" + it] for it in items}
        news = {"RECV_" + it: jax.ShapeDtypeStruct(s.shape, s.dtype)
                for it, s in ((it, buf["SEND_" + it]) for it in items)}
        return dict(plan=plan, n=n_copies, srcs=srcs, news=news)

    def pair_share(items):
        triples = [(n, l, buf["RED_" + n].shape[1] // 2) for n, l in map(split, items)]
        plan, n_copies = _plan_pair_share(triples)
        return dict(plan=plan, n=n_copies, bufs={"RED_" + n: buf["RED_" + n] for n, _, _ in triples})

    def run(fn, *args, stages=(), name=None):
        comm = _Copies()
        for st in (stages() if callable(stages) else stages):
            comm.add(st["plan"], st["n"], srcs=st.get("srcs"), bufs=st.get("bufs"), news=st.get("news"))
        out = _copy_call(comm, name) if fn is None else fn(*args, comm=comm)
        buf.update(comm.out_bufs)
        buf.update(comm.out_news)
        return out

    first = ["wg1.0", "wu1.0", "wd1.0", "small.0"]
    run(None, stages=[ici(first)], name="gather_first_ici")
    run(None, stages=[d2d(first)], name="gather_first_d2d")
    small = buf["small"].reshape(N_CHIPS, SMALL_ROWS, dq).transpose(1, 0, 2).reshape(SMALL_ROWS, d)
    meta_full = small[:N_META]
    gains = small[16:22].reshape(DEPTH, 3, 1, d)
    biases = small[24:30].reshape(DEPTH, 3, 1, d)
    conv_w_full = small[32:35]
    fb_pad = jnp.pad(f_bias, (0, LANES - n_heads))[None]
    conv_w8 = ["win.0", "wout.0"]
    ffn2_l0 = ["wg2.0", "wu2.0", "wd2.0", "kv.0"]
    attn_ffn2_l1 = ["wq.0", "wo.0", "wg2.1", "wu2.1", "wd2.1"]
    ffn1_l1 = ["wg1.1", "wu1.1", "wd1.1"]

    meta_pad = jnp.concatenate([jnp.zeros((PAD, d), f32), meta_full], axis=0)
    h0, h0b = run(_embed, meta_pad, x2, "embed", stages=[ici(conv_w8)])
    a1, b1, s1 = run(_ffn_up, h0b, buf["wg1"], buf["wu1"], 0, "ffn_up",
                     stages=lambda: [d2d(conv_w8), ici(ffn2_l0)])
    r1, h1, h1b = run(_down_ln, s1, buf["wd1"], 0, h0, gains[0, 0], biases[0, 0], 0.5, "ffn_down_ln",
                      stages=lambda: [d2d(ffn2_l0), ici(attn_ffn2_l1)])
    n_in = conv_w_in.shape[-1]
    w_in = buf["win"].reshape(N_CHIPS, d, n_in)
    w_out = buf["wout"].reshape(1, 1, d, d)
    p = run(_nn_matmul, h1b, w_in, f32, "conv_in", stages=lambda: [d2d(attn_ffn2_l1), ici(ffn1_l1)])
    z = _conv_fwd(p, conv_w_full, "conv_fwd")
    r2, h2, h2b = run(_down_ln, z[None], w_out, 0, h1, gains[0, 1], biases[0, 1], 1.0, "mix_out_ln",
                      stages=lambda: [d2d(ffn1_l1)])
    wg1, wu1, wd1, wg2, wu2, wd2 = (buf[n] for n in ("wg1", "wu1", "wd1", "wg2", "wu2", "wd2"))
    w_q = buf["wq"].reshape(1, d, d)
    w_o = buf["wo"].reshape(1, 1, d, d)
    kv_full = buf["kv"].reshape(N_CHIPS, d, n_kv).transpose(1, 0, 2).reshape(d, N_CHIPS * n_kv)
    w_k = kv_full[:, :d][None]
    w_v = kv_full[:, d:2 * d][None]
    w_f = jnp.pad(kv_full[:, 2 * d:], ((0, 0), (0, LANES - n_heads)))[None]
    a2, b2, s2 = _ffn_up(h2b, wg2, wu2, 0, "ffn_up")
    r3, h3, h3b = _down_ln(s2, wd2, 0, h2, gains[0, 2], biases[0, 2], 0.5, "ffn_down_ln")
    kk = _nn_matmul(h3b, w_k, bf16, "proj_bf16")
    vv = _nn_matmul(h3b, w_v, bf16, "proj_bf16")
    flog = _nn_matmul(h3b, w_f, f32, "proj_gate")
    cum = _row_scan(flog, "gate_cumsum", fb_pad)
    bk = _tile(t, 640, LANES)
    c_ht = cum[:, :n_heads].T
    c_keys = jnp.where(jnp.arange(t)[None, :] < PAD, 1e30, c_ht)
    cq_rep = jnp.broadcast_to(c_ht[:, :, None], (n_heads, t, LANES))
    ck_rep = jnp.broadcast_to(c_keys[:, :, None], (n_heads, t, LANES))
    ck_rows = c_keys.reshape(n_heads, t // bk, 1, bk)
    a3, b3, s3 = _ffn_up(h3b, wg1, wu1, 1, "ffn_up")
    r4, h4, h4b = _down_ln(s3, wd1, 1, h3, gains[1, 0], biases[1, 0], 0.5, "ffn_down_ln")
    q = _nn_matmul(h4b, w_q, bf16, "proj_bf16")
    o, o32, lse_rep = _attn_fwd(q, kk, vv, cq_rep, ck_rows, "attn_fwd")
    r5, h5, h5b = _down_ln(o[None], w_o, 0, h4, gains[1, 1], biases[1, 1], 1.0, "mix_out_ln")
    a4, b4, s4 = _ffn_up(h5b, wg2, wu2, 1, "ffn_up")
    r6, h6, _ = _down_ln(s4, wd2, 1, h5, gains[1, 2], biases[1, 2], 0.5, "ffn_down_ln")
    dy, sq = _loss_head(h6, target, "loss_head")
    loss_part = 0.5 * sq[0, 0] / d

    m_small = small_pack(m_meta, m_ln_gain, m_ln_bias, m_conv_w, m_f_bias)
    v_small = small_pack(v_meta, v_ln_gain, v_ln_bias, v_conv_w, v_f_bias)
    m3 = {"wg1": m_ffn1_wg, "wu1": m_ffn1_wu, "wd1": m_ffn1_wd, "wg2": m_ffn2_wg, "wu2": m_ffn2_wu,
          "wd2": m_ffn2_wd, "win": m_conv_w_in, "wout": m_conv_w_out, "kv": m_kv_w[None],
          "wq": m_attn_w_q, "wo": m_attn_w_o, "small": m_small[None]}
    v3 = {"wg1": v_ffn1_wg, "wu1": v_ffn1_wu, "wd1": v_ffn1_wd, "wg2": v_ffn2_wg, "wu2": v_ffn2_wu,
          "wd2": v_ffn2_wd, "win": v_conv_w_in, "wout": v_conv_w_out, "kv": v_kv_w[None],
          "wq": v_attn_w_q, "wo": v_attn_w_o, "small": v_small[None]}
    stepped = {}

    def prepare(items):
        for it in items:
            buf["SEND_" + it] = _rs_prepare(buf["G_" + it], buf["PAIR_" + it], idx, "grad_prepare")

    def finish(items):
        for it in items:
            n, l = split(it)
            buf["RED_" + n] = _rs_finish(buf["G_" + it], buf["PAIR_" + it], buf["RECV_" + it], idx, l,
                                         w3[n].shape[0], buf.get("RED_" + n), "grad_finish")

    def adam(items, grads=None):
        for it in items:
            n, l = split(it)
            g = buf["RED_" + n] if grads is None else grads[n]
            stepped[n] = _adamw(w3[n], g, m3[n], v3[n], l, stepped.get(n), "adamw")

    def ffn_bwd(dr, drb, hb_in, a, b, s, f, layer, on_act=(), after_act=None, on_dwd=(),
                after_dwd=None, on_dx=()):
        da, db = run(_ffn_bwd_act, drb, buf["wd" + f], layer, a, b, "ffn_bwd_act", stages=on_act)
        if after_act is not None:
            after_act()
        (buf[f"G_wd{f}.{layer}"],) = run(_tn_matmul, s, "stack", [drb], ["shared"], N_CHIPS, 0.5,
                                         "ffn_dwd", stages=on_dwd)
        if after_dwd is not None:
            after_dwd()
        buf[f"G_wg{f}.{layer}"], buf[f"G_wu{f}.{layer}"] = _tn_matmul(
            hb_in, "shared", [da, db], ["stack", "stack"], N_CHIPS, 1.0, "ffn_dwgu")
        return run(_nt_sum, [(da, "stack", buf["wg" + f], layer), (db, "stack", buf["wu" + f], layer)],
                   dr, ALPHA, N_CHIPS, f32, "ffn_dx", stages=on_dx)

    ffn2_1 = ["wg2.1", "wu2.1", "wd2.1"]
    ffn1_1 = ["wg1.1", "wu1.1", "wd1.1"]
    ffn2_0 = ["wg2.0", "wu2.0", "wd2.0"]
    ffn1_0 = ["wg1.0", "wu1.0", "wd1.0"]
    conv_items = ["wout.0", "win.0"]

    dr6, dr6b, dg12, db12 = _ln_bwd(dy, r6, gains[1, 2], "ln_bwd")
    dh5 = ffn_bwd(dr6, dr6b, h5b, a4, b4, s4, "2", 1, on_dx=lambda: [pair_exchange(ffn2_1)])
    prepare(ffn2_1)
    dr5, dr5b, dg11, db11 = _ln_bwd(dh5, r5, gains[1, 1], "ln_bwd")
    (dwo,) = _tn_matmul(o, "shared", [dr5b], ["shared"], 1, 1.0, "sq_dw")
    buf["G_wo.0"] = dwo.reshape(N_CHIPS, dq, d)
    do = run(_nt_sum, [(dr5b, "cols", w_o[0], None)], None, 1.0, 1, bf16, "sq_dx_bf16",
             stages=lambda: [pair_exchange(["wo.0"])])
    prepare(["wo.0"])
    cl, delta = _attn_stats(o32, do, lse_rep, cq_rep, bk, "attn_stats")
    kt = kk.reshape(t // bk, bk, n_heads, HEAD_DIM).transpose(2, 0, 3, 1)
    dq_att, dc_q, dk, dv, dc_k = run(_attn_bwd, q, kk, vv, kt, do, ck_rep, cl, delta, "attn_bwd",
                                     stages=lambda: [chip_exchange(ffn2_1 + ["wo.0"])])
    finish(ffn2_1 + ["wo.0"])
    dc = (dc_q + dc_k).reshape(n_heads, t)
    (dwq,) = run(_tn_matmul, h4b, "shared", [dq_att], ["shared"], 1, 1.0, "sq_dw",
                 stages=lambda: [pair_share(ffn2_1 + ["wo.0"])])
    buf["G_wq.0"] = dwq.reshape(N_CHIPS, dq, d)
    adam(ffn2_1 + ["wo.0"])
    dh4 = run(_nt_sum, [(dq_att, "cols", w_q, None)], dr5, ALPHA, 1, f32, "sq_dx_res",
              stages=lambda: [pair_exchange(["wq.0"])])
    prepare(["wq.0"])
    dr4, dr4b, dg10, db10 = _ln_bwd(dh4, r4, gains[1, 0], "ln_bwd")
    dh3a = ffn_bwd(dr4, dr4b, h3b, a3, b3, s3, "1", 1,
                   on_act=lambda: [chip_exchange(["wq.0"])],
                   on_dx=lambda: [pair_exchange(ffn1_1)])
    prepare(ffn1_1)
    finish(["wq.0"])
    dc_t = jnp.pad(dc.T, ((0, 0), (0, LANES - n_heads)))
    dlogf = _row_scan(dc_t[::-1], "rev_cumsum")[::-1]
    dfl, dfb_cols = _fgate_bwd(dlogf, flog, fb_pad, "gate_bwd")
    dwk, dwv, dwf = run(_tn_matmul, h3b, "shared", [dk, dv, dfl], ["shared"] * 3, 1, 1.0, "kv_dw",
                        stages=lambda: [chip_exchange(ffn1_1), pair_share(["wq.0"])])
    adam(["wq.0"])

    def by_chip(full):
        rows = full.shape[0]
        return full.reshape(rows, N_CHIPS, full.shape[1] // N_CHIPS).transpose(1, 0, 2)

    buf["G_kv.0"] = by_chip(jnp.concatenate([dwk[0], dwv[0], dwf[0][:, :n_heads]], axis=1))
    dh3 = run(_nt_sum, [(dk, "cols", w_k, None), (dv, "cols", w_v, None), (dfl, "cols", w_f, None)],
              dh3a, 1.0, 1, f32, "kv_dx", stages=lambda: [pair_exchange(["kv.0"])])
    prepare(["kv.0"])
    finish(ffn1_1)
    dr3, dr3b, dg02, db02 = _ln_bwd(dh3, r3, gains[0, 2], "ln_bwd")
    dh2 = ffn_bwd(dr3, dr3b, h2b, a2, b2, s2, "2", 0,
                  on_act=lambda: [chip_exchange(["kv.0"]), pair_share(ffn1_1)],
                  after_act=lambda: (adam(ffn1_1), finish(["kv.0"])),
                  on_dwd=lambda: [pair_share(["kv.0"])],
                  after_dwd=lambda: adam(["kv.0"]),
                  on_dx=lambda: [pair_exchange(ffn2_0)])
    prepare(ffn2_0)
    dr2, dr2b, dg01, db01 = _ln_bwd(dh2, r2, gains[0, 1], "ln_bwd")
    (dwout,) = _tn_matmul(z, "shared", [dr2b], ["shared"], 1, 1.0, "sq_dw")
    buf["G_wout.0"] = dwout.reshape(N_CHIPS, dq, d)
    dz = _nt_sum([(dr2b, "cols", w_out[0], None)], None, 1.0, 1, f32, "sq_dx_f32")
    dp, dconv_w = run(_conv_bwd, dz, p, conv_w_full, "conv_bwd",
                      stages=lambda: [chip_exchange(ffn2_0)])
    (buf["G_win.0"],) = _tn_matmul(h1b, "shared", [dp], ["cols"], N_CHIPS, 1.0, "conv_dwin")
    finish(ffn2_0)
    dh1 = run(_nt_sum, [(dp, "cols", w_in, None)], dr2, ALPHA, N_CHIPS, f32, "conv_dx",
              stages=lambda: [pair_exchange(conv_items), pair_share(ffn2_0)])
    prepare(conv_items)
    adam(ffn2_0)
    dr1, dr1b, dg00, db00 = _ln_bwd(dh1, r1, gains[0, 0], "ln_bwd")
    dh0 = ffn_bwd(dr1, dr1b, h0b, a1, b1, s1, "1", 0,
                  on_act=lambda: [chip_exchange(conv_items)],
                  after_act=lambda: finish(conv_items),
                  on_dwd=lambda: [pair_share(conv_items)],
                  after_dwd=lambda: adam(conv_items),
                  on_dx=lambda: [pair_exchange(ffn1_0)])
    prepare(ffn1_0)
    grad_x = dh0[PAD + N_META:][None]
    dmeta = dh0[PAD:PAD + N_META]
    buf["G_small.0"] = by_chip(jnp.concatenate(
        [dmeta, rows8(jnp.concatenate([dg00, dg01, dg02, dg10, dg11, dg12], axis=0)),
         rows8(jnp.concatenate([db00, db01, db02, db10, db11, db12], axis=0)),
         rows8(dconv_w), jnp.zeros((8, d), f32)], axis=0))
    run(None, stages=lambda: [chip_exchange(ffn1_0), pair_exchange(["small.0"])], name="grad_tail_1")
    prepare(["small.0"])
    finish(ffn1_0)
    run(None, stages=lambda: [chip_exchange(["small.0"]), pair_share(ffn1_0)], name="grad_tail_2")
    finish(["small.0"])
    adam(ffn1_0)
    run(None, stages=lambda: [pair_share(["small.0"])], name="grad_tail_3")

    tail = jnp.zeros((LANES,), f32).at[:n_heads].set(dfb_cols[0, :n_heads]).at[n_heads].set(loss_part)
    tail = lax.psum(tail, ("x", "y", "c"))
    loss = tail[n_heads]
    g_fb = tail[:n_heads]
    g_small = jnp.concatenate([buf["RED_small"][0, :40],
                               rows8(jnp.pad(g_fb, (0, dq - n_heads))[None])], axis=0)
    adam(["small.0"], grads={"small": g_small[None]})

    def unpack(pk):
        return (pk[:16], pk[16:22].reshape(DEPTH, 3, dq), pk[24:30].reshape(DEPTH, 3, dq),
                pk[32:35].reshape(1, 3, dq), pk[40, :n_heads])

    def order(pick):
        mt, g, b, cw, fb = unpack(pick("small")[0])
        big = {n: pick(n).reshape(w3[n].shape[-3:] if n != "kv" else kv_w.shape) for n in w3
               if n != "small"}
        return [mt, big["wg1"], big["wu1"], big["wd1"], big["wg2"], big["wu2"], big["wd2"], g, b,
                big["win"], cw, big["wout"], big["kv"], fb, big["wq"], big["wo"]]

    grads_out = dict({n: buf["RED_" + n] for n in w3}, small=g_small[None])
    return (loss, grad_x, *order(lambda n: grads_out[n]), *order(lambda n: stepped[n][0]),
            *order(lambda n: stepped[n][1]), *order(lambda n: stepped[n][2]))
```

```python
import functools
import math

import jax
import jax.numpy as jnp
from jax import lax
from jax.experimental import pallas as pl
from jax.experimental.pallas import tpu as pltpu

f32 = jnp.float32
bf16 = jnp.bfloat16

N_META = 16
PAD = 112
HEAD_DIM = 128
DEPTH = 2
LN_EPS = 1e-5
ALPHA = (2 * DEPTH) ** 0.25
NEG_INF = -1e30
N_CHIPS = 4
SMALL_ROWS = 48
LANES = 128

ADAM_LR = 0.001
ADAM_B1 = 0.9
ADAM_B2 = 0.999
ADAM_EPS = 1e-08
ADAM_WD = 0.01
ADAM_STEP = 10

VMEM_LIMIT_BYTES = 56 * 1024 * 1024
ROWS_WIDE = 1664
ROWS_ACC = 1040
STRIP = 16
MESH = pl.DeviceIdType.MESH

NT_DIMS = (((1,), (1,)), ((), ()))
TN_DIMS = (((0,), (0,)), ((), ()))


def _tile(n, target, mult):
    best = None
    for d in range(mult, min(n, target) + 1, mult):
        if n % d == 0:
            best = d
    assert best is not None, (n, target, mult)
    return best


def _params(*sem):
    return pltpu.CompilerParams(dimension_semantics=sem, vmem_limit_bytes=VMEM_LIMIT_BYTES)


class _Place:
    def __init__(self):
        self.cx, self.cy, self.c = lax.axis_index("x"), lax.axis_index("y"), lax.axis_index("c")
        self.chips = [(1 - self.cx, self.cy), (self.cx, 1 - self.cy), (1 - self.cx, 1 - self.cy)]
        self.me = 2 * self.cx + self.cy
        self.slots = [self.me] + [2 * px + py for px, py in self.chips]
        self.sib = (self.cx, self.cy, 1 - self.c)


class _Copies:
    def __init__(self):
        self.srcs, self.bufs, self.news = {}, {}, {}
        self.plans = []
        self.out_bufs, self.out_news = {}, {}

    def add(self, plan, n_copies, srcs=None, bufs=None, news=None):
        for have, more in ((self.srcs, srcs), (self.bufs, bufs), (self.news, news)):
            for key, val in (more or {}).items():
                assert key not in have or have[key] is val, key
                have[key] = val
        self.plans.append((plan, n_copies))

    def empty(self):
        return not self.plans

    def count(self):
        return sum(n for _, n in self.plans)

    def copies(self, src_refs, buf_refs, new_refs, send, recv):
        place = _Place()
        srcs = dict(zip(self.srcs, src_refs))
        bufs = dict(zip(self.bufs, buf_refs))
        news = dict(zip(self.news, new_refs))
        out = []
        for plan, n_copies in self.plans:
            triples = plan(srcs, bufs, news, place)
            assert len(triples) == n_copies
            for src, dst, dev in triples:
                n = len(out)
                out.append(pltpu.make_async_remote_copy(
                    src_ref=src, dst_ref=dst, send_sem=send.at[n], recv_sem=recv.at[n],
                    device_id=dev, device_id_type=MESH))
        return out

    def land(self, results):
        n_b = len(self.bufs)
        self.out_bufs = dict(zip(self.bufs, results[:n_b]))
        self.out_news = dict(zip(self.news, results[n_b:]))


def _pallas(comm, body, *, name, grid, in_specs, out_specs, out_shape, compiler_params,
            scratch_shapes=(), input_output_aliases=None):
    aliases = dict(input_output_aliases or {})
    if comm is None or comm.empty():
        return pl.pallas_call(body, name=name, grid=grid, in_specs=in_specs, out_specs=out_specs,
                              out_shape=out_shape, scratch_shapes=list(scratch_shapes),
                              input_output_aliases=aliases, compiler_params=compiler_params)
    single = not isinstance(out_shape, (list, tuple))
    out_shapes = [out_shape] if single else list(out_shape)
    out_specs_l = [out_specs] if single else list(out_specs)
    n_in, n_out, n_scr = len(in_specs), len(out_shapes), len(scratch_shapes)
    n_s, n_b, n_n = len(comm.srcs), len(comm.bufs), len(comm.news)
    n_copies = comm.count()

    def wrapped(*refs):
        ins = refs[:n_in]
        src_refs = refs[n_in:n_in + n_s]
        o0 = n_in + n_s + n_b
        outs = refs[o0:o0 + n_out]
        buf_refs = refs[o0 + n_out:o0 + n_out + n_b]
        new_refs = refs[o0 + n_out + n_b:o0 + n_out + n_b + n_n]
        rest = refs[o0 + n_out + n_b + n_n:]
        scratch, (send, recv) = rest[:n_scr], rest[n_scr:]
        ids = [pl.program_id(a) for a in range(len(grid))]
        first = functools.reduce(jnp.logical_and, [i == 0 for i in ids])
        last = functools.reduce(jnp.logical_and, [i == g - 1 for i, g in zip(ids, grid)])

        @pl.when(first)
        def _():
            for cp in comm.copies(src_refs, buf_refs, new_refs, send, recv):
                cp.start()

        body(*ins, *outs, *scratch)

        @pl.when(last)
        def _():
            for cp in comm.copies(src_refs, buf_refs, new_refs, send, recv):
                cp.wait()

    hbm = pl.BlockSpec(memory_space=pl.ANY)
    for j in range(n_b):
        aliases[n_in + n_s + j] = n_out + j
    call = pl.pallas_call(
        wrapped, name=name, grid=grid,
        in_specs=[*in_specs, *([hbm] * (n_s + n_b))],
        out_specs=[*out_specs_l, *([hbm] * (n_b + n_n))],
        out_shape=[*out_shapes,
                   *[jax.ShapeDtypeStruct(a.shape, a.dtype) for a in comm.bufs.values()],
                   *comm.news.values()],
        scratch_shapes=[*scratch_shapes, pltpu.SemaphoreType.DMA((n_copies,)),
                        pltpu.SemaphoreType.DMA((n_copies,))],
        input_output_aliases=aliases, compiler_params=compiler_params)

    def run(*args):
        res = call(*args, *comm.srcs.values(), *comm.bufs.values())
        comm.land(res[n_out:])
        return res[0] if single else res[:n_out]

    return run


def _copy_call(comm, name):
    n_s, n_b, n_n = len(comm.srcs), len(comm.bufs), len(comm.news)
    n_copies = comm.count()

    def body(*refs):
        src_refs = refs[:n_s]
        buf_refs = refs[n_s + n_b:n_s + 2 * n_b]
        new_refs = refs[n_s + 2 * n_b:n_s + 2 * n_b + n_n]
        send, recv = refs[n_s + 2 * n_b + n_n:]
        copies = comm.copies(src_refs, buf_refs, new_refs, send, recv)
        for cp in copies:
            cp.start()
        for cp in copies:
            cp.wait()

    hbm = pl.BlockSpec(memory_space=pl.ANY)
    res = pl.pallas_call(
        body, name=name,
        in_specs=[hbm] * (n_s + n_b), out_specs=[hbm] * (n_b + n_n),
        out_shape=[*[jax.ShapeDtypeStruct(a.shape, a.dtype) for a in comm.bufs.values()],
                   *comm.news.values()],
        input_output_aliases={n_s + j: j for j in range(n_b)},
        scratch_shapes=[pltpu.SemaphoreType.DMA((n_copies,)), pltpu.SemaphoreType.DMA((n_copies,))],
    )(*comm.srcs.values(), *comm.bufs.values())
    comm.land(res)


def _embed(meta_pad, x, name, comm=None):
    seq, d = x.shape
    t = seq + LANES

    def body(m_ref, x_ref, h_ref, hb_ref):
        first = pl.program_id(0) == 0
        v = jnp.where(first, m_ref[...], x_ref[...])
        h_ref[...] = v
        hb_ref[...] = v.astype(bf16)

    return _pallas(
        comm, body, name=name, grid=(t // LANES,),
        in_specs=[pl.BlockSpec((LANES, d), lambda i: (0, 0)),
                  pl.BlockSpec((LANES, d), lambda i: (jnp.maximum(i - 1, 0), 0))],
        out_specs=[pl.BlockSpec((LANES, d), lambda i: (i, 0)),
                   pl.BlockSpec((LANES, d), lambda i: (i, 0))],
        out_shape=[jax.ShapeDtypeStruct((t, d), f32), jax.ShapeDtypeStruct((t, d), bf16)],
        compiler_params=_params("parallel"),
    )(meta_pad, x)


def _nn_matmul(x, w, out_dtype, name, comm=None):
    t, k = x.shape
    s_n, _, n = w.shape
    assert s_n == 1 or n % LANES == 0
    tm = _tile(t, ROWS_WIDE, 16)

    def body(x_ref, w_ref, o_ref):
        o_ref[...] = jnp.dot(x_ref[...].astype(bf16), w_ref[...],
                             preferred_element_type=f32).astype(o_ref.dtype)

    return _pallas(
        comm, body, name=name, grid=(s_n, t // tm),
        in_specs=[pl.BlockSpec((tm, k), lambda s, i: (i, 0)),
                  pl.BlockSpec((None, k, n), lambda s, i: (s, 0, 0))],
        out_specs=pl.BlockSpec((tm, n), lambda s, i: (i, s)),
        out_shape=jax.ShapeDtypeStruct((t, s_n * n), out_dtype),
        compiler_params=_params("parallel", "parallel"),
    )(x, w)


def _ffn_up(hb, wg, wu, layer, name, comm=None):
    t, d = hb.shape
    s_n, _, n, _ = wg.shape
    tm = _tile(t, ROWS_WIDE, 16)

    def body(x_ref, wg_ref, wu_ref, a_ref, b_ref, s_ref):
        x = x_ref[...]
        a = lax.dot_general(x, wg_ref[...], NT_DIMS, preferred_element_type=f32)
        b = lax.dot_general(x, wu_ref[...], NT_DIMS, preferred_element_type=f32)
        a_ref[...] = a.astype(bf16)
        b_ref[...] = b.astype(bf16)
        s_ref[...] = (a * jax.nn.sigmoid(a) * b).astype(bf16)

    wspec = pl.BlockSpec((None, None, n, d), lambda s, i: (s, layer, 0, 0))
    ospec = pl.BlockSpec((None, tm, n), lambda s, i: (s, i, 0))
    return _pallas(
        comm, body, name=name, grid=(s_n, t // tm),
        in_specs=[pl.BlockSpec((tm, d), lambda s, i: (i, 0)), wspec, wspec],
        out_specs=[ospec, ospec, ospec],
        out_shape=[jax.ShapeDtypeStruct((s_n, t, n), bf16)] * 3,
        compiler_params=_params("parallel", "parallel"),
    )(hb, wg, wu)


def _down_ln(x, w, layer, hprev, gain, bias, beta, name, comm=None):
    s_n, t, k = x.shape
    d = w.shape[-1]
    tm = _tile(t, ROWS_ACC, 16)

    def body(x_ref, w_ref, h_ref, g_ref, b_ref, r_out, h_out, hb_out, acc):
        s = pl.program_id(1)

        @pl.when(s == 0)
        def _():
            acc[...] = jnp.zeros_like(acc)

        acc[...] += jnp.dot(x_ref[...], w_ref[...], preferred_element_type=f32)

        @pl.when(s == s_n - 1)
        def _():
            r = ALPHA * h_ref[...] + beta * acc[...]
            mu = jnp.mean(r, axis=-1, keepdims=True)
            xc = r - mu
            var = jnp.mean(xc * xc, axis=-1, keepdims=True)
            y = xc * lax.rsqrt(var + LN_EPS) * g_ref[...] + b_ref[...]
            r_out[...] = r
            h_out[...] = y
            hb_out[...] = y.astype(bf16)

    row = pl.BlockSpec((tm, d), lambda i, s: (i, 0))
    vec = pl.BlockSpec((1, d), lambda i, s: (0, 0))
    return _pallas(
        comm, body, name=name, grid=(t // tm, s_n),
        in_specs=[pl.BlockSpec((None, tm, k), lambda i, s: (s, i, 0)),
                  pl.BlockSpec((None, None, k, d), lambda i, s: (s, layer, 0, 0)),
                  row, vec, vec],
        out_specs=[row, row, row],
        out_shape=[jax.ShapeDtypeStruct((t, d), f32), jax.ShapeDtypeStruct((t, d), f32),
                   jax.ShapeDtypeStruct((t, d), bf16)],
        scratch_shapes=[pltpu.VMEM((tm, d), f32)],
        compiler_params=_params("parallel", "arbitrary"),
    )(x, w, hprev, gain, bias)


def _conv_fwd(p, conv_w, name):
    t, d3 = p.shape
    d = d3 // 3
    tm = _tile(t, 320, 8)
    hb = tm // 8

    def body(p_ref, prev_ref, w_ref, z_ref):
        i = pl.program_id(0)
        rows = i * tm - 8 + lax.broadcasted_iota(jnp.int32, (tm + 8, 1), 0)
        cg = jnp.concatenate([prev_ref[:, d:2 * d], p_ref[:, d:2 * d]], axis=0)
        val = jnp.concatenate([prev_ref[:, 2 * d:], p_ref[:, 2 * d:]], axis=0)
        u = jnp.where(rows >= PAD, cg * val, 0.0)
        y = (w_ref[2:3, :] * u + w_ref[1:2, :] * pltpu.roll(u, 1, 0)
             + w_ref[0:1, :] * pltpu.roll(u, 2, 0))
        z_ref[...] = (p_ref[:, :d] * y[8:]).astype(bf16)

    return pl.pallas_call(
        body, name=name, grid=(t // tm,),
        in_specs=[pl.BlockSpec((tm, d3), lambda i: (i, 0)),
                  pl.BlockSpec((8, d3), lambda i: (jnp.maximum(i * hb - 1, 0), 0)),
                  pl.BlockSpec((3, d), lambda i: (0, 0))],
        out_specs=pl.BlockSpec((tm, d), lambda i: (i, 0)),
        out_shape=jax.ShapeDtypeStruct((t, d), bf16),
        compiler_params=_params("parallel"),
    )(p, p, conv_w)


def _row_scan(x, name, fbias=None):
    t, n = x.shape
    blk = LANES
    gate = fbias is not None

    def body(*refs):
        if gate:
            x_ref, fb_ref, o_ref, carry = refs
        else:
            x_ref, o_ref, carry = refs
        i = pl.program_id(0)

        @pl.when(i == 0)
        def _():
            carry[...] = jnp.zeros_like(carry)

        v = x_ref[...]
        r = lax.broadcasted_iota(jnp.int32, (blk, n), 0)
        if gate:
            v = v + fb_ref[...]
            v = jnp.minimum(v, 0.0) - jnp.log1p(jnp.exp(-jnp.abs(v)))
            v = jnp.where(i * blk + r >= PAD, v, 0.0)
        sh = 1
        while sh < blk:
            v = v + jnp.where(r >= sh, pltpu.roll(v, sh, 0), 0.0)
            sh *= 2
        v = v + carry[...]
        o_ref[...] = v
        carry[...] = o_ref[blk - 1:blk, :]

    in_specs = [pl.BlockSpec((blk, n), lambda i: (i, 0))]
    args = [x]
    if gate:
        in_specs.append(pl.BlockSpec((1, n), lambda i: (0, 0)))
        args.append(fbias)
    return pl.pallas_call(
        body, name=name, grid=(t // blk,),
        in_specs=in_specs,
        out_specs=pl.BlockSpec((blk, n), lambda i: (i, 0)),
        out_shape=jax.ShapeDtypeStruct((t, n), f32),
        scratch_shapes=[pltpu.VMEM((1, n), f32)],
        compiler_params=_params("arbitrary"),
    )(*args)


def _lanes(x, n):
    return jnp.concatenate([x] * (n // LANES), axis=1)


def _attn_fwd(q, k, v, cq_rep, ck_rows, name):
    t, d = q.shape
    n_heads = d // HEAD_DIM
    bk = ck_rows.shape[-1]
    bq = bk
    scale = 1.0 / math.sqrt(HEAD_DIM)

    def lane_fold(x, op):
        out = x[:, :LANES]
        for c0 in range(LANES, bk, LANES):
            out = op(out, x[:, c0:c0 + LANES])
        return out

    def body(q_ref, k_ref, v_ref, cq_ref, ck_ref, o_ref, o32_ref, lse_ref,
             s_scr, p_scr, m_scr, l_scr, red_scr, acc_scr):
        i = pl.program_id(1)
        m_scr[...] = jnp.full_like(m_scr, NEG_INF)
        l_scr[...] = jnp.zeros_like(l_scr)
        acc_scr[...] = jnp.zeros_like(acc_scr)
        qb = q_ref[...]
        ahead = (lax.broadcasted_iota(jnp.int32, (STRIP, bk), 1)
                 - lax.broadcasted_iota(jnp.int32, (STRIP, bk), 0))

        def tile(j, diagonal):
            k0 = pl.multiple_of(j * bk, bk)
            s_scr[...] = lax.dot_general(qb, k_ref[pl.ds(k0, bk), :], NT_DIMS,
                                         preferred_element_type=f32)
            ck = ck_ref[j]
            for r in range(0, bq, STRIP):
                rows = slice(r, r + STRIP)
                s = s_scr[rows, :] * scale + _lanes(cq_ref[rows, :], bk) - ck
                if diagonal:
                    s = jnp.where(ahead <= r, s, NEG_INF)
                s_scr[rows, :] = s
                red_scr[rows, :] = lane_fold(s, jnp.maximum)
            m_old = m_scr[...]
            m_new = jnp.maximum(m_old, jnp.broadcast_to(
                jnp.max(red_scr[...], axis=1, keepdims=True), (bq, LANES)))
            a = jnp.exp(m_old - m_new)
            m_scr[...] = m_new
            for r in range(0, bq, STRIP):
                rows = slice(r, r + STRIP)
                pr = jnp.exp(s_scr[rows, :] - _lanes(m_scr[rows, :], bk))
                red_scr[rows, :] = lane_fold(pr, jnp.add)
                p_scr[rows, :] = pr.astype(bf16)
            l_scr[...] = a * l_scr[...] + jnp.broadcast_to(
                jnp.sum(red_scr[...], axis=1, keepdims=True), (bq, LANES))
            acc_scr[...] = a * acc_scr[...] + jnp.dot(
                p_scr[...], v_ref[pl.ds(k0, bk), :], preferred_element_type=f32)

        def full_tile(j, carry):
            tile(j, False)
            return carry

        lax.fori_loop(0, i, full_tile, 0)
        tile(i, True)
        out = acc_scr[...] / l_scr[...]
        o_ref[...] = out.astype(bf16)
        o32_ref[...] = out
        lse_ref[...] = m_scr[...] + jnp.log(l_scr[...])

    qblk = pl.BlockSpec((bq, HEAD_DIM), lambda h, i: (i, h))
    head_rows = pl.BlockSpec((t, HEAD_DIM), lambda h, i: (0, h))
    rep = pl.BlockSpec((None, bq, LANES), lambda h, i: (h, i, 0))
    col = pltpu.VMEM((bq, LANES), f32)
    return pl.pallas_call(
        body, name=name, grid=(n_heads, t // bq),
        in_specs=[qblk, head_rows, head_rows, rep,
                  pl.BlockSpec((None, t // bk, 1, bk), lambda h, i: (h, 0, 0, 0))],
        out_specs=[qblk, qblk, rep],
        out_shape=[jax.ShapeDtypeStruct((t, d), bf16), jax.ShapeDtypeStruct((t, d), f32),
                   jax.ShapeDtypeStruct((n_heads, t, LANES), f32)],
        scratch_shapes=[pltpu.VMEM((bq, bk), f32), pltpu.VMEM((bq, bk), bf16), col, col, col,
                        pltpu.VMEM((bq, HEAD_DIM), f32)],
        compiler_params=_params("parallel", "parallel"),
    )(q, k, v, cq_rep, ck_rows)


def _loss_head(h, target, name):
    t, d = h.shape

    def body(h_ref, t_ref, dy_ref, loss_ref):
        i = pl.program_id(0)

        @pl.when(i == 0)
        def _():
            loss_ref[...] = jnp.zeros_like(loss_ref)

        diff = jnp.where(i >= 1, h_ref[...] - t_ref[...], 0.0)
        dy_ref[...] = diff * (1.0 / d)
        loss_ref[...] += jnp.sum(diff * diff)

    return pl.pallas_call(
        body, name=name, grid=(t // LANES,),
        in_specs=[pl.BlockSpec((LANES, d), lambda i: (i, 0)),
                  pl.BlockSpec((LANES, d), lambda i: (jnp.maximum(i - 1, 0), 0))],
        out_specs=[pl.BlockSpec((LANES, d), lambda i: (i, 0)),
                   pl.BlockSpec((1, LANES), lambda i: (0, 0))],
        out_shape=[jax.ShapeDtypeStruct((t, d), f32), jax.ShapeDtypeStruct((1, LANES), f32)],
        compiler_params=_params("arbitrary"),
    )(h, target)


def _ln_bwd(dh, r, gain, name):
    t, d = r.shape
    tm = _tile(t, 640, 16)

    def body(dh_ref, r_ref, g_ref, dr_ref, drb_ref, dg_ref, db_ref):
        @pl.when(pl.program_id(0) == 0)
        def _():
            dg_ref[...] = jnp.zeros_like(dg_ref)
            db_ref[...] = jnp.zeros_like(db_ref)

        rr = r_ref[...]
        mu = jnp.mean(rr, axis=-1, keepdims=True)
        xc = rr - mu
        var = jnp.mean(xc * xc, axis=-1, keepdims=True)
        rstd = lax.rsqrt(var + LN_EPS)
        xhat = xc * rstd
        dy = dh_ref[...]
        dxh = dy * g_ref[...]
        m1 = jnp.mean(dxh, axis=-1, keepdims=True)
        m2 = jnp.mean(dxh * xhat, axis=-1, keepdims=True)
        dr = rstd * (dxh - m1 - xhat * m2)
        dr_ref[...] = dr
        drb_ref[...] = dr.astype(bf16)
        dg_ref[...] += jnp.sum(dy * xhat, axis=0, keepdims=True)
        db_ref[...] += jnp.sum(dy, axis=0, keepdims=True)

    row = pl.BlockSpec((tm, d), lambda i: (i, 0))
    vec = pl.BlockSpec((1, d), lambda i: (0, 0))
    return pl.pallas_call(
        body, name=name, grid=(t // tm,),
        in_specs=[row, row, vec],
        out_specs=[row, row, vec, vec],
        out_shape=[jax.ShapeDtypeStruct((t, d), f32), jax.ShapeDtypeStruct((t, d), bf16),
                   jax.ShapeDtypeStruct((1, d), f32), jax.ShapeDtypeStruct((1, d), f32)],
        compiler_params=_params("arbitrary"),
    )(dh, r, gain)


def _ffn_bwd_act(drb, wd, layer, a, b, name, comm=None):
    t, d = drb.shape
    s_n, _, n = a.shape
    tm = _tile(t, ROWS_WIDE, 16)

    def body(dr_ref, w_ref, a_ref, b_ref, da_ref, db_ref):
        ds = 0.5 * lax.dot_general(dr_ref[...], w_ref[...], NT_DIMS, preferred_element_type=f32)
        av = a_ref[...].astype(f32)
        sig = jax.nn.sigmoid(av)
        da_ref[...] = (ds * b_ref[...].astype(f32) * (sig * (1.0 + av * (1.0 - sig)))).astype(bf16)
        db_ref[...] = (ds * (av * sig)).astype(bf16)

    act = pl.BlockSpec((None, tm, n), lambda s, i: (s, i, 0))
    return _pallas(
        comm, body, name=name, grid=(s_n, t // tm),
        in_specs=[pl.BlockSpec((tm, d), lambda s, i: (i, 0)),
                  pl.BlockSpec((None, None, n, d), lambda s, i: (s, layer, 0, 0)), act, act],
        out_specs=[act, act],
        out_shape=[jax.ShapeDtypeStruct((s_n, t, n), bf16), jax.ShapeDtypeStruct((s_n, t, n), bf16)],
        compiler_params=_params("parallel", "parallel"),
    )(drb, wd, a, b)


def _act_spec(mode, tt, k, t_first):
    def fix(fn):
        return (lambda i, s: fn(s, i)) if t_first else fn
    if mode == "shared":
        return pl.BlockSpec((tt, k), fix(lambda s, i: (i, 0)))
    if mode == "cols":
        return pl.BlockSpec((tt, k), fix(lambda s, i: (i, s)))
    assert mode == "stack"
    return pl.BlockSpec((None, tt, k), fix(lambda s, i: (s, i, 0)))


def _act_width(arr, mode, s_n):
    return arr.shape[-1] // s_n if mode == "cols" else arr.shape[-1]


def _tn_matmul(pairs, s_n, scale, name, comm=None):
    t = pairs[0][0].shape[-2]
    tt = _tile(t, 2080, 16)
    n_t = t // tt
    arrays, specs, where = [], [], []
    for x, xmode, y, ymode in pairs:
        pos = []
        for arr, mode in ((x, xmode), (y, ymode)):
            hit = [j for j, a in enumerate(arrays) if a is arr]
            if not hit:
                arrays.append(arr)
                specs.append(_act_spec(mode, tt, _act_width(arr, mode, s_n), False))
                hit = [len(arrays) - 1]
            pos.append(hit[0])
        where.append(pos)
    widths = [(_act_width(x, xm, s_n), _act_width(y, ym, s_n)) for x, xm, y, ym in pairs]
    n_a = len(arrays)

    def body(*refs):
        i = pl.program_id(1)
        for (px, py), o_ref in zip(where, refs[n_a:]):
            part = lax.dot_general(refs[px][...].astype(bf16), refs[py][...].astype(bf16), TN_DIMS,
                                   preferred_element_type=f32)

            @pl.when(i == 0)
            def _():
                o_ref[...] = part

            @pl.when(i > 0)
            def _():
                o_ref[...] += part

            if scale != 1.0:
                @pl.when(i == n_t - 1)
                def _():
                    o_ref[...] = o_ref[...] * scale

    return _pallas(
        comm, body, name=name, grid=(s_n, n_t),
        in_specs=specs,
        out_specs=[pl.BlockSpec((None, kx, ky), lambda s, i: (s, 0, 0)) for kx, ky in widths],
        out_shape=[jax.ShapeDtypeStruct((s_n, kx, ky), f32) for kx, ky in widths],
        compiler_params=_params("parallel", "arbitrary"),
    )(*arrays)


def _nt_sum(pairs, base, base_scale, s_n, out_dtype, name, comm=None):
    t = pairs[0][0].shape[-2]
    pairs = [(*pr, False)[:5] for pr in pairs]
    d = pairs[0][2].shape[-1] if pairs[0][4] else pairs[0][2].shape[-2]
    tm = _tile(t, ROWS_ACC, 16)
    n_p = len(pairs)
    has_base = base is not None
    flipped = [pr[4] for pr in pairs]

    def body(*refs):
        dy_refs = refs[0:2 * n_p:2]
        w_refs = refs[1:2 * n_p:2]
        rest = refs[2 * n_p:]
        base_ref = rest[0] if has_base else None
        o_ref, acc = rest[-2], rest[-1]
        s = pl.program_id(1)

        @pl.when(s == 0)
        def _():
            acc[...] = jnp.zeros_like(acc)

        tot = None
        for dy_ref, w_ref, flip in zip(dy_refs, w_refs, flipped):
            dyv = dy_ref[...].astype(bf16)
            if flip:
                part = jnp.dot(dyv, w_ref[...], preferred_element_type=f32)
            else:
                part = lax.dot_general(dyv, w_ref[...], NT_DIMS, preferred_element_type=f32)
            tot = part if tot is None else tot + part
        acc[...] += tot

        @pl.when(s == s_n - 1)
        def _():
            res = acc[...]
            if has_base:
                res = base_scale * base_ref[...] + res
            o_ref[...] = res.astype(o_ref.dtype)

    in_specs, args = [], []
    for dy, mode, w, layer, flip in pairs:
        k = _act_width(dy, mode, s_n)
        in_specs.append(_act_spec(mode, tm, k, True))
        wshape = (k, d) if flip else (d, k)
        if layer is None:
            in_specs.append(pl.BlockSpec((None, *wshape), lambda i, s: (s, 0, 0)))
        else:
            in_specs.append(pl.BlockSpec((None, None, *wshape),
                                         functools.partial(lambda i, s, l: (s, l, 0, 0), l=layer)))
        args += [dy, w]
    row = pl.BlockSpec((tm, d), lambda i, s: (i, 0))
    if has_base:
        in_specs.append(row)
        args.append(base)
    return _pallas(
        comm, body, name=name, grid=(t // tm, s_n),
        in_specs=in_specs, out_specs=row,
        out_shape=jax.ShapeDtypeStruct((t, d), out_dtype),
        scratch_shapes=[pltpu.VMEM((tm, d), f32)],
        compiler_params=_params("parallel", "arbitrary"),
    )(*args)


def _conv_bwd(dz, p, conv_w, name, comm=None):
    t, d3 = p.shape
    d = d3 // 3
    tm = _tile(t, 320, 8)
    hb = tm // 8
    last8 = t // 8 - 1
    n_ext = tm + 8

    def body(dz_ref, dzn_ref, p_ref, pp_ref, pn_ref, w_ref, dp_ref, dw_ref):
        i = pl.program_id(0)

        @pl.when(i == 0)
        def _():
            dw_ref[...] = jnp.zeros_like(dw_ref)

        w0, w1, w2 = w_ref[0:1, :], w_ref[1:2, :], w_ref[2:3, :]
        rows_u = i * tm - 8 + lax.broadcasted_iota(jnp.int32, (n_ext, 1), 0)
        cg = jnp.concatenate([pp_ref[:, d:2 * d], p_ref[:, d:2 * d]], axis=0)
        val = jnp.concatenate([pp_ref[:, 2 * d:], p_ref[:, 2 * d:]], axis=0)
        u = jnp.where(rows_u >= PAD, cg * val, 0.0)
        u1 = pltpu.roll(u, 1, 0)
        u2 = pltpu.roll(u, 2, 0)
        y = (w2 * u + w1 * u1 + w0 * u2)[8:]
        dzv = dz_ref[...]
        bg = p_ref[:, :d]
        rows_n = (i + 1) * tm + lax.broadcasted_iota(jnp.int32, (8, 1), 0)
        dy_main = dzv * bg
        dy_next = jnp.where(rows_n < t, dzn_ref[...] * pn_ref[:, :d], 0.0)
        dye = jnp.concatenate([dy_main, dy_next], axis=0)
        du = (w2 * dye + w1 * pltpu.roll(dye, n_ext - 1, 0)
              + w0 * pltpu.roll(dye, n_ext - 2, 0))[:tm]
        du = jnp.where(rows_u[8:] >= PAD, du, 0.0)
        dp_ref[:, :d] = (dzv * y).astype(bf16)
        dp_ref[:, d:2 * d] = (du * val[8:]).astype(bf16)
        dp_ref[:, 2 * d:] = (du * cg[8:]).astype(bf16)
        dw_ref[0:1, :] += jnp.sum(dy_main * u2[8:], axis=0, keepdims=True)
        dw_ref[1:2, :] += jnp.sum(dy_main * u1[8:], axis=0, keepdims=True)
        dw_ref[2:3, :] += jnp.sum(dy_main * u[8:], axis=0, keepdims=True)

    nxt = lambda i: (jnp.minimum((i + 1) * hb, last8), 0)
    return _pallas(
        comm, body, name=name, grid=(t // tm,),
        in_specs=[pl.BlockSpec((tm, d), lambda i: (i, 0)),
                  pl.BlockSpec((8, d), nxt),
                  pl.BlockSpec((tm, d3), lambda i: (i, 0)),
                  pl.BlockSpec((8, d3), lambda i: (jnp.maximum(i * hb - 1, 0), 0)),
                  pl.BlockSpec((8, d3), nxt),
                  pl.BlockSpec((3, d), lambda i: (0, 0))],
        out_specs=[pl.BlockSpec((tm, d3), lambda i: (i, 0)),
                   pl.BlockSpec((3, d), lambda i: (0, 0))],
        out_shape=[jax.ShapeDtypeStruct((t, d3), bf16), jax.ShapeDtypeStruct((3, d), f32)],
        compiler_params=_params("arbitrary"),
    )(dz, dz, p, p, p, conv_w)


def _attn_stats(o, do, lse_rep, cq_rep, bq, name):
    t, d = o.shape
    n_heads = d // HEAD_DIM

    def body(o_ref, do_ref, lse_ref, cq_ref, cl_ref, delta_ref):
        for c0 in range(0, bq, LANES):
            rows = slice(c0, c0 + LANES)
            cl_ref[:, rows] = (cq_ref[rows, :] - lse_ref[rows, :]).T[0:1, :]
            prod = o_ref[rows, :] * do_ref[rows, :].astype(f32)
            delta_ref[:, rows] = jnp.sum(prod.T, axis=0, keepdims=True)

    qblk = pl.BlockSpec((bq, HEAD_DIM), lambda h, i: (i, h))
    rep = pl.BlockSpec((None, bq, LANES), lambda h, i: (h, i, 0))
    row = pl.BlockSpec((None, None, 1, bq), lambda h, i: (h, i, 0, 0))
    shp = jax.ShapeDtypeStruct((n_heads, t // bq, 1, bq), f32)
    return pl.pallas_call(
        body, name=name, grid=(n_heads, t // bq),
        in_specs=[qblk, qblk, rep, rep],
        out_specs=[row, row], out_shape=[shp, shp],
        compiler_params=_params("parallel", "parallel"),
    )(o, do, lse_rep, cq_rep)


def _attn_bwd(q, k, v, kt, do, ckey, cl_rows, delta_rows, name, comm=None):
    t, d = q.shape
    n_heads = d // HEAD_DIM
    bk = kt.shape[-1]
    bq = bk
    n_kv = t // bk
    n_q = t // bq
    scale = 1.0 / math.sqrt(HEAD_DIM)

    def body(q_ref, do_ref, cl_ref, dl_ref, k_ref, v_ref, kt_ref, ck_ref,
             dq_ref, dcq_ref, dk_ref, dv_ref, dck_ref,
             st_scr, dp_scr, p_scr, ds_scr, dqt, dk_acc, dv_acc, dck_acc):
        j = pl.program_id(1)

        @pl.when(j == 0)
        def _():
            dqt[...] = jnp.zeros_like(dqt)
            dcq_ref[...] = jnp.zeros_like(dcq_ref)

        dk_acc[...] = jnp.zeros_like(dk_acc)
        dv_acc[...] = jnp.zeros_like(dv_acc)
        dck_acc[...] = jnp.zeros_like(dck_acc)
        kb = k_ref[...]
        vb = v_ref[...]
        behind = (lax.broadcasted_iota(jnp.int32, (STRIP, bq), 1)
                  - lax.broadcasted_iota(jnp.int32, (STRIP, bq), 0))

        def tile(i, diagonal):
            r0 = pl.multiple_of(i * bq, bq)
            qi = q_ref[pl.ds(r0, bq), :]
            doi = do_ref[pl.ds(r0, bq), :]
            st_scr[...] = lax.dot_general(kb, qi, NT_DIMS, preferred_element_type=f32)
            dp_scr[...] = lax.dot_general(vb, doi, NT_DIMS, preferred_element_type=f32)
            cl = cl_ref[i]
            dl = dl_ref[i]
            over_keys = jnp.zeros((STRIP, bq), f32)
            for r in range(0, bk, STRIP):
                keys = slice(r, r + STRIP)
                st = st_scr[keys, :] * scale + cl - _lanes(ck_ref[keys, :], bq)
                if diagonal:
                    st = jnp.where(behind >= r, st, NEG_INF)
                pr = jnp.exp(st)
                ds = pr * (dp_scr[keys, :] - dl)
                over_keys = over_keys + ds
                dck_acc[keys, :] -= jnp.sum(ds, axis=1, keepdims=True)
                p_scr[keys, :] = pr.astype(bf16)
                ds_scr[keys, :] = ds.astype(bf16)
            dcq_ref[i] += jnp.sum(over_keys, axis=0, keepdims=True)
            dv_acc[...] += jnp.dot(p_scr[...], doi, preferred_element_type=f32)
            dk_acc[...] += jnp.dot(ds_scr[...], qi, preferred_element_type=f32)
            dqt[i] += jnp.dot(kt_ref[...], ds_scr[...], preferred_element_type=f32)

        def full_tile(i, carry):
            tile(i, False)
            return carry

        tile(j, True)
        lax.fori_loop(j + 1, n_q, full_tile, 0)
        dk_ref[...] = (dk_acc[...] * scale).astype(bf16)
        dv_ref[...] = dv_acc[...].astype(bf16)
        for c0 in range(0, bk, LANES):
            keys = slice(c0, c0 + LANES)
            dck_ref[:, keys] = jnp.broadcast_to(dck_acc[keys, :], (LANES, LANES)).T[0:1, :]

        @pl.when(j == n_kv - 1)
        def _():
            def emit(i, carry):
                r0 = pl.multiple_of(i * bq, bq)
                dq_ref[pl.ds(r0, bq), :] = dqt[i].T * scale
                return carry
            lax.fori_loop(0, n_q, emit, 0)

    head_rows = pl.BlockSpec((t, HEAD_DIM), lambda h, j: (0, h))
    head_stat = pl.BlockSpec((None, n_q, 1, bq), lambda h, j: (h, 0, 0, 0))
    kblk = pl.BlockSpec((bk, HEAD_DIM), lambda h, j: (j, h))
    return _pallas(
        comm, body, name=name, grid=(n_heads, n_kv),
        in_specs=[head_rows, head_rows, head_stat, head_stat, kblk, kblk,
                  pl.BlockSpec((None, None, HEAD_DIM, bk), lambda h, j: (h, j, 0, 0)),
                  pl.BlockSpec((None, bk, LANES), lambda h, j: (h, j, 0))],
        out_specs=[head_rows, head_stat, kblk, kblk,
                   pl.BlockSpec((None, None, 1, bk), lambda h, j: (h, j, 0, 0))],
        out_shape=[jax.ShapeDtypeStruct((t, d), f32),
                   jax.ShapeDtypeStruct((n_heads, n_q, 1, bq), f32),
                   jax.ShapeDtypeStruct((t, d), bf16), jax.ShapeDtypeStruct((t, d), bf16),
                   jax.ShapeDtypeStruct((n_heads, n_kv, 1, bk), f32)],
        scratch_shapes=[pltpu.VMEM((bk, bq), f32), pltpu.VMEM((bk, bq), f32),
                        pltpu.VMEM((bk, bq), bf16), pltpu.VMEM((bk, bq), bf16),
                        pltpu.VMEM((n_q, HEAD_DIM, bq), f32),
                        pltpu.VMEM((bk, HEAD_DIM), f32), pltpu.VMEM((bk, HEAD_DIM), f32),
                        pltpu.VMEM((bk, 1), f32)],
        compiler_params=_params("parallel", "arbitrary"),
    )(q, do, cl_rows, delta_rows, k, v, kt, ckey)


def _fgate_bwd(dlogf, flog, fbias, name):
    t, n = flog.shape

    def body(dl_ref, fl_ref, fb_ref, o_ref, sum_ref):
        i = pl.program_id(0)

        @pl.when(i == 0)
        def _():
            sum_ref[...] = jnp.zeros_like(sum_ref)

        r = i * LANES + lax.broadcasted_iota(jnp.int32, (LANES, n), 0)
        g = dl_ref[...] * jax.nn.sigmoid(-(fl_ref[...] + fb_ref[...]))
        g = jnp.where(r >= PAD, g, 0.0)
        o_ref[...] = g
        sum_ref[...] += jnp.sum(g, axis=0, keepdims=True)

    blk = pl.BlockSpec((LANES, n), lambda i: (i, 0))
    vec = pl.BlockSpec((1, n), lambda i: (0, 0))
    return pl.pallas_call(
        body, name=name, grid=(t // LANES,),
        in_specs=[blk, blk, vec], out_specs=[blk, vec],
        out_shape=[jax.ShapeDtypeStruct((t, n), f32), jax.ShapeDtypeStruct((1, n), f32)],
        compiler_params=_params("arbitrary"),
    )(dlogf, flog, fbias)


def _place_shard(w, idx, name):
    n_l, r, c_n = w.shape
    out_dtype = bf16 if r * c_n > 2 ** 16 else w.dtype
    tr = _tile(r, 512, 16) if r % 16 == 0 else r

    def body(idx_ref, w_ref, o_ref):
        o_ref[...] = w_ref[...].astype(out_dtype)

    grid_spec = pltpu.PrefetchScalarGridSpec(
        num_scalar_prefetch=1, grid=(n_l, r // tr),
        in_specs=[pl.BlockSpec((None, tr, c_n), lambda l, i, idx: (l, i, 0))],
        out_specs=pl.BlockSpec((None, None, tr, c_n), lambda l, i, idx: (idx[0], l, i, 0)))
    return pl.pallas_call(
        body, name=name, grid_spec=grid_spec,
        out_shape=jax.ShapeDtypeStruct((N_CHIPS, n_l, r, c_n), out_dtype),
        compiler_params=_params("parallel", "parallel"),
    )(idx, w)


def _plan_gather_ici(items):
    def plan(srcs, bufs, news, p):
        out = []
        for name, layer, r2 in items:
            mine = bufs[name].at[p.me, layer, pl.ds(p.c * r2, r2)]
            out += [(mine, mine, (*chip, p.c)) for chip in p.chips]
        return out
    return plan, 3 * len(items)


def _plan_gather_d2d(items):
    def plan(srcs, bufs, news, p):
        out = []
        for name, layer, r2 in items:
            for px, py in p.chips:
                landed = bufs[name].at[2 * px + py, layer, pl.ds(p.c * r2, r2)]
                out.append((landed, landed, p.sib))
        return out
    return plan, 3 * len(items)


def _plan_pair_exchange(names, r2s):
    def plan(srcs, bufs, news, p):
        out = []
        for name, r2 in zip(names, r2s):
            for k, slot in enumerate(p.slots):
                out.append((srcs["G_" + name].at[slot, pl.ds((1 - p.c) * r2, r2)],
                            news["PAIR_" + name].at[k], p.sib))
        return out
    return plan, 4 * len(names)


def _plan_chip_exchange(names):
    def plan(srcs, bufs, news, p):
        out = []
        for name in names:
            for k, chip in enumerate(p.chips):
                out.append((srcs["SEND_" + name].at[k], news["RECV_" + name].at[k], (*chip, p.c)))
        return out
    return plan, 3 * len(names)


def _plan_pair_share(items):
    def plan(srcs, bufs, news, p):
        out = []
        for name, layer, r2 in items:
            mine = bufs["RED_" + name].at[layer, pl.ds(p.c * r2, r2)]
            out.append((mine, mine, p.sib))
        return out
    return plan, len(items)


def _rs_prepare(g, pair, idx, name):
    _, r2, c_n = pair.shape
    tr = _tile(r2, 256, 8)
    nb = r2 // tr

    def body(idx_ref, g_ref, p_ref, o_ref):
        o_ref[...] = (g_ref[...] + p_ref[...]).astype(bf16)

    grid_spec = pltpu.PrefetchScalarGridSpec(
        num_scalar_prefetch=1, grid=(3, nb),
        in_specs=[pl.BlockSpec((None, tr, c_n), lambda k, i, idx: (idx[k + 1], idx[4] * nb + i, 0)),
                  pl.BlockSpec((None, tr, c_n), lambda k, i, idx: (k + 1, i, 0))],
        out_specs=pl.BlockSpec((None, tr, c_n), lambda k, i, idx: (k, i, 0)))
    return pl.pallas_call(
        body, name=name, grid_spec=grid_spec,
        out_shape=jax.ShapeDtypeStruct((3, r2, c_n), bf16),
        compiler_params=_params("parallel", "parallel"),
    )(idx, g, pair)


def _rs_finish(g, pair, recv, idx, layer, n_layers, into, name):
    _, r2, c_n = pair.shape
    tr = _tile(r2, 256, 8)
    nb = r2 // tr

    def body(*refs):
        g_ref, p_ref, r0_ref, r1_ref, r2_ref = refs[1:6]
        o_ref = refs[-1]
        acc = g_ref[...] + p_ref[...]
        acc = acc + r0_ref[...].astype(f32)
        acc = acc + r1_ref[...].astype(f32)
        acc = acc + r2_ref[...].astype(f32)
        o_ref[...] = acc

    def rspec(k):
        return pl.BlockSpec((None, tr, c_n), functools.partial(lambda i, idx, kk: (kk, i, 0), kk=k))

    in_specs = [pl.BlockSpec((None, tr, c_n), lambda i, idx: (idx[0], idx[4] * nb + i, 0)),
                rspec(0), rspec(0), rspec(1), rspec(2)]
    args = [idx, g, pair, recv, recv, recv]
    aliases = {}
    if into is not None:
        in_specs.append(pl.BlockSpec(memory_space=pl.ANY))
        args.append(into)
        aliases = {6: 0}
    grid_spec = pltpu.PrefetchScalarGridSpec(
        num_scalar_prefetch=1, grid=(nb,), in_specs=in_specs,
        out_specs=pl.BlockSpec((None, tr, c_n), lambda i, idx: (layer, idx[4] * nb + i, 0)))
    return pl.pallas_call(
        body, name=name, grid_spec=grid_spec,
        out_shape=jax.ShapeDtypeStruct((n_layers, 2 * r2, c_n), f32),
        input_output_aliases=aliases,
        compiler_params=_params("parallel"),
    )(*args)


def _adamw(w, g, m, v, layer, into, name):
    n_l, r, c_n = w.shape
    tr = _tile(r, 256, 8)

    def body(*refs):
        w_ref, g_ref, m_ref, v_ref = refs[:4]
        d_ref, mo_ref, vo_ref = refs[-3:]
        gv = g_ref[...]
        mn = ADAM_B1 * m_ref[...] + (1.0 - ADAM_B1) * gv
        vn = ADAM_B2 * v_ref[...] + (1.0 - ADAM_B2) * (gv * gv)
        m_hat = mn / (1.0 - ADAM_B1 ** ADAM_STEP)
        v_hat = vn / (1.0 - ADAM_B2 ** ADAM_STEP)
        d_ref[...] = -ADAM_LR * (m_hat / (jnp.sqrt(v_hat) + ADAM_EPS) + ADAM_WD * w_ref[...])
        mo_ref[...] = mn
        vo_ref[...] = vn

    blk = pl.BlockSpec((None, tr, c_n), lambda i: (layer, i, 0))
    shp = jax.ShapeDtypeStruct((n_l, r, c_n), f32)
    in_specs = [blk] * 4
    args = [w, g, m, v]
    aliases = {}
    if into is not None:
        in_specs = in_specs + [pl.BlockSpec(memory_space=pl.ANY)] * 3
        args += list(into)
        aliases = {4: 0, 5: 1, 6: 2}
    return pl.pallas_call(
        body, name=name, grid=(r // tr,),
        in_specs=in_specs, out_specs=[blk] * 3, out_shape=[shp] * 3,
        input_output_aliases=aliases,
        compiler_params=_params("parallel"),
    )(*args)


def kernel(x, meta, ffn1_wg, ffn1_wu, ffn1_wd, ffn2_wg, ffn2_wu, ffn2_wd, ln_gain, ln_bias, conv_w_in, conv_w, conv_w_out, kv_w, f_bias, attn_w_q, attn_w_o, loss_target, m_meta, m_ffn1_wg, m_ffn1_wu, m_ffn1_wd, m_ffn2_wg, m_ffn2_wu, m_ffn2_wd, m_ln_gain, m_ln_bias, m_conv_w_in, m_conv_w, m_conv_w_out, m_kv_w, m_f_bias, m_attn_w_q, m_attn_w_o, v_meta, v_ffn1_wg, v_ffn1_wu, v_ffn1_wd, v_ffn2_wg, v_ffn2_wu, v_ffn2_wd, v_ln_gain, v_ln_bias, v_conv_w_in, v_conv_w, v_conv_w_out, v_kv_w, v_f_bias, v_attn_w_q, v_attn_w_o):
    seq, d = x.shape[1], x.shape[2]
    t = PAD + N_META + seq
    n_heads = d // HEAD_DIM
    dq = d // N_CHIPS
    n_kv = kv_w.shape[1]
    x2 = x[0]
    target = loss_target[0]

    def rows8(a):
        return jnp.pad(a, ((0, 8 - a.shape[0]), (0, 0)))

    def small_pack(mt, g, b, cw, fb):
        fb_row = jnp.pad(fb, (0, mt.shape[1] - n_heads))[None]
        return jnp.concatenate([mt, rows8(g.reshape(6, -1)), rows8(b.reshape(6, -1)),
                                rows8(cw.reshape(3, -1)), rows8(fb_row)], axis=0)

    w_small = small_pack(meta, ln_gain, ln_bias, conv_w, f_bias)

    cx, cy, c = lax.axis_index("x"), lax.axis_index("y"), lax.axis_index("c")
    idx = jnp.stack([2 * cx + cy, 2 * (1 - cx) + cy, 2 * cx + (1 - cy), 2 * (1 - cx) + (1 - cy), c]
                    ).astype(jnp.int32)

    def tr(a):
        return a.transpose(0, 2, 1)

    w3 = {"wg1": tr(ffn1_wg), "wu1": tr(ffn1_wu), "wd1": ffn1_wd, "wg2": tr(ffn2_wg),
          "wu2": tr(ffn2_wu), "wd2": ffn2_wd, "win": conv_w_in, "wout": conv_w_out, "kv": kv_w[None],
          "wq": attn_w_q, "wo": attn_w_o, "small": w_small[None]}
    transposed = ("wg1", "wu1", "wg2", "wu2")
    buf = {n: _place_shard(w, idx, "place_shard") for n, w in w3.items()}

    def split(item):
        name, layer = item.split(".")
        return name, int(layer)

    def gather_stage(planner, items):
        triples = [(n, l, buf[n].shape[2] // 2) for n, l in map(split, items)]
        plan, n_copies = planner(triples)
        return dict(plan=plan, n=n_copies, bufs={n: buf[n] for n, _, _ in triples})

    def ici(items):
        return gather_stage(_plan_gather_ici, items)

    def d2d(items):
        return gather_stage(_plan_gather_d2d, items)

    def pair_exchange(items):
        r2s = [buf["G_" + it].shape[1] // 2 for it in items]
        plan, n_copies = _plan_pair_exchange(items, r2s)
        news = {"PAIR_" + it: jax.ShapeDtypeStruct((N_CHIPS, r2, buf["G_" + it].shape[2]), f32)
                for it, r2 in zip(items, r2s)}
        return dict(plan=plan, n=n_copies, srcs={"G_" + it: buf["G_" + it] for it in items}, news=news)

    def chip_exchange(items):
        plan, n_copies = _plan_chip_exchange(items)
        srcs = {"SEND_" + it: buf["SEND_" + it] for it in items}
        news = {"RECV_" + it: jax.ShapeDtypeStruct(s.shape, s.dtype)
                for it, s in ((it, buf["SEND_" + it]) for it in items)}
        return dict(plan=plan, n=n_copies, srcs=srcs, news=news)

    def pair_share(items):
        triples = [(n, l, buf["RED_" + n].shape[1] // 2) for n, l in map(split, items)]
        plan, n_copies = _plan_pair_share(triples)
        return dict(plan=plan, n=n_copies, bufs={"RED_" + n: buf["RED_" + n] for n, _, _ in triples})

    def run(fn, *args, stages=(), name=None):
        comm = _Copies()
        for st in (stages() if callable(stages) else stages):
            comm.add(st["plan"], st["n"], srcs=st.get("srcs"), bufs=st.get("bufs"), news=st.get("news"))
        out = _copy_call(comm, name) if fn is None else fn(*args, comm=comm)
        buf.update(comm.out_bufs)
        buf.update(comm.out_news)
        return out

    first = ["wg1.0", "wu1.0", "wd1.0", "small.0"]
    run(None, stages=[ici(first)], name="gather_first_ici")
    run(None, stages=[d2d(first)], name="gather_first_d2d")
    small = buf["small"].reshape(N_CHIPS, SMALL_ROWS, dq).transpose(1, 0, 2).reshape(SMALL_ROWS, d)
    meta_full = small[:N_META]
    gains = small[16:22].reshape(DEPTH, 3, 1, d)
    biases = small[24:30].reshape(DEPTH, 3, 1, d)
    conv_w_full = small[32:35]
    fb_pad = jnp.pad(f_bias, (0, LANES - n_heads))[None]
    conv_w8 = ["win.0", "wout.0"]
    ffn2_l0 = ["wg2.0", "wu2.0", "wd2.0", "kv.0"]
    attn_ffn2_l1 = ["wq.0", "wo.0", "wg2.1", "wu2.1", "wd2.1"]
    ffn1_l1 = ["wg1.1", "wu1.1", "wd1.1"]

    meta_pad = jnp.concatenate([jnp.zeros((PAD, d), f32), meta_full], axis=0)
    h0, h0b = run(_embed, meta_pad, x2, "embed", stages=[ici(conv_w8)])
    a1, b1, s1 = run(_ffn_up, h0b, buf["wg1"], buf["wu1"], 0, "ffn_up",
                     stages=lambda: [d2d(conv_w8), ici(ffn2_l0)])
    r1, h1, h1b = run(_down_ln, s1, buf["wd1"], 0, h0, gains[0, 0], biases[0, 0], 0.5, "ffn_down_ln",
                      stages=lambda: [d2d(ffn2_l0), ici(attn_ffn2_l1)])
    n_in = conv_w_in.shape[-1]
    w_in = buf["win"].reshape(N_CHIPS, d, n_in)
    w_out = buf["wout"].reshape(1, 1, d, d)
    p = run(_nn_matmul, h1b, w_in, f32, "conv_in", stages=lambda: [d2d(attn_ffn2_l1), ici(ffn1_l1)])
    z = _conv_fwd(p, conv_w_full, "conv_fwd")
    r2, h2, h2b = run(_down_ln, z[None], w_out, 0, h1, gains[0, 1], biases[0, 1], 1.0, "mix_out_ln",
                      stages=lambda: [d2d(ffn1_l1)])
    wg1, wu1, wd1, wg2, wu2, wd2 = (buf[n] for n in ("wg1", "wu1", "wd1", "wg2", "wu2", "wd2"))
    w_q = buf["wq"].reshape(1, d, d)
    w_o = buf["wo"].reshape(1, 1, d, d)
    kv_full = buf["kv"].reshape(N_CHIPS, d, n_kv).transpose(1, 0, 2).reshape(d, N_CHIPS * n_kv)
    w_k = kv_full[:, :d][None]
    w_v = kv_full[:, d:2 * d][None]
    w_f = jnp.pad(kv_full[:, 2 * d:], ((0, 0), (0, LANES - n_heads)))[None]
    a2, b2, s2 = _ffn_up(h2b, wg2, wu2, 0, "ffn_up")
    r3, h3, h3b = _down_ln(s2, wd2, 0, h2, gains[0, 2], biases[0, 2], 0.5, "ffn_down_ln")
    kk = _nn_matmul(h3b, w_k, bf16, "proj_bf16")
    vv = _nn_matmul(h3b, w_v, bf16, "proj_bf16")
    flog = _nn_matmul(h3b, w_f, f32, "proj_gate")
    cum = _row_scan(flog, "gate_cumsum", fb_pad)
    bk = _tile(t, 640, LANES)
    c_ht = cum[:, :n_heads].T
    c_keys = jnp.where(jnp.arange(t)[None, :] < PAD, 1e30, c_ht)
    cq_rep = jnp.broadcast_to(c_ht[:, :, None], (n_heads, t, LANES))
    ck_rep = jnp.broadcast_to(c_keys[:, :, None], (n_heads, t, LANES))
    ck_rows = c_keys.reshape(n_heads, t // bk, 1, bk)
    a3, b3, s3 = _ffn_up(h3b, wg1, wu1, 1, "ffn_up")
    r4, h4, h4b = _down_ln(s3, wd1, 1, h3, gains[1, 0], biases[1, 0], 0.5, "ffn_down_ln")
    q = _nn_matmul(h4b, w_q, bf16, "proj_bf16")
    o, o32, lse_rep = _attn_fwd(q, kk, vv, cq_rep, ck_rows, "attn_fwd")
    r5, h5, h5b = _down_ln(o[None], w_o, 0, h4, gains[1, 1], biases[1, 1], 1.0, "mix_out_ln")
    a4, b4, s4 = _ffn_up(h5b, wg2, wu2, 1, "ffn_up")
    r6, h6, _ = _down_ln(s4, wd2, 1, h5, gains[1, 2], biases[1, 2], 0.5, "ffn_down_ln")
    dy, sq = _loss_head(h6, target, "loss_head")
    loss_part = 0.5 * sq[0, 0] / d

    m_small = small_pack(m_meta, m_ln_gain, m_ln_bias, m_conv_w, m_f_bias)
    v_small = small_pack(v_meta, v_ln_gain, v_ln_bias, v_conv_w, v_f_bias)
    m3 = {"wg1": tr(m_ffn1_wg), "wu1": tr(m_ffn1_wu), "wd1": m_ffn1_wd, "wg2": tr(m_ffn2_wg),
          "wu2": tr(m_ffn2_wu), "wd2": m_ffn2_wd, "win": m_conv_w_in, "wout": m_conv_w_out,
          "kv": m_kv_w[None], "wq": m_attn_w_q, "wo": m_attn_w_o, "small": m_small[None]}
    v3 = {"wg1": tr(v_ffn1_wg), "wu1": tr(v_ffn1_wu), "wd1": v_ffn1_wd, "wg2": tr(v_ffn2_wg),
          "wu2": tr(v_ffn2_wu), "wd2": v_ffn2_wd, "win": v_conv_w_in, "wout": v_conv_w_out,
          "kv": v_kv_w[None], "wq": v_attn_w_q, "wo": v_attn_w_o, "small": v_small[None]}
    stepped = {}

    def prepare(items):
        for it in items:
            buf["SEND_" + it] = _rs_prepare(buf["G_" + it], buf["PAIR_" + it], idx, "grad_prepare")

    def finish(items):
        for it in items:
            n, l = split(it)
            buf["RED_" + n] = _rs_finish(buf["G_" + it], buf["PAIR_" + it], buf["RECV_" + it], idx, l,
                                         w3[n].shape[0], buf.get("RED_" + n), "grad_finish")

    def adam(items, grads=None):
        for it in items:
            n, l = split(it)
            g = buf["RED_" + n] if grads is None else grads[n]
            stepped[n] = _adamw(w3[n], g, m3[n], v3[n], l, stepped.get(n), "adamw")

    def ffn_bwd(dr, drb, hb_in, a, b, s, f, layer, on_act=(), after_act=None, on_dwd=(),
                after_dwd=None, on_dx=()):
        da, db = run(_ffn_bwd_act, drb, buf["wd" + f], layer, a, b, "ffn_bwd_act", stages=on_act)
        if after_act is not None:
            after_act()
        (buf[f"G_wd{f}.{layer}"],) = run(_tn_matmul, [(s, "stack", drb, "shared")], N_CHIPS, 0.5,
                                         "ffn_dwd", stages=on_dwd)
        if after_dwd is not None:
            after_dwd()
        buf[f"G_wg{f}.{layer}"], buf[f"G_wu{f}.{layer}"] = _tn_matmul(
            [(da, "stack", hb_in, "shared"), (db, "stack", hb_in, "shared")], N_CHIPS, 1.0, "ffn_dwgu")
        return run(_nt_sum, [(da, "stack", buf["wg" + f], layer, True),
                             (db, "stack", buf["wu" + f], layer, True)],
                   dr, ALPHA, N_CHIPS, f32, "ffn_dx", stages=on_dx)

    ffn2_1 = ["wg2.1", "wu2.1", "wd2.1"]
    ffn1_1 = ["wg1.1", "wu1.1", "wd1.1"]
    ffn2_0 = ["wg2.0", "wu2.0", "wd2.0"]
    ffn1_0 = ["wg1.0", "wu1.0", "wd1.0"]
    conv_items = ["wout.0", "win.0"]

    dr6, dr6b, dg12, db12 = _ln_bwd(dy, r6, gains[1, 2], "ln_bwd")
    dh5 = ffn_bwd(dr6, dr6b, h5b, a4, b4, s4, "2", 1, on_dx=lambda: [pair_exchange(ffn2_1)])
    prepare(ffn2_1)
    dr5, dr5b, dg11, db11 = _ln_bwd(dh5, r5, gains[1, 1], "ln_bwd")
    (dwo,) = _tn_matmul([(o, "shared", dr5b, "shared")], 1, 1.0, "sq_dw")
    buf["G_wo.0"] = dwo.reshape(N_CHIPS, dq, d)
    do = run(_nt_sum, [(dr5b, "cols", w_o[0], None)], None, 1.0, 1, bf16, "sq_dx_bf16",
             stages=lambda: [pair_exchange(["wo.0"])])
    prepare(["wo.0"])
    cl, delta = _attn_stats(o32, do, lse_rep, cq_rep, bk, "attn_stats")
    kt = kk.reshape(t // bk, bk, n_heads, HEAD_DIM).transpose(2, 0, 3, 1)
    dq_att, dc_q, dk, dv, dc_k = run(_attn_bwd, q, kk, vv, kt, do, ck_rep, cl, delta, "attn_bwd",
                                     stages=lambda: [chip_exchange(ffn2_1 + ["wo.0"])])
    finish(ffn2_1 + ["wo.0"])
    dc = (dc_q + dc_k).reshape(n_heads, t)
    (dwq,) = run(_tn_matmul, [(h4b, "shared", dq_att, "shared")], 1, 1.0, "sq_dw",
                 stages=lambda: [pair_share(ffn2_1 + ["wo.0"])])
    buf["G_wq.0"] = dwq.reshape(N_CHIPS, dq, d)
    adam(ffn2_1 + ["wo.0"])
    dh4 = run(_nt_sum, [(dq_att, "cols", w_q, None)], dr5, ALPHA, 1, f32, "sq_dx_res",
              stages=lambda: [pair_exchange(["wq.0"])])
    prepare(["wq.0"])
    dr4, dr4b, dg10, db10 = _ln_bwd(dh4, r4, gains[1, 0], "ln_bwd")
    dh3a = ffn_bwd(dr4, dr4b, h3b, a3, b3, s3, "1", 1,
                   on_act=lambda: [chip_exchange(["wq.0"])],
                   on_dx=lambda: [pair_exchange(ffn1_1)])
    prepare(ffn1_1)
    finish(["wq.0"])
    dc_t = jnp.pad(dc.T, ((0, 0), (0, LANES - n_heads)))
    dlogf = _row_scan(dc_t[::-1], "rev_cumsum")[::-1]
    dfl, dfb_cols = _fgate_bwd(dlogf, flog, fb_pad, "gate_bwd")
    dwk, dwv, dwf = run(_tn_matmul, [(h3b, "shared", g, "shared") for g in (dk, dv, dfl)], 1, 1.0,
                        "kv_dw",
                        stages=lambda: [chip_exchange(ffn1_1), pair_share(["wq.0"])])
    adam(["wq.0"])

    def by_chip(full):
        rows = full.shape[0]
        return full.reshape(rows, N_CHIPS, full.shape[1] // N_CHIPS).transpose(1, 0, 2)

    buf["G_kv.0"] = by_chip(jnp.concatenate([dwk[0], dwv[0], dwf[0][:, :n_heads]], axis=1))
    dh3 = run(_nt_sum, [(dk, "cols", w_k, None), (dv, "cols", w_v, None), (dfl, "cols", w_f, None)],
              dh3a, 1.0, 1, f32, "kv_dx", stages=lambda: [pair_exchange(["kv.0"])])
    prepare(["kv.0"])
    finish(ffn1_1)
    dr3, dr3b, dg02, db02 = _ln_bwd(dh3, r3, gains[0, 2], "ln_bwd")
    dh2 = ffn_bwd(dr3, dr3b, h2b, a2, b2, s2, "2", 0,
                  on_act=lambda: [chip_exchange(["kv.0"]), pair_share(ffn1_1)],
                  after_act=lambda: (adam(ffn1_1), finish(["kv.0"])),
                  on_dwd=lambda: [pair_share(["kv.0"])],
                  after_dwd=lambda: adam(["kv.0"]),
                  on_dx=lambda: [pair_exchange(ffn2_0)])
    prepare(ffn2_0)
    dr2, dr2b, dg01, db01 = _ln_bwd(dh2, r2, gains[0, 1], "ln_bwd")
    (dwout,) = _tn_matmul([(z, "shared", dr2b, "shared")], 1, 1.0, "sq_dw")
    buf["G_wout.0"] = dwout.reshape(N_CHIPS, dq, d)
    dz = _nt_sum([(dr2b, "cols", w_out[0], None)], None, 1.0, 1, f32, "sq_dx_f32")
    dp, dconv_w = run(_conv_bwd, dz, p, conv_w_full, "conv_bwd",
                      stages=lambda: [chip_exchange(ffn2_0)])
    (buf["G_win.0"],) = _tn_matmul([(h1b, "shared", dp, "cols")], N_CHIPS, 1.0, "conv_dwin")
    finish(ffn2_0)
    dh1 = run(_nt_sum, [(dp, "cols", w_in, None)], dr2, ALPHA, N_CHIPS, f32, "conv_dx",
              stages=lambda: [pair_exchange(conv_items), pair_share(ffn2_0)])
    prepare(conv_items)
    adam(ffn2_0)
    dr1, dr1b, dg00, db00 = _ln_bwd(dh1, r1, gains[0, 0], "ln_bwd")
    dh0 = ffn_bwd(dr1, dr1b, h0b, a1, b1, s1, "1", 0,
                  on_act=lambda: [chip_exchange(conv_items)],
                  after_act=lambda: finish(conv_items),
                  on_dwd=lambda: [pair_share(conv_items)],
                  after_dwd=lambda: adam(conv_items),
                  on_dx=lambda: [pair_exchange(ffn1_0)])
    prepare(ffn1_0)
    grad_x = dh0[PAD + N_META:][None]
    dmeta = dh0[PAD:PAD + N_META]
    buf["G_small.0"] = by_chip(jnp.concatenate(
        [dmeta, rows8(jnp.concatenate([dg00, dg01, dg02, dg10, dg11, dg12], axis=0)),
         rows8(jnp.concatenate([db00, db01, db02, db10, db11, db12], axis=0)),
         rows8(dconv_w), jnp.zeros((8, d), f32)], axis=0))
    run(None, stages=lambda: [chip_exchange(ffn1_0), pair_exchange(["small.0"])], name="grad_tail_1")
    prepare(["small.0"])
    finish(ffn1_0)
    run(None, stages=lambda: [chip_exchange(["small.0"]), pair_share(ffn1_0)], name="grad_tail_2")
    finish(["small.0"])
    adam(ffn1_0)
    run(None, stages=lambda: [pair_share(["small.0"])], name="grad_tail_3")

    tail = jnp.zeros((LANES,), f32).at[:n_heads].set(dfb_cols[0, :n_heads]).at[n_heads].set(loss_part)
    tail = lax.psum(tail, ("x", "y", "c"))
    loss = tail[n_heads]
    g_fb = tail[:n_heads]
    g_small = jnp.concatenate([buf["RED_small"][0, :40],
                               rows8(jnp.pad(g_fb, (0, dq - n_heads))[None])], axis=0)
    adam(["small.0"], grads={"small": g_small[None]})

    def unpack(pk):
        return (pk[:16], pk[16:22].reshape(DEPTH, 3, dq), pk[24:30].reshape(DEPTH, 3, dq),
                pk[32:35].reshape(1, 3, dq), pk[40, :n_heads])

    def order(pick):
        mt, g, b, cw, fb = unpack(pick("small")[0])
        big = {n: pick(n) for n in w3 if n != "small"}
        big["kv"] = big["kv"][0]
        for n in transposed:
            big[n] = tr(big[n])
        return [mt, big["wg1"], big["wu1"], big["wd1"], big["wg2"], big["wu2"], big["wd2"], g, b,
                big["win"], cw, big["wout"], big["kv"], fb, big["wq"], big["wo"]]

    grads_out = dict({n: buf["RED_" + n] for n in w3}, small=g_small[None])
    return (loss, grad_x, *order(lambda n: grads_out[n]), *order(lambda n: stepped[n][0]),
            *order(lambda n: stepped[n][1]), *order(lambda n: stepped[n][2]))
```

```python
import functools
import math

import jax
import jax.numpy as jnp
from jax import lax
from jax.experimental import pallas as pl
from jax.experimental.pallas import tpu as pltpu

f32 = jnp.float32
bf16 = jnp.bfloat16

N_META = 16
PAD = 112
HEAD_DIM = 128
DEPTH = 2
LN_EPS = 1e-5
ALPHA = (2 * DEPTH) ** 0.25
NEG_INF = -1e30
N_CHIPS = 4
SMALL_ROWS = 48
LANES = 128

ADAM_LR = 0.001
ADAM_B1 = 0.9
ADAM_B2 = 0.999
ADAM_EPS = 1e-08
ADAM_WD = 0.01
ADAM_STEP = 10

VMEM_LIMIT_BYTES = 56 * 1024 * 1024
ROWS_WIDE = 1664
ROWS_ACC = 1040
ROWS_ACC_LN = 832
STRIP = 16
MESH = pl.DeviceIdType.MESH

NT_DIMS = (((1,), (1,)), ((), ()))
TN_DIMS = (((0,), (0,)), ((), ()))


def _tile(n, target, mult):
    best = None
    for d in range(mult, min(n, target) + 1, mult):
        if n % d == 0:
            best = d
    assert best is not None, (n, target, mult)
    return best


def _params(*sem):
    return pltpu.CompilerParams(dimension_semantics=sem, vmem_limit_bytes=VMEM_LIMIT_BYTES)


class _Place:
    def __init__(self):
        self.cx, self.cy, self.c = lax.axis_index("x"), lax.axis_index("y"), lax.axis_index("c")
        self.chips = [(1 - self.cx, self.cy), (self.cx, 1 - self.cy), (1 - self.cx, 1 - self.cy)]
        self.me = 2 * self.cx + self.cy
        self.slots = [self.me] + [2 * px + py for px, py in self.chips]
        self.sib = (self.cx, self.cy, 1 - self.c)


class _Copies:
    def __init__(self):
        self.srcs, self.bufs, self.news = {}, {}, {}
        self.plans = []
        self.out_bufs, self.out_news = {}, {}

    def add(self, plan, n_copies, srcs=None, bufs=None, news=None):
        for have, more in ((self.srcs, srcs), (self.bufs, bufs), (self.news, news)):
            for key, val in (more or {}).items():
                assert key not in have or have[key] is val, key
                have[key] = val
        self.plans.append((plan, n_copies))

    def empty(self):
        return not self.plans

    def count(self):
        return sum(n for _, n in self.plans)

    def copies(self, src_refs, buf_refs, new_refs, send, recv):
        place = _Place()
        srcs = dict(zip(self.srcs, src_refs))
        bufs = dict(zip(self.bufs, buf_refs))
        news = dict(zip(self.news, new_refs))
        out = []
        for plan, n_copies in self.plans:
            triples = plan(srcs, bufs, news, place)
            assert len(triples) == n_copies
            for src, dst, dev in triples:
                n = len(out)
                out.append(pltpu.make_async_remote_copy(
                    src_ref=src, dst_ref=dst, send_sem=send.at[n], recv_sem=recv.at[n],
                    device_id=dev, device_id_type=MESH))
        return out

    def land(self, results):
        n_b = len(self.bufs)
        self.out_bufs = dict(zip(self.bufs, results[:n_b]))
        self.out_news = dict(zip(self.news, results[n_b:]))


def _pallas(comm, body, *, name, grid, in_specs, out_specs, out_shape, compiler_params,
            scratch_shapes=(), input_output_aliases=None):
    aliases = dict(input_output_aliases or {})
    if comm is None or comm.empty():
        return pl.pallas_call(body, name=name, grid=grid, in_specs=in_specs, out_specs=out_specs,
                              out_shape=out_shape, scratch_shapes=list(scratch_shapes),
                              input_output_aliases=aliases, compiler_params=compiler_params)
    single = not isinstance(out_shape, (list, tuple))
    out_shapes = [out_shape] if single else list(out_shape)
    out_specs_l = [out_specs] if single else list(out_specs)
    n_in, n_out, n_scr = len(in_specs), len(out_shapes), len(scratch_shapes)
    n_s, n_b, n_n = len(comm.srcs), len(comm.bufs), len(comm.news)
    n_copies = comm.count()

    def wrapped(*refs):
        ins = refs[:n_in]
        src_refs = refs[n_in:n_in + n_s]
        o0 = n_in + n_s + n_b
        outs = refs[o0:o0 + n_out]
        buf_refs = refs[o0 + n_out:o0 + n_out + n_b]
        new_refs = refs[o0 + n_out + n_b:o0 + n_out + n_b + n_n]
        rest = refs[o0 + n_out + n_b + n_n:]
        scratch, (send, recv) = rest[:n_scr], rest[n_scr:]
        ids = [pl.program_id(a) for a in range(len(grid))]
        first = functools.reduce(jnp.logical_and, [i == 0 for i in ids])
        last = functools.reduce(jnp.logical_and, [i == g - 1 for i, g in zip(ids, grid)])

        @pl.when(first)
        def _():
            for cp in comm.copies(src_refs, buf_refs, new_refs, send, recv):
                cp.start()

        body(*ins, *outs, *scratch)

        @pl.when(last)
        def _():
            for cp in comm.copies(src_refs, buf_refs, new_refs, send, recv):
                cp.wait()

    hbm = pl.BlockSpec(memory_space=pl.ANY)
    for j in range(n_b):
        aliases[n_in + n_s + j] = n_out + j
    call = pl.pallas_call(
        wrapped, name=name, grid=grid,
        in_specs=[*in_specs, *([hbm] * (n_s + n_b))],
        out_specs=[*out_specs_l, *([hbm] * (n_b + n_n))],
        out_shape=[*out_shapes,
                   *[jax.ShapeDtypeStruct(a.shape, a.dtype) for a in comm.bufs.values()],
                   *comm.news.values()],
        scratch_shapes=[*scratch_shapes, pltpu.SemaphoreType.DMA((n_copies,)),
                        pltpu.SemaphoreType.DMA((n_copies,))],
        input_output_aliases=aliases, compiler_params=compiler_params)

    def run(*args):
        res = call(*args, *comm.srcs.values(), *comm.bufs.values())
        comm.land(res[n_out:])
        return res[0] if single else res[:n_out]

    return run


def _copy_call(comm, name):
    n_s, n_b, n_n = len(comm.srcs), len(comm.bufs), len(comm.news)
    n_copies = comm.count()

    def body(*refs):
        src_refs = refs[:n_s]
        buf_refs = refs[n_s + n_b:n_s + 2 * n_b]
        new_refs = refs[n_s + 2 * n_b:n_s + 2 * n_b + n_n]
        send, recv = refs[n_s + 2 * n_b + n_n:]
        copies = comm.copies(src_refs, buf_refs, new_refs, send, recv)
        for cp in copies:
            cp.start()
        for cp in copies:
            cp.wait()

    hbm = pl.BlockSpec(memory_space=pl.ANY)
    res = pl.pallas_call(
        body, name=name,
        in_specs=[hbm] * (n_s + n_b), out_specs=[hbm] * (n_b + n_n),
        out_shape=[*[jax.ShapeDtypeStruct(a.shape, a.dtype) for a in comm.bufs.values()],
                   *comm.news.values()],
        input_output_aliases={n_s + j: j for j in range(n_b)},
        scratch_shapes=[pltpu.SemaphoreType.DMA((n_copies,)), pltpu.SemaphoreType.DMA((n_copies,))],
    )(*comm.srcs.values(), *comm.bufs.values())
    comm.land(res)


def _embed(meta_pad, x, name, comm=None):
    seq, d = x.shape
    t = seq + LANES

    def body(m_ref, x_ref, h_ref, hb_ref):
        first = pl.program_id(0) == 0
        v = jnp.where(first, m_ref[...], x_ref[...])
        h_ref[...] = v
        hb_ref[...] = v.astype(bf16)

    return _pallas(
        comm, body, name=name, grid=(t // LANES,),
        in_specs=[pl.BlockSpec((LANES, d), lambda i: (0, 0)),
                  pl.BlockSpec((LANES, d), lambda i: (jnp.maximum(i - 1, 0), 0))],
        out_specs=[pl.BlockSpec((LANES, d), lambda i: (i, 0)),
                   pl.BlockSpec((LANES, d), lambda i: (i, 0))],
        out_shape=[jax.ShapeDtypeStruct((t, d), f32), jax.ShapeDtypeStruct((t, d), bf16)],
        compiler_params=_params("parallel"),
    )(meta_pad, x)


def _nn_matmul(x, w, out_dtype, name, comm=None):
    t, k = x.shape
    s_n, _, n = w.shape
    assert s_n == 1 or n % LANES == 0
    tm = _tile(t, ROWS_WIDE, 16)

    def body(x_ref, w_ref, o_ref):
        o_ref[...] = jnp.dot(x_ref[...].astype(bf16), w_ref[...],
                             preferred_element_type=f32).astype(o_ref.dtype)

    return _pallas(
        comm, body, name=name, grid=(s_n, t // tm),
        in_specs=[pl.BlockSpec((tm, k), lambda s, i: (i, 0)),
                  pl.BlockSpec((None, k, n), lambda s, i: (s, 0, 0))],
        out_specs=pl.BlockSpec((tm, n), lambda s, i: (i, s)),
        out_shape=jax.ShapeDtypeStruct((t, s_n * n), out_dtype),
        compiler_params=_params("parallel", "parallel"),
    )(x, w)


def _ffn_up(hb, wg, wu, layer, name, comm=None):
    t, d = hb.shape
    s_n, _, n, _ = wg.shape
    tm = _tile(t, ROWS_WIDE, 16)

    def body(x_ref, wg_ref, wu_ref, a_ref, b_ref, s_ref):
        x = x_ref[...]
        a = lax.dot_general(x, wg_ref[...], NT_DIMS, preferred_element_type=f32)
        b = lax.dot_general(x, wu_ref[...], NT_DIMS, preferred_element_type=f32)
        a_ref[...] = a.astype(bf16)
        b_ref[...] = b.astype(bf16)
        s_ref[...] = (a * jax.nn.sigmoid(a) * b).astype(bf16)

    wspec = pl.BlockSpec((None, None, n, d), lambda s, i: (s, layer, 0, 0))
    ospec = pl.BlockSpec((None, tm, n), lambda s, i: (s, i, 0))
    return _pallas(
        comm, body, name=name, grid=(s_n, t // tm),
        in_specs=[pl.BlockSpec((tm, d), lambda s, i: (i, 0)), wspec, wspec],
        out_specs=[ospec, ospec, ospec],
        out_shape=[jax.ShapeDtypeStruct((s_n, t, n), bf16)] * 3,
        compiler_params=_params("parallel", "parallel"),
    )(hb, wg, wu)


def _down_ln(x, w, layer, hprev, gain, bias, beta, name, comm=None):
    s_n, t, k = x.shape
    d = w.shape[-1]
    tm = _tile(t, ROWS_ACC, 16)

    def body(x_ref, w_ref, h_ref, g_ref, b_ref, r_out, h_out, hb_out, acc):
        s = pl.program_id(1)

        @pl.when(s == 0)
        def _():
            acc[...] = jnp.zeros_like(acc)

        acc[...] += jnp.dot(x_ref[...], w_ref[...], preferred_element_type=f32)

        @pl.when(s == s_n - 1)
        def _():
            r = ALPHA * h_ref[...] + beta * acc[...]
            mu = jnp.mean(r, axis=-1, keepdims=True)
            xc = r - mu
            var = jnp.mean(xc * xc, axis=-1, keepdims=True)
            y = xc * lax.rsqrt(var + LN_EPS) * g_ref[...] + b_ref[...]
            r_out[...] = r
            h_out[...] = y
            hb_out[...] = y.astype(bf16)

    row = pl.BlockSpec((tm, d), lambda i, s: (i, 0))
    vec = pl.BlockSpec((1, d), lambda i, s: (0, 0))
    return _pallas(
        comm, body, name=name, grid=(t // tm, s_n),
        in_specs=[pl.BlockSpec((None, tm, k), lambda i, s: (s, i, 0)),
                  pl.BlockSpec((None, None, k, d), lambda i, s: (s, layer, 0, 0)),
                  row, vec, vec],
        out_specs=[row, row, row],
        out_shape=[jax.ShapeDtypeStruct((t, d), f32), jax.ShapeDtypeStruct((t, d), f32),
                   jax.ShapeDtypeStruct((t, d), bf16)],
        scratch_shapes=[pltpu.VMEM((tm, d), f32)],
        compiler_params=_params("parallel", "arbitrary"),
    )(x, w, hprev, gain, bias)


def _conv_fwd(p, conv_w, name):
    t, d3 = p.shape
    d = d3 // 3
    tm = _tile(t, 320, 8)
    hb = tm // 8

    def body(p_ref, prev_ref, w_ref, z_ref):
        i = pl.program_id(0)
        rows = i * tm - 8 + lax.broadcasted_iota(jnp.int32, (tm + 8, 1), 0)
        cg = jnp.concatenate([prev_ref[:, d:2 * d], p_ref[:, d:2 * d]], axis=0)
        val = jnp.concatenate([prev_ref[:, 2 * d:], p_ref[:, 2 * d:]], axis=0)
        u = jnp.where(rows >= PAD, cg * val, 0.0)
        y = (w_ref[2:3, :] * u + w_ref[1:2, :] * pltpu.roll(u, 1, 0)
             + w_ref[0:1, :] * pltpu.roll(u, 2, 0))
        z_ref[...] = (p_ref[:, :d] * y[8:]).astype(bf16)

    return pl.pallas_call(
        body, name=name, grid=(t // tm,),
        in_specs=[pl.BlockSpec((tm, d3), lambda i: (i, 0)),
                  pl.BlockSpec((8, d3), lambda i: (jnp.maximum(i * hb - 1, 0), 0)),
                  pl.BlockSpec((3, d), lambda i: (0, 0))],
        out_specs=pl.BlockSpec((tm, d), lambda i: (i, 0)),
        out_shape=jax.ShapeDtypeStruct((t, d), bf16),
        compiler_params=_params("parallel"),
    )(p, p, conv_w)


def _row_scan(x, name, fbias=None):
    t, n = x.shape
    blk = LANES
    gate = fbias is not None

    def body(*refs):
        if gate:
            x_ref, fb_ref, o_ref, carry = refs
        else:
            x_ref, o_ref, carry = refs
        i = pl.program_id(0)

        @pl.when(i == 0)
        def _():
            carry[...] = jnp.zeros_like(carry)

        v = x_ref[...]
        r = lax.broadcasted_iota(jnp.int32, (blk, n), 0)
        if gate:
            v = v + fb_ref[...]
            v = jnp.minimum(v, 0.0) - jnp.log1p(jnp.exp(-jnp.abs(v)))
            v = jnp.where(i * blk + r >= PAD, v, 0.0)
        sh = 1
        while sh < blk:
            v = v + jnp.where(r >= sh, pltpu.roll(v, sh, 0), 0.0)
            sh *= 2
        v = v + carry[...]
        o_ref[...] = v
        carry[...] = o_ref[blk - 1:blk, :]

    in_specs = [pl.BlockSpec((blk, n), lambda i: (i, 0))]
    args = [x]
    if gate:
        in_specs.append(pl.BlockSpec((1, n), lambda i: (0, 0)))
        args.append(fbias)
    return pl.pallas_call(
        body, name=name, grid=(t // blk,),
        in_specs=in_specs,
        out_specs=pl.BlockSpec((blk, n), lambda i: (i, 0)),
        out_shape=jax.ShapeDtypeStruct((t, n), f32),
        scratch_shapes=[pltpu.VMEM((1, n), f32)],
        compiler_params=_params("arbitrary"),
    )(*args)


def _lanes(x, n):
    return jnp.concatenate([x] * (n // LANES), axis=1)


def _attn_fwd(q, k, v, cq_rep, ck_rows, name):
    t, d = q.shape
    n_heads = d // HEAD_DIM
    bk = ck_rows.shape[-1]
    bq = bk
    scale = 1.0 / math.sqrt(HEAD_DIM)

    def lane_fold(x, op):
        out = x[:, :LANES]
        for c0 in range(LANES, bk, LANES):
            out = op(out, x[:, c0:c0 + LANES])
        return out

    def body(q_ref, k_ref, v_ref, cq_ref, ck_ref, o_ref, o32_ref, lse_ref,
             s_scr, p_scr, m_scr, l_scr, red_scr, acc_scr):
        i = pl.program_id(1)
        m_scr[...] = jnp.full_like(m_scr, NEG_INF)
        l_scr[...] = jnp.zeros_like(l_scr)
        acc_scr[...] = jnp.zeros_like(acc_scr)
        qb = q_ref[...]
        ahead = (lax.broadcasted_iota(jnp.int32, (STRIP, bk), 1)
                 - lax.broadcasted_iota(jnp.int32, (STRIP, bk), 0))

        def tile(j, diagonal):
            k0 = pl.multiple_of(j * bk, bk)
            s_scr[...] = lax.dot_general(qb, k_ref[pl.ds(k0, bk), :], NT_DIMS,
                                         preferred_element_type=f32)
            ck = ck_ref[j]
            for r in range(0, bq, STRIP):
                rows = slice(r, r + STRIP)
                s = s_scr[rows, :] * scale + _lanes(cq_ref[rows, :], bk) - ck
                if diagonal:
                    s = jnp.where(ahead <= r, s, NEG_INF)
                s_scr[rows, :] = s
                red_scr[rows, :] = lane_fold(s, jnp.maximum)
            m_old = m_scr[...]
            m_new = jnp.maximum(m_old, jnp.broadcast_to(
                jnp.max(red_scr[...], axis=1, keepdims=True), (bq, LANES)))
            a = jnp.exp(m_old - m_new)
            m_scr[...] = m_new
            for r in range(0, bq, STRIP):
                rows = slice(r, r + STRIP)
                pr = jnp.exp(s_scr[rows, :] - _lanes(m_scr[rows, :], bk))
                red_scr[rows, :] = lane_fold(pr, jnp.add)
                p_scr[rows, :] = pr.astype(bf16)
            l_scr[...] = a * l_scr[...] + jnp.broadcast_to(
                jnp.sum(red_scr[...], axis=1, keepdims=True), (bq, LANES))
            acc_scr[...] = a * acc_scr[...] + jnp.dot(
                p_scr[...], v_ref[pl.ds(k0, bk), :], preferred_element_type=f32)

        def full_tile(j, carry):
            tile(j, False)
            return carry

        lax.fori_loop(0, i, full_tile, 0)
        tile(i, True)
        out = acc_scr[...] / l_scr[...]
        o_ref[...] = out.astype(bf16)
        o32_ref[...] = out
        lse_ref[...] = m_scr[...] + jnp.log(l_scr[...])

    qblk = pl.BlockSpec((bq, HEAD_DIM), lambda h, i: (i, h))
    head_rows = pl.BlockSpec((t, HEAD_DIM), lambda h, i: (0, h))
    rep = pl.BlockSpec((None, bq, LANES), lambda h, i: (h, i, 0))
    col = pltpu.VMEM((bq, LANES), f32)
    return pl.pallas_call(
        body, name=name, grid=(n_heads, t // bq),
        in_specs=[qblk, head_rows, head_rows, rep,
                  pl.BlockSpec((None, t // bk, 1, bk), lambda h, i: (h, 0, 0, 0))],
        out_specs=[qblk, qblk, rep],
        out_shape=[jax.ShapeDtypeStruct((t, d), bf16), jax.ShapeDtypeStruct((t, d), f32),
                   jax.ShapeDtypeStruct((n_heads, t, LANES), f32)],
        scratch_shapes=[pltpu.VMEM((bq, bk), f32), pltpu.VMEM((bq, bk), bf16), col, col, col,
                        pltpu.VMEM((bq, HEAD_DIM), f32)],
        compiler_params=_params("parallel", "parallel"),
    )(q, k, v, cq_rep, ck_rows)


def _loss_head(h, target, name):
    t, d = h.shape

    def body(h_ref, t_ref, dy_ref, loss_ref):
        i = pl.program_id(0)

        @pl.when(i == 0)
        def _():
            loss_ref[...] = jnp.zeros_like(loss_ref)

        diff = jnp.where(i >= 1, h_ref[...] - t_ref[...], 0.0)
        dy_ref[...] = diff * (1.0 / d)
        loss_ref[...] += jnp.sum(diff * diff)

    return pl.pallas_call(
        body, name=name, grid=(t // LANES,),
        in_specs=[pl.BlockSpec((LANES, d), lambda i: (i, 0)),
                  pl.BlockSpec((LANES, d), lambda i: (jnp.maximum(i - 1, 0), 0))],
        out_specs=[pl.BlockSpec((LANES, d), lambda i: (i, 0)),
                   pl.BlockSpec((1, LANES), lambda i: (0, 0))],
        out_shape=[jax.ShapeDtypeStruct((t, d), f32), jax.ShapeDtypeStruct((1, LANES), f32)],
        compiler_params=_params("arbitrary"),
    )(h, target)


def _ln_bwd_rows(dy, rr, gain):
    mu = jnp.mean(rr, axis=-1, keepdims=True)
    xc = rr - mu
    var = jnp.mean(xc * xc, axis=-1, keepdims=True)
    rstd = lax.rsqrt(var + LN_EPS)
    xhat = xc * rstd
    dxh = dy * gain
    m1 = jnp.mean(dxh, axis=-1, keepdims=True)
    m2 = jnp.mean(dxh * xhat, axis=-1, keepdims=True)
    dr = rstd * (dxh - m1 - xhat * m2)
    return dr, jnp.sum(dy * xhat, axis=0, keepdims=True), jnp.sum(dy, axis=0, keepdims=True)


def _ln_bwd(dh, r, gain, name):
    t, d = r.shape
    tm = _tile(t, 640, 16)

    def body(dh_ref, r_ref, g_ref, dr_ref, drb_ref, dg_ref, db_ref):
        @pl.when(pl.program_id(0) == 0)
        def _():
            dg_ref[...] = jnp.zeros_like(dg_ref)
            db_ref[...] = jnp.zeros_like(db_ref)

        dr, dg, db = _ln_bwd_rows(dh_ref[...], r_ref[...], g_ref[...])
        dr_ref[...] = dr
        drb_ref[...] = dr.astype(bf16)
        dg_ref[...] += dg
        db_ref[...] += db

    row = pl.BlockSpec((tm, d), lambda i: (i, 0))
    vec = pl.BlockSpec((1, d), lambda i: (0, 0))
    return pl.pallas_call(
        body, name=name, grid=(t // tm,),
        in_specs=[row, row, vec],
        out_specs=[row, row, vec, vec],
        out_shape=[jax.ShapeDtypeStruct((t, d), f32), jax.ShapeDtypeStruct((t, d), bf16),
                   jax.ShapeDtypeStruct((1, d), f32), jax.ShapeDtypeStruct((1, d), f32)],
        compiler_params=_params("arbitrary"),
    )(dh, r, gain)


def _ffn_bwd_act(drb, wd, layer, a, b, name, comm=None):
    t, d = drb.shape
    s_n, _, n = a.shape
    tm = _tile(t, ROWS_WIDE, 16)

    def body(dr_ref, w_ref, a_ref, b_ref, da_ref, db_ref):
        ds = 0.5 * lax.dot_general(dr_ref[...], w_ref[...], NT_DIMS, preferred_element_type=f32)
        av = a_ref[...].astype(f32)
        sig = jax.nn.sigmoid(av)
        da_ref[...] = (ds * b_ref[...].astype(f32) * (sig * (1.0 + av * (1.0 - sig)))).astype(bf16)
        db_ref[...] = (ds * (av * sig)).astype(bf16)

    act = pl.BlockSpec((None, tm, n), lambda s, i: (s, i, 0))
    return _pallas(
        comm, body, name=name, grid=(s_n, t // tm),
        in_specs=[pl.BlockSpec((tm, d), lambda s, i: (i, 0)),
                  pl.BlockSpec((None, None, n, d), lambda s, i: (s, layer, 0, 0)), act, act],
        out_specs=[act, act],
        out_shape=[jax.ShapeDtypeStruct((s_n, t, n), bf16), jax.ShapeDtypeStruct((s_n, t, n), bf16)],
        compiler_params=_params("parallel", "parallel"),
    )(drb, wd, a, b)


def _act_spec(mode, tt, k, t_first):
    def fix(fn):
        return (lambda i, s: fn(s, i)) if t_first else fn
    if mode == "shared":
        return pl.BlockSpec((tt, k), fix(lambda s, i: (i, 0)))
    if mode == "cols":
        return pl.BlockSpec((tt, k), fix(lambda s, i: (i, s)))
    assert mode == "stack"
    return pl.BlockSpec((None, tt, k), fix(lambda s, i: (s, i, 0)))


def _act_width(arr, mode, s_n):
    return arr.shape[-1] // s_n if mode == "cols" else arr.shape[-1]


def _tn_matmul(pairs, s_n, scale, name, comm=None):
    t = pairs[0][0].shape[-2]
    tt = _tile(t, 2080, 16)
    n_t = t // tt
    arrays, specs, where = [], [], []
    for x, xmode, y, ymode in pairs:
        pos = []
        for arr, mode in ((x, xmode), (y, ymode)):
            hit = [j for j, a in enumerate(arrays) if a is arr]
            if not hit:
                arrays.append(arr)
                specs.append(_act_spec(mode, tt, _act_width(arr, mode, s_n), False))
                hit = [len(arrays) - 1]
            pos.append(hit[0])
        where.append(pos)
    widths = [(_act_width(x, xm, s_n), _act_width(y, ym, s_n)) for x, xm, y, ym in pairs]
    n_a = len(arrays)

    def body(*refs):
        i = pl.program_id(1)
        for (px, py), o_ref in zip(where, refs[n_a:]):
            part = lax.dot_general(refs[px][...].astype(bf16), refs[py][...].astype(bf16), TN_DIMS,
                                   preferred_element_type=f32)

            @pl.when(i == 0)
            def _():
                o_ref[...] = part

            @pl.when(i > 0)
            def _():
                o_ref[...] += part

            if scale != 1.0:
                @pl.when(i == n_t - 1)
                def _():
                    o_ref[...] = o_ref[...] * scale

    return _pallas(
        comm, body, name=name, grid=(s_n, n_t),
        in_specs=specs,
        out_specs=[pl.BlockSpec((None, kx, ky), lambda s, i: (s, 0, 0)) for kx, ky in widths],
        out_shape=[jax.ShapeDtypeStruct((s_n, kx, ky), f32) for kx, ky in widths],
        compiler_params=_params("parallel", "arbitrary"),
    )(*arrays)


def _nt_sum(pairs, base, base_scale, s_n, out_dtype, name, comm=None, ln=None):
    t = pairs[0][0].shape[-2]
    pairs = [(*pr, False)[:5] for pr in pairs]
    d = pairs[0][2].shape[-1] if pairs[0][4] else pairs[0][2].shape[-2]
    tm = _tile(t, ROWS_ACC if ln is None else ROWS_ACC_LN, 16)
    n_p = len(pairs)
    has_base = base is not None
    flipped = [pr[4] for pr in pairs]
    n_out = 1 if ln is None else 4

    def body(*refs):
        dy_refs = refs[0:2 * n_p:2]
        w_refs = refs[1:2 * n_p:2]
        rest = refs[2 * n_p:]
        base_ref = rest[0] if has_base else None
        o_refs, acc = rest[-1 - n_out:-1], rest[-1]
        s = pl.program_id(1)

        if ln is not None:
            @pl.when(jnp.logical_and(pl.program_id(0) == 0, s == 0))
            def _():
                o_refs[2][...] = jnp.zeros_like(o_refs[2])
                o_refs[3][...] = jnp.zeros_like(o_refs[3])

        @pl.when(s == 0)
        def _():
            acc[...] = jnp.zeros_like(acc)

        tot = None
        for dy_ref, w_ref, flip in zip(dy_refs, w_refs, flipped):
            dyv = dy_ref[...].astype(bf16)
            if flip:
                part = jnp.dot(dyv, w_ref[...], preferred_element_type=f32)
            else:
                part = lax.dot_general(dyv, w_ref[...], NT_DIMS, preferred_element_type=f32)
            tot = part if tot is None else tot + part
        acc[...] += tot

        @pl.when(s == s_n - 1)
        def _():
            res = acc[...]
            if has_base:
                res = base_scale * base_ref[...] + res
            if ln is None:
                o_refs[0][...] = res.astype(o_refs[0].dtype)
            else:
                r_ref, g_ref = rest[-7], rest[-6]
                dr, dg, db = _ln_bwd_rows(res, r_ref[...], g_ref[...])
                o_refs[0][...] = dr
                o_refs[1][...] = dr.astype(bf16)
                o_refs[2][...] += dg
                o_refs[3][...] += db

    in_specs, args = [], []
    for dy, mode, w, layer, flip in pairs:
        k = _act_width(dy, mode, s_n)
        in_specs.append(_act_spec(mode, tm, k, True))
        wshape = (k, d) if flip else (d, k)
        if layer is None:
            in_specs.append(pl.BlockSpec((None, *wshape), lambda i, s: (s, 0, 0)))
        else:
            in_specs.append(pl.BlockSpec((None, None, *wshape),
                                         functools.partial(lambda i, s, l: (s, l, 0, 0), l=layer)))
        args += [dy, w]
    row = pl.BlockSpec((tm, d), lambda i, s: (i, 0))
    vec = pl.BlockSpec((1, d), lambda i, s: (0, 0))
    if has_base:
        in_specs.append(row)
        args.append(base)
    if ln is None:
        out_specs = row
        out_shape = jax.ShapeDtypeStruct((t, d), out_dtype)
    else:
        in_specs += [row, vec]
        args += list(ln)
        out_specs = [row, row, vec, vec]
        out_shape = [jax.ShapeDtypeStruct((t, d), f32), jax.ShapeDtypeStruct((t, d), bf16),
                     jax.ShapeDtypeStruct((1, d), f32), jax.ShapeDtypeStruct((1, d), f32)]
    return _pallas(
        comm, body, name=name, grid=(t // tm, s_n),
        in_specs=in_specs, out_specs=out_specs, out_shape=out_shape,
        scratch_shapes=[pltpu.VMEM((tm, d), f32)],
        compiler_params=_params("parallel" if ln is None else "arbitrary", "arbitrary"),
    )(*args)


def _conv_bwd(dz, p, conv_w, name, comm=None):
    t, d3 = p.shape
    d = d3 // 3
    tm = _tile(t, 320, 8)
    hb = tm // 8
    last8 = t // 8 - 1
    n_ext = tm + 8

    def body(dz_ref, dzn_ref, p_ref, pp_ref, pn_ref, w_ref, dp_ref, dw_ref):
        i = pl.program_id(0)

        @pl.when(i == 0)
        def _():
            dw_ref[...] = jnp.zeros_like(dw_ref)

        w0, w1, w2 = w_ref[0:1, :], w_ref[1:2, :], w_ref[2:3, :]
        rows_u = i * tm - 8 + lax.broadcasted_iota(jnp.int32, (n_ext, 1), 0)
        cg = jnp.concatenate([pp_ref[:, d:2 * d], p_ref[:, d:2 * d]], axis=0)
        val = jnp.concatenate([pp_ref[:, 2 * d:], p_ref[:, 2 * d:]], axis=0)
        u = jnp.where(rows_u >= PAD, cg * val, 0.0)
        u1 = pltpu.roll(u, 1, 0)
        u2 = pltpu.roll(u, 2, 0)
        y = (w2 * u + w1 * u1 + w0 * u2)[8:]
        dzv = dz_ref[...]
        bg = p_ref[:, :d]
        rows_n = (i + 1) * tm + lax.broadcasted_iota(jnp.int32, (8, 1), 0)
        dy_main = dzv * bg
        dy_next = jnp.where(rows_n < t, dzn_ref[...] * pn_ref[:, :d], 0.0)
        dye = jnp.concatenate([dy_main, dy_next], axis=0)
        du = (w2 * dye + w1 * pltpu.roll(dye, n_ext - 1, 0)
              + w0 * pltpu.roll(dye, n_ext - 2, 0))[:tm]
        du = jnp.where(rows_u[8:] >= PAD, du, 0.0)
        dp_ref[:, :d] = (dzv * y).astype(bf16)
        dp_ref[:, d:2 * d] = (du * val[8:]).astype(bf16)
        dp_ref[:, 2 * d:] = (du * cg[8:]).astype(bf16)
        dw_ref[0:1, :] += jnp.sum(dy_main * u2[8:], axis=0, keepdims=True)
        dw_ref[1:2, :] += jnp.sum(dy_main * u1[8:], axis=0, keepdims=True)
        dw_ref[2:3, :] += jnp.sum(dy_main * u[8:], axis=0, keepdims=True)

    nxt = lambda i: (jnp.minimum((i + 1) * hb, last8), 0)
    return _pallas(
        comm, body, name=name, grid=(t // tm,),
        in_specs=[pl.BlockSpec((tm, d), lambda i: (i, 0)),
                  pl.BlockSpec((8, d), nxt),
                  pl.BlockSpec((tm, d3), lambda i: (i, 0)),
                  pl.BlockSpec((8, d3), lambda i: (jnp.maximum(i * hb - 1, 0), 0)),
                  pl.BlockSpec((8, d3), nxt),
                  pl.BlockSpec((3, d), lambda i: (0, 0))],
        out_specs=[pl.BlockSpec((tm, d3), lambda i: (i, 0)),
                   pl.BlockSpec((3, d), lambda i: (0, 0))],
        out_shape=[jax.ShapeDtypeStruct((t, d3), bf16), jax.ShapeDtypeStruct((3, d), f32)],
        compiler_params=_params("arbitrary"),
    )(dz, dz, p, p, p, conv_w)


def _attn_stats(o, do, lse_rep, cq_rep, bq, name):
    t, d = o.shape
    n_heads = d // HEAD_DIM

    def body(o_ref, do_ref, lse_ref, cq_ref, cl_ref, delta_ref):
        for c0 in range(0, bq, LANES):
            rows = slice(c0, c0 + LANES)
            cl_ref[:, rows] = (cq_ref[rows, :] - lse_ref[rows, :]).T[0:1, :]
            prod = o_ref[rows, :] * do_ref[rows, :].astype(f32)
            delta_ref[:, rows] = jnp.sum(prod.T, axis=0, keepdims=True)

    qblk = pl.BlockSpec((bq, HEAD_DIM), lambda h, i: (i, h))
    rep = pl.BlockSpec((None, bq, LANES), lambda h, i: (h, i, 0))
    row = pl.BlockSpec((None, None, 1, bq), lambda h, i: (h, i, 0, 0))
    shp = jax.ShapeDtypeStruct((n_heads, t // bq, 1, bq), f32)
    return pl.pallas_call(
        body, name=name, grid=(n_heads, t // bq),
        in_specs=[qblk, qblk, rep, rep],
        out_specs=[row, row], out_shape=[shp, shp],
        compiler_params=_params("parallel", "parallel"),
    )(o, do, lse_rep, cq_rep)


def _attn_bwd(q, k, v, kt, do, ckey, cl_rows, delta_rows, name, comm=None):
    t, d = q.shape
    n_heads = d // HEAD_DIM
    bk = kt.shape[-1]
    bq = bk
    n_kv = t // bk
    n_q = t // bq
    scale = 1.0 / math.sqrt(HEAD_DIM)

    def body(q_ref, do_ref, cl_ref, dl_ref, k_ref, v_ref, kt_ref, ck_ref,
             dq_ref, dcq_ref, dk_ref, dv_ref, dck_ref,
             st_scr, dp_scr, p_scr, ds_scr, dqt, dk_acc, dv_acc, dck_acc):
        j = pl.program_id(1)

        @pl.when(j == 0)
        def _():
            dqt[...] = jnp.zeros_like(dqt)
            dcq_ref[...] = jnp.zeros_like(dcq_ref)

        dk_acc[...] = jnp.zeros_like(dk_acc)
        dv_acc[...] = jnp.zeros_like(dv_acc)
        dck_acc[...] = jnp.zeros_like(dck_acc)
        kb = k_ref[...]
        vb = v_ref[...]
        behind = (lax.broadcasted_iota(jnp.int32, (STRIP, bq), 1)
                  - lax.broadcasted_iota(jnp.int32, (STRIP, bq), 0))

        def tile(i, diagonal):
            r0 = pl.multiple_of(i * bq, bq)
            qi = q_ref[pl.ds(r0, bq), :]
            doi = do_ref[pl.ds(r0, bq), :]
            st_scr[...] = lax.dot_general(kb, qi, NT_DIMS, preferred_element_type=f32)
            dp_scr[...] = lax.dot_general(vb, doi, NT_DIMS, preferred_element_type=f32)
            cl = cl_ref[i]
            dl = dl_ref[i]
            over_keys = jnp.zeros((STRIP, bq), f32)
            for r in range(0, bk, STRIP):
                keys = slice(r, r + STRIP)
                st = st_scr[keys, :] * scale + cl - _lanes(ck_ref[keys, :], bq)
                if diagonal:
                    st = jnp.where(behind >= r, st, NEG_INF)
                pr = jnp.exp(st)
                ds = pr * (dp_scr[keys, :] - dl)
                over_keys = over_keys + ds
                dck_acc[keys, :] -= jnp.sum(ds, axis=1, keepdims=True)
                p_scr[keys, :] = pr.astype(bf16)
                ds_scr[keys, :] = ds.astype(bf16)
            dcq_ref[i] += jnp.sum(over_keys, axis=0, keepdims=True)
            dv_acc[...] += jnp.dot(p_scr[...], doi, preferred_element_type=f32)
            dk_acc[...] += jnp.dot(ds_scr[...], qi, preferred_element_type=f32)
            dqt[i] += jnp.dot(kt_ref[...], ds_scr[...], preferred_element_type=f32)

        def full_tile(i, carry):
            tile(i, False)
            return carry

        tile(j, True)
        lax.fori_loop(j + 1, n_q, full_tile, 0)
        dk_ref[...] = (dk_acc[...] * scale).astype(bf16)
        dv_ref[...] = dv_acc[...].astype(bf16)
        for c0 in range(0, bk, LANES):
            keys = slice(c0, c0 + LANES)
            dck_ref[:, keys] = jnp.broadcast_to(dck_acc[keys, :], (LANES, LANES)).T[0:1, :]

        @pl.when(j == n_kv - 1)
        def _():
            def emit(i, carry):
                r0 = pl.multiple_of(i * bq, bq)
                dq_ref[pl.ds(r0, bq), :] = dqt[i].T * scale
                return carry
            lax.fori_loop(0, n_q, emit, 0)

    head_rows = pl.BlockSpec((t, HEAD_DIM), lambda h, j: (0, h))
    head_stat = pl.BlockSpec((None, n_q, 1, bq), lambda h, j: (h, 0, 0, 0))
    kblk = pl.BlockSpec((bk, HEAD_DIM), lambda h, j: (j, h))
    return _pallas(
        comm, body, name=name, grid=(n_heads, n_kv),
        in_specs=[head_rows, head_rows, head_stat, head_stat, kblk, kblk,
                  pl.BlockSpec((None, None, HEAD_DIM, bk), lambda h, j: (h, j, 0, 0)),
                  pl.BlockSpec((None, bk, LANES), lambda h, j: (h, j, 0))],
        out_specs=[head_rows, head_stat, kblk, kblk,
                   pl.BlockSpec((None, None, 1, bk), lambda h, j: (h, j, 0, 0))],
        out_shape=[jax.ShapeDtypeStruct((t, d), f32),
                   jax.ShapeDtypeStruct((n_heads, n_q, 1, bq), f32),
                   jax.ShapeDtypeStruct((t, d), bf16), jax.ShapeDtypeStruct((t, d), bf16),
                   jax.ShapeDtypeStruct((n_heads, n_kv, 1, bk), f32)],
        scratch_shapes=[pltpu.VMEM((bk, bq), f32), pltpu.VMEM((bk, bq), f32),
                        pltpu.VMEM((bk, bq), bf16), pltpu.VMEM((bk, bq), bf16),
                        pltpu.VMEM((n_q, HEAD_DIM, bq), f32),
                        pltpu.VMEM((bk, HEAD_DIM), f32), pltpu.VMEM((bk, HEAD_DIM), f32),
                        pltpu.VMEM((bk, 1), f32)],
        compiler_params=_params("parallel", "arbitrary"),
    )(q, do, cl_rows, delta_rows, k, v, kt, ckey)


def _fgate_bwd(dlogf, flog, fbias, name):
    t, n = flog.shape

    def body(dl_ref, fl_ref, fb_ref, o_ref, sum_ref):
        i = pl.program_id(0)

        @pl.when(i == 0)
        def _():
            sum_ref[...] = jnp.zeros_like(sum_ref)

        r = i * LANES + lax.broadcasted_iota(jnp.int32, (LANES, n), 0)
        g = dl_ref[...] * jax.nn.sigmoid(-(fl_ref[...] + fb_ref[...]))
        g = jnp.where(r >= PAD, g, 0.0)
        o_ref[...] = g
        sum_ref[...] += jnp.sum(g, axis=0, keepdims=True)

    blk = pl.BlockSpec((LANES, n), lambda i: (i, 0))
    vec = pl.BlockSpec((1, n), lambda i: (0, 0))
    return pl.pallas_call(
        body, name=name, grid=(t // LANES,),
        in_specs=[blk, blk, vec], out_specs=[blk, vec],
        out_shape=[jax.ShapeDtypeStruct((t, n), f32), jax.ShapeDtypeStruct((1, n), f32)],
        compiler_params=_params("arbitrary"),
    )(dlogf, flog, fbias)


def _place_shard(w, idx, name):
    n_l, r, c_n = w.shape
    out_dtype = bf16 if r * c_n > 2 ** 16 else w.dtype
    tr = _tile(r, 512, 16) if r % 16 == 0 else r

    def body(idx_ref, w_ref, o_ref):
        o_ref[...] = w_ref[...].astype(out_dtype)

    grid_spec = pltpu.PrefetchScalarGridSpec(
        num_scalar_prefetch=1, grid=(n_l, r // tr),
        in_specs=[pl.BlockSpec((None, tr, c_n), lambda l, i, idx: (l, i, 0))],
        out_specs=pl.BlockSpec((None, None, tr, c_n), lambda l, i, idx: (idx[0], l, i, 0)))
    return pl.pallas_call(
        body, name=name, grid_spec=grid_spec,
        out_shape=jax.ShapeDtypeStruct((N_CHIPS, n_l, r, c_n), out_dtype),
        compiler_params=_params("parallel", "parallel"),
    )(idx, w)


def _plan_gather_ici(items):
    def plan(srcs, bufs, news, p):
        out = []
        for name, layer, r2 in items:
            mine = bufs[name].at[p.me, layer, pl.ds(p.c * r2, r2)]
            out += [(mine, mine, (*chip, p.c)) for chip in p.chips]
        return out
    return plan, 3 * len(items)


def _plan_gather_d2d(items):
    def plan(srcs, bufs, news, p):
        out = []
        for name, layer, r2 in items:
            for px, py in p.chips:
                landed = bufs[name].at[2 * px + py, layer, pl.ds(p.c * r2, r2)]
                out.append((landed, landed, p.sib))
        return out
    return plan, 3 * len(items)


def _plan_pair_exchange(names, r2s):
    def plan(srcs, bufs, news, p):
        out = []
        for name, r2 in zip(names, r2s):
            for k, slot in enumerate(p.slots):
                out.append((srcs["G_" + name].at[slot, pl.ds((1 - p.c) * r2, r2)],
                            news["PAIR_" + name].at[k], p.sib))
        return out
    return plan, 4 * len(names)


def _plan_chip_exchange(names):
    def plan(srcs, bufs, news, p):
        out = []
        for name in names:
            for k, chip in enumerate(p.chips):
                out.append((srcs["SEND_" + name].at[k], news["RECV_" + name].at[k], (*chip, p.c)))
        return out
    return plan, 3 * len(names)


def _plan_pair_share(items):
    def plan(srcs, bufs, news, p):
        out = []
        for name, layer, r2 in items:
            mine = bufs["RED_" + name].at[layer, pl.ds(p.c * r2, r2)]
            out.append((mine, mine, p.sib))
        return out
    return plan, len(items)


def _rs_prepare(g, pair, idx, name):
    _, r2, c_n = pair.shape
    tr = _tile(r2, 256, 8)
    nb = r2 // tr

    def body(idx_ref, g_ref, p_ref, o_ref):
        o_ref[...] = (g_ref[...] + p_ref[...]).astype(bf16)

    grid_spec = pltpu.PrefetchScalarGridSpec(
        num_scalar_prefetch=1, grid=(3, nb),
        in_specs=[pl.BlockSpec((None, tr, c_n), lambda k, i, idx: (idx[k + 1], idx[4] * nb + i, 0)),
                  pl.BlockSpec((None, tr, c_n), lambda k, i, idx: (k + 1, i, 0))],
        out_specs=pl.BlockSpec((None, tr, c_n), lambda k, i, idx: (k, i, 0)))
    return pl.pallas_call(
        body, name=name, grid_spec=grid_spec,
        out_shape=jax.ShapeDtypeStruct((3, r2, c_n), bf16),
        compiler_params=_params("parallel", "parallel"),
    )(idx, g, pair)


def _rs_finish(g, pair, recv, idx, layer, n_layers, into, name):
    _, r2, c_n = pair.shape
    tr = _tile(r2, 256, 8)
    nb = r2 // tr

    def body(*refs):
        g_ref, p_ref, r0_ref, r1_ref, r2_ref = refs[1:6]
        o_ref = refs[-1]
        acc = g_ref[...] + p_ref[...]
        acc = acc + r0_ref[...].astype(f32)
        acc = acc + r1_ref[...].astype(f32)
        acc = acc + r2_ref[...].astype(f32)
        o_ref[...] = acc

    def rspec(k):
        return pl.BlockSpec((None, tr, c_n), functools.partial(lambda i, idx, kk: (kk, i, 0), kk=k))

    in_specs = [pl.BlockSpec((None, tr, c_n), lambda i, idx: (idx[0], idx[4] * nb + i, 0)),
                rspec(0), rspec(0), rspec(1), rspec(2)]
    args = [idx, g, pair, recv, recv, recv]
    aliases = {}
    if into is not None:
        in_specs.append(pl.BlockSpec(memory_space=pl.ANY))
        args.append(into)
        aliases = {6: 0}
    grid_spec = pltpu.PrefetchScalarGridSpec(
        num_scalar_prefetch=1, grid=(nb,), in_specs=in_specs,
        out_specs=pl.BlockSpec((None, tr, c_n), lambda i, idx: (layer, idx[4] * nb + i, 0)))
    return pl.pallas_call(
        body, name=name, grid_spec=grid_spec,
        out_shape=jax.ShapeDtypeStruct((n_layers, 2 * r2, c_n), f32),
        input_output_aliases=aliases,
        compiler_params=_params("parallel"),
    )(*args)


def _adamw(w, g, m, v, layer, into, name):
    n_l, r, c_n = w.shape
    tr = _tile(r, 256, 8)

    def body(*refs):
        w_ref, g_ref, m_ref, v_ref = refs[:4]
        d_ref, mo_ref, vo_ref = refs[-3:]
        gv = g_ref[...]
        mn = ADAM_B1 * m_ref[...] + (1.0 - ADAM_B1) * gv
        vn = ADAM_B2 * v_ref[...] + (1.0 - ADAM_B2) * (gv * gv)
        m_hat = mn / (1.0 - ADAM_B1 ** ADAM_STEP)
        v_hat = vn / (1.0 - ADAM_B2 ** ADAM_STEP)
        d_ref[...] = -ADAM_LR * (m_hat / (jnp.sqrt(v_hat) + ADAM_EPS) + ADAM_WD * w_ref[...])
        mo_ref[...] = mn
        vo_ref[...] = vn

    blk = pl.BlockSpec((None, tr, c_n), lambda i: (layer, i, 0))
    shp = jax.ShapeDtypeStruct((n_l, r, c_n), f32)
    in_specs = [blk] * 4
    args = [w, g, m, v]
    aliases = {}
    if into is not None:
        in_specs = in_specs + [pl.BlockSpec(memory_space=pl.ANY)] * 3
        args += list(into)
        aliases = {4: 0, 5: 1, 6: 2}
    return pl.pallas_call(
        body, name=name, grid=(r // tr,),
        in_specs=in_specs, out_specs=[blk] * 3, out_shape=[shp] * 3,
        input_output_aliases=aliases,
        compiler_params=_params("parallel"),
    )(*args)


def kernel(x, meta, ffn1_wg, ffn1_wu, ffn1_wd, ffn2_wg, ffn2_wu, ffn2_wd, ln_gain, ln_bias, conv_w_in, conv_w, conv_w_out, kv_w, f_bias, attn_w_q, attn_w_o, loss_target, m_meta, m_ffn1_wg, m_ffn1_wu, m_ffn1_wd, m_ffn2_wg, m_ffn2_wu, m_ffn2_wd, m_ln_gain, m_ln_bias, m_conv_w_in, m_conv_w, m_conv_w_out, m_kv_w, m_f_bias, m_attn_w_q, m_attn_w_o, v_meta, v_ffn1_wg, v_ffn1_wu, v_ffn1_wd, v_ffn2_wg, v_ffn2_wu, v_ffn2_wd, v_ln_gain, v_ln_bias, v_conv_w_in, v_conv_w, v_conv_w_out, v_kv_w, v_f_bias, v_attn_w_q, v_attn_w_o):
    seq, d = x.shape[1], x.shape[2]
    t = PAD + N_META + seq
    n_heads = d // HEAD_DIM
    dq = d // N_CHIPS
    n_kv = kv_w.shape[1]
    x2 = x[0]
    target = loss_target[0]

    def rows8(a):
        return jnp.pad(a, ((0, 8 - a.shape[0]), (0, 0)))

    def small_pack(mt, g, b, cw, fb):
        fb_row = jnp.pad(fb, (0, mt.shape[1] - n_heads))[None]
        return jnp.concatenate([mt, rows8(g.reshape(6, -1)), rows8(b.reshape(6, -1)),
                                rows8(cw.reshape(3, -1)), rows8(fb_row)], axis=0)

    w_small = small_pack(meta, ln_gain, ln_bias, conv_w, f_bias)

    cx, cy, c = lax.axis_index("x"), lax.axis_index("y"), lax.axis_index("c")
    idx = jnp.stack([2 * cx + cy, 2 * (1 - cx) + cy, 2 * cx + (1 - cy), 2 * (1 - cx) + (1 - cy), c]
                    ).astype(jnp.int32)

    def tr(a):
        return a.transpose(0, 2, 1)

    w3 = {"wg1": tr(ffn1_wg), "wu1": tr(ffn1_wu), "wd1": ffn1_wd, "wg2": tr(ffn2_wg),
          "wu2": tr(ffn2_wu), "wd2": ffn2_wd, "win": conv_w_in, "wout": conv_w_out, "kv": kv_w[None],
          "wq": attn_w_q, "wo": attn_w_o, "small": w_small[None]}
    transposed = ("wg1", "wu1", "wg2", "wu2")
    buf = {n: _place_shard(w, idx, "place_shard") for n, w in w3.items()}

    def split(item):
        name, layer = item.split(".")
        return name, int(layer)

    def gather_stage(planner, items):
        triples = [(n, l, buf[n].shape[2] // 2) for n, l in map(split, items)]
        plan, n_copies = planner(triples)
        return dict(plan=plan, n=n_copies, bufs={n: buf[n] for n, _, _ in triples})

    def ici(items):
        return gather_stage(_plan_gather_ici, items)

    def d2d(items):
        return gather_stage(_plan_gather_d2d, items)

    def pair_exchange(items):
        r2s = [buf["G_" + it].shape[1] // 2 for it in items]
        plan, n_copies = _plan_pair_exchange(items, r2s)
        news = {"PAIR_" + it: jax.ShapeDtypeStruct((N_CHIPS, r2, buf["G_" + it].shape[2]), f32)
                for it, r2 in zip(items, r2s)}
        return dict(plan=plan, n=n_copies, srcs={"G_" + it: buf["G_" + it] for it in items}, news=news)

    def chip_exchange(items):
        plan, n_copies = _plan_chip_exchange(items)
        srcs = {"SEND_" + it: buf["SEND_" + it] for it in items}
        news = {"RECV_" + it: jax.ShapeDtypeStruct(s.shape, s.dtype)
                for it, s in ((it, buf["SEND_" + it]) for it in items)}
        return dict(plan=plan, n=n_copies, srcs=srcs, news=news)

    def pair_share(items):
        triples = [(n, l, buf["RED_" + n].shape[1] // 2) for n, l in map(split, items)]
        plan, n_copies = _plan_pair_share(triples)
        return dict(plan=plan, n=n_copies, bufs={"RED_" + n: buf["RED_" + n] for n, _, _ in triples})

    def run(fn, *args, stages=(), name=None, **kw):
        comm = _Copies()
        for st in (stages() if callable(stages) else stages):
            comm.add(st["plan"], st["n"], srcs=st.get("srcs"), bufs=st.get("bufs"), news=st.get("news"))
        out = _copy_call(comm, name) if fn is None else fn(*args, comm=comm, **kw)
        buf.update(comm.out_bufs)
        buf.update(comm.out_news)
        return out

    first = ["wg1.0", "wu1.0", "small.0"]
    run(None, stages=[ici(first)], name="gather_first_ici")
    run(None, stages=[d2d(first)], name="gather_first_d2d")
    small = buf["small"].reshape(N_CHIPS, SMALL_ROWS, dq).transpose(1, 0, 2).reshape(SMALL_ROWS, d)
    meta_full = small[:N_META]
    gains = small[16:22].reshape(DEPTH, 3, 1, d)
    biases = small[24:30].reshape(DEPTH, 3, 1, d)
    conv_w_full = small[32:35]
    fb_pad = jnp.pad(f_bias, (0, LANES - n_heads))[None]
    conv_w8 = ["wd1.0", "win.0", "wout.0"]
    ffn2_l0 = ["wg2.0", "wu2.0", "wd2.0", "kv.0"]
    attn_ffn2_l1 = ["wq.0", "wo.0", "wg2.1", "wu2.1", "wd2.1"]
    ffn1_l1 = ["wg1.1", "wu1.1", "wd1.1"]

    meta_pad = jnp.concatenate([jnp.zeros((PAD, d), f32), meta_full], axis=0)
    h0, h0b = run(_embed, meta_pad, x2, "embed", stages=[ici(conv_w8)])
    a1, b1, s1 = run(_ffn_up, h0b, buf["wg1"], buf["wu1"], 0, "ffn_up",
                     stages=lambda: [d2d(conv_w8), ici(ffn2_l0)])
    r1, h1, h1b = run(_down_ln, s1, buf["wd1"], 0, h0, gains[0, 0], biases[0, 0], 0.5, "ffn_down_ln",
                      stages=lambda: [d2d(ffn2_l0), ici(attn_ffn2_l1)])
    n_in = conv_w_in.shape[-1]
    w_in = buf["win"].reshape(N_CHIPS, d, n_in)
    w_out = buf["wout"].reshape(1, 1, d, d)
    p = run(_nn_matmul, h1b, w_in, f32, "conv_in", stages=lambda: [d2d(attn_ffn2_l1), ici(ffn1_l1)])
    z = _conv_fwd(p, conv_w_full, "conv_fwd")
    r2, h2, h2b = run(_down_ln, z[None], w_out, 0, h1, gains[0, 1], biases[0, 1], 1.0, "mix_out_ln",
                      stages=lambda: [d2d(ffn1_l1)])
    wg1, wu1, wd1, wg2, wu2, wd2 = (buf[n] for n in ("wg1", "wu1", "wd1", "wg2", "wu2", "wd2"))
    w_q = buf["wq"].reshape(1, d, d)
    w_o = buf["wo"].reshape(1, 1, d, d)
    kv_full = buf["kv"].reshape(N_CHIPS, d, n_kv).transpose(1, 0, 2).reshape(d, N_CHIPS * n_kv)
    w_k = kv_full[:, :d][None]
    w_v = kv_full[:, d:2 * d][None]
    w_f = jnp.pad(kv_full[:, 2 * d:], ((0, 0), (0, LANES - n_heads)))[None]
    a2, b2, s2 = _ffn_up(h2b, wg2, wu2, 0, "ffn_up")
    r3, h3, h3b = _down_ln(s2, wd2, 0, h2, gains[0, 2], biases[0, 2], 0.5, "ffn_down_ln")
    kk = _nn_matmul(h3b, w_k, bf16, "proj_bf16")
    vv = _nn_matmul(h3b, w_v, bf16, "proj_bf16")
    flog = _nn_matmul(h3b, w_f, f32, "proj_gate")
    cum = _row_scan(flog, "gate_cumsum", fb_pad)
    bk = _tile(t, 640, LANES)
    c_ht = cum[:, :n_heads].T
    c_keys = jnp.where(jnp.arange(t)[None, :] < PAD, 1e30, c_ht)
    cq_rep = jnp.broadcast_to(c_ht[:, :, None], (n_heads, t, LANES))
    ck_rep = jnp.broadcast_to(c_keys[:, :, None], (n_heads, t, LANES))
    ck_rows = c_keys.reshape(n_heads, t // bk, 1, bk)
    a3, b3, s3 = _ffn_up(h3b, wg1, wu1, 1, "ffn_up")
    r4, h4, h4b = _down_ln(s3, wd1, 1, h3, gains[1, 0], biases[1, 0], 0.5, "ffn_down_ln")
    q = _nn_matmul(h4b, w_q, bf16, "proj_bf16")
    o, o32, lse_rep = _attn_fwd(q, kk, vv, cq_rep, ck_rows, "attn_fwd")
    r5, h5, h5b = _down_ln(o[None], w_o, 0, h4, gains[1, 1], biases[1, 1], 1.0, "mix_out_ln")
    a4, b4, s4 = _ffn_up(h5b, wg2, wu2, 1, "ffn_up")
    r6, h6, _ = _down_ln(s4, wd2, 1, h5, gains[1, 2], biases[1, 2], 0.5, "ffn_down_ln")
    dy, sq = _loss_head(h6, target, "loss_head")
    loss_part = 0.5 * sq[0, 0] / d

    m_small = small_pack(m_meta, m_ln_gain, m_ln_bias, m_conv_w, m_f_bias)
    v_small = small_pack(v_meta, v_ln_gain, v_ln_bias, v_conv_w, v_f_bias)
    m3 = {"wg1": tr(m_ffn1_wg), "wu1": tr(m_ffn1_wu), "wd1": m_ffn1_wd, "wg2": tr(m_ffn2_wg),
          "wu2": tr(m_ffn2_wu), "wd2": m_ffn2_wd, "win": m_conv_w_in, "wout": m_conv_w_out,
          "kv": m_kv_w[None], "wq": m_attn_w_q, "wo": m_attn_w_o, "small": m_small[None]}
    v3 = {"wg1": tr(v_ffn1_wg), "wu1": tr(v_ffn1_wu), "wd1": v_ffn1_wd, "wg2": tr(v_ffn2_wg),
          "wu2": tr(v_ffn2_wu), "wd2": v_ffn2_wd, "win": v_conv_w_in, "wout": v_conv_w_out,
          "kv": v_kv_w[None], "wq": v_attn_w_q, "wo": v_attn_w_o, "small": v_small[None]}
    stepped = {}

    def prepare(items):
        for it in items:
            buf["SEND_" + it] = _rs_prepare(buf["G_" + it], buf["PAIR_" + it], idx, "grad_prepare")

    def finish(items):
        for it in items:
            n, l = split(it)
            buf["RED_" + n] = _rs_finish(buf["G_" + it], buf["PAIR_" + it], buf["RECV_" + it], idx, l,
                                         w3[n].shape[0], buf.get("RED_" + n), "grad_finish")

    def adam(items, grads=None):
        for it in items:
            n, l = split(it)
            g = buf["RED_" + n] if grads is None else grads[n]
            stepped[n] = _adamw(w3[n], g, m3[n], v3[n], l, stepped.get(n), "adamw")

    def ffn_bwd(dr, drb, hb_in, a, b, s, f, layer, on_act=(), after_act=None, on_dwd=(),
                after_dwd=None, on_dx=(), ln=None):
        da, db = run(_ffn_bwd_act, drb, buf["wd" + f], layer, a, b, "ffn_bwd_act", stages=on_act)
        if after_act is not None:
            after_act()
        (buf[f"G_wd{f}.{layer}"],) = run(_tn_matmul, [(s, "stack", drb, "shared")], N_CHIPS, 0.5,
                                         "ffn_dwd", stages=on_dwd)
        if after_dwd is not None:
            after_dwd()
        buf[f"G_wg{f}.{layer}"], buf[f"G_wu{f}.{layer}"] = _tn_matmul(
            [(da, "stack", hb_in, "shared"), (db, "stack", hb_in, "shared")], N_CHIPS, 1.0, "ffn_dwgu")
        return run(_nt_sum, [(da, "stack", buf["wg" + f], layer, True),
                             (db, "stack", buf["wu" + f], layer, True)],
                   dr, ALPHA, N_CHIPS, f32, "ffn_dx", stages=on_dx, ln=ln)

    ffn2_1 = ["wg2.1", "wu2.1", "wd2.1"]
    ffn1_1 = ["wg1.1", "wu1.1", "wd1.1"]
    ffn2_0 = ["wg2.0", "wu2.0", "wd2.0"]
    conv_items = ["wout.0", "win.0"]

    dr6, dr6b, dg12, db12 = _ln_bwd(dy, r6, gains[1, 2], "ln_bwd")
    dr5, dr5b, dg11, db11 = ffn_bwd(dr6, dr6b, h5b, a4, b4, s4, "2", 1,
                                    on_dx=lambda: [pair_exchange(ffn2_1)], ln=(r5, gains[1, 1]))
    prepare(ffn2_1)
    (dwo,) = _tn_matmul([(o, "shared", dr5b, "shared")], 1, 1.0, "sq_dw")
    buf["G_wo.0"] = dwo.reshape(N_CHIPS, dq, d)
    do = run(_nt_sum, [(dr5b, "cols", w_o[0], None)], None, 1.0, 1, bf16, "sq_dx_bf16",
             stages=lambda: [pair_exchange(["wo.0"])])
    prepare(["wo.0"])
    cl, delta = _attn_stats(o32, do, lse_rep, cq_rep, bk, "attn_stats")
    kt = kk.reshape(t // bk, bk, n_heads, HEAD_DIM).transpose(2, 0, 3, 1)
    dq_att, dc_q, dk, dv, dc_k = run(_attn_bwd, q, kk, vv, kt, do, ck_rep, cl, delta, "attn_bwd",
                                     stages=lambda: [chip_exchange(ffn2_1 + ["wo.0"])])
    finish(ffn2_1 + ["wo.0"])
    dc = (dc_q + dc_k).reshape(n_heads, t)
    (dwq,) = run(_tn_matmul, [(h4b, "shared", dq_att, "shared")], 1, 1.0, "sq_dw",
                 stages=lambda: [pair_share(ffn2_1 + ["wo.0"])])
    buf["G_wq.0"] = dwq.reshape(N_CHIPS, dq, d)
    adam(ffn2_1 + ["wo.0"])
    dr4, dr4b, dg10, db10 = run(_nt_sum, [(dq_att, "cols", w_q, None)], dr5, ALPHA, 1, f32,
                                "sq_dx_res", stages=lambda: [pair_exchange(["wq.0"])],
                                ln=(r4, gains[1, 0]))
    prepare(["wq.0"])
    dh3a = ffn_bwd(dr4, dr4b, h3b, a3, b3, s3, "1", 1,
                   on_act=lambda: [chip_exchange(["wq.0"])],
                   on_dx=lambda: [pair_exchange(ffn1_1)])
    prepare(ffn1_1)
    finish(["wq.0"])
    dc_t = jnp.pad(dc.T, ((0, 0), (0, LANES - n_heads)))
    dlogf = _row_scan(dc_t[::-1], "rev_cumsum")[::-1]
    dfl, dfb_cols = _fgate_bwd(dlogf, flog, fb_pad, "gate_bwd")
    dwk, dwv, dwf = run(_tn_matmul, [(h3b, "shared", g, "shared") for g in (dk, dv, dfl)], 1, 1.0,
                        "kv_dw",
                        stages=lambda: [chip_exchange(ffn1_1), pair_share(["wq.0"])])
    adam(["wq.0"])

    def by_chip(full):
        rows = full.shape[0]
        return full.reshape(rows, N_CHIPS, full.shape[1] // N_CHIPS).transpose(1, 0, 2)

    buf["G_kv.0"] = by_chip(jnp.concatenate([dwk[0], dwv[0], dwf[0][:, :n_heads]], axis=1))
    dr3, dr3b, dg02, db02 = run(
        _nt_sum, [(dk, "cols", w_k, None), (dv, "cols", w_v, None), (dfl, "cols", w_f, None)],
        dh3a, 1.0, 1, f32, "kv_dx", stages=lambda: [pair_exchange(["kv.0"])], ln=(r3, gains[0, 2]))
    prepare(["kv.0"])
    finish(ffn1_1)
    dr2, dr2b, dg01, db01 = ffn_bwd(dr3, dr3b, h2b, a2, b2, s2, "2", 0,
                                    on_act=lambda: [chip_exchange(["kv.0"]), pair_share(ffn1_1)],
                                    after_act=lambda: (adam(ffn1_1), finish(["kv.0"])),
                                    on_dwd=lambda: [pair_share(["kv.0"])],
                                    after_dwd=lambda: adam(["kv.0"]),
                                    on_dx=lambda: [pair_exchange(ffn2_0)], ln=(r2, gains[0, 1]))
    prepare(ffn2_0)
    (dwout,) = _tn_matmul([(z, "shared", dr2b, "shared")], 1, 1.0, "sq_dw")
    buf["G_wout.0"] = dwout.reshape(N_CHIPS, dq, d)
    dz = _nt_sum([(dr2b, "cols", w_out[0], None)], None, 1.0, 1, f32, "sq_dx_f32")
    dp, dconv_w = run(_conv_bwd, dz, p, conv_w_full, "conv_bwd",
                      stages=lambda: [chip_exchange(ffn2_0)])
    (buf["G_win.0"],) = _tn_matmul([(h1b, "shared", dp, "cols")], N_CHIPS, 1.0, "conv_dwin")
    finish(ffn2_0)
    dr1, dr1b, dg00, db00 = run(_nt_sum, [(dp, "cols", w_in, None)], dr2, ALPHA, N_CHIPS, f32,
                                "conv_dx",
                                stages=lambda: [pair_exchange(conv_items), pair_share(ffn2_0)],
                                ln=(r1, gains[0, 0]))
    prepare(conv_items)
    adam(ffn2_0)
    gate_up = ["wg1.0", "wu1.0"]
    da, db = run(_ffn_bwd_act, dr1b, buf["wd1"], 0, a1, b1, "ffn_bwd_act",
                 stages=lambda: [chip_exchange(conv_items)])
    finish(conv_items)
    buf["G_wg1.0"], buf["G_wu1.0"] = run(
        _tn_matmul, [(da, "stack", h0b, "shared"), (db, "stack", h0b, "shared")], N_CHIPS, 1.0,
        "ffn_dwgu", stages=lambda: [pair_share(conv_items)])
    adam(conv_items)
    (buf["G_wd1.0"],) = run(_tn_matmul, [(s1, "stack", dr1b, "shared")], N_CHIPS, 0.5, "ffn_dwd",
                            stages=lambda: [pair_exchange(gate_up)])
    prepare(gate_up)
    dh0 = run(_nt_sum, [(da, "stack", buf["wg1"], 0, True), (db, "stack", buf["wu1"], 0, True)],
              dr1, ALPHA, N_CHIPS, f32, "ffn_dx",
              stages=lambda: [chip_exchange(gate_up), pair_exchange(["wd1.0"])])
    prepare(["wd1.0"])
    finish(gate_up)
    grad_x = dh0[PAD + N_META:][None]
    dmeta = dh0[PAD:PAD + N_META]
    buf["G_small.0"] = by_chip(jnp.concatenate(
        [dmeta, rows8(jnp.concatenate([dg00, dg01, dg02, dg10, dg11, dg12], axis=0)),
         rows8(jnp.concatenate([db00, db01, db02, db10, db11, db12], axis=0)),
         rows8(dconv_w), jnp.zeros((8, d), f32)], axis=0))
    run(None, stages=lambda: [chip_exchange(["wd1.0"]), pair_exchange(["small.0"]), pair_share(gate_up)],
        name="grad_tail_1")
    prepare(["small.0"])
    finish(["wd1.0"])
    adam(gate_up)
    run(None, stages=lambda: [chip_exchange(["small.0"]), pair_share(["wd1.0"])], name="grad_tail_2")
    finish(["small.0"])
    adam(["wd1.0"])
    run(None, stages=lambda: [pair_share(["small.0"])], name="grad_tail_3")

    tail = jnp.zeros((LANES,), f32).at[:n_heads].set(dfb_cols[0, :n_heads]).at[n_heads].set(loss_part)
    tail = lax.psum(tail, ("x", "y", "c"))
    loss = tail[n_heads]
    g_fb = tail[:n_heads]
    g_small = jnp.concatenate([buf["RED_small"][0, :40],
                               rows8(jnp.pad(g_fb, (0, dq - n_heads))[None])], axis=0)
    adam(["small.0"], grads={"small": g_small[None]})

    def unpack(pk):
        return (pk[:16], pk[16:22].reshape(DEPTH, 3, dq), pk[24:30].reshape(DEPTH, 3, dq),
                pk[32:35].reshape(1, 3, dq), pk[40, :n_heads])

    def order(pick):
        mt, g, b, cw, fb = unpack(pick("small")[0])
        big = {n: pick(n) for n in w3 if n != "small"}
        big["kv"] = big["kv"][0]
        for n in transposed:
            big[n] = tr(big[n])
        return [mt, big["wg1"], big["wu1"], big["wd1"], big["wg2"], big["wu2"], big["wd2"], g, b,
                big["win"], cw, big["wout"], big["kv"], fb, big["wq"], big["wo"]]

    grads_out = dict({n: buf["RED_" + n] for n in w3}, small=g_small[None])
    return (loss, grad_x, *order(lambda n: grads_out[n]), *order(lambda n: stepped[n][0]),
            *order(lambda n: stepped[n][1]), *order(lambda n: stepped[n][2]))
```

```python
import functools
import math

import jax
import jax.numpy as jnp
from jax import lax
from jax.experimental import pallas as pl
from jax.experimental.pallas import tpu as pltpu

f32 = jnp.float32
bf16 = jnp.bfloat16

N_META = 16
PAD = 112
HEAD_DIM = 128
DEPTH = 2
LN_EPS = 1e-5
ALPHA = (2 * DEPTH) ** 0.25
NEG_INF = -1e30
N_CHIPS = 4
SMALL_ROWS = 48
LANES = 128

ADAM_LR = 0.001
ADAM_B1 = 0.9
ADAM_B2 = 0.999
ADAM_EPS = 1e-08
ADAM_WD = 0.01
ADAM_STEP = 10

VMEM_LIMIT_BYTES = 56 * 1024 * 1024
ROWS_WIDE = 1664
ROWS_ACC = 1040
ROWS_ACC_LN = 832
STRIP = 16
MESH = pl.DeviceIdType.MESH

NT_DIMS = (((1,), (1,)), ((), ()))
TN_DIMS = (((0,), (0,)), ((), ()))


def _tile(n, target, mult):
    best = None
    for d in range(mult, min(n, target) + 1, mult):
        if n % d == 0:
            best = d
    assert best is not None, (n, target, mult)
    return best


def _params(*sem):
    return pltpu.CompilerParams(dimension_semantics=sem, vmem_limit_bytes=VMEM_LIMIT_BYTES)


class _Place:
    def __init__(self):
        self.cx, self.cy, self.c = lax.axis_index("x"), lax.axis_index("y"), lax.axis_index("c")
        self.chips = [(1 - self.cx, self.cy), (self.cx, 1 - self.cy), (1 - self.cx, 1 - self.cy)]
        self.me = 2 * self.cx + self.cy
        self.slots = [self.me] + [2 * px + py for px, py in self.chips]
        self.sib = (self.cx, self.cy, 1 - self.c)


class _Copies:
    def __init__(self):
        self.srcs, self.bufs, self.news = {}, {}, {}
        self.plans = []
        self.out_bufs, self.out_news = {}, {}

    def add(self, plan, n_copies, srcs=None, bufs=None, news=None):
        for have, more in ((self.srcs, srcs), (self.bufs, bufs), (self.news, news)):
            for key, val in (more or {}).items():
                assert key not in have or have[key] is val, key
                have[key] = val
        self.plans.append((plan, n_copies))

    def empty(self):
        return not self.plans

    def count(self):
        return sum(n for _, n in self.plans)

    def copies(self, src_refs, buf_refs, new_refs, send, recv):
        place = _Place()
        srcs = dict(zip(self.srcs, src_refs))
        bufs = dict(zip(self.bufs, buf_refs))
        news = dict(zip(self.news, new_refs))
        out = []
        for plan, n_copies in self.plans:
            triples = plan(srcs, bufs, news, place)
            assert len(triples) == n_copies
            for src, dst, dev in triples:
                n = len(out)
                out.append(pltpu.make_async_remote_copy(
                    src_ref=src, dst_ref=dst, send_sem=send.at[n], recv_sem=recv.at[n],
                    device_id=dev, device_id_type=MESH))
        return out

    def land(self, results):
        n_b = len(self.bufs)
        self.out_bufs = dict(zip(self.bufs, results[:n_b]))
        self.out_news = dict(zip(self.news, results[n_b:]))


def _pallas(comm, body, *, name, grid, in_specs, out_specs, out_shape, compiler_params,
            scratch_shapes=(), input_output_aliases=None):
    aliases = dict(input_output_aliases or {})
    if comm is None or comm.empty():
        return pl.pallas_call(body, name=name, grid=grid, in_specs=in_specs, out_specs=out_specs,
                              out_shape=out_shape, scratch_shapes=list(scratch_shapes),
                              input_output_aliases=aliases, compiler_params=compiler_params)
    single = not isinstance(out_shape, (list, tuple))
    out_shapes = [out_shape] if single else list(out_shape)
    out_specs_l = [out_specs] if single else list(out_specs)
    n_in, n_out, n_scr = len(in_specs), len(out_shapes), len(scratch_shapes)
    n_s, n_b, n_n = len(comm.srcs), len(comm.bufs), len(comm.news)
    n_copies = comm.count()

    def wrapped(*refs):
        ins = refs[:n_in]
        src_refs = refs[n_in:n_in + n_s]
        o0 = n_in + n_s + n_b
        outs = refs[o0:o0 + n_out]
        buf_refs = refs[o0 + n_out:o0 + n_out + n_b]
        new_refs = refs[o0 + n_out + n_b:o0 + n_out + n_b + n_n]
        rest = refs[o0 + n_out + n_b + n_n:]
        scratch, (send, recv) = rest[:n_scr], rest[n_scr:]
        ids = [pl.program_id(a) for a in range(len(grid))]
        first = functools.reduce(jnp.logical_and, [i == 0 for i in ids])
        last = functools.reduce(jnp.logical_and, [i == g - 1 for i, g in zip(ids, grid)])

        @pl.when(first)
        def _():
            for cp in comm.copies(src_refs, buf_refs, new_refs, send, recv):
                cp.start()

        body(*ins, *outs, *scratch)

        @pl.when(last)
        def _():
            for cp in comm.copies(src_refs, buf_refs, new_refs, send, recv):
                cp.wait()

    hbm = pl.BlockSpec(memory_space=pl.ANY)
    for j in range(n_b):
        aliases[n_in + n_s + j] = n_out + j
    call = pl.pallas_call(
        wrapped, name=name, grid=grid,
        in_specs=[*in_specs, *([hbm] * (n_s + n_b))],
        out_specs=[*out_specs_l, *([hbm] * (n_b + n_n))],
        out_shape=[*out_shapes,
                   *[jax.ShapeDtypeStruct(a.shape, a.dtype) for a in comm.bufs.values()],
                   *comm.news.values()],
        scratch_shapes=[*scratch_shapes, pltpu.SemaphoreType.DMA((n_copies,)),
                        pltpu.SemaphoreType.DMA((n_copies,))],
        input_output_aliases=aliases, compiler_params=compiler_params)

    def run(*args):
        res = call(*args, *comm.srcs.values(), *comm.bufs.values())
        comm.land(res[n_out:])
        return res[0] if single else res[:n_out]

    return run


def _copy_call(comm, name):
    n_s, n_b, n_n = len(comm.srcs), len(comm.bufs), len(comm.news)
    n_copies = comm.count()

    def body(*refs):
        src_refs = refs[:n_s]
        buf_refs = refs[n_s + n_b:n_s + 2 * n_b]
        new_refs = refs[n_s + 2 * n_b:n_s + 2 * n_b + n_n]
        send, recv = refs[n_s + 2 * n_b + n_n:]
        copies = comm.copies(src_refs, buf_refs, new_refs, send, recv)
        for cp in copies:
            cp.start()
        for cp in copies:
            cp.wait()

    hbm = pl.BlockSpec(memory_space=pl.ANY)
    res = pl.pallas_call(
        body, name=name,
        in_specs=[hbm] * (n_s + n_b), out_specs=[hbm] * (n_b + n_n),
        out_shape=[*[jax.ShapeDtypeStruct(a.shape, a.dtype) for a in comm.bufs.values()],
                   *comm.news.values()],
        input_output_aliases={n_s + j: j for j in range(n_b)},
        scratch_shapes=[pltpu.SemaphoreType.DMA((n_copies,)), pltpu.SemaphoreType.DMA((n_copies,))],
    )(*comm.srcs.values(), *comm.bufs.values())
    comm.land(res)


def _embed(meta_pad, x, name, comm=None):
    seq, d = x.shape
    t = seq + LANES

    def body(m_ref, x_ref, h_ref, hb_ref):
        first = pl.program_id(0) == 0
        v = jnp.where(first, m_ref[...], x_ref[...])
        h_ref[...] = v
        hb_ref[...] = v.astype(bf16)

    return _pallas(
        comm, body, name=name, grid=(t // LANES,),
        in_specs=[pl.BlockSpec((LANES, d), lambda i: (0, 0)),
                  pl.BlockSpec((LANES, d), lambda i: (jnp.maximum(i - 1, 0), 0))],
        out_specs=[pl.BlockSpec((LANES, d), lambda i: (i, 0)),
                   pl.BlockSpec((LANES, d), lambda i: (i, 0))],
        out_shape=[jax.ShapeDtypeStruct((t, d), f32), jax.ShapeDtypeStruct((t, d), bf16)],
        compiler_params=_params("parallel"),
    )(meta_pad, x)


def _nn_matmul(x, w, out_dtype, name, comm=None, out_scale=None):
    t, k = x.shape
    s_n, _, n = w.shape
    assert s_n == 1 or n % LANES == 0
    tm = _tile(t, ROWS_WIDE, 16)

    def body(x_ref, w_ref, o_ref):
        res = jnp.dot(x_ref[...].astype(bf16), w_ref[...], preferred_element_type=f32)
        if out_scale is not None:
            res = res * out_scale
        o_ref[...] = res.astype(o_ref.dtype)

    return _pallas(
        comm, body, name=name, grid=(s_n, t // tm),
        in_specs=[pl.BlockSpec((tm, k), lambda s, i: (i, 0)),
                  pl.BlockSpec((None, k, n), lambda s, i: (s, 0, 0))],
        out_specs=pl.BlockSpec((tm, n), lambda s, i: (i, s)),
        out_shape=jax.ShapeDtypeStruct((t, s_n * n), out_dtype),
        compiler_params=_params("parallel", "parallel"),
    )(x, w)


def _ffn_up(hb, wg, wu, layer, name, comm=None):
    t, d = hb.shape
    s_n, _, n, _ = wg.shape
    tm = _tile(t, ROWS_WIDE, 16)

    def body(x_ref, wg_ref, wu_ref, a_ref, b_ref, s_ref):
        x = x_ref[...]
        a = lax.dot_general(x, wg_ref[...], NT_DIMS, preferred_element_type=f32)
        b = lax.dot_general(x, wu_ref[...], NT_DIMS, preferred_element_type=f32)
        a_ref[...] = a.astype(bf16)
        b_ref[...] = b.astype(bf16)
        s_ref[...] = (a * jax.nn.sigmoid(a) * b).astype(bf16)

    wspec = pl.BlockSpec((None, None, n, d), lambda s, i: (s, layer, 0, 0))
    ospec = pl.BlockSpec((None, tm, n), lambda s, i: (s, i, 0))
    return _pallas(
        comm, body, name=name, grid=(s_n, t // tm),
        in_specs=[pl.BlockSpec((tm, d), lambda s, i: (i, 0)), wspec, wspec],
        out_specs=[ospec, ospec, ospec],
        out_shape=[jax.ShapeDtypeStruct((s_n, t, n), bf16)] * 3,
        compiler_params=_params("parallel", "parallel"),
    )(hb, wg, wu)


def _down_ln(x, w, layer, hprev, gain, bias, beta, name, comm=None):
    s_n, t, k = x.shape
    d = w.shape[-1]
    tm = _tile(t, ROWS_ACC, 16)

    def body(x_ref, w_ref, h_ref, g_ref, b_ref, r_out, h_out, hb_out, acc):
        s = pl.program_id(1)

        @pl.when(s == 0)
        def _():
            acc[...] = jnp.zeros_like(acc)

        acc[...] += jnp.dot(x_ref[...], w_ref[...], preferred_element_type=f32)

        @pl.when(s == s_n - 1)
        def _():
            r = ALPHA * h_ref[...] + beta * acc[...]
            mu = jnp.mean(r, axis=-1, keepdims=True)
            xc = r - mu
            var = jnp.mean(xc * xc, axis=-1, keepdims=True)
            y = xc * lax.rsqrt(var + LN_EPS) * g_ref[...] + b_ref[...]
            r_out[...] = r
            h_out[...] = y
            hb_out[...] = y.astype(bf16)

    row = pl.BlockSpec((tm, d), lambda i, s: (i, 0))
    vec = pl.BlockSpec((1, d), lambda i, s: (0, 0))
    return _pallas(
        comm, body, name=name, grid=(t // tm, s_n),
        in_specs=[pl.BlockSpec((None, tm, k), lambda i, s: (s, i, 0)),
                  pl.BlockSpec((None, None, k, d), lambda i, s: (s, layer, 0, 0)),
                  row, vec, vec],
        out_specs=[row, row, row],
        out_shape=[jax.ShapeDtypeStruct((t, d), f32), jax.ShapeDtypeStruct((t, d), f32),
                   jax.ShapeDtypeStruct((t, d), bf16)],
        scratch_shapes=[pltpu.VMEM((tm, d), f32)],
        compiler_params=_params("parallel", "arbitrary"),
    )(x, w, hprev, gain, bias)


def _conv_fwd(p, conv_w, name):
    t, d3 = p.shape
    d = d3 // 3
    tm = _tile(t, 320, 8)
    hb = tm // 8

    def body(p_ref, prev_ref, w_ref, z_ref):
        i = pl.program_id(0)
        rows = i * tm - 8 + lax.broadcasted_iota(jnp.int32, (tm + 8, 1), 0)
        cg = jnp.concatenate([prev_ref[:, d:2 * d], p_ref[:, d:2 * d]], axis=0)
        val = jnp.concatenate([prev_ref[:, 2 * d:], p_ref[:, 2 * d:]], axis=0)
        u = jnp.where(rows >= PAD, cg * val, 0.0)
        y = (w_ref[2:3, :] * u + w_ref[1:2, :] * pltpu.roll(u, 1, 0)
             + w_ref[0:1, :] * pltpu.roll(u, 2, 0))
        z_ref[...] = (p_ref[:, :d] * y[8:]).astype(bf16)

    return pl.pallas_call(
        body, name=name, grid=(t // tm,),
        in_specs=[pl.BlockSpec((tm, d3), lambda i: (i, 0)),
                  pl.BlockSpec((8, d3), lambda i: (jnp.maximum(i * hb - 1, 0), 0)),
                  pl.BlockSpec((3, d), lambda i: (0, 0))],
        out_specs=pl.BlockSpec((tm, d), lambda i: (i, 0)),
        out_shape=jax.ShapeDtypeStruct((t, d), bf16),
        compiler_params=_params("parallel"),
    )(p, p, conv_w)


def _row_scan(x, name, fbias=None, reverse=False):
    t, n = x.shape
    blk = LANES
    n_blk = t // blk
    gate = fbias is not None

    def body(*refs):
        if gate:
            x_ref, fb_ref, o_ref, carry = refs
        else:
            x_ref, o_ref, carry = refs
        i = pl.program_id(0)

        @pl.when(i == 0)
        def _():
            carry[...] = jnp.zeros_like(carry)

        v = x_ref[...]
        r = lax.broadcasted_iota(jnp.int32, (blk, n), 0)
        if gate:
            v = v + fb_ref[...]
            v = jnp.minimum(v, 0.0) - jnp.log1p(jnp.exp(-jnp.abs(v)))
            v = jnp.where(i * blk + r >= PAD, v, 0.0)
        sh = 1
        while sh < blk:
            if reverse:
                v = v + jnp.where(r < blk - sh, pltpu.roll(v, blk - sh, 0), 0.0)
            else:
                v = v + jnp.where(r >= sh, pltpu.roll(v, sh, 0), 0.0)
            sh *= 2
        v = v + carry[...]
        o_ref[...] = v
        carry[...] = o_ref[0:1, :] if reverse else o_ref[blk - 1:blk, :]

    order = (lambda i: (n_blk - 1 - i, 0)) if reverse else (lambda i: (i, 0))
    in_specs = [pl.BlockSpec((blk, n), order)]
    args = [x]
    if gate:
        in_specs.append(pl.BlockSpec((1, n), lambda i: (0, 0)))
        args.append(fbias)
    return pl.pallas_call(
        body, name=name, grid=(n_blk,),
        in_specs=in_specs,
        out_specs=pl.BlockSpec((blk, n), order),
        out_shape=jax.ShapeDtypeStruct((t, n), f32),
        scratch_shapes=[pltpu.VMEM((1, n), f32)],
        compiler_params=_params("arbitrary"),
    )(*args)


def _lanes(x, n):
    return jnp.concatenate([x] * (n // LANES), axis=1)


def _attn_fwd(q, k, v, cq_rep, ck_rows, name):
    t, d = q.shape
    n_heads = d // HEAD_DIM
    bk = ck_rows.shape[-1]
    bq = bk

    def lane_fold(x, op):
        out = x[:, :LANES]
        for c0 in range(LANES, bk, LANES):
            out = op(out, x[:, c0:c0 + LANES])
        return out

    def body(q_ref, k_ref, v_ref, cq_ref, ck_ref, o_ref, o32_ref, lse_ref,
             s_scr, p_scr, m_scr, l_scr, red_scr, acc_scr):
        i = pl.program_id(1)
        m_scr[...] = jnp.full_like(m_scr, NEG_INF)
        l_scr[...] = jnp.zeros_like(l_scr)
        acc_scr[...] = jnp.zeros_like(acc_scr)
        qb = q_ref[...]
        ahead = (lax.broadcasted_iota(jnp.int32, (STRIP, bk), 1)
                 - lax.broadcasted_iota(jnp.int32, (STRIP, bk), 0))

        def tile(j, diagonal):
            k0 = pl.multiple_of(j * bk, bk)
            s_scr[...] = lax.dot_general(qb, k_ref[pl.ds(k0, bk), :], NT_DIMS,
                                         preferred_element_type=f32)
            ck = ck_ref[j]
            for r in range(0, bq, STRIP):
                rows = slice(r, r + STRIP)
                s = s_scr[rows, :] + _lanes(cq_ref[rows, :], bk) - ck
                if diagonal:
                    s = jnp.where(ahead <= r, s, NEG_INF)
                s_scr[rows, :] = s
                red_scr[rows, :] = lane_fold(s, jnp.maximum)
            m_old = m_scr[...]
            m_new = jnp.maximum(m_old, jnp.broadcast_to(
                jnp.max(red_scr[...], axis=1, keepdims=True), (bq, LANES)))
            a = jnp.exp(m_old - m_new)
            m_scr[...] = m_new
            for r in range(0, bq, STRIP):
                rows = slice(r, r + STRIP)
                pr = jnp.exp(s_scr[rows, :] - _lanes(m_scr[rows, :], bk))
                red_scr[rows, :] = lane_fold(pr, jnp.add)
                p_scr[rows, :] = pr.astype(bf16)
            l_scr[...] = a * l_scr[...] + jnp.broadcast_to(
                jnp.sum(red_scr[...], axis=1, keepdims=True), (bq, LANES))
            acc_scr[...] = a * acc_scr[...] + jnp.dot(
                p_scr[...], v_ref[pl.ds(k0, bk), :], preferred_element_type=f32)

        def full_tile(j, carry):
            tile(j, False)
            return carry

        lax.fori_loop(0, i, full_tile, 0)
        tile(i, True)
        out = acc_scr[...] / l_scr[...]
        o_ref[...] = out.astype(bf16)
        o32_ref[...] = out
        lse_ref[...] = m_scr[...] + jnp.log(l_scr[...])

    qblk = pl.BlockSpec((bq, HEAD_DIM), lambda h, i: (i, h))
    head_rows = pl.BlockSpec((t, HEAD_DIM), lambda h, i: (0, h))
    rep = pl.BlockSpec((None, bq, LANES), lambda h, i: (h, i, 0))
    col = pltpu.VMEM((bq, LANES), f32)
    return pl.pallas_call(
        body, name=name, grid=(n_heads, t // bq),
        in_specs=[qblk, head_rows, head_rows, rep,
                  pl.BlockSpec((None, t // bk, 1, bk), lambda h, i: (h, 0, 0, 0))],
        out_specs=[qblk, qblk, rep],
        out_shape=[jax.ShapeDtypeStruct((t, d), bf16), jax.ShapeDtypeStruct((t, d), f32),
                   jax.ShapeDtypeStruct((n_heads, t, LANES), f32)],
        scratch_shapes=[pltpu.VMEM((bq, bk), f32), pltpu.VMEM((bq, bk), bf16), col, col, col,
                        pltpu.VMEM((bq, HEAD_DIM), f32)],
        compiler_params=_params("parallel", "parallel"),
    )(q, k, v, cq_rep, ck_rows)


def _loss_head(h, target, r, gain, name):
    t, d = h.shape

    def body(h_ref, t_ref, r_ref, g_ref, dr_ref, drb_ref, dg_ref, db_ref, loss_ref):
        i = pl.program_id(0)

        @pl.when(i == 0)
        def _():
            loss_ref[...] = jnp.zeros_like(loss_ref)
            dg_ref[...] = jnp.zeros_like(dg_ref)
            db_ref[...] = jnp.zeros_like(db_ref)

        diff = jnp.where(i >= 1, h_ref[...] - t_ref[...], 0.0)
        loss_ref[...] += jnp.sum(diff * diff)
        dr, dg, db = _ln_bwd_rows(diff * (1.0 / d), r_ref[...], g_ref[...])
        dr_ref[...] = dr
        drb_ref[...] = dr.astype(bf16)
        dg_ref[...] += dg
        db_ref[...] += db

    row = pl.BlockSpec((LANES, d), lambda i: (i, 0))
    vec = pl.BlockSpec((1, d), lambda i: (0, 0))
    return pl.pallas_call(
        body, name=name, grid=(t // LANES,),
        in_specs=[row, pl.BlockSpec((LANES, d), lambda i: (jnp.maximum(i - 1, 0), 0)), row, vec],
        out_specs=[row, row, vec, vec, pl.BlockSpec((1, LANES), lambda i: (0, 0))],
        out_shape=[jax.ShapeDtypeStruct((t, d), f32), jax.ShapeDtypeStruct((t, d), bf16),
                   jax.ShapeDtypeStruct((1, d), f32), jax.ShapeDtypeStruct((1, d), f32),
                   jax.ShapeDtypeStruct((1, LANES), f32)],
        compiler_params=_params("arbitrary"),
    )(h, target, r, gain)


def _ln_bwd_rows(dy, rr, gain):
    mu = jnp.mean(rr, axis=-1, keepdims=True)
    xc = rr - mu
    var = jnp.mean(xc * xc, axis=-1, keepdims=True)
    rstd = lax.rsqrt(var + LN_EPS)
    xhat = xc * rstd
    dxh = dy * gain
    m1 = jnp.mean(dxh, axis=-1, keepdims=True)
    m2 = jnp.mean(dxh * xhat, axis=-1, keepdims=True)
    dr = rstd * (dxh - m1 - xhat * m2)
    return dr, jnp.sum(dy * xhat, axis=0, keepdims=True), jnp.sum(dy, axis=0, keepdims=True)


def _ffn_bwd_act(drb, wd, layer, a, b, name, comm=None):
    t, d = drb.shape
    s_n, _, n = a.shape
    tm = _tile(t, ROWS_WIDE, 16)

    def body(dr_ref, w_ref, a_ref, b_ref, da_ref, db_ref):
        ds = 0.5 * lax.dot_general(dr_ref[...], w_ref[...], NT_DIMS, preferred_element_type=f32)
        av = a_ref[...].astype(f32)
        sig = jax.nn.sigmoid(av)
        da_ref[...] = (ds * b_ref[...].astype(f32) * (sig * (1.0 + av * (1.0 - sig)))).astype(bf16)
        db_ref[...] = (ds * (av * sig)).astype(bf16)

    act = pl.BlockSpec((None, tm, n), lambda s, i: (s, i, 0))
    return _pallas(
        comm, body, name=name, grid=(s_n, t // tm),
        in_specs=[pl.BlockSpec((tm, d), lambda s, i: (i, 0)),
                  pl.BlockSpec((None, None, n, d), lambda s, i: (s, layer, 0, 0)), act, act],
        out_specs=[act, act],
        out_shape=[jax.ShapeDtypeStruct((s_n, t, n), bf16), jax.ShapeDtypeStruct((s_n, t, n), bf16)],
        compiler_params=_params("parallel", "parallel"),
    )(drb, wd, a, b)


def _act_spec(mode, tt, k, t_first):
    def fix(fn):
        return (lambda i, s: fn(s, i)) if t_first else fn
    if mode == "shared":
        return pl.BlockSpec((tt, k), fix(lambda s, i: (i, 0)))
    if mode == "cols":
        return pl.BlockSpec((tt, k), fix(lambda s, i: (i, s)))
    assert mode == "stack"
    return pl.BlockSpec((None, tt, k), fix(lambda s, i: (s, i, 0)))


def _act_width(arr, mode, s_n):
    return arr.shape[-1] // s_n if mode == "cols" else arr.shape[-1]


def _tn_matmul(pairs, s_n, scale, name, comm=None):
    t = pairs[0][0].shape[-2]
    tt = _tile(t, 2080, 16)
    n_t = t // tt
    arrays, specs, where = [], [], []
    for x, xmode, y, ymode in pairs:
        pos = []
        for arr, mode in ((x, xmode), (y, ymode)):
            hit = [j for j, a in enumerate(arrays) if a is arr]
            if not hit:
                arrays.append(arr)
                specs.append(_act_spec(mode, tt, _act_width(arr, mode, s_n), False))
                hit = [len(arrays) - 1]
            pos.append(hit[0])
        where.append(pos)
    widths = [(_act_width(x, xm, s_n), _act_width(y, ym, s_n)) for x, xm, y, ym in pairs]
    n_a = len(arrays)

    def body(*refs):
        i = pl.program_id(1)
        for (px, py), o_ref in zip(where, refs[n_a:]):
            part = lax.dot_general(refs[px][...].astype(bf16), refs[py][...].astype(bf16), TN_DIMS,
                                   preferred_element_type=f32)

            @pl.when(i == 0)
            def _():
                o_ref[...] = part

            @pl.when(i > 0)
            def _():
                o_ref[...] += part

            if scale != 1.0:
                @pl.when(i == n_t - 1)
                def _():
                    o_ref[...] = o_ref[...] * scale

    return _pallas(
        comm, body, name=name, grid=(s_n, n_t),
        in_specs=specs,
        out_specs=[pl.BlockSpec((None, kx, ky), lambda s, i: (s, 0, 0)) for kx, ky in widths],
        out_shape=[jax.ShapeDtypeStruct((s_n, kx, ky), f32) for kx, ky in widths],
        compiler_params=_params("parallel", "arbitrary"),
    )(*arrays)


def _nt_sum(pairs, base, base_scale, s_n, out_dtype, name, comm=None, ln=None):
    t = pairs[0][0].shape[-2]
    pairs = [(*pr, False)[:5] for pr in pairs]
    d = pairs[0][2].shape[-1] if pairs[0][4] else pairs[0][2].shape[-2]
    tm = _tile(t, ROWS_ACC if ln is None else ROWS_ACC_LN, 16)
    n_p = len(pairs)
    has_base = base is not None
    flipped = [pr[4] for pr in pairs]
    n_out = 1 if ln is None else 4

    def body(*refs):
        dy_refs = refs[0:2 * n_p:2]
        w_refs = refs[1:2 * n_p:2]
        rest = refs[2 * n_p:]
        base_ref = rest[0] if has_base else None
        o_refs, acc = rest[-1 - n_out:-1], rest[-1]
        s = pl.program_id(1)

        if ln is not None:
            @pl.when(jnp.logical_and(pl.program_id(0) == 0, s == 0))
            def _():
                o_refs[2][...] = jnp.zeros_like(o_refs[2])
                o_refs[3][...] = jnp.zeros_like(o_refs[3])

        @pl.when(s == 0)
        def _():
            acc[...] = jnp.zeros_like(acc)

        tot = None
        for dy_ref, w_ref, flip in zip(dy_refs, w_refs, flipped):
            dyv = dy_ref[...].astype(bf16)
            if flip:
                part = jnp.dot(dyv, w_ref[...], preferred_element_type=f32)
            else:
                part = lax.dot_general(dyv, w_ref[...], NT_DIMS, preferred_element_type=f32)
            tot = part if tot is None else tot + part
        acc[...] += tot

        @pl.when(s == s_n - 1)
        def _():
            res = acc[...]
            if has_base:
                res = base_scale * base_ref[...] + res
            if ln is None:
                o_refs[0][...] = res.astype(o_refs[0].dtype)
            else:
                r_ref, g_ref = rest[-7], rest[-6]
                dr, dg, db = _ln_bwd_rows(res, r_ref[...], g_ref[...])
                o_refs[0][...] = dr
                o_refs[1][...] = dr.astype(bf16)
                o_refs[2][...] += dg
                o_refs[3][...] += db

    in_specs, args = [], []
    for dy, mode, w, layer, flip in pairs:
        k = _act_width(dy, mode, s_n)
        in_specs.append(_act_spec(mode, tm, k, True))
        wshape = (k, d) if flip else (d, k)
        if layer is None:
            in_specs.append(pl.BlockSpec((None, *wshape), lambda i, s: (s, 0, 0)))
        else:
            in_specs.append(pl.BlockSpec((None, None, *wshape),
                                         functools.partial(lambda i, s, l: (s, l, 0, 0), l=layer)))
        args += [dy, w]
    row = pl.BlockSpec((tm, d), lambda i, s: (i, 0))
    vec = pl.BlockSpec((1, d), lambda i, s: (0, 0))
    if has_base:
        in_specs.append(row)
        args.append(base)
    if ln is None:
        out_specs = row
        out_shape = jax.ShapeDtypeStruct((t, d), out_dtype)
    else:
        in_specs += [row, vec]
        args += list(ln)
        out_specs = [row, row, vec, vec]
        out_shape = [jax.ShapeDtypeStruct((t, d), f32), jax.ShapeDtypeStruct((t, d), bf16),
                     jax.ShapeDtypeStruct((1, d), f32), jax.ShapeDtypeStruct((1, d), f32)]
    return _pallas(
        comm, body, name=name, grid=(t // tm, s_n),
        in_specs=in_specs, out_specs=out_specs, out_shape=out_shape,
        scratch_shapes=[pltpu.VMEM((tm, d), f32)],
        compiler_params=_params("parallel" if ln is None else "arbitrary", "arbitrary"),
    )(*args)


def _conv_bwd(dz, p, conv_w, name, comm=None):
    t, d3 = p.shape
    d = d3 // 3
    tm = _tile(t, 320, 8)
    hb = tm // 8
    last8 = t // 8 - 1
    n_ext = tm + 8

    def body(dz_ref, dzn_ref, p_ref, pp_ref, pn_ref, w_ref, dp_ref, dw_ref):
        i = pl.program_id(0)

        @pl.when(i == 0)
        def _():
            dw_ref[...] = jnp.zeros_like(dw_ref)

        w0, w1, w2 = w_ref[0:1, :], w_ref[1:2, :], w_ref[2:3, :]
        rows_u = i * tm - 8 + lax.broadcasted_iota(jnp.int32, (n_ext, 1), 0)
        cg = jnp.concatenate([pp_ref[:, d:2 * d], p_ref[:, d:2 * d]], axis=0)
        val = jnp.concatenate([pp_ref[:, 2 * d:], p_ref[:, 2 * d:]], axis=0)
        u = jnp.where(rows_u >= PAD, cg * val, 0.0)
        u1 = pltpu.roll(u, 1, 0)
        u2 = pltpu.roll(u, 2, 0)
        y = (w2 * u + w1 * u1 + w0 * u2)[8:]
        dzv = dz_ref[...]
        bg = p_ref[:, :d]
        rows_n = (i + 1) * tm + lax.broadcasted_iota(jnp.int32, (8, 1), 0)
        dy_main = dzv * bg
        dy_next = jnp.where(rows_n < t, dzn_ref[...] * pn_ref[:, :d], 0.0)
        dye = jnp.concatenate([dy_main, dy_next], axis=0)
        du = (w2 * dye + w1 * pltpu.roll(dye, n_ext - 1, 0)
              + w0 * pltpu.roll(dye, n_ext - 2, 0))[:tm]
        du = jnp.where(rows_u[8:] >= PAD, du, 0.0)
        dp_ref[:, :d] = (dzv * y).astype(bf16)
        dp_ref[:, d:2 * d] = (du * val[8:]).astype(bf16)
        dp_ref[:, 2 * d:] = (du * cg[8:]).astype(bf16)
        dw_ref[0:1, :] += jnp.sum(dy_main * u2[8:], axis=0, keepdims=True)
        dw_ref[1:2, :] += jnp.sum(dy_main * u1[8:], axis=0, keepdims=True)
        dw_ref[2:3, :] += jnp.sum(dy_main * u[8:], axis=0, keepdims=True)

    nxt = lambda i: (jnp.minimum((i + 1) * hb, last8), 0)
    return _pallas(
        comm, body, name=name, grid=(t // tm,),
        in_specs=[pl.BlockSpec((tm, d), lambda i: (i, 0)),
                  pl.BlockSpec((8, d), nxt),
                  pl.BlockSpec((tm, d3), lambda i: (i, 0)),
                  pl.BlockSpec((8, d3), lambda i: (jnp.maximum(i * hb - 1, 0), 0)),
                  pl.BlockSpec((8, d3), nxt),
                  pl.BlockSpec((3, d), lambda i: (0, 0))],
        out_specs=[pl.BlockSpec((tm, d3), lambda i: (i, 0)),
                   pl.BlockSpec((3, d), lambda i: (0, 0))],
        out_shape=[jax.ShapeDtypeStruct((t, d3), bf16), jax.ShapeDtypeStruct((3, d), f32)],
        compiler_params=_params("arbitrary"),
    )(dz, dz, p, p, p, conv_w)


def _attn_stats(o, do, lse_rep, cq_rep, bq, name):
    t, d = o.shape
    n_heads = d // HEAD_DIM

    def body(o_ref, do_ref, lse_ref, cq_ref, cl_ref, delta_ref):
        for c0 in range(0, bq, LANES):
            rows = slice(c0, c0 + LANES)
            cl_ref[:, rows] = (cq_ref[rows, :] - lse_ref[rows, :]).T[0:1, :]
            prod = o_ref[rows, :] * do_ref[rows, :].astype(f32)
            delta_ref[:, rows] = jnp.sum(prod.T, axis=0, keepdims=True)

    qblk = pl.BlockSpec((bq, HEAD_DIM), lambda h, i: (i, h))
    rep = pl.BlockSpec((None, bq, LANES), lambda h, i: (h, i, 0))
    row = pl.BlockSpec((None, None, 1, bq), lambda h, i: (h, i, 0, 0))
    shp = jax.ShapeDtypeStruct((n_heads, t // bq, 1, bq), f32)
    return pl.pallas_call(
        body, name=name, grid=(n_heads, t // bq),
        in_specs=[qblk, qblk, rep, rep],
        out_specs=[row, row], out_shape=[shp, shp],
        compiler_params=_params("parallel", "parallel"),
    )(o, do, lse_rep, cq_rep)


def _attn_bwd(q, k, v, kt, do, ckey, cl_rows, delta_rows, name, comm=None):
    t, d = q.shape
    n_heads = d // HEAD_DIM
    bk = kt.shape[-1]
    bq = bk
    n_kv = t // bk
    n_q = t // bq
    scale = 1.0 / math.sqrt(HEAD_DIM)

    def body(q_ref, do_ref, cl_ref, dl_ref, k_ref, v_ref, kt_ref, ck_ref,
             dq_ref, dcq_ref, dk_ref, dv_ref, dck_ref,
             st_scr, dp_scr, p_scr, ds_scr, dqt, dk_acc, dv_acc, dck_acc):
        j = pl.program_id(1)

        @pl.when(j == 0)
        def _():
            dqt[...] = jnp.zeros_like(dqt)
            dcq_ref[...] = jnp.zeros_like(dcq_ref)

        dk_acc[...] = jnp.zeros_like(dk_acc)
        dv_acc[...] = jnp.zeros_like(dv_acc)
        dck_acc[...] = jnp.zeros_like(dck_acc)
        kb = k_ref[...]
        vb = v_ref[...]
        behind = (lax.broadcasted_iota(jnp.int32, (STRIP, bq), 1)
                  - lax.broadcasted_iota(jnp.int32, (STRIP, bq), 0))

        def tile(i, diagonal):
            r0 = pl.multiple_of(i * bq, bq)
            qi = q_ref[pl.ds(r0, bq), :]
            doi = do_ref[pl.ds(r0, bq), :]
            st_scr[...] = lax.dot_general(kb, qi, NT_DIMS, preferred_element_type=f32)
            dp_scr[...] = lax.dot_general(vb, doi, NT_DIMS, preferred_element_type=f32)
            cl = cl_ref[i]
            dl = dl_ref[i]
            over_keys = jnp.zeros((STRIP, bq), f32)
            for r in range(0, bk, STRIP):
                keys = slice(r, r + STRIP)
                st = st_scr[keys, :] + cl - _lanes(ck_ref[keys, :], bq)
                if diagonal:
                    st = jnp.where(behind >= r, st, NEG_INF)
                pr = jnp.exp(st)
                ds = pr * (dp_scr[keys, :] - dl)
                over_keys = over_keys + ds
                dck_acc[keys, :] -= jnp.sum(ds, axis=1, keepdims=True)
                p_scr[keys, :] = pr.astype(bf16)
                ds_scr[keys, :] = ds.astype(bf16)
            dcq_ref[i] += jnp.sum(over_keys, axis=0, keepdims=True)
            dv_acc[...] += jnp.dot(p_scr[...], doi, preferred_element_type=f32)
            dk_acc[...] += jnp.dot(ds_scr[...], qi, preferred_element_type=f32)
            dqt[i] += jnp.dot(kt_ref[...], ds_scr[...], preferred_element_type=f32)

        def full_tile(i, carry):
            tile(i, False)
            return carry

        tile(j, True)
        lax.fori_loop(j + 1, n_q, full_tile, 0)
        dk_ref[...] = dk_acc[...].astype(bf16)
        dv_ref[...] = dv_acc[...].astype(bf16)
        for c0 in range(0, bk, LANES):
            keys = slice(c0, c0 + LANES)
            dck_ref[:, keys] = jnp.broadcast_to(dck_acc[keys, :], (LANES, LANES)).T[0:1, :]

        @pl.when(j == n_kv - 1)
        def _():
            def emit(i, carry):
                r0 = pl.multiple_of(i * bq, bq)
                dq_ref[pl.ds(r0, bq), :] = dqt[i].T * scale
                return carry
            lax.fori_loop(0, n_q, emit, 0)

    head_rows = pl.BlockSpec((t, HEAD_DIM), lambda h, j: (0, h))
    head_stat = pl.BlockSpec((None, n_q, 1, bq), lambda h, j: (h, 0, 0, 0))
    kblk = pl.BlockSpec((bk, HEAD_DIM), lambda h, j: (j, h))
    return _pallas(
        comm, body, name=name, grid=(n_heads, n_kv),
        in_specs=[head_rows, head_rows, head_stat, head_stat, kblk, kblk,
                  pl.BlockSpec((None, None, HEAD_DIM, bk), lambda h, j: (h, j, 0, 0)),
                  pl.BlockSpec((None, bk, LANES), lambda h, j: (h, j, 0))],
        out_specs=[head_rows, head_stat, kblk, kblk,
                   pl.BlockSpec((None, None, 1, bk), lambda h, j: (h, j, 0, 0))],
        out_shape=[jax.ShapeDtypeStruct((t, d), f32),
                   jax.ShapeDtypeStruct((n_heads, n_q, 1, bq), f32),
                   jax.ShapeDtypeStruct((t, d), bf16), jax.ShapeDtypeStruct((t, d), bf16),
                   jax.ShapeDtypeStruct((n_heads, n_kv, 1, bk), f32)],
        scratch_shapes=[pltpu.VMEM((bk, bq), f32), pltpu.VMEM((bk, bq), f32),
                        pltpu.VMEM((bk, bq), bf16), pltpu.VMEM((bk, bq), bf16),
                        pltpu.VMEM((n_q, HEAD_DIM, bq), f32),
                        pltpu.VMEM((bk, HEAD_DIM), f32), pltpu.VMEM((bk, HEAD_DIM), f32),
                        pltpu.VMEM((bk, 1), f32)],
        compiler_params=_params("parallel", "arbitrary"),
    )(q, do, cl_rows, delta_rows, k, v, kt, ckey)


def _fgate_bwd(dlogf, flog, fbias, name):
    t, n = flog.shape

    def body(dl_ref, fl_ref, fb_ref, o_ref, sum_ref):
        i = pl.program_id(0)

        @pl.when(i == 0)
        def _():
            sum_ref[...] = jnp.zeros_like(sum_ref)

        r = i * LANES + lax.broadcasted_iota(jnp.int32, (LANES, n), 0)
        g = dl_ref[...] * jax.nn.sigmoid(-(fl_ref[...] + fb_ref[...]))
        g = jnp.where(r >= PAD, g, 0.0)
        o_ref[...] = g
        sum_ref[...] += jnp.sum(g, axis=0, keepdims=True)

    blk = pl.BlockSpec((LANES, n), lambda i: (i, 0))
    vec = pl.BlockSpec((1, n), lambda i: (0, 0))
    return pl.pallas_call(
        body, name=name, grid=(t // LANES,),
        in_specs=[blk, blk, vec], out_specs=[blk, vec],
        out_shape=[jax.ShapeDtypeStruct((t, n), f32), jax.ShapeDtypeStruct((1, n), f32)],
        compiler_params=_params("arbitrary"),
    )(dlogf, flog, fbias)


def _place_shard(w, idx, name):
    n_l, r, c_n = w.shape
    out_dtype = bf16 if r * c_n > 2 ** 16 else w.dtype
    tr = _tile(r, 512, 16) if r % 16 == 0 else r

    def body(idx_ref, w_ref, o_ref):
        o_ref[...] = w_ref[...].astype(out_dtype)

    grid_spec = pltpu.PrefetchScalarGridSpec(
        num_scalar_prefetch=1, grid=(n_l, r // tr),
        in_specs=[pl.BlockSpec((None, tr, c_n), lambda l, i, idx: (l, i, 0))],
        out_specs=pl.BlockSpec((None, None, tr, c_n), lambda l, i, idx: (idx[0], l, i, 0)))
    return pl.pallas_call(
        body, name=name, grid_spec=grid_spec,
        out_shape=jax.ShapeDtypeStruct((N_CHIPS, n_l, r, c_n), out_dtype),
        compiler_params=_params("parallel", "parallel"),
    )(idx, w)


def _plan_gather_ici(items):
    def plan(srcs, bufs, news, p):
        out = []
        for name, layer, r2 in items:
            mine = bufs[name].at[p.me, layer, pl.ds(p.c * r2, r2)]
            out += [(mine, mine, (*chip, p.c)) for chip in p.chips]
        return out
    return plan, 3 * len(items)


def _plan_gather_d2d(items):
    def plan(srcs, bufs, news, p):
        out = []
        for name, layer, r2 in items:
            for px, py in p.chips:
                landed = bufs[name].at[2 * px + py, layer, pl.ds(p.c * r2, r2)]
                out.append((landed, landed, p.sib))
        return out
    return plan, 3 * len(items)


def _plan_pair_exchange(names, r2s):
    def plan(srcs, bufs, news, p):
        out = []
        for name, r2 in zip(names, r2s):
            for k, slot in enumerate(p.slots):
                out.append((srcs["G_" + name].at[slot, pl.ds((1 - p.c) * r2, r2)],
                            news["PAIR_" + name].at[k], p.sib))
        return out
    return plan, 4 * len(names)


def _plan_chip_exchange(names):
    def plan(srcs, bufs, news, p):
        out = []
        for name in names:
            for k, chip in enumerate(p.chips):
                out.append((srcs["SEND_" + name].at[k], news["RECV_" + name].at[k], (*chip, p.c)))
        return out
    return plan, 3 * len(names)


def _plan_pair_share(items):
    def plan(srcs, bufs, news, p):
        out = []
        for name, layer, r2 in items:
            mine = bufs["RED_" + name].at[layer, pl.ds(p.c * r2, r2)]
            out.append((mine, mine, p.sib))
        return out
    return plan, len(items)


def _rs_prepare(g, pair, idx, name):
    _, r2, c_n = pair.shape
    tr = _tile(r2, 256, 8)
    nb = r2 // tr

    def body(idx_ref, g_ref, p_ref, o_ref):
        o_ref[...] = (g_ref[...] + p_ref[...]).astype(bf16)

    grid_spec = pltpu.PrefetchScalarGridSpec(
        num_scalar_prefetch=1, grid=(3, nb),
        in_specs=[pl.BlockSpec((None, tr, c_n), lambda k, i, idx: (idx[k + 1], idx[4] * nb + i, 0)),
                  pl.BlockSpec((None, tr, c_n), lambda k, i, idx: (k + 1, i, 0))],
        out_specs=pl.BlockSpec((None, tr, c_n), lambda k, i, idx: (k, i, 0)))
    return pl.pallas_call(
        body, name=name, grid_spec=grid_spec,
        out_shape=jax.ShapeDtypeStruct((3, r2, c_n), bf16),
        compiler_params=_params("parallel", "parallel"),
    )(idx, g, pair)


def _rs_finish(g, pair, recv, idx, layer, n_layers, into, name):
    _, r2, c_n = pair.shape
    tr = _tile(r2, 256, 8)
    nb = r2 // tr

    def body(*refs):
        g_ref, p_ref, r0_ref, r1_ref, r2_ref = refs[1:6]
        o_ref = refs[-1]
        acc = g_ref[...] + p_ref[...]
        acc = acc + r0_ref[...].astype(f32)
        acc = acc + r1_ref[...].astype(f32)
        acc = acc + r2_ref[...].astype(f32)
        o_ref[...] = acc

    def rspec(k):
        return pl.BlockSpec((None, tr, c_n), functools.partial(lambda i, idx, kk: (kk, i, 0), kk=k))

    in_specs = [pl.BlockSpec((None, tr, c_n), lambda i, idx: (idx[0], idx[4] * nb + i, 0)),
                rspec(0), rspec(0), rspec(1), rspec(2)]
    args = [idx, g, pair, recv, recv, recv]
    aliases = {}
    if into is not None:
        in_specs.append(pl.BlockSpec(memory_space=pl.ANY))
        args.append(into)
        aliases = {6: 0}
    grid_spec = pltpu.PrefetchScalarGridSpec(
        num_scalar_prefetch=1, grid=(nb,), in_specs=in_specs,
        out_specs=pl.BlockSpec((None, tr, c_n), lambda i, idx: (layer, idx[4] * nb + i, 0)))
    return pl.pallas_call(
        body, name=name, grid_spec=grid_spec,
        out_shape=jax.ShapeDtypeStruct((n_layers, 2 * r2, c_n), f32),
        input_output_aliases=aliases,
        compiler_params=_params("parallel"),
    )(*args)


def _adamw(w, g, m, v, layer, into, name):
    n_l, r, c_n = w.shape
    tr = _tile(r, 256, 8)

    def body(*refs):
        w_ref, g_ref, m_ref, v_ref = refs[:4]
        d_ref, mo_ref, vo_ref = refs[-3:]
        gv = g_ref[...]
        mn = ADAM_B1 * m_ref[...] + (1.0 - ADAM_B1) * gv
        vn = ADAM_B2 * v_ref[...] + (1.0 - ADAM_B2) * (gv * gv)
        m_hat = mn / (1.0 - ADAM_B1 ** ADAM_STEP)
        v_hat = vn / (1.0 - ADAM_B2 ** ADAM_STEP)
        d_ref[...] = -ADAM_LR * (m_hat / (jnp.sqrt(v_hat) + ADAM_EPS) + ADAM_WD * w_ref[...])
        mo_ref[...] = mn
        vo_ref[...] = vn

    blk = pl.BlockSpec((None, tr, c_n), lambda i: (layer, i, 0))
    shp = jax.ShapeDtypeStruct((n_l, r, c_n), f32)
    in_specs = [blk] * 4
    args = [w, g, m, v]
    aliases = {}
    if into is not None:
        in_specs = in_specs + [pl.BlockSpec(memory_space=pl.ANY)] * 3
        args += list(into)
        aliases = {4: 0, 5: 1, 6: 2}
    return pl.pallas_call(
        body, name=name, grid=(r // tr,),
        in_specs=in_specs, out_specs=[blk] * 3, out_shape=[shp] * 3,
        input_output_aliases=aliases,
        compiler_params=_params("parallel"),
    )(*args)


def kernel(x, meta, ffn1_wg, ffn1_wu, ffn1_wd, ffn2_wg, ffn2_wu, ffn2_wd, ln_gain, ln_bias, conv_w_in, conv_w, conv_w_out, kv_w, f_bias, attn_w_q, attn_w_o, loss_target, m_meta, m_ffn1_wg, m_ffn1_wu, m_ffn1_wd, m_ffn2_wg, m_ffn2_wu, m_ffn2_wd, m_ln_gain, m_ln_bias, m_conv_w_in, m_conv_w, m_conv_w_out, m_kv_w, m_f_bias, m_attn_w_q, m_attn_w_o, v_meta, v_ffn1_wg, v_ffn1_wu, v_ffn1_wd, v_ffn2_wg, v_ffn2_wu, v_ffn2_wd, v_ln_gain, v_ln_bias, v_conv_w_in, v_conv_w, v_conv_w_out, v_kv_w, v_f_bias, v_attn_w_q, v_attn_w_o):
    seq, d = x.shape[1], x.shape[2]
    t = PAD + N_META + seq
    n_heads = d // HEAD_DIM
    dq = d // N_CHIPS
    n_kv = kv_w.shape[1]
    x2 = x[0]
    target = loss_target[0]

    def rows8(a):
        return jnp.pad(a, ((0, 8 - a.shape[0]), (0, 0)))

    def small_pack(mt, g, b, cw, fb):
        fb_row = jnp.pad(fb, (0, mt.shape[1] - n_heads))[None]
        return jnp.concatenate([mt, rows8(g.reshape(6, -1)), rows8(b.reshape(6, -1)),
                                rows8(cw.reshape(3, -1)), rows8(fb_row)], axis=0)

    w_small = small_pack(meta, ln_gain, ln_bias, conv_w, f_bias)

    cx, cy, c = lax.axis_index("x"), lax.axis_index("y"), lax.axis_index("c")
    idx = jnp.stack([2 * cx + cy, 2 * (1 - cx) + cy, 2 * cx + (1 - cy), 2 * (1 - cx) + (1 - cy), c]
                    ).astype(jnp.int32)

    def tr(a):
        return a.transpose(0, 2, 1)

    w3 = {"wg1": tr(ffn1_wg), "wu1": tr(ffn1_wu), "wd1": ffn1_wd, "wg2": tr(ffn2_wg),
          "wu2": tr(ffn2_wu), "wd2": ffn2_wd, "win": conv_w_in, "wout": conv_w_out, "kv": kv_w[None],
          "wq": attn_w_q, "wo": attn_w_o, "small": w_small[None]}
    transposed = ("wg1", "wu1", "wg2", "wu2")
    buf = {n: _place_shard(w, idx, "place_shard") for n, w in w3.items()}

    def split(item):
        name, layer = item.split(".")
        return name, int(layer)

    def gather_stage(planner, items):
        triples = [(n, l, buf[n].shape[2] // 2) for n, l in map(split, items)]
        plan, n_copies = planner(triples)
        return dict(plan=plan, n=n_copies, bufs={n: buf[n] for n, _, _ in triples})

    def ici(items):
        return gather_stage(_plan_gather_ici, items)

    def d2d(items):
        return gather_stage(_plan_gather_d2d, items)

    def pair_exchange(items):
        r2s = [buf["G_" + it].shape[1] // 2 for it in items]
        plan, n_copies = _plan_pair_exchange(items, r2s)
        news = {"PAIR_" + it: jax.ShapeDtypeStruct((N_CHIPS, r2, buf["G_" + it].shape[2]), f32)
                for it, r2 in zip(items, r2s)}
        return dict(plan=plan, n=n_copies, srcs={"G_" + it: buf["G_" + it] for it in items}, news=news)

    def chip_exchange(items):
        plan, n_copies = _plan_chip_exchange(items)
        srcs = {"SEND_" + it: buf["SEND_" + it] for it in items}
        news = {"RECV_" + it: jax.ShapeDtypeStruct(s.shape, s.dtype)
                for it, s in ((it, buf["SEND_" + it]) for it in items)}
        return dict(plan=plan, n=n_copies, srcs=srcs, news=news)

    def pair_share(items):
        triples = [(n, l, buf["RED_" + n].shape[1] // 2) for n, l in map(split, items)]
        plan, n_copies = _plan_pair_share(triples)
        return dict(plan=plan, n=n_copies, bufs={"RED_" + n: buf["RED_" + n] for n, _, _ in triples})

    def run(fn, *args, stages=(), name=None, **kw):
        comm = _Copies()
        for st in (stages() if callable(stages) else stages):
            comm.add(st["plan"], st["n"], srcs=st.get("srcs"), bufs=st.get("bufs"), news=st.get("news"))
        out = _copy_call(comm, name) if fn is None else fn(*args, comm=comm, **kw)
        buf.update(comm.out_bufs)
        buf.update(comm.out_news)
        return out

    first = ["wg1.0", "wu1.0", "small.0"]
    run(None, stages=[ici(first)], name="gather_first_ici")
    run(None, stages=[d2d(first)], name="gather_first_d2d")
    small = buf["small"].reshape(N_CHIPS, SMALL_ROWS, dq).transpose(1, 0, 2).reshape(SMALL_ROWS, d)
    meta_full = small[:N_META]
    gains = small[16:22].reshape(DEPTH, 3, 1, d)
    biases = small[24:30].reshape(DEPTH, 3, 1, d)
    conv_w_full = small[32:35]
    fb_pad = jnp.pad(f_bias, (0, LANES - n_heads))[None]
    conv_w8 = ["wd1.0", "win.0", "wout.0"]
    ffn2_l0 = ["wg2.0", "wu2.0", "wd2.0", "kv.0"]
    attn_ffn2_l1 = ["wq.0", "wo.0", "wg2.1", "wu2.1", "wd2.1"]
    ffn1_l1 = ["wg1.1", "wu1.1", "wd1.1"]

    meta_pad = jnp.concatenate([jnp.zeros((PAD, d), f32), meta_full], axis=0)
    h0, h0b = run(_embed, meta_pad, x2, "embed", stages=[ici(conv_w8)])
    a1, b1, s1 = run(_ffn_up, h0b, buf["wg1"], buf["wu1"], 0, "ffn_up",
                     stages=lambda: [d2d(conv_w8), ici(ffn2_l0)])
    r1, h1, h1b = run(_down_ln, s1, buf["wd1"], 0, h0, gains[0, 0], biases[0, 0], 0.5, "ffn_down_ln",
                      stages=lambda: [d2d(ffn2_l0), ici(attn_ffn2_l1)])
    n_in = conv_w_in.shape[-1]
    w_in = buf["win"].reshape(N_CHIPS, d, n_in)
    w_out = buf["wout"].reshape(1, 1, d, d)
    p = run(_nn_matmul, h1b, w_in, f32, "conv_in", stages=lambda: [d2d(attn_ffn2_l1), ici(ffn1_l1)])
    z = _conv_fwd(p, conv_w_full, "conv_fwd")
    r2, h2, h2b = run(_down_ln, z[None], w_out, 0, h1, gains[0, 1], biases[0, 1], 1.0, "mix_out_ln",
                      stages=lambda: [d2d(ffn1_l1)])
    wg1, wu1, wd1, wg2, wu2, wd2 = (buf[n] for n in ("wg1", "wu1", "wd1", "wg2", "wu2", "wd2"))
    w_q = buf["wq"].reshape(1, d, d)
    w_o = buf["wo"].reshape(1, 1, d, d)
    kv_full = buf["kv"].reshape(N_CHIPS, d, n_kv).transpose(1, 0, 2).reshape(d, N_CHIPS * n_kv)
    w_k = kv_full[:, :d][None]
    w_v = kv_full[:, d:2 * d][None]
    w_f = jnp.pad(kv_full[:, 2 * d:], ((0, 0), (0, LANES - n_heads)))[None]
    a2, b2, s2 = _ffn_up(h2b, wg2, wu2, 0, "ffn_up")
    r3, h3, h3b = _down_ln(s2, wd2, 0, h2, gains[0, 2], biases[0, 2], 0.5, "ffn_down_ln")
    kk = _nn_matmul(h3b, w_k, bf16, "proj_bf16")
    vv = _nn_matmul(h3b, w_v, bf16, "proj_bf16")
    flog = _nn_matmul(h3b, w_f, f32, "proj_gate")
    cum = _row_scan(flog, "gate_cumsum", fb_pad)
    bk = _tile(t, 640, LANES)
    c_ht = cum[:, :n_heads].T
    c_keys = jnp.where(jnp.arange(t)[None, :] < PAD, 1e30, c_ht)
    cq_rep = jnp.broadcast_to(c_ht[:, :, None], (n_heads, t, LANES))
    ck_rep = jnp.broadcast_to(c_keys[:, :, None], (n_heads, t, LANES))
    ck_rows = c_keys.reshape(n_heads, t // bk, 1, bk)
    a3, b3, s3 = _ffn_up(h3b, wg1, wu1, 1, "ffn_up")
    r4, h4, h4b = _down_ln(s3, wd1, 1, h3, gains[1, 0], biases[1, 0], 0.5, "ffn_down_ln")
    q = _nn_matmul(h4b, w_q, bf16, "proj_q", out_scale=1.0 / math.sqrt(HEAD_DIM))
    o, o32, lse_rep = _attn_fwd(q, kk, vv, cq_rep, ck_rows, "attn_fwd")
    r5, h5, h5b = _down_ln(o[None], w_o, 0, h4, gains[1, 1], biases[1, 1], 1.0, "mix_out_ln")
    a4, b4, s4 = _ffn_up(h5b, wg2, wu2, 1, "ffn_up")
    r6, h6, _ = _down_ln(s4, wd2, 1, h5, gains[1, 2], biases[1, 2], 0.5, "ffn_down_ln")
    dr6, dr6b, dg12, db12, sq = _loss_head(h6, target, r6, gains[1, 2], "loss_head")
    loss_part = 0.5 * sq[0, 0] / d

    m_small = small_pack(m_meta, m_ln_gain, m_ln_bias, m_conv_w, m_f_bias)
    v_small = small_pack(v_meta, v_ln_gain, v_ln_bias, v_conv_w, v_f_bias)
    m3 = {"wg1": tr(m_ffn1_wg), "wu1": tr(m_ffn1_wu), "wd1": m_ffn1_wd, "wg2": tr(m_ffn2_wg),
          "wu2": tr(m_ffn2_wu), "wd2": m_ffn2_wd, "win": m_conv_w_in, "wout": m_conv_w_out,
          "kv": m_kv_w[None], "wq": m_attn_w_q, "wo": m_attn_w_o, "small": m_small[None]}
    v3 = {"wg1": tr(v_ffn1_wg), "wu1": tr(v_ffn1_wu), "wd1": v_ffn1_wd, "wg2": tr(v_ffn2_wg),
          "wu2": tr(v_ffn2_wu), "wd2": v_ffn2_wd, "win": v_conv_w_in, "wout": v_conv_w_out,
          "kv": v_kv_w[None], "wq": v_attn_w_q, "wo": v_attn_w_o, "small": v_small[None]}
    stepped = {}

    def prepare(items):
        for it in items:
            buf["SEND_" + it] = _rs_prepare(buf["G_" + it], buf["PAIR_" + it], idx, "grad_prepare")

    def finish(items):
        for it in items:
            n, l = split(it)
            buf["RED_" + n] = _rs_finish(buf["G_" + it], buf["PAIR_" + it], buf["RECV_" + it], idx, l,
                                         w3[n].shape[0], buf.get("RED_" + n), "grad_finish")

    def adam(items, grads=None):
        for it in items:
            n, l = split(it)
            g = buf["RED_" + n] if grads is None else grads[n]
            stepped[n] = _adamw(w3[n], g, m3[n], v3[n], l, stepped.get(n), "adamw")

    def ffn_bwd(dr, drb, hb_in, a, b, s, f, layer, on_act=(), after_act=None, on_dwd=(),
                after_dwd=None, on_dx=(), ln=None):
        da, db = run(_ffn_bwd_act, drb, buf["wd" + f], layer, a, b, "ffn_bwd_act", stages=on_act)
        if after_act is not None:
            after_act()
        (buf[f"G_wd{f}.{layer}"],) = run(_tn_matmul, [(s, "stack", drb, "shared")], N_CHIPS, 0.5,
                                         "ffn_dwd", stages=on_dwd)
        if after_dwd is not None:
            after_dwd()
        buf[f"G_wg{f}.{layer}"], buf[f"G_wu{f}.{layer}"] = _tn_matmul(
            [(da, "stack", hb_in, "shared"), (db, "stack", hb_in, "shared")], N_CHIPS, 1.0, "ffn_dwgu")
        return run(_nt_sum, [(da, "stack", buf["wg" + f], layer, True),
                             (db, "stack", buf["wu" + f], layer, True)],
                   dr, ALPHA, N_CHIPS, f32, "ffn_dx", stages=on_dx, ln=ln)

    ffn2_1 = ["wg2.1", "wu2.1", "wd2.1"]
    ffn1_1 = ["wg1.1", "wu1.1", "wd1.1"]
    ffn2_0 = ["wg2.0", "wu2.0", "wd2.0"]
    conv_items = ["wout.0", "win.0"]

    dr5, dr5b, dg11, db11 = ffn_bwd(dr6, dr6b, h5b, a4, b4, s4, "2", 1,
                                    on_dx=lambda: [pair_exchange(ffn2_1)], ln=(r5, gains[1, 1]))
    prepare(ffn2_1)
    (dwo,) = _tn_matmul([(o, "shared", dr5b, "shared")], 1, 1.0, "sq_dw")
    buf["G_wo.0"] = dwo.reshape(N_CHIPS, dq, d)
    do = run(_nt_sum, [(dr5b, "cols", w_o[0], None)], None, 1.0, 1, bf16, "sq_dx_bf16",
             stages=lambda: [pair_exchange(["wo.0"])])
    prepare(["wo.0"])
    cl, delta = _attn_stats(o32, do, lse_rep, cq_rep, bk, "attn_stats")
    kt = kk.reshape(t // bk, bk, n_heads, HEAD_DIM).transpose(2, 0, 3, 1)
    dq_att, dc_q, dk, dv, dc_k = run(_attn_bwd, q, kk, vv, kt, do, ck_rep, cl, delta, "attn_bwd",
                                     stages=lambda: [chip_exchange(ffn2_1 + ["wo.0"])])
    finish(ffn2_1 + ["wo.0"])
    dc = (dc_q + dc_k).reshape(n_heads, t)
    (dwq,) = run(_tn_matmul, [(h4b, "shared", dq_att, "shared")], 1, 1.0, "sq_dw",
                 stages=lambda: [pair_share(ffn2_1 + ["wo.0"])])
    buf["G_wq.0"] = dwq.reshape(N_CHIPS, dq, d)
    adam(ffn2_1 + ["wo.0"])
    dr4, dr4b, dg10, db10 = run(_nt_sum, [(dq_att, "cols", w_q, None)], dr5, ALPHA, 1, f32,
                                "sq_dx_res", stages=lambda: [pair_exchange(["wq.0"])],
                                ln=(r4, gains[1, 0]))
    prepare(["wq.0"])
    dh3a = ffn_bwd(dr4, dr4b, h3b, a3, b3, s3, "1", 1,
                   on_act=lambda: [chip_exchange(["wq.0"])],
                   on_dx=lambda: [pair_exchange(ffn1_1)])
    prepare(ffn1_1)
    finish(["wq.0"])
    dc_t = jnp.pad(dc.T, ((0, 0), (0, LANES - n_heads)))
    dlogf = _row_scan(dc_t, "rev_cumsum", reverse=True)
    dfl, dfb_cols = _fgate_bwd(dlogf, flog, fb_pad, "gate_bwd")
    dwk, dwv, dwf = run(_tn_matmul, [(h3b, "shared", g, "shared") for g in (dk, dv, dfl)], 1, 1.0,
                        "kv_dw",
                        stages=lambda: [chip_exchange(ffn1_1), pair_share(["wq.0"])])
    adam(["wq.0"])

    def by_chip(full):
        rows = full.shape[0]
        return full.reshape(rows, N_CHIPS, full.shape[1] // N_CHIPS).transpose(1, 0, 2)

    buf["G_kv.0"] = by_chip(jnp.concatenate([dwk[0], dwv[0], dwf[0][:, :n_heads]], axis=1))
    dr3, dr3b, dg02, db02 = run(
        _nt_sum, [(dk, "cols", w_k, None), (dv, "cols", w_v, None), (dfl, "cols", w_f, None)],
        dh3a, 1.0, 1, f32, "kv_dx", stages=lambda: [pair_exchange(["kv.0"])], ln=(r3, gains[0, 2]))
    prepare(["kv.0"])
    finish(ffn1_1)
    dr2, dr2b, dg01, db01 = ffn_bwd(dr3, dr3b, h2b, a2, b2, s2, "2", 0,
                                    on_act=lambda: [chip_exchange(["kv.0"]), pair_share(ffn1_1)],
                                    after_act=lambda: (adam(ffn1_1), finish(["kv.0"])),
                                    on_dwd=lambda: [pair_share(["kv.0"])],
                                    after_dwd=lambda: adam(["kv.0"]),
                                    on_dx=lambda: [pair_exchange(ffn2_0)], ln=(r2, gains[0, 1]))
    prepare(ffn2_0)
    (dwout,) = _tn_matmul([(z, "shared", dr2b, "shared")], 1, 1.0, "sq_dw")
    buf["G_wout.0"] = dwout.reshape(N_CHIPS, dq, d)
    dz = _nt_sum([(dr2b, "cols", w_out[0], None)], None, 1.0, 1, f32, "sq_dx_f32")
    dp, dconv_w = run(_conv_bwd, dz, p, conv_w_full, "conv_bwd",
                      stages=lambda: [chip_exchange(ffn2_0)])
    (buf["G_win.0"],) = _tn_matmul([(h1b, "shared", dp, "cols")], N_CHIPS, 1.0, "conv_dwin")
    finish(ffn2_0)
    dr1, dr1b, dg00, db00 = run(_nt_sum, [(dp, "cols", w_in, None)], dr2, ALPHA, N_CHIPS, f32,
                                "conv_dx",
                                stages=lambda: [pair_exchange(conv_items), pair_share(ffn2_0)],
                                ln=(r1, gains[0, 0]))
    prepare(conv_items)
    adam(ffn2_0)
    gate_up = ["wg1.0", "wu1.0"]
    da, db = run(_ffn_bwd_act, dr1b, buf["wd1"], 0, a1, b1, "ffn_bwd_act",
                 stages=lambda: [chip_exchange(conv_items)])
    finish(conv_items)
    buf["G_wg1.0"], buf["G_wu1.0"] = run(
        _tn_matmul, [(da, "stack", h0b, "shared"), (db, "stack", h0b, "shared")], N_CHIPS, 1.0,
        "ffn_dwgu", stages=lambda: [pair_share(conv_items)])
    adam(conv_items)
    (buf["G_wd1.0"],) = run(_tn_matmul, [(s1, "stack", dr1b, "shared")], N_CHIPS, 0.5, "ffn_dwd",
                            stages=lambda: [pair_exchange(gate_up)])
    prepare(gate_up)
    dh0 = run(_nt_sum, [(da, "stack", buf["wg1"], 0, True), (db, "stack", buf["wu1"], 0, True)],
              dr1, ALPHA, N_CHIPS, f32, "ffn_dx",
              stages=lambda: [chip_exchange(gate_up), pair_exchange(["wd1.0"])])
    prepare(["wd1.0"])
    finish(gate_up)
    grad_x = dh0[PAD + N_META:][None]
    dmeta = dh0[PAD:PAD + N_META]
    buf["G_small.0"] = by_chip(jnp.concatenate(
        [dmeta, rows8(jnp.concatenate([dg00, dg01, dg02, dg10, dg11, dg12], axis=0)),
         rows8(jnp.concatenate([db00, db01, db02, db10, db11, db12], axis=0)),
         rows8(dconv_w), jnp.zeros((8, d), f32)], axis=0))
    run(None, stages=lambda: [chip_exchange(["wd1.0"]), pair_exchange(["small.0"]), pair_share(gate_up)],
        name="grad_tail_1")
    prepare(["small.0"])
    finish(["wd1.0"])
    adam(gate_up)
    run(None, stages=lambda: [chip_exchange(["small.0"]), pair_share(["wd1.0"])], name="grad_tail_2")
    finish(["small.0"])
    adam(["wd1.0"])
    run(None, stages=lambda: [pair_share(["small.0"])], name="grad_tail_3")

    tail = jnp.zeros((LANES,), f32).at[:n_heads].set(dfb_cols[0, :n_heads]).at[n_heads].set(loss_part)
    tail = lax.psum(tail, ("x", "y", "c"))
    loss = tail[n_heads]
    g_fb = tail[:n_heads]
    g_small = jnp.concatenate([buf["RED_small"][0, :40],
                               rows8(jnp.pad(g_fb, (0, dq - n_heads))[None])], axis=0)
    adam(["small.0"], grads={"small": g_small[None]})

    def unpack(pk):
        return (pk[:16], pk[16:22].reshape(DEPTH, 3, dq), pk[24:30].reshape(DEPTH, 3, dq),
                pk[32:35].reshape(1, 3, dq), pk[40, :n_heads])

    def order(pick):
        mt, g, b, cw, fb = unpack(pick("small")[0])
        big = {n: pick(n) for n in w3 if n != "small"}
        big["kv"] = big["kv"][0]
        for n in transposed:
            big[n] = tr(big[n])
        return [mt, big["wg1"], big["wu1"], big["wd1"], big["wg2"], big["wu2"], big["wd2"], g, b,
                big["win"], cw, big["wout"], big["kv"], fb, big["wq"], big["wo"]]

    grads_out = dict({n: buf["RED_" + n] for n in w3}, small=g_small[None])
    return (loss, grad_x, *order(lambda n: grads_out[n]), *order(lambda n: stepped[n][0]),
            *order(lambda n: stepped[n][1]), *order(lambda n: stepped[n][2]))
```

```python
import functools
import math

import jax
import jax.numpy as jnp
from jax import lax
from jax.experimental import pallas as pl
from jax.experimental.pallas import tpu as pltpu

f32 = jnp.float32
bf16 = jnp.bfloat16

N_META = 16
PAD = 112
HEAD_DIM = 128
DEPTH = 2
LN_EPS = 1e-5
ALPHA = (2 * DEPTH) ** 0.25
NEG_INF = -1e30
N_CHIPS = 4
SMALL_ROWS = 48
LANES = 128

ADAM_LR = 0.001
ADAM_B1 = 0.9
ADAM_B2 = 0.999
ADAM_EPS = 1e-08
ADAM_WD = 0.01
ADAM_STEP = 10

VMEM_LIMIT_BYTES = 56 * 1024 * 1024
ROWS_WIDE = 1664
ROWS_ACC = 1040
ROWS_ACC_LN = 832
ROWS_FUSED = 640
STRIP = 16
MESH = pl.DeviceIdType.MESH

NT_DIMS = (((1,), (1,)), ((), ()))
TN_DIMS = (((0,), (0,)), ((), ()))


def _tile(n, target, mult):
    best = None
    for d in range(mult, min(n, target) + 1, mult):
        if n % d == 0:
            best = d
    assert best is not None, (n, target, mult)
    return best


def _params(*sem):
    return pltpu.CompilerParams(dimension_semantics=sem, vmem_limit_bytes=VMEM_LIMIT_BYTES)


class _Place:
    def __init__(self):
        self.cx, self.cy, self.c = lax.axis_index("x"), lax.axis_index("y"), lax.axis_index("c")
        self.chips = [(1 - self.cx, self.cy), (self.cx, 1 - self.cy), (1 - self.cx, 1 - self.cy)]
        self.me = 2 * self.cx + self.cy
        self.slots = [self.me] + [2 * px + py for px, py in self.chips]
        self.sib = (self.cx, self.cy, 1 - self.c)


class _Copies:
    def __init__(self):
        self.srcs, self.bufs, self.news = {}, {}, {}
        self.plans = []
        self.out_bufs, self.out_news = {}, {}

    def add(self, plan, n_copies, srcs=None, bufs=None, news=None):
        for have, more in ((self.srcs, srcs), (self.bufs, bufs), (self.news, news)):
            for key, val in (more or {}).items():
                assert key not in have or have[key] is val, key
                have[key] = val
        self.plans.append((plan, n_copies))

    def empty(self):
        return not self.plans

    def count(self):
        return sum(n for _, n in self.plans)

    def copies(self, src_refs, buf_refs, new_refs, send, recv):
        place = _Place()
        srcs = dict(zip(self.srcs, src_refs))
        bufs = dict(zip(self.bufs, buf_refs))
        news = dict(zip(self.news, new_refs))
        out = []
        for plan, n_copies in self.plans:
            triples = plan(srcs, bufs, news, place)
            assert len(triples) == n_copies
            for src, dst, dev in triples:
                n = len(out)
                out.append(pltpu.make_async_remote_copy(
                    src_ref=src, dst_ref=dst, send_sem=send.at[n], recv_sem=recv.at[n],
                    device_id=dev, device_id_type=MESH))
        return out

    def land(self, results):
        n_b = len(self.bufs)
        self.out_bufs = dict(zip(self.bufs, results[:n_b]))
        self.out_news = dict(zip(self.news, results[n_b:]))


def _pallas(comm, body, *, name, grid, in_specs, out_specs, out_shape, compiler_params,
            scratch_shapes=(), input_output_aliases=None):
    aliases = dict(input_output_aliases or {})
    if comm is None or comm.empty():
        return pl.pallas_call(body, name=name, grid=grid, in_specs=in_specs, out_specs=out_specs,
                              out_shape=out_shape, scratch_shapes=list(scratch_shapes),
                              input_output_aliases=aliases, compiler_params=compiler_params)
    single = not isinstance(out_shape, (list, tuple))
    out_shapes = [out_shape] if single else list(out_shape)
    out_specs_l = [out_specs] if single else list(out_specs)
    n_in, n_out, n_scr = len(in_specs), len(out_shapes), len(scratch_shapes)
    n_s, n_b, n_n = len(comm.srcs), len(comm.bufs), len(comm.news)
    n_copies = comm.count()

    def wrapped(*refs):
        ins = refs[:n_in]
        src_refs = refs[n_in:n_in + n_s]
        o0 = n_in + n_s + n_b
        outs = refs[o0:o0 + n_out]
        buf_refs = refs[o0 + n_out:o0 + n_out + n_b]
        new_refs = refs[o0 + n_out + n_b:o0 + n_out + n_b + n_n]
        rest = refs[o0 + n_out + n_b + n_n:]
        scratch, (send, recv) = rest[:n_scr], rest[n_scr:]
        ids = [pl.program_id(a) for a in range(len(grid))]
        first = functools.reduce(jnp.logical_and, [i == 0 for i in ids])
        last = functools.reduce(jnp.logical_and, [i == g - 1 for i, g in zip(ids, grid)])

        @pl.when(first)
        def _():
            for cp in comm.copies(src_refs, buf_refs, new_refs, send, recv):
                cp.start()

        body(*ins, *outs, *scratch)

        @pl.when(last)
        def _():
            for cp in comm.copies(src_refs, buf_refs, new_refs, send, recv):
                cp.wait()

    hbm = pl.BlockSpec(memory_space=pl.ANY)
    for j in range(n_b):
        aliases[n_in + n_s + j] = n_out + j
    call = pl.pallas_call(
        wrapped, name=name, grid=grid,
        in_specs=[*in_specs, *([hbm] * (n_s + n_b))],
        out_specs=[*out_specs_l, *([hbm] * (n_b + n_n))],
        out_shape=[*out_shapes,
                   *[jax.ShapeDtypeStruct(a.shape, a.dtype) for a in comm.bufs.values()],
                   *comm.news.values()],
        scratch_shapes=[*scratch_shapes, pltpu.SemaphoreType.DMA((n_copies,)),
                        pltpu.SemaphoreType.DMA((n_copies,))],
        input_output_aliases=aliases, compiler_params=compiler_params)

    def run(*args):
        res = call(*args, *comm.srcs.values(), *comm.bufs.values())
        comm.land(res[n_out:])
        return res[0] if single else res[:n_out]

    return run


def _copy_call(comm, name):
    n_s, n_b, n_n = len(comm.srcs), len(comm.bufs), len(comm.news)
    n_copies = comm.count()

    def body(*refs):
        src_refs = refs[:n_s]
        buf_refs = refs[n_s + n_b:n_s + 2 * n_b]
        new_refs = refs[n_s + 2 * n_b:n_s + 2 * n_b + n_n]
        send, recv = refs[n_s + 2 * n_b + n_n:]
        copies = comm.copies(src_refs, buf_refs, new_refs, send, recv)
        for cp in copies:
            cp.start()
        for cp in copies:
            cp.wait()

    hbm = pl.BlockSpec(memory_space=pl.ANY)
    res = pl.pallas_call(
        body, name=name,
        in_specs=[hbm] * (n_s + n_b), out_specs=[hbm] * (n_b + n_n),
        out_shape=[*[jax.ShapeDtypeStruct(a.shape, a.dtype) for a in comm.bufs.values()],
                   *comm.news.values()],
        input_output_aliases={n_s + j: j for j in range(n_b)},
        scratch_shapes=[pltpu.SemaphoreType.DMA((n_copies,)), pltpu.SemaphoreType.DMA((n_copies,))],
    )(*comm.srcs.values(), *comm.bufs.values())
    comm.land(res)


def _embed(meta_pad, x, name, comm=None):
    seq, d = x.shape
    t = seq + LANES

    def body(m_ref, x_ref, h_ref, hb_ref):
        first = pl.program_id(0) == 0
        v = jnp.where(first, m_ref[...], x_ref[...])
        h_ref[...] = v
        hb_ref[...] = v.astype(bf16)

    return _pallas(
        comm, body, name=name, grid=(t // LANES,),
        in_specs=[pl.BlockSpec((LANES, d), lambda i: (0, 0)),
                  pl.BlockSpec((LANES, d), lambda i: (jnp.maximum(i - 1, 0), 0))],
        out_specs=[pl.BlockSpec((LANES, d), lambda i: (i, 0)),
                   pl.BlockSpec((LANES, d), lambda i: (i, 0))],
        out_shape=[jax.ShapeDtypeStruct((t, d), f32), jax.ShapeDtypeStruct((t, d), bf16)],
        compiler_params=_params("parallel"),
    )(meta_pad, x)


def _nn_matmul(x, w, out_dtype, name, comm=None, out_scale=None):
    t, k = x.shape
    s_n, _, n = w.shape
    assert s_n == 1 or n % LANES == 0
    tm = _tile(t, ROWS_WIDE, 16)

    def body(x_ref, w_ref, o_ref):
        res = jnp.dot(x_ref[...].astype(bf16), w_ref[...], preferred_element_type=f32)
        if out_scale is not None:
            res = res * out_scale
        o_ref[...] = res.astype(o_ref.dtype)

    return _pallas(
        comm, body, name=name, grid=(s_n, t // tm),
        in_specs=[pl.BlockSpec((tm, k), lambda s, i: (i, 0)),
                  pl.BlockSpec((None, k, n), lambda s, i: (s, 0, 0))],
        out_specs=pl.BlockSpec((tm, n), lambda s, i: (i, s)),
        out_shape=jax.ShapeDtypeStruct((t, s_n * n), out_dtype),
        compiler_params=_params("parallel", "parallel"),
    )(x, w)


def _ffn_up(hb, wg, wu, layer, name, comm=None):
    t, d = hb.shape
    s_n, _, n, _ = wg.shape
    tm = _tile(t, ROWS_WIDE, 16)

    def body(x_ref, wg_ref, wu_ref, a_ref, b_ref, s_ref):
        x = x_ref[...]
        a = lax.dot_general(x, wg_ref[...], NT_DIMS, preferred_element_type=f32)
        b = lax.dot_general(x, wu_ref[...], NT_DIMS, preferred_element_type=f32)
        a_ref[...] = a.astype(bf16)
        b_ref[...] = b.astype(bf16)
        s_ref[...] = (a * jax.nn.sigmoid(a) * b).astype(bf16)

    wspec = pl.BlockSpec((None, None, n, d), lambda s, i: (s, layer, 0, 0))
    ospec = pl.BlockSpec((None, tm, n), lambda s, i: (s, i, 0))
    return _pallas(
        comm, body, name=name, grid=(s_n, t // tm),
        in_specs=[pl.BlockSpec((tm, d), lambda s, i: (i, 0)), wspec, wspec],
        out_specs=[ospec, ospec, ospec],
        out_shape=[jax.ShapeDtypeStruct((s_n, t, n), bf16)] * 3,
        compiler_params=_params("parallel", "parallel"),
    )(hb, wg, wu)


def _down_ln(x, w, layer, hprev, gain, bias, beta, name, comm=None):
    s_n, t, k = x.shape
    d = w.shape[-1]
    tm = _tile(t, ROWS_ACC, 16)

    def body(x_ref, w_ref, h_ref, g_ref, b_ref, r_out, h_out, hb_out, acc):
        s = pl.program_id(1)

        @pl.when(s == 0)
        def _():
            acc[...] = jnp.zeros_like(acc)

        acc[...] += jnp.dot(x_ref[...], w_ref[...], preferred_element_type=f32)

        @pl.when(s == s_n - 1)
        def _():
            r = ALPHA * h_ref[...] + beta * acc[...]
            mu = jnp.mean(r, axis=-1, keepdims=True)
            xc = r - mu
            var = jnp.mean(xc * xc, axis=-1, keepdims=True)
            y = xc * lax.rsqrt(var + LN_EPS) * g_ref[...] + b_ref[...]
            r_out[...] = r
            h_out[...] = y
            hb_out[...] = y.astype(bf16)

    row = pl.BlockSpec((tm, d), lambda i, s: (i, 0))
    vec = pl.BlockSpec((1, d), lambda i, s: (0, 0))
    return _pallas(
        comm, body, name=name, grid=(t // tm, s_n),
        in_specs=[pl.BlockSpec((None, tm, k), lambda i, s: (s, i, 0)),
                  pl.BlockSpec((None, None, k, d), lambda i, s: (s, layer, 0, 0)),
                  row, vec, vec],
        out_specs=[row, row, row],
        out_shape=[jax.ShapeDtypeStruct((t, d), f32), jax.ShapeDtypeStruct((t, d), f32),
                   jax.ShapeDtypeStruct((t, d), bf16)],
        scratch_shapes=[pltpu.VMEM((tm, d), f32)],
        compiler_params=_params("parallel", "arbitrary"),
    )(x, w, hprev, gain, bias)


def _conv_fwd(p, conv_w, name):
    t, d3 = p.shape
    d = d3 // 3
    tm = _tile(t, 320, 8)
    hb = tm // 8

    def body(p_ref, prev_ref, w_ref, z_ref):
        i = pl.program_id(0)
        rows = i * tm - 8 + lax.broadcasted_iota(jnp.int32, (tm + 8, 1), 0)
        cg = jnp.concatenate([prev_ref[:, d:2 * d], p_ref[:, d:2 * d]], axis=0)
        val = jnp.concatenate([prev_ref[:, 2 * d:], p_ref[:, 2 * d:]], axis=0)
        u = jnp.where(rows >= PAD, cg * val, 0.0)
        y = (w_ref[2:3, :] * u + w_ref[1:2, :] * pltpu.roll(u, 1, 0)
             + w_ref[0:1, :] * pltpu.roll(u, 2, 0))
        z_ref[...] = (p_ref[:, :d] * y[8:]).astype(bf16)

    return pl.pallas_call(
        body, name=name, grid=(t // tm,),
        in_specs=[pl.BlockSpec((tm, d3), lambda i: (i, 0)),
                  pl.BlockSpec((8, d3), lambda i: (jnp.maximum(i * hb - 1, 0), 0)),
                  pl.BlockSpec((3, d), lambda i: (0, 0))],
        out_specs=pl.BlockSpec((tm, d), lambda i: (i, 0)),
        out_shape=jax.ShapeDtypeStruct((t, d), bf16),
        compiler_params=_params("parallel"),
    )(p, p, conv_w)


def _row_scan(x, name, fbias=None, reverse=False):
    t, n = x.shape
    blk = LANES
    n_blk = t // blk
    gate = fbias is not None

    def body(*refs):
        if gate:
            x_ref, fb_ref, o_ref, carry = refs
        else:
            x_ref, o_ref, carry = refs
        i = pl.program_id(0)

        @pl.when(i == 0)
        def _():
            carry[...] = jnp.zeros_like(carry)

        v = x_ref[...]
        r = lax.broadcasted_iota(jnp.int32, (blk, n), 0)
        if gate:
            v = v + fb_ref[...]
            v = jnp.minimum(v, 0.0) - jnp.log1p(jnp.exp(-jnp.abs(v)))
            v = jnp.where(i * blk + r >= PAD, v, 0.0)
        sh = 1
        while sh < blk:
            if reverse:
                v = v + jnp.where(r < blk - sh, pltpu.roll(v, blk - sh, 0), 0.0)
            else:
                v = v + jnp.where(r >= sh, pltpu.roll(v, sh, 0), 0.0)
            sh *= 2
        v = v + carry[...]
        o_ref[...] = v
        carry[...] = o_ref[0:1, :] if reverse else o_ref[blk - 1:blk, :]

    order = (lambda i: (n_blk - 1 - i, 0)) if reverse else (lambda i: (i, 0))
    in_specs = [pl.BlockSpec((blk, n), order)]
    args = [x]
    if gate:
        in_specs.append(pl.BlockSpec((1, n), lambda i: (0, 0)))
        args.append(fbias)
    return pl.pallas_call(
        body, name=name, grid=(n_blk,),
        in_specs=in_specs,
        out_specs=pl.BlockSpec((blk, n), order),
        out_shape=jax.ShapeDtypeStruct((t, n), f32),
        scratch_shapes=[pltpu.VMEM((1, n), f32)],
        compiler_params=_params("arbitrary"),
    )(*args)


def _lanes(x, n):
    return jnp.concatenate([x] * (n // LANES), axis=1)


def _attn_fwd(q, k, v, cq_rep, ck_rows, name):
    t, d = q.shape
    n_heads = d // HEAD_DIM
    bk = ck_rows.shape[-1]
    bq = bk

    def lane_fold(x, op):
        out = x[:, :LANES]
        for c0 in range(LANES, bk, LANES):
            out = op(out, x[:, c0:c0 + LANES])
        return out

    def body(q_ref, k_ref, v_ref, cq_ref, ck_ref, o_ref, o32_ref, lse_ref,
             s_scr, p_scr, m_scr, l_scr, red_scr, acc_scr):
        i = pl.program_id(1)
        m_scr[...] = jnp.full_like(m_scr, NEG_INF)
        l_scr[...] = jnp.zeros_like(l_scr)
        acc_scr[...] = jnp.zeros_like(acc_scr)
        qb = q_ref[...]
        ahead = (lax.broadcasted_iota(jnp.int32, (STRIP, bk), 1)
                 - lax.broadcasted_iota(jnp.int32, (STRIP, bk), 0))

        def tile(j, diagonal):
            k0 = pl.multiple_of(j * bk, bk)
            s_scr[...] = lax.dot_general(qb, k_ref[pl.ds(k0, bk), :], NT_DIMS,
                                         preferred_element_type=f32)
            ck = ck_ref[j]
            for r in range(0, bq, STRIP):
                rows = slice(r, r + STRIP)
                s = s_scr[rows, :] + _lanes(cq_ref[rows, :], bk) - ck
                if diagonal:
                    s = jnp.where(ahead <= r, s, NEG_INF)
                s_scr[rows, :] = s
                red_scr[rows, :] = lane_fold(s, jnp.maximum)
            m_old = m_scr[...]
            m_new = jnp.maximum(m_old, jnp.broadcast_to(
                jnp.max(red_scr[...], axis=1, keepdims=True), (bq, LANES)))
            a = jnp.exp(m_old - m_new)
            m_scr[...] = m_new
            for r in range(0, bq, STRIP):
                rows = slice(r, r + STRIP)
                pr = jnp.exp(s_scr[rows, :] - _lanes(m_scr[rows, :], bk))
                red_scr[rows, :] = lane_fold(pr, jnp.add)
                p_scr[rows, :] = pr.astype(bf16)
            l_scr[...] = a * l_scr[...] + jnp.broadcast_to(
                jnp.sum(red_scr[...], axis=1, keepdims=True), (bq, LANES))
            acc_scr[...] = a * acc_scr[...] + jnp.dot(
                p_scr[...], v_ref[pl.ds(k0, bk), :], preferred_element_type=f32)

        def full_tile(j, carry):
            tile(j, False)
            return carry

        lax.fori_loop(0, i, full_tile, 0)
        tile(i, True)
        out = acc_scr[...] / l_scr[...]
        o_ref[...] = out.astype(bf16)
        o32_ref[...] = out
        lse_ref[...] = m_scr[...] + jnp.log(l_scr[...])

    qblk = pl.BlockSpec((bq, HEAD_DIM), lambda h, i: (i, h))
    head_rows = pl.BlockSpec((t, HEAD_DIM), lambda h, i: (0, h))
    rep = pl.BlockSpec((None, bq, LANES), lambda h, i: (h, i, 0))
    col = pltpu.VMEM((bq, LANES), f32)
    return pl.pallas_call(
        body, name=name, grid=(n_heads, t // bq),
        in_specs=[qblk, head_rows, head_rows, rep,
                  pl.BlockSpec((None, t // bk, 1, bk), lambda h, i: (h, 0, 0, 0))],
        out_specs=[qblk, qblk, rep],
        out_shape=[jax.ShapeDtypeStruct((t, d), bf16), jax.ShapeDtypeStruct((t, d), f32),
                   jax.ShapeDtypeStruct((n_heads, t, LANES), f32)],
        scratch_shapes=[pltpu.VMEM((bq, bk), f32), pltpu.VMEM((bq, bk), bf16), col, col, col,
                        pltpu.VMEM((bq, HEAD_DIM), f32)],
        compiler_params=_params("parallel", "parallel"),
    )(q, k, v, cq_rep, ck_rows)


def _loss_head(h, target, r, gain, name):
    t, d = h.shape

    def body(h_ref, t_ref, r_ref, g_ref, dr_ref, drb_ref, dg_ref, db_ref, loss_ref):
        i = pl.program_id(0)

        @pl.when(i == 0)
        def _():
            loss_ref[...] = jnp.zeros_like(loss_ref)
            dg_ref[...] = jnp.zeros_like(dg_ref)
            db_ref[...] = jnp.zeros_like(db_ref)

        diff = jnp.where(i >= 1, h_ref[...] - t_ref[...], 0.0)
        loss_ref[...] += jnp.sum(diff * diff)
        dr, dg, db = _ln_bwd_rows(diff * (1.0 / d), r_ref[...], g_ref[...])
        dr_ref[...] = dr
        drb_ref[...] = dr.astype(bf16)
        dg_ref[...] += dg
        db_ref[...] += db

    row = pl.BlockSpec((LANES, d), lambda i: (i, 0))
    vec = pl.BlockSpec((1, d), lambda i: (0, 0))
    return pl.pallas_call(
        body, name=name, grid=(t // LANES,),
        in_specs=[row, pl.BlockSpec((LANES, d), lambda i: (jnp.maximum(i - 1, 0), 0)), row, vec],
        out_specs=[row, row, vec, vec, pl.BlockSpec((1, LANES), lambda i: (0, 0))],
        out_shape=[jax.ShapeDtypeStruct((t, d), f32), jax.ShapeDtypeStruct((t, d), bf16),
                   jax.ShapeDtypeStruct((1, d), f32), jax.ShapeDtypeStruct((1, d), f32),
                   jax.ShapeDtypeStruct((1, LANES), f32)],
        compiler_params=_params("arbitrary"),
    )(h, target, r, gain)


def _ln_bwd_rows(dy, rr, gain):
    mu = jnp.mean(rr, axis=-1, keepdims=True)
    xc = rr - mu
    var = jnp.mean(xc * xc, axis=-1, keepdims=True)
    rstd = lax.rsqrt(var + LN_EPS)
    xhat = xc * rstd
    dxh = dy * gain
    m1 = jnp.mean(dxh, axis=-1, keepdims=True)
    m2 = jnp.mean(dxh * xhat, axis=-1, keepdims=True)
    dr = rstd * (dxh - m1 - xhat * m2)
    return dr, jnp.sum(dy * xhat, axis=0, keepdims=True), jnp.sum(dy, axis=0, keepdims=True)


def _ffn_bwd_act(drb, wd, layer, a, b, name, comm=None):
    t, d = drb.shape
    s_n, _, n = a.shape
    tm = _tile(t, ROWS_WIDE, 16)

    def body(dr_ref, w_ref, a_ref, b_ref, da_ref, db_ref):
        ds = 0.5 * lax.dot_general(dr_ref[...], w_ref[...], NT_DIMS, preferred_element_type=f32)
        da_ref[...], db_ref[...] = _swiglu_bwd(ds, a_ref, b_ref)

    act = pl.BlockSpec((None, tm, n), lambda s, i: (s, i, 0))
    return _pallas(
        comm, body, name=name, grid=(s_n, t // tm),
        in_specs=[pl.BlockSpec((tm, d), lambda s, i: (i, 0)),
                  pl.BlockSpec((None, None, n, d), lambda s, i: (s, layer, 0, 0)), act, act],
        out_specs=[act, act],
        out_shape=[jax.ShapeDtypeStruct((s_n, t, n), bf16), jax.ShapeDtypeStruct((s_n, t, n), bf16)],
        compiler_params=_params("parallel", "parallel"),
    )(drb, wd, a, b)


def _swiglu_bwd(ds, a_ref, b_ref):
    av = a_ref[...].astype(f32)
    sig = jax.nn.sigmoid(av)
    da = ds * b_ref[...].astype(f32) * (sig * (1.0 + av * (1.0 - sig)))
    return da.astype(bf16), (ds * (av * sig)).astype(bf16)


def _ffn_bwd_dx(drb, dr, wd, wg, wu, layer, a, b, ln, name, comm=None):
    t, d = drb.shape
    s_n, _, n = a.shape
    tm = _tile(t, ROWS_FUSED, 16)
    n_ln = 0 if ln is None else 2

    def body(*refs):
        drb_ref, dr_ref, wd_ref, wg_ref, wu_ref, a_ref, b_ref = refs[:7]
        r_ref, g_ref = refs[7:7 + n_ln] if ln is not None else (None, None)
        da_ref, db_ref = refs[7 + n_ln:9 + n_ln]
        tail, acc = refs[9 + n_ln:-1], refs[-1]
        s = pl.program_id(1)

        if ln is not None:
            dri_ref, drib_ref, dg_ref, dbias_ref = tail

            @pl.when(jnp.logical_and(pl.program_id(0) == 0, s == 0))
            def _():
                dg_ref[...] = jnp.zeros_like(dg_ref)
                dbias_ref[...] = jnp.zeros_like(dbias_ref)

        ds = 0.5 * lax.dot_general(drb_ref[...], wd_ref[...], NT_DIMS, preferred_element_type=f32)
        da, db = _swiglu_bwd(ds, a_ref, b_ref)
        da_ref[...] = da
        db_ref[...] = db
        part = (jnp.dot(da, wg_ref[...], preferred_element_type=f32)
                + jnp.dot(db, wu_ref[...], preferred_element_type=f32))

        @pl.when(s == 0)
        def _():
            acc[...] = part

        @pl.when(s > 0)
        def _():
            acc[...] += part

        @pl.when(s == s_n - 1)
        def _():
            dh = ALPHA * dr_ref[...] + acc[...]
            if ln is None:
                tail[0][...] = dh
            else:
                dri, dg, dbias = _ln_bwd_rows(dh, r_ref[...], g_ref[...])
                dri_ref[...] = dri
                drib_ref[...] = dri.astype(bf16)
                dg_ref[...] += dg
                dbias_ref[...] += dbias

    row = pl.BlockSpec((tm, d), lambda i, s: (i, 0))
    vec = pl.BlockSpec((1, d), lambda i, s: (0, 0))
    act = pl.BlockSpec((None, tm, n), lambda i, s: (s, i, 0))
    wspec = pl.BlockSpec((None, None, n, d), lambda i, s: (s, layer, 0, 0))
    acts = jax.ShapeDtypeStruct((s_n, t, n), bf16)
    if ln is None:
        extra_in, extra_args = [], []
        out_specs = [act, act, row]
        out_shape = [acts, acts, jax.ShapeDtypeStruct((t, d), f32)]
    else:
        extra_in, extra_args = [row, vec], list(ln)
        out_specs = [act, act, row, row, vec, vec]
        out_shape = [acts, acts, jax.ShapeDtypeStruct((t, d), f32), jax.ShapeDtypeStruct((t, d), bf16),
                     jax.ShapeDtypeStruct((1, d), f32), jax.ShapeDtypeStruct((1, d), f32)]
    return _pallas(
        comm, body, name=name, grid=(t // tm, s_n),
        in_specs=[row, row, wspec, wspec, wspec, act, act, *extra_in],
        out_specs=out_specs, out_shape=out_shape,
        scratch_shapes=[pltpu.VMEM((tm, d), f32)],
        compiler_params=_params("arbitrary", "arbitrary"),
    )(drb, dr, wd, wg, wu, a, b, *extra_args)


def _act_spec(mode, tt, k, t_first):
    def fix(fn):
        return (lambda i, s: fn(s, i)) if t_first else fn
    if mode == "shared":
        return pl.BlockSpec((tt, k), fix(lambda s, i: (i, 0)))
    if mode == "cols":
        return pl.BlockSpec((tt, k), fix(lambda s, i: (i, s)))
    assert mode == "stack"
    return pl.BlockSpec((None, tt, k), fix(lambda s, i: (s, i, 0)))


def _act_width(arr, mode, s_n):
    return arr.shape[-1] // s_n if mode == "cols" else arr.shape[-1]


def _tn_matmul(pairs, s_n, scale, name, comm=None):
    t = pairs[0][0].shape[-2]
    tt = _tile(t, 2080, 16)
    n_t = t // tt
    arrays, specs, where = [], [], []
    for x, xmode, y, ymode in pairs:
        pos = []
        for arr, mode in ((x, xmode), (y, ymode)):
            hit = [j for j, a in enumerate(arrays) if a is arr]
            if not hit:
                arrays.append(arr)
                specs.append(_act_spec(mode, tt, _act_width(arr, mode, s_n), False))
                hit = [len(arrays) - 1]
            pos.append(hit[0])
        where.append(pos)
    widths = [(_act_width(x, xm, s_n), _act_width(y, ym, s_n)) for x, xm, y, ym in pairs]
    n_a = len(arrays)

    def body(*refs):
        i = pl.program_id(1)
        for (px, py), o_ref in zip(where, refs[n_a:]):
            part = lax.dot_general(refs[px][...].astype(bf16), refs[py][...].astype(bf16), TN_DIMS,
                                   preferred_element_type=f32)

            @pl.when(i == 0)
            def _():
                o_ref[...] = part

            @pl.when(i > 0)
            def _():
                o_ref[...] += part

            if scale != 1.0:
                @pl.when(i == n_t - 1)
                def _():
                    o_ref[...] = o_ref[...] * scale

    return _pallas(
        comm, body, name=name, grid=(s_n, n_t),
        in_specs=specs,
        out_specs=[pl.BlockSpec((None, kx, ky), lambda s, i: (s, 0, 0)) for kx, ky in widths],
        out_shape=[jax.ShapeDtypeStruct((s_n, kx, ky), f32) for kx, ky in widths],
        compiler_params=_params("parallel", "arbitrary"),
    )(*arrays)


def _nt_sum(pairs, base, base_scale, s_n, out_dtype, name, comm=None, ln=None):
    t = pairs[0][0].shape[-2]
    pairs = [(*pr, False)[:5] for pr in pairs]
    d = pairs[0][2].shape[-1] if pairs[0][4] else pairs[0][2].shape[-2]
    tm = _tile(t, ROWS_ACC if ln is None else ROWS_ACC_LN, 16)
    n_p = len(pairs)
    has_base = base is not None
    flipped = [pr[4] for pr in pairs]
    n_out = 1 if ln is None else 4

    def body(*refs):
        dy_refs = refs[0:2 * n_p:2]
        w_refs = refs[1:2 * n_p:2]
        rest = refs[2 * n_p:]
        base_ref = rest[0] if has_base else None
        o_refs, acc = rest[-1 - n_out:-1], rest[-1]
        s = pl.program_id(1)

        if ln is not None:
            @pl.when(jnp.logical_and(pl.program_id(0) == 0, s == 0))
            def _():
                o_refs[2][...] = jnp.zeros_like(o_refs[2])
                o_refs[3][...] = jnp.zeros_like(o_refs[3])

        @pl.when(s == 0)
        def _():
            acc[...] = jnp.zeros_like(acc)

        tot = None
        for dy_ref, w_ref, flip in zip(dy_refs, w_refs, flipped):
            dyv = dy_ref[...].astype(bf16)
            if flip:
                part = jnp.dot(dyv, w_ref[...], preferred_element_type=f32)
            else:
                part = lax.dot_general(dyv, w_ref[...], NT_DIMS, preferred_element_type=f32)
            tot = part if tot is None else tot + part
        acc[...] += tot

        @pl.when(s == s_n - 1)
        def _():
            res = acc[...]
            if has_base:
                res = base_scale * base_ref[...] + res
            if ln is None:
                o_refs[0][...] = res.astype(o_refs[0].dtype)
            else:
                r_ref, g_ref = rest[-7], rest[-6]
                dr, dg, db = _ln_bwd_rows(res, r_ref[...], g_ref[...])
                o_refs[0][...] = dr
                o_refs[1][...] = dr.astype(bf16)
                o_refs[2][...] += dg
                o_refs[3][...] += db

    in_specs, args = [], []
    for dy, mode, w, layer, flip in pairs:
        k = _act_width(dy, mode, s_n)
        in_specs.append(_act_spec(mode, tm, k, True))
        wshape = (k, d) if flip else (d, k)
        if layer is None:
            in_specs.append(pl.BlockSpec((None, *wshape), lambda i, s: (s, 0, 0)))
        else:
            in_specs.append(pl.BlockSpec((None, None, *wshape),
                                         functools.partial(lambda i, s, l: (s, l, 0, 0), l=layer)))
        args += [dy, w]
    row = pl.BlockSpec((tm, d), lambda i, s: (i, 0))
    vec = pl.BlockSpec((1, d), lambda i, s: (0, 0))
    if has_base:
        in_specs.append(row)
        args.append(base)
    if ln is None:
        out_specs = row
        out_shape = jax.ShapeDtypeStruct((t, d), out_dtype)
    else:
        in_specs += [row, vec]
        args += list(ln)
        out_specs = [row, row, vec, vec]
        out_shape = [jax.ShapeDtypeStruct((t, d), f32), jax.ShapeDtypeStruct((t, d), bf16),
                     jax.ShapeDtypeStruct((1, d), f32), jax.ShapeDtypeStruct((1, d), f32)]
    return _pallas(
        comm, body, name=name, grid=(t // tm, s_n),
        in_specs=in_specs, out_specs=out_specs, out_shape=out_shape,
        scratch_shapes=[pltpu.VMEM((tm, d), f32)],
        compiler_params=_params("parallel" if ln is None else "arbitrary", "arbitrary"),
    )(*args)


def _conv_bwd(dz, p, conv_w, name, comm=None):
    t, d3 = p.shape
    d = d3 // 3
    tm = _tile(t, 320, 8)
    hb = tm // 8
    last8 = t // 8 - 1
    n_ext = tm + 8

    def body(dz_ref, dzn_ref, p_ref, pp_ref, pn_ref, w_ref, dp_ref, dw_ref):
        i = pl.program_id(0)

        @pl.when(i == 0)
        def _():
            dw_ref[...] = jnp.zeros_like(dw_ref)

        w0, w1, w2 = w_ref[0:1, :], w_ref[1:2, :], w_ref[2:3, :]
        rows_u = i * tm - 8 + lax.broadcasted_iota(jnp.int32, (n_ext, 1), 0)
        cg = jnp.concatenate([pp_ref[:, d:2 * d], p_ref[:, d:2 * d]], axis=0)
        val = jnp.concatenate([pp_ref[:, 2 * d:], p_ref[:, 2 * d:]], axis=0)
        u = jnp.where(rows_u >= PAD, cg * val, 0.0)
        u1 = pltpu.roll(u, 1, 0)
        u2 = pltpu.roll(u, 2, 0)
        y = (w2 * u + w1 * u1 + w0 * u2)[8:]
        dzv = dz_ref[...]
        bg = p_ref[:, :d]
        rows_n = (i + 1) * tm + lax.broadcasted_iota(jnp.int32, (8, 1), 0)
        dy_main = dzv * bg
        dy_next = jnp.where(rows_n < t, dzn_ref[...] * pn_ref[:, :d], 0.0)
        dye = jnp.concatenate([dy_main, dy_next], axis=0)
        du = (w2 * dye + w1 * pltpu.roll(dye, n_ext - 1, 0)
              + w0 * pltpu.roll(dye, n_ext - 2, 0))[:tm]
        du = jnp.where(rows_u[8:] >= PAD, du, 0.0)
        dp_ref[:, :d] = (dzv * y).astype(bf16)
        dp_ref[:, d:2 * d] = (du * val[8:]).astype(bf16)
        dp_ref[:, 2 * d:] = (du * cg[8:]).astype(bf16)
        dw_ref[0:1, :] += jnp.sum(dy_main * u2[8:], axis=0, keepdims=True)
        dw_ref[1:2, :] += jnp.sum(dy_main * u1[8:], axis=0, keepdims=True)
        dw_ref[2:3, :] += jnp.sum(dy_main * u[8:], axis=0, keepdims=True)

    nxt = lambda i: (jnp.minimum((i + 1) * hb, last8), 0)
    return _pallas(
        comm, body, name=name, grid=(t // tm,),
        in_specs=[pl.BlockSpec((tm, d), lambda i: (i, 0)),
                  pl.BlockSpec((8, d), nxt),
                  pl.BlockSpec((tm, d3), lambda i: (i, 0)),
                  pl.BlockSpec((8, d3), lambda i: (jnp.maximum(i * hb - 1, 0), 0)),
                  pl.BlockSpec((8, d3), nxt),
                  pl.BlockSpec((3, d), lambda i: (0, 0))],
        out_specs=[pl.BlockSpec((tm, d3), lambda i: (i, 0)),
                   pl.BlockSpec((3, d), lambda i: (0, 0))],
        out_shape=[jax.ShapeDtypeStruct((t, d3), bf16), jax.ShapeDtypeStruct((3, d), f32)],
        compiler_params=_params("arbitrary"),
    )(dz, dz, p, p, p, conv_w)


def _attn_stats(o, do, lse_rep, cq_rep, bq, name):
    t, d = o.shape
    n_heads = d // HEAD_DIM

    def body(o_ref, do_ref, lse_ref, cq_ref, cl_ref, delta_ref):
        for c0 in range(0, bq, LANES):
            rows = slice(c0, c0 + LANES)
            cl_ref[:, rows] = (cq_ref[rows, :] - lse_ref[rows, :]).T[0:1, :]
            prod = o_ref[rows, :] * do_ref[rows, :].astype(f32)
            delta_ref[:, rows] = jnp.sum(prod.T, axis=0, keepdims=True)

    qblk = pl.BlockSpec((bq, HEAD_DIM), lambda h, i: (i, h))
    rep = pl.BlockSpec((None, bq, LANES), lambda h, i: (h, i, 0))
    row = pl.BlockSpec((None, None, 1, bq), lambda h, i: (h, i, 0, 0))
    shp = jax.ShapeDtypeStruct((n_heads, t // bq, 1, bq), f32)
    return pl.pallas_call(
        body, name=name, grid=(n_heads, t // bq),
        in_specs=[qblk, qblk, rep, rep],
        out_specs=[row, row], out_shape=[shp, shp],
        compiler_params=_params("parallel", "parallel"),
    )(o, do, lse_rep, cq_rep)


def _attn_bwd(q, k, v, kt, do, ckey, cl_rows, delta_rows, name, comm=None):
    t, d = q.shape
    n_heads = d // HEAD_DIM
    bk = kt.shape[-1]
    bq = bk
    n_kv = t // bk
    n_q = t // bq
    scale = 1.0 / math.sqrt(HEAD_DIM)

    def body(q_ref, do_ref, cl_ref, dl_ref, k_ref, v_ref, kt_ref, ck_ref,
             dq_ref, dcq_ref, dk_ref, dv_ref, dck_ref,
             st_scr, dp_scr, p_scr, ds_scr, dqt, dk_acc, dv_acc, dck_acc):
        j = pl.program_id(1)

        @pl.when(j == 0)
        def _():
            dqt[...] = jnp.zeros_like(dqt)
            dcq_ref[...] = jnp.zeros_like(dcq_ref)

        dk_acc[...] = jnp.zeros_like(dk_acc)
        dv_acc[...] = jnp.zeros_like(dv_acc)
        dck_acc[...] = jnp.zeros_like(dck_acc)
        kb = k_ref[...]
        vb = v_ref[...]
        behind = (lax.broadcasted_iota(jnp.int32, (STRIP, bq), 1)
                  - lax.broadcasted_iota(jnp.int32, (STRIP, bq), 0))

        def tile(i, diagonal):
            r0 = pl.multiple_of(i * bq, bq)
            qi = q_ref[pl.ds(r0, bq), :]
            doi = do_ref[pl.ds(r0, bq), :]
            st_scr[...] = lax.dot_general(kb, qi, NT_DIMS, preferred_element_type=f32)
            dp_scr[...] = lax.dot_general(vb, doi, NT_DIMS, preferred_element_type=f32)
            cl = cl_ref[i]
            dl = dl_ref[i]
            over_keys = jnp.zeros((STRIP, bq), f32)
            for r in range(0, bk, STRIP):
                keys = slice(r, r + STRIP)
                st = st_scr[keys, :] + cl - _lanes(ck_ref[keys, :], bq)
                if diagonal:
                    st = jnp.where(behind >= r, st, NEG_INF)
                pr = jnp.exp(st)
                ds = pr * (dp_scr[keys, :] - dl)
                over_keys = over_keys + ds
                dck_acc[keys, :] -= jnp.sum(ds, axis=1, keepdims=True)
                p_scr[keys, :] = pr.astype(bf16)
                ds_scr[keys, :] = ds.astype(bf16)
            dcq_ref[i] += jnp.sum(over_keys, axis=0, keepdims=True)
            dv_acc[...] += jnp.dot(p_scr[...], doi, preferred_element_type=f32)
            dk_acc[...] += jnp.dot(ds_scr[...], qi, preferred_element_type=f32)
            dqt[i] += jnp.dot(kt_ref[...], ds_scr[...], preferred_element_type=f32)

        def full_tile(i, carry):
            tile(i, False)
            return carry

        tile(j, True)
        lax.fori_loop(j + 1, n_q, full_tile, 0)
        dk_ref[...] = dk_acc[...].astype(bf16)
        dv_ref[...] = dv_acc[...].astype(bf16)
        for c0 in range(0, bk, LANES):
            keys = slice(c0, c0 + LANES)
            dck_ref[:, keys] = jnp.broadcast_to(dck_acc[keys, :], (LANES, LANES)).T[0:1, :]

        @pl.when(j == n_kv - 1)
        def _():
            def emit(i, carry):
                r0 = pl.multiple_of(i * bq, bq)
                dq_ref[pl.ds(r0, bq), :] = dqt[i].T * scale
                return carry
            lax.fori_loop(0, n_q, emit, 0)

    head_rows = pl.BlockSpec((t, HEAD_DIM), lambda h, j: (0, h))
    head_stat = pl.BlockSpec((None, n_q, 1, bq), lambda h, j: (h, 0, 0, 0))
    kblk = pl.BlockSpec((bk, HEAD_DIM), lambda h, j: (j, h))
    return _pallas(
        comm, body, name=name, grid=(n_heads, n_kv),
        in_specs=[head_rows, head_rows, head_stat, head_stat, kblk, kblk,
                  pl.BlockSpec((None, None, HEAD_DIM, bk), lambda h, j: (h, j, 0, 0)),
                  pl.BlockSpec((None, bk, LANES), lambda h, j: (h, j, 0))],
        out_specs=[head_rows, head_stat, kblk, kblk,
                   pl.BlockSpec((None, None, 1, bk), lambda h, j: (h, j, 0, 0))],
        out_shape=[jax.ShapeDtypeStruct((t, d), f32),
                   jax.ShapeDtypeStruct((n_heads, n_q, 1, bq), f32),
                   jax.ShapeDtypeStruct((t, d), bf16), jax.ShapeDtypeStruct((t, d), bf16),
                   jax.ShapeDtypeStruct((n_heads, n_kv, 1, bk), f32)],
        scratch_shapes=[pltpu.VMEM((bk, bq), f32), pltpu.VMEM((bk, bq), f32),
                        pltpu.VMEM((bk, bq), bf16), pltpu.VMEM((bk, bq), bf16),
                        pltpu.VMEM((n_q, HEAD_DIM, bq), f32),
                        pltpu.VMEM((bk, HEAD_DIM), f32), pltpu.VMEM((bk, HEAD_DIM), f32),
                        pltpu.VMEM((bk, 1), f32)],
        compiler_params=_params("parallel", "arbitrary"),
    )(q, do, cl_rows, delta_rows, k, v, kt, ckey)


def _fgate_bwd(dlogf, flog, fbias, name):
    t, n = flog.shape

    def body(dl_ref, fl_ref, fb_ref, o_ref, sum_ref):
        i = pl.program_id(0)

        @pl.when(i == 0)
        def _():
            sum_ref[...] = jnp.zeros_like(sum_ref)

        r = i * LANES + lax.broadcasted_iota(jnp.int32, (LANES, n), 0)
        g = dl_ref[...] * jax.nn.sigmoid(-(fl_ref[...] + fb_ref[...]))
        g = jnp.where(r >= PAD, g, 0.0)
        o_ref[...] = g
        sum_ref[...] += jnp.sum(g, axis=0, keepdims=True)

    blk = pl.BlockSpec((LANES, n), lambda i: (i, 0))
    vec = pl.BlockSpec((1, n), lambda i: (0, 0))
    return pl.pallas_call(
        body, name=name, grid=(t // LANES,),
        in_specs=[blk, blk, vec], out_specs=[blk, vec],
        out_shape=[jax.ShapeDtypeStruct((t, n), f32), jax.ShapeDtypeStruct((1, n), f32)],
        compiler_params=_params("arbitrary"),
    )(dlogf, flog, fbias)


def _place_shard(w, idx, name):
    n_l, r, c_n = w.shape
    out_dtype = bf16 if r * c_n > 2 ** 16 else w.dtype
    tr = _tile(r, 512, 16) if r % 16 == 0 else r

    def body(idx_ref, w_ref, o_ref):
        o_ref[...] = w_ref[...].astype(out_dtype)

    grid_spec = pltpu.PrefetchScalarGridSpec(
        num_scalar_prefetch=1, grid=(n_l, r // tr),
        in_specs=[pl.BlockSpec((None, tr, c_n), lambda l, i, idx: (l, i, 0))],
        out_specs=pl.BlockSpec((None, None, tr, c_n), lambda l, i, idx: (idx[0], l, i, 0)))
    return pl.pallas_call(
        body, name=name, grid_spec=grid_spec,
        out_shape=jax.ShapeDtypeStruct((N_CHIPS, n_l, r, c_n), out_dtype),
        compiler_params=_params("parallel", "parallel"),
    )(idx, w)


def _plan_gather_ici(items):
    def plan(srcs, bufs, news, p):
        out = []
        for name, layer, r2 in items:
            mine = bufs[name].at[p.me, layer, pl.ds(p.c * r2, r2)]
            out += [(mine, mine, (*chip, p.c)) for chip in p.chips]
        return out
    return plan, 3 * len(items)


def _plan_gather_d2d(items):
    def plan(srcs, bufs, news, p):
        out = []
        for name, layer, r2 in items:
            for px, py in p.chips:
                landed = bufs[name].at[2 * px + py, layer, pl.ds(p.c * r2, r2)]
                out.append((landed, landed, p.sib))
        return out
    return plan, 3 * len(items)


def _plan_pair_exchange(names, r2s):
    def plan(srcs, bufs, news, p):
        out = []
        for name, r2 in zip(names, r2s):
            for k, slot in enumerate(p.slots):
                out.append((srcs["G_" + name].at[slot, pl.ds((1 - p.c) * r2, r2)],
                            news["PAIR_" + name].at[k], p.sib))
        return out
    return plan, 4 * len(names)


def _plan_chip_exchange(names):
    def plan(srcs, bufs, news, p):
        out = []
        for name in names:
            for k, chip in enumerate(p.chips):
                out.append((srcs["SEND_" + name].at[k], news["RECV_" + name].at[k], (*chip, p.c)))
        return out
    return plan, 3 * len(names)


def _plan_pair_share(items):
    def plan(srcs, bufs, news, p):
        out = []
        for name, layer, r2 in items:
            mine = bufs["RED_" + name].at[layer, pl.ds(p.c * r2, r2)]
            out.append((mine, mine, p.sib))
        return out
    return plan, len(items)


def _rs_prepare(g, pair, idx, name):
    _, r2, c_n = pair.shape
    tr = _tile(r2, 256, 8)
    nb = r2 // tr

    def body(idx_ref, g_ref, p_ref, o_ref):
        o_ref[...] = (g_ref[...] + p_ref[...]).astype(bf16)

    grid_spec = pltpu.PrefetchScalarGridSpec(
        num_scalar_prefetch=1, grid=(3, nb),
        in_specs=[pl.BlockSpec((None, tr, c_n), lambda k, i, idx: (idx[k + 1], idx[4] * nb + i, 0)),
                  pl.BlockSpec((None, tr, c_n), lambda k, i, idx: (k + 1, i, 0))],
        out_specs=pl.BlockSpec((None, tr, c_n), lambda k, i, idx: (k, i, 0)))
    return pl.pallas_call(
        body, name=name, grid_spec=grid_spec,
        out_shape=jax.ShapeDtypeStruct((3, r2, c_n), bf16),
        compiler_params=_params("parallel", "parallel"),
    )(idx, g, pair)


def _rs_finish(g, pair, recv, idx, layer, n_layers, into, name):
    _, r2, c_n = pair.shape
    tr = _tile(r2, 256, 8)
    nb = r2 // tr

    def body(*refs):
        g_ref, p_ref, r0_ref, r1_ref, r2_ref = refs[1:6]
        o_ref = refs[-1]
        acc = g_ref[...] + p_ref[...]
        acc = acc + r0_ref[...].astype(f32)
        acc = acc + r1_ref[...].astype(f32)
        acc = acc + r2_ref[...].astype(f32)
        o_ref[...] = acc

    def rspec(k):
        return pl.BlockSpec((None, tr, c_n), functools.partial(lambda i, idx, kk: (kk, i, 0), kk=k))

    in_specs = [pl.BlockSpec((None, tr, c_n), lambda i, idx: (idx[0], idx[4] * nb + i, 0)),
                rspec(0), rspec(0), rspec(1), rspec(2)]
    args = [idx, g, pair, recv, recv, recv]
    aliases = {}
    if into is not None:
        in_specs.append(pl.BlockSpec(memory_space=pl.ANY))
        args.append(into)
        aliases = {6: 0}
    grid_spec = pltpu.PrefetchScalarGridSpec(
        num_scalar_prefetch=1, grid=(nb,), in_specs=in_specs,
        out_specs=pl.BlockSpec((None, tr, c_n), lambda i, idx: (layer, idx[4] * nb + i, 0)))
    return pl.pallas_call(
        body, name=name, grid_spec=grid_spec,
        out_shape=jax.ShapeDtypeStruct((n_layers, 2 * r2, c_n), f32),
        input_output_aliases=aliases,
        compiler_params=_params("parallel"),
    )(*args)


def _adamw(w, g, m, v, layer, into, name):
    n_l, r, c_n = w.shape
    tr = _tile(r, 256, 8)

    def body(*refs):
        w_ref, g_ref, m_ref, v_ref = refs[:4]
        d_ref, mo_ref, vo_ref = refs[-3:]
        gv = g_ref[...]
        mn = ADAM_B1 * m_ref[...] + (1.0 - ADAM_B1) * gv
        vn = ADAM_B2 * v_ref[...] + (1.0 - ADAM_B2) * (gv * gv)
        m_hat = mn / (1.0 - ADAM_B1 ** ADAM_STEP)
        v_hat = vn / (1.0 - ADAM_B2 ** ADAM_STEP)
        d_ref[...] = -ADAM_LR * (m_hat / (jnp.sqrt(v_hat) + ADAM_EPS) + ADAM_WD * w_ref[...])
        mo_ref[...] = mn
        vo_ref[...] = vn

    blk = pl.BlockSpec((None, tr, c_n), lambda i: (layer, i, 0))
    shp = jax.ShapeDtypeStruct((n_l, r, c_n), f32)
    in_specs = [blk] * 4
    args = [w, g, m, v]
    aliases = {}
    if into is not None:
        in_specs = in_specs + [pl.BlockSpec(memory_space=pl.ANY)] * 3
        args += list(into)
        aliases = {4: 0, 5: 1, 6: 2}
    return pl.pallas_call(
        body, name=name, grid=(r // tr,),
        in_specs=in_specs, out_specs=[blk] * 3, out_shape=[shp] * 3,
        input_output_aliases=aliases,
        compiler_params=_params("parallel"),
    )(*args)


def kernel(x, meta, ffn1_wg, ffn1_wu, ffn1_wd, ffn2_wg, ffn2_wu, ffn2_wd, ln_gain, ln_bias, conv_w_in, conv_w, conv_w_out, kv_w, f_bias, attn_w_q, attn_w_o, loss_target, m_meta, m_ffn1_wg, m_ffn1_wu, m_ffn1_wd, m_ffn2_wg, m_ffn2_wu, m_ffn2_wd, m_ln_gain, m_ln_bias, m_conv_w_in, m_conv_w, m_conv_w_out, m_kv_w, m_f_bias, m_attn_w_q, m_attn_w_o, v_meta, v_ffn1_wg, v_ffn1_wu, v_ffn1_wd, v_ffn2_wg, v_ffn2_wu, v_ffn2_wd, v_ln_gain, v_ln_bias, v_conv_w_in, v_conv_w, v_conv_w_out, v_kv_w, v_f_bias, v_attn_w_q, v_attn_w_o):
    seq, d = x.shape[1], x.shape[2]
    t = PAD + N_META + seq
    n_heads = d // HEAD_DIM
    dq = d // N_CHIPS
    n_kv = kv_w.shape[1]
    x2 = x[0]
    target = loss_target[0]

    def rows8(a):
        return jnp.pad(a, ((0, 8 - a.shape[0]), (0, 0)))

    def small_pack(mt, g, b, cw, fb):
        fb_row = jnp.pad(fb, (0, mt.shape[1] - n_heads))[None]
        return jnp.concatenate([mt, rows8(g.reshape(6, -1)), rows8(b.reshape(6, -1)),
                                rows8(cw.reshape(3, -1)), rows8(fb_row)], axis=0)

    w_small = small_pack(meta, ln_gain, ln_bias, conv_w, f_bias)

    cx, cy, c = lax.axis_index("x"), lax.axis_index("y"), lax.axis_index("c")
    idx = jnp.stack([2 * cx + cy, 2 * (1 - cx) + cy, 2 * cx + (1 - cy), 2 * (1 - cx) + (1 - cy), c]
                    ).astype(jnp.int32)

    def tr(a):
        return a.transpose(0, 2, 1)

    w3 = {"wg1": tr(ffn1_wg), "wu1": tr(ffn1_wu), "wd1": ffn1_wd, "wg2": tr(ffn2_wg),
          "wu2": tr(ffn2_wu), "wd2": ffn2_wd, "win": conv_w_in, "wout": conv_w_out, "kv": kv_w[None],
          "wq": attn_w_q, "wo": attn_w_o, "small": w_small[None]}
    transposed = ("wg1", "wu1", "wg2", "wu2")
    buf = {n: _place_shard(w, idx, "place_shard") for n, w in w3.items()}

    def split(item):
        name, layer = item.split(".")
        return name, int(layer)

    def gather_stage(planner, items):
        triples = [(n, l, buf[n].shape[2] // 2) for n, l in map(split, items)]
        plan, n_copies = planner(triples)
        return dict(plan=plan, n=n_copies, bufs={n: buf[n] for n, _, _ in triples})

    def ici(items):
        return gather_stage(_plan_gather_ici, items)

    def d2d(items):
        return gather_stage(_plan_gather_d2d, items)

    def pair_exchange(items):
        r2s = [buf["G_" + it].shape[1] // 2 for it in items]
        plan, n_copies = _plan_pair_exchange(items, r2s)
        news = {"PAIR_" + it: jax.ShapeDtypeStruct((N_CHIPS, r2, buf["G_" + it].shape[2]), f32)
                for it, r2 in zip(items, r2s)}
        return dict(plan=plan, n=n_copies, srcs={"G_" + it: buf["G_" + it] for it in items}, news=news)

    def chip_exchange(items):
        plan, n_copies = _plan_chip_exchange(items)
        srcs = {"SEND_" + it: buf["SEND_" + it] for it in items}
        news = {"RECV_" + it: jax.ShapeDtypeStruct(s.shape, s.dtype)
                for it, s in ((it, buf["SEND_" + it]) for it in items)}
        return dict(plan=plan, n=n_copies, srcs=srcs, news=news)

    def pair_share(items):
        triples = [(n, l, buf["RED_" + n].shape[1] // 2) for n, l in map(split, items)]
        plan, n_copies = _plan_pair_share(triples)
        return dict(plan=plan, n=n_copies, bufs={"RED_" + n: buf["RED_" + n] for n, _, _ in triples})

    def run(fn, *args, stages=(), name=None, **kw):
        comm = _Copies()
        for st in (stages() if callable(stages) else stages):
            comm.add(st["plan"], st["n"], srcs=st.get("srcs"), bufs=st.get("bufs"), news=st.get("news"))
        out = _copy_call(comm, name) if fn is None else fn(*args, comm=comm, **kw)
        buf.update(comm.out_bufs)
        buf.update(comm.out_news)
        return out

    first = ["wg1.0", "wu1.0", "small.0"]
    run(None, stages=[ici(first)], name="gather_first_ici")
    run(None, stages=[d2d(first)], name="gather_first_d2d")
    small = buf["small"].reshape(N_CHIPS, SMALL_ROWS, dq).transpose(1, 0, 2).reshape(SMALL_ROWS, d)
    meta_full = small[:N_META]
    gains = small[16:22].reshape(DEPTH, 3, 1, d)
    biases = small[24:30].reshape(DEPTH, 3, 1, d)
    conv_w_full = small[32:35]
    fb_pad = jnp.pad(f_bias, (0, LANES - n_heads))[None]
    conv_w8 = ["wd1.0", "win.0", "wout.0"]
    ffn2_l0 = ["wg2.0", "wu2.0", "wd2.0", "kv.0"]
    attn_ffn2_l1 = ["wq.0", "wo.0", "wg2.1", "wu2.1", "wd2.1"]
    ffn1_l1 = ["wg1.1", "wu1.1", "wd1.1"]

    meta_pad = jnp.concatenate([jnp.zeros((PAD, d), f32), meta_full], axis=0)
    h0, h0b = run(_embed, meta_pad, x2, "embed", stages=[ici(conv_w8)])
    a1, b1, s1 = run(_ffn_up, h0b, buf["wg1"], buf["wu1"], 0, "ffn_up",
                     stages=lambda: [d2d(conv_w8), ici(ffn2_l0)])
    r1, h1, h1b = run(_down_ln, s1, buf["wd1"], 0, h0, gains[0, 0], biases[0, 0], 0.5, "ffn_down_ln",
                      stages=lambda: [d2d(ffn2_l0), ici(attn_ffn2_l1)])
    n_in = conv_w_in.shape[-1]
    w_in = buf["win"].reshape(N_CHIPS, d, n_in)
    w_out = buf["wout"].reshape(1, 1, d, d)
    p = run(_nn_matmul, h1b, w_in, f32, "conv_in", stages=lambda: [d2d(attn_ffn2_l1), ici(ffn1_l1)])
    z = _conv_fwd(p, conv_w_full, "conv_fwd")
    r2, h2, h2b = run(_down_ln, z[None], w_out, 0, h1, gains[0, 1], biases[0, 1], 1.0, "mix_out_ln",
                      stages=lambda: [d2d(ffn1_l1)])
    wg1, wu1, wd1, wg2, wu2, wd2 = (buf[n] for n in ("wg1", "wu1", "wd1", "wg2", "wu2", "wd2"))
    w_q = buf["wq"].reshape(1, d, d)
    w_o = buf["wo"].reshape(1, 1, d, d)
    kv_full = buf["kv"].reshape(N_CHIPS, d, n_kv).transpose(1, 0, 2).reshape(d, N_CHIPS * n_kv)
    w_k = kv_full[:, :d][None]
    w_v = kv_full[:, d:2 * d][None]
    w_f = jnp.pad(kv_full[:, 2 * d:], ((0, 0), (0, LANES - n_heads)))[None]
    a2, b2, s2 = _ffn_up(h2b, wg2, wu2, 0, "ffn_up")
    r3, h3, h3b = _down_ln(s2, wd2, 0, h2, gains[0, 2], biases[0, 2], 0.5, "ffn_down_ln")
    kk = _nn_matmul(h3b, w_k, bf16, "proj_bf16")
    vv = _nn_matmul(h3b, w_v, bf16, "proj_bf16")
    flog = _nn_matmul(h3b, w_f, f32, "proj_gate")
    cum = _row_scan(flog, "gate_cumsum", fb_pad)
    bk = _tile(t, 640, LANES)
    c_ht = cum[:, :n_heads].T
    c_keys = jnp.where(jnp.arange(t)[None, :] < PAD, 1e30, c_ht)
    cq_rep = jnp.broadcast_to(c_ht[:, :, None], (n_heads, t, LANES))
    ck_rep = jnp.broadcast_to(c_keys[:, :, None], (n_heads, t, LANES))
    ck_rows = c_keys.reshape(n_heads, t // bk, 1, bk)
    a3, b3, s3 = _ffn_up(h3b, wg1, wu1, 1, "ffn_up")
    r4, h4, h4b = _down_ln(s3, wd1, 1, h3, gains[1, 0], biases[1, 0], 0.5, "ffn_down_ln")
    q = _nn_matmul(h4b, w_q, bf16, "proj_q", out_scale=1.0 / math.sqrt(HEAD_DIM))
    o, o32, lse_rep = _attn_fwd(q, kk, vv, cq_rep, ck_rows, "attn_fwd")
    r5, h5, h5b = _down_ln(o[None], w_o, 0, h4, gains[1, 1], biases[1, 1], 1.0, "mix_out_ln")
    a4, b4, s4 = _ffn_up(h5b, wg2, wu2, 1, "ffn_up")
    r6, h6, _ = _down_ln(s4, wd2, 1, h5, gains[1, 2], biases[1, 2], 0.5, "ffn_down_ln")
    dr6, dr6b, dg12, db12, sq = _loss_head(h6, target, r6, gains[1, 2], "loss_head")
    loss_part = 0.5 * sq[0, 0] / d

    m_small = small_pack(m_meta, m_ln_gain, m_ln_bias, m_conv_w, m_f_bias)
    v_small = small_pack(v_meta, v_ln_gain, v_ln_bias, v_conv_w, v_f_bias)
    m3 = {"wg1": tr(m_ffn1_wg), "wu1": tr(m_ffn1_wu), "wd1": m_ffn1_wd, "wg2": tr(m_ffn2_wg),
          "wu2": tr(m_ffn2_wu), "wd2": m_ffn2_wd, "win": m_conv_w_in, "wout": m_conv_w_out,
          "kv": m_kv_w[None], "wq": m_attn_w_q, "wo": m_attn_w_o, "small": m_small[None]}
    v3 = {"wg1": tr(v_ffn1_wg), "wu1": tr(v_ffn1_wu), "wd1": v_ffn1_wd, "wg2": tr(v_ffn2_wg),
          "wu2": tr(v_ffn2_wu), "wd2": v_ffn2_wd, "win": v_conv_w_in, "wout": v_conv_w_out,
          "kv": v_kv_w[None], "wq": v_attn_w_q, "wo": v_attn_w_o, "small": v_small[None]}
    stepped = {}

    def prepare(items):
        for it in items:
            buf["SEND_" + it] = _rs_prepare(buf["G_" + it], buf["PAIR_" + it], idx, "grad_prepare")

    def finish(items):
        for it in items:
            n, l = split(it)
            buf["RED_" + n] = _rs_finish(buf["G_" + it], buf["PAIR_" + it], buf["RECV_" + it], idx, l,
                                         w3[n].shape[0], buf.get("RED_" + n), "grad_finish")

    def adam(items, grads=None):
        for it in items:
            n, l = split(it)
            g = buf["RED_" + n] if grads is None else grads[n]
            stepped[n] = _adamw(w3[n], g, m3[n], v3[n], l, stepped.get(n), "adamw")

    def ffn_bwd(dr, drb, hb_in, a, b, s, f, layer, ln, on_dx=(), after_dx=None, on_dwd=(),
                after_dwd=None):
        da, db, *rest = run(_ffn_bwd_dx, drb, dr, buf["wd" + f], buf["wg" + f], buf["wu" + f], layer,
                            a, b, ln, "ffn_bwd_dx", stages=on_dx)
        if after_dx is not None:
            after_dx()
        (buf[f"G_wd{f}.{layer}"],) = run(_tn_matmul, [(s, "stack", drb, "shared")], N_CHIPS, 0.5,
                                         "ffn_dwd", stages=on_dwd)
        if after_dwd is not None:
            after_dwd()
        buf[f"G_wg{f}.{layer}"], buf[f"G_wu{f}.{layer}"] = _tn_matmul(
            [(da, "stack", hb_in, "shared"), (db, "stack", hb_in, "shared")], N_CHIPS, 1.0, "ffn_dwgu")
        return rest

    ffn2_1 = ["wg2.1", "wu2.1", "wd2.1"]
    ffn1_1 = ["wg1.1", "wu1.1", "wd1.1"]
    ffn2_0 = ["wg2.0", "wu2.0", "wd2.0"]
    conv_items = ["wout.0", "win.0"]

    dr5, dr5b, dg11, db11 = ffn_bwd(dr6, dr6b, h5b, a4, b4, s4, "2", 1, (r5, gains[1, 1]))
    (dwo,) = run(_tn_matmul, [(o, "shared", dr5b, "shared")], 1, 1.0, "sq_dw",
                 stages=lambda: [pair_exchange(ffn2_1)])
    prepare(ffn2_1)
    buf["G_wo.0"] = dwo.reshape(N_CHIPS, dq, d)
    do = run(_nt_sum, [(dr5b, "cols", w_o[0], None)], None, 1.0, 1, bf16, "sq_dx_bf16",
             stages=lambda: [pair_exchange(["wo.0"])])
    prepare(["wo.0"])
    cl, delta = _attn_stats(o32, do, lse_rep, cq_rep, bk, "attn_stats")
    kt = kk.reshape(t // bk, bk, n_heads, HEAD_DIM).transpose(2, 0, 3, 1)
    dq_att, dc_q, dk, dv, dc_k = run(_attn_bwd, q, kk, vv, kt, do, ck_rep, cl, delta, "attn_bwd",
                                     stages=lambda: [chip_exchange(ffn2_1 + ["wo.0"])])
    finish(ffn2_1 + ["wo.0"])
    dc = (dc_q + dc_k).reshape(n_heads, t)
    (dwq,) = run(_tn_matmul, [(h4b, "shared", dq_att, "shared")], 1, 1.0, "sq_dw",
                 stages=lambda: [pair_share(ffn2_1 + ["wo.0"])])
    buf["G_wq.0"] = dwq.reshape(N_CHIPS, dq, d)
    adam(ffn2_1 + ["wo.0"])
    dr4, dr4b, dg10, db10 = run(_nt_sum, [(dq_att, "cols", w_q, None)], dr5, ALPHA, 1, f32,
                                "sq_dx_res", stages=lambda: [pair_exchange(["wq.0"])],
                                ln=(r4, gains[1, 0]))
    prepare(["wq.0"])
    (dh3a,) = ffn_bwd(dr4, dr4b, h3b, a3, b3, s3, "1", 1, None,
                      on_dx=lambda: [chip_exchange(["wq.0"])], after_dx=lambda: finish(["wq.0"]))
    dc_t = jnp.pad(dc.T, ((0, 0), (0, LANES - n_heads)))
    dlogf = _row_scan(dc_t, "rev_cumsum", reverse=True)
    dfl, dfb_cols = _fgate_bwd(dlogf, flog, fb_pad, "gate_bwd")
    dwk, dwv, dwf = run(_tn_matmul, [(h3b, "shared", g, "shared") for g in (dk, dv, dfl)], 1, 1.0,
                        "kv_dw",
                        stages=lambda: [pair_exchange(ffn1_1), pair_share(["wq.0"])])
    prepare(ffn1_1)
    adam(["wq.0"])

    def by_chip(full):
        rows = full.shape[0]
        return full.reshape(rows, N_CHIPS, full.shape[1] // N_CHIPS).transpose(1, 0, 2)

    buf["G_kv.0"] = by_chip(jnp.concatenate([dwk[0], dwv[0], dwf[0][:, :n_heads]], axis=1))
    dr3, dr3b, dg02, db02 = run(
        _nt_sum, [(dk, "cols", w_k, None), (dv, "cols", w_v, None), (dfl, "cols", w_f, None)],
        dh3a, 1.0, 1, f32, "kv_dx",
        stages=lambda: [pair_exchange(["kv.0"]), chip_exchange(ffn1_1)], ln=(r3, gains[0, 2]))
    prepare(["kv.0"])
    finish(ffn1_1)
    dr2, dr2b, dg01, db01 = ffn_bwd(dr3, dr3b, h2b, a2, b2, s2, "2", 0, (r2, gains[0, 1]),
                                    on_dx=lambda: [chip_exchange(["kv.0"]), pair_share(ffn1_1)],
                                    after_dx=lambda: (adam(ffn1_1), finish(["kv.0"])),
                                    on_dwd=lambda: [pair_share(["kv.0"])],
                                    after_dwd=lambda: adam(["kv.0"]))
    (dwout,) = run(_tn_matmul, [(z, "shared", dr2b, "shared")], 1, 1.0, "sq_dw",
                   stages=lambda: [pair_exchange(ffn2_0)])
    prepare(ffn2_0)
    buf["G_wout.0"] = dwout.reshape(N_CHIPS, dq, d)
    dz = _nt_sum([(dr2b, "cols", w_out[0], None)], None, 1.0, 1, f32, "sq_dx_f32")
    dp, dconv_w = run(_conv_bwd, dz, p, conv_w_full, "conv_bwd",
                      stages=lambda: [chip_exchange(ffn2_0)])
    (buf["G_win.0"],) = _tn_matmul([(h1b, "shared", dp, "cols")], N_CHIPS, 1.0, "conv_dwin")
    finish(ffn2_0)
    dr1, dr1b, dg00, db00 = run(_nt_sum, [(dp, "cols", w_in, None)], dr2, ALPHA, N_CHIPS, f32,
                                "conv_dx",
                                stages=lambda: [pair_exchange(conv_items), pair_share(ffn2_0)],
                                ln=(r1, gains[0, 0]))
    prepare(conv_items)
    adam(ffn2_0)
    gate_up = ["wg1.0", "wu1.0"]
    da, db = run(_ffn_bwd_act, dr1b, buf["wd1"], 0, a1, b1, "ffn_bwd_act",
                 stages=lambda: [chip_exchange(conv_items)])
    finish(conv_items)
    buf["G_wg1.0"], buf["G_wu1.0"] = run(
        _tn_matmul, [(da, "stack", h0b, "shared"), (db, "stack", h0b, "shared")], N_CHIPS, 1.0,
        "ffn_dwgu", stages=lambda: [pair_share(conv_items)])
    adam(conv_items)
    (buf["G_wd1.0"],) = run(_tn_matmul, [(s1, "stack", dr1b, "shared")], N_CHIPS, 0.5, "ffn_dwd",
                            stages=lambda: [pair_exchange(gate_up)])
    prepare(gate_up)
    dh0 = run(_nt_sum, [(da, "stack", buf["wg1"], 0, True), (db, "stack", buf["wu1"], 0, True)],
              dr1, ALPHA, N_CHIPS, f32, "ffn_dx",
              stages=lambda: [chip_exchange(gate_up), pair_exchange(["wd1.0"])])
    prepare(["wd1.0"])
    finish(gate_up)
    grad_x = dh0[PAD + N_META:][None]
    dmeta = dh0[PAD:PAD + N_META]
    buf["G_small.0"] = by_chip(jnp.concatenate(
        [dmeta, rows8(jnp.concatenate([dg00, dg01, dg02, dg10, dg11, dg12], axis=0)),
         rows8(jnp.concatenate([db00, db01, db02, db10, db11, db12], axis=0)),
         rows8(dconv_w), jnp.zeros((8, d), f32)], axis=0))
    run(None, stages=lambda: [chip_exchange(["wd1.0"]), pair_exchange(["small.0"]), pair_share(gate_up)],
        name="grad_tail_1")
    prepare(["small.0"])
    finish(["wd1.0"])
    adam(gate_up)
    run(None, stages=lambda: [chip_exchange(["small.0"]), pair_share(["wd1.0"])], name="grad_tail_2")
    finish(["small.0"])
    adam(["wd1.0"])
    run(None, stages=lambda: [pair_share(["small.0"])], name="grad_tail_3")

    tail = jnp.zeros((LANES,), f32).at[:n_heads].set(dfb_cols[0, :n_heads]).at[n_heads].set(loss_part)
    tail = lax.psum(tail, ("x", "y", "c"))
    loss = tail[n_heads]
    g_fb = tail[:n_heads]
    g_small = jnp.concatenate([buf["RED_small"][0, :40],
                               rows8(jnp.pad(g_fb, (0, dq - n_heads))[None])], axis=0)
    adam(["small.0"], grads={"small": g_small[None]})

    def unpack(pk):
        return (pk[:16], pk[16:22].reshape(DEPTH, 3, dq), pk[24:30].reshape(DEPTH, 3, dq),
                pk[32:35].reshape(1, 3, dq), pk[40, :n_heads])

    def order(pick):
        mt, g, b, cw, fb = unpack(pick("small")[0])
        big = {n: pick(n) for n in w3 if n != "small"}
        big["kv"] = big["kv"][0]
        for n in transposed:
            big[n] = tr(big[n])
        return [mt, big["wg1"], big["wu1"], big["wd1"], big["wg2"], big["wu2"], big["wd2"], g, b,
                big["win"], cw, big["wout"], big["kv"], fb, big["wq"], big["wo"]]

    grads_out = dict({n: buf["RED_" + n] for n in w3}, small=g_small[None])
    return (loss, grad_x, *order(lambda n: grads_out[n]), *order(lambda n: stepped[n][0]),
            *order(lambda n: stepped[n][1]), *order(lambda n: stepped[n][2]))
```

```python
import functools
import math

import jax
import jax.numpy as jnp
from jax import lax
from jax.experimental import pallas as pl
from jax.experimental.pallas import tpu as pltpu

f32 = jnp.float32
bf16 = jnp.bfloat16

N_META = 16
PAD = 112
HEAD_DIM = 128
DEPTH = 2
LN_EPS = 1e-5
ALPHA = (2 * DEPTH) ** 0.25
NEG_INF = -1e30
N_CHIPS = 4
SMALL_ROWS = 48
LANES = 128

ADAM_LR = 0.001
ADAM_B1 = 0.9
ADAM_B2 = 0.999
ADAM_EPS = 1e-08
ADAM_WD = 0.01
ADAM_STEP = 10

VMEM_LIMIT_BYTES = 56 * 1024 * 1024
ROWS_WIDE = 1664
ROWS_ACC = 1040
ROWS_ACC_LN = 832
STRIP = 16
MESH = pl.DeviceIdType.MESH

NT_DIMS = (((1,), (1,)), ((), ()))
TN_DIMS = (((0,), (0,)), ((), ()))


def _tile(n, target, mult):
    best = None
    for d in range(mult, min(n, target) + 1, mult):
        if n % d == 0:
            best = d
    assert best is not None, (n, target, mult)
    return best


def _params(*sem):
    return pltpu.CompilerParams(dimension_semantics=sem, vmem_limit_bytes=VMEM_LIMIT_BYTES)


class _Place:
    def __init__(self):
        self.cx, self.cy, self.c = lax.axis_index("x"), lax.axis_index("y"), lax.axis_index("c")
        self.chips = [(1 - self.cx, self.cy), (self.cx, 1 - self.cy), (1 - self.cx, 1 - self.cy)]
        self.me = 2 * self.cx + self.cy
        self.slots = [self.me] + [2 * px + py for px, py in self.chips]
        self.sib = (self.cx, self.cy, 1 - self.c)


class _Copies:
    def __init__(self):
        self.srcs, self.bufs, self.news = {}, {}, {}
        self.plans = []
        self.out_bufs, self.out_news = {}, {}

    def add(self, plan, n_copies, srcs=None, bufs=None, news=None):
        for have, more in ((self.srcs, srcs), (self.bufs, bufs), (self.news, news)):
            for key, val in (more or {}).items():
                assert key not in have or have[key] is val, key
                have[key] = val
        self.plans.append((plan, n_copies))

    def empty(self):
        return not self.plans

    def count(self):
        return sum(n for _, n in self.plans)

    def copies(self, src_refs, buf_refs, new_refs, send, recv):
        place = _Place()
        srcs = dict(zip(self.srcs, src_refs))
        bufs = dict(zip(self.bufs, buf_refs))
        news = dict(zip(self.news, new_refs))
        out = []
        for plan, n_copies in self.plans:
            triples = plan(srcs, bufs, news, place)
            assert len(triples) == n_copies
            for src, dst, dev in triples:
                n = len(out)
                out.append(pltpu.make_async_remote_copy(
                    src_ref=src, dst_ref=dst, send_sem=send.at[n], recv_sem=recv.at[n],
                    device_id=dev, device_id_type=MESH))
        return out

    def land(self, results):
        n_b = len(self.bufs)
        self.out_bufs = dict(zip(self.bufs, results[:n_b]))
        self.out_news = dict(zip(self.news, results[n_b:]))


def _pallas(comm, body, *, name, grid, in_specs, out_specs, out_shape, compiler_params,
            scratch_shapes=(), input_output_aliases=None):
    aliases = dict(input_output_aliases or {})
    if comm is None or comm.empty():
        return pl.pallas_call(body, name=name, grid=grid, in_specs=in_specs, out_specs=out_specs,
                              out_shape=out_shape, scratch_shapes=list(scratch_shapes),
                              input_output_aliases=aliases, compiler_params=compiler_params)
    single = not isinstance(out_shape, (list, tuple))
    out_shapes = [out_shape] if single else list(out_shape)
    out_specs_l = [out_specs] if single else list(out_specs)
    n_in, n_out, n_scr = len(in_specs), len(out_shapes), len(scratch_shapes)
    n_s, n_b, n_n = len(comm.srcs), len(comm.bufs), len(comm.news)
    n_copies = comm.count()

    def wrapped(*refs):
        ins = refs[:n_in]
        src_refs = refs[n_in:n_in + n_s]
        o0 = n_in + n_s + n_b
        outs = refs[o0:o0 + n_out]
        buf_refs = refs[o0 + n_out:o0 + n_out + n_b]
        new_refs = refs[o0 + n_out + n_b:o0 + n_out + n_b + n_n]
        rest = refs[o0 + n_out + n_b + n_n:]
        scratch, (send, recv) = rest[:n_scr], rest[n_scr:]
        ids = [pl.program_id(a) for a in range(len(grid))]
        first = functools.reduce(jnp.logical_and, [i == 0 for i in ids])
        last = functools.reduce(jnp.logical_and, [i == g - 1 for i, g in zip(ids, grid)])

        @pl.when(first)
        def _():
            for cp in comm.copies(src_refs, buf_refs, new_refs, send, recv):
                cp.start()

        body(*ins, *outs, *scratch)

        @pl.when(last)
        def _():
            for cp in comm.copies(src_refs, buf_refs, new_refs, send, recv):
                cp.wait()

    hbm = pl.BlockSpec(memory_space=pl.ANY)
    for j in range(n_b):
        aliases[n_in + n_s + j] = n_out + j
    call = pl.pallas_call(
        wrapped, name=name, grid=grid,
        in_specs=[*in_specs, *([hbm] * (n_s + n_b))],
        out_specs=[*out_specs_l, *([hbm] * (n_b + n_n))],
        out_shape=[*out_shapes,
                   *[jax.ShapeDtypeStruct(a.shape, a.dtype) for a in comm.bufs.values()],
                   *comm.news.values()],
        scratch_shapes=[*scratch_shapes, pltpu.SemaphoreType.DMA((n_copies,)),
                        pltpu.SemaphoreType.DMA((n_copies,))],
        input_output_aliases=aliases, compiler_params=compiler_params)

    def run(*args):
        res = call(*args, *comm.srcs.values(), *comm.bufs.values())
        comm.land(res[n_out:])
        return res[0] if single else res[:n_out]

    return run


def _copy_call(comm, name):
    n_s, n_b, n_n = len(comm.srcs), len(comm.bufs), len(comm.news)
    n_copies = comm.count()

    def body(*refs):
        src_refs = refs[:n_s]
        buf_refs = refs[n_s + n_b:n_s + 2 * n_b]
        new_refs = refs[n_s + 2 * n_b:n_s + 2 * n_b + n_n]
        send, recv = refs[n_s + 2 * n_b + n_n:]
        copies = comm.copies(src_refs, buf_refs, new_refs, send, recv)
        for cp in copies:
            cp.start()
        for cp in copies:
            cp.wait()

    hbm = pl.BlockSpec(memory_space=pl.ANY)
    res = pl.pallas_call(
        body, name=name,
        in_specs=[hbm] * (n_s + n_b), out_specs=[hbm] * (n_b + n_n),
        out_shape=[*[jax.ShapeDtypeStruct(a.shape, a.dtype) for a in comm.bufs.values()],
                   *comm.news.values()],
        input_output_aliases={n_s + j: j for j in range(n_b)},
        scratch_shapes=[pltpu.SemaphoreType.DMA((n_copies,)), pltpu.SemaphoreType.DMA((n_copies,))],
    )(*comm.srcs.values(), *comm.bufs.values())
    comm.land(res)


def _embed(meta_pad, x, name, comm=None):
    seq, d = x.shape
    t = seq + LANES

    def body(m_ref, x_ref, h_ref, hb_ref):
        first = pl.program_id(0) == 0
        v = jnp.where(first, m_ref[...], x_ref[...])
        h_ref[...] = v
        hb_ref[...] = v.astype(bf16)

    return _pallas(
        comm, body, name=name, grid=(t // LANES,),
        in_specs=[pl.BlockSpec((LANES, d), lambda i: (0, 0)),
                  pl.BlockSpec((LANES, d), lambda i: (jnp.maximum(i - 1, 0), 0))],
        out_specs=[pl.BlockSpec((LANES, d), lambda i: (i, 0)),
                   pl.BlockSpec((LANES, d), lambda i: (i, 0))],
        out_shape=[jax.ShapeDtypeStruct((t, d), f32), jax.ShapeDtypeStruct((t, d), bf16)],
        compiler_params=_params("parallel"),
    )(meta_pad, x)


def _nn_matmul(x, w, out_dtype, name, comm=None, out_scale=None):
    t, k = x.shape
    s_n, _, n = w.shape
    assert s_n == 1 or n % LANES == 0
    tm = _tile(t, ROWS_WIDE, 16)

    def body(x_ref, w_ref, o_ref):
        res = jnp.dot(x_ref[...].astype(bf16), w_ref[...], preferred_element_type=f32)
        if out_scale is not None:
            res = res * out_scale
        o_ref[...] = res.astype(o_ref.dtype)

    return _pallas(
        comm, body, name=name, grid=(s_n, t // tm),
        in_specs=[pl.BlockSpec((tm, k), lambda s, i: (i, 0)),
                  pl.BlockSpec((None, k, n), lambda s, i: (s, 0, 0))],
        out_specs=pl.BlockSpec((tm, n), lambda s, i: (i, s)),
        out_shape=jax.ShapeDtypeStruct((t, s_n * n), out_dtype),
        compiler_params=_params("parallel", "parallel"),
    )(x, w)


def _ffn_up(hb, wg, wu, layer, name, comm=None):
    t, d = hb.shape
    s_n, _, n, _ = wg.shape
    tm = _tile(t, ROWS_WIDE, 16)

    def body(x_ref, wg_ref, wu_ref, a_ref, b_ref, s_ref):
        x = x_ref[...]
        a = lax.dot_general(x, wg_ref[...], NT_DIMS, preferred_element_type=f32)
        b = lax.dot_general(x, wu_ref[...], NT_DIMS, preferred_element_type=f32)
        a_ref[...] = a.astype(bf16)
        b_ref[...] = b.astype(bf16)
        s_ref[...] = (a * jax.nn.sigmoid(a) * b).astype(bf16)

    wspec = pl.BlockSpec((None, None, n, d), lambda s, i: (s, layer, 0, 0))
    ospec = pl.BlockSpec((None, tm, n), lambda s, i: (s, i, 0))
    return _pallas(
        comm, body, name=name, grid=(s_n, t // tm),
        in_specs=[pl.BlockSpec((tm, d), lambda s, i: (i, 0)), wspec, wspec],
        out_specs=[ospec, ospec, ospec],
        out_shape=[jax.ShapeDtypeStruct((s_n, t, n), bf16)] * 3,
        compiler_params=_params("parallel", "parallel"),
    )(hb, wg, wu)


def _down_ln(x, w, layer, hprev, gain, bias, beta, name, comm=None):
    s_n, t, k = x.shape
    d = w.shape[-1]
    tm = _tile(t, ROWS_ACC, 16)

    def body(x_ref, w_ref, h_ref, g_ref, b_ref, r_out, h_out, hb_out, acc):
        s = pl.program_id(1)

        @pl.when(s == 0)
        def _():
            acc[...] = jnp.zeros_like(acc)

        acc[...] += jnp.dot(x_ref[...], w_ref[...], preferred_element_type=f32)

        @pl.when(s == s_n - 1)
        def _():
            r = ALPHA * h_ref[...] + beta * acc[...]
            mu = jnp.mean(r, axis=-1, keepdims=True)
            xc = r - mu
            var = jnp.mean(xc * xc, axis=-1, keepdims=True)
            y = xc * lax.rsqrt(var + LN_EPS) * g_ref[...] + b_ref[...]
            r_out[...] = r
            h_out[...] = y
            hb_out[...] = y.astype(bf16)

    row = pl.BlockSpec((tm, d), lambda i, s: (i, 0))
    vec = pl.BlockSpec((1, d), lambda i, s: (0, 0))
    return _pallas(
        comm, body, name=name, grid=(t // tm, s_n),
        in_specs=[pl.BlockSpec((None, tm, k), lambda i, s: (s, i, 0)),
                  pl.BlockSpec((None, None, k, d), lambda i, s: (s, layer, 0, 0)),
                  row, vec, vec],
        out_specs=[row, row, row],
        out_shape=[jax.ShapeDtypeStruct((t, d), f32), jax.ShapeDtypeStruct((t, d), f32),
                   jax.ShapeDtypeStruct((t, d), bf16)],
        scratch_shapes=[pltpu.VMEM((tm, d), f32)],
        compiler_params=_params("parallel", "arbitrary"),
    )(x, w, hprev, gain, bias)


def _conv_fwd(p, conv_w, name):
    t, d3 = p.shape
    d = d3 // 3
    tm = _tile(t, 320, 8)
    hb = tm // 8

    def body(p_ref, prev_ref, w_ref, z_ref):
        i = pl.program_id(0)
        rows = i * tm - 8 + lax.broadcasted_iota(jnp.int32, (tm + 8, 1), 0)
        cg = jnp.concatenate([prev_ref[:, d:2 * d], p_ref[:, d:2 * d]], axis=0)
        val = jnp.concatenate([prev_ref[:, 2 * d:], p_ref[:, 2 * d:]], axis=0)
        u = jnp.where(rows >= PAD, cg * val, 0.0)
        y = (w_ref[2:3, :] * u + w_ref[1:2, :] * pltpu.roll(u, 1, 0)
             + w_ref[0:1, :] * pltpu.roll(u, 2, 0))
        z_ref[...] = (p_ref[:, :d] * y[8:]).astype(bf16)

    return pl.pallas_call(
        body, name=name, grid=(t // tm,),
        in_specs=[pl.BlockSpec((tm, d3), lambda i: (i, 0)),
                  pl.BlockSpec((8, d3), lambda i: (jnp.maximum(i * hb - 1, 0), 0)),
                  pl.BlockSpec((3, d), lambda i: (0, 0))],
        out_specs=pl.BlockSpec((tm, d), lambda i: (i, 0)),
        out_shape=jax.ShapeDtypeStruct((t, d), bf16),
        compiler_params=_params("parallel"),
    )(p, p, conv_w)


def _row_scan(x, name, fbias=None, reverse=False):
    t, n = x.shape
    blk = LANES
    n_blk = t // blk
    gate = fbias is not None

    def body(*refs):
        if gate:
            x_ref, fb_ref, o_ref, carry = refs
        else:
            x_ref, o_ref, carry = refs
        i = pl.program_id(0)

        @pl.when(i == 0)
        def _():
            carry[...] = jnp.zeros_like(carry)

        v = x_ref[...]
        r = lax.broadcasted_iota(jnp.int32, (blk, n), 0)
        if gate:
            v = v + fb_ref[...]
            v = jnp.minimum(v, 0.0) - jnp.log1p(jnp.exp(-jnp.abs(v)))
            v = jnp.where(i * blk + r >= PAD, v, 0.0)
        sh = 1
        while sh < blk:
            if reverse:
                v = v + jnp.where(r < blk - sh, pltpu.roll(v, blk - sh, 0), 0.0)
            else:
                v = v + jnp.where(r >= sh, pltpu.roll(v, sh, 0), 0.0)
            sh *= 2
        v = v + carry[...]
        o_ref[...] = v
        carry[...] = o_ref[0:1, :] if reverse else o_ref[blk - 1:blk, :]

    order = (lambda i: (n_blk - 1 - i, 0)) if reverse else (lambda i: (i, 0))
    in_specs = [pl.BlockSpec((blk, n), order)]
    args = [x]
    if gate:
        in_specs.append(pl.BlockSpec((1, n), lambda i: (0, 0)))
        args.append(fbias)
    return pl.pallas_call(
        body, name=name, grid=(n_blk,),
        in_specs=in_specs,
        out_specs=pl.BlockSpec((blk, n), order),
        out_shape=jax.ShapeDtypeStruct((t, n), f32),
        scratch_shapes=[pltpu.VMEM((1, n), f32)],
        compiler_params=_params("arbitrary"),
    )(*args)


def _lanes(x, n):
    return jnp.concatenate([x] * (n // LANES), axis=1)


def _attn_fwd(q, k, v, cq_rep, ck_rows, name):
    t, d = q.shape
    n_heads = d // HEAD_DIM
    bk = ck_rows.shape[-1]
    bq = bk

    def lane_fold(x, op):
        out = x[:, :LANES]
        for c0 in range(LANES, bk, LANES):
            out = op(out, x[:, c0:c0 + LANES])
        return out

    def body(q_ref, k_ref, v_ref, cq_ref, ck_ref, o_ref, o32_ref, lse_ref,
             s_scr, p_scr, m_scr, l_scr, red_scr, acc_scr):
        i = pl.program_id(1)
        m_scr[...] = jnp.full_like(m_scr, NEG_INF)
        l_scr[...] = jnp.zeros_like(l_scr)
        acc_scr[...] = jnp.zeros_like(acc_scr)
        qb = q_ref[...]
        ahead = (lax.broadcasted_iota(jnp.int32, (STRIP, bk), 1)
                 - lax.broadcasted_iota(jnp.int32, (STRIP, bk), 0))

        def tile(j, diagonal):
            k0 = pl.multiple_of(j * bk, bk)
            s_scr[...] = lax.dot_general(qb, k_ref[pl.ds(k0, bk), :], NT_DIMS,
                                         preferred_element_type=f32)
            ck = ck_ref[j]
            for r in range(0, bq, STRIP):
                rows = slice(r, r + STRIP)
                s = s_scr[rows, :] + _lanes(cq_ref[rows, :], bk) - ck
                if diagonal:
                    s = jnp.where(ahead <= r, s, NEG_INF)
                s_scr[rows, :] = s
                red_scr[rows, :] = lane_fold(s, jnp.maximum)
            m_old = m_scr[...]
            m_new = jnp.maximum(m_old, jnp.broadcast_to(
                jnp.max(red_scr[...], axis=1, keepdims=True), (bq, LANES)))
            a = jnp.exp(m_old - m_new)
            m_scr[...] = m_new
            for r in range(0, bq, STRIP):
                rows = slice(r, r + STRIP)
                pr = jnp.exp(s_scr[rows, :] - _lanes(m_scr[rows, :], bk))
                red_scr[rows, :] = lane_fold(pr, jnp.add)
                p_scr[rows, :] = pr.astype(bf16)
            l_scr[...] = a * l_scr[...] + jnp.broadcast_to(
                jnp.sum(red_scr[...], axis=1, keepdims=True), (bq, LANES))
            acc_scr[...] = a * acc_scr[...] + jnp.dot(
                p_scr[...], v_ref[pl.ds(k0, bk), :], preferred_element_type=f32)

        def full_tile(j, carry):
            tile(j, False)
            return carry

        lax.fori_loop(0, i, full_tile, 0)
        tile(i, True)
        out = acc_scr[...] / l_scr[...]
        o_ref[...] = out.astype(bf16)
        o32_ref[...] = out
        lse_ref[...] = m_scr[...] + jnp.log(l_scr[...])

    qblk = pl.BlockSpec((bq, HEAD_DIM), lambda h, i: (i, h))
    head_rows = pl.BlockSpec((t, HEAD_DIM), lambda h, i: (0, h))
    rep = pl.BlockSpec((None, bq, LANES), lambda h, i: (h, i, 0))
    col = pltpu.VMEM((bq, LANES), f32)
    return pl.pallas_call(
        body, name=name, grid=(n_heads, t // bq),
        in_specs=[qblk, head_rows, head_rows, rep,
                  pl.BlockSpec((None, t // bk, 1, bk), lambda h, i: (h, 0, 0, 0))],
        out_specs=[qblk, qblk, rep],
        out_shape=[jax.ShapeDtypeStruct((t, d), bf16), jax.ShapeDtypeStruct((t, d), f32),
                   jax.ShapeDtypeStruct((n_heads, t, LANES), f32)],
        scratch_shapes=[pltpu.VMEM((bq, bk), f32), pltpu.VMEM((bq, bk), bf16), col, col, col,
                        pltpu.VMEM((bq, HEAD_DIM), f32)],
        compiler_params=_params("parallel", "parallel"),
    )(q, k, v, cq_rep, ck_rows)


def _loss_head(h, target, r, gain, name):
    t, d = h.shape

    def body(h_ref, t_ref, r_ref, g_ref, dr_ref, drb_ref, dg_ref, db_ref, loss_ref):
        i = pl.program_id(0)

        @pl.when(i == 0)
        def _():
            loss_ref[...] = jnp.zeros_like(loss_ref)
            dg_ref[...] = jnp.zeros_like(dg_ref)
            db_ref[...] = jnp.zeros_like(db_ref)

        diff = jnp.where(i >= 1, h_ref[...] - t_ref[...], 0.0)
        loss_ref[...] += jnp.sum(diff * diff)
        dr, dg, db = _ln_bwd_rows(diff * (1.0 / d), r_ref[...], g_ref[...])
        dr_ref[...] = dr
        drb_ref[...] = dr.astype(bf16)
        dg_ref[...] += dg
        db_ref[...] += db

    row = pl.BlockSpec((LANES, d), lambda i: (i, 0))
    vec = pl.BlockSpec((1, d), lambda i: (0, 0))
    return pl.pallas_call(
        body, name=name, grid=(t // LANES,),
        in_specs=[row, pl.BlockSpec((LANES, d), lambda i: (jnp.maximum(i - 1, 0), 0)), row, vec],
        out_specs=[row, row, vec, vec, pl.BlockSpec((1, LANES), lambda i: (0, 0))],
        out_shape=[jax.ShapeDtypeStruct((t, d), f32), jax.ShapeDtypeStruct((t, d), bf16),
                   jax.ShapeDtypeStruct((1, d), f32), jax.ShapeDtypeStruct((1, d), f32),
                   jax.ShapeDtypeStruct((1, LANES), f32)],
        compiler_params=_params("arbitrary"),
    )(h, target, r, gain)


def _ln_bwd_rows(dy, rr, gain):
    mu = jnp.mean(rr, axis=-1, keepdims=True)
    xc = rr - mu
    var = jnp.mean(xc * xc, axis=-1, keepdims=True)
    rstd = lax.rsqrt(var + LN_EPS)
    xhat = xc * rstd
    dxh = dy * gain
    m1 = jnp.mean(dxh, axis=-1, keepdims=True)
    m2 = jnp.mean(dxh * xhat, axis=-1, keepdims=True)
    dr = rstd * (dxh - m1 - xhat * m2)
    return dr, jnp.sum(dy * xhat, axis=0, keepdims=True), jnp.sum(dy, axis=0, keepdims=True)


def _ffn_bwd_act(drb, wd, layer, a, b, name, comm=None):
    t, d = drb.shape
    s_n, _, n = a.shape
    tm = _tile(t, ROWS_WIDE, 16)

    def body(dr_ref, w_ref, a_ref, b_ref, da_ref, db_ref):
        ds = 0.5 * lax.dot_general(dr_ref[...], w_ref[...], NT_DIMS, preferred_element_type=f32)
        da_ref[...], db_ref[...] = _swiglu_bwd(ds, a_ref, b_ref)

    act = pl.BlockSpec((None, tm, n), lambda s, i: (s, i, 0))
    return _pallas(
        comm, body, name=name, grid=(s_n, t // tm),
        in_specs=[pl.BlockSpec((tm, d), lambda s, i: (i, 0)),
                  pl.BlockSpec((None, None, n, d), lambda s, i: (s, layer, 0, 0)), act, act],
        out_specs=[act, act],
        out_shape=[jax.ShapeDtypeStruct((s_n, t, n), bf16), jax.ShapeDtypeStruct((s_n, t, n), bf16)],
        compiler_params=_params("parallel", "parallel"),
    )(drb, wd, a, b)


def _swiglu_bwd(ds, a_ref, b_ref):
    av = a_ref[...].astype(f32)
    sig = jax.nn.sigmoid(av)
    da = ds * b_ref[...].astype(f32) * (sig * (1.0 + av * (1.0 - sig)))
    return da.astype(bf16), (ds * (av * sig)).astype(bf16)


def _act_spec(mode, tt, k, t_first):
    def fix(fn):
        return (lambda i, s: fn(s, i)) if t_first else fn
    if mode == "shared":
        return pl.BlockSpec((tt, k), fix(lambda s, i: (i, 0)))
    if mode == "cols":
        return pl.BlockSpec((tt, k), fix(lambda s, i: (i, s)))
    assert mode == "stack"
    return pl.BlockSpec((None, tt, k), fix(lambda s, i: (s, i, 0)))


def _act_width(arr, mode, s_n):
    return arr.shape[-1] // s_n if mode == "cols" else arr.shape[-1]


def _tn_matmul(pairs, s_n, scale, name, comm=None):
    t = pairs[0][0].shape[-2]
    tt = _tile(t, 2080, 16)
    n_t = t // tt
    arrays, specs, where = [], [], []
    for x, xmode, y, ymode in pairs:
        pos = []
        for arr, mode in ((x, xmode), (y, ymode)):
            hit = [j for j, a in enumerate(arrays) if a is arr]
            if not hit:
                arrays.append(arr)
                specs.append(_act_spec(mode, tt, _act_width(arr, mode, s_n), False))
                hit = [len(arrays) - 1]
            pos.append(hit[0])
        where.append(pos)
    widths = [(_act_width(x, xm, s_n), _act_width(y, ym, s_n)) for x, xm, y, ym in pairs]
    n_a = len(arrays)

    def body(*refs):
        i = pl.program_id(1)
        for (px, py), o_ref in zip(where, refs[n_a:]):
            part = lax.dot_general(refs[px][...].astype(bf16), refs[py][...].astype(bf16), TN_DIMS,
                                   preferred_element_type=f32)

            @pl.when(i == 0)
            def _():
                o_ref[...] = part

            @pl.when(i > 0)
            def _():
                o_ref[...] += part

            if scale != 1.0:
                @pl.when(i == n_t - 1)
                def _():
                    o_ref[...] = o_ref[...] * scale

    return _pallas(
        comm, body, name=name, grid=(s_n, n_t),
        in_specs=specs,
        out_specs=[pl.BlockSpec((None, kx, ky), lambda s, i: (s, 0, 0)) for kx, ky in widths],
        out_shape=[jax.ShapeDtypeStruct((s_n, kx, ky), f32) for kx, ky in widths],
        compiler_params=_params("parallel", "arbitrary"),
    )(*arrays)


def _nt_sum(pairs, base, base_scale, s_n, out_dtype, name, comm=None, ln=None):
    t = pairs[0][0].shape[-2]
    pairs = [(*pr, False)[:5] for pr in pairs]
    d = pairs[0][2].shape[-1] if pairs[0][4] else pairs[0][2].shape[-2]
    tm = _tile(t, ROWS_ACC if ln is None else ROWS_ACC_LN, 16)
    n_p = len(pairs)
    has_base = base is not None
    flipped = [pr[4] for pr in pairs]
    n_out = 1 if ln is None else 4

    def body(*refs):
        dy_refs = refs[0:2 * n_p:2]
        w_refs = refs[1:2 * n_p:2]
        rest = refs[2 * n_p:]
        base_ref = rest[0] if has_base else None
        o_refs, acc = rest[-1 - n_out:-1], rest[-1]
        s = pl.program_id(1)

        if ln is not None:
            @pl.when(jnp.logical_and(pl.program_id(0) == 0, s == 0))
            def _():
                o_refs[2][...] = jnp.zeros_like(o_refs[2])
                o_refs[3][...] = jnp.zeros_like(o_refs[3])

        @pl.when(s == 0)
        def _():
            acc[...] = jnp.zeros_like(acc)

        tot = None
        for dy_ref, w_ref, flip in zip(dy_refs, w_refs, flipped):
            dyv = dy_ref[...].astype(bf16)
            if flip:
                part = jnp.dot(dyv, w_ref[...], preferred_element_type=f32)
            else:
                part = lax.dot_general(dyv, w_ref[...], NT_DIMS, preferred_element_type=f32)
            tot = part if tot is None else tot + part
        acc[...] += tot

        @pl.when(s == s_n - 1)
        def _():
            res = acc[...]
            if has_base:
                res = base_scale * base_ref[...] + res
            if ln is None:
                o_refs[0][...] = res.astype(o_refs[0].dtype)
            else:
                r_ref, g_ref = rest[-7], rest[-6]
                dr, dg, db = _ln_bwd_rows(res, r_ref[...], g_ref[...])
                o_refs[0][...] = dr
                o_refs[1][...] = dr.astype(bf16)
                o_refs[2][...] += dg
                o_refs[3][...] += db

    in_specs, args = [], []
    for dy, mode, w, layer, flip in pairs:
        k = _act_width(dy, mode, s_n)
        in_specs.append(_act_spec(mode, tm, k, True))
        wshape = (k, d) if flip else (d, k)
        if layer is None:
            in_specs.append(pl.BlockSpec((None, *wshape), lambda i, s: (s, 0, 0)))
        else:
            in_specs.append(pl.BlockSpec((None, None, *wshape),
                                         functools.partial(lambda i, s, l: (s, l, 0, 0), l=layer)))
        args += [dy, w]
    row = pl.BlockSpec((tm, d), lambda i, s: (i, 0))
    vec = pl.BlockSpec((1, d), lambda i, s: (0, 0))
    if has_base:
        in_specs.append(row)
        args.append(base)
    if ln is None:
        out_specs = row
        out_shape = jax.ShapeDtypeStruct((t, d), out_dtype)
    else:
        in_specs += [row, vec]
        args += list(ln)
        out_specs = [row, row, vec, vec]
        out_shape = [jax.ShapeDtypeStruct((t, d), f32), jax.ShapeDtypeStruct((t, d), bf16),
                     jax.ShapeDtypeStruct((1, d), f32), jax.ShapeDtypeStruct((1, d), f32)]
    return _pallas(
        comm, body, name=name, grid=(t // tm, s_n),
        in_specs=in_specs, out_specs=out_specs, out_shape=out_shape,
        scratch_shapes=[pltpu.VMEM((tm, d), f32)],
        compiler_params=_params("parallel" if ln is None else "arbitrary", "arbitrary"),
    )(*args)


def _conv_bwd(dz, p, conv_w, name, comm=None):
    t, d3 = p.shape
    d = d3 // 3
    tm = _tile(t, 320, 8)
    hb = tm // 8
    last8 = t // 8 - 1
    n_ext = tm + 8

    def body(dz_ref, dzn_ref, p_ref, pp_ref, pn_ref, w_ref, dp_ref, dw_ref):
        i = pl.program_id(0)

        @pl.when(i == 0)
        def _():
            dw_ref[...] = jnp.zeros_like(dw_ref)

        w0, w1, w2 = w_ref[0:1, :], w_ref[1:2, :], w_ref[2:3, :]
        rows_u = i * tm - 8 + lax.broadcasted_iota(jnp.int32, (n_ext, 1), 0)
        cg = jnp.concatenate([pp_ref[:, d:2 * d], p_ref[:, d:2 * d]], axis=0)
        val = jnp.concatenate([pp_ref[:, 2 * d:], p_ref[:, 2 * d:]], axis=0)
        u = jnp.where(rows_u >= PAD, cg * val, 0.0)
        u1 = pltpu.roll(u, 1, 0)
        u2 = pltpu.roll(u, 2, 0)
        y = (w2 * u + w1 * u1 + w0 * u2)[8:]
        dzv = dz_ref[...]
        bg = p_ref[:, :d]
        rows_n = (i + 1) * tm + lax.broadcasted_iota(jnp.int32, (8, 1), 0)
        dy_main = dzv * bg
        dy_next = jnp.where(rows_n < t, dzn_ref[...] * pn_ref[:, :d], 0.0)
        dye = jnp.concatenate([dy_main, dy_next], axis=0)
        du = (w2 * dye + w1 * pltpu.roll(dye, n_ext - 1, 0)
              + w0 * pltpu.roll(dye, n_ext - 2, 0))[:tm]
        du = jnp.where(rows_u[8:] >= PAD, du, 0.0)
        dp_ref[:, :d] = (dzv * y).astype(bf16)
        dp_ref[:, d:2 * d] = (du * val[8:]).astype(bf16)
        dp_ref[:, 2 * d:] = (du * cg[8:]).astype(bf16)
        dw_ref[0:1, :] += jnp.sum(dy_main * u2[8:], axis=0, keepdims=True)
        dw_ref[1:2, :] += jnp.sum(dy_main * u1[8:], axis=0, keepdims=True)
        dw_ref[2:3, :] += jnp.sum(dy_main * u[8:], axis=0, keepdims=True)

    nxt = lambda i: (jnp.minimum((i + 1) * hb, last8), 0)
    return _pallas(
        comm, body, name=name, grid=(t // tm,),
        in_specs=[pl.BlockSpec((tm, d), lambda i: (i, 0)),
                  pl.BlockSpec((8, d), nxt),
                  pl.BlockSpec((tm, d3), lambda i: (i, 0)),
                  pl.BlockSpec((8, d3), lambda i: (jnp.maximum(i * hb - 1, 0), 0)),
                  pl.BlockSpec((8, d3), nxt),
                  pl.BlockSpec((3, d), lambda i: (0, 0))],
        out_specs=[pl.BlockSpec((tm, d3), lambda i: (i, 0)),
                   pl.BlockSpec((3, d), lambda i: (0, 0))],
        out_shape=[jax.ShapeDtypeStruct((t, d3), bf16), jax.ShapeDtypeStruct((3, d), f32)],
        compiler_params=_params("arbitrary"),
    )(dz, dz, p, p, p, conv_w)


def _attn_stats(o, do, lse_rep, cq_rep, bq, name):
    t, d = o.shape
    n_heads = d // HEAD_DIM

    def body(o_ref, do_ref, lse_ref, cq_ref, cl_ref, delta_ref):
        for c0 in range(0, bq, LANES):
            rows = slice(c0, c0 + LANES)
            cl_ref[:, rows] = (cq_ref[rows, :] - lse_ref[rows, :]).T[0:1, :]
            prod = o_ref[rows, :] * do_ref[rows, :].astype(f32)
            delta_ref[:, rows] = jnp.sum(prod.T, axis=0, keepdims=True)

    qblk = pl.BlockSpec((bq, HEAD_DIM), lambda h, i: (i, h))
    rep = pl.BlockSpec((None, bq, LANES), lambda h, i: (h, i, 0))
    row = pl.BlockSpec((None, None, 1, bq), lambda h, i: (h, i, 0, 0))
    shp = jax.ShapeDtypeStruct((n_heads, t // bq, 1, bq), f32)
    return pl.pallas_call(
        body, name=name, grid=(n_heads, t // bq),
        in_specs=[qblk, qblk, rep, rep],
        out_specs=[row, row], out_shape=[shp, shp],
        compiler_params=_params("parallel", "parallel"),
    )(o, do, lse_rep, cq_rep)


def _attn_bwd(q, k, v, kt, do, ckey, cl_rows, delta_rows, name, comm=None):
    t, d = q.shape
    n_heads = d // HEAD_DIM
    bk = kt.shape[-1]
    bq = bk
    n_kv = t // bk
    n_q = t // bq
    scale = 1.0 / math.sqrt(HEAD_DIM)

    def body(q_ref, do_ref, cl_ref, dl_ref, k_ref, v_ref, kt_ref, ck_ref,
             dq_ref, dcq_ref, dk_ref, dv_ref, dck_ref,
             st_scr, dp_scr, p_scr, ds_scr, dqt, dk_acc, dv_acc, dck_acc):
        j = pl.program_id(1)

        @pl.when(j == 0)
        def _():
            dqt[...] = jnp.zeros_like(dqt)
            dcq_ref[...] = jnp.zeros_like(dcq_ref)

        dk_acc[...] = jnp.zeros_like(dk_acc)
        dv_acc[...] = jnp.zeros_like(dv_acc)
        dck_acc[...] = jnp.zeros_like(dck_acc)
        kb = k_ref[...]
        vb = v_ref[...]
        behind = (lax.broadcasted_iota(jnp.int32, (STRIP, bq), 1)
                  - lax.broadcasted_iota(jnp.int32, (STRIP, bq), 0))

        def tile(i, diagonal):
            r0 = pl.multiple_of(i * bq, bq)
            qi = q_ref[pl.ds(r0, bq), :]
            doi = do_ref[pl.ds(r0, bq), :]
            st_scr[...] = lax.dot_general(kb, qi, NT_DIMS, preferred_element_type=f32)
            dp_scr[...] = lax.dot_general(vb, doi, NT_DIMS, preferred_element_type=f32)
            cl = cl_ref[i]
            dl = dl_ref[i]
            over_keys = jnp.zeros((STRIP, bq), f32)
            for r in range(0, bk, STRIP):
                keys = slice(r, r + STRIP)
                st = st_scr[keys, :] + cl - _lanes(ck_ref[keys, :], bq)
                if diagonal:
                    st = jnp.where(behind >= r, st, NEG_INF)
                pr = jnp.exp(st)
                ds = pr * (dp_scr[keys, :] - dl)
                over_keys = over_keys + ds
                dck_acc[keys, :] -= jnp.sum(ds, axis=1, keepdims=True)
                p_scr[keys, :] = pr.astype(bf16)
                ds_scr[keys, :] = ds.astype(bf16)
            dcq_ref[i] += jnp.sum(over_keys, axis=0, keepdims=True)
            dv_acc[...] += jnp.dot(p_scr[...], doi, preferred_element_type=f32)
            dk_acc[...] += jnp.dot(ds_scr[...], qi, preferred_element_type=f32)
            dqt[i] += jnp.dot(kt_ref[...], ds_scr[...], preferred_element_type=f32)

        def full_tile(i, carry):
            tile(i, False)
            return carry

        tile(j, True)
        lax.fori_loop(j + 1, n_q, full_tile, 0)
        dk_ref[...] = dk_acc[...].astype(bf16)
        dv_ref[...] = dv_acc[...].astype(bf16)
        for c0 in range(0, bk, LANES):
            keys = slice(c0, c0 + LANES)
            dck_ref[:, keys] = jnp.broadcast_to(dck_acc[keys, :], (LANES, LANES)).T[0:1, :]

        @pl.when(j == n_kv - 1)
        def _():
            def emit(i, carry):
                r0 = pl.multiple_of(i * bq, bq)
                dq_ref[pl.ds(r0, bq), :] = dqt[i].T * scale
                return carry
            lax.fori_loop(0, n_q, emit, 0)

    head_rows = pl.BlockSpec((t, HEAD_DIM), lambda h, j: (0, h))
    head_stat = pl.BlockSpec((None, n_q, 1, bq), lambda h, j: (h, 0, 0, 0))
    kblk = pl.BlockSpec((bk, HEAD_DIM), lambda h, j: (j, h))
    return _pallas(
        comm, body, name=name, grid=(n_heads, n_kv),
        in_specs=[head_rows, head_rows, head_stat, head_stat, kblk, kblk,
                  pl.BlockSpec((None, None, HEAD_DIM, bk), lambda h, j: (h, j, 0, 0)),
                  pl.BlockSpec((None, bk, LANES), lambda h, j: (h, j, 0))],
        out_specs=[head_rows, head_stat, kblk, kblk,
                   pl.BlockSpec((None, None, 1, bk), lambda h, j: (h, j, 0, 0))],
        out_shape=[jax.ShapeDtypeStruct((t, d), f32),
                   jax.ShapeDtypeStruct((n_heads, n_q, 1, bq), f32),
                   jax.ShapeDtypeStruct((t, d), bf16), jax.ShapeDtypeStruct((t, d), bf16),
                   jax.ShapeDtypeStruct((n_heads, n_kv, 1, bk), f32)],
        scratch_shapes=[pltpu.VMEM((bk, bq), f32), pltpu.VMEM((bk, bq), f32),
                        pltpu.VMEM((bk, bq), bf16), pltpu.VMEM((bk, bq), bf16),
                        pltpu.VMEM((n_q, HEAD_DIM, bq), f32),
                        pltpu.VMEM((bk, HEAD_DIM), f32), pltpu.VMEM((bk, HEAD_DIM), f32),
                        pltpu.VMEM((bk, 1), f32)],
        compiler_params=_params("parallel", "arbitrary"),
    )(q, do, cl_rows, delta_rows, k, v, kt, ckey)


def _fgate_bwd(dlogf, flog, fbias, name):
    t, n = flog.shape

    def body(dl_ref, fl_ref, fb_ref, o_ref, sum_ref):
        i = pl.program_id(0)

        @pl.when(i == 0)
        def _():
            sum_ref[...] = jnp.zeros_like(sum_ref)

        r = i * LANES + lax.broadcasted_iota(jnp.int32, (LANES, n), 0)
        g = dl_ref[...] * jax.nn.sigmoid(-(fl_ref[...] + fb_ref[...]))
        g = jnp.where(r >= PAD, g, 0.0)
        o_ref[...] = g
        sum_ref[...] += jnp.sum(g, axis=0, keepdims=True)

    blk = pl.BlockSpec((LANES, n), lambda i: (i, 0))
    vec = pl.BlockSpec((1, n), lambda i: (0, 0))
    return pl.pallas_call(
        body, name=name, grid=(t // LANES,),
        in_specs=[blk, blk, vec], out_specs=[blk, vec],
        out_shape=[jax.ShapeDtypeStruct((t, n), f32), jax.ShapeDtypeStruct((1, n), f32)],
        compiler_params=_params("arbitrary"),
    )(dlogf, flog, fbias)


def _place_shard(w, idx, name):
    n_l, r, c_n = w.shape
    out_dtype = bf16 if r * c_n > 2 ** 16 else w.dtype
    tr = _tile(r, 512, 16) if r % 16 == 0 else r

    def body(idx_ref, w_ref, o_ref):
        o_ref[...] = w_ref[...].astype(out_dtype)

    grid_spec = pltpu.PrefetchScalarGridSpec(
        num_scalar_prefetch=1, grid=(n_l, r // tr),
        in_specs=[pl.BlockSpec((None, tr, c_n), lambda l, i, idx: (l, i, 0))],
        out_specs=pl.BlockSpec((None, None, tr, c_n), lambda l, i, idx: (idx[0], l, i, 0)))
    return pl.pallas_call(
        body, name=name, grid_spec=grid_spec,
        out_shape=jax.ShapeDtypeStruct((N_CHIPS, n_l, r, c_n), out_dtype),
        compiler_params=_params("parallel", "parallel"),
    )(idx, w)


def _plan_gather_ici(items):
    def plan(srcs, bufs, news, p):
        out = []
        for name, layer, r2 in items:
            mine = bufs[name].at[p.me, layer, pl.ds(p.c * r2, r2)]
            out += [(mine, mine, (*chip, p.c)) for chip in p.chips]
        return out
    return plan, 3 * len(items)


def _plan_gather_d2d(items):
    def plan(srcs, bufs, news, p):
        out = []
        for name, layer, r2 in items:
            for px, py in p.chips:
                landed = bufs[name].at[2 * px + py, layer, pl.ds(p.c * r2, r2)]
                out.append((landed, landed, p.sib))
        return out
    return plan, 3 * len(items)


def _plan_pair_exchange(names, r2s):
    def plan(srcs, bufs, news, p):
        out = []
        for name, r2 in zip(names, r2s):
            for k, slot in enumerate(p.slots):
                out.append((srcs["G_" + name].at[slot, pl.ds((1 - p.c) * r2, r2)],
                            news["PAIR_" + name].at[k], p.sib))
        return out
    return plan, 4 * len(names)


def _plan_chip_exchange(names):
    def plan(srcs, bufs, news, p):
        out = []
        for name in names:
            for k, chip in enumerate(p.chips):
                out.append((srcs["SEND_" + name].at[k], news["RECV_" + name].at[k], (*chip, p.c)))
        return out
    return plan, 3 * len(names)


def _plan_pair_share(items):
    def plan(srcs, bufs, news, p):
        out = []
        for name, layer, r2 in items:
            mine = bufs["RED_" + name].at[layer, pl.ds(p.c * r2, r2)]
            out.append((mine, mine, p.sib))
        return out
    return plan, len(items)


def _rs_prepare(gs, pairs, idx, name):
    n = len(gs)
    _, r2, c_n = pairs[0].shape
    tr = _tile(r2, 256, 8)
    nb = r2 // tr

    def body(idx_ref, *refs):
        for g_ref, p_ref, o_ref in zip(refs[:n], refs[n:2 * n], refs[2 * n:]):
            o_ref[...] = (g_ref[...] + p_ref[...]).astype(bf16)

    g_spec = pl.BlockSpec((None, tr, c_n), lambda k, i, idx: (idx[k + 1], idx[4] * nb + i, 0))
    p_spec = pl.BlockSpec((None, tr, c_n), lambda k, i, idx: (k + 1, i, 0))
    grid_spec = pltpu.PrefetchScalarGridSpec(
        num_scalar_prefetch=1, grid=(3, nb), in_specs=[g_spec] * n + [p_spec] * n,
        out_specs=[pl.BlockSpec((None, tr, c_n), lambda k, i, idx: (k, i, 0))] * n)
    return pl.pallas_call(
        body, name=name, grid_spec=grid_spec,
        out_shape=[jax.ShapeDtypeStruct((3, r2, c_n), bf16)] * n,
        compiler_params=_params("parallel", "parallel"),
    )(idx, *gs, *pairs)


def _rs_finish(gs, pairs, recvs, idx, layer, n_layers, intos, name):
    n = len(gs)
    _, r2, c_n = pairs[0].shape
    tr = _tile(r2, 256, 8)
    nb = r2 // tr

    def body(idx_ref, *refs):
        for j in range(n):
            g_ref, p_ref, r0_ref, r1_ref, r2_ref = refs[5 * j:5 * j + 5]
            acc = g_ref[...] + p_ref[...]
            acc = acc + r0_ref[...].astype(f32)
            acc = acc + r1_ref[...].astype(f32)
            acc = acc + r2_ref[...].astype(f32)
            refs[len(refs) - n + j][...] = acc

    def rspec(k):
        return pl.BlockSpec((None, tr, c_n), functools.partial(lambda i, idx, kk: (kk, i, 0), kk=k))

    item_specs = [pl.BlockSpec((None, tr, c_n), lambda i, idx: (idx[0], idx[4] * nb + i, 0)),
                  rspec(0), rspec(0), rspec(1), rspec(2)]
    in_specs = item_specs * n
    args = [idx]
    for g, pair, recv in zip(gs, pairs, recvs):
        args += [g, pair, recv, recv, recv]
    aliases = {}
    if intos is not None:
        in_specs = in_specs + [pl.BlockSpec(memory_space=pl.ANY)] * n
        args += list(intos)
        aliases = {1 + 5 * n + j: j for j in range(n)}
    grid_spec = pltpu.PrefetchScalarGridSpec(
        num_scalar_prefetch=1, grid=(nb,), in_specs=in_specs,
        out_specs=[pl.BlockSpec((None, tr, c_n), lambda i, idx: (layer, idx[4] * nb + i, 0))] * n)
    return pl.pallas_call(
        body, name=name, grid_spec=grid_spec,
        out_shape=[jax.ShapeDtypeStruct((n_layers, 2 * r2, c_n), f32)] * n,
        input_output_aliases=aliases,
        compiler_params=_params("parallel"),
    )(*args)


def _adamw(ws, gs, ms, vs, layer, intos, name):
    n = len(ws)
    n_l, r, c_n = ws[0].shape
    tr = _tile(r, 256, 8)

    def body(*refs):
        for j in range(n):
            w_ref, g_ref, m_ref, v_ref = refs[4 * j:4 * j + 4]
            d_ref, mo_ref, vo_ref = refs[len(refs) - 3 * n + 3 * j:len(refs) - 3 * n + 3 * j + 3]
            gv = g_ref[...]
            mn = ADAM_B1 * m_ref[...] + (1.0 - ADAM_B1) * gv
            vn = ADAM_B2 * v_ref[...] + (1.0 - ADAM_B2) * (gv * gv)
            m_hat = mn / (1.0 - ADAM_B1 ** ADAM_STEP)
            v_hat = vn / (1.0 - ADAM_B2 ** ADAM_STEP)
            d_ref[...] = -ADAM_LR * (m_hat / (jnp.sqrt(v_hat) + ADAM_EPS) + ADAM_WD * w_ref[...])
            mo_ref[...] = mn
            vo_ref[...] = vn

    blk = pl.BlockSpec((None, tr, c_n), lambda i: (layer, i, 0))
    shp = jax.ShapeDtypeStruct((n_l, r, c_n), f32)
    in_specs = [blk] * (4 * n)
    args = []
    for w, g, m, v in zip(ws, gs, ms, vs):
        args += [w, g, m, v]
    aliases = {}
    if intos is not None:
        in_specs = in_specs + [pl.BlockSpec(memory_space=pl.ANY)] * (3 * n)
        for into in intos:
            args += list(into)
        aliases = {4 * n + j: j for j in range(3 * n)}
    res = pl.pallas_call(
        body, name=name, grid=(r // tr,),
        in_specs=in_specs, out_specs=[blk] * (3 * n), out_shape=[shp] * (3 * n),
        input_output_aliases=aliases,
        compiler_params=_params("parallel"),
    )(*args)
    return [tuple(res[3 * j:3 * j + 3]) for j in range(n)]


def kernel(x, meta, ffn1_wg, ffn1_wu, ffn1_wd, ffn2_wg, ffn2_wu, ffn2_wd, ln_gain, ln_bias, conv_w_in, conv_w, conv_w_out, kv_w, f_bias, attn_w_q, attn_w_o, loss_target, m_meta, m_ffn1_wg, m_ffn1_wu, m_ffn1_wd, m_ffn2_wg, m_ffn2_wu, m_ffn2_wd, m_ln_gain, m_ln_bias, m_conv_w_in, m_conv_w, m_conv_w_out, m_kv_w, m_f_bias, m_attn_w_q, m_attn_w_o, v_meta, v_ffn1_wg, v_ffn1_wu, v_ffn1_wd, v_ffn2_wg, v_ffn2_wu, v_ffn2_wd, v_ln_gain, v_ln_bias, v_conv_w_in, v_conv_w, v_conv_w_out, v_kv_w, v_f_bias, v_attn_w_q, v_attn_w_o):
    seq, d = x.shape[1], x.shape[2]
    t = PAD + N_META + seq
    n_heads = d // HEAD_DIM
    dq = d // N_CHIPS
    n_kv = kv_w.shape[1]
    x2 = x[0]
    target = loss_target[0]

    def rows8(a):
        return jnp.pad(a, ((0, 8 - a.shape[0]), (0, 0)))

    def small_pack(mt, g, b, cw, fb):
        fb_row = jnp.pad(fb, (0, mt.shape[1] - n_heads))[None]
        return jnp.concatenate([mt, rows8(g.reshape(6, -1)), rows8(b.reshape(6, -1)),
                                rows8(cw.reshape(3, -1)), rows8(fb_row)], axis=0)

    w_small = small_pack(meta, ln_gain, ln_bias, conv_w, f_bias)

    cx, cy, c = lax.axis_index("x"), lax.axis_index("y"), lax.axis_index("c")
    idx = jnp.stack([2 * cx + cy, 2 * (1 - cx) + cy, 2 * cx + (1 - cy), 2 * (1 - cx) + (1 - cy), c]
                    ).astype(jnp.int32)

    def tr(a):
        return a.transpose(0, 2, 1)

    w3 = {"wg1": tr(ffn1_wg), "wu1": tr(ffn1_wu), "wd1": ffn1_wd, "wg2": tr(ffn2_wg),
          "wu2": tr(ffn2_wu), "wd2": ffn2_wd, "win": conv_w_in, "wout": conv_w_out, "kv": kv_w[None],
          "wq": attn_w_q, "wo": attn_w_o, "small": w_small[None]}
    transposed = ("wg1", "wu1", "wg2", "wu2")
    buf = {n: _place_shard(w, idx, "place_shard") for n, w in w3.items()}

    def split(item):
        name, layer = item.split(".")
        return name, int(layer)

    def gather_stage(planner, items):
        triples = [(n, l, buf[n].shape[2] // 2) for n, l in map(split, items)]
        plan, n_copies = planner(triples)
        return dict(plan=plan, n=n_copies, bufs={n: buf[n] for n, _, _ in triples})

    def ici(items):
        return gather_stage(_plan_gather_ici, items)

    def d2d(items):
        return gather_stage(_plan_gather_d2d, items)

    def pair_exchange(items):
        r2s = [buf["G_" + it].shape[1] // 2 for it in items]
        plan, n_copies = _plan_pair_exchange(items, r2s)
        news = {"PAIR_" + it: jax.ShapeDtypeStruct((N_CHIPS, r2, buf["G_" + it].shape[2]), f32)
                for it, r2 in zip(items, r2s)}
        return dict(plan=plan, n=n_copies, srcs={"G_" + it: buf["G_" + it] for it in items}, news=news)

    def chip_exchange(items):
        plan, n_copies = _plan_chip_exchange(items)
        srcs = {"SEND_" + it: buf["SEND_" + it] for it in items}
        news = {"RECV_" + it: jax.ShapeDtypeStruct(s.shape, s.dtype)
                for it, s in ((it, buf["SEND_" + it]) for it in items)}
        return dict(plan=plan, n=n_copies, srcs=srcs, news=news)

    def pair_share(items):
        triples = [(n, l, buf["RED_" + n].shape[1] // 2) for n, l in map(split, items)]
        plan, n_copies = _plan_pair_share(triples)
        return dict(plan=plan, n=n_copies, bufs={"RED_" + n: buf["RED_" + n] for n, _, _ in triples})

    def run(fn, *args, stages=(), name=None, **kw):
        comm = _Copies()
        for st in (stages() if callable(stages) else stages):
            comm.add(st["plan"], st["n"], srcs=st.get("srcs"), bufs=st.get("bufs"), news=st.get("news"))
        out = _copy_call(comm, name) if fn is None else fn(*args, comm=comm, **kw)
        buf.update(comm.out_bufs)
        buf.update(comm.out_news)
        return out

    first = ["wg1.0", "wu1.0", "small.0"]
    run(None, stages=[ici(first)], name="gather_first_ici")
    run(None, stages=[d2d(first)], name="gather_first_d2d")
    small = buf["small"].reshape(N_CHIPS, SMALL_ROWS, dq).transpose(1, 0, 2).reshape(SMALL_ROWS, d)
    meta_full = small[:N_META]
    gains = small[16:22].reshape(DEPTH, 3, 1, d)
    biases = small[24:30].reshape(DEPTH, 3, 1, d)
    conv_w_full = small[32:35]
    fb_pad = jnp.pad(f_bias, (0, LANES - n_heads))[None]
    conv_w8 = ["wd1.0", "win.0", "wout.0"]
    ffn2_l0 = ["wg2.0", "wu2.0", "wd2.0", "kv.0"]
    attn_ffn2_l1 = ["wq.0", "wo.0", "wg2.1", "wu2.1", "wd2.1"]
    ffn1_l1 = ["wg1.1", "wu1.1", "wd1.1"]

    meta_pad = jnp.concatenate([jnp.zeros((PAD, d), f32), meta_full], axis=0)
    h0, h0b = run(_embed, meta_pad, x2, "embed", stages=[ici(conv_w8)])
    a1, b1, s1 = run(_ffn_up, h0b, buf["wg1"], buf["wu1"], 0, "ffn_up",
                     stages=lambda: [d2d(conv_w8), ici(ffn2_l0)])
    r1, h1, h1b = run(_down_ln, s1, buf["wd1"], 0, h0, gains[0, 0], biases[0, 0], 0.5, "ffn_down_ln",
                      stages=lambda: [d2d(ffn2_l0), ici(attn_ffn2_l1)])
    n_in = conv_w_in.shape[-1]
    w_in = buf["win"].reshape(N_CHIPS, d, n_in)
    w_out = buf["wout"].reshape(1, 1, d, d)
    p = run(_nn_matmul, h1b, w_in, f32, "conv_in", stages=lambda: [d2d(attn_ffn2_l1), ici(ffn1_l1)])
    z = _conv_fwd(p, conv_w_full, "conv_fwd")
    r2, h2, h2b = run(_down_ln, z[None], w_out, 0, h1, gains[0, 1], biases[0, 1], 1.0, "mix_out_ln",
                      stages=lambda: [d2d(ffn1_l1)])
    wg1, wu1, wd1, wg2, wu2, wd2 = (buf[n] for n in ("wg1", "wu1", "wd1", "wg2", "wu2", "wd2"))
    w_q = buf["wq"].reshape(1, d, d)
    w_o = buf["wo"].reshape(1, 1, d, d)
    kv_full = buf["kv"].reshape(N_CHIPS, d, n_kv).transpose(1, 0, 2).reshape(d, N_CHIPS * n_kv)
    w_k = kv_full[:, :d][None]
    w_v = kv_full[:, d:2 * d][None]
    w_f = jnp.pad(kv_full[:, 2 * d:], ((0, 0), (0, LANES - n_heads)))[None]
    a2, b2, s2 = _ffn_up(h2b, wg2, wu2, 0, "ffn_up")
    r3, h3, h3b = _down_ln(s2, wd2, 0, h2, gains[0, 2], biases[0, 2], 0.5, "ffn_down_ln")
    kk = _nn_matmul(h3b, w_k, bf16, "proj_bf16")
    vv = _nn_matmul(h3b, w_v, bf16, "proj_bf16")
    flog = _nn_matmul(h3b, w_f, f32, "proj_gate")
    cum = _row_scan(flog, "gate_cumsum", fb_pad)
    bk = _tile(t, 640, LANES)
    c_ht = cum[:, :n_heads].T
    c_keys = jnp.where(jnp.arange(t)[None, :] < PAD, 1e30, c_ht)
    cq_rep = jnp.broadcast_to(c_ht[:, :, None], (n_heads, t, LANES))
    ck_rep = jnp.broadcast_to(c_keys[:, :, None], (n_heads, t, LANES))
    ck_rows = c_keys.reshape(n_heads, t // bk, 1, bk)
    a3, b3, s3 = _ffn_up(h3b, wg1, wu1, 1, "ffn_up")
    r4, h4, h4b = _down_ln(s3, wd1, 1, h3, gains[1, 0], biases[1, 0], 0.5, "ffn_down_ln")
    q = _nn_matmul(h4b, w_q, bf16, "proj_q", out_scale=1.0 / math.sqrt(HEAD_DIM))
    o, o32, lse_rep = _attn_fwd(q, kk, vv, cq_rep, ck_rows, "attn_fwd")
    r5, h5, h5b = _down_ln(o[None], w_o, 0, h4, gains[1, 1], biases[1, 1], 1.0, "mix_out_ln")
    a4, b4, s4 = _ffn_up(h5b, wg2, wu2, 1, "ffn_up")
    r6, h6, _ = _down_ln(s4, wd2, 1, h5, gains[1, 2], biases[1, 2], 0.5, "ffn_down_ln")
    dr6, dr6b, dg12, db12, sq = _loss_head(h6, target, r6, gains[1, 2], "loss_head")
    loss_part = 0.5 * sq[0, 0] / d

    m_small = small_pack(m_meta, m_ln_gain, m_ln_bias, m_conv_w, m_f_bias)
    v_small = small_pack(v_meta, v_ln_gain, v_ln_bias, v_conv_w, v_f_bias)
    m3 = {"wg1": tr(m_ffn1_wg), "wu1": tr(m_ffn1_wu), "wd1": m_ffn1_wd, "wg2": tr(m_ffn2_wg),
          "wu2": tr(m_ffn2_wu), "wd2": m_ffn2_wd, "win": m_conv_w_in, "wout": m_conv_w_out,
          "kv": m_kv_w[None], "wq": m_attn_w_q, "wo": m_attn_w_o, "small": m_small[None]}
    v3 = {"wg1": tr(v_ffn1_wg), "wu1": tr(v_ffn1_wu), "wd1": v_ffn1_wd, "wg2": tr(v_ffn2_wg),
          "wu2": tr(v_ffn2_wu), "wd2": v_ffn2_wd, "win": v_conv_w_in, "wout": v_conv_w_out,
          "kv": v_kv_w[None], "wq": v_attn_w_q, "wo": v_attn_w_o, "small": v_small[None]}
    stepped = {}

    def alike(items):
        groups = {}
        for it in items:
            n, l = split(it)
            groups.setdefault((w3[n].shape, l, ("RED_" + n) in buf, n in stepped), []).append(it)
        return groups.values()

    def prepare(items):
        for group in alike(items):
            sends = _rs_prepare([buf["G_" + it] for it in group], [buf["PAIR_" + it] for it in group],
                                idx, "grad_prepare")
            buf.update({"SEND_" + it: s for it, s in zip(group, sends)})

    def finish(items):
        for group in alike(items):
            names = [split(it)[0] for it in group]
            layer = split(group[0])[1]
            intos = [buf["RED_" + n] for n in names] if ("RED_" + names[0]) in buf else None
            reds = _rs_finish([buf["G_" + it] for it in group], [buf["PAIR_" + it] for it in group],
                              [buf["RECV_" + it] for it in group], idx, layer, w3[names[0]].shape[0],
                              intos, "grad_finish")
            buf.update({"RED_" + n: r for n, r in zip(names, reds)})

    def adam(items, grads=None):
        for group in alike(items):
            names = [split(it)[0] for it in group]
            layer = split(group[0])[1]
            gs = [buf["RED_" + n] if grads is None else grads[n] for n in names]
            intos = [stepped[n] for n in names] if names[0] in stepped else None
            res = _adamw([w3[n] for n in names], gs, [m3[n] for n in names], [v3[n] for n in names],
                         layer, intos, "adamw")
            stepped.update(dict(zip(names, res)))

    def ffn_bwd(dr, drb, hb_in, a, b, s, f, layer, on_act=(), after_act=None, on_dwd=(),
                after_dwd=None, on_dx=(), ln=None):
        da, db = run(_ffn_bwd_act, drb, buf["wd" + f], layer, a, b, "ffn_bwd_act", stages=on_act)
        if after_act is not None:
            after_act()
        (buf[f"G_wd{f}.{layer}"],) = run(_tn_matmul, [(s, "stack", drb, "shared")], N_CHIPS, 0.5,
                                         "ffn_dwd", stages=on_dwd)
        if after_dwd is not None:
            after_dwd()
        buf[f"G_wg{f}.{layer}"], buf[f"G_wu{f}.{layer}"] = _tn_matmul(
            [(da, "stack", hb_in, "shared"), (db, "stack", hb_in, "shared")], N_CHIPS, 1.0, "ffn_dwgu")
        return run(_nt_sum, [(da, "stack", buf["wg" + f], layer, True),
                             (db, "stack", buf["wu" + f], layer, True)],
                   dr, ALPHA, N_CHIPS, f32, "ffn_dx", stages=on_dx, ln=ln)

    ffn2_1 = ["wg2.1", "wu2.1", "wd2.1"]
    ffn1_1 = ["wg1.1", "wu1.1", "wd1.1"]
    ffn2_0 = ["wg2.0", "wu2.0", "wd2.0"]
    conv_items = ["wout.0", "win.0"]

    dr5, dr5b, dg11, db11 = ffn_bwd(dr6, dr6b, h5b, a4, b4, s4, "2", 1,
                                    on_dx=lambda: [pair_exchange(ffn2_1)], ln=(r5, gains[1, 1]))
    prepare(ffn2_1)
    (dwo,) = _tn_matmul([(o, "shared", dr5b, "shared")], 1, 1.0, "sq_dw")
    buf["G_wo.0"] = dwo.reshape(N_CHIPS, dq, d)
    do = run(_nt_sum, [(dr5b, "cols", w_o[0], None)], None, 1.0, 1, bf16, "sq_dx_bf16",
             stages=lambda: [pair_exchange(["wo.0"])])
    prepare(["wo.0"])
    cl, delta = _attn_stats(o32, do, lse_rep, cq_rep, bk, "attn_stats")
    kt = kk.reshape(t // bk, bk, n_heads, HEAD_DIM).transpose(2, 0, 3, 1)
    dq_att, dc_q, dk, dv, dc_k = run(_attn_bwd, q, kk, vv, kt, do, ck_rep, cl, delta, "attn_bwd",
                                     stages=lambda: [chip_exchange(ffn2_1 + ["wo.0"])])
    finish(ffn2_1 + ["wo.0"])
    dc = (dc_q + dc_k).reshape(n_heads, t)
    (dwq,) = run(_tn_matmul, [(h4b, "shared", dq_att, "shared")], 1, 1.0, "sq_dw",
                 stages=lambda: [pair_share(ffn2_1 + ["wo.0"])])
    buf["G_wq.0"] = dwq.reshape(N_CHIPS, dq, d)
    adam(ffn2_1 + ["wo.0"])
    dr4, dr4b, dg10, db10 = run(_nt_sum, [(dq_att, "cols", w_q, None)], dr5, ALPHA, 1, f32,
                                "sq_dx_res", stages=lambda: [pair_exchange(["wq.0"])],
                                ln=(r4, gains[1, 0]))
    prepare(["wq.0"])
    dh3a = ffn_bwd(dr4, dr4b, h3b, a3, b3, s3, "1", 1,
                   on_act=lambda: [chip_exchange(["wq.0"])],
                   on_dx=lambda: [pair_exchange(ffn1_1)])
    prepare(ffn1_1)
    finish(["wq.0"])
    dc_t = jnp.pad(dc.T, ((0, 0), (0, LANES - n_heads)))
    dlogf = _row_scan(dc_t, "rev_cumsum", reverse=True)
    dfl, dfb_cols = _fgate_bwd(dlogf, flog, fb_pad, "gate_bwd")
    dwk, dwv, dwf = run(_tn_matmul, [(h3b, "shared", g, "shared") for g in (dk, dv, dfl)], 1, 1.0,
                        "kv_dw",
                        stages=lambda: [chip_exchange(ffn1_1), pair_share(["wq.0"])])
    adam(["wq.0"])

    def by_chip(full):
        rows = full.shape[0]
        return full.reshape(rows, N_CHIPS, full.shape[1] // N_CHIPS).transpose(1, 0, 2)

    buf["G_kv.0"] = by_chip(jnp.concatenate([dwk[0], dwv[0], dwf[0][:, :n_heads]], axis=1))
    dr3, dr3b, dg02, db02 = run(
        _nt_sum, [(dk, "cols", w_k, None), (dv, "cols", w_v, None), (dfl, "cols", w_f, None)],
        dh3a, 1.0, 1, f32, "kv_dx", stages=lambda: [pair_exchange(["kv.0"])], ln=(r3, gains[0, 2]))
    prepare(["kv.0"])
    finish(ffn1_1)
    dr2, dr2b, dg01, db01 = ffn_bwd(dr3, dr3b, h2b, a2, b2, s2, "2", 0,
                                    on_act=lambda: [chip_exchange(["kv.0"]), pair_share(ffn1_1)],
                                    after_act=lambda: (adam(ffn1_1), finish(["kv.0"])),
                                    on_dwd=lambda: [pair_share(["kv.0"])],
                                    after_dwd=lambda: adam(["kv.0"]),
                                    on_dx=lambda: [pair_exchange(ffn2_0)], ln=(r2, gains[0, 1]))
    prepare(ffn2_0)
    (dwout,) = _tn_matmul([(z, "shared", dr2b, "shared")], 1, 1.0, "sq_dw")
    buf["G_wout.0"] = dwout.reshape(N_CHIPS, dq, d)
    dz = _nt_sum([(dr2b, "cols", w_out[0], None)], None, 1.0, 1, f32, "sq_dx_f32")
    dp, dconv_w = run(_conv_bwd, dz, p, conv_w_full, "conv_bwd",
                      stages=lambda: [chip_exchange(ffn2_0)])
    (buf["G_win.0"],) = _tn_matmul([(h1b, "shared", dp, "cols")], N_CHIPS, 1.0, "conv_dwin")
    finish(ffn2_0)
    dr1, dr1b, dg00, db00 = run(_nt_sum, [(dp, "cols", w_in, None)], dr2, ALPHA, N_CHIPS, f32,
                                "conv_dx",
                                stages=lambda: [pair_exchange(conv_items), pair_share(ffn2_0)],
                                ln=(r1, gains[0, 0]))
    prepare(conv_items)
    adam(ffn2_0)
    gate_up = ["wg1.0", "wu1.0"]
    da, db = run(_ffn_bwd_act, dr1b, buf["wd1"], 0, a1, b1, "ffn_bwd_act",
                 stages=lambda: [chip_exchange(conv_items)])
    finish(conv_items)
    buf["G_wg1.0"], buf["G_wu1.0"] = run(
        _tn_matmul, [(da, "stack", h0b, "shared"), (db, "stack", h0b, "shared")], N_CHIPS, 1.0,
        "ffn_dwgu", stages=lambda: [pair_share(conv_items)])
    adam(conv_items)
    (buf["G_wd1.0"],) = run(_tn_matmul, [(s1, "stack", dr1b, "shared")], N_CHIPS, 0.5, "ffn_dwd",
                            stages=lambda: [pair_exchange(gate_up)])
    prepare(gate_up)
    dh0 = run(_nt_sum, [(da, "stack", buf["wg1"], 0, True), (db, "stack", buf["wu1"], 0, True)],
              dr1, ALPHA, N_CHIPS, f32, "ffn_dx",
              stages=lambda: [chip_exchange(gate_up), pair_exchange(["wd1.0"])])
    prepare(["wd1.0"])
    finish(gate_up)
    grad_x = dh0[PAD + N_META:][None]
    dmeta = dh0[PAD:PAD + N_META]
    buf["G_small.0"] = by_chip(jnp.concatenate(
        [dmeta, rows8(jnp.concatenate([dg00, dg01, dg02, dg10, dg11, dg12], axis=0)),
         rows8(jnp.concatenate([db00, db01, db02, db10, db11, db12], axis=0)),
         rows8(dconv_w), jnp.zeros((8, d), f32)], axis=0))
    run(None, stages=lambda: [chip_exchange(["wd1.0"]), pair_exchange(["small.0"]), pair_share(gate_up)],
        name="grad_tail_1")
    prepare(["small.0"])
    finish(["wd1.0"])
    adam(gate_up)
    run(None, stages=lambda: [chip_exchange(["small.0"]), pair_share(["wd1.0"])], name="grad_tail_2")
    finish(["small.0"])
    adam(["wd1.0"])
    run(None, stages=lambda: [pair_share(["small.0"])], name="grad_tail_3")

    tail = jnp.zeros((LANES,), f32).at[:n_heads].set(dfb_cols[0, :n_heads]).at[n_heads].set(loss_part)
    tail = lax.psum(tail, ("x", "y", "c"))
    loss = tail[n_heads]
    g_fb = tail[:n_heads]
    g_small = jnp.concatenate([buf["RED_small"][0, :40],
                               rows8(jnp.pad(g_fb, (0, dq - n_heads))[None])], axis=0)
    adam(["small.0"], grads={"small": g_small[None]})

    def unpack(pk):
        return (pk[:16], pk[16:22].reshape(DEPTH, 3, dq), pk[24:30].reshape(DEPTH, 3, dq),
                pk[32:35].reshape(1, 3, dq), pk[40, :n_heads])

    def order(pick):
        mt, g, b, cw, fb = unpack(pick("small")[0])
        big = {n: pick(n) for n in w3 if n != "small"}
        big["kv"] = big["kv"][0]
        for n in transposed:
            big[n] = tr(big[n])
        return [mt, big["wg1"], big["wu1"], big["wd1"], big["wg2"], big["wu2"], big["wd2"], g, b,
                big["win"], cw, big["wout"], big["kv"], fb, big["wq"], big["wo"]]

    grads_out = dict({n: buf["RED_" + n] for n in w3}, small=g_small[None])
    return (loss, grad_x, *order(lambda n: grads_out[n]), *order(lambda n: stepped[n][0]),
            *order(lambda n: stepped[n][1]), *order(lambda n: stepped[n][2]))
```

```python
import functools
import math

import jax
import jax.numpy as jnp
from jax import lax
from jax.experimental import pallas as pl
from jax.experimental.pallas import tpu as pltpu

f32 = jnp.float32
bf16 = jnp.bfloat16

N_META = 16
PAD = 112
HEAD_DIM = 128
DEPTH = 2
LN_EPS = 1e-5
ALPHA = (2 * DEPTH) ** 0.25
NEG_INF = -1e30
N_CHIPS = 4
SMALL_ROWS = 48
LANES = 128

ADAM_LR = 0.001
ADAM_B1 = 0.9
ADAM_B2 = 0.999
ADAM_EPS = 1e-08
ADAM_WD = 0.01
ADAM_STEP = 10

VMEM_LIMIT_BYTES = 56 * 1024 * 1024
ROWS_WIDE = 1664
ROWS_ACC = 1040
ROWS_ACC_LN = 832
STRIP = 16
MESH = pl.DeviceIdType.MESH

NT_DIMS = (((1,), (1,)), ((), ()))
TN_DIMS = (((0,), (0,)), ((), ()))


def _tile(n, target, mult):
    best = None
    for d in range(mult, min(n, target) + 1, mult):
        if n % d == 0:
            best = d
    assert best is not None, (n, target, mult)
    return best


def _params(*sem):
    return pltpu.CompilerParams(dimension_semantics=sem, vmem_limit_bytes=VMEM_LIMIT_BYTES)


class _Place:
    def __init__(self):
        self.cx, self.cy, self.c = lax.axis_index("x"), lax.axis_index("y"), lax.axis_index("c")
        self.chips = [(1 - self.cx, self.cy), (self.cx, 1 - self.cy), (1 - self.cx, 1 - self.cy)]
        self.me = 2 * self.cx + self.cy
        self.slots = [self.me] + [2 * px + py for px, py in self.chips]
        self.sib = (self.cx, self.cy, 1 - self.c)


class _Copies:
    def __init__(self):
        self.srcs, self.bufs, self.news = {}, {}, {}
        self.plans = []
        self.out_bufs, self.out_news = {}, {}

    def add(self, plan, n_copies, srcs=None, bufs=None, news=None):
        for have, more in ((self.srcs, srcs), (self.bufs, bufs), (self.news, news)):
            for key, val in (more or {}).items():
                assert key not in have or have[key] is val, key
                have[key] = val
        self.plans.append((plan, n_copies))

    def empty(self):
        return not self.plans

    def count(self):
        return sum(n for _, n in self.plans)

    def copies(self, src_refs, buf_refs, new_refs, send, recv):
        place = _Place()
        srcs = dict(zip(self.srcs, src_refs))
        bufs = dict(zip(self.bufs, buf_refs))
        news = dict(zip(self.news, new_refs))
        out = []
        for plan, n_copies in self.plans:
            triples = plan(srcs, bufs, news, place)
            assert len(triples) == n_copies
            for src, dst, dev in triples:
                n = len(out)
                out.append(pltpu.make_async_remote_copy(
                    src_ref=src, dst_ref=dst, send_sem=send.at[n], recv_sem=recv.at[n],
                    device_id=dev, device_id_type=MESH))
        return out

    def land(self, results):
        n_b = len(self.bufs)
        self.out_bufs = dict(zip(self.bufs, results[:n_b]))
        self.out_news = dict(zip(self.news, results[n_b:]))


def _pallas(comm, body, *, name, grid, in_specs, out_specs, out_shape, compiler_params,
            scratch_shapes=(), input_output_aliases=None):
    aliases = dict(input_output_aliases or {})
    if comm is None or comm.empty():
        return pl.pallas_call(body, name=name, grid=grid, in_specs=in_specs, out_specs=out_specs,
                              out_shape=out_shape, scratch_shapes=list(scratch_shapes),
                              input_output_aliases=aliases, compiler_params=compiler_params)
    single = not isinstance(out_shape, (list, tuple))
    out_shapes = [out_shape] if single else list(out_shape)
    out_specs_l = [out_specs] if single else list(out_specs)
    n_in, n_out, n_scr = len(in_specs), len(out_shapes), len(scratch_shapes)
    n_s, n_b, n_n = len(comm.srcs), len(comm.bufs), len(comm.news)
    n_copies = comm.count()

    def wrapped(*refs):
        ins = refs[:n_in]
        src_refs = refs[n_in:n_in + n_s]
        o0 = n_in + n_s + n_b
        outs = refs[o0:o0 + n_out]
        buf_refs = refs[o0 + n_out:o0 + n_out + n_b]
        new_refs = refs[o0 + n_out + n_b:o0 + n_out + n_b + n_n]
        rest = refs[o0 + n_out + n_b + n_n:]
        scratch, (send, recv) = rest[:n_scr], rest[n_scr:]
        ids = [pl.program_id(a) for a in range(len(grid))]
        first = functools.reduce(jnp.logical_and, [i == 0 for i in ids])
        last = functools.reduce(jnp.logical_and, [i == g - 1 for i, g in zip(ids, grid)])

        @pl.when(first)
        def _():
            for cp in comm.copies(src_refs, buf_refs, new_refs, send, recv):
                cp.start()

        body(*ins, *outs, *scratch)

        @pl.when(last)
        def _():
            for cp in comm.copies(src_refs, buf_refs, new_refs, send, recv):
                cp.wait()

    hbm = pl.BlockSpec(memory_space=pl.ANY)
    for j in range(n_b):
        aliases[n_in + n_s + j] = n_out + j
    call = pl.pallas_call(
        wrapped, name=name, grid=grid,
        in_specs=[*in_specs, *([hbm] * (n_s + n_b))],
        out_specs=[*out_specs_l, *([hbm] * (n_b + n_n))],
        out_shape=[*out_shapes,
                   *[jax.ShapeDtypeStruct(a.shape, a.dtype) for a in comm.bufs.values()],
                   *comm.news.values()],
        scratch_shapes=[*scratch_shapes, pltpu.SemaphoreType.DMA((n_copies,)),
                        pltpu.SemaphoreType.DMA((n_copies,))],
        input_output_aliases=aliases, compiler_params=compiler_params)

    def run(*args):
        res = call(*args, *comm.srcs.values(), *comm.bufs.values())
        comm.land(res[n_out:])
        return res[0] if single else res[:n_out]

    return run


def _copy_call(comm, name):
    n_s, n_b, n_n = len(comm.srcs), len(comm.bufs), len(comm.news)
    n_copies = comm.count()

    def body(*refs):
        src_refs = refs[:n_s]
        buf_refs = refs[n_s + n_b:n_s + 2 * n_b]
        new_refs = refs[n_s + 2 * n_b:n_s + 2 * n_b + n_n]
        send, recv = refs[n_s + 2 * n_b + n_n:]
        copies = comm.copies(src_refs, buf_refs, new_refs, send, recv)
        for cp in copies:
            cp.start()
        for cp in copies:
            cp.wait()

    hbm = pl.BlockSpec(memory_space=pl.ANY)
    res = pl.pallas_call(
        body, name=name,
        in_specs=[hbm] * (n_s + n_b), out_specs=[hbm] * (n_b + n_n),
        out_shape=[*[jax.ShapeDtypeStruct(a.shape, a.dtype) for a in comm.bufs.values()],
                   *comm.news.values()],
        input_output_aliases={n_s + j: j for j in range(n_b)},
        scratch_shapes=[pltpu.SemaphoreType.DMA((n_copies,)), pltpu.SemaphoreType.DMA((n_copies,))],
    )(*comm.srcs.values(), *comm.bufs.values())
    comm.land(res)


def _embed(meta_pad, x, name, comm=None):
    seq, d = x.shape
    t = seq + LANES

    def body(m_ref, x_ref, h_ref, hb_ref):
        first = pl.program_id(0) == 0
        v = jnp.where(first, m_ref[...], x_ref[...])
        h_ref[...] = v
        hb_ref[...] = v.astype(bf16)

    return _pallas(
        comm, body, name=name, grid=(t // LANES,),
        in_specs=[pl.BlockSpec((LANES, d), lambda i: (0, 0)),
                  pl.BlockSpec((LANES, d), lambda i: (jnp.maximum(i - 1, 0), 0))],
        out_specs=[pl.BlockSpec((LANES, d), lambda i: (i, 0)),
                   pl.BlockSpec((LANES, d), lambda i: (i, 0))],
        out_shape=[jax.ShapeDtypeStruct((t, d), f32), jax.ShapeDtypeStruct((t, d), bf16)],
        compiler_params=_params("parallel"),
    )(meta_pad, x)


def _nn_matmul(x, w, out_dtype, name, comm=None, out_scale=None):
    t, k = x.shape
    s_n, _, n = w.shape
    assert s_n == 1 or n % LANES == 0
    tm = _tile(t, ROWS_WIDE, 16)

    def body(x_ref, w_ref, o_ref):
        res = jnp.dot(x_ref[...].astype(bf16), w_ref[...], preferred_element_type=f32)
        if out_scale is not None:
            res = res * out_scale
        o_ref[...] = res.astype(o_ref.dtype)

    return _pallas(
        comm, body, name=name, grid=(s_n, t // tm),
        in_specs=[pl.BlockSpec((tm, k), lambda s, i: (i, 0)),
                  pl.BlockSpec((None, k, n), lambda s, i: (s, 0, 0))],
        out_specs=pl.BlockSpec((tm, n), lambda s, i: (i, s)),
        out_shape=jax.ShapeDtypeStruct((t, s_n * n), out_dtype),
        compiler_params=_params("parallel", "parallel"),
    )(x, w)


def _ffn_up(hb, wg, wu, layer, name, comm=None):
    t, d = hb.shape
    s_n, _, n, _ = wg.shape
    tm = _tile(t, ROWS_WIDE, 16)

    def body(x_ref, wg_ref, wu_ref, a_ref, b_ref, s_ref):
        x = x_ref[...]
        a = lax.dot_general(x, wg_ref[...], NT_DIMS, preferred_element_type=f32)
        b = lax.dot_general(x, wu_ref[...], NT_DIMS, preferred_element_type=f32)
        a_ref[...] = a.astype(bf16)
        b_ref[...] = b.astype(bf16)
        s_ref[...] = (a * jax.nn.sigmoid(a) * b).astype(bf16)

    wspec = pl.BlockSpec((None, None, n, d), lambda s, i: (s, layer, 0, 0))
    ospec = pl.BlockSpec((None, tm, n), lambda s, i: (s, i, 0))
    return _pallas(
        comm, body, name=name, grid=(s_n, t // tm),
        in_specs=[pl.BlockSpec((tm, d), lambda s, i: (i, 0)), wspec, wspec],
        out_specs=[ospec, ospec, ospec],
        out_shape=[jax.ShapeDtypeStruct((s_n, t, n), bf16)] * 3,
        compiler_params=_params("parallel", "parallel"),
    )(hb, wg, wu)


def _down_ln(x, w, layer, hprev, gain, bias, beta, name, comm=None):
    s_n, t, k = x.shape
    d = w.shape[-1]
    tm = _tile(t, ROWS_ACC, 16)

    def body(x_ref, w_ref, h_ref, g_ref, b_ref, r_out, h_out, hb_out, acc):
        s = pl.program_id(1)

        @pl.when(s == 0)
        def _():
            acc[...] = jnp.zeros_like(acc)

        acc[...] += jnp.dot(x_ref[...], w_ref[...], preferred_element_type=f32)

        @pl.when(s == s_n - 1)
        def _():
            r = ALPHA * h_ref[...] + beta * acc[...]
            mu = jnp.mean(r, axis=-1, keepdims=True)
            xc = r - mu
            var = jnp.mean(xc * xc, axis=-1, keepdims=True)
            y = xc * lax.rsqrt(var + LN_EPS) * g_ref[...] + b_ref[...]
            r_out[...] = r
            h_out[...] = y
            hb_out[...] = y.astype(bf16)

    row = pl.BlockSpec((tm, d), lambda i, s: (i, 0))
    vec = pl.BlockSpec((1, d), lambda i, s: (0, 0))
    return _pallas(
        comm, body, name=name, grid=(t // tm, s_n),
        in_specs=[pl.BlockSpec((None, tm, k), lambda i, s: (s, i, 0)),
                  pl.BlockSpec((None, None, k, d), lambda i, s: (s, layer, 0, 0)),
                  row, vec, vec],
        out_specs=[row, row, row],
        out_shape=[jax.ShapeDtypeStruct((t, d), f32), jax.ShapeDtypeStruct((t, d), f32),
                   jax.ShapeDtypeStruct((t, d), bf16)],
        scratch_shapes=[pltpu.VMEM((tm, d), f32)],
        compiler_params=_params("parallel", "arbitrary"),
    )(x, w, hprev, gain, bias)


def _conv_fwd(p, conv_w, name):
    t, d3 = p.shape
    d = d3 // 3
    tm = _tile(t, 320, 8)
    hb = tm // 8

    def body(p_ref, prev_ref, w_ref, z_ref):
        i = pl.program_id(0)
        rows = i * tm - 8 + lax.broadcasted_iota(jnp.int32, (tm + 8, 1), 0)
        cg = jnp.concatenate([prev_ref[:, d:2 * d], p_ref[:, d:2 * d]], axis=0)
        val = jnp.concatenate([prev_ref[:, 2 * d:], p_ref[:, 2 * d:]], axis=0)
        u = jnp.where(rows >= PAD, cg * val, 0.0)
        y = (w_ref[2:3, :] * u + w_ref[1:2, :] * pltpu.roll(u, 1, 0)
             + w_ref[0:1, :] * pltpu.roll(u, 2, 0))
        z_ref[...] = (p_ref[:, :d] * y[8:]).astype(bf16)

    return pl.pallas_call(
        body, name=name, grid=(t // tm,),
        in_specs=[pl.BlockSpec((tm, d3), lambda i: (i, 0)),
                  pl.BlockSpec((8, d3), lambda i: (jnp.maximum(i * hb - 1, 0), 0)),
                  pl.BlockSpec((3, d), lambda i: (0, 0))],
        out_specs=pl.BlockSpec((tm, d), lambda i: (i, 0)),
        out_shape=jax.ShapeDtypeStruct((t, d), bf16),
        compiler_params=_params("parallel"),
    )(p, p, conv_w)


def _row_scan(x, name, fbias=None, reverse=False):
    t, n = x.shape
    blk = _tile(t, 640, LANES)
    n_blk = t // blk
    gate = fbias is not None

    def body(*refs):
        if gate:
            x_ref, fb_ref, o_ref, carry = refs
        else:
            x_ref, o_ref, carry = refs
        i = pl.program_id(0)

        @pl.when(i == 0)
        def _():
            carry[...] = jnp.zeros_like(carry)

        v = x_ref[...]
        r = lax.broadcasted_iota(jnp.int32, (blk, n), 0)
        if gate:
            v = v + fb_ref[...]
            v = jnp.minimum(v, 0.0) - jnp.log1p(jnp.exp(-jnp.abs(v)))
            v = jnp.where(i * blk + r >= PAD, v, 0.0)
        sh = 1
        while sh < blk:
            if reverse:
                v = v + jnp.where(r < blk - sh, pltpu.roll(v, blk - sh, 0), 0.0)
            else:
                v = v + jnp.where(r >= sh, pltpu.roll(v, sh, 0), 0.0)
            sh *= 2
        v = v + carry[...]
        o_ref[...] = v
        carry[...] = o_ref[0:1, :] if reverse else o_ref[blk - 1:blk, :]

    order = (lambda i: (n_blk - 1 - i, 0)) if reverse else (lambda i: (i, 0))
    in_specs = [pl.BlockSpec((blk, n), order)]
    args = [x]
    if gate:
        in_specs.append(pl.BlockSpec((1, n), lambda i: (0, 0)))
        args.append(fbias)
    return pl.pallas_call(
        body, name=name, grid=(n_blk,),
        in_specs=in_specs,
        out_specs=pl.BlockSpec((blk, n), order),
        out_shape=jax.ShapeDtypeStruct((t, n), f32),
        scratch_shapes=[pltpu.VMEM((1, n), f32)],
        compiler_params=_params("arbitrary"),
    )(*args)


def _lanes(x, n):
    return jnp.concatenate([x] * (n // LANES), axis=1)


def _attn_fwd(q, k, v, cq_rep, ck_rows, name):
    t, d = q.shape
    n_heads = d // HEAD_DIM
    bk = ck_rows.shape[-1]
    bq = bk

    def lane_fold(x, op):
        out = x[:, :LANES]
        for c0 in range(LANES, bk, LANES):
            out = op(out, x[:, c0:c0 + LANES])
        return out

    def body(q_ref, k_ref, v_ref, cq_ref, ck_ref, o_ref, o32_ref, lse_ref,
             s_scr, p_scr, m_scr, l_scr, red_scr, acc_scr):
        i = pl.program_id(1)
        m_scr[...] = jnp.full_like(m_scr, NEG_INF)
        l_scr[...] = jnp.zeros_like(l_scr)
        acc_scr[...] = jnp.zeros_like(acc_scr)
        qb = q_ref[...]
        ahead = (lax.broadcasted_iota(jnp.int32, (STRIP, bk), 1)
                 - lax.broadcasted_iota(jnp.int32, (STRIP, bk), 0))

        def tile(j, diagonal):
            k0 = pl.multiple_of(j * bk, bk)
            s_scr[...] = lax.dot_general(qb, k_ref[pl.ds(k0, bk), :], NT_DIMS,
                                         preferred_element_type=f32)
            ck = ck_ref[j]
            for r in range(0, bq, STRIP):
                rows = slice(r, r + STRIP)
                s = s_scr[rows, :] + _lanes(cq_ref[rows, :], bk) - ck
                if diagonal:
                    s = jnp.where(ahead <= r, s, NEG_INF)
                s_scr[rows, :] = s
                red_scr[rows, :] = lane_fold(s, jnp.maximum)
            m_old = m_scr[...]
            m_new = jnp.maximum(m_old, jnp.broadcast_to(
                jnp.max(red_scr[...], axis=1, keepdims=True), (bq, LANES)))
            a = jnp.exp(m_old - m_new)
            m_scr[...] = m_new
            for r in range(0, bq, STRIP):
                rows = slice(r, r + STRIP)
                pr = jnp.exp(s_scr[rows, :] - _lanes(m_scr[rows, :], bk))
                red_scr[rows, :] = lane_fold(pr, jnp.add)
                p_scr[rows, :] = pr.astype(bf16)
            l_scr[...] = a * l_scr[...] + jnp.broadcast_to(
                jnp.sum(red_scr[...], axis=1, keepdims=True), (bq, LANES))
            acc_scr[...] = a * acc_scr[...] + jnp.dot(
                p_scr[...], v_ref[pl.ds(k0, bk), :], preferred_element_type=f32)

        def full_tile(j, carry):
            tile(j, False)
            return carry

        lax.fori_loop(0, i, full_tile, 0)
        tile(i, True)
        out = acc_scr[...] / l_scr[...]
        o_ref[...] = out.astype(bf16)
        o32_ref[...] = out
        lse_ref[...] = m_scr[...] + jnp.log(l_scr[...])

    qblk = pl.BlockSpec((bq, HEAD_DIM), lambda h, i: (i, h))
    head_rows = pl.BlockSpec((t, HEAD_DIM), lambda h, i: (0, h))
    rep = pl.BlockSpec((None, bq, LANES), lambda h, i: (h, i, 0))
    col = pltpu.VMEM((bq, LANES), f32)
    return pl.pallas_call(
        body, name=name, grid=(n_heads, t // bq),
        in_specs=[qblk, head_rows, head_rows, rep,
                  pl.BlockSpec((None, t // bk, 1, bk), lambda h, i: (h, 0, 0, 0))],
        out_specs=[qblk, qblk, rep],
        out_shape=[jax.ShapeDtypeStruct((t, d), bf16), jax.ShapeDtypeStruct((t, d), f32),
                   jax.ShapeDtypeStruct((n_heads, t, LANES), f32)],
        scratch_shapes=[pltpu.VMEM((bq, bk), f32), pltpu.VMEM((bq, bk), bf16), col, col, col,
                        pltpu.VMEM((bq, HEAD_DIM), f32)],
        compiler_params=_params("parallel", "parallel"),
    )(q, k, v, cq_rep, ck_rows)


def _loss_head(h, target, r, gain, name):
    t, d = h.shape

    def body(h_ref, t_ref, r_ref, g_ref, dr_ref, drb_ref, dg_ref, db_ref, loss_ref):
        i = pl.program_id(0)

        @pl.when(i == 0)
        def _():
            loss_ref[...] = jnp.zeros_like(loss_ref)
            dg_ref[...] = jnp.zeros_like(dg_ref)
            db_ref[...] = jnp.zeros_like(db_ref)

        diff = jnp.where(i >= 1, h_ref[...] - t_ref[...], 0.0)
        loss_ref[...] += jnp.sum(diff * diff)
        dr, dg, db = _ln_bwd_rows(diff * (1.0 / d), r_ref[...], g_ref[...])
        dr_ref[...] = dr
        drb_ref[...] = dr.astype(bf16)
        dg_ref[...] += dg
        db_ref[...] += db

    row = pl.BlockSpec((LANES, d), lambda i: (i, 0))
    vec = pl.BlockSpec((1, d), lambda i: (0, 0))
    return pl.pallas_call(
        body, name=name, grid=(t // LANES,),
        in_specs=[row, pl.BlockSpec((LANES, d), lambda i: (jnp.maximum(i - 1, 0), 0)), row, vec],
        out_specs=[row, row, vec, vec, pl.BlockSpec((1, LANES), lambda i: (0, 0))],
        out_shape=[jax.ShapeDtypeStruct((t, d), f32), jax.ShapeDtypeStruct((t, d), bf16),
                   jax.ShapeDtypeStruct((1, d), f32), jax.ShapeDtypeStruct((1, d), f32),
                   jax.ShapeDtypeStruct((1, LANES), f32)],
        compiler_params=_params("arbitrary"),
    )(h, target, r, gain)


def _ln_bwd_rows(dy, rr, gain):
    mu = jnp.mean(rr, axis=-1, keepdims=True)
    xc = rr - mu
    var = jnp.mean(xc * xc, axis=-1, keepdims=True)
    rstd = lax.rsqrt(var + LN_EPS)
    xhat = xc * rstd
    dxh = dy * gain
    m1 = jnp.mean(dxh, axis=-1, keepdims=True)
    m2 = jnp.mean(dxh * xhat, axis=-1, keepdims=True)
    dr = rstd * (dxh - m1 - xhat * m2)
    return dr, jnp.sum(dy * xhat, axis=0, keepdims=True), jnp.sum(dy, axis=0, keepdims=True)


def _ffn_bwd_act(drb, wd, layer, a, b, name, comm=None):
    t, d = drb.shape
    s_n, _, n = a.shape
    tm = _tile(t, ROWS_WIDE, 16)

    def body(dr_ref, w_ref, a_ref, b_ref, da_ref, db_ref):
        ds = 0.5 * lax.dot_general(dr_ref[...], w_ref[...], NT_DIMS, preferred_element_type=f32)
        da_ref[...], db_ref[...] = _swiglu_bwd(ds, a_ref, b_ref)

    act = pl.BlockSpec((None, tm, n), lambda s, i: (s, i, 0))
    return _pallas(
        comm, body, name=name, grid=(s_n, t // tm),
        in_specs=[pl.BlockSpec((tm, d), lambda s, i: (i, 0)),
                  pl.BlockSpec((None, None, n, d), lambda s, i: (s, layer, 0, 0)), act, act],
        out_specs=[act, act],
        out_shape=[jax.ShapeDtypeStruct((s_n, t, n), bf16), jax.ShapeDtypeStruct((s_n, t, n), bf16)],
        compiler_params=_params("parallel", "parallel"),
    )(drb, wd, a, b)


def _swiglu_bwd(ds, a_ref, b_ref):
    av = a_ref[...].astype(f32)
    sig = jax.nn.sigmoid(av)
    da = ds * b_ref[...].astype(f32) * (sig * (1.0 + av * (1.0 - sig)))
    return da.astype(bf16), (ds * (av * sig)).astype(bf16)


def _act_spec(mode, tt, k, t_first):
    def fix(fn):
        return (lambda i, s: fn(s, i)) if t_first else fn
    if mode == "shared":
        return pl.BlockSpec((tt, k), fix(lambda s, i: (i, 0)))
    if mode == "cols":
        return pl.BlockSpec((tt, k), fix(lambda s, i: (i, s)))
    assert mode == "stack"
    return pl.BlockSpec((None, tt, k), fix(lambda s, i: (s, i, 0)))


def _act_width(arr, mode, s_n):
    return arr.shape[-1] // s_n if mode == "cols" else arr.shape[-1]


def _tn_matmul(pairs, s_n, scale, name, comm=None):
    t = pairs[0][0].shape[-2]
    tt = _tile(t, 2080, 16)
    n_t = t // tt
    arrays, specs, where = [], [], []
    for x, xmode, y, ymode in pairs:
        pos = []
        for arr, mode in ((x, xmode), (y, ymode)):
            hit = [j for j, a in enumerate(arrays) if a is arr]
            if not hit:
                arrays.append(arr)
                specs.append(_act_spec(mode, tt, _act_width(arr, mode, s_n), False))
                hit = [len(arrays) - 1]
            pos.append(hit[0])
        where.append(pos)
    widths = [(_act_width(x, xm, s_n), _act_width(y, ym, s_n)) for x, xm, y, ym in pairs]
    n_a = len(arrays)

    def body(*refs):
        i = pl.program_id(1)
        for (px, py), o_ref in zip(where, refs[n_a:]):
            part = lax.dot_general(refs[px][...].astype(bf16), refs[py][...].astype(bf16), TN_DIMS,
                                   preferred_element_type=f32)

            @pl.when(i == 0)
            def _():
                o_ref[...] = part

            @pl.when(i > 0)
            def _():
                o_ref[...] += part

            if scale != 1.0:
                @pl.when(i == n_t - 1)
                def _():
                    o_ref[...] = o_ref[...] * scale

    return _pallas(
        comm, body, name=name, grid=(s_n, n_t),
        in_specs=specs,
        out_specs=[pl.BlockSpec((None, kx, ky), lambda s, i: (s, 0, 0)) for kx, ky in widths],
        out_shape=[jax.ShapeDtypeStruct((s_n, kx, ky), f32) for kx, ky in widths],
        compiler_params=_params("parallel", "arbitrary"),
    )(*arrays)


def _nt_sum(pairs, base, base_scale, s_n, out_dtype, name, comm=None, ln=None):
    t = pairs[0][0].shape[-2]
    pairs = [(*pr, False)[:5] for pr in pairs]
    d = pairs[0][2].shape[-1] if pairs[0][4] else pairs[0][2].shape[-2]
    tm = _tile(t, ROWS_ACC if ln is None else ROWS_ACC_LN, 16)
    n_p = len(pairs)
    has_base = base is not None
    flipped = [pr[4] for pr in pairs]
    n_out = 1 if ln is None else 4

    def body(*refs):
        dy_refs = refs[0:2 * n_p:2]
        w_refs = refs[1:2 * n_p:2]
        rest = refs[2 * n_p:]
        base_ref = rest[0] if has_base else None
        o_refs, acc = rest[-1 - n_out:-1], rest[-1]
        s = pl.program_id(1)

        if ln is not None:
            @pl.when(jnp.logical_and(pl.program_id(0) == 0, s == 0))
            def _():
                o_refs[2][...] = jnp.zeros_like(o_refs[2])
                o_refs[3][...] = jnp.zeros_like(o_refs[3])

        @pl.when(s == 0)
        def _():
            acc[...] = jnp.zeros_like(acc)

        tot = None
        for dy_ref, w_ref, flip in zip(dy_refs, w_refs, flipped):
            dyv = dy_ref[...].astype(bf16)
            if flip:
                part = jnp.dot(dyv, w_ref[...], preferred_element_type=f32)
            else:
                part = lax.dot_general(dyv, w_ref[...], NT_DIMS, preferred_element_type=f32)
            tot = part if tot is None else tot + part
        acc[...] += tot

        @pl.when(s == s_n - 1)
        def _():
            res = acc[...]
            if has_base:
                res = base_scale * base_ref[...] + res
            if ln is None:
                o_refs[0][...] = res.astype(o_refs[0].dtype)
            else:
                r_ref, g_ref = rest[-7], rest[-6]
                dr, dg, db = _ln_bwd_rows(res, r_ref[...], g_ref[...])
                o_refs[0][...] = dr
                o_refs[1][...] = dr.astype(bf16)
                o_refs[2][...] += dg
                o_refs[3][...] += db

    in_specs, args = [], []
    for dy, mode, w, layer, flip in pairs:
        k = _act_width(dy, mode, s_n)
        in_specs.append(_act_spec(mode, tm, k, True))
        wshape = (k, d) if flip else (d, k)
        if layer is None:
            in_specs.append(pl.BlockSpec((None, *wshape), lambda i, s: (s, 0, 0)))
        else:
            in_specs.append(pl.BlockSpec((None, None, *wshape),
                                         functools.partial(lambda i, s, l: (s, l, 0, 0), l=layer)))
        args += [dy, w]
    row = pl.BlockSpec((tm, d), lambda i, s: (i, 0))
    vec = pl.BlockSpec((1, d), lambda i, s: (0, 0))
    if has_base:
        in_specs.append(row)
        args.append(base)
    if ln is None:
        out_specs = row
        out_shape = jax.ShapeDtypeStruct((t, d), out_dtype)
    else:
        in_specs += [row, vec]
        args += list(ln)
        out_specs = [row, row, vec, vec]
        out_shape = [jax.ShapeDtypeStruct((t, d), f32), jax.ShapeDtypeStruct((t, d), bf16),
                     jax.ShapeDtypeStruct((1, d), f32), jax.ShapeDtypeStruct((1, d), f32)]
    return _pallas(
        comm, body, name=name, grid=(t // tm, s_n),
        in_specs=in_specs, out_specs=out_specs, out_shape=out_shape,
        scratch_shapes=[pltpu.VMEM((tm, d), f32)],
        compiler_params=_params("parallel" if ln is None else "arbitrary", "arbitrary"),
    )(*args)


def _conv_bwd(dz, p, conv_w, name, comm=None):
    t, d3 = p.shape
    d = d3 // 3
    tm = _tile(t, 320, 8)
    hb = tm // 8
    last8 = t // 8 - 1
    n_ext = tm + 8

    def body(dz_ref, dzn_ref, p_ref, pp_ref, pn_ref, w_ref, dp_ref, dw_ref):
        i = pl.program_id(0)

        @pl.when(i == 0)
        def _():
            dw_ref[...] = jnp.zeros_like(dw_ref)

        w0, w1, w2 = w_ref[0:1, :], w_ref[1:2, :], w_ref[2:3, :]
        rows_u = i * tm - 8 + lax.broadcasted_iota(jnp.int32, (n_ext, 1), 0)
        cg = jnp.concatenate([pp_ref[:, d:2 * d], p_ref[:, d:2 * d]], axis=0)
        val = jnp.concatenate([pp_ref[:, 2 * d:], p_ref[:, 2 * d:]], axis=0)
        u = jnp.where(rows_u >= PAD, cg * val, 0.0)
        u1 = pltpu.roll(u, 1, 0)
        u2 = pltpu.roll(u, 2, 0)
        y = (w2 * u + w1 * u1 + w0 * u2)[8:]
        dzv = dz_ref[...]
        bg = p_ref[:, :d]
        rows_n = (i + 1) * tm + lax.broadcasted_iota(jnp.int32, (8, 1), 0)
        dy_main = dzv * bg
        dy_next = jnp.where(rows_n < t, dzn_ref[...] * pn_ref[:, :d], 0.0)
        dye = jnp.concatenate([dy_main, dy_next], axis=0)
        du = (w2 * dye + w1 * pltpu.roll(dye, n_ext - 1, 0)
              + w0 * pltpu.roll(dye, n_ext - 2, 0))[:tm]
        du = jnp.where(rows_u[8:] >= PAD, du, 0.0)
        dp_ref[:, :d] = (dzv * y).astype(bf16)
        dp_ref[:, d:2 * d] = (du * val[8:]).astype(bf16)
        dp_ref[:, 2 * d:] = (du * cg[8:]).astype(bf16)
        dw_ref[0:1, :] += jnp.sum(dy_main * u2[8:], axis=0, keepdims=True)
        dw_ref[1:2, :] += jnp.sum(dy_main * u1[8:], axis=0, keepdims=True)
        dw_ref[2:3, :] += jnp.sum(dy_main * u[8:], axis=0, keepdims=True)

    nxt = lambda i: (jnp.minimum((i + 1) * hb, last8), 0)
    return _pallas(
        comm, body, name=name, grid=(t // tm,),
        in_specs=[pl.BlockSpec((tm, d), lambda i: (i, 0)),
                  pl.BlockSpec((8, d), nxt),
                  pl.BlockSpec((tm, d3), lambda i: (i, 0)),
                  pl.BlockSpec((8, d3), lambda i: (jnp.maximum(i * hb - 1, 0), 0)),
                  pl.BlockSpec((8, d3), nxt),
                  pl.BlockSpec((3, d), lambda i: (0, 0))],
        out_specs=[pl.BlockSpec((tm, d3), lambda i: (i, 0)),
                   pl.BlockSpec((3, d), lambda i: (0, 0))],
        out_shape=[jax.ShapeDtypeStruct((t, d3), bf16), jax.ShapeDtypeStruct((3, d), f32)],
        compiler_params=_params("arbitrary"),
    )(dz, dz, p, p, p, conv_w)


def _attn_stats(o, do, lse_rep, cq_rep, bq, name):
    t, d = o.shape
    n_heads = d // HEAD_DIM

    def body(o_ref, do_ref, lse_ref, cq_ref, cl_ref, delta_ref):
        for c0 in range(0, bq, LANES):
            rows = slice(c0, c0 + LANES)
            cl_ref[:, rows] = (cq_ref[rows, :] - lse_ref[rows, :]).T[0:1, :]
            prod = o_ref[rows, :] * do_ref[rows, :].astype(f32)
            delta_ref[:, rows] = jnp.sum(prod.T, axis=0, keepdims=True)

    qblk = pl.BlockSpec((bq, HEAD_DIM), lambda h, i: (i, h))
    rep = pl.BlockSpec((None, bq, LANES), lambda h, i: (h, i, 0))
    row = pl.BlockSpec((None, None, 1, bq), lambda h, i: (h, i, 0, 0))
    shp = jax.ShapeDtypeStruct((n_heads, t // bq, 1, bq), f32)
    return pl.pallas_call(
        body, name=name, grid=(n_heads, t // bq),
        in_specs=[qblk, qblk, rep, rep],
        out_specs=[row, row], out_shape=[shp, shp],
        compiler_params=_params("parallel", "parallel"),
    )(o, do, lse_rep, cq_rep)


def _attn_bwd(q, k, v, kt, do, ckey, cl_rows, delta_rows, name, comm=None):
    t, d = q.shape
    n_heads = d // HEAD_DIM
    bk = kt.shape[-1]
    bq = bk
    n_kv = t // bk
    n_q = t // bq
    scale = 1.0 / math.sqrt(HEAD_DIM)

    def body(q_ref, do_ref, cl_ref, dl_ref, k_ref, v_ref, kt_ref, ck_ref,
             dq_ref, dcq_ref, dk_ref, dv_ref, dck_ref,
             st_scr, dp_scr, p_scr, ds_scr, dqt, dk_acc, dv_acc, dck_acc):
        j = pl.program_id(1)

        @pl.when(j == 0)
        def _():
            dqt[...] = jnp.zeros_like(dqt)
            dcq_ref[...] = jnp.zeros_like(dcq_ref)

        dk_acc[...] = jnp.zeros_like(dk_acc)
        dv_acc[...] = jnp.zeros_like(dv_acc)
        dck_acc[...] = jnp.zeros_like(dck_acc)
        kb = k_ref[...]
        vb = v_ref[...]
        behind = (lax.broadcasted_iota(jnp.int32, (STRIP, bq), 1)
                  - lax.broadcasted_iota(jnp.int32, (STRIP, bq), 0))

        def tile(i, diagonal):
            r0 = pl.multiple_of(i * bq, bq)
            qi = q_ref[pl.ds(r0, bq), :]
            doi = do_ref[pl.ds(r0, bq), :]
            st_scr[...] = lax.dot_general(kb, qi, NT_DIMS, preferred_element_type=f32)
            dp_scr[...] = lax.dot_general(vb, doi, NT_DIMS, preferred_element_type=f32)
            cl = cl_ref[i]
            dl = dl_ref[i]
            over_keys = jnp.zeros((STRIP, bq), f32)
            for r in range(0, bk, STRIP):
                keys = slice(r, r + STRIP)
                st = st_scr[keys, :] + cl - _lanes(ck_ref[keys, :], bq)
                if diagonal:
                    st = jnp.where(behind >= r, st, NEG_INF)
                pr = jnp.exp(st)
                ds = pr * (dp_scr[keys, :] - dl)
                over_keys = over_keys + ds
                dck_acc[keys, :] -= jnp.sum(ds, axis=1, keepdims=True)
                p_scr[keys, :] = pr.astype(bf16)
                ds_scr[keys, :] = ds.astype(bf16)
            dcq_ref[i] += jnp.sum(over_keys, axis=0, keepdims=True)
            dv_acc[...] += jnp.dot(p_scr[...], doi, preferred_element_type=f32)
            dk_acc[...] += jnp.dot(ds_scr[...], qi, preferred_element_type=f32)
            dqt[i] += jnp.dot(kt_ref[...], ds_scr[...], preferred_element_type=f32)

        def full_tile(i, carry):
            tile(i, False)
            return carry

        tile(j, True)
        lax.fori_loop(j + 1, n_q, full_tile, 0)
        dk_ref[...] = dk_acc[...].astype(bf16)
        dv_ref[...] = dv_acc[...].astype(bf16)
        for c0 in range(0, bk, LANES):
            keys = slice(c0, c0 + LANES)
            dck_ref[:, keys] = jnp.broadcast_to(dck_acc[keys, :], (LANES, LANES)).T[0:1, :]

        @pl.when(j == n_kv - 1)
        def _():
            def emit(i, carry):
                r0 = pl.multiple_of(i * bq, bq)
                dq_ref[pl.ds(r0, bq), :] = dqt[i].T * scale
                return carry
            lax.fori_loop(0, n_q, emit, 0)

    head_rows = pl.BlockSpec((t, HEAD_DIM), lambda h, j: (0, h))
    head_stat = pl.BlockSpec((None, n_q, 1, bq), lambda h, j: (h, 0, 0, 0))
    kblk = pl.BlockSpec((bk, HEAD_DIM), lambda h, j: (j, h))
    return _pallas(
        comm, body, name=name, grid=(n_heads, n_kv),
        in_specs=[head_rows, head_rows, head_stat, head_stat, kblk, kblk,
                  pl.BlockSpec((None, None, HEAD_DIM, bk), lambda h, j: (h, j, 0, 0)),
                  pl.BlockSpec((None, bk, LANES), lambda h, j: (h, j, 0))],
        out_specs=[head_rows, head_stat, kblk, kblk,
                   pl.BlockSpec((None, None, 1, bk), lambda h, j: (h, j, 0, 0))],
        out_shape=[jax.ShapeDtypeStruct((t, d), f32),
                   jax.ShapeDtypeStruct((n_heads, n_q, 1, bq), f32),
                   jax.ShapeDtypeStruct((t, d), bf16), jax.ShapeDtypeStruct((t, d), bf16),
                   jax.ShapeDtypeStruct((n_heads, n_kv, 1, bk), f32)],
        scratch_shapes=[pltpu.VMEM((bk, bq), f32), pltpu.VMEM((bk, bq), f32),
                        pltpu.VMEM((bk, bq), bf16), pltpu.VMEM((bk, bq), bf16),
                        pltpu.VMEM((n_q, HEAD_DIM, bq), f32),
                        pltpu.VMEM((bk, HEAD_DIM), f32), pltpu.VMEM((bk, HEAD_DIM), f32),
                        pltpu.VMEM((bk, 1), f32)],
        compiler_params=_params("parallel", "arbitrary"),
    )(q, do, cl_rows, delta_rows, k, v, kt, ckey)


def _fgate_bwd(dlogf, flog, fbias, name):
    t, n = flog.shape
    rows = _tile(t, 640, LANES)

    def body(dl_ref, fl_ref, fb_ref, o_ref, sum_ref):
        i = pl.program_id(0)

        @pl.when(i == 0)
        def _():
            sum_ref[...] = jnp.zeros_like(sum_ref)

        r = i * rows + lax.broadcasted_iota(jnp.int32, (rows, n), 0)
        g = dl_ref[...] * jax.nn.sigmoid(-(fl_ref[...] + fb_ref[...]))
        g = jnp.where(r >= PAD, g, 0.0)
        o_ref[...] = g
        sum_ref[...] += jnp.sum(g, axis=0, keepdims=True)

    blk = pl.BlockSpec((rows, n), lambda i: (i, 0))
    vec = pl.BlockSpec((1, n), lambda i: (0, 0))
    return pl.pallas_call(
        body, name=name, grid=(t // rows,),
        in_specs=[blk, blk, vec], out_specs=[blk, vec],
        out_shape=[jax.ShapeDtypeStruct((t, n), f32), jax.ShapeDtypeStruct((1, n), f32)],
        compiler_params=_params("arbitrary"),
    )(dlogf, flog, fbias)


def _place_shard(w, idx, name):
    n_l, r, c_n = w.shape
    out_dtype = bf16 if r * c_n > 2 ** 16 else w.dtype
    tr = _tile(r, 512, 16) if r % 16 == 0 else r

    def body(idx_ref, w_ref, o_ref):
        o_ref[...] = w_ref[...].astype(out_dtype)

    grid_spec = pltpu.PrefetchScalarGridSpec(
        num_scalar_prefetch=1, grid=(n_l, r // tr),
        in_specs=[pl.BlockSpec((None, tr, c_n), lambda l, i, idx: (l, i, 0))],
        out_specs=pl.BlockSpec((None, None, tr, c_n), lambda l, i, idx: (idx[0], l, i, 0)))
    return pl.pallas_call(
        body, name=name, grid_spec=grid_spec,
        out_shape=jax.ShapeDtypeStruct((N_CHIPS, n_l, r, c_n), out_dtype),
        compiler_params=_params("parallel", "parallel"),
    )(idx, w)


def _plan_gather_ici(items):
    def plan(srcs, bufs, news, p):
        out = []
        for name, layer, r2 in items:
            mine = bufs[name].at[p.me, layer, pl.ds(p.c * r2, r2)]
            out += [(mine, mine, (*chip, p.c)) for chip in p.chips]
        return out
    return plan, 3 * len(items)


def _plan_gather_d2d(items):
    def plan(srcs, bufs, news, p):
        out = []
        for name, layer, r2 in items:
            for px, py in p.chips:
                landed = bufs[name].at[2 * px + py, layer, pl.ds(p.c * r2, r2)]
                out.append((landed, landed, p.sib))
        return out
    return plan, 3 * len(items)


def _plan_pair_exchange(names, r2s):
    def plan(srcs, bufs, news, p):
        out = []
        for name, r2 in zip(names, r2s):
            for k, slot in enumerate(p.slots):
                out.append((srcs["G_" + name].at[slot, pl.ds((1 - p.c) * r2, r2)],
                            news["PAIR_" + name].at[k], p.sib))
        return out
    return plan, 4 * len(names)


def _plan_chip_exchange(names):
    def plan(srcs, bufs, news, p):
        out = []
        for name in names:
            for k, chip in enumerate(p.chips):
                out.append((srcs["SEND_" + name].at[k], news["RECV_" + name].at[k], (*chip, p.c)))
        return out
    return plan, 3 * len(names)


def _plan_pair_share(items):
    def plan(srcs, bufs, news, p):
        out = []
        for name, layer, r2 in items:
            mine = bufs["RED_" + name].at[layer, pl.ds(p.c * r2, r2)]
            out.append((mine, mine, p.sib))
        return out
    return plan, len(items)


def _rs_prepare(gs, pairs, idx, name):
    n = len(gs)
    _, r2, c_n = pairs[0].shape
    tr = _tile(r2, 256, 8)
    nb = r2 // tr

    def body(idx_ref, *refs):
        for g_ref, p_ref, o_ref in zip(refs[:n], refs[n:2 * n], refs[2 * n:]):
            o_ref[...] = (g_ref[...] + p_ref[...]).astype(bf16)

    g_spec = pl.BlockSpec((None, tr, c_n), lambda k, i, idx: (idx[k + 1], idx[4] * nb + i, 0))
    p_spec = pl.BlockSpec((None, tr, c_n), lambda k, i, idx: (k + 1, i, 0))
    grid_spec = pltpu.PrefetchScalarGridSpec(
        num_scalar_prefetch=1, grid=(3, nb), in_specs=[g_spec] * n + [p_spec] * n,
        out_specs=[pl.BlockSpec((None, tr, c_n), lambda k, i, idx: (k, i, 0))] * n)
    return pl.pallas_call(
        body, name=name, grid_spec=grid_spec,
        out_shape=[jax.ShapeDtypeStruct((3, r2, c_n), bf16)] * n,
        compiler_params=_params("parallel", "parallel"),
    )(idx, *gs, *pairs)


def _rs_finish(gs, pairs, recvs, idx, layer, n_layers, intos, name):
    n = len(gs)
    _, r2, c_n = pairs[0].shape
    tr = _tile(r2, 256, 8)
    nb = r2 // tr

    def body(idx_ref, *refs):
        for j in range(n):
            g_ref, p_ref, r0_ref, r1_ref, r2_ref = refs[5 * j:5 * j + 5]
            acc = g_ref[...] + p_ref[...]
            acc = acc + r0_ref[...].astype(f32)
            acc = acc + r1_ref[...].astype(f32)
            acc = acc + r2_ref[...].astype(f32)
            refs[len(refs) - n + j][...] = acc

    def rspec(k):
        return pl.BlockSpec((None, tr, c_n), functools.partial(lambda i, idx, kk: (kk, i, 0), kk=k))

    item_specs = [pl.BlockSpec((None, tr, c_n), lambda i, idx: (idx[0], idx[4] * nb + i, 0)),
                  rspec(0), rspec(0), rspec(1), rspec(2)]
    in_specs = item_specs * n
    args = [idx]
    for g, pair, recv in zip(gs, pairs, recvs):
        args += [g, pair, recv, recv, recv]
    aliases = {}
    if intos is not None:
        in_specs = in_specs + [pl.BlockSpec(memory_space=pl.ANY)] * n
        args += list(intos)
        aliases = {1 + 5 * n + j: j for j in range(n)}
    grid_spec = pltpu.PrefetchScalarGridSpec(
        num_scalar_prefetch=1, grid=(nb,), in_specs=in_specs,
        out_specs=[pl.BlockSpec((None, tr, c_n), lambda i, idx: (layer, idx[4] * nb + i, 0))] * n)
    return pl.pallas_call(
        body, name=name, grid_spec=grid_spec,
        out_shape=[jax.ShapeDtypeStruct((n_layers, 2 * r2, c_n), f32)] * n,
        input_output_aliases=aliases,
        compiler_params=_params("parallel"),
    )(*args)


def _adamw(ws, gs, ms, vs, layer, intos, name):
    n = len(ws)
    n_l, r, c_n = ws[0].shape
    tr = _tile(r, 256, 8)

    def body(*refs):
        for j in range(n):
            w_ref, g_ref, m_ref, v_ref = refs[4 * j:4 * j + 4]
            o0 = len(refs) - 4 * n + 4 * j
            d_ref, mo_ref, vo_ref, go_ref = refs[o0:o0 + 4]
            gv = g_ref[...]
            go_ref[...] = gv
            mn = ADAM_B1 * m_ref[...] + (1.0 - ADAM_B1) * gv
            vn = ADAM_B2 * v_ref[...] + (1.0 - ADAM_B2) * (gv * gv)
            m_hat = mn / (1.0 - ADAM_B1 ** ADAM_STEP)
            v_hat = vn / (1.0 - ADAM_B2 ** ADAM_STEP)
            d_ref[...] = -ADAM_LR * (m_hat / (jnp.sqrt(v_hat) + ADAM_EPS) + ADAM_WD * w_ref[...])
            mo_ref[...] = mn
            vo_ref[...] = vn

    blk = pl.BlockSpec((None, tr, c_n), lambda i: (layer, i, 0))
    shp = jax.ShapeDtypeStruct((n_l, r, c_n), f32)
    in_specs = [blk] * (4 * n)
    args = []
    for w, g, m, v in zip(ws, gs, ms, vs):
        args += [w, g, m, v]
    aliases = {}
    if intos is not None:
        in_specs = in_specs + [pl.BlockSpec(memory_space=pl.ANY)] * (4 * n)
        for into in intos:
            args += list(into)
        aliases = {4 * n + j: j for j in range(4 * n)}
    res = pl.pallas_call(
        body, name=name, grid=(r // tr,),
        in_specs=in_specs, out_specs=[blk] * (4 * n), out_shape=[shp] * (4 * n),
        input_output_aliases=aliases,
        compiler_params=_params("parallel"),
    )(*args)
    return [tuple(res[4 * j:4 * j + 4]) for j in range(n)]


def kernel(x, meta, ffn1_wg, ffn1_wu, ffn1_wd, ffn2_wg, ffn2_wu, ffn2_wd, ln_gain, ln_bias, conv_w_in, conv_w, conv_w_out, kv_w, f_bias, attn_w_q, attn_w_o, loss_target, m_meta, m_ffn1_wg, m_ffn1_wu, m_ffn1_wd, m_ffn2_wg, m_ffn2_wu, m_ffn2_wd, m_ln_gain, m_ln_bias, m_conv_w_in, m_conv_w, m_conv_w_out, m_kv_w, m_f_bias, m_attn_w_q, m_attn_w_o, v_meta, v_ffn1_wg, v_ffn1_wu, v_ffn1_wd, v_ffn2_wg, v_ffn2_wu, v_ffn2_wd, v_ln_gain, v_ln_bias, v_conv_w_in, v_conv_w, v_conv_w_out, v_kv_w, v_f_bias, v_attn_w_q, v_attn_w_o):
    seq, d = x.shape[1], x.shape[2]
    t = PAD + N_META + seq
    n_heads = d // HEAD_DIM
    dq = d // N_CHIPS
    n_kv = kv_w.shape[1]
    x2 = x[0]
    target = loss_target[0]

    def rows8(a):
        return jnp.pad(a, ((0, 8 - a.shape[0]), (0, 0)))

    def small_pack(mt, g, b, cw, fb):
        fb_row = jnp.pad(fb, (0, mt.shape[1] - n_heads))[None]
        return jnp.concatenate([mt, rows8(g.reshape(6, -1)), rows8(b.reshape(6, -1)),
                                rows8(cw.reshape(3, -1)), rows8(fb_row)], axis=0)

    w_small = small_pack(meta, ln_gain, ln_bias, conv_w, f_bias)

    cx, cy, c = lax.axis_index("x"), lax.axis_index("y"), lax.axis_index("c")
    idx = jnp.stack([2 * cx + cy, 2 * (1 - cx) + cy, 2 * cx + (1 - cy), 2 * (1 - cx) + (1 - cy), c]
                    ).astype(jnp.int32)

    def tr(a):
        return a.transpose(0, 2, 1)

    w3 = {"wg1": tr(ffn1_wg), "wu1": tr(ffn1_wu), "wd1": ffn1_wd, "wg2": tr(ffn2_wg),
          "wu2": tr(ffn2_wu), "wd2": ffn2_wd, "win": conv_w_in, "wout": conv_w_out, "kv": kv_w[None],
          "wq": attn_w_q, "wo": attn_w_o, "small": w_small[None]}
    transposed = ("wg1", "wu1", "wg2", "wu2")
    buf = {n: _place_shard(w, idx, "place_shard") for n, w in w3.items()}

    def split(item):
        name, layer = item.split(".")
        return name, int(layer)

    def gather_stage(planner, items):
        triples = [(n, l, buf[n].shape[2] // 2) for n, l in map(split, items)]
        plan, n_copies = planner(triples)
        return dict(plan=plan, n=n_copies, bufs={n: buf[n] for n, _, _ in triples})

    def ici(items):
        return gather_stage(_plan_gather_ici, items)

    def d2d(items):
        return gather_stage(_plan_gather_d2d, items)

    def pair_exchange(items):
        r2s = [buf["G_" + it].shape[1] // 2 for it in items]
        plan, n_copies = _plan_pair_exchange(items, r2s)
        news = {"PAIR_" + it: jax.ShapeDtypeStruct((N_CHIPS, r2, buf["G_" + it].shape[2]), f32)
                for it, r2 in zip(items, r2s)}
        return dict(plan=plan, n=n_copies, srcs={"G_" + it: buf["G_" + it] for it in items}, news=news)

    def chip_exchange(items):
        plan, n_copies = _plan_chip_exchange(items)
        srcs = {"SEND_" + it: buf["SEND_" + it] for it in items}
        news = {"RECV_" + it: jax.ShapeDtypeStruct(s.shape, s.dtype)
                for it, s in ((it, buf["SEND_" + it]) for it in items)}
        return dict(plan=plan, n=n_copies, srcs=srcs, news=news)

    def pair_share(items):
        triples = [(n, l, buf["RED_" + n].shape[1] // 2) for n, l in map(split, items)]
        plan, n_copies = _plan_pair_share(triples)
        return dict(plan=plan, n=n_copies, bufs={"RED_" + n: buf["RED_" + n] for n, _, _ in triples})

    def run(fn, *args, stages=(), name=None, **kw):
        comm = _Copies()
        for st in (stages() if callable(stages) else stages):
            comm.add(st["plan"], st["n"], srcs=st.get("srcs"), bufs=st.get("bufs"), news=st.get("news"))
        out = _copy_call(comm, name) if fn is None else fn(*args, comm=comm, **kw)
        buf.update(comm.out_bufs)
        buf.update(comm.out_news)
        return out

    first = ["wg1.0", "wu1.0", "small.0"]
    run(None, stages=[ici(first)], name="gather_first_ici")
    run(None, stages=[d2d(first)], name="gather_first_d2d")
    small = buf["small"].reshape(N_CHIPS, SMALL_ROWS, dq).transpose(1, 0, 2).reshape(SMALL_ROWS, d)
    meta_full = small[:N_META]
    gains = small[16:22].reshape(DEPTH, 3, 1, d)
    biases = small[24:30].reshape(DEPTH, 3, 1, d)
    conv_w_full = small[32:35]
    fb_pad = jnp.pad(f_bias, (0, LANES - n_heads))[None]
    down1 = ["wd1.0"]
    ffn2_l0 = ["win.0", "wout.0", "wg2.0", "wu2.0", "wd2.0", "kv.0"]
    attn_ffn2_l1 = ["wq.0", "wo.0", "wg2.1", "wu2.1", "wd2.1"]
    ffn1_l1 = ["wg1.1", "wu1.1", "wd1.1"]

    meta_pad = jnp.concatenate([jnp.zeros((PAD, d), f32), meta_full], axis=0)
    h0, h0b = run(_embed, meta_pad, x2, "embed", stages=[ici(down1)])
    a1, b1, s1 = run(_ffn_up, h0b, buf["wg1"], buf["wu1"], 0, "ffn_up",
                     stages=lambda: [d2d(down1), ici(ffn2_l0)])
    r1, h1, h1b = run(_down_ln, s1, buf["wd1"], 0, h0, gains[0, 0], biases[0, 0], 0.5, "ffn_down_ln",
                      stages=lambda: [d2d(ffn2_l0), ici(attn_ffn2_l1)])
    n_in = conv_w_in.shape[-1]
    w_in = buf["win"].reshape(N_CHIPS, d, n_in)
    w_out = buf["wout"].reshape(1, 1, d, d)
    p = run(_nn_matmul, h1b, w_in, f32, "conv_in", stages=lambda: [d2d(attn_ffn2_l1), ici(ffn1_l1)])
    z = _conv_fwd(p, conv_w_full, "conv_fwd")
    r2, h2, h2b = run(_down_ln, z[None], w_out, 0, h1, gains[0, 1], biases[0, 1], 1.0, "mix_out_ln",
                      stages=lambda: [d2d(ffn1_l1)])
    wg1, wu1, wd1, wg2, wu2, wd2 = (buf[n] for n in ("wg1", "wu1", "wd1", "wg2", "wu2", "wd2"))
    w_q = buf["wq"].reshape(1, d, d)
    w_o = buf["wo"].reshape(1, 1, d, d)
    kv_full = buf["kv"].reshape(N_CHIPS, d, n_kv).transpose(1, 0, 2).reshape(d, N_CHIPS * n_kv)
    w_k = kv_full[:, :d][None]
    w_v = kv_full[:, d:2 * d][None]
    w_f = jnp.pad(kv_full[:, 2 * d:], ((0, 0), (0, LANES - n_heads)))[None]
    a2, b2, s2 = _ffn_up(h2b, wg2, wu2, 0, "ffn_up")
    r3, h3, h3b = _down_ln(s2, wd2, 0, h2, gains[0, 2], biases[0, 2], 0.5, "ffn_down_ln")
    kk = _nn_matmul(h3b, w_k, bf16, "proj_bf16")
    vv = _nn_matmul(h3b, w_v, bf16, "proj_bf16")
    flog = _nn_matmul(h3b, w_f, f32, "proj_gate")
    cum = _row_scan(flog, "gate_cumsum", fb_pad)
    bk = _tile(t, 640, LANES)
    c_ht = cum[:, :n_heads].T
    c_keys = jnp.where(jnp.arange(t)[None, :] < PAD, 1e30, c_ht)
    cq_rep = jnp.broadcast_to(c_ht[:, :, None], (n_heads, t, LANES))
    ck_rep = jnp.broadcast_to(c_keys[:, :, None], (n_heads, t, LANES))
    ck_rows = c_keys.reshape(n_heads, t // bk, 1, bk)
    a3, b3, s3 = _ffn_up(h3b, wg1, wu1, 1, "ffn_up")
    r4, h4, h4b = _down_ln(s3, wd1, 1, h3, gains[1, 0], biases[1, 0], 0.5, "ffn_down_ln")
    q = _nn_matmul(h4b, w_q, bf16, "proj_q", out_scale=1.0 / math.sqrt(HEAD_DIM))
    o, o32, lse_rep = _attn_fwd(q, kk, vv, cq_rep, ck_rows, "attn_fwd")
    r5, h5, h5b = _down_ln(o[None], w_o, 0, h4, gains[1, 1], biases[1, 1], 1.0, "mix_out_ln")
    a4, b4, s4 = _ffn_up(h5b, wg2, wu2, 1, "ffn_up")
    r6, h6, _ = _down_ln(s4, wd2, 1, h5, gains[1, 2], biases[1, 2], 0.5, "ffn_down_ln")
    dr6, dr6b, dg12, db12, sq = _loss_head(h6, target, r6, gains[1, 2], "loss_head")
    loss_part = 0.5 * sq[0, 0] / d

    m_small = small_pack(m_meta, m_ln_gain, m_ln_bias, m_conv_w, m_f_bias)
    v_small = small_pack(v_meta, v_ln_gain, v_ln_bias, v_conv_w, v_f_bias)
    m3 = {"wg1": tr(m_ffn1_wg), "wu1": tr(m_ffn1_wu), "wd1": m_ffn1_wd, "wg2": tr(m_ffn2_wg),
          "wu2": tr(m_ffn2_wu), "wd2": m_ffn2_wd, "win": m_conv_w_in, "wout": m_conv_w_out,
          "kv": m_kv_w[None], "wq": m_attn_w_q, "wo": m_attn_w_o, "small": m_small[None]}
    v3 = {"wg1": tr(v_ffn1_wg), "wu1": tr(v_ffn1_wu), "wd1": v_ffn1_wd, "wg2": tr(v_ffn2_wg),
          "wu2": tr(v_ffn2_wu), "wd2": v_ffn2_wd, "win": v_conv_w_in, "wout": v_conv_w_out,
          "kv": v_kv_w[None], "wq": v_attn_w_q, "wo": v_attn_w_o, "small": v_small[None]}
    stepped = {}

    def alike(items):
        groups = {}
        for it in items:
            n, l = split(it)
            groups.setdefault((w3[n].shape, l, ("RED_" + n) in buf, n in stepped), []).append(it)
        return groups.values()

    def prepare(items):
        for group in alike(items):
            sends = _rs_prepare([buf["G_" + it] for it in group], [buf["PAIR_" + it] for it in group],
                                idx, "grad_prepare")
            buf.update({"SEND_" + it: s for it, s in zip(group, sends)})

    def finish(items):
        for group in alike(items):
            names = [split(it)[0] for it in group]
            layer = split(group[0])[1]
            intos = [buf["RED_" + n] for n in names] if ("RED_" + names[0]) in buf else None
            reds = _rs_finish([buf["G_" + it] for it in group], [buf["PAIR_" + it] for it in group],
                              [buf["RECV_" + it] for it in group], idx, layer, w3[names[0]].shape[0],
                              intos, "grad_finish")
            buf.update({"RED_" + n: r for n, r in zip(names, reds)})

    def adam(items, grads=None):
        for group in alike(items):
            names = [split(it)[0] for it in group]
            layer = split(group[0])[1]
            gs = [buf["RED_" + n] if grads is None else grads[n] for n in names]
            intos = [stepped[n] for n in names] if names[0] in stepped else None
            res = _adamw([w3[n] for n in names], gs, [m3[n] for n in names], [v3[n] for n in names],
                         layer, intos, "adamw")
            stepped.update(dict(zip(names, res)))

    def ffn_bwd(dr, drb, hb_in, a, b, s, f, layer, on_act=(), after_act=None, on_dwd=(),
                after_dwd=None, on_dx=(), ln=None):
        da, db = run(_ffn_bwd_act, drb, buf["wd" + f], layer, a, b, "ffn_bwd_act", stages=on_act)
        if after_act is not None:
            after_act()
        (buf[f"G_wd{f}.{layer}"],) = run(_tn_matmul, [(s, "stack", drb, "shared")], N_CHIPS, 0.5,
                                         "ffn_dwd", stages=on_dwd)
        if after_dwd is not None:
            after_dwd()
        buf[f"G_wg{f}.{layer}"], buf[f"G_wu{f}.{layer}"] = _tn_matmul(
            [(da, "stack", hb_in, "shared"), (db, "stack", hb_in, "shared")], N_CHIPS, 1.0, "ffn_dwgu")
        return run(_nt_sum, [(da, "stack", buf["wg" + f], layer, True),
                             (db, "stack", buf["wu" + f], layer, True)],
                   dr, ALPHA, N_CHIPS, f32, "ffn_dx", stages=on_dx, ln=ln)

    ffn2_1 = ["wg2.1", "wu2.1", "wd2.1"]
    ffn1_1 = ["wg1.1", "wu1.1", "wd1.1"]
    ffn2_0 = ["wg2.0", "wu2.0", "wd2.0"]
    conv_items = ["wout.0", "win.0"]

    dr5, dr5b, dg11, db11 = ffn_bwd(dr6, dr6b, h5b, a4, b4, s4, "2", 1,
                                    on_dx=lambda: [pair_exchange(ffn2_1)], ln=(r5, gains[1, 1]))
    prepare(ffn2_1)
    (dwo,) = _tn_matmul([(o, "shared", dr5b, "shared")], 1, 1.0, "sq_dw")
    buf["G_wo.0"] = dwo.reshape(N_CHIPS, dq, d)
    do = run(_nt_sum, [(dr5b, "cols", w_o[0], None)], None, 1.0, 1, bf16, "sq_dx_bf16",
             stages=lambda: [pair_exchange(["wo.0"])])
    prepare(["wo.0"])
    cl, delta = _attn_stats(o32, do, lse_rep, cq_rep, bk, "attn_stats")
    kt = kk.reshape(t // bk, bk, n_heads, HEAD_DIM).transpose(2, 0, 3, 1)
    dq_att, dc_q, dk, dv, dc_k = run(_attn_bwd, q, kk, vv, kt, do, ck_rep, cl, delta, "attn_bwd",
                                     stages=lambda: [chip_exchange(ffn2_1 + ["wo.0"])])
    finish(ffn2_1 + ["wo.0"])
    dc = (dc_q + dc_k).reshape(n_heads, t)
    (dwq,) = run(_tn_matmul, [(h4b, "shared", dq_att, "shared")], 1, 1.0, "sq_dw",
                 stages=lambda: [pair_share(ffn2_1 + ["wo.0"])])
    buf["G_wq.0"] = dwq.reshape(N_CHIPS, dq, d)
    adam(ffn2_1 + ["wo.0"])
    dr4, dr4b, dg10, db10 = run(_nt_sum, [(dq_att, "cols", w_q, None)], dr5, ALPHA, 1, f32,
                                "sq_dx_res", stages=lambda: [pair_exchange(["wq.0"])],
                                ln=(r4, gains[1, 0]))
    prepare(["wq.0"])
    dh3a = ffn_bwd(dr4, dr4b, h3b, a3, b3, s3, "1", 1,
                   on_act=lambda: [chip_exchange(["wq.0"])],
                   on_dx=lambda: [pair_exchange(ffn1_1)])
    prepare(ffn1_1)
    finish(["wq.0"])
    dc_t = jnp.pad(dc.T, ((0, 0), (0, LANES - n_heads)))
    dlogf = _row_scan(dc_t, "rev_cumsum", reverse=True)
    dfl, dfb_cols = _fgate_bwd(dlogf, flog, fb_pad, "gate_bwd")
    dwk, dwv, dwf = run(_tn_matmul, [(h3b, "shared", g, "shared") for g in (dk, dv, dfl)], 1, 1.0,
                        "kv_dw",
                        stages=lambda: [chip_exchange(ffn1_1), pair_share(["wq.0"])])
    adam(["wq.0"])

    def by_chip(full):
        rows = full.shape[0]
        return full.reshape(rows, N_CHIPS, full.shape[1] // N_CHIPS).transpose(1, 0, 2)

    buf["G_kv.0"] = by_chip(jnp.concatenate([dwk[0], dwv[0], dwf[0][:, :n_heads]], axis=1))
    dr3, dr3b, dg02, db02 = run(
        _nt_sum, [(dk, "cols", w_k, None), (dv, "cols", w_v, None), (dfl, "cols", w_f, None)],
        dh3a, 1.0, 1, f32, "kv_dx", stages=lambda: [pair_exchange(["kv.0"])], ln=(r3, gains[0, 2]))
    prepare(["kv.0"])
    finish(ffn1_1)
    dr2, dr2b, dg01, db01 = ffn_bwd(dr3, dr3b, h2b, a2, b2, s2, "2", 0,
                                    on_act=lambda: [chip_exchange(["kv.0"]), pair_share(ffn1_1)],
                                    after_act=lambda: (adam(ffn1_1), finish(["kv.0"])),
                                    on_dwd=lambda: [pair_share(["kv.0"])],
                                    after_dwd=lambda: adam(["kv.0"]),
                                    on_dx=lambda: [pair_exchange(ffn2_0)], ln=(r2, gains[0, 1]))
    prepare(ffn2_0)
    (dwout,) = _tn_matmul([(z, "shared", dr2b, "shared")], 1, 1.0, "sq_dw")
    buf["G_wout.0"] = dwout.reshape(N_CHIPS, dq, d)
    dz = _nt_sum([(dr2b, "cols", w_out[0], None)], None, 1.0, 1, f32, "sq_dx_f32")
    dp, dconv_w = run(_conv_bwd, dz, p, conv_w_full, "conv_bwd",
                      stages=lambda: [chip_exchange(ffn2_0)])
    (buf["G_win.0"],) = _tn_matmul([(h1b, "shared", dp, "cols")], N_CHIPS, 1.0, "conv_dwin")
    finish(ffn2_0)
    dr1, dr1b, dg00, db00 = run(_nt_sum, [(dp, "cols", w_in, None)], dr2, ALPHA, N_CHIPS, f32,
                                "conv_dx",
                                stages=lambda: [pair_exchange(conv_items), pair_share(ffn2_0)],
                                ln=(r1, gains[0, 0]))
    prepare(conv_items)
    adam(ffn2_0)
    gate_up = ["wg1.0", "wu1.0"]
    da, db = run(_ffn_bwd_act, dr1b, buf["wd1"], 0, a1, b1, "ffn_bwd_act",
                 stages=lambda: [chip_exchange(conv_items)])
    finish(conv_items)
    buf["G_wg1.0"], buf["G_wu1.0"] = run(
        _tn_matmul, [(da, "stack", h0b, "shared"), (db, "stack", h0b, "shared")], N_CHIPS, 1.0,
        "ffn_dwgu", stages=lambda: [pair_share(conv_items)])
    adam(conv_items)
    (buf["G_wd1.0"],) = run(_tn_matmul, [(s1, "stack", dr1b, "shared")], N_CHIPS, 0.5, "ffn_dwd",
                            stages=lambda: [pair_exchange(gate_up)])
    prepare(gate_up)
    dh0 = run(_nt_sum, [(da, "stack", buf["wg1"], 0, True), (db, "stack", buf["wu1"], 0, True)],
              dr1, ALPHA, N_CHIPS, f32, "ffn_dx",
              stages=lambda: [chip_exchange(gate_up), pair_exchange(["wd1.0"])])
    prepare(["wd1.0"])
    finish(gate_up)
    grad_x = dh0[PAD + N_META:][None]
    dmeta = dh0[PAD:PAD + N_META]
    buf["G_small.0"] = by_chip(jnp.concatenate(
        [dmeta, rows8(jnp.concatenate([dg00, dg01, dg02, dg10, dg11, dg12], axis=0)),
         rows8(jnp.concatenate([db00, db01, db02, db10, db11, db12], axis=0)),
         rows8(dconv_w), jnp.zeros((8, d), f32)], axis=0))
    run(None, stages=lambda: [chip_exchange(["wd1.0"]), pair_exchange(["small.0"]), pair_share(gate_up)],
        name="grad_tail_1")
    prepare(["small.0"])
    finish(["wd1.0"])
    adam(gate_up)
    run(None, stages=lambda: [chip_exchange(["small.0"]), pair_share(["wd1.0"])], name="grad_tail_2")
    finish(["small.0"])
    adam(["wd1.0"])
    run(None, stages=lambda: [pair_share(["small.0"])], name="grad_tail_3")

    tail = jnp.zeros((LANES,), f32).at[:n_heads].set(dfb_cols[0, :n_heads]).at[n_heads].set(loss_part)
    tail = lax.psum(tail, ("x", "y", "c"))
    loss = tail[n_heads]
    g_fb = tail[:n_heads]
    g_small = jnp.concatenate([buf["RED_small"][0, :40],
                               rows8(jnp.pad(g_fb, (0, dq - n_heads))[None])], axis=0)
    adam(["small.0"], grads={"small": g_small[None]})

    def unpack(pk):
        return (pk[:16], pk[16:22].reshape(DEPTH, 3, dq), pk[24:30].reshape(DEPTH, 3, dq),
                pk[32:35].reshape(1, 3, dq), pk[40, :n_heads])

    def order(pick):
        mt, g, b, cw, fb = unpack(pick("small")[0])
        big = {n: pick(n) for n in w3 if n != "small"}
        big["kv"] = big["kv"][0]
        for n in transposed:
            big[n] = tr(big[n])
        return [mt, big["wg1"], big["wu1"], big["wd1"], big["wg2"], big["wu2"], big["wd2"], g, b,
                big["win"], cw, big["wout"], big["kv"], fb, big["wq"], big["wo"]]

    return (loss, grad_x, *order(lambda n: stepped[n][3]), *order(lambda n: stepped[n][0]),
            *order(lambda n: stepped[n][1]), *order(lambda n: stepped[n][2]))
```

```python
import functools
import math

import jax
import jax.numpy as jnp
from jax import lax
from jax.experimental import pallas as pl
from jax.experimental.pallas import tpu as pltpu

f32 = jnp.float32
bf16 = jnp.bfloat16

N_META = 16
PAD = 112
HEAD_DIM = 128
DEPTH = 2
LN_EPS = 1e-5
ALPHA = (2 * DEPTH) ** 0.25
NEG_INF = -1e30
N_CHIPS = 4
SMALL_ROWS = 48
LANES = 128

ADAM_LR = 0.001
ADAM_B1 = 0.9
ADAM_B2 = 0.999
ADAM_EPS = 1e-08
ADAM_WD = 0.01
ADAM_STEP = 10

VMEM_LIMIT_BYTES = 56 * 1024 * 1024
ROWS_WIDE = 1664
ROWS_ACC = 1040
ROWS_ACC_LN = 832
STRIP = 16
MESH = pl.DeviceIdType.MESH

NT_DIMS = (((1,), (1,)), ((), ()))
TN_DIMS = (((0,), (0,)), ((), ()))


def _tile(n, target, mult):
    best = None
    for d in range(mult, min(n, target) + 1, mult):
        if n % d == 0:
            best = d
    assert best is not None, (n, target, mult)
    return best


def _params(*sem):
    return pltpu.CompilerParams(dimension_semantics=sem, vmem_limit_bytes=VMEM_LIMIT_BYTES)


class _Place:
    def __init__(self):
        self.cx, self.cy, self.c = lax.axis_index("x"), lax.axis_index("y"), lax.axis_index("c")
        self.chips = [(1 - self.cx, self.cy), (self.cx, 1 - self.cy), (1 - self.cx, 1 - self.cy)]
        self.me = 2 * self.cx + self.cy
        self.slots = [self.me] + [2 * px + py for px, py in self.chips]
        self.sib = (self.cx, self.cy, 1 - self.c)


class _Copies:
    def __init__(self):
        self.srcs, self.bufs, self.news = {}, {}, {}
        self.plans = []
        self.out_bufs, self.out_news = {}, {}

    def add(self, plan, n_copies, srcs=None, bufs=None, news=None):
        for have, more in ((self.srcs, srcs), (self.bufs, bufs), (self.news, news)):
            for key, val in (more or {}).items():
                assert key not in have or have[key] is val, key
                have[key] = val
        self.plans.append((plan, n_copies))

    def empty(self):
        return not self.plans

    def count(self):
        return sum(n for _, n in self.plans)

    def copies(self, src_refs, buf_refs, new_refs, send, recv):
        place = _Place()
        srcs = dict(zip(self.srcs, src_refs))
        bufs = dict(zip(self.bufs, buf_refs))
        news = dict(zip(self.news, new_refs))
        out = []
        for plan, n_copies in self.plans:
            triples = plan(srcs, bufs, news, place)
            assert len(triples) == n_copies
            for src, dst, dev in triples:
                n = len(out)
                out.append(pltpu.make_async_remote_copy(
                    src_ref=src, dst_ref=dst, send_sem=send.at[n], recv_sem=recv.at[n],
                    device_id=dev, device_id_type=MESH))
        return out

    def land(self, results):
        n_b = len(self.bufs)
        self.out_bufs = dict(zip(self.bufs, results[:n_b]))
        self.out_news = dict(zip(self.news, results[n_b:]))


def _pallas(comm, body, *, name, grid, in_specs, out_specs, out_shape, compiler_params,
            scratch_shapes=(), input_output_aliases=None):
    aliases = dict(input_output_aliases or {})
    if comm is None or comm.empty():
        return pl.pallas_call(body, name=name, grid=grid, in_specs=in_specs, out_specs=out_specs,
                              out_shape=out_shape, scratch_shapes=list(scratch_shapes),
                              input_output_aliases=aliases, compiler_params=compiler_params)
    single = not isinstance(out_shape, (list, tuple))
    out_shapes = [out_shape] if single else list(out_shape)
    out_specs_l = [out_specs] if single else list(out_specs)
    n_in, n_out, n_scr = len(in_specs), len(out_shapes), len(scratch_shapes)
    n_s, n_b, n_n = len(comm.srcs), len(comm.bufs), len(comm.news)
    n_copies = comm.count()

    def wrapped(*refs):
        ins = refs[:n_in]
        src_refs = refs[n_in:n_in + n_s]
        o0 = n_in + n_s + n_b
        outs = refs[o0:o0 + n_out]
        buf_refs = refs[o0 + n_out:o0 + n_out + n_b]
        new_refs = refs[o0 + n_out + n_b:o0 + n_out + n_b + n_n]
        rest = refs[o0 + n_out + n_b + n_n:]
        scratch, (send, recv) = rest[:n_scr], rest[n_scr:]
        ids = [pl.program_id(a) for a in range(len(grid))]
        first = functools.reduce(jnp.logical_and, [i == 0 for i in ids])
        last = functools.reduce(jnp.logical_and, [i == g - 1 for i, g in zip(ids, grid)])

        @pl.when(first)
        def _():
            for cp in comm.copies(src_refs, buf_refs, new_refs, send, recv):
                cp.start()

        body(*ins, *outs, *scratch)

        @pl.when(last)
        def _():
            for cp in comm.copies(src_refs, buf_refs, new_refs, send, recv):
                cp.wait()

    hbm = pl.BlockSpec(memory_space=pl.ANY)
    for j in range(n_b):
        aliases[n_in + n_s + j] = n_out + j
    call = pl.pallas_call(
        wrapped, name=name, grid=grid,
        in_specs=[*in_specs, *([hbm] * (n_s + n_b))],
        out_specs=[*out_specs_l, *([hbm] * (n_b + n_n))],
        out_shape=[*out_shapes,
                   *[jax.ShapeDtypeStruct(a.shape, a.dtype) for a in comm.bufs.values()],
                   *comm.news.values()],
        scratch_shapes=[*scratch_shapes, pltpu.SemaphoreType.DMA((n_copies,)),
                        pltpu.SemaphoreType.DMA((n_copies,))],
        input_output_aliases=aliases, compiler_params=compiler_params)

    def run(*args):
        res = call(*args, *comm.srcs.values(), *comm.bufs.values())
        comm.land(res[n_out:])
        return res[0] if single else res[:n_out]

    return run


def _copy_call(comm, name):
    n_s, n_b, n_n = len(comm.srcs), len(comm.bufs), len(comm.news)
    n_copies = comm.count()

    def body(*refs):
        src_refs = refs[:n_s]
        buf_refs = refs[n_s + n_b:n_s + 2 * n_b]
        new_refs = refs[n_s + 2 * n_b:n_s + 2 * n_b + n_n]
        send, recv = refs[n_s + 2 * n_b + n_n:]
        copies = comm.copies(src_refs, buf_refs, new_refs, send, recv)
        for cp in copies:
            cp.start()
        for cp in copies:
            cp.wait()

    hbm = pl.BlockSpec(memory_space=pl.ANY)
    res = pl.pallas_call(
        body, name=name,
        in_specs=[hbm] * (n_s + n_b), out_specs=[hbm] * (n_b + n_n),
        out_shape=[*[jax.ShapeDtypeStruct(a.shape, a.dtype) for a in comm.bufs.values()],
                   *comm.news.values()],
        input_output_aliases={n_s + j: j for j in range(n_b)},
        scratch_shapes=[pltpu.SemaphoreType.DMA((n_copies,)), pltpu.SemaphoreType.DMA((n_copies,))],
    )(*comm.srcs.values(), *comm.bufs.values())
    comm.land(res)


def _embed(meta_pad, x, name, comm=None):
    seq, d = x.shape
    t = seq + LANES

    def body(m_ref, x_ref, h_ref, hb_ref):
        first = pl.program_id(0) == 0
        v = jnp.where(first, m_ref[...], x_ref[...])
        h_ref[...] = v
        hb_ref[...] = v.astype(bf16)

    return _pallas(
        comm, body, name=name, grid=(t // LANES,),
        in_specs=[pl.BlockSpec((LANES, d), lambda i: (0, 0)),
                  pl.BlockSpec((LANES, d), lambda i: (jnp.maximum(i - 1, 0), 0))],
        out_specs=[pl.BlockSpec((LANES, d), lambda i: (i, 0)),
                   pl.BlockSpec((LANES, d), lambda i: (i, 0))],
        out_shape=[jax.ShapeDtypeStruct((t, d), f32), jax.ShapeDtypeStruct((t, d), bf16)],
        compiler_params=_params("parallel"),
    )(meta_pad, x)


def _nn_matmul(x, w, out_dtype, name, comm=None, out_scale=None):
    t, k = x.shape
    s_n, _, n = w.shape
    assert s_n == 1 or n % LANES == 0
    tm = _tile(t, ROWS_WIDE, 16)

    def body(x_ref, w_ref, o_ref):
        res = jnp.dot(x_ref[...].astype(bf16), w_ref[...], preferred_element_type=f32)
        if out_scale is not None:
            res = res * out_scale
        o_ref[...] = res.astype(o_ref.dtype)

    return _pallas(
        comm, body, name=name, grid=(s_n, t // tm),
        in_specs=[pl.BlockSpec((tm, k), lambda s, i: (i, 0)),
                  pl.BlockSpec((None, k, n), lambda s, i: (s, 0, 0))],
        out_specs=pl.BlockSpec((tm, n), lambda s, i: (i, s)),
        out_shape=jax.ShapeDtypeStruct((t, s_n * n), out_dtype),
        compiler_params=_params("parallel", "parallel"),
    )(x, w)


def _ffn_up(hb, wg, wu, layer, name, comm=None):
    t, d = hb.shape
    s_n, _, n, _ = wg.shape
    tm = _tile(t, ROWS_WIDE, 16)

    def body(x_ref, wg_ref, wu_ref, a_ref, b_ref, s_ref):
        x = x_ref[...]
        a = lax.dot_general(x, wg_ref[...], NT_DIMS, preferred_element_type=f32)
        b = lax.dot_general(x, wu_ref[...], NT_DIMS, preferred_element_type=f32)
        a_ref[...] = a.astype(bf16)
        b_ref[...] = b.astype(bf16)
        s_ref[...] = (a * jax.nn.sigmoid(a) * b).astype(bf16)

    wspec = pl.BlockSpec((None, None, n, d), lambda s, i: (s, layer, 0, 0))
    ospec = pl.BlockSpec((None, tm, n), lambda s, i: (s, i, 0))
    return _pallas(
        comm, body, name=name, grid=(s_n, t // tm),
        in_specs=[pl.BlockSpec((tm, d), lambda s, i: (i, 0)), wspec, wspec],
        out_specs=[ospec, ospec, ospec],
        out_shape=[jax.ShapeDtypeStruct((s_n, t, n), bf16)] * 3,
        compiler_params=_params("parallel", "parallel"),
    )(hb, wg, wu)


def _down_ln(x, w, layer, hprev, gain, bias, beta, name, comm=None):
    s_n, t, k = x.shape
    d = w.shape[-1]
    tm = _tile(t, ROWS_ACC, 16)

    def body(x_ref, w_ref, h_ref, g_ref, b_ref, r_out, h_out, hb_out, acc):
        s = pl.program_id(1)

        @pl.when(s == 0)
        def _():
            acc[...] = jnp.zeros_like(acc)

        acc[...] += jnp.dot(x_ref[...], w_ref[...], preferred_element_type=f32)

        @pl.when(s == s_n - 1)
        def _():
            r = ALPHA * h_ref[...] + beta * acc[...]
            mu = jnp.mean(r, axis=-1, keepdims=True)
            xc = r - mu
            var = jnp.mean(xc * xc, axis=-1, keepdims=True)
            y = xc * lax.rsqrt(var + LN_EPS) * g_ref[...] + b_ref[...]
            r_out[...] = r
            h_out[...] = y
            hb_out[...] = y.astype(bf16)

    row = pl.BlockSpec((tm, d), lambda i, s: (i, 0))
    vec = pl.BlockSpec((1, d), lambda i, s: (0, 0))
    return _pallas(
        comm, body, name=name, grid=(t // tm, s_n),
        in_specs=[pl.BlockSpec((None, tm, k), lambda i, s: (s, i, 0)),
                  pl.BlockSpec((None, None, k, d), lambda i, s: (s, layer, 0, 0)),
                  row, vec, vec],
        out_specs=[row, row, row],
        out_shape=[jax.ShapeDtypeStruct((t, d), f32), jax.ShapeDtypeStruct((t, d), f32),
                   jax.ShapeDtypeStruct((t, d), bf16)],
        scratch_shapes=[pltpu.VMEM((tm, d), f32)],
        compiler_params=_params("parallel", "arbitrary"),
    )(x, w, hprev, gain, bias)


def _conv_fwd(p, conv_w, name):
    t, d3 = p.shape
    d = d3 // 3
    tm = _tile(t, 320, 8)
    hb = tm // 8

    def body(p_ref, prev_ref, w_ref, z_ref):
        i = pl.program_id(0)
        rows = i * tm - 8 + lax.broadcasted_iota(jnp.int32, (tm + 8, 1), 0)
        cg = jnp.concatenate([prev_ref[:, d:2 * d], p_ref[:, d:2 * d]], axis=0)
        val = jnp.concatenate([prev_ref[:, 2 * d:], p_ref[:, 2 * d:]], axis=0)
        u = jnp.where(rows >= PAD, cg * val, 0.0)
        y = (w_ref[2:3, :] * u + w_ref[1:2, :] * pltpu.roll(u, 1, 0)
             + w_ref[0:1, :] * pltpu.roll(u, 2, 0))
        z_ref[...] = (p_ref[:, :d] * y[8:]).astype(bf16)

    return pl.pallas_call(
        body, name=name, grid=(t // tm,),
        in_specs=[pl.BlockSpec((tm, d3), lambda i: (i, 0)),
                  pl.BlockSpec((8, d3), lambda i: (jnp.maximum(i * hb - 1, 0), 0)),
                  pl.BlockSpec((3, d), lambda i: (0, 0))],
        out_specs=pl.BlockSpec((tm, d), lambda i: (i, 0)),
        out_shape=jax.ShapeDtypeStruct((t, d), bf16),
        compiler_params=_params("parallel"),
    )(p, p, conv_w)


def _row_scan(x, name, fbias=None, reverse=False):
    t, n = x.shape
    blk = _tile(t, 640, LANES)
    n_blk = t // blk
    gate = fbias is not None

    def body(*refs):
        if gate:
            x_ref, fb_ref, o_ref, carry = refs
        else:
            x_ref, o_ref, carry = refs
        i = pl.program_id(0)

        @pl.when(i == 0)
        def _():
            carry[...] = jnp.zeros_like(carry)

        v = x_ref[...]
        r = lax.broadcasted_iota(jnp.int32, (blk, n), 0)
        if gate:
            v = v + fb_ref[...]
            v = jnp.minimum(v, 0.0) - jnp.log1p(jnp.exp(-jnp.abs(v)))
            v = jnp.where(i * blk + r >= PAD, v, 0.0)
        sh = 1
        while sh < blk:
            if reverse:
                v = v + jnp.where(r < blk - sh, pltpu.roll(v, blk - sh, 0), 0.0)
            else:
                v = v + jnp.where(r >= sh, pltpu.roll(v, sh, 0), 0.0)
            sh *= 2
        v = v + carry[...]
        o_ref[...] = v
        carry[...] = o_ref[0:1, :] if reverse else o_ref[blk - 1:blk, :]

    order = (lambda i: (n_blk - 1 - i, 0)) if reverse else (lambda i: (i, 0))
    in_specs = [pl.BlockSpec((blk, n), order)]
    args = [x]
    if gate:
        in_specs.append(pl.BlockSpec((1, n), lambda i: (0, 0)))
        args.append(fbias)
    return pl.pallas_call(
        body, name=name, grid=(n_blk,),
        in_specs=in_specs,
        out_specs=pl.BlockSpec((blk, n), order),
        out_shape=jax.ShapeDtypeStruct((t, n), f32),
        scratch_shapes=[pltpu.VMEM((1, n), f32)],
        compiler_params=_params("arbitrary"),
    )(*args)


def _lanes(x, n):
    return jnp.concatenate([x] * (n // LANES), axis=1)


def _attn_fwd(q, k, v, cq_rep, ck_rows, name):
    t, d = q.shape
    n_heads = d // HEAD_DIM
    bk = ck_rows.shape[-1]
    bq = bk

    def lane_fold(x, op):
        out = x[:, :LANES]
        for c0 in range(LANES, bk, LANES):
            out = op(out, x[:, c0:c0 + LANES])
        return out

    def body(q_ref, k_ref, v_ref, cq_ref, ck_ref, o_ref, o32_ref, cl_ref,
             s_scr, p_scr, m_scr, l_scr, red_scr, acc_scr):
        i = pl.program_id(1)
        m_scr[...] = jnp.full_like(m_scr, NEG_INF)
        l_scr[...] = jnp.zeros_like(l_scr)
        acc_scr[...] = jnp.zeros_like(acc_scr)
        qb = q_ref[...]
        ahead = (lax.broadcasted_iota(jnp.int32, (STRIP, bk), 1)
                 - lax.broadcasted_iota(jnp.int32, (STRIP, bk), 0))

        def tile(j, diagonal):
            k0 = pl.multiple_of(j * bk, bk)
            s_scr[...] = lax.dot_general(qb, k_ref[pl.ds(k0, bk), :], NT_DIMS,
                                         preferred_element_type=f32)
            ck = ck_ref[j]
            for r in range(0, bq, STRIP):
                rows = slice(r, r + STRIP)
                s = s_scr[rows, :] + _lanes(cq_ref[rows, :], bk) - ck
                if diagonal:
                    s = jnp.where(ahead <= r, s, NEG_INF)
                s_scr[rows, :] = s
                red_scr[rows, :] = lane_fold(s, jnp.maximum)
            m_old = m_scr[...]
            m_new = jnp.maximum(m_old, jnp.broadcast_to(
                jnp.max(red_scr[...], axis=1, keepdims=True), (bq, LANES)))
            a = jnp.exp(m_old - m_new)
            m_scr[...] = m_new
            for r in range(0, bq, STRIP):
                rows = slice(r, r + STRIP)
                pr = jnp.exp(s_scr[rows, :] - _lanes(m_scr[rows, :], bk))
                red_scr[rows, :] = lane_fold(pr, jnp.add)
                p_scr[rows, :] = pr.astype(bf16)
            l_scr[...] = a * l_scr[...] + jnp.broadcast_to(
                jnp.sum(red_scr[...], axis=1, keepdims=True), (bq, LANES))
            acc_scr[...] = a * acc_scr[...] + jnp.dot(
                p_scr[...], v_ref[pl.ds(k0, bk), :], preferred_element_type=f32)

        def full_tile(j, carry):
            tile(j, False)
            return carry

        lax.fori_loop(0, i, full_tile, 0)
        tile(i, True)
        out = acc_scr[...] / l_scr[...]
        o_ref[...] = out.astype(bf16)
        o32_ref[...] = out
        red_scr[...] = cq_ref[...] - (m_scr[...] + jnp.log(l_scr[...]))
        for c0 in range(0, bq, LANES):
            cl_ref[:, c0:c0 + LANES] = red_scr[c0:c0 + LANES, :].T[0:1, :]

    qblk = pl.BlockSpec((bq, HEAD_DIM), lambda h, i: (i, h))
    head_rows = pl.BlockSpec((t, HEAD_DIM), lambda h, i: (0, h))
    rep = pl.BlockSpec((None, bq, LANES), lambda h, i: (h, i, 0))
    col = pltpu.VMEM((bq, LANES), f32)
    return pl.pallas_call(
        body, name=name, grid=(n_heads, t // bq),
        in_specs=[qblk, head_rows, head_rows, rep,
                  pl.BlockSpec((None, t // bk, 1, bk), lambda h, i: (h, 0, 0, 0))],
        out_specs=[qblk, qblk, pl.BlockSpec((None, None, 1, bq), lambda h, i: (h, i, 0, 0))],
        out_shape=[jax.ShapeDtypeStruct((t, d), bf16), jax.ShapeDtypeStruct((t, d), f32),
                   jax.ShapeDtypeStruct((n_heads, t // bq, 1, bq), f32)],
        scratch_shapes=[pltpu.VMEM((bq, bk), f32), pltpu.VMEM((bq, bk), bf16), col, col, col,
                        pltpu.VMEM((bq, HEAD_DIM), f32)],
        compiler_params=_params("parallel", "parallel"),
    )(q, k, v, cq_rep, ck_rows)


def _loss_head(h, target, r, gain, name):
    t, d = h.shape

    def body(h_ref, t_ref, r_ref, g_ref, dr_ref, drb_ref, dg_ref, db_ref, loss_ref):
        i = pl.program_id(0)

        @pl.when(i == 0)
        def _():
            loss_ref[...] = jnp.zeros_like(loss_ref)
            dg_ref[...] = jnp.zeros_like(dg_ref)
            db_ref[...] = jnp.zeros_like(db_ref)

        diff = jnp.where(i >= 1, h_ref[...] - t_ref[...], 0.0)
        loss_ref[...] += jnp.sum(diff * diff)
        dr, dg, db = _ln_bwd_rows(diff * (1.0 / d), r_ref[...], g_ref[...])
        dr_ref[...] = dr
        drb_ref[...] = dr.astype(bf16)
        dg_ref[...] += dg
        db_ref[...] += db

    row = pl.BlockSpec((LANES, d), lambda i: (i, 0))
    vec = pl.BlockSpec((1, d), lambda i: (0, 0))
    return pl.pallas_call(
        body, name=name, grid=(t // LANES,),
        in_specs=[row, pl.BlockSpec((LANES, d), lambda i: (jnp.maximum(i - 1, 0), 0)), row, vec],
        out_specs=[row, row, vec, vec, pl.BlockSpec((1, LANES), lambda i: (0, 0))],
        out_shape=[jax.ShapeDtypeStruct((t, d), f32), jax.ShapeDtypeStruct((t, d), bf16),
                   jax.ShapeDtypeStruct((1, d), f32), jax.ShapeDtypeStruct((1, d), f32),
                   jax.ShapeDtypeStruct((1, LANES), f32)],
        compiler_params=_params("arbitrary"),
    )(h, target, r, gain)


def _ln_bwd_rows(dy, rr, gain):
    mu = jnp.mean(rr, axis=-1, keepdims=True)
    xc = rr - mu
    var = jnp.mean(xc * xc, axis=-1, keepdims=True)
    rstd = lax.rsqrt(var + LN_EPS)
    xhat = xc * rstd
    dxh = dy * gain
    m1 = jnp.mean(dxh, axis=-1, keepdims=True)
    m2 = jnp.mean(dxh * xhat, axis=-1, keepdims=True)
    dr = rstd * (dxh - m1 - xhat * m2)
    return dr, jnp.sum(dy * xhat, axis=0, keepdims=True), jnp.sum(dy, axis=0, keepdims=True)


def _ffn_bwd_act(drb, wd, layer, a, b, name, comm=None):
    t, d = drb.shape
    s_n, _, n = a.shape
    tm = _tile(t, ROWS_WIDE, 16)

    def body(dr_ref, w_ref, a_ref, b_ref, da_ref, db_ref):
        ds = 0.5 * lax.dot_general(dr_ref[...], w_ref[...], NT_DIMS, preferred_element_type=f32)
        da_ref[...], db_ref[...] = _swiglu_bwd(ds, a_ref, b_ref)

    act = pl.BlockSpec((None, tm, n), lambda s, i: (s, i, 0))
    return _pallas(
        comm, body, name=name, grid=(s_n, t // tm),
        in_specs=[pl.BlockSpec((tm, d), lambda s, i: (i, 0)),
                  pl.BlockSpec((None, None, n, d), lambda s, i: (s, layer, 0, 0)), act, act],
        out_specs=[act, act],
        out_shape=[jax.ShapeDtypeStruct((s_n, t, n), bf16), jax.ShapeDtypeStruct((s_n, t, n), bf16)],
        compiler_params=_params("parallel", "parallel"),
    )(drb, wd, a, b)


def _swiglu_bwd(ds, a_ref, b_ref):
    av = a_ref[...].astype(f32)
    sig = jax.nn.sigmoid(av)
    da = ds * b_ref[...].astype(f32) * (sig * (1.0 + av * (1.0 - sig)))
    return da.astype(bf16), (ds * (av * sig)).astype(bf16)


def _act_spec(mode, tt, k, t_first):
    def fix(fn):
        return (lambda i, s: fn(s, i)) if t_first else fn
    if mode == "shared":
        return pl.BlockSpec((tt, k), fix(lambda s, i: (i, 0)))
    if mode == "cols":
        return pl.BlockSpec((tt, k), fix(lambda s, i: (i, s)))
    assert mode == "stack"
    return pl.BlockSpec((None, tt, k), fix(lambda s, i: (s, i, 0)))


def _act_width(arr, mode, s_n):
    return arr.shape[-1] // s_n if mode == "cols" else arr.shape[-1]


def _tn_matmul(pairs, s_n, scale, name, comm=None):
    t = pairs[0][0].shape[-2]
    tt = _tile(t, 2080, 16)
    n_t = t // tt
    arrays, specs, where = [], [], []
    for x, xmode, y, ymode in pairs:
        pos = []
        for arr, mode in ((x, xmode), (y, ymode)):
            hit = [j for j, a in enumerate(arrays) if a is arr]
            if not hit:
                arrays.append(arr)
                specs.append(_act_spec(mode, tt, _act_width(arr, mode, s_n), False))
                hit = [len(arrays) - 1]
            pos.append(hit[0])
        where.append(pos)
    widths = [(_act_width(x, xm, s_n), _act_width(y, ym, s_n)) for x, xm, y, ym in pairs]
    n_a = len(arrays)

    def body(*refs):
        i = pl.program_id(1)
        for (px, py), o_ref in zip(where, refs[n_a:]):
            part = lax.dot_general(refs[px][...].astype(bf16), refs[py][...].astype(bf16), TN_DIMS,
                                   preferred_element_type=f32)

            @pl.when(i == 0)
            def _():
                o_ref[...] = part

            @pl.when(i > 0)
            def _():
                o_ref[...] += part

            if scale != 1.0:
                @pl.when(i == n_t - 1)
                def _():
                    o_ref[...] = o_ref[...] * scale

    return _pallas(
        comm, body, name=name, grid=(s_n, n_t),
        in_specs=specs,
        out_specs=[pl.BlockSpec((None, kx, ky), lambda s, i: (s, 0, 0)) for kx, ky in widths],
        out_shape=[jax.ShapeDtypeStruct((s_n, kx, ky), f32) for kx, ky in widths],
        compiler_params=_params("parallel", "arbitrary"),
    )(*arrays)


def _nt_sum(pairs, base, base_scale, s_n, out_dtype, name, comm=None, ln=None):
    t = pairs[0][0].shape[-2]
    pairs = [(*pr, False)[:5] for pr in pairs]
    d = pairs[0][2].shape[-1] if pairs[0][4] else pairs[0][2].shape[-2]
    tm = _tile(t, ROWS_ACC if ln is None else ROWS_ACC_LN, 16)
    n_p = len(pairs)
    has_base = base is not None
    flipped = [pr[4] for pr in pairs]
    n_out = 1 if ln is None else 4

    def body(*refs):
        dy_refs = refs[0:2 * n_p:2]
        w_refs = refs[1:2 * n_p:2]
        rest = refs[2 * n_p:]
        base_ref = rest[0] if has_base else None
        o_refs, acc = rest[-1 - n_out:-1], rest[-1]
        s = pl.program_id(1)

        if ln is not None:
            @pl.when(jnp.logical_and(pl.program_id(0) == 0, s == 0))
            def _():
                o_refs[2][...] = jnp.zeros_like(o_refs[2])
                o_refs[3][...] = jnp.zeros_like(o_refs[3])

        @pl.when(s == 0)
        def _():
            acc[...] = jnp.zeros_like(acc)

        tot = None
        for dy_ref, w_ref, flip in zip(dy_refs, w_refs, flipped):
            dyv = dy_ref[...].astype(bf16)
            if flip:
                part = jnp.dot(dyv, w_ref[...], preferred_element_type=f32)
            else:
                part = lax.dot_general(dyv, w_ref[...], NT_DIMS, preferred_element_type=f32)
            tot = part if tot is None else tot + part
        acc[...] += tot

        @pl.when(s == s_n - 1)
        def _():
            res = acc[...]
            if has_base:
                res = base_scale * base_ref[...] + res
            if ln is None:
                o_refs[0][...] = res.astype(o_refs[0].dtype)
            else:
                r_ref, g_ref = rest[-7], rest[-6]
                dr, dg, db = _ln_bwd_rows(res, r_ref[...], g_ref[...])
                o_refs[0][...] = dr
                o_refs[1][...] = dr.astype(bf16)
                o_refs[2][...] += dg
                o_refs[3][...] += db

    in_specs, args = [], []
    for dy, mode, w, layer, flip in pairs:
        k = _act_width(dy, mode, s_n)
        in_specs.append(_act_spec(mode, tm, k, True))
        wshape = (k, d) if flip else (d, k)
        if layer is None:
            in_specs.append(pl.BlockSpec((None, *wshape), lambda i, s: (s, 0, 0)))
        else:
            in_specs.append(pl.BlockSpec((None, None, *wshape),
                                         functools.partial(lambda i, s, l: (s, l, 0, 0), l=layer)))
        args += [dy, w]
    row = pl.BlockSpec((tm, d), lambda i, s: (i, 0))
    vec = pl.BlockSpec((1, d), lambda i, s: (0, 0))
    if has_base:
        in_specs.append(row)
        args.append(base)
    if ln is None:
        out_specs = row
        out_shape = jax.ShapeDtypeStruct((t, d), out_dtype)
    else:
        in_specs += [row, vec]
        args += list(ln)
        out_specs = [row, row, vec, vec]
        out_shape = [jax.ShapeDtypeStruct((t, d), f32), jax.ShapeDtypeStruct((t, d), bf16),
                     jax.ShapeDtypeStruct((1, d), f32), jax.ShapeDtypeStruct((1, d), f32)]
    return _pallas(
        comm, body, name=name, grid=(t // tm, s_n),
        in_specs=in_specs, out_specs=out_specs, out_shape=out_shape,
        scratch_shapes=[pltpu.VMEM((tm, d), f32)],
        compiler_params=_params("parallel" if ln is None else "arbitrary", "arbitrary"),
    )(*args)


def _conv_bwd(dz, p, conv_w, name, comm=None):
    t, d3 = p.shape
    d = d3 // 3
    tm = _tile(t, 320, 8)
    hb = tm // 8
    last8 = t // 8 - 1
    n_ext = tm + 8

    def body(dz_ref, dzn_ref, p_ref, pp_ref, pn_ref, w_ref, dp_ref, dw_ref):
        i = pl.program_id(0)

        @pl.when(i == 0)
        def _():
            dw_ref[...] = jnp.zeros_like(dw_ref)

        w0, w1, w2 = w_ref[0:1, :], w_ref[1:2, :], w_ref[2:3, :]
        rows_u = i * tm - 8 + lax.broadcasted_iota(jnp.int32, (n_ext, 1), 0)
        cg = jnp.concatenate([pp_ref[:, d:2 * d], p_ref[:, d:2 * d]], axis=0)
        val = jnp.concatenate([pp_ref[:, 2 * d:], p_ref[:, 2 * d:]], axis=0)
        u = jnp.where(rows_u >= PAD, cg * val, 0.0)
        u1 = pltpu.roll(u, 1, 0)
        u2 = pltpu.roll(u, 2, 0)
        y = (w2 * u + w1 * u1 + w0 * u2)[8:]
        dzv = dz_ref[...]
        bg = p_ref[:, :d]
        rows_n = (i + 1) * tm + lax.broadcasted_iota(jnp.int32, (8, 1), 0)
        dy_main = dzv * bg
        dy_next = jnp.where(rows_n < t, dzn_ref[...] * pn_ref[:, :d], 0.0)
        dye = jnp.concatenate([dy_main, dy_next], axis=0)
        du = (w2 * dye + w1 * pltpu.roll(dye, n_ext - 1, 0)
              + w0 * pltpu.roll(dye, n_ext - 2, 0))[:tm]
        du = jnp.where(rows_u[8:] >= PAD, du, 0.0)
        dp_ref[:, :d] = (dzv * y).astype(bf16)
        dp_ref[:, d:2 * d] = (du * val[8:]).astype(bf16)
        dp_ref[:, 2 * d:] = (du * cg[8:]).astype(bf16)
        dw_ref[0:1, :] += jnp.sum(dy_main * u2[8:], axis=0, keepdims=True)
        dw_ref[1:2, :] += jnp.sum(dy_main * u1[8:], axis=0, keepdims=True)
        dw_ref[2:3, :] += jnp.sum(dy_main * u[8:], axis=0, keepdims=True)

    nxt = lambda i: (jnp.minimum((i + 1) * hb, last8), 0)
    return _pallas(
        comm, body, name=name, grid=(t // tm,),
        in_specs=[pl.BlockSpec((tm, d), lambda i: (i, 0)),
                  pl.BlockSpec((8, d), nxt),
                  pl.BlockSpec((tm, d3), lambda i: (i, 0)),
                  pl.BlockSpec((8, d3), lambda i: (jnp.maximum(i * hb - 1, 0), 0)),
                  pl.BlockSpec((8, d3), nxt),
                  pl.BlockSpec((3, d), lambda i: (0, 0))],
        out_specs=[pl.BlockSpec((tm, d3), lambda i: (i, 0)),
                   pl.BlockSpec((3, d), lambda i: (0, 0))],
        out_shape=[jax.ShapeDtypeStruct((t, d3), bf16), jax.ShapeDtypeStruct((3, d), f32)],
        compiler_params=_params("arbitrary"),
    )(dz, dz, p, p, p, conv_w)


def _attn_stats(o, do, bq, name):
    t, d = o.shape
    n_heads = d // HEAD_DIM

    def body(o_ref, do_ref, delta_ref):
        for c0 in range(0, bq, LANES):
            rows = slice(c0, c0 + LANES)
            prod = o_ref[rows, :] * do_ref[rows, :].astype(f32)
            delta_ref[:, rows] = jnp.sum(prod.T, axis=0, keepdims=True)

    qblk = pl.BlockSpec((bq, HEAD_DIM), lambda h, i: (i, h))
    return pl.pallas_call(
        body, name=name, grid=(n_heads, t // bq),
        in_specs=[qblk, qblk],
        out_specs=pl.BlockSpec((None, None, 1, bq), lambda h, i: (h, i, 0, 0)),
        out_shape=jax.ShapeDtypeStruct((n_heads, t // bq, 1, bq), f32),
        compiler_params=_params("parallel", "parallel"),
    )(o, do)


def _attn_bwd(q, k, v, kt, do, ckey, cl_rows, delta_rows, name, comm=None):
    t, d = q.shape
    n_heads = d // HEAD_DIM
    bk = kt.shape[-1]
    bq = bk
    n_kv = t // bk
    n_q = t // bq
    scale = 1.0 / math.sqrt(HEAD_DIM)

    def body(q_ref, do_ref, cl_ref, dl_ref, k_ref, v_ref, kt_ref, ck_ref,
             dq_ref, dcq_ref, dk_ref, dv_ref, dck_ref,
             st_scr, dp_scr, p_scr, ds_scr, dqt, dk_acc, dv_acc, dck_acc):
        j = pl.program_id(1)

        @pl.when(j == 0)
        def _():
            dqt[...] = jnp.zeros_like(dqt)
            dcq_ref[...] = jnp.zeros_like(dcq_ref)

        dk_acc[...] = jnp.zeros_like(dk_acc)
        dv_acc[...] = jnp.zeros_like(dv_acc)
        dck_acc[...] = jnp.zeros_like(dck_acc)
        kb = k_ref[...]
        vb = v_ref[...]
        behind = (lax.broadcasted_iota(jnp.int32, (STRIP, bq), 1)
                  - lax.broadcasted_iota(jnp.int32, (STRIP, bq), 0))

        def tile(i, diagonal):
            r0 = pl.multiple_of(i * bq, bq)
            qi = q_ref[pl.ds(r0, bq), :]
            doi = do_ref[pl.ds(r0, bq), :]
            st_scr[...] = lax.dot_general(kb, qi, NT_DIMS, preferred_element_type=f32)
            dp_scr[...] = lax.dot_general(vb, doi, NT_DIMS, preferred_element_type=f32)
            cl = cl_ref[i]
            dl = dl_ref[i]
            over_keys = jnp.zeros((STRIP, bq), f32)
            for r in range(0, bk, STRIP):
                keys = slice(r, r + STRIP)
                st = st_scr[keys, :] + cl - _lanes(ck_ref[keys, :], bq)
                if diagonal:
                    st = jnp.where(behind >= r, st, NEG_INF)
                pr = jnp.exp(st)
                ds = pr * (dp_scr[keys, :] - dl)
                over_keys = over_keys + ds
                dck_acc[keys, :] -= jnp.sum(ds, axis=1, keepdims=True)
                p_scr[keys, :] = pr.astype(bf16)
                ds_scr[keys, :] = ds.astype(bf16)
            dcq_ref[i] += jnp.sum(over_keys, axis=0, keepdims=True)
            dv_acc[...] += jnp.dot(p_scr[...], doi, preferred_element_type=f32)
            dk_acc[...] += jnp.dot(ds_scr[...], qi, preferred_element_type=f32)
            dqt[i] += jnp.dot(kt_ref[...], ds_scr[...], preferred_element_type=f32)

        def full_tile(i, carry):
            tile(i, False)
            return carry

        tile(j, True)
        lax.fori_loop(j + 1, n_q, full_tile, 0)
        dk_ref[...] = dk_acc[...].astype(bf16)
        dv_ref[...] = dv_acc[...].astype(bf16)
        for c0 in range(0, bk, LANES):
            keys = slice(c0, c0 + LANES)
            dck_ref[:, keys] = jnp.broadcast_to(dck_acc[keys, :], (LANES, LANES)).T[0:1, :]

        @pl.when(j == n_kv - 1)
        def _():
            def emit(i, carry):
                r0 = pl.multiple_of(i * bq, bq)
                dq_ref[pl.ds(r0, bq), :] = dqt[i].T * scale
                return carry
            lax.fori_loop(0, n_q, emit, 0)

    head_rows = pl.BlockSpec((t, HEAD_DIM), lambda h, j: (0, h))
    head_stat = pl.BlockSpec((None, n_q, 1, bq), lambda h, j: (h, 0, 0, 0))
    kblk = pl.BlockSpec((bk, HEAD_DIM), lambda h, j: (j, h))
    return _pallas(
        comm, body, name=name, grid=(n_heads, n_kv),
        in_specs=[head_rows, head_rows, head_stat, head_stat, kblk, kblk,
                  pl.BlockSpec((None, None, HEAD_DIM, bk), lambda h, j: (h, j, 0, 0)),
                  pl.BlockSpec((None, bk, LANES), lambda h, j: (h, j, 0))],
        out_specs=[head_rows, head_stat, kblk, kblk,
                   pl.BlockSpec((None, None, 1, bk), lambda h, j: (h, j, 0, 0))],
        out_shape=[jax.ShapeDtypeStruct((t, d), f32),
                   jax.ShapeDtypeStruct((n_heads, n_q, 1, bq), f32),
                   jax.ShapeDtypeStruct((t, d), bf16), jax.ShapeDtypeStruct((t, d), bf16),
                   jax.ShapeDtypeStruct((n_heads, n_kv, 1, bk), f32)],
        scratch_shapes=[pltpu.VMEM((bk, bq), f32), pltpu.VMEM((bk, bq), f32),
                        pltpu.VMEM((bk, bq), bf16), pltpu.VMEM((bk, bq), bf16),
                        pltpu.VMEM((n_q, HEAD_DIM, bq), f32),
                        pltpu.VMEM((bk, HEAD_DIM), f32), pltpu.VMEM((bk, HEAD_DIM), f32),
                        pltpu.VMEM((bk, 1), f32)],
        compiler_params=_params("parallel", "arbitrary"),
    )(q, do, cl_rows, delta_rows, k, v, kt, ckey)


def _fgate_bwd(dlogf, flog, fbias, name):
    t, n = flog.shape
    rows = _tile(t, 640, LANES)

    def body(dl_ref, fl_ref, fb_ref, o_ref, sum_ref):
        i = pl.program_id(0)

        @pl.when(i == 0)
        def _():
            sum_ref[...] = jnp.zeros_like(sum_ref)

        r = i * rows + lax.broadcasted_iota(jnp.int32, (rows, n), 0)
        g = dl_ref[...] * jax.nn.sigmoid(-(fl_ref[...] + fb_ref[...]))
        g = jnp.where(r >= PAD, g, 0.0)
        o_ref[...] = g
        sum_ref[...] += jnp.sum(g, axis=0, keepdims=True)

    blk = pl.BlockSpec((rows, n), lambda i: (i, 0))
    vec = pl.BlockSpec((1, n), lambda i: (0, 0))
    return pl.pallas_call(
        body, name=name, grid=(t // rows,),
        in_specs=[blk, blk, vec], out_specs=[blk, vec],
        out_shape=[jax.ShapeDtypeStruct((t, n), f32), jax.ShapeDtypeStruct((1, n), f32)],
        compiler_params=_params("arbitrary"),
    )(dlogf, flog, fbias)


def _place_shard(w, idx, name):
    n_l, r, c_n = w.shape
    out_dtype = bf16 if r * c_n > 2 ** 16 else w.dtype
    tr = _tile(r, 512, 16) if r % 16 == 0 else r

    def body(idx_ref, w_ref, o_ref):
        o_ref[...] = w_ref[...].astype(out_dtype)

    grid_spec = pltpu.PrefetchScalarGridSpec(
        num_scalar_prefetch=1, grid=(n_l, r // tr),
        in_specs=[pl.BlockSpec((None, tr, c_n), lambda l, i, idx: (l, i, 0))],
        out_specs=pl.BlockSpec((None, None, tr, c_n), lambda l, i, idx: (idx[0], l, i, 0)))
    return pl.pallas_call(
        body, name=name, grid_spec=grid_spec,
        out_shape=jax.ShapeDtypeStruct((N_CHIPS, n_l, r, c_n), out_dtype),
        compiler_params=_params("parallel", "parallel"),
    )(idx, w)


def _plan_gather_ici(items):
    def plan(srcs, bufs, news, p):
        out = []
        for name, layer, r2 in items:
            mine = bufs[name].at[p.me, layer, pl.ds(p.c * r2, r2)]
            out += [(mine, mine, (*chip, p.c)) for chip in p.chips]
        return out
    return plan, 3 * len(items)


def _plan_gather_d2d(items):
    def plan(srcs, bufs, news, p):
        out = []
        for name, layer, r2 in items:
            for px, py in p.chips:
                landed = bufs[name].at[2 * px + py, layer, pl.ds(p.c * r2, r2)]
                out.append((landed, landed, p.sib))
        return out
    return plan, 3 * len(items)


def _plan_pair_exchange(names, r2s):
    def plan(srcs, bufs, news, p):
        out = []
        for name, r2 in zip(names, r2s):
            for k, slot in enumerate(p.slots):
                out.append((srcs["G_" + name].at[slot, pl.ds((1 - p.c) * r2, r2)],
                            news["PAIR_" + name].at[k], p.sib))
        return out
    return plan, 4 * len(names)


def _plan_chip_exchange(names):
    def plan(srcs, bufs, news, p):
        out = []
        for name in names:
            for k, chip in enumerate(p.chips):
                out.append((srcs["SEND_" + name].at[k], news["RECV_" + name].at[k], (*chip, p.c)))
        return out
    return plan, 3 * len(names)


def _plan_pair_share(items):
    def plan(srcs, bufs, news, p):
        out = []
        for name, layer, r2 in items:
            mine = bufs["RED_" + name].at[layer, pl.ds(p.c * r2, r2)]
            out.append((mine, mine, p.sib))
        return out
    return plan, len(items)


def _rs_prepare(gs, pairs, idx, name):
    n = len(gs)
    _, r2, c_n = pairs[0].shape
    tr = _tile(r2, 256, 8)
    nb = r2 // tr

    def body(idx_ref, *refs):
        for g_ref, p_ref, o_ref in zip(refs[:n], refs[n:2 * n], refs[2 * n:]):
            o_ref[...] = (g_ref[...] + p_ref[...]).astype(bf16)

    g_spec = pl.BlockSpec((None, tr, c_n), lambda k, i, idx: (idx[k + 1], idx[4] * nb + i, 0))
    p_spec = pl.BlockSpec((None, tr, c_n), lambda k, i, idx: (k + 1, i, 0))
    grid_spec = pltpu.PrefetchScalarGridSpec(
        num_scalar_prefetch=1, grid=(3, nb), in_specs=[g_spec] * n + [p_spec] * n,
        out_specs=[pl.BlockSpec((None, tr, c_n), lambda k, i, idx: (k, i, 0))] * n)
    return pl.pallas_call(
        body, name=name, grid_spec=grid_spec,
        out_shape=[jax.ShapeDtypeStruct((3, r2, c_n), bf16)] * n,
        compiler_params=_params("parallel", "parallel"),
    )(idx, *gs, *pairs)


def _rs_finish(gs, pairs, recvs, idx, layer, n_layers, intos, name):
    n = len(gs)
    _, r2, c_n = pairs[0].shape
    tr = _tile(r2, 256, 8)
    nb = r2 // tr

    def body(idx_ref, *refs):
        for j in range(n):
            g_ref, p_ref, r0_ref, r1_ref, r2_ref = refs[5 * j:5 * j + 5]
            acc = g_ref[...] + p_ref[...]
            acc = acc + r0_ref[...].astype(f32)
            acc = acc + r1_ref[...].astype(f32)
            acc = acc + r2_ref[...].astype(f32)
            refs[len(refs) - n + j][...] = acc

    def rspec(k):
        return pl.BlockSpec((None, tr, c_n), functools.partial(lambda i, idx, kk: (kk, i, 0), kk=k))

    item_specs = [pl.BlockSpec((None, tr, c_n), lambda i, idx: (idx[0], idx[4] * nb + i, 0)),
                  rspec(0), rspec(0), rspec(1), rspec(2)]
    in_specs = item_specs * n
    args = [idx]
    for g, pair, recv in zip(gs, pairs, recvs):
        args += [g, pair, recv, recv, recv]
    aliases = {}
    if intos is not None:
        in_specs = in_specs + [pl.BlockSpec(memory_space=pl.ANY)] * n
        args += list(intos)
        aliases = {1 + 5 * n + j: j for j in range(n)}
    grid_spec = pltpu.PrefetchScalarGridSpec(
        num_scalar_prefetch=1, grid=(nb,), in_specs=in_specs,
        out_specs=[pl.BlockSpec((None, tr, c_n), lambda i, idx: (layer, idx[4] * nb + i, 0))] * n)
    return pl.pallas_call(
        body, name=name, grid_spec=grid_spec,
        out_shape=[jax.ShapeDtypeStruct((n_layers, 2 * r2, c_n), f32)] * n,
        input_output_aliases=aliases,
        compiler_params=_params("parallel"),
    )(*args)


def _adamw(ws, gs, ms, vs, layer, intos, name):
    n = len(ws)
    n_l, r, c_n = ws[0].shape
    tr = _tile(r, 256, 8)

    def body(*refs):
        for j in range(n):
            w_ref, g_ref, m_ref, v_ref = refs[4 * j:4 * j + 4]
            o0 = len(refs) - 4 * n + 4 * j
            d_ref, mo_ref, vo_ref, go_ref = refs[o0:o0 + 4]
            gv = g_ref[...]
            go_ref[...] = gv
            mn = ADAM_B1 * m_ref[...] + (1.0 - ADAM_B1) * gv
            vn = ADAM_B2 * v_ref[...] + (1.0 - ADAM_B2) * (gv * gv)
            m_hat = mn / (1.0 - ADAM_B1 ** ADAM_STEP)
            v_hat = vn / (1.0 - ADAM_B2 ** ADAM_STEP)
            d_ref[...] = -ADAM_LR * (m_hat / (jnp.sqrt(v_hat) + ADAM_EPS) + ADAM_WD * w_ref[...])
            mo_ref[...] = mn
            vo_ref[...] = vn

    blk = pl.BlockSpec((None, tr, c_n), lambda i: (layer, i, 0))
    shp = jax.ShapeDtypeStruct((n_l, r, c_n), f32)
    in_specs = [blk] * (4 * n)
    args = []
    for w, g, m, v in zip(ws, gs, ms, vs):
        args += [w, g, m, v]
    aliases = {}
    if intos is not None:
        in_specs = in_specs + [pl.BlockSpec(memory_space=pl.ANY)] * (4 * n)
        for into in intos:
            args += list(into)
        aliases = {4 * n + j: j for j in range(4 * n)}
    res = pl.pallas_call(
        body, name=name, grid=(r // tr,),
        in_specs=in_specs, out_specs=[blk] * (4 * n), out_shape=[shp] * (4 * n),
        input_output_aliases=aliases,
        compiler_params=_params("parallel"),
    )(*args)
    return [tuple(res[4 * j:4 * j + 4]) for j in range(n)]


def kernel(x, meta, ffn1_wg, ffn1_wu, ffn1_wd, ffn2_wg, ffn2_wu, ffn2_wd, ln_gain, ln_bias, conv_w_in, conv_w, conv_w_out, kv_w, f_bias, attn_w_q, attn_w_o, loss_target, m_meta, m_ffn1_wg, m_ffn1_wu, m_ffn1_wd, m_ffn2_wg, m_ffn2_wu, m_ffn2_wd, m_ln_gain, m_ln_bias, m_conv_w_in, m_conv_w, m_conv_w_out, m_kv_w, m_f_bias, m_attn_w_q, m_attn_w_o, v_meta, v_ffn1_wg, v_ffn1_wu, v_ffn1_wd, v_ffn2_wg, v_ffn2_wu, v_ffn2_wd, v_ln_gain, v_ln_bias, v_conv_w_in, v_conv_w, v_conv_w_out, v_kv_w, v_f_bias, v_attn_w_q, v_attn_w_o):
    seq, d = x.shape[1], x.shape[2]
    t = PAD + N_META + seq
    n_heads = d // HEAD_DIM
    dq = d // N_CHIPS
    n_kv = kv_w.shape[1]
    x2 = x[0]
    target = loss_target[0]

    def rows8(a):
        return jnp.pad(a, ((0, 8 - a.shape[0]), (0, 0)))

    def small_pack(mt, g, b, cw, fb):
        fb_row = jnp.pad(fb, (0, mt.shape[1] - n_heads))[None]
        return jnp.concatenate([mt, rows8(g.reshape(6, -1)), rows8(b.reshape(6, -1)),
                                rows8(cw.reshape(3, -1)), rows8(fb_row)], axis=0)

    w_small = small_pack(meta, ln_gain, ln_bias, conv_w, f_bias)

    cx, cy, c = lax.axis_index("x"), lax.axis_index("y"), lax.axis_index("c")
    idx = jnp.stack([2 * cx + cy, 2 * (1 - cx) + cy, 2 * cx + (1 - cy), 2 * (1 - cx) + (1 - cy), c]
                    ).astype(jnp.int32)

    def tr(a):
        return a.transpose(0, 2, 1)

    w3 = {"wg1": tr(ffn1_wg), "wu1": tr(ffn1_wu), "wd1": ffn1_wd, "wg2": tr(ffn2_wg),
          "wu2": tr(ffn2_wu), "wd2": ffn2_wd, "win": conv_w_in, "wout": conv_w_out, "kv": kv_w[None],
          "wq": attn_w_q, "wo": attn_w_o, "small": w_small[None]}
    transposed = ("wg1", "wu1", "wg2", "wu2")
    buf = {n: _place_shard(w, idx, "place_shard") for n, w in w3.items()}

    def split(item):
        name, layer = item.split(".")
        return name, int(layer)

    def gather_stage(planner, items):
        triples = [(n, l, buf[n].shape[2] // 2) for n, l in map(split, items)]
        plan, n_copies = planner(triples)
        return dict(plan=plan, n=n_copies, bufs={n: buf[n] for n, _, _ in triples})

    def ici(items):
        return gather_stage(_plan_gather_ici, items)

    def d2d(items):
        return gather_stage(_plan_gather_d2d, items)

    def pair_exchange(items):
        r2s = [buf["G_" + it].shape[1] // 2 for it in items]
        plan, n_copies = _plan_pair_exchange(items, r2s)
        news = {"PAIR_" + it: jax.ShapeDtypeStruct((N_CHIPS, r2, buf["G_" + it].shape[2]), f32)
                for it, r2 in zip(items, r2s)}
        return dict(plan=plan, n=n_copies, srcs={"G_" + it: buf["G_" + it] for it in items}, news=news)

    def chip_exchange(items):
        plan, n_copies = _plan_chip_exchange(items)
        srcs = {"SEND_" + it: buf["SEND_" + it] for it in items}
        news = {"RECV_" + it: jax.ShapeDtypeStruct(s.shape, s.dtype)
                for it, s in ((it, buf["SEND_" + it]) for it in items)}
        return dict(plan=plan, n=n_copies, srcs=srcs, news=news)

    def pair_share(items):
        triples = [(n, l, buf["RED_" + n].shape[1] // 2) for n, l in map(split, items)]
        plan, n_copies = _plan_pair_share(triples)
        return dict(plan=plan, n=n_copies, bufs={"RED_" + n: buf["RED_" + n] for n, _, _ in triples})

    def run(fn, *args, stages=(), name=None, **kw):
        comm = _Copies()
        for st in (stages() if callable(stages) else stages):
            comm.add(st["plan"], st["n"], srcs=st.get("srcs"), bufs=st.get("bufs"), news=st.get("news"))
        out = _copy_call(comm, name) if fn is None else fn(*args, comm=comm, **kw)
        buf.update(comm.out_bufs)
        buf.update(comm.out_news)
        return out

    first = ["wg1.0", "wu1.0", "small.0"]
    run(None, stages=[ici(first)], name="gather_first_ici")
    run(None, stages=[d2d(first)], name="gather_first_d2d")
    small = buf["small"].reshape(N_CHIPS, SMALL_ROWS, dq).transpose(1, 0, 2).reshape(SMALL_ROWS, d)
    meta_full = small[:N_META]
    gains = small[16:22].reshape(DEPTH, 3, 1, d)
    biases = small[24:30].reshape(DEPTH, 3, 1, d)
    conv_w_full = small[32:35]
    fb_pad = jnp.pad(f_bias, (0, LANES - n_heads))[None]
    down1 = ["wd1.0"]
    ffn2_l0 = ["win.0", "wout.0", "wg2.0", "wu2.0", "wd2.0", "kv.0"]
    attn_ffn2_l1 = ["wq.0", "wo.0", "wg2.1", "wu2.1", "wd2.1"]
    ffn1_l1 = ["wg1.1", "wu1.1", "wd1.1"]

    meta_pad = jnp.concatenate([jnp.zeros((PAD, d), f32), meta_full], axis=0)
    h0, h0b = run(_embed, meta_pad, x2, "embed", stages=[ici(down1)])
    a1, b1, s1 = run(_ffn_up, h0b, buf["wg1"], buf["wu1"], 0, "ffn_up",
                     stages=lambda: [d2d(down1), ici(ffn2_l0)])
    r1, h1, h1b = run(_down_ln, s1, buf["wd1"], 0, h0, gains[0, 0], biases[0, 0], 0.5, "ffn_down_ln",
                      stages=lambda: [d2d(ffn2_l0), ici(attn_ffn2_l1)])
    n_in = conv_w_in.shape[-1]
    w_in = buf["win"].reshape(N_CHIPS, d, n_in)
    w_out = buf["wout"].reshape(1, 1, d, d)
    p = run(_nn_matmul, h1b, w_in, f32, "conv_in", stages=lambda: [d2d(attn_ffn2_l1), ici(ffn1_l1)])
    z = _conv_fwd(p, conv_w_full, "conv_fwd")
    r2, h2, h2b = run(_down_ln, z[None], w_out, 0, h1, gains[0, 1], biases[0, 1], 1.0, "mix_out_ln",
                      stages=lambda: [d2d(ffn1_l1)])
    wg1, wu1, wd1, wg2, wu2, wd2 = (buf[n] for n in ("wg1", "wu1", "wd1", "wg2", "wu2", "wd2"))
    w_q = buf["wq"].reshape(1, d, d)
    w_o = buf["wo"].reshape(1, 1, d, d)
    kv_full = buf["kv"].reshape(N_CHIPS, d, n_kv).transpose(1, 0, 2).reshape(d, N_CHIPS * n_kv)
    w_k = kv_full[:, :d][None]
    w_v = kv_full[:, d:2 * d][None]
    w_f = jnp.pad(kv_full[:, 2 * d:], ((0, 0), (0, LANES - n_heads)))[None]
    a2, b2, s2 = _ffn_up(h2b, wg2, wu2, 0, "ffn_up")
    r3, h3, h3b = _down_ln(s2, wd2, 0, h2, gains[0, 2], biases[0, 2], 0.5, "ffn_down_ln")
    kk = _nn_matmul(h3b, w_k, bf16, "proj_bf16")
    vv = _nn_matmul(h3b, w_v, bf16, "proj_bf16")
    flog = _nn_matmul(h3b, w_f, f32, "proj_gate")
    cum = _row_scan(flog, "gate_cumsum", fb_pad)
    bk = _tile(t, 640, LANES)
    c_ht = cum[:, :n_heads].T
    c_keys = jnp.where(jnp.arange(t)[None, :] < PAD, 1e30, c_ht)
    cq_rep = jnp.broadcast_to(c_ht[:, :, None], (n_heads, t, LANES))
    ck_rep = jnp.broadcast_to(c_keys[:, :, None], (n_heads, t, LANES))
    ck_rows = c_keys.reshape(n_heads, t // bk, 1, bk)
    a3, b3, s3 = _ffn_up(h3b, wg1, wu1, 1, "ffn_up")
    r4, h4, h4b = _down_ln(s3, wd1, 1, h3, gains[1, 0], biases[1, 0], 0.5, "ffn_down_ln")
    q = _nn_matmul(h4b, w_q, bf16, "proj_q", out_scale=1.0 / math.sqrt(HEAD_DIM))
    o, o32, cl = _attn_fwd(q, kk, vv, cq_rep, ck_rows, "attn_fwd")
    r5, h5, h5b = _down_ln(o[None], w_o, 0, h4, gains[1, 1], biases[1, 1], 1.0, "mix_out_ln")
    a4, b4, s4 = _ffn_up(h5b, wg2, wu2, 1, "ffn_up")
    r6, h6, _ = _down_ln(s4, wd2, 1, h5, gains[1, 2], biases[1, 2], 0.5, "ffn_down_ln")
    dr6, dr6b, dg12, db12, sq = _loss_head(h6, target, r6, gains[1, 2], "loss_head")
    loss_part = 0.5 * sq[0, 0] / d

    m_small = small_pack(m_meta, m_ln_gain, m_ln_bias, m_conv_w, m_f_bias)
    v_small = small_pack(v_meta, v_ln_gain, v_ln_bias, v_conv_w, v_f_bias)
    m3 = {"wg1": tr(m_ffn1_wg), "wu1": tr(m_ffn1_wu), "wd1": m_ffn1_wd, "wg2": tr(m_ffn2_wg),
          "wu2": tr(m_ffn2_wu), "wd2": m_ffn2_wd, "win": m_conv_w_in, "wout": m_conv_w_out,
          "kv": m_kv_w[None], "wq": m_attn_w_q, "wo": m_attn_w_o, "small": m_small[None]}
    v3 = {"wg1": tr(v_ffn1_wg), "wu1": tr(v_ffn1_wu), "wd1": v_ffn1_wd, "wg2": tr(v_ffn2_wg),
          "wu2": tr(v_ffn2_wu), "wd2": v_ffn2_wd, "win": v_conv_w_in, "wout": v_conv_w_out,
          "kv": v_kv_w[None], "wq": v_attn_w_q, "wo": v_attn_w_o, "small": v_small[None]}
    stepped = {}

    def alike(items):
        groups = {}
        for it in items:
            n, l = split(it)
            groups.setdefault((w3[n].shape, l, ("RED_" + n) in buf, n in stepped), []).append(it)
        return groups.values()

    def prepare(items):
        for group in alike(items):
            sends = _rs_prepare([buf["G_" + it] for it in group], [buf["PAIR_" + it] for it in group],
                                idx, "grad_prepare")
            buf.update({"SEND_" + it: s for it, s in zip(group, sends)})

    def finish(items):
        for group in alike(items):
            names = [split(it)[0] for it in group]
            layer = split(group[0])[1]
            intos = [buf["RED_" + n] for n in names] if ("RED_" + names[0]) in buf else None
            reds = _rs_finish([buf["G_" + it] for it in group], [buf["PAIR_" + it] for it in group],
                              [buf["RECV_" + it] for it in group], idx, layer, w3[names[0]].shape[0],
                              intos, "grad_finish")
            buf.update({"RED_" + n: r for n, r in zip(names, reds)})

    def adam(items, grads=None):
        for group in alike(items):
            names = [split(it)[0] for it in group]
            layer = split(group[0])[1]
            gs = [buf["RED_" + n] if grads is None else grads[n] for n in names]
            intos = [stepped[n] for n in names] if names[0] in stepped else None
            res = _adamw([w3[n] for n in names], gs, [m3[n] for n in names], [v3[n] for n in names],
                         layer, intos, "adamw")
            stepped.update(dict(zip(names, res)))

    def ffn_bwd(dr, drb, hb_in, a, b, s, f, layer, on_act=(), after_act=None, on_dwd=(),
                after_dwd=None, on_dx=(), ln=None):
        da, db = run(_ffn_bwd_act, drb, buf["wd" + f], layer, a, b, "ffn_bwd_act", stages=on_act)
        if after_act is not None:
            after_act()
        (buf[f"G_wd{f}.{layer}"],) = run(_tn_matmul, [(s, "stack", drb, "shared")], N_CHIPS, 0.5,
                                         "ffn_dwd", stages=on_dwd)
        if after_dwd is not None:
            after_dwd()
        buf[f"G_wg{f}.{layer}"], buf[f"G_wu{f}.{layer}"] = _tn_matmul(
            [(da, "stack", hb_in, "shared"), (db, "stack", hb_in, "shared")], N_CHIPS, 1.0, "ffn_dwgu")
        return run(_nt_sum, [(da, "stack", buf["wg" + f], layer, True),
                             (db, "stack", buf["wu" + f], layer, True)],
                   dr, ALPHA, N_CHIPS, f32, "ffn_dx", stages=on_dx, ln=ln)

    ffn2_1 = ["wg2.1", "wu2.1", "wd2.1"]
    ffn1_1 = ["wg1.1", "wu1.1", "wd1.1"]
    ffn2_0 = ["wg2.0", "wu2.0", "wd2.0"]
    conv_items = ["wout.0", "win.0"]

    dr5, dr5b, dg11, db11 = ffn_bwd(dr6, dr6b, h5b, a4, b4, s4, "2", 1,
                                    on_dx=lambda: [pair_exchange(ffn2_1)], ln=(r5, gains[1, 1]))
    prepare(ffn2_1)
    (dwo,) = _tn_matmul([(o, "shared", dr5b, "shared")], 1, 1.0, "sq_dw")
    buf["G_wo.0"] = dwo.reshape(N_CHIPS, dq, d)
    do = run(_nt_sum, [(dr5b, "cols", w_o[0], None)], None, 1.0, 1, bf16, "sq_dx_bf16",
             stages=lambda: [pair_exchange(["wo.0"])])
    prepare(["wo.0"])
    delta = _attn_stats(o32, do, bk, "attn_stats")
    kt = kk.reshape(t // bk, bk, n_heads, HEAD_DIM).transpose(2, 0, 3, 1)
    dq_att, dc_q, dk, dv, dc_k = run(_attn_bwd, q, kk, vv, kt, do, ck_rep, cl, delta, "attn_bwd",
                                     stages=lambda: [chip_exchange(ffn2_1 + ["wo.0"])])
    finish(ffn2_1 + ["wo.0"])
    dc = (dc_q + dc_k).reshape(n_heads, t)
    (dwq,) = run(_tn_matmul, [(h4b, "shared", dq_att, "shared")], 1, 1.0, "sq_dw",
                 stages=lambda: [pair_share(ffn2_1 + ["wo.0"])])
    buf["G_wq.0"] = dwq.reshape(N_CHIPS, dq, d)
    adam(ffn2_1 + ["wo.0"])
    dr4, dr4b, dg10, db10 = run(_nt_sum, [(dq_att, "cols", w_q, None)], dr5, ALPHA, 1, f32,
                                "sq_dx_res", stages=lambda: [pair_exchange(["wq.0"])],
                                ln=(r4, gains[1, 0]))
    prepare(["wq.0"])
    dh3a = ffn_bwd(dr4, dr4b, h3b, a3, b3, s3, "1", 1,
                   on_act=lambda: [chip_exchange(["wq.0"])],
                   on_dx=lambda: [pair_exchange(ffn1_1)])
    prepare(ffn1_1)
    finish(["wq.0"])
    dc_t = jnp.pad(dc.T, ((0, 0), (0, LANES - n_heads)))
    dlogf = _row_scan(dc_t, "rev_cumsum", reverse=True)
    dfl, dfb_cols = _fgate_bwd(dlogf, flog, fb_pad, "gate_bwd")
    dwk, dwv, dwf = run(_tn_matmul, [(h3b, "shared", g, "shared") for g in (dk, dv, dfl)], 1, 1.0,
                        "kv_dw",
                        stages=lambda: [chip_exchange(ffn1_1), pair_share(["wq.0"])])
    adam(["wq.0"])

    def by_chip(full):
        rows = full.shape[0]
        return full.reshape(rows, N_CHIPS, full.shape[1] // N_CHIPS).transpose(1, 0, 2)

    buf["G_kv.0"] = by_chip(jnp.concatenate([dwk[0], dwv[0], dwf[0][:, :n_heads]], axis=1))
    dr3, dr3b, dg02, db02 = run(
        _nt_sum, [(dk, "cols", w_k, None), (dv, "cols", w_v, None), (dfl, "cols", w_f, None)],
        dh3a, 1.0, 1, f32, "kv_dx", stages=lambda: [pair_exchange(["kv.0"])], ln=(r3, gains[0, 2]))
    prepare(["kv.0"])
    finish(ffn1_1)
    dr2, dr2b, dg01, db01 = ffn_bwd(dr3, dr3b, h2b, a2, b2, s2, "2", 0,
                                    on_act=lambda: [chip_exchange(["kv.0"]), pair_share(ffn1_1)],
                                    after_act=lambda: (adam(ffn1_1), finish(["kv.0"])),
                                    on_dwd=lambda: [pair_share(["kv.0"])],
                                    after_dwd=lambda: adam(["kv.0"]),
                                    on_dx=lambda: [pair_exchange(ffn2_0)], ln=(r2, gains[0, 1]))
    prepare(ffn2_0)
    (dwout,) = _tn_matmul([(z, "shared", dr2b, "shared")], 1, 1.0, "sq_dw")
    buf["G_wout.0"] = dwout.reshape(N_CHIPS, dq, d)
    dz = _nt_sum([(dr2b, "cols", w_out[0], None)], None, 1.0, 1, f32, "sq_dx_f32")
    dp, dconv_w = run(_conv_bwd, dz, p, conv_w_full, "conv_bwd",
                      stages=lambda: [chip_exchange(ffn2_0)])
    (buf["G_win.0"],) = _tn_matmul([(h1b, "shared", dp, "cols")], N_CHIPS, 1.0, "conv_dwin")
    finish(ffn2_0)
    dr1, dr1b, dg00, db00 = run(_nt_sum, [(dp, "cols", w_in, None)], dr2, ALPHA, N_CHIPS, f32,
                                "conv_dx",
                                stages=lambda: [pair_exchange(conv_items), pair_share(ffn2_0)],
                                ln=(r1, gains[0, 0]))
    prepare(conv_items)
    adam(ffn2_0)
    gate_up = ["wg1.0", "wu1.0"]
    da, db = run(_ffn_bwd_act, dr1b, buf["wd1"], 0, a1, b1, "ffn_bwd_act",
                 stages=lambda: [chip_exchange(conv_items)])
    finish(conv_items)
    buf["G_wg1.0"], buf["G_wu1.0"] = run(
        _tn_matmul, [(da, "stack", h0b, "shared"), (db, "stack", h0b, "shared")], N_CHIPS, 1.0,
        "ffn_dwgu", stages=lambda: [pair_share(conv_items)])
    adam(conv_items)
    (buf["G_wd1.0"],) = run(_tn_matmul, [(s1, "stack", dr1b, "shared")], N_CHIPS, 0.5, "ffn_dwd",
                            stages=lambda: [pair_exchange(gate_up)])
    prepare(gate_up)
    dh0 = run(_nt_sum, [(da, "stack", buf["wg1"], 0, True), (db, "stack", buf["wu1"], 0, True)],
              dr1, ALPHA, N_CHIPS, f32, "ffn_dx",
              stages=lambda: [chip_exchange(gate_up), pair_exchange(["wd1.0"])])
    prepare(["wd1.0"])
    finish(gate_up)
    grad_x = dh0[PAD + N_META:][None]
    dmeta = dh0[PAD:PAD + N_META]
    buf["G_small.0"] = by_chip(jnp.concatenate(
        [dmeta, rows8(jnp.concatenate([dg00, dg01, dg02, dg10, dg11, dg12], axis=0)),
         rows8(jnp.concatenate([db00, db01, db02, db10, db11, db12], axis=0)),
         rows8(dconv_w), jnp.zeros((8, d), f32)], axis=0))
    run(None, stages=lambda: [chip_exchange(["wd1.0"]), pair_exchange(["small.0"]), pair_share(gate_up)],
        name="grad_tail_1")
    prepare(["small.0"])
    finish(["wd1.0"])
    adam(gate_up)
    run(None, stages=lambda: [chip_exchange(["small.0"]), pair_share(["wd1.0"])], name="grad_tail_2")
    finish(["small.0"])
    adam(["wd1.0"])
    run(None, stages=lambda: [pair_share(["small.0"])], name="grad_tail_3")

    tail = jnp.zeros((LANES,), f32).at[:n_heads].set(dfb_cols[0, :n_heads]).at[n_heads].set(loss_part)
    tail = lax.psum(tail, ("x", "y", "c"))
    loss = tail[n_heads]
    g_fb = tail[:n_heads]
    g_small = jnp.concatenate([buf["RED_small"][0, :40],
                               rows8(jnp.pad(g_fb, (0, dq - n_heads))[None])], axis=0)
    adam(["small.0"], grads={"small": g_small[None]})

    def unpack(pk):
        return (pk[:16], pk[16:22].reshape(DEPTH, 3, dq), pk[24:30].reshape(DEPTH, 3, dq),
                pk[32:35].reshape(1, 3, dq), pk[40, :n_heads])

    def order(pick):
        mt, g, b, cw, fb = unpack(pick("small")[0])
        big = {n: pick(n) for n in w3 if n != "small"}
        big["kv"] = big["kv"][0]
        for n in transposed:
            big[n] = tr(big[n])
        return [mt, big["wg1"], big["wu1"], big["wd1"], big["wg2"], big["wu2"], big["wd2"], g, b,
                big["win"], cw, big["wout"], big["kv"], fb, big["wq"], big["wo"]]

    return (loss, grad_x, *order(lambda n: stepped[n][3]), *order(lambda n: stepped[n][0]),
            *order(lambda n: stepped[n][1]), *order(lambda n: stepped[n][2]))
```

```python
import functools
import math

import jax
import jax.numpy as jnp
from jax import lax
from jax.experimental import pallas as pl
from jax.experimental.pallas import tpu as pltpu

f32 = jnp.float32
bf16 = jnp.bfloat16

N_META = 16
PAD = 112
HEAD_DIM = 128
DEPTH = 2
LN_EPS = 1e-5
ALPHA = (2 * DEPTH) ** 0.25
NEG_INF = -1e30
N_CHIPS = 4
SMALL_ROWS = 48
LANES = 128

ADAM_LR = 0.001
ADAM_B1 = 0.9
ADAM_B2 = 0.999
ADAM_EPS = 1e-08
ADAM_WD = 0.01
ADAM_STEP = 10

VMEM_LIMIT_BYTES = 56 * 1024 * 1024
ROWS_WIDE = 1664
ROWS_ACC = 1040
ROWS_ACC_LN = 832
STRIP = 16
MESH = pl.DeviceIdType.MESH

NT_DIMS = (((1,), (1,)), ((), ()))
TN_DIMS = (((0,), (0,)), ((), ()))


def _tile(n, target, mult):
    best = None
    for d in range(mult, min(n, target) + 1, mult):
        if n % d == 0:
            best = d
    assert best is not None, (n, target, mult)
    return best


def _params(*sem):
    return pltpu.CompilerParams(dimension_semantics=sem, vmem_limit_bytes=VMEM_LIMIT_BYTES)


class _Place:
    def __init__(self):
        self.cx, self.cy, self.c = lax.axis_index("x"), lax.axis_index("y"), lax.axis_index("c")
        self.chips = [(1 - self.cx, self.cy), (self.cx, 1 - self.cy), (1 - self.cx, 1 - self.cy)]
        self.me = 2 * self.cx + self.cy
        self.slots = [self.me] + [2 * px + py for px, py in self.chips]
        self.sib = (self.cx, self.cy, 1 - self.c)


class _Copies:
    def __init__(self):
        self.srcs, self.bufs, self.news = {}, {}, {}
        self.plans = []
        self.out_bufs, self.out_news = {}, {}

    def add(self, plan, n_copies, srcs=None, bufs=None, news=None):
        for have, more in ((self.srcs, srcs), (self.bufs, bufs), (self.news, news)):
            for key, val in (more or {}).items():
                assert key not in have or have[key] is val, key
                have[key] = val
        self.plans.append((plan, n_copies))

    def empty(self):
        return not self.plans

    def count(self):
        return sum(n for _, n in self.plans)

    def copies(self, src_refs, buf_refs, new_refs, send, recv):
        place = _Place()
        srcs = dict(zip(self.srcs, src_refs))
        bufs = dict(zip(self.bufs, buf_refs))
        news = dict(zip(self.news, new_refs))
        out = []
        for plan, n_copies in self.plans:
            triples = plan(srcs, bufs, news, place)
            assert len(triples) == n_copies
            for src, dst, dev in triples:
                n = len(out)
                out.append(pltpu.make_async_remote_copy(
                    src_ref=src, dst_ref=dst, send_sem=send.at[n], recv_sem=recv.at[n],
                    device_id=dev, device_id_type=MESH))
        return out

    def land(self, results):
        n_b = len(self.bufs)
        self.out_bufs = dict(zip(self.bufs, results[:n_b]))
        self.out_news = dict(zip(self.news, results[n_b:]))


def _pallas(comm, body, *, name, grid, in_specs, out_specs, out_shape, compiler_params,
            scratch_shapes=(), input_output_aliases=None):
    aliases = dict(input_output_aliases or {})
    if comm is None or comm.empty():
        return pl.pallas_call(body, name=name, grid=grid, in_specs=in_specs, out_specs=out_specs,
                              out_shape=out_shape, scratch_shapes=list(scratch_shapes),
                              input_output_aliases=aliases, compiler_params=compiler_params)
    single = not isinstance(out_shape, (list, tuple))
    out_shapes = [out_shape] if single else list(out_shape)
    out_specs_l = [out_specs] if single else list(out_specs)
    n_in, n_out, n_scr = len(in_specs), len(out_shapes), len(scratch_shapes)
    n_s, n_b, n_n = len(comm.srcs), len(comm.bufs), len(comm.news)
    n_copies = comm.count()

    def wrapped(*refs):
        ins = refs[:n_in]
        src_refs = refs[n_in:n_in + n_s]
        o0 = n_in + n_s + n_b
        outs = refs[o0:o0 + n_out]
        buf_refs = refs[o0 + n_out:o0 + n_out + n_b]
        new_refs = refs[o0 + n_out + n_b:o0 + n_out + n_b + n_n]
        rest = refs[o0 + n_out + n_b + n_n:]
        scratch, (send, recv) = rest[:n_scr], rest[n_scr:]
        ids = [pl.program_id(a) for a in range(len(grid))]
        first = functools.reduce(jnp.logical_and, [i == 0 for i in ids])
        last = functools.reduce(jnp.logical_and, [i == g - 1 for i, g in zip(ids, grid)])

        @pl.when(first)
        def _():
            for cp in comm.copies(src_refs, buf_refs, new_refs, send, recv):
                cp.start()

        body(*ins, *outs, *scratch)

        @pl.when(last)
        def _():
            for cp in comm.copies(src_refs, buf_refs, new_refs, send, recv):
                cp.wait()

    hbm = pl.BlockSpec(memory_space=pl.ANY)
    for j in range(n_b):
        aliases[n_in + n_s + j] = n_out + j
    call = pl.pallas_call(
        wrapped, name=name, grid=grid,
        in_specs=[*in_specs, *([hbm] * (n_s + n_b))],
        out_specs=[*out_specs_l, *([hbm] * (n_b + n_n))],
        out_shape=[*out_shapes,
                   *[jax.ShapeDtypeStruct(a.shape, a.dtype) for a in comm.bufs.values()],
                   *comm.news.values()],
        scratch_shapes=[*scratch_shapes, pltpu.SemaphoreType.DMA((n_copies,)),
                        pltpu.SemaphoreType.DMA((n_copies,))],
        input_output_aliases=aliases, compiler_params=compiler_params)

    def run(*args):
        res = call(*args, *comm.srcs.values(), *comm.bufs.values())
        comm.land(res[n_out:])
        return res[0] if single else res[:n_out]

    return run


def _copy_call(comm, name):
    n_s, n_b, n_n = len(comm.srcs), len(comm.bufs), len(comm.news)
    n_copies = comm.count()

    def body(*refs):
        src_refs = refs[:n_s]
        buf_refs = refs[n_s + n_b:n_s + 2 * n_b]
        new_refs = refs[n_s + 2 * n_b:n_s + 2 * n_b + n_n]
        send, recv = refs[n_s + 2 * n_b + n_n:]
        copies = comm.copies(src_refs, buf_refs, new_refs, send, recv)
        for cp in copies:
            cp.start()
        for cp in copies:
            cp.wait()

    hbm = pl.BlockSpec(memory_space=pl.ANY)
    res = pl.pallas_call(
        body, name=name,
        in_specs=[hbm] * (n_s + n_b), out_specs=[hbm] * (n_b + n_n),
        out_shape=[*[jax.ShapeDtypeStruct(a.shape, a.dtype) for a in comm.bufs.values()],
                   *comm.news.values()],
        input_output_aliases={n_s + j: j for j in range(n_b)},
        scratch_shapes=[pltpu.SemaphoreType.DMA((n_copies,)), pltpu.SemaphoreType.DMA((n_copies,))],
    )(*comm.srcs.values(), *comm.bufs.values())
    comm.land(res)


def _embed(meta_pad, x, name, comm=None):
    seq, d = x.shape
    t = seq + LANES

    def body(m_ref, x_ref, h_ref, hb_ref):
        first = pl.program_id(0) == 0
        v = jnp.where(first, m_ref[...], x_ref[...])
        h_ref[...] = v
        hb_ref[...] = v.astype(bf16)

    return _pallas(
        comm, body, name=name, grid=(t // LANES,),
        in_specs=[pl.BlockSpec((LANES, d), lambda i: (0, 0)),
                  pl.BlockSpec((LANES, d), lambda i: (jnp.maximum(i - 1, 0), 0))],
        out_specs=[pl.BlockSpec((LANES, d), lambda i: (i, 0)),
                   pl.BlockSpec((LANES, d), lambda i: (i, 0))],
        out_shape=[jax.ShapeDtypeStruct((t, d), f32), jax.ShapeDtypeStruct((t, d), bf16)],
        compiler_params=_params("parallel"),
    )(meta_pad, x)


def _nn_matmul(x, w, out_dtype, name, comm=None, out_scale=None):
    t, k = x.shape
    s_n, _, n = w.shape
    assert s_n == 1 or n % LANES == 0
    tm = _tile(t, ROWS_WIDE, 16)

    def body(x_ref, w_ref, o_ref):
        res = jnp.dot(x_ref[...].astype(bf16), w_ref[...], preferred_element_type=f32)
        if out_scale is not None:
            res = res * out_scale
        o_ref[...] = res.astype(o_ref.dtype)

    return _pallas(
        comm, body, name=name, grid=(s_n, t // tm),
        in_specs=[pl.BlockSpec((tm, k), lambda s, i: (i, 0)),
                  pl.BlockSpec((None, k, n), lambda s, i: (s, 0, 0))],
        out_specs=pl.BlockSpec((tm, n), lambda s, i: (i, s)),
        out_shape=jax.ShapeDtypeStruct((t, s_n * n), out_dtype),
        compiler_params=_params("parallel", "parallel"),
    )(x, w)


def _ffn_up(hb, wg, wu, layer, name, comm=None):
    t, d = hb.shape
    s_n, _, n, _ = wg.shape
    tm = _tile(t, ROWS_WIDE, 16)

    def body(x_ref, wg_ref, wu_ref, a_ref, b_ref, s_ref):
        x = x_ref[...]
        a = lax.dot_general(x, wg_ref[...], NT_DIMS, preferred_element_type=f32)
        b = lax.dot_general(x, wu_ref[...], NT_DIMS, preferred_element_type=f32)
        a_ref[...] = a.astype(bf16)
        b_ref[...] = b.astype(bf16)
        s_ref[...] = (a * jax.nn.sigmoid(a) * b).astype(bf16)

    wspec = pl.BlockSpec((None, None, n, d), lambda s, i: (s, layer, 0, 0))
    ospec = pl.BlockSpec((None, tm, n), lambda s, i: (s, i, 0))
    return _pallas(
        comm, body, name=name, grid=(s_n, t // tm),
        in_specs=[pl.BlockSpec((tm, d), lambda s, i: (i, 0)), wspec, wspec],
        out_specs=[ospec, ospec, ospec],
        out_shape=[jax.ShapeDtypeStruct((s_n, t, n), bf16)] * 3,
        compiler_params=_params("parallel", "parallel"),
    )(hb, wg, wu)


def _down_ln(x, w, layer, hprev, gain, bias, beta, name, comm=None):
    s_n, t, k = x.shape
    d = w.shape[-1]
    tm = _tile(t, ROWS_ACC, 16)

    def body(x_ref, w_ref, h_ref, g_ref, b_ref, r_out, h_out, hb_out, acc):
        s = pl.program_id(1)

        @pl.when(s == 0)
        def _():
            acc[...] = jnp.zeros_like(acc)

        acc[...] += jnp.dot(x_ref[...], w_ref[...], preferred_element_type=f32)

        @pl.when(s == s_n - 1)
        def _():
            r = ALPHA * h_ref[...] + beta * acc[...]
            mu = jnp.mean(r, axis=-1, keepdims=True)
            xc = r - mu
            var = jnp.mean(xc * xc, axis=-1, keepdims=True)
            y = xc * lax.rsqrt(var + LN_EPS) * g_ref[...] + b_ref[...]
            r_out[...] = r
            h_out[...] = y
            hb_out[...] = y.astype(bf16)

    row = pl.BlockSpec((tm, d), lambda i, s: (i, 0))
    vec = pl.BlockSpec((1, d), lambda i, s: (0, 0))
    return _pallas(
        comm, body, name=name, grid=(t // tm, s_n),
        in_specs=[pl.BlockSpec((None, tm, k), lambda i, s: (s, i, 0)),
                  pl.BlockSpec((None, None, k, d), lambda i, s: (s, layer, 0, 0)),
                  row, vec, vec],
        out_specs=[row, row, row],
        out_shape=[jax.ShapeDtypeStruct((t, d), f32), jax.ShapeDtypeStruct((t, d), f32),
                   jax.ShapeDtypeStruct((t, d), bf16)],
        scratch_shapes=[pltpu.VMEM((tm, d), f32)],
        compiler_params=_params("parallel", "arbitrary"),
    )(x, w, hprev, gain, bias)


def _conv_fwd(p, conv_w, name):
    t, d3 = p.shape
    d = d3 // 3
    tm = _tile(t, 320, 8)
    hb = tm // 8

    def body(p_ref, prev_ref, w_ref, z_ref):
        i = pl.program_id(0)
        rows = i * tm - 8 + lax.broadcasted_iota(jnp.int32, (tm + 8, 1), 0)
        cg = jnp.concatenate([prev_ref[:, d:2 * d], p_ref[:, d:2 * d]], axis=0)
        val = jnp.concatenate([prev_ref[:, 2 * d:], p_ref[:, 2 * d:]], axis=0)
        u = jnp.where(rows >= PAD, cg * val, 0.0)
        y = (w_ref[2:3, :] * u + w_ref[1:2, :] * pltpu.roll(u, 1, 0)
             + w_ref[0:1, :] * pltpu.roll(u, 2, 0))
        z_ref[...] = (p_ref[:, :d] * y[8:]).astype(bf16)

    return pl.pallas_call(
        body, name=name, grid=(t // tm,),
        in_specs=[pl.BlockSpec((tm, d3), lambda i: (i, 0)),
                  pl.BlockSpec((8, d3), lambda i: (jnp.maximum(i * hb - 1, 0), 0)),
                  pl.BlockSpec((3, d), lambda i: (0, 0))],
        out_specs=pl.BlockSpec((tm, d), lambda i: (i, 0)),
        out_shape=jax.ShapeDtypeStruct((t, d), bf16),
        compiler_params=_params("parallel"),
    )(p, p, conv_w)


def _row_scan(x, name, fbias=None, reverse=False):
    t, n = x.shape
    blk = _tile(t, 640, LANES)
    n_blk = t // blk
    gate = fbias is not None

    def body(*refs):
        if gate:
            x_ref, fb_ref, o_ref, carry = refs
        else:
            x_ref, o_ref, carry = refs
        i = pl.program_id(0)

        @pl.when(i == 0)
        def _():
            carry[...] = jnp.zeros_like(carry)

        v = x_ref[...]
        r = lax.broadcasted_iota(jnp.int32, (blk, n), 0)
        if gate:
            v = v + fb_ref[...]
            v = jnp.minimum(v, 0.0) - jnp.log1p(jnp.exp(-jnp.abs(v)))
            v = jnp.where(i * blk + r >= PAD, v, 0.0)
        sh = 1
        while sh < blk:
            if reverse:
                v = v + jnp.where(r < blk - sh, pltpu.roll(v, blk - sh, 0), 0.0)
            else:
                v = v + jnp.where(r >= sh, pltpu.roll(v, sh, 0), 0.0)
            sh *= 2
        v = v + carry[...]
        o_ref[...] = v
        carry[...] = o_ref[0:1, :] if reverse else o_ref[blk - 1:blk, :]

    order = (lambda i: (n_blk - 1 - i, 0)) if reverse else (lambda i: (i, 0))
    in_specs = [pl.BlockSpec((blk, n), order)]
    args = [x]
    if gate:
        in_specs.append(pl.BlockSpec((1, n), lambda i: (0, 0)))
        args.append(fbias)
    return pl.pallas_call(
        body, name=name, grid=(n_blk,),
        in_specs=in_specs,
        out_specs=pl.BlockSpec((blk, n), order),
        out_shape=jax.ShapeDtypeStruct((t, n), f32),
        scratch_shapes=[pltpu.VMEM((1, n), f32)],
        compiler_params=_params("arbitrary"),
    )(*args)


def _lanes(x, n):
    return jnp.concatenate([x] * (n // LANES), axis=1)


def _attn_fwd(q, k, v, cq_rep, ck_rows, name):
    t, d = q.shape
    n_heads = d // HEAD_DIM
    bk = ck_rows.shape[-1]
    bq = bk

    def lane_fold(x, op):
        out = x[:, :LANES]
        for c0 in range(LANES, bk, LANES):
            out = op(out, x[:, c0:c0 + LANES])
        return out

    def body(q_ref, k_ref, v_ref, cq_ref, ck_ref, o_ref, o32_ref, cl_ref,
             s_scr, p_scr, m_scr, l_scr, red_scr, acc_scr):
        i = pl.program_id(1)
        m_scr[...] = jnp.full_like(m_scr, NEG_INF)
        l_scr[...] = jnp.zeros_like(l_scr)
        acc_scr[...] = jnp.zeros_like(acc_scr)
        qb = q_ref[...]
        ahead = (lax.broadcasted_iota(jnp.int32, (STRIP, bk), 1)
                 - lax.broadcasted_iota(jnp.int32, (STRIP, bk), 0))

        def tile(j, diagonal):
            k0 = pl.multiple_of(j * bk, bk)
            s_scr[...] = lax.dot_general(qb, k_ref[pl.ds(k0, bk), :], NT_DIMS,
                                         preferred_element_type=f32)
            ck = ck_ref[j]
            for r in range(0, bq, STRIP):
                rows = slice(r, r + STRIP)
                s = s_scr[rows, :] + _lanes(cq_ref[rows, :], bk) - ck
                if diagonal:
                    s = jnp.where(ahead <= r, s, NEG_INF)
                s_scr[rows, :] = s
                red_scr[rows, :] = lane_fold(s, jnp.maximum)
            m_old = m_scr[...]
            m_new = jnp.maximum(m_old, jnp.broadcast_to(
                jnp.max(red_scr[...], axis=1, keepdims=True), (bq, LANES)))
            a = jnp.exp(m_old - m_new)
            m_scr[...] = m_new
            for r in range(0, bq, STRIP):
                rows = slice(r, r + STRIP)
                pr = jnp.exp(s_scr[rows, :] - _lanes(m_scr[rows, :], bk))
                red_scr[rows, :] = lane_fold(pr, jnp.add)
                p_scr[rows, :] = pr.astype(bf16)
            l_scr[...] = a * l_scr[...] + jnp.broadcast_to(
                jnp.sum(red_scr[...], axis=1, keepdims=True), (bq, LANES))
            acc_scr[...] = a * acc_scr[...] + jnp.dot(
                p_scr[...], v_ref[pl.ds(k0, bk), :], preferred_element_type=f32)

        def full_tile(j, carry):
            tile(j, False)
            return carry

        lax.fori_loop(0, i, full_tile, 0)
        tile(i, True)
        out = acc_scr[...] / l_scr[...]
        o_ref[...] = out.astype(bf16)
        o32_ref[...] = out
        red_scr[...] = cq_ref[...] - (m_scr[...] + jnp.log(l_scr[...]))
        for c0 in range(0, bq, LANES):
            cl_ref[:, c0:c0 + LANES] = red_scr[c0:c0 + LANES, :].T[0:1, :]

    qblk = pl.BlockSpec((bq, HEAD_DIM), lambda h, i: (i, h))
    head_rows = pl.BlockSpec((t, HEAD_DIM), lambda h, i: (0, h))
    rep = pl.BlockSpec((None, bq, LANES), lambda h, i: (h, i, 0))
    col = pltpu.VMEM((bq, LANES), f32)
    return pl.pallas_call(
        body, name=name, grid=(n_heads, t // bq),
        in_specs=[qblk, head_rows, head_rows, rep,
                  pl.BlockSpec((None, t // bk, 1, bk), lambda h, i: (h, 0, 0, 0))],
        out_specs=[qblk, qblk, pl.BlockSpec((None, None, 1, bq), lambda h, i: (h, i, 0, 0))],
        out_shape=[jax.ShapeDtypeStruct((t, d), bf16), jax.ShapeDtypeStruct((t, d), f32),
                   jax.ShapeDtypeStruct((n_heads, t // bq, 1, bq), f32)],
        scratch_shapes=[pltpu.VMEM((bq, bk), f32), pltpu.VMEM((bq, bk), bf16), col, col, col,
                        pltpu.VMEM((bq, HEAD_DIM), f32)],
        compiler_params=_params("parallel", "parallel"),
    )(q, k, v, cq_rep, ck_rows)


def _loss_head(h, target, r, gain, name):
    t, d = h.shape

    def body(h_ref, t_ref, r_ref, g_ref, dr_ref, drb_ref, dg_ref, db_ref, loss_ref):
        i = pl.program_id(0)

        @pl.when(i == 0)
        def _():
            loss_ref[...] = jnp.zeros_like(loss_ref)
            dg_ref[...] = jnp.zeros_like(dg_ref)
            db_ref[...] = jnp.zeros_like(db_ref)

        diff = jnp.where(i >= 1, h_ref[...] - t_ref[...], 0.0)
        loss_ref[...] += jnp.sum(diff * diff)
        dr, dg, db = _ln_bwd_rows(diff * (1.0 / d), r_ref[...], g_ref[...])
        dr_ref[...] = dr
        drb_ref[...] = dr.astype(bf16)
        dg_ref[...] += dg
        db_ref[...] += db

    row = pl.BlockSpec((LANES, d), lambda i: (i, 0))
    vec = pl.BlockSpec((1, d), lambda i: (0, 0))
    return pl.pallas_call(
        body, name=name, grid=(t // LANES,),
        in_specs=[row, pl.BlockSpec((LANES, d), lambda i: (jnp.maximum(i - 1, 0), 0)), row, vec],
        out_specs=[row, row, vec, vec, pl.BlockSpec((1, LANES), lambda i: (0, 0))],
        out_shape=[jax.ShapeDtypeStruct((t, d), f32), jax.ShapeDtypeStruct((t, d), bf16),
                   jax.ShapeDtypeStruct((1, d), f32), jax.ShapeDtypeStruct((1, d), f32),
                   jax.ShapeDtypeStruct((1, LANES), f32)],
        compiler_params=_params("arbitrary"),
    )(h, target, r, gain)


def _ln_bwd_rows(dy, rr, gain):
    mu = jnp.mean(rr, axis=-1, keepdims=True)
    xc = rr - mu
    var = jnp.mean(xc * xc, axis=-1, keepdims=True)
    rstd = lax.rsqrt(var + LN_EPS)
    xhat = xc * rstd
    dxh = dy * gain
    m1 = jnp.mean(dxh, axis=-1, keepdims=True)
    m2 = jnp.mean(dxh * xhat, axis=-1, keepdims=True)
    dr = rstd * (dxh - m1 - xhat * m2)
    return dr, jnp.sum(dy * xhat, axis=0, keepdims=True), jnp.sum(dy, axis=0, keepdims=True)


def _ffn_bwd_act(drb, wd, layer, a, b, name, comm=None):
    t, d = drb.shape
    s_n, _, n = a.shape
    tm = _tile(t, ROWS_WIDE, 16)

    def body(dr_ref, w_ref, a_ref, b_ref, da_ref, db_ref):
        ds = 0.5 * lax.dot_general(dr_ref[...], w_ref[...], NT_DIMS, preferred_element_type=f32)
        da_ref[...], db_ref[...] = _swiglu_bwd(ds, a_ref, b_ref)

    act = pl.BlockSpec((None, tm, n), lambda s, i: (s, i, 0))
    return _pallas(
        comm, body, name=name, grid=(s_n, t // tm),
        in_specs=[pl.BlockSpec((tm, d), lambda s, i: (i, 0)),
                  pl.BlockSpec((None, None, n, d), lambda s, i: (s, layer, 0, 0)), act, act],
        out_specs=[act, act],
        out_shape=[jax.ShapeDtypeStruct((s_n, t, n), bf16), jax.ShapeDtypeStruct((s_n, t, n), bf16)],
        compiler_params=_params("parallel", "parallel"),
    )(drb, wd, a, b)


def _swiglu_bwd(ds, a_ref, b_ref):
    av = a_ref[...].astype(f32)
    sig = jax.nn.sigmoid(av)
    da = ds * b_ref[...].astype(f32) * (sig * (1.0 + av * (1.0 - sig)))
    return da.astype(bf16), (ds * (av * sig)).astype(bf16)


def _act_spec(mode, tt, k, t_first):
    def fix(fn):
        return (lambda i, s: fn(s, i)) if t_first else fn
    if mode == "shared":
        return pl.BlockSpec((tt, k), fix(lambda s, i: (i, 0)))
    if mode == "cols":
        return pl.BlockSpec((tt, k), fix(lambda s, i: (i, s)))
    assert mode == "stack"
    return pl.BlockSpec((None, tt, k), fix(lambda s, i: (s, i, 0)))


def _act_width(arr, mode, s_n):
    return arr.shape[-1] // s_n if mode == "cols" else arr.shape[-1]


def _tn_matmul(pairs, s_n, scale, name, comm=None):
    t = pairs[0][0].shape[-2]
    tt = _tile(t, 2080, 16)
    n_t = t // tt
    arrays, specs, where = [], [], []
    for x, xmode, y, ymode in pairs:
        pos = []
        for arr, mode in ((x, xmode), (y, ymode)):
            hit = [j for j, a in enumerate(arrays) if a is arr]
            if not hit:
                arrays.append(arr)
                specs.append(_act_spec(mode, tt, _act_width(arr, mode, s_n), False))
                hit = [len(arrays) - 1]
            pos.append(hit[0])
        where.append(pos)
    widths = [(_act_width(x, xm, s_n), _act_width(y, ym, s_n)) for x, xm, y, ym in pairs]
    n_a = len(arrays)

    def body(*refs):
        i = pl.program_id(1)
        for (px, py), o_ref in zip(where, refs[n_a:]):
            part = lax.dot_general(refs[px][...].astype(bf16), refs[py][...].astype(bf16), TN_DIMS,
                                   preferred_element_type=f32)

            @pl.when(i == 0)
            def _():
                o_ref[...] = part

            @pl.when(i > 0)
            def _():
                o_ref[...] += part

            if scale != 1.0:
                @pl.when(i == n_t - 1)
                def _():
                    o_ref[...] = o_ref[...] * scale

    return _pallas(
        comm, body, name=name, grid=(s_n, n_t),
        in_specs=specs,
        out_specs=[pl.BlockSpec((None, kx, ky), lambda s, i: (s, 0, 0)) for kx, ky in widths],
        out_shape=[jax.ShapeDtypeStruct((s_n, kx, ky), f32) for kx, ky in widths],
        compiler_params=_params("parallel", "arbitrary"),
    )(*arrays)


def _nt_sum(pairs, base, base_scale, s_n, out_dtype, name, comm=None, ln=None):
    t = pairs[0][0].shape[-2]
    pairs = [(*pr, False)[:5] for pr in pairs]
    d = pairs[0][2].shape[-1] if pairs[0][4] else pairs[0][2].shape[-2]
    tm = _tile(t, ROWS_ACC if ln is None else ROWS_ACC_LN, 16)
    n_p = len(pairs)
    has_base = base is not None
    flipped = [pr[4] for pr in pairs]
    n_out = 1 if ln is None else 4

    def body(*refs):
        dy_refs = refs[0:2 * n_p:2]
        w_refs = refs[1:2 * n_p:2]
        rest = refs[2 * n_p:]
        base_ref = rest[0] if has_base else None
        o_refs, acc = rest[-1 - n_out:-1], rest[-1]
        s = pl.program_id(1)

        if ln is not None:
            @pl.when(jnp.logical_and(pl.program_id(0) == 0, s == 0))
            def _():
                o_refs[2][...] = jnp.zeros_like(o_refs[2])
                o_refs[3][...] = jnp.zeros_like(o_refs[3])

        @pl.when(s == 0)
        def _():
            acc[...] = jnp.zeros_like(acc)

        tot = None
        for dy_ref, w_ref, flip in zip(dy_refs, w_refs, flipped):
            dyv = dy_ref[...].astype(bf16)
            if flip:
                part = jnp.dot(dyv, w_ref[...], preferred_element_type=f32)
            else:
                part = lax.dot_general(dyv, w_ref[...], NT_DIMS, preferred_element_type=f32)
            tot = part if tot is None else tot + part
        acc[...] += tot

        @pl.when(s == s_n - 1)
        def _():
            res = acc[...]
            if has_base:
                res = base_scale * base_ref[...] + res
            if ln is None:
                o_refs[0][...] = res.astype(o_refs[0].dtype)
            else:
                r_ref, g_ref = rest[-7], rest[-6]
                dr, dg, db = _ln_bwd_rows(res, r_ref[...], g_ref[...])
                o_refs[0][...] = dr
                o_refs[1][...] = dr.astype(bf16)
                o_refs[2][...] += dg
                o_refs[3][...] += db

    in_specs, args = [], []
    for dy, mode, w, layer, flip in pairs:
        k = _act_width(dy, mode, s_n)
        in_specs.append(_act_spec(mode, tm, k, True))
        wshape = (k, d) if flip else (d, k)
        if layer is None:
            in_specs.append(pl.BlockSpec((None, *wshape), lambda i, s: (s, 0, 0)))
        else:
            in_specs.append(pl.BlockSpec((None, None, *wshape),
                                         functools.partial(lambda i, s, l: (s, l, 0, 0), l=layer)))
        args += [dy, w]
    row = pl.BlockSpec((tm, d), lambda i, s: (i, 0))
    vec = pl.BlockSpec((1, d), lambda i, s: (0, 0))
    if has_base:
        in_specs.append(row)
        args.append(base)
    if ln is None:
        out_specs = row
        out_shape = jax.ShapeDtypeStruct((t, d), out_dtype)
    else:
        in_specs += [row, vec]
        args += list(ln)
        out_specs = [row, row, vec, vec]
        out_shape = [jax.ShapeDtypeStruct((t, d), f32), jax.ShapeDtypeStruct((t, d), bf16),
                     jax.ShapeDtypeStruct((1, d), f32), jax.ShapeDtypeStruct((1, d), f32)]
    return _pallas(
        comm, body, name=name, grid=(t // tm, s_n),
        in_specs=in_specs, out_specs=out_specs, out_shape=out_shape,
        scratch_shapes=[pltpu.VMEM((tm, d), f32)],
        compiler_params=_params("parallel" if ln is None else "arbitrary", "arbitrary"),
    )(*args)


def _conv_bwd(dz, p, conv_w, name, comm=None):
    t, d3 = p.shape
    d = d3 // 3
    tm = _tile(t, 320, 8)
    hb = tm // 8
    last8 = t // 8 - 1
    n_ext = tm + 8

    def body(dz_ref, dzn_ref, p_ref, pp_ref, pn_ref, w_ref, dp_ref, dw_ref):
        i = pl.program_id(0)

        @pl.when(i == 0)
        def _():
            dw_ref[...] = jnp.zeros_like(dw_ref)

        w0, w1, w2 = w_ref[0:1, :], w_ref[1:2, :], w_ref[2:3, :]
        rows_u = i * tm - 8 + lax.broadcasted_iota(jnp.int32, (n_ext, 1), 0)
        cg = jnp.concatenate([pp_ref[:, d:2 * d], p_ref[:, d:2 * d]], axis=0)
        val = jnp.concatenate([pp_ref[:, 2 * d:], p_ref[:, 2 * d:]], axis=0)
        u = jnp.where(rows_u >= PAD, cg * val, 0.0)
        u1 = pltpu.roll(u, 1, 0)
        u2 = pltpu.roll(u, 2, 0)
        y = (w2 * u + w1 * u1 + w0 * u2)[8:]
        dzv = dz_ref[...]
        bg = p_ref[:, :d]
        rows_n = (i + 1) * tm + lax.broadcasted_iota(jnp.int32, (8, 1), 0)
        dy_main = dzv * bg
        dy_next = jnp.where(rows_n < t, dzn_ref[...] * pn_ref[:, :d], 0.0)
        dye = jnp.concatenate([dy_main, dy_next], axis=0)
        du = (w2 * dye + w1 * pltpu.roll(dye, n_ext - 1, 0)
              + w0 * pltpu.roll(dye, n_ext - 2, 0))[:tm]
        du = jnp.where(rows_u[8:] >= PAD, du, 0.0)
        dp_ref[:, :d] = (dzv * y).astype(bf16)
        dp_ref[:, d:2 * d] = (du * val[8:]).astype(bf16)
        dp_ref[:, 2 * d:] = (du * cg[8:]).astype(bf16)
        dw_ref[0:1, :] += jnp.sum(dy_main * u2[8:], axis=0, keepdims=True)
        dw_ref[1:2, :] += jnp.sum(dy_main * u1[8:], axis=0, keepdims=True)
        dw_ref[2:3, :] += jnp.sum(dy_main * u[8:], axis=0, keepdims=True)

    nxt = lambda i: (jnp.minimum((i + 1) * hb, last8), 0)
    return _pallas(
        comm, body, name=name, grid=(t // tm,),
        in_specs=[pl.BlockSpec((tm, d), lambda i: (i, 0)),
                  pl.BlockSpec((8, d), nxt),
                  pl.BlockSpec((tm, d3), lambda i: (i, 0)),
                  pl.BlockSpec((8, d3), lambda i: (jnp.maximum(i * hb - 1, 0), 0)),
                  pl.BlockSpec((8, d3), nxt),
                  pl.BlockSpec((3, d), lambda i: (0, 0))],
        out_specs=[pl.BlockSpec((tm, d3), lambda i: (i, 0)),
                   pl.BlockSpec((3, d), lambda i: (0, 0))],
        out_shape=[jax.ShapeDtypeStruct((t, d3), bf16), jax.ShapeDtypeStruct((3, d), f32)],
        compiler_params=_params("arbitrary"),
    )(dz, dz, p, p, p, conv_w)


def _attn_do(drb, w_o, o, bq, name, comm=None):
    t, d = drb.shape
    n_heads = d // HEAD_DIM

    def body(dr_ref, w_ref, o_ref, do_ref, delta_ref):
        do = lax.dot_general(dr_ref[...], w_ref[...], NT_DIMS, preferred_element_type=f32).astype(bf16)
        do_ref[...] = do
        prod = o_ref[...] * do.astype(f32)
        for h in range(n_heads):
            for c0 in range(0, bq, LANES):
                blk = prod[c0:c0 + LANES, h * HEAD_DIM:(h + 1) * HEAD_DIM]
                delta_ref[h, :, c0:c0 + LANES] = jnp.sum(blk.T, axis=0, keepdims=True)

    row = pl.BlockSpec((bq, d), lambda i: (i, 0))
    return _pallas(
        comm, body, name=name, grid=(t // bq,),
        in_specs=[row, pl.BlockSpec((d, d), lambda i: (0, 0)), row],
        out_specs=[row, pl.BlockSpec((n_heads, None, 1, bq), lambda i: (0, i, 0, 0))],
        out_shape=[jax.ShapeDtypeStruct((t, d), bf16),
                   jax.ShapeDtypeStruct((n_heads, t // bq, 1, bq), f32)],
        compiler_params=_params("parallel"),
    )(drb, w_o, o)


def _attn_bwd(q, k, v, kt, do, ckey, cl_rows, delta_rows, name, comm=None):
    t, d = q.shape
    n_heads = d // HEAD_DIM
    bk = kt.shape[-1]
    bq = bk
    n_kv = t // bk
    n_q = t // bq
    scale = 1.0 / math.sqrt(HEAD_DIM)

    def body(q_ref, do_ref, cl_ref, dl_ref, k_ref, v_ref, kt_ref, ck_ref,
             dq_ref, dcq_ref, dk_ref, dv_ref, dck_ref,
             st_scr, dp_scr, p_scr, ds_scr, dqt, dk_acc, dv_acc, dck_acc):
        j = pl.program_id(1)

        @pl.when(j == 0)
        def _():
            dqt[...] = jnp.zeros_like(dqt)
            dcq_ref[...] = jnp.zeros_like(dcq_ref)

        dk_acc[...] = jnp.zeros_like(dk_acc)
        dv_acc[...] = jnp.zeros_like(dv_acc)
        dck_acc[...] = jnp.zeros_like(dck_acc)
        kb = k_ref[...]
        vb = v_ref[...]
        behind = (lax.broadcasted_iota(jnp.int32, (STRIP, bq), 1)
                  - lax.broadcasted_iota(jnp.int32, (STRIP, bq), 0))

        def tile(i, diagonal):
            r0 = pl.multiple_of(i * bq, bq)
            qi = q_ref[pl.ds(r0, bq), :]
            doi = do_ref[pl.ds(r0, bq), :]
            st_scr[...] = lax.dot_general(kb, qi, NT_DIMS, preferred_element_type=f32)
            dp_scr[...] = lax.dot_general(vb, doi, NT_DIMS, preferred_element_type=f32)
            cl = cl_ref[i]
            dl = dl_ref[i]
            over_keys = jnp.zeros((STRIP, bq), f32)
            for r in range(0, bk, STRIP):
                keys = slice(r, r + STRIP)
                st = st_scr[keys, :] + cl - _lanes(ck_ref[keys, :], bq)
                if diagonal:
                    st = jnp.where(behind >= r, st, NEG_INF)
                pr = jnp.exp(st)
                ds = pr * (dp_scr[keys, :] - dl)
                over_keys = over_keys + ds
                dck_acc[keys, :] -= jnp.sum(ds, axis=1, keepdims=True)
                p_scr[keys, :] = pr.astype(bf16)
                ds_scr[keys, :] = ds.astype(bf16)
            dcq_ref[i] += jnp.sum(over_keys, axis=0, keepdims=True)
            dv_acc[...] += jnp.dot(p_scr[...], doi, preferred_element_type=f32)
            dk_acc[...] += jnp.dot(ds_scr[...], qi, preferred_element_type=f32)
            dqt[i] += jnp.dot(kt_ref[...], ds_scr[...], preferred_element_type=f32)

        def full_tile(i, carry):
            tile(i, False)
            return carry

        tile(j, True)
        lax.fori_loop(j + 1, n_q, full_tile, 0)
        dk_ref[...] = dk_acc[...].astype(bf16)
        dv_ref[...] = dv_acc[...].astype(bf16)
        for c0 in range(0, bk, LANES):
            keys = slice(c0, c0 + LANES)
            dck_ref[:, keys] = jnp.broadcast_to(dck_acc[keys, :], (LANES, LANES)).T[0:1, :]

        @pl.when(j == n_kv - 1)
        def _():
            def emit(i, carry):
                r0 = pl.multiple_of(i * bq, bq)
                dq_ref[pl.ds(r0, bq), :] = dqt[i].T * scale
                return carry
            lax.fori_loop(0, n_q, emit, 0)

    head_rows = pl.BlockSpec((t, HEAD_DIM), lambda h, j: (0, h))
    head_stat = pl.BlockSpec((None, n_q, 1, bq), lambda h, j: (h, 0, 0, 0))
    kblk = pl.BlockSpec((bk, HEAD_DIM), lambda h, j: (j, h))
    return _pallas(
        comm, body, name=name, grid=(n_heads, n_kv),
        in_specs=[head_rows, head_rows, head_stat, head_stat, kblk, kblk,
                  pl.BlockSpec((None, None, HEAD_DIM, bk), lambda h, j: (h, j, 0, 0)),
                  pl.BlockSpec((None, bk, LANES), lambda h, j: (h, j, 0))],
        out_specs=[head_rows, head_stat, kblk, kblk,
                   pl.BlockSpec((None, None, 1, bk), lambda h, j: (h, j, 0, 0))],
        out_shape=[jax.ShapeDtypeStruct((t, d), f32),
                   jax.ShapeDtypeStruct((n_heads, n_q, 1, bq), f32),
                   jax.ShapeDtypeStruct((t, d), bf16), jax.ShapeDtypeStruct((t, d), bf16),
                   jax.ShapeDtypeStruct((n_heads, n_kv, 1, bk), f32)],
        scratch_shapes=[pltpu.VMEM((bk, bq), f32), pltpu.VMEM((bk, bq), f32),
                        pltpu.VMEM((bk, bq), bf16), pltpu.VMEM((bk, bq), bf16),
                        pltpu.VMEM((n_q, HEAD_DIM, bq), f32),
                        pltpu.VMEM((bk, HEAD_DIM), f32), pltpu.VMEM((bk, HEAD_DIM), f32),
                        pltpu.VMEM((bk, 1), f32)],
        compiler_params=_params("parallel", "arbitrary"),
    )(q, do, cl_rows, delta_rows, k, v, kt, ckey)


def _fgate_bwd(dlogf, flog, fbias, name):
    t, n = flog.shape
    rows = _tile(t, 640, LANES)

    def body(dl_ref, fl_ref, fb_ref, o_ref, sum_ref):
        i = pl.program_id(0)

        @pl.when(i == 0)
        def _():
            sum_ref[...] = jnp.zeros_like(sum_ref)

        r = i * rows + lax.broadcasted_iota(jnp.int32, (rows, n), 0)
        g = dl_ref[...] * jax.nn.sigmoid(-(fl_ref[...] + fb_ref[...]))
        g = jnp.where(r >= PAD, g, 0.0)
        o_ref[...] = g
        sum_ref[...] += jnp.sum(g, axis=0, keepdims=True)

    blk = pl.BlockSpec((rows, n), lambda i: (i, 0))
    vec = pl.BlockSpec((1, n), lambda i: (0, 0))
    return pl.pallas_call(
        body, name=name, grid=(t // rows,),
        in_specs=[blk, blk, vec], out_specs=[blk, vec],
        out_shape=[jax.ShapeDtypeStruct((t, n), f32), jax.ShapeDtypeStruct((1, n), f32)],
        compiler_params=_params("arbitrary"),
    )(dlogf, flog, fbias)


def _place_shard(w, idx, name):
    n_l, r, c_n = w.shape
    out_dtype = bf16 if r * c_n > 2 ** 16 else w.dtype
    tr = _tile(r, 512, 16) if r % 16 == 0 else r

    def body(idx_ref, w_ref, o_ref):
        o_ref[...] = w_ref[...].astype(out_dtype)

    grid_spec = pltpu.PrefetchScalarGridSpec(
        num_scalar_prefetch=1, grid=(n_l, r // tr),
        in_specs=[pl.BlockSpec((None, tr, c_n), lambda l, i, idx: (l, i, 0))],
        out_specs=pl.BlockSpec((None, None, tr, c_n), lambda l, i, idx: (idx[0], l, i, 0)))
    return pl.pallas_call(
        body, name=name, grid_spec=grid_spec,
        out_shape=jax.ShapeDtypeStruct((N_CHIPS, n_l, r, c_n), out_dtype),
        compiler_params=_params("parallel", "parallel"),
    )(idx, w)


def _plan_gather_ici(items):
    def plan(srcs, bufs, news, p):
        out = []
        for name, layer, r2 in items:
            mine = bufs[name].at[p.me, layer, pl.ds(p.c * r2, r2)]
            out += [(mine, mine, (*chip, p.c)) for chip in p.chips]
        return out
    return plan, 3 * len(items)


def _plan_gather_d2d(items):
    def plan(srcs, bufs, news, p):
        out = []
        for name, layer, r2 in items:
            for px, py in p.chips:
                landed = bufs[name].at[2 * px + py, layer, pl.ds(p.c * r2, r2)]
                out.append((landed, landed, p.sib))
        return out
    return plan, 3 * len(items)


def _plan_pair_exchange(names, r2s):
    def plan(srcs, bufs, news, p):
        out = []
        for name, r2 in zip(names, r2s):
            for k, slot in enumerate(p.slots):
                out.append((srcs["G_" + name].at[slot, pl.ds((1 - p.c) * r2, r2)],
                            news["PAIR_" + name].at[k], p.sib))
        return out
    return plan, 4 * len(names)


def _plan_chip_exchange(names):
    def plan(srcs, bufs, news, p):
        out = []
        for name in names:
            for k, chip in enumerate(p.chips):
                out.append((srcs["SEND_" + name].at[k], news["RECV_" + name].at[k], (*chip, p.c)))
        return out
    return plan, 3 * len(names)


def _plan_pair_share(items):
    def plan(srcs, bufs, news, p):
        out = []
        for name, layer, r2 in items:
            mine = bufs["RED_" + name].at[layer, pl.ds(p.c * r2, r2)]
            out.append((mine, mine, p.sib))
        return out
    return plan, len(items)


def _rs_prepare(gs, pairs, idx, name):
    n = len(gs)
    _, r2, c_n = pairs[0].shape
    tr = _tile(r2, 256, 8)
    nb = r2 // tr

    def body(idx_ref, *refs):
        for g_ref, p_ref, o_ref in zip(refs[:n], refs[n:2 * n], refs[2 * n:]):
            o_ref[...] = (g_ref[...] + p_ref[...]).astype(bf16)

    g_spec = pl.BlockSpec((None, tr, c_n), lambda k, i, idx: (idx[k + 1], idx[4] * nb + i, 0))
    p_spec = pl.BlockSpec((None, tr, c_n), lambda k, i, idx: (k + 1, i, 0))
    grid_spec = pltpu.PrefetchScalarGridSpec(
        num_scalar_prefetch=1, grid=(3, nb), in_specs=[g_spec] * n + [p_spec] * n,
        out_specs=[pl.BlockSpec((None, tr, c_n), lambda k, i, idx: (k, i, 0))] * n)
    return pl.pallas_call(
        body, name=name, grid_spec=grid_spec,
        out_shape=[jax.ShapeDtypeStruct((3, r2, c_n), bf16)] * n,
        compiler_params=_params("parallel", "parallel"),
    )(idx, *gs, *pairs)


def _rs_finish(gs, pairs, recvs, idx, layer, n_layers, intos, name):
    n = len(gs)
    _, r2, c_n = pairs[0].shape
    tr = _tile(r2, 256, 8)
    nb = r2 // tr

    def body(idx_ref, *refs):
        for j in range(n):
            g_ref, p_ref, r0_ref, r1_ref, r2_ref = refs[5 * j:5 * j + 5]
            acc = g_ref[...] + p_ref[...]
            acc = acc + r0_ref[...].astype(f32)
            acc = acc + r1_ref[...].astype(f32)
            acc = acc + r2_ref[...].astype(f32)
            refs[len(refs) - n + j][...] = acc

    def rspec(k):
        return pl.BlockSpec((None, tr, c_n), functools.partial(lambda i, idx, kk: (kk, i, 0), kk=k))

    item_specs = [pl.BlockSpec((None, tr, c_n), lambda i, idx: (idx[0], idx[4] * nb + i, 0)),
                  rspec(0), rspec(0), rspec(1), rspec(2)]
    in_specs = item_specs * n
    args = [idx]
    for g, pair, recv in zip(gs, pairs, recvs):
        args += [g, pair, recv, recv, recv]
    aliases = {}
    if intos is not None:
        in_specs = in_specs + [pl.BlockSpec(memory_space=pl.ANY)] * n
        args += list(intos)
        aliases = {1 + 5 * n + j: j for j in range(n)}
    grid_spec = pltpu.PrefetchScalarGridSpec(
        num_scalar_prefetch=1, grid=(nb,), in_specs=in_specs,
        out_specs=[pl.BlockSpec((None, tr, c_n), lambda i, idx: (layer, idx[4] * nb + i, 0))] * n)
    return pl.pallas_call(
        body, name=name, grid_spec=grid_spec,
        out_shape=[jax.ShapeDtypeStruct((n_layers, 2 * r2, c_n), f32)] * n,
        input_output_aliases=aliases,
        compiler_params=_params("parallel"),
    )(*args)


def _adamw(ws, gs, ms, vs, layer, intos, name):
    n = len(ws)
    n_l, r, c_n = ws[0].shape
    tr = _tile(r, 256, 8)

    def body(*refs):
        for j in range(n):
            w_ref, g_ref, m_ref, v_ref = refs[4 * j:4 * j + 4]
            o0 = len(refs) - 4 * n + 4 * j
            d_ref, mo_ref, vo_ref, go_ref = refs[o0:o0 + 4]
            gv = g_ref[...]
            go_ref[...] = gv
            mn = ADAM_B1 * m_ref[...] + (1.0 - ADAM_B1) * gv
            vn = ADAM_B2 * v_ref[...] + (1.0 - ADAM_B2) * (gv * gv)
            m_hat = mn / (1.0 - ADAM_B1 ** ADAM_STEP)
            v_hat = vn / (1.0 - ADAM_B2 ** ADAM_STEP)
            d_ref[...] = -ADAM_LR * (m_hat / (jnp.sqrt(v_hat) + ADAM_EPS) + ADAM_WD * w_ref[...])
            mo_ref[...] = mn
            vo_ref[...] = vn

    blk = pl.BlockSpec((None, tr, c_n), lambda i: (layer, i, 0))
    shp = jax.ShapeDtypeStruct((n_l, r, c_n), f32)
    in_specs = [blk] * (4 * n)
    args = []
    for w, g, m, v in zip(ws, gs, ms, vs):
        args += [w, g, m, v]
    aliases = {}
    if intos is not None:
        in_specs = in_specs + [pl.BlockSpec(memory_space=pl.ANY)] * (4 * n)
        for into in intos:
            args += list(into)
        aliases = {4 * n + j: j for j in range(4 * n)}
    res = pl.pallas_call(
        body, name=name, grid=(r // tr,),
        in_specs=in_specs, out_specs=[blk] * (4 * n), out_shape=[shp] * (4 * n),
        input_output_aliases=aliases,
        compiler_params=_params("parallel"),
    )(*args)
    return [tuple(res[4 * j:4 * j + 4]) for j in range(n)]


def kernel(x, meta, ffn1_wg, ffn1_wu, ffn1_wd, ffn2_wg, ffn2_wu, ffn2_wd, ln_gain, ln_bias, conv_w_in, conv_w, conv_w_out, kv_w, f_bias, attn_w_q, attn_w_o, loss_target, m_meta, m_ffn1_wg, m_ffn1_wu, m_ffn1_wd, m_ffn2_wg, m_ffn2_wu, m_ffn2_wd, m_ln_gain, m_ln_bias, m_conv_w_in, m_conv_w, m_conv_w_out, m_kv_w, m_f_bias, m_attn_w_q, m_attn_w_o, v_meta, v_ffn1_wg, v_ffn1_wu, v_ffn1_wd, v_ffn2_wg, v_ffn2_wu, v_ffn2_wd, v_ln_gain, v_ln_bias, v_conv_w_in, v_conv_w, v_conv_w_out, v_kv_w, v_f_bias, v_attn_w_q, v_attn_w_o):
    seq, d = x.shape[1], x.shape[2]
    t = PAD + N_META + seq
    n_heads = d // HEAD_DIM
    dq = d // N_CHIPS
    n_kv = kv_w.shape[1]
    x2 = x[0]
    target = loss_target[0]

    def rows8(a):
        return jnp.pad(a, ((0, 8 - a.shape[0]), (0, 0)))

    def small_pack(mt, g, b, cw, fb):
        fb_row = jnp.pad(fb, (0, mt.shape[1] - n_heads))[None]
        return jnp.concatenate([mt, rows8(g.reshape(6, -1)), rows8(b.reshape(6, -1)),
                                rows8(cw.reshape(3, -1)), rows8(fb_row)], axis=0)

    w_small = small_pack(meta, ln_gain, ln_bias, conv_w, f_bias)

    cx, cy, c = lax.axis_index("x"), lax.axis_index("y"), lax.axis_index("c")
    idx = jnp.stack([2 * cx + cy, 2 * (1 - cx) + cy, 2 * cx + (1 - cy), 2 * (1 - cx) + (1 - cy), c]
                    ).astype(jnp.int32)

    def tr(a):
        return a.transpose(0, 2, 1)

    w3 = {"wg1": tr(ffn1_wg), "wu1": tr(ffn1_wu), "wd1": ffn1_wd, "wg2": tr(ffn2_wg),
          "wu2": tr(ffn2_wu), "wd2": ffn2_wd, "win": conv_w_in, "wout": conv_w_out, "kv": kv_w[None],
          "wq": attn_w_q, "wo": attn_w_o, "small": w_small[None]}
    transposed = ("wg1", "wu1", "wg2", "wu2")
    buf = {n: _place_shard(w, idx, "place_shard") for n, w in w3.items()}

    def split(item):
        name, layer = item.split(".")
        return name, int(layer)

    def gather_stage(planner, items):
        triples = [(n, l, buf[n].shape[2] // 2) for n, l in map(split, items)]
        plan, n_copies = planner(triples)
        return dict(plan=plan, n=n_copies, bufs={n: buf[n] for n, _, _ in triples})

    def ici(items):
        return gather_stage(_plan_gather_ici, items)

    def d2d(items):
        return gather_stage(_plan_gather_d2d, items)

    def pair_exchange(items):
        r2s = [buf["G_" + it].shape[1] // 2 for it in items]
        plan, n_copies = _plan_pair_exchange(items, r2s)
        news = {"PAIR_" + it: jax.ShapeDtypeStruct((N_CHIPS, r2, buf["G_" + it].shape[2]), f32)
                for it, r2 in zip(items, r2s)}
        return dict(plan=plan, n=n_copies, srcs={"G_" + it: buf["G_" + it] for it in items}, news=news)

    def chip_exchange(items):
        plan, n_copies = _plan_chip_exchange(items)
        srcs = {"SEND_" + it: buf["SEND_" + it] for it in items}
        news = {"RECV_" + it: jax.ShapeDtypeStruct(s.shape, s.dtype)
                for it, s in ((it, buf["SEND_" + it]) for it in items)}
        return dict(plan=plan, n=n_copies, srcs=srcs, news=news)

    def pair_share(items):
        triples = [(n, l, buf["RED_" + n].shape[1] // 2) for n, l in map(split, items)]
        plan, n_copies = _plan_pair_share(triples)
        return dict(plan=plan, n=n_copies, bufs={"RED_" + n: buf["RED_" + n] for n, _, _ in triples})

    def run(fn, *args, stages=(), name=None, **kw):
        comm = _Copies()
        for st in (stages() if callable(stages) else stages):
            comm.add(st["plan"], st["n"], srcs=st.get("srcs"), bufs=st.get("bufs"), news=st.get("news"))
        out = _copy_call(comm, name) if fn is None else fn(*args, comm=comm, **kw)
        buf.update(comm.out_bufs)
        buf.update(comm.out_news)
        return out

    first = ["wg1.0", "wu1.0", "small.0"]
    run(None, stages=[ici(first)], name="gather_first_ici")
    run(None, stages=[d2d(first)], name="gather_first_d2d")
    small = buf["small"].reshape(N_CHIPS, SMALL_ROWS, dq).transpose(1, 0, 2).reshape(SMALL_ROWS, d)
    meta_full = small[:N_META]
    gains = small[16:22].reshape(DEPTH, 3, 1, d)
    biases = small[24:30].reshape(DEPTH, 3, 1, d)
    conv_w_full = small[32:35]
    fb_pad = jnp.pad(f_bias, (0, LANES - n_heads))[None]
    down1 = ["wd1.0"]
    ffn2_l0 = ["win.0", "wout.0", "wg2.0", "wu2.0", "wd2.0", "kv.0"]
    attn_ffn2_l1 = ["wq.0", "wo.0", "wg2.1", "wu2.1", "wd2.1"]
    ffn1_l1 = ["wg1.1", "wu1.1", "wd1.1"]

    meta_pad = jnp.concatenate([jnp.zeros((PAD, d), f32), meta_full], axis=0)
    h0, h0b = run(_embed, meta_pad, x2, "embed", stages=[ici(down1)])
    a1, b1, s1 = run(_ffn_up, h0b, buf["wg1"], buf["wu1"], 0, "ffn_up",
                     stages=lambda: [d2d(down1), ici(ffn2_l0)])
    r1, h1, h1b = run(_down_ln, s1, buf["wd1"], 0, h0, gains[0, 0], biases[0, 0], 0.5, "ffn_down_ln",
                      stages=lambda: [d2d(ffn2_l0), ici(attn_ffn2_l1)])
    n_in = conv_w_in.shape[-1]
    w_in = buf["win"].reshape(N_CHIPS, d, n_in)
    w_out = buf["wout"].reshape(1, 1, d, d)
    p = run(_nn_matmul, h1b, w_in, f32, "conv_in", stages=lambda: [d2d(attn_ffn2_l1), ici(ffn1_l1)])
    z = _conv_fwd(p, conv_w_full, "conv_fwd")
    r2, h2, h2b = run(_down_ln, z[None], w_out, 0, h1, gains[0, 1], biases[0, 1], 1.0, "mix_out_ln",
                      stages=lambda: [d2d(ffn1_l1)])
    wg1, wu1, wd1, wg2, wu2, wd2 = (buf[n] for n in ("wg1", "wu1", "wd1", "wg2", "wu2", "wd2"))
    w_q = buf["wq"].reshape(1, d, d)
    w_o = buf["wo"].reshape(1, 1, d, d)
    kv_full = buf["kv"].reshape(N_CHIPS, d, n_kv).transpose(1, 0, 2).reshape(d, N_CHIPS * n_kv)
    w_k = kv_full[:, :d][None]
    w_v = kv_full[:, d:2 * d][None]
    w_f = jnp.pad(kv_full[:, 2 * d:], ((0, 0), (0, LANES - n_heads)))[None]
    a2, b2, s2 = _ffn_up(h2b, wg2, wu2, 0, "ffn_up")
    r3, h3, h3b = _down_ln(s2, wd2, 0, h2, gains[0, 2], biases[0, 2], 0.5, "ffn_down_ln")
    kk = _nn_matmul(h3b, w_k, bf16, "proj_bf16")
    vv = _nn_matmul(h3b, w_v, bf16, "proj_bf16")
    flog = _nn_matmul(h3b, w_f, f32, "proj_gate")
    cum = _row_scan(flog, "gate_cumsum", fb_pad)
    bk = _tile(t, 640, LANES)
    c_ht = cum[:, :n_heads].T
    c_keys = jnp.where(jnp.arange(t)[None, :] < PAD, 1e30, c_ht)
    cq_rep = jnp.broadcast_to(c_ht[:, :, None], (n_heads, t, LANES))
    ck_rep = jnp.broadcast_to(c_keys[:, :, None], (n_heads, t, LANES))
    ck_rows = c_keys.reshape(n_heads, t // bk, 1, bk)
    a3, b3, s3 = _ffn_up(h3b, wg1, wu1, 1, "ffn_up")
    r4, h4, h4b = _down_ln(s3, wd1, 1, h3, gains[1, 0], biases[1, 0], 0.5, "ffn_down_ln")
    q = _nn_matmul(h4b, w_q, bf16, "proj_q", out_scale=1.0 / math.sqrt(HEAD_DIM))
    o, o32, cl = _attn_fwd(q, kk, vv, cq_rep, ck_rows, "attn_fwd")
    r5, h5, h5b = _down_ln(o[None], w_o, 0, h4, gains[1, 1], biases[1, 1], 1.0, "mix_out_ln")
    a4, b4, s4 = _ffn_up(h5b, wg2, wu2, 1, "ffn_up")
    r6, h6, _ = _down_ln(s4, wd2, 1, h5, gains[1, 2], biases[1, 2], 0.5, "ffn_down_ln")
    dr6, dr6b, dg12, db12, sq = _loss_head(h6, target, r6, gains[1, 2], "loss_head")
    loss_part = 0.5 * sq[0, 0] / d

    m_small = small_pack(m_meta, m_ln_gain, m_ln_bias, m_conv_w, m_f_bias)
    v_small = small_pack(v_meta, v_ln_gain, v_ln_bias, v_conv_w, v_f_bias)
    m3 = {"wg1": tr(m_ffn1_wg), "wu1": tr(m_ffn1_wu), "wd1": m_ffn1_wd, "wg2": tr(m_ffn2_wg),
          "wu2": tr(m_ffn2_wu), "wd2": m_ffn2_wd, "win": m_conv_w_in, "wout": m_conv_w_out,
          "kv": m_kv_w[None], "wq": m_attn_w_q, "wo": m_attn_w_o, "small": m_small[None]}
    v3 = {"wg1": tr(v_ffn1_wg), "wu1": tr(v_ffn1_wu), "wd1": v_ffn1_wd, "wg2": tr(v_ffn2_wg),
          "wu2": tr(v_ffn2_wu), "wd2": v_ffn2_wd, "win": v_conv_w_in, "wout": v_conv_w_out,
          "kv": v_kv_w[None], "wq": v_attn_w_q, "wo": v_attn_w_o, "small": v_small[None]}
    stepped = {}

    def alike(items):
        groups = {}
        for it in items:
            n, l = split(it)
            groups.setdefault((w3[n].shape, l, ("RED_" + n) in buf, n in stepped), []).append(it)
        return groups.values()

    def prepare(items):
        for group in alike(items):
            sends = _rs_prepare([buf["G_" + it] for it in group], [buf["PAIR_" + it] for it in group],
                                idx, "grad_prepare")
            buf.update({"SEND_" + it: s for it, s in zip(group, sends)})

    def finish(items):
        for group in alike(items):
            names = [split(it)[0] for it in group]
            layer = split(group[0])[1]
            intos = [buf["RED_" + n] for n in names] if ("RED_" + names[0]) in buf else None
            reds = _rs_finish([buf["G_" + it] for it in group], [buf["PAIR_" + it] for it in group],
                              [buf["RECV_" + it] for it in group], idx, layer, w3[names[0]].shape[0],
                              intos, "grad_finish")
            buf.update({"RED_" + n: r for n, r in zip(names, reds)})

    def adam(items, grads=None):
        for group in alike(items):
            names = [split(it)[0] for it in group]
            layer = split(group[0])[1]
            gs = [buf["RED_" + n] if grads is None else grads[n] for n in names]
            intos = [stepped[n] for n in names] if names[0] in stepped else None
            res = _adamw([w3[n] for n in names], gs, [m3[n] for n in names], [v3[n] for n in names],
                         layer, intos, "adamw")
            stepped.update(dict(zip(names, res)))

    def ffn_bwd(dr, drb, hb_in, a, b, s, f, layer, on_act=(), after_act=None, on_dwd=(),
                after_dwd=None, on_dx=(), ln=None):
        da, db = run(_ffn_bwd_act, drb, buf["wd" + f], layer, a, b, "ffn_bwd_act", stages=on_act)
        if after_act is not None:
            after_act()
        (buf[f"G_wd{f}.{layer}"],) = run(_tn_matmul, [(s, "stack", drb, "shared")], N_CHIPS, 0.5,
                                         "ffn_dwd", stages=on_dwd)
        if after_dwd is not None:
            after_dwd()
        buf[f"G_wg{f}.{layer}"], buf[f"G_wu{f}.{layer}"] = _tn_matmul(
            [(da, "stack", hb_in, "shared"), (db, "stack", hb_in, "shared")], N_CHIPS, 1.0, "ffn_dwgu")
        return run(_nt_sum, [(da, "stack", buf["wg" + f], layer, True),
                             (db, "stack", buf["wu" + f], layer, True)],
                   dr, ALPHA, N_CHIPS, f32, "ffn_dx", stages=on_dx, ln=ln)

    ffn2_1 = ["wg2.1", "wu2.1", "wd2.1"]
    ffn1_1 = ["wg1.1", "wu1.1", "wd1.1"]
    ffn2_0 = ["wg2.0", "wu2.0", "wd2.0"]
    conv_items = ["wout.0", "win.0"]

    dr5, dr5b, dg11, db11 = ffn_bwd(dr6, dr6b, h5b, a4, b4, s4, "2", 1,
                                    on_dx=lambda: [pair_exchange(ffn2_1)], ln=(r5, gains[1, 1]))
    prepare(ffn2_1)
    (dwo,) = _tn_matmul([(o, "shared", dr5b, "shared")], 1, 1.0, "sq_dw")
    buf["G_wo.0"] = dwo.reshape(N_CHIPS, dq, d)
    do, delta = run(_attn_do, dr5b, w_o[0, 0], o32, bk, "attn_do",
                    stages=lambda: [pair_exchange(["wo.0"])])
    prepare(["wo.0"])
    kt = kk.reshape(t // bk, bk, n_heads, HEAD_DIM).transpose(2, 0, 3, 1)
    dq_att, dc_q, dk, dv, dc_k = run(_attn_bwd, q, kk, vv, kt, do, ck_rep, cl, delta, "attn_bwd",
                                     stages=lambda: [chip_exchange(ffn2_1 + ["wo.0"])])
    finish(ffn2_1 + ["wo.0"])
    dc = (dc_q + dc_k).reshape(n_heads, t)
    (dwq,) = run(_tn_matmul, [(h4b, "shared", dq_att, "shared")], 1, 1.0, "sq_dw",
                 stages=lambda: [pair_share(ffn2_1 + ["wo.0"])])
    buf["G_wq.0"] = dwq.reshape(N_CHIPS, dq, d)
    adam(ffn2_1 + ["wo.0"])
    dr4, dr4b, dg10, db10 = run(_nt_sum, [(dq_att, "cols", w_q, None)], dr5, ALPHA, 1, f32,
                                "sq_dx_res", stages=lambda: [pair_exchange(["wq.0"])],
                                ln=(r4, gains[1, 0]))
    prepare(["wq.0"])
    dh3a = ffn_bwd(dr4, dr4b, h3b, a3, b3, s3, "1", 1,
                   on_act=lambda: [chip_exchange(["wq.0"])],
                   on_dx=lambda: [pair_exchange(ffn1_1)])
    prepare(ffn1_1)
    finish(["wq.0"])
    dc_t = jnp.pad(dc.T, ((0, 0), (0, LANES - n_heads)))
    dlogf = _row_scan(dc_t, "rev_cumsum", reverse=True)
    dfl, dfb_cols = _fgate_bwd(dlogf, flog, fb_pad, "gate_bwd")
    dwk, dwv, dwf = run(_tn_matmul, [(h3b, "shared", g, "shared") for g in (dk, dv, dfl)], 1, 1.0,
                        "kv_dw",
                        stages=lambda: [chip_exchange(ffn1_1), pair_share(["wq.0"])])
    adam(["wq.0"])

    def by_chip(full):
        rows = full.shape[0]
        return full.reshape(rows, N_CHIPS, full.shape[1] // N_CHIPS).transpose(1, 0, 2)

    buf["G_kv.0"] = by_chip(jnp.concatenate([dwk[0], dwv[0], dwf[0][:, :n_heads]], axis=1))
    dr3, dr3b, dg02, db02 = run(
        _nt_sum, [(dk, "cols", w_k, None), (dv, "cols", w_v, None), (dfl, "cols", w_f, None)],
        dh3a, 1.0, 1, f32, "kv_dx", stages=lambda: [pair_exchange(["kv.0"])], ln=(r3, gains[0, 2]))
    prepare(["kv.0"])
    finish(ffn1_1)
    dr2, dr2b, dg01, db01 = ffn_bwd(dr3, dr3b, h2b, a2, b2, s2, "2", 0,
                                    on_act=lambda: [chip_exchange(["kv.0"]), pair_share(ffn1_1)],
                                    after_act=lambda: (adam(ffn1_1), finish(["kv.0"])),
                                    on_dwd=lambda: [pair_share(["kv.0"])],
                                    after_dwd=lambda: adam(["kv.0"]),
                                    on_dx=lambda: [pair_exchange(ffn2_0)], ln=(r2, gains[0, 1]))
    prepare(ffn2_0)
    (dwout,) = _tn_matmul([(z, "shared", dr2b, "shared")], 1, 1.0, "sq_dw")
    buf["G_wout.0"] = dwout.reshape(N_CHIPS, dq, d)
    dz = _nt_sum([(dr2b, "cols", w_out[0], None)], None, 1.0, 1, f32, "sq_dx_f32")
    dp, dconv_w = run(_conv_bwd, dz, p, conv_w_full, "conv_bwd",
                      stages=lambda: [chip_exchange(ffn2_0)])
    (buf["G_win.0"],) = _tn_matmul([(h1b, "shared", dp, "cols")], N_CHIPS, 1.0, "conv_dwin")
    finish(ffn2_0)
    dr1, dr1b, dg00, db00 = run(_nt_sum, [(dp, "cols", w_in, None)], dr2, ALPHA, N_CHIPS, f32,
                                "conv_dx",
                                stages=lambda: [pair_exchange(conv_items), pair_share(ffn2_0)],
                                ln=(r1, gains[0, 0]))
    prepare(conv_items)
    adam(ffn2_0)
    gate_up = ["wg1.0", "wu1.0"]
    da, db = run(_ffn_bwd_act, dr1b, buf["wd1"], 0, a1, b1, "ffn_bwd_act",
                 stages=lambda: [chip_exchange(conv_items)])
    finish(conv_items)
    buf["G_wg1.0"], buf["G_wu1.0"] = run(
        _tn_matmul, [(da, "stack", h0b, "shared"), (db, "stack", h0b, "shared")], N_CHIPS, 1.0,
        "ffn_dwgu", stages=lambda: [pair_share(conv_items)])
    adam(conv_items)
    (buf["G_wd1.0"],) = run(_tn_matmul, [(s1, "stack", dr1b, "shared")], N_CHIPS, 0.5, "ffn_dwd",
                            stages=lambda: [pair_exchange(gate_up)])
    prepare(gate_up)
    dh0 = run(_nt_sum, [(da, "stack", buf["wg1"], 0, True), (db, "stack", buf["wu1"], 0, True)],
              dr1, ALPHA, N_CHIPS, f32, "ffn_dx",
              stages=lambda: [chip_exchange(gate_up), pair_exchange(["wd1.0"])])
    prepare(["wd1.0"])
    finish(gate_up)
    grad_x = dh0[PAD + N_META:][None]
    dmeta = dh0[PAD:PAD + N_META]
    buf["G_small.0"] = by_chip(jnp.concatenate(
        [dmeta, rows8(jnp.concatenate([dg00, dg01, dg02, dg10, dg11, dg12], axis=0)),
         rows8(jnp.concatenate([db00, db01, db02, db10, db11, db12], axis=0)),
         rows8(dconv_w), jnp.zeros((8, d), f32)], axis=0))
    run(None, stages=lambda: [chip_exchange(["wd1.0"]), pair_exchange(["small.0"]), pair_share(gate_up)],
        name="grad_tail_1")
    prepare(["small.0"])
    finish(["wd1.0"])
    adam(gate_up)
    run(None, stages=lambda: [chip_exchange(["small.0"]), pair_share(["wd1.0"])], name="grad_tail_2")
    finish(["small.0"])
    adam(["wd1.0"])
    run(None, stages=lambda: [pair_share(["small.0"])], name="grad_tail_3")

    tail = jnp.zeros((LANES,), f32).at[:n_heads].set(dfb_cols[0, :n_heads]).at[n_heads].set(loss_part)
    tail = lax.psum(tail, ("x", "y", "c"))
    loss = tail[n_heads]
    g_fb = tail[:n_heads]
    g_small = jnp.concatenate([buf["RED_small"][0, :40],
                               rows8(jnp.pad(g_fb, (0, dq - n_heads))[None])], axis=0)
    adam(["small.0"], grads={"small": g_small[None]})

    def unpack(pk):
        return (pk[:16], pk[16:22].reshape(DEPTH, 3, dq), pk[24:30].reshape(DEPTH, 3, dq),
                pk[32:35].reshape(1, 3, dq), pk[40, :n_heads])

    def order(pick):
        mt, g, b, cw, fb = unpack(pick("small")[0])
        big = {n: pick(n) for n in w3 if n != "small"}
        big["kv"] = big["kv"][0]
        for n in transposed:
            big[n] = tr(big[n])
        return [mt, big["wg1"], big["wu1"], big["wd1"], big["wg2"], big["wu2"], big["wd2"], g, b,
                big["win"], cw, big["wout"], big["kv"], fb, big["wq"], big["wo"]]

    return (loss, grad_x, *order(lambda n: stepped[n][3]), *order(lambda n: stepped[n][0]),
            *order(lambda n: stepped[n][1]), *order(lambda n: stepped[n][2]))
```

```python
import functools
import math

import jax
import jax.numpy as jnp
from jax import lax
from jax.experimental import pallas as pl
from jax.experimental.pallas import tpu as pltpu

f32 = jnp.float32
bf16 = jnp.bfloat16

N_META = 16
PAD = 112
HEAD_DIM = 128
DEPTH = 2
LN_EPS = 1e-5
ALPHA = (2 * DEPTH) ** 0.25
NEG_INF = -1e30
N_CHIPS = 4
SMALL_ROWS = 48
LANES = 128

ADAM_LR = 0.001
ADAM_B1 = 0.9
ADAM_B2 = 0.999
ADAM_EPS = 1e-08
ADAM_WD = 0.01
ADAM_STEP = 10

VMEM_LIMIT_BYTES = 56 * 1024 * 1024
ROWS_WIDE = 2080
ROWS_ACC = 1040
ROWS_ACC_LN = 832
STRIP = 16
MESH = pl.DeviceIdType.MESH

NT_DIMS = (((1,), (1,)), ((), ()))
TN_DIMS = (((0,), (0,)), ((), ()))


def _tile(n, target, mult):
    best = None
    for d in range(mult, min(n, target) + 1, mult):
        if n % d == 0:
            best = d
    assert best is not None, (n, target, mult)
    return best


def _params(*sem):
    return pltpu.CompilerParams(dimension_semantics=sem, vmem_limit_bytes=VMEM_LIMIT_BYTES)


class _Place:
    def __init__(self):
        self.cx, self.cy, self.c = lax.axis_index("x"), lax.axis_index("y"), lax.axis_index("c")
        self.chips = [(1 - self.cx, self.cy), (self.cx, 1 - self.cy), (1 - self.cx, 1 - self.cy)]
        self.me = 2 * self.cx + self.cy
        self.slots = [self.me] + [2 * px + py for px, py in self.chips]
        self.sib = (self.cx, self.cy, 1 - self.c)


class _Copies:
    def __init__(self):
        self.srcs, self.bufs, self.news = {}, {}, {}
        self.plans = []
        self.out_bufs, self.out_news = {}, {}

    def add(self, plan, n_copies, srcs=None, bufs=None, news=None):
        for have, more in ((self.srcs, srcs), (self.bufs, bufs), (self.news, news)):
            for key, val in (more or {}).items():
                assert key not in have or have[key] is val, key
                have[key] = val
        self.plans.append((plan, n_copies))

    def empty(self):
        return not self.plans

    def count(self):
        return sum(n for _, n in self.plans)

    def copies(self, src_refs, buf_refs, new_refs, send, recv):
        place = _Place()
        srcs = dict(zip(self.srcs, src_refs))
        bufs = dict(zip(self.bufs, buf_refs))
        news = dict(zip(self.news, new_refs))
        out = []
        for plan, n_copies in self.plans:
            triples = plan(srcs, bufs, news, place)
            assert len(triples) == n_copies
            for src, dst, dev in triples:
                n = len(out)
                out.append(pltpu.make_async_remote_copy(
                    src_ref=src, dst_ref=dst, send_sem=send.at[n], recv_sem=recv.at[n],
                    device_id=dev, device_id_type=MESH))
        return out

    def land(self, results):
        n_b = len(self.bufs)
        self.out_bufs = dict(zip(self.bufs, results[:n_b]))
        self.out_news = dict(zip(self.news, results[n_b:]))


def _pallas(comm, body, *, name, grid, in_specs, out_specs, out_shape, compiler_params,
            scratch_shapes=(), input_output_aliases=None):
    aliases = dict(input_output_aliases or {})
    if comm is None or comm.empty():
        return pl.pallas_call(body, name=name, grid=grid, in_specs=in_specs, out_specs=out_specs,
                              out_shape=out_shape, scratch_shapes=list(scratch_shapes),
                              input_output_aliases=aliases, compiler_params=compiler_params)
    single = not isinstance(out_shape, (list, tuple))
    out_shapes = [out_shape] if single else list(out_shape)
    out_specs_l = [out_specs] if single else list(out_specs)
    n_in, n_out, n_scr = len(in_specs), len(out_shapes), len(scratch_shapes)
    n_s, n_b, n_n = len(comm.srcs), len(comm.bufs), len(comm.news)
    n_copies = comm.count()

    def wrapped(*refs):
        ins = refs[:n_in]
        src_refs = refs[n_in:n_in + n_s]
        o0 = n_in + n_s + n_b
        outs = refs[o0:o0 + n_out]
        buf_refs = refs[o0 + n_out:o0 + n_out + n_b]
        new_refs = refs[o0 + n_out + n_b:o0 + n_out + n_b + n_n]
        rest = refs[o0 + n_out + n_b + n_n:]
        scratch, (send, recv) = rest[:n_scr], rest[n_scr:]
        ids = [pl.program_id(a) for a in range(len(grid))]
        first = functools.reduce(jnp.logical_and, [i == 0 for i in ids])
        last = functools.reduce(jnp.logical_and, [i == g - 1 for i, g in zip(ids, grid)])

        @pl.when(first)
        def _():
            for cp in comm.copies(src_refs, buf_refs, new_refs, send, recv):
                cp.start()

        body(*ins, *outs, *scratch)

        @pl.when(last)
        def _():
            for cp in comm.copies(src_refs, buf_refs, new_refs, send, recv):
                cp.wait()

    hbm = pl.BlockSpec(memory_space=pl.ANY)
    for j in range(n_b):
        aliases[n_in + n_s + j] = n_out + j
    call = pl.pallas_call(
        wrapped, name=name, grid=grid,
        in_specs=[*in_specs, *([hbm] * (n_s + n_b))],
        out_specs=[*out_specs_l, *([hbm] * (n_b + n_n))],
        out_shape=[*out_shapes,
                   *[jax.ShapeDtypeStruct(a.shape, a.dtype) for a in comm.bufs.values()],
                   *comm.news.values()],
        scratch_shapes=[*scratch_shapes, pltpu.SemaphoreType.DMA((n_copies,)),
                        pltpu.SemaphoreType.DMA((n_copies,))],
        input_output_aliases=aliases, compiler_params=compiler_params)

    def run(*args):
        res = call(*args, *comm.srcs.values(), *comm.bufs.values())
        comm.land(res[n_out:])
        return res[0] if single else res[:n_out]

    return run


def _copy_call(comm, name):
    n_s, n_b, n_n = len(comm.srcs), len(comm.bufs), len(comm.news)
    n_copies = comm.count()

    def body(*refs):
        src_refs = refs[:n_s]
        buf_refs = refs[n_s + n_b:n_s + 2 * n_b]
        new_refs = refs[n_s + 2 * n_b:n_s + 2 * n_b + n_n]
        send, recv = refs[n_s + 2 * n_b + n_n:]
        copies = comm.copies(src_refs, buf_refs, new_refs, send, recv)
        for cp in copies:
            cp.start()
        for cp in copies:
            cp.wait()

    hbm = pl.BlockSpec(memory_space=pl.ANY)
    res = pl.pallas_call(
        body, name=name,
        in_specs=[hbm] * (n_s + n_b), out_specs=[hbm] * (n_b + n_n),
        out_shape=[*[jax.ShapeDtypeStruct(a.shape, a.dtype) for a in comm.bufs.values()],
                   *comm.news.values()],
        input_output_aliases={n_s + j: j for j in range(n_b)},
        scratch_shapes=[pltpu.SemaphoreType.DMA((n_copies,)), pltpu.SemaphoreType.DMA((n_copies,))],
    )(*comm.srcs.values(), *comm.bufs.values())
    comm.land(res)


def _embed(meta_pad, x, name, comm=None):
    seq, d = x.shape
    t = seq + LANES

    def body(m_ref, x_ref, h_ref, hb_ref):
        first = pl.program_id(0) == 0
        v = jnp.where(first, m_ref[...], x_ref[...])
        h_ref[...] = v
        hb_ref[...] = v.astype(bf16)

    return _pallas(
        comm, body, name=name, grid=(t // LANES,),
        in_specs=[pl.BlockSpec((LANES, d), lambda i: (0, 0)),
                  pl.BlockSpec((LANES, d), lambda i: (jnp.maximum(i - 1, 0), 0))],
        out_specs=[pl.BlockSpec((LANES, d), lambda i: (i, 0)),
                   pl.BlockSpec((LANES, d), lambda i: (i, 0))],
        out_shape=[jax.ShapeDtypeStruct((t, d), f32), jax.ShapeDtypeStruct((t, d), bf16)],
        compiler_params=_params("parallel"),
    )(meta_pad, x)


def _nn_matmul(x, w, out_dtype, name, comm=None, out_scale=None):
    t, k = x.shape
    s_n, _, n = w.shape
    assert s_n == 1 or n % LANES == 0
    tm = _tile(t, ROWS_WIDE, 16)

    def body(x_ref, w_ref, o_ref):
        res = jnp.dot(x_ref[...].astype(bf16), w_ref[...], preferred_element_type=f32)
        if out_scale is not None:
            res = res * out_scale
        o_ref[...] = res.astype(o_ref.dtype)

    return _pallas(
        comm, body, name=name, grid=(s_n, t // tm),
        in_specs=[pl.BlockSpec((tm, k), lambda s, i: (i, 0)),
                  pl.BlockSpec((None, k, n), lambda s, i: (s, 0, 0))],
        out_specs=pl.BlockSpec((tm, n), lambda s, i: (i, s)),
        out_shape=jax.ShapeDtypeStruct((t, s_n * n), out_dtype),
        compiler_params=_params("parallel", "parallel"),
    )(x, w)


def _ffn_up(hb, wg, wu, layer, name, comm=None):
    t, d = hb.shape
    s_n, _, n, _ = wg.shape
    tm = _tile(t, ROWS_WIDE, 16)

    def body(x_ref, wg_ref, wu_ref, a_ref, b_ref, s_ref):
        x = x_ref[...]
        a = lax.dot_general(x, wg_ref[...], NT_DIMS, preferred_element_type=f32)
        b = lax.dot_general(x, wu_ref[...], NT_DIMS, preferred_element_type=f32)
        a_ref[...] = a.astype(bf16)
        b_ref[...] = b.astype(bf16)
        s_ref[...] = (a * jax.nn.sigmoid(a) * b).astype(bf16)

    wspec = pl.BlockSpec((None, None, n, d), lambda s, i: (s, layer, 0, 0))
    ospec = pl.BlockSpec((None, tm, n), lambda s, i: (s, i, 0))
    return _pallas(
        comm, body, name=name, grid=(s_n, t // tm),
        in_specs=[pl.BlockSpec((tm, d), lambda s, i: (i, 0)), wspec, wspec],
        out_specs=[ospec, ospec, ospec],
        out_shape=[jax.ShapeDtypeStruct((s_n, t, n), bf16)] * 3,
        compiler_params=_params("parallel", "parallel"),
    )(hb, wg, wu)


def _down_ln(x, w, layer, hprev, gain, bias, beta, name, comm=None):
    s_n, t, k = x.shape
    d = w.shape[-1]
    tm = _tile(t, ROWS_ACC, 16)

    def body(x_ref, w_ref, h_ref, g_ref, b_ref, r_out, h_out, hb_out, acc):
        s = pl.program_id(1)

        @pl.when(s == 0)
        def _():
            acc[...] = jnp.zeros_like(acc)

        acc[...] += jnp.dot(x_ref[...], w_ref[...], preferred_element_type=f32)

        @pl.when(s == s_n - 1)
        def _():
            r = ALPHA * h_ref[...] + beta * acc[...]
            mu = jnp.mean(r, axis=-1, keepdims=True)
            xc = r - mu
            var = jnp.mean(xc * xc, axis=-1, keepdims=True)
            y = xc * lax.rsqrt(var + LN_EPS) * g_ref[...] + b_ref[...]
            r_out[...] = r
            h_out[...] = y
            hb_out[...] = y.astype(bf16)

    row = pl.BlockSpec((tm, d), lambda i, s: (i, 0))
    vec = pl.BlockSpec((1, d), lambda i, s: (0, 0))
    return _pallas(
        comm, body, name=name, grid=(t // tm, s_n),
        in_specs=[pl.BlockSpec((None, tm, k), lambda i, s: (s, i, 0)),
                  pl.BlockSpec((None, None, k, d), lambda i, s: (s, layer, 0, 0)),
                  row, vec, vec],
        out_specs=[row, row, row],
        out_shape=[jax.ShapeDtypeStruct((t, d), f32), jax.ShapeDtypeStruct((t, d), f32),
                   jax.ShapeDtypeStruct((t, d), bf16)],
        scratch_shapes=[pltpu.VMEM((tm, d), f32)],
        compiler_params=_params("parallel", "arbitrary"),
    )(x, w, hprev, gain, bias)


def _conv_fwd(p, conv_w, name):
    t, d3 = p.shape
    d = d3 // 3
    tm = _tile(t, 320, 8)
    hb = tm // 8

    def body(p_ref, prev_ref, w_ref, z_ref):
        i = pl.program_id(0)
        rows = i * tm - 8 + lax.broadcasted_iota(jnp.int32, (tm + 8, 1), 0)
        cg = jnp.concatenate([prev_ref[:, d:2 * d], p_ref[:, d:2 * d]], axis=0)
        val = jnp.concatenate([prev_ref[:, 2 * d:], p_ref[:, 2 * d:]], axis=0)
        u = jnp.where(rows >= PAD, cg * val, 0.0)
        y = (w_ref[2:3, :] * u + w_ref[1:2, :] * pltpu.roll(u, 1, 0)
             + w_ref[0:1, :] * pltpu.roll(u, 2, 0))
        z_ref[...] = (p_ref[:, :d] * y[8:]).astype(bf16)

    return pl.pallas_call(
        body, name=name, grid=(t // tm,),
        in_specs=[pl.BlockSpec((tm, d3), lambda i: (i, 0)),
                  pl.BlockSpec((8, d3), lambda i: (jnp.maximum(i * hb - 1, 0), 0)),
                  pl.BlockSpec((3, d), lambda i: (0, 0))],
        out_specs=pl.BlockSpec((tm, d), lambda i: (i, 0)),
        out_shape=jax.ShapeDtypeStruct((t, d), bf16),
        compiler_params=_params("parallel"),
    )(p, p, conv_w)


def _row_scan(x, name, fbias=None, reverse=False):
    t, n = x.shape
    blk = _tile(t, 640, LANES)
    n_blk = t // blk
    gate = fbias is not None

    def body(*refs):
        if gate:
            x_ref, fb_ref, o_ref, carry = refs
        else:
            x_ref, o_ref, carry = refs
        i = pl.program_id(0)

        @pl.when(i == 0)
        def _():
            carry[...] = jnp.zeros_like(carry)

        v = x_ref[...]
        r = lax.broadcasted_iota(jnp.int32, (blk, n), 0)
        if gate:
            v = v + fb_ref[...]
            v = jnp.minimum(v, 0.0) - jnp.log1p(jnp.exp(-jnp.abs(v)))
            v = jnp.where(i * blk + r >= PAD, v, 0.0)
        sh = 1
        while sh < blk:
            if reverse:
                v = v + jnp.where(r < blk - sh, pltpu.roll(v, blk - sh, 0), 0.0)
            else:
                v = v + jnp.where(r >= sh, pltpu.roll(v, sh, 0), 0.0)
            sh *= 2
        v = v + carry[...]
        o_ref[...] = v
        carry[...] = o_ref[0:1, :] if reverse else o_ref[blk - 1:blk, :]

    order = (lambda i: (n_blk - 1 - i, 0)) if reverse else (lambda i: (i, 0))
    in_specs = [pl.BlockSpec((blk, n), order)]
    args = [x]
    if gate:
        in_specs.append(pl.BlockSpec((1, n), lambda i: (0, 0)))
        args.append(fbias)
    return pl.pallas_call(
        body, name=name, grid=(n_blk,),
        in_specs=in_specs,
        out_specs=pl.BlockSpec((blk, n), order),
        out_shape=jax.ShapeDtypeStruct((t, n), f32),
        scratch_shapes=[pltpu.VMEM((1, n), f32)],
        compiler_params=_params("arbitrary"),
    )(*args)


def _lanes(x, n):
    return jnp.concatenate([x] * (n // LANES), axis=1)


def _attn_fwd(q, k, v, cq_rep, ck_rows, name):
    t, d = q.shape
    n_heads = d // HEAD_DIM
    bk = ck_rows.shape[-1]
    bq = bk

    def lane_fold(x, op):
        out = x[:, :LANES]
        for c0 in range(LANES, bk, LANES):
            out = op(out, x[:, c0:c0 + LANES])
        return out

    def body(q_ref, k_ref, v_ref, cq_ref, ck_ref, o_ref, o32_ref, cl_ref,
             s_scr, p_scr, m_scr, l_scr, red_scr, acc_scr):
        i = pl.program_id(1)
        m_scr[...] = jnp.full_like(m_scr, NEG_INF)
        l_scr[...] = jnp.zeros_like(l_scr)
        acc_scr[...] = jnp.zeros_like(acc_scr)
        qb = q_ref[...]
        ahead = (lax.broadcasted_iota(jnp.int32, (STRIP, bk), 1)
                 - lax.broadcasted_iota(jnp.int32, (STRIP, bk), 0))

        def tile(j, diagonal):
            k0 = pl.multiple_of(j * bk, bk)
            s_scr[...] = lax.dot_general(qb, k_ref[pl.ds(k0, bk), :], NT_DIMS,
                                         preferred_element_type=f32)
            ck = ck_ref[j]
            for r in range(0, bq, STRIP):
                rows = slice(r, r + STRIP)
                s = s_scr[rows, :] + _lanes(cq_ref[rows, :], bk) - ck
                if diagonal:
                    s = jnp.where(ahead <= r, s, NEG_INF)
                s_scr[rows, :] = s
                red_scr[rows, :] = lane_fold(s, jnp.maximum)
            m_old = m_scr[...]
            m_new = jnp.maximum(m_old, jnp.broadcast_to(
                jnp.max(red_scr[...], axis=1, keepdims=True), (bq, LANES)))
            a = jnp.exp(m_old - m_new)
            m_scr[...] = m_new
            for r in range(0, bq, STRIP):
                rows = slice(r, r + STRIP)
                pr = jnp.exp(s_scr[rows, :] - _lanes(m_scr[rows, :], bk))
                red_scr[rows, :] = lane_fold(pr, jnp.add)
                p_scr[rows, :] = pr.astype(bf16)
            l_scr[...] = a * l_scr[...] + jnp.broadcast_to(
                jnp.sum(red_scr[...], axis=1, keepdims=True), (bq, LANES))
            acc_scr[...] = a * acc_scr[...] + jnp.dot(
                p_scr[...], v_ref[pl.ds(k0, bk), :], preferred_element_type=f32)

        def full_tile(j, carry):
            tile(j, False)
            return carry

        lax.fori_loop(0, i, full_tile, 0)
        tile(i, True)
        out = acc_scr[...] / l_scr[...]
        o_ref[...] = out.astype(bf16)
        o32_ref[...] = out
        red_scr[...] = cq_ref[...] - (m_scr[...] + jnp.log(l_scr[...]))
        for c0 in range(0, bq, LANES):
            cl_ref[:, c0:c0 + LANES] = red_scr[c0:c0 + LANES, :].T[0:1, :]

    qblk = pl.BlockSpec((bq, HEAD_DIM), lambda h, i: (i, h))
    head_rows = pl.BlockSpec((t, HEAD_DIM), lambda h, i: (0, h))
    rep = pl.BlockSpec((None, bq, LANES), lambda h, i: (h, i, 0))
    col = pltpu.VMEM((bq, LANES), f32)
    return pl.pallas_call(
        body, name=name, grid=(n_heads, t // bq),
        in_specs=[qblk, head_rows, head_rows, rep,
                  pl.BlockSpec((None, t // bk, 1, bk), lambda h, i: (h, 0, 0, 0))],
        out_specs=[qblk, qblk, pl.BlockSpec((None, None, 1, bq), lambda h, i: (h, i, 0, 0))],
        out_shape=[jax.ShapeDtypeStruct((t, d), bf16), jax.ShapeDtypeStruct((t, d), f32),
                   jax.ShapeDtypeStruct((n_heads, t // bq, 1, bq), f32)],
        scratch_shapes=[pltpu.VMEM((bq, bk), f32), pltpu.VMEM((bq, bk), bf16), col, col, col,
                        pltpu.VMEM((bq, HEAD_DIM), f32)],
        compiler_params=_params("parallel", "parallel"),
    )(q, k, v, cq_rep, ck_rows)


def _loss_head(h, target, r, gain, name):
    t, d = h.shape

    def body(h_ref, t_ref, r_ref, g_ref, dr_ref, drb_ref, dg_ref, db_ref, loss_ref):
        i = pl.program_id(0)

        @pl.when(i == 0)
        def _():
            loss_ref[...] = jnp.zeros_like(loss_ref)
            dg_ref[...] = jnp.zeros_like(dg_ref)
            db_ref[...] = jnp.zeros_like(db_ref)

        diff = jnp.where(i >= 1, h_ref[...] - t_ref[...], 0.0)
        loss_ref[...] += jnp.sum(diff * diff)
        dr, dg, db = _ln_bwd_rows(diff * (1.0 / d), r_ref[...], g_ref[...])
        dr_ref[...] = dr
        drb_ref[...] = dr.astype(bf16)
        dg_ref[...] += dg
        db_ref[...] += db

    row = pl.BlockSpec((LANES, d), lambda i: (i, 0))
    vec = pl.BlockSpec((1, d), lambda i: (0, 0))
    return pl.pallas_call(
        body, name=name, grid=(t // LANES,),
        in_specs=[row, pl.BlockSpec((LANES, d), lambda i: (jnp.maximum(i - 1, 0), 0)), row, vec],
        out_specs=[row, row, vec, vec, pl.BlockSpec((1, LANES), lambda i: (0, 0))],
        out_shape=[jax.ShapeDtypeStruct((t, d), f32), jax.ShapeDtypeStruct((t, d), bf16),
                   jax.ShapeDtypeStruct((1, d), f32), jax.ShapeDtypeStruct((1, d), f32),
                   jax.ShapeDtypeStruct((1, LANES), f32)],
        compiler_params=_params("arbitrary"),
    )(h, target, r, gain)


def _ln_bwd_rows(dy, rr, gain):
    mu = jnp.mean(rr, axis=-1, keepdims=True)
    xc = rr - mu
    var = jnp.mean(xc * xc, axis=-1, keepdims=True)
    rstd = lax.rsqrt(var + LN_EPS)
    xhat = xc * rstd
    dxh = dy * gain
    m1 = jnp.mean(dxh, axis=-1, keepdims=True)
    m2 = jnp.mean(dxh * xhat, axis=-1, keepdims=True)
    dr = rstd * (dxh - m1 - xhat * m2)
    return dr, jnp.sum(dy * xhat, axis=0, keepdims=True), jnp.sum(dy, axis=0, keepdims=True)


def _ffn_bwd_act(drb, wd, layer, a, b, name, comm=None):
    t, d = drb.shape
    s_n, _, n = a.shape
    tm = _tile(t, ROWS_WIDE, 16)

    def body(dr_ref, w_ref, a_ref, b_ref, da_ref, db_ref):
        ds = 0.5 * lax.dot_general(dr_ref[...], w_ref[...], NT_DIMS, preferred_element_type=f32)
        da_ref[...], db_ref[...] = _swiglu_bwd(ds, a_ref, b_ref)

    act = pl.BlockSpec((None, tm, n), lambda s, i: (s, i, 0))
    return _pallas(
        comm, body, name=name, grid=(s_n, t // tm),
        in_specs=[pl.BlockSpec((tm, d), lambda s, i: (i, 0)),
                  pl.BlockSpec((None, None, n, d), lambda s, i: (s, layer, 0, 0)), act, act],
        out_specs=[act, act],
        out_shape=[jax.ShapeDtypeStruct((s_n, t, n), bf16), jax.ShapeDtypeStruct((s_n, t, n), bf16)],
        compiler_params=_params("parallel", "parallel"),
    )(drb, wd, a, b)


def _swiglu_bwd(ds, a_ref, b_ref):
    av = a_ref[...].astype(f32)
    sig = jax.nn.sigmoid(av)
    da = ds * b_ref[...].astype(f32) * (sig * (1.0 + av * (1.0 - sig)))
    return da.astype(bf16), (ds * (av * sig)).astype(bf16)


def _act_spec(mode, tt, k, t_first):
    def fix(fn):
        return (lambda i, s: fn(s, i)) if t_first else fn
    if mode == "shared":
        return pl.BlockSpec((tt, k), fix(lambda s, i: (i, 0)))
    if mode == "cols":
        return pl.BlockSpec((tt, k), fix(lambda s, i: (i, s)))
    assert mode == "stack"
    return pl.BlockSpec((None, tt, k), fix(lambda s, i: (s, i, 0)))


def _act_width(arr, mode, s_n):
    return arr.shape[-1] // s_n if mode == "cols" else arr.shape[-1]


def _tn_matmul(pairs, s_n, scale, name, comm=None):
    t = pairs[0][0].shape[-2]
    tt = _tile(t, 2080, 16)
    n_t = t // tt
    arrays, specs, where = [], [], []
    for x, xmode, y, ymode in pairs:
        pos = []
        for arr, mode in ((x, xmode), (y, ymode)):
            hit = [j for j, a in enumerate(arrays) if a is arr]
            if not hit:
                arrays.append(arr)
                specs.append(_act_spec(mode, tt, _act_width(arr, mode, s_n), False))
                hit = [len(arrays) - 1]
            pos.append(hit[0])
        where.append(pos)
    widths = [(_act_width(x, xm, s_n), _act_width(y, ym, s_n)) for x, xm, y, ym in pairs]
    n_a = len(arrays)

    def body(*refs):
        i = pl.program_id(1)
        for (px, py), o_ref in zip(where, refs[n_a:]):
            part = lax.dot_general(refs[px][...].astype(bf16), refs[py][...].astype(bf16), TN_DIMS,
                                   preferred_element_type=f32)

            @pl.when(i == 0)
            def _():
                o_ref[...] = part

            @pl.when(i > 0)
            def _():
                o_ref[...] += part

            if scale != 1.0:
                @pl.when(i == n_t - 1)
                def _():
                    o_ref[...] = o_ref[...] * scale

    return _pallas(
        comm, body, name=name, grid=(s_n, n_t),
        in_specs=specs,
        out_specs=[pl.BlockSpec((None, kx, ky), lambda s, i: (s, 0, 0)) for kx, ky in widths],
        out_shape=[jax.ShapeDtypeStruct((s_n, kx, ky), f32) for kx, ky in widths],
        compiler_params=_params("parallel", "arbitrary"),
    )(*arrays)


def _nt_sum(pairs, base, base_scale, s_n, out_dtype, name, comm=None, ln=None):
    t = pairs[0][0].shape[-2]
    pairs = [(*pr, False)[:5] for pr in pairs]
    d = pairs[0][2].shape[-1] if pairs[0][4] else pairs[0][2].shape[-2]
    tm = _tile(t, ROWS_ACC if ln is None else ROWS_ACC_LN, 16)
    n_p = len(pairs)
    has_base = base is not None
    flipped = [pr[4] for pr in pairs]
    n_out = 1 if ln is None else 4

    def body(*refs):
        dy_refs = refs[0:2 * n_p:2]
        w_refs = refs[1:2 * n_p:2]
        rest = refs[2 * n_p:]
        base_ref = rest[0] if has_base else None
        o_refs, acc = rest[-1 - n_out:-1], rest[-1]
        s = pl.program_id(1)

        if ln is not None:
            @pl.when(jnp.logical_and(pl.program_id(0) == 0, s == 0))
            def _():
                o_refs[2][...] = jnp.zeros_like(o_refs[2])
                o_refs[3][...] = jnp.zeros_like(o_refs[3])

        @pl.when(s == 0)
        def _():
            acc[...] = jnp.zeros_like(acc)

        tot = None
        for dy_ref, w_ref, flip in zip(dy_refs, w_refs, flipped):
            dyv = dy_ref[...].astype(bf16)
            if flip:
                part = jnp.dot(dyv, w_ref[...], preferred_element_type=f32)
            else:
                part = lax.dot_general(dyv, w_ref[...], NT_DIMS, preferred_element_type=f32)
            tot = part if tot is None else tot + part
        acc[...] += tot

        @pl.when(s == s_n - 1)
        def _():
            res = acc[...]
            if has_base:
                res = base_scale * base_ref[...] + res
            if ln is None:
                o_refs[0][...] = res.astype(o_refs[0].dtype)
            else:
                r_ref, g_ref = rest[-7], rest[-6]
                dr, dg, db = _ln_bwd_rows(res, r_ref[...], g_ref[...])
                o_refs[0][...] = dr
                o_refs[1][...] = dr.astype(bf16)
                o_refs[2][...] += dg
                o_refs[3][...] += db

    in_specs, args = [], []
    for dy, mode, w, layer, flip in pairs:
        k = _act_width(dy, mode, s_n)
        in_specs.append(_act_spec(mode, tm, k, True))
        wshape = (k, d) if flip else (d, k)
        if layer is None:
            in_specs.append(pl.BlockSpec((None, *wshape), lambda i, s: (s, 0, 0)))
        else:
            in_specs.append(pl.BlockSpec((None, None, *wshape),
                                         functools.partial(lambda i, s, l: (s, l, 0, 0), l=layer)))
        args += [dy, w]
    row = pl.BlockSpec((tm, d), lambda i, s: (i, 0))
    vec = pl.BlockSpec((1, d), lambda i, s: (0, 0))
    if has_base:
        in_specs.append(row)
        args.append(base)
    if ln is None:
        out_specs = row
        out_shape = jax.ShapeDtypeStruct((t, d), out_dtype)
    else:
        in_specs += [row, vec]
        args += list(ln)
        out_specs = [row, row, vec, vec]
        out_shape = [jax.ShapeDtypeStruct((t, d), f32), jax.ShapeDtypeStruct((t, d), bf16),
                     jax.ShapeDtypeStruct((1, d), f32), jax.ShapeDtypeStruct((1, d), f32)]
    return _pallas(
        comm, body, name=name, grid=(t // tm, s_n),
        in_specs=in_specs, out_specs=out_specs, out_shape=out_shape,
        scratch_shapes=[pltpu.VMEM((tm, d), f32)],
        compiler_params=_params("parallel" if ln is None else "arbitrary", "arbitrary"),
    )(*args)


def _conv_bwd(dz, p, conv_w, name, comm=None):
    t, d3 = p.shape
    d = d3 // 3
    tm = _tile(t, 320, 8)
    hb = tm // 8
    last8 = t // 8 - 1
    n_ext = tm + 8

    def body(dz_ref, dzn_ref, p_ref, pp_ref, pn_ref, w_ref, dp_ref, dw_ref):
        i = pl.program_id(0)

        @pl.when(i == 0)
        def _():
            dw_ref[...] = jnp.zeros_like(dw_ref)

        w0, w1, w2 = w_ref[0:1, :], w_ref[1:2, :], w_ref[2:3, :]
        rows_u = i * tm - 8 + lax.broadcasted_iota(jnp.int32, (n_ext, 1), 0)
        cg = jnp.concatenate([pp_ref[:, d:2 * d], p_ref[:, d:2 * d]], axis=0)
        val = jnp.concatenate([pp_ref[:, 2 * d:], p_ref[:, 2 * d:]], axis=0)
        u = jnp.where(rows_u >= PAD, cg * val, 0.0)
        u1 = pltpu.roll(u, 1, 0)
        u2 = pltpu.roll(u, 2, 0)
        y = (w2 * u + w1 * u1 + w0 * u2)[8:]
        dzv = dz_ref[...]
        bg = p_ref[:, :d]
        rows_n = (i + 1) * tm + lax.broadcasted_iota(jnp.int32, (8, 1), 0)
        dy_main = dzv * bg
        dy_next = jnp.where(rows_n < t, dzn_ref[...] * pn_ref[:, :d], 0.0)
        dye = jnp.concatenate([dy_main, dy_next], axis=0)
        du = (w2 * dye + w1 * pltpu.roll(dye, n_ext - 1, 0)
              + w0 * pltpu.roll(dye, n_ext - 2, 0))[:tm]
        du = jnp.where(rows_u[8:] >= PAD, du, 0.0)
        dp_ref[:, :d] = (dzv * y).astype(bf16)
        dp_ref[:, d:2 * d] = (du * val[8:]).astype(bf16)
        dp_ref[:, 2 * d:] = (du * cg[8:]).astype(bf16)
        dw_ref[0:1, :] += jnp.sum(dy_main * u2[8:], axis=0, keepdims=True)
        dw_ref[1:2, :] += jnp.sum(dy_main * u1[8:], axis=0, keepdims=True)
        dw_ref[2:3, :] += jnp.sum(dy_main * u[8:], axis=0, keepdims=True)

    nxt = lambda i: (jnp.minimum((i + 1) * hb, last8), 0)
    return _pallas(
        comm, body, name=name, grid=(t // tm,),
        in_specs=[pl.BlockSpec((tm, d), lambda i: (i, 0)),
                  pl.BlockSpec((8, d), nxt),
                  pl.BlockSpec((tm, d3), lambda i: (i, 0)),
                  pl.BlockSpec((8, d3), lambda i: (jnp.maximum(i * hb - 1, 0), 0)),
                  pl.BlockSpec((8, d3), nxt),
                  pl.BlockSpec((3, d), lambda i: (0, 0))],
        out_specs=[pl.BlockSpec((tm, d3), lambda i: (i, 0)),
                   pl.BlockSpec((3, d), lambda i: (0, 0))],
        out_shape=[jax.ShapeDtypeStruct((t, d3), bf16), jax.ShapeDtypeStruct((3, d), f32)],
        compiler_params=_params("arbitrary"),
    )(dz, dz, p, p, p, conv_w)


def _attn_do(drb, w_o, o, bq, name, comm=None):
    t, d = drb.shape
    n_heads = d // HEAD_DIM

    def body(dr_ref, w_ref, o_ref, do_ref, delta_ref):
        do = lax.dot_general(dr_ref[...], w_ref[...], NT_DIMS, preferred_element_type=f32).astype(bf16)
        do_ref[...] = do
        prod = o_ref[...] * do.astype(f32)
        for h in range(n_heads):
            for c0 in range(0, bq, LANES):
                blk = prod[c0:c0 + LANES, h * HEAD_DIM:(h + 1) * HEAD_DIM]
                delta_ref[h, :, c0:c0 + LANES] = jnp.sum(blk.T, axis=0, keepdims=True)

    row = pl.BlockSpec((bq, d), lambda i: (i, 0))
    return _pallas(
        comm, body, name=name, grid=(t // bq,),
        in_specs=[row, pl.BlockSpec((d, d), lambda i: (0, 0)), row],
        out_specs=[row, pl.BlockSpec((n_heads, None, 1, bq), lambda i: (0, i, 0, 0))],
        out_shape=[jax.ShapeDtypeStruct((t, d), bf16),
                   jax.ShapeDtypeStruct((n_heads, t // bq, 1, bq), f32)],
        compiler_params=_params("parallel"),
    )(drb, w_o, o)


def _attn_bwd(q, k, v, kt, do, ckey, cl_rows, delta_rows, name, comm=None):
    t, d = q.shape
    n_heads = d // HEAD_DIM
    bk = kt.shape[-1]
    bq = bk
    n_kv = t // bk
    n_q = t // bq
    scale = 1.0 / math.sqrt(HEAD_DIM)

    def body(q_ref, do_ref, cl_ref, dl_ref, k_ref, v_ref, kt_ref, ck_ref,
             dq_ref, dcq_ref, dk_ref, dv_ref, dck_ref,
             st_scr, dp_scr, p_scr, ds_scr, dqt, dk_acc, dv_acc, dck_acc):
        j = pl.program_id(1)

        @pl.when(j == 0)
        def _():
            dqt[...] = jnp.zeros_like(dqt)
            dcq_ref[...] = jnp.zeros_like(dcq_ref)

        dk_acc[...] = jnp.zeros_like(dk_acc)
        dv_acc[...] = jnp.zeros_like(dv_acc)
        dck_acc[...] = jnp.zeros_like(dck_acc)
        kb = k_ref[...]
        vb = v_ref[...]
        behind = (lax.broadcasted_iota(jnp.int32, (STRIP, bq), 1)
                  - lax.broadcasted_iota(jnp.int32, (STRIP, bq), 0))

        def tile(i, diagonal):
            r0 = pl.multiple_of(i * bq, bq)
            qi = q_ref[pl.ds(r0, bq), :]
            doi = do_ref[pl.ds(r0, bq), :]
            st_scr[...] = lax.dot_general(kb, qi, NT_DIMS, preferred_element_type=f32)
            dp_scr[...] = lax.dot_general(vb, doi, NT_DIMS, preferred_element_type=f32)
            cl = cl_ref[i]
            dl = dl_ref[i]
            over_keys = jnp.zeros((STRIP, bq), f32)
            for r in range(0, bk, STRIP):
                keys = slice(r, r + STRIP)
                st = st_scr[keys, :] + cl - _lanes(ck_ref[keys, :], bq)
                if diagonal:
                    st = jnp.where(behind >= r, st, NEG_INF)
                pr = jnp.exp(st)
                ds = pr * (dp_scr[keys, :] - dl)
                over_keys = over_keys + ds
                dck_acc[keys, :] -= jnp.sum(ds, axis=1, keepdims=True)
                p_scr[keys, :] = pr.astype(bf16)
                ds_scr[keys, :] = ds.astype(bf16)
            dcq_ref[i] += jnp.sum(over_keys, axis=0, keepdims=True)
            dv_acc[...] += jnp.dot(p_scr[...], doi, preferred_element_type=f32)
            dk_acc[...] += jnp.dot(ds_scr[...], qi, preferred_element_type=f32)
            dqt[i] += jnp.dot(kt_ref[...], ds_scr[...], preferred_element_type=f32)

        def full_tile(i, carry):
            tile(i, False)
            return carry

        tile(j, True)
        lax.fori_loop(j + 1, n_q, full_tile, 0)
        dk_ref[...] = dk_acc[...].astype(bf16)
        dv_ref[...] = dv_acc[...].astype(bf16)
        for c0 in range(0, bk, LANES):
            keys = slice(c0, c0 + LANES)
            dck_ref[:, keys] = jnp.broadcast_to(dck_acc[keys, :], (LANES, LANES)).T[0:1, :]

        @pl.when(j == n_kv - 1)
        def _():
            def emit(i, carry):
                r0 = pl.multiple_of(i * bq, bq)
                dq_ref[pl.ds(r0, bq), :] = dqt[i].T * scale
                return carry
            lax.fori_loop(0, n_q, emit, 0)

    head_rows = pl.BlockSpec((t, HEAD_DIM), lambda h, j: (0, h))
    head_stat = pl.BlockSpec((None, n_q, 1, bq), lambda h, j: (h, 0, 0, 0))
    kblk = pl.BlockSpec((bk, HEAD_DIM), lambda h, j: (j, h))
    return _pallas(
        comm, body, name=name, grid=(n_heads, n_kv),
        in_specs=[head_rows, head_rows, head_stat, head_stat, kblk, kblk,
                  pl.BlockSpec((None, None, HEAD_DIM, bk), lambda h, j: (h, j, 0, 0)),
                  pl.BlockSpec((None, bk, LANES), lambda h, j: (h, j, 0))],
        out_specs=[head_rows, head_stat, kblk, kblk,
                   pl.BlockSpec((None, None, 1, bk), lambda h, j: (h, j, 0, 0))],
        out_shape=[jax.ShapeDtypeStruct((t, d), f32),
                   jax.ShapeDtypeStruct((n_heads, n_q, 1, bq), f32),
                   jax.ShapeDtypeStruct((t, d), bf16), jax.ShapeDtypeStruct((t, d), bf16),
                   jax.ShapeDtypeStruct((n_heads, n_kv, 1, bk), f32)],
        scratch_shapes=[pltpu.VMEM((bk, bq), f32), pltpu.VMEM((bk, bq), f32),
                        pltpu.VMEM((bk, bq), bf16), pltpu.VMEM((bk, bq), bf16),
                        pltpu.VMEM((n_q, HEAD_DIM, bq), f32),
                        pltpu.VMEM((bk, HEAD_DIM), f32), pltpu.VMEM((bk, HEAD_DIM), f32),
                        pltpu.VMEM((bk, 1), f32)],
        compiler_params=_params("parallel", "arbitrary"),
    )(q, do, cl_rows, delta_rows, k, v, kt, ckey)


def _fgate_bwd(dlogf, flog, fbias, name):
    t, n = flog.shape
    rows = _tile(t, 640, LANES)

    def body(dl_ref, fl_ref, fb_ref, o_ref, sum_ref):
        i = pl.program_id(0)

        @pl.when(i == 0)
        def _():
            sum_ref[...] = jnp.zeros_like(sum_ref)

        r = i * rows + lax.broadcasted_iota(jnp.int32, (rows, n), 0)
        g = dl_ref[...] * jax.nn.sigmoid(-(fl_ref[...] + fb_ref[...]))
        g = jnp.where(r >= PAD, g, 0.0)
        o_ref[...] = g
        sum_ref[...] += jnp.sum(g, axis=0, keepdims=True)

    blk = pl.BlockSpec((rows, n), lambda i: (i, 0))
    vec = pl.BlockSpec((1, n), lambda i: (0, 0))
    return pl.pallas_call(
        body, name=name, grid=(t // rows,),
        in_specs=[blk, blk, vec], out_specs=[blk, vec],
        out_shape=[jax.ShapeDtypeStruct((t, n), f32), jax.ShapeDtypeStruct((1, n), f32)],
        compiler_params=_params("arbitrary"),
    )(dlogf, flog, fbias)


def _place_shard(w, idx, name):
    n_l, r, c_n = w.shape
    out_dtype = bf16 if r * c_n > 2 ** 16 else w.dtype
    tr = _tile(r, 512, 16) if r % 16 == 0 else r

    def body(idx_ref, w_ref, o_ref):
        o_ref[...] = w_ref[...].astype(out_dtype)

    grid_spec = pltpu.PrefetchScalarGridSpec(
        num_scalar_prefetch=1, grid=(n_l, r // tr),
        in_specs=[pl.BlockSpec((None, tr, c_n), lambda l, i, idx: (l, i, 0))],
        out_specs=pl.BlockSpec((None, None, tr, c_n), lambda l, i, idx: (idx[0], l, i, 0)))
    return pl.pallas_call(
        body, name=name, grid_spec=grid_spec,
        out_shape=jax.ShapeDtypeStruct((N_CHIPS, n_l, r, c_n), out_dtype),
        compiler_params=_params("parallel", "parallel"),
    )(idx, w)


def _plan_gather_ici(items):
    def plan(srcs, bufs, news, p):
        out = []
        for name, layer, r2 in items:
            mine = bufs[name].at[p.me, layer, pl.ds(p.c * r2, r2)]
            out += [(mine, mine, (*chip, p.c)) for chip in p.chips]
        return out
    return plan, 3 * len(items)


def _plan_gather_d2d(items):
    def plan(srcs, bufs, news, p):
        out = []
        for name, layer, r2 in items:
            for px, py in p.chips:
                landed = bufs[name].at[2 * px + py, layer, pl.ds(p.c * r2, r2)]
                out.append((landed, landed, p.sib))
        return out
    return plan, 3 * len(items)


def _plan_pair_exchange(names, r2s):
    def plan(srcs, bufs, news, p):
        out = []
        for name, r2 in zip(names, r2s):
            for k, slot in enumerate(p.slots):
                out.append((srcs["G_" + name].at[slot, pl.ds((1 - p.c) * r2, r2)],
                            news["PAIR_" + name].at[k], p.sib))
        return out
    return plan, 4 * len(names)


def _plan_chip_exchange(names):
    def plan(srcs, bufs, news, p):
        out = []
        for name in names:
            for k, chip in enumerate(p.chips):
                out.append((srcs["SEND_" + name].at[k], news["RECV_" + name].at[k], (*chip, p.c)))
        return out
    return plan, 3 * len(names)


def _plan_pair_share(items):
    def plan(srcs, bufs, news, p):
        out = []
        for name, layer, r2 in items:
            mine = bufs["RED_" + name].at[layer, pl.ds(p.c * r2, r2)]
            out.append((mine, mine, p.sib))
        return out
    return plan, len(items)


def _rs_prepare(gs, pairs, idx, name):
    n = len(gs)
    _, r2, c_n = pairs[0].shape
    tr = _tile(r2, 256, 8)
    nb = r2 // tr

    def body(idx_ref, *refs):
        for g_ref, p_ref, o_ref in zip(refs[:n], refs[n:2 * n], refs[2 * n:]):
            o_ref[...] = (g_ref[...] + p_ref[...]).astype(bf16)

    g_spec = pl.BlockSpec((None, tr, c_n), lambda k, i, idx: (idx[k + 1], idx[4] * nb + i, 0))
    p_spec = pl.BlockSpec((None, tr, c_n), lambda k, i, idx: (k + 1, i, 0))
    grid_spec = pltpu.PrefetchScalarGridSpec(
        num_scalar_prefetch=1, grid=(3, nb), in_specs=[g_spec] * n + [p_spec] * n,
        out_specs=[pl.BlockSpec((None, tr, c_n), lambda k, i, idx: (k, i, 0))] * n)
    return pl.pallas_call(
        body, name=name, grid_spec=grid_spec,
        out_shape=[jax.ShapeDtypeStruct((3, r2, c_n), bf16)] * n,
        compiler_params=_params("parallel", "parallel"),
    )(idx, *gs, *pairs)


def _rs_finish(gs, pairs, recvs, idx, layer, n_layers, intos, name):
    n = len(gs)
    _, r2, c_n = pairs[0].shape
    tr = _tile(r2, 256, 8)
    nb = r2 // tr

    def body(idx_ref, *refs):
        for j in range(n):
            g_ref, p_ref, r0_ref, r1_ref, r2_ref = refs[5 * j:5 * j + 5]
            acc = g_ref[...] + p_ref[...]
            acc = acc + r0_ref[...].astype(f32)
            acc = acc + r1_ref[...].astype(f32)
            acc = acc + r2_ref[...].astype(f32)
            refs[len(refs) - n + j][...] = acc

    def rspec(k):
        return pl.BlockSpec((None, tr, c_n), functools.partial(lambda i, idx, kk: (kk, i, 0), kk=k))

    item_specs = [pl.BlockSpec((None, tr, c_n), lambda i, idx: (idx[0], idx[4] * nb + i, 0)),
                  rspec(0), rspec(0), rspec(1), rspec(2)]
    in_specs = item_specs * n
    args = [idx]
    for g, pair, recv in zip(gs, pairs, recvs):
        args += [g, pair, recv, recv, recv]
    aliases = {}
    if intos is not None:
        in_specs = in_specs + [pl.BlockSpec(memory_space=pl.ANY)] * n
        args += list(intos)
        aliases = {1 + 5 * n + j: j for j in range(n)}
    grid_spec = pltpu.PrefetchScalarGridSpec(
        num_scalar_prefetch=1, grid=(nb,), in_specs=in_specs,
        out_specs=[pl.BlockSpec((None, tr, c_n), lambda i, idx: (layer, idx[4] * nb + i, 0))] * n)
    return pl.pallas_call(
        body, name=name, grid_spec=grid_spec,
        out_shape=[jax.ShapeDtypeStruct((n_layers, 2 * r2, c_n), f32)] * n,
        input_output_aliases=aliases,
        compiler_params=_params("parallel"),
    )(*args)


def _adamw(ws, gs, ms, vs, layer, intos, name):
    n = len(ws)
    n_l, r, c_n = ws[0].shape
    tr = _tile(r, 256, 8)

    def body(*refs):
        for j in range(n):
            w_ref, g_ref, m_ref, v_ref = refs[4 * j:4 * j + 4]
            o0 = len(refs) - 4 * n + 4 * j
            d_ref, mo_ref, vo_ref, go_ref = refs[o0:o0 + 4]
            gv = g_ref[...]
            go_ref[...] = gv
            mn = ADAM_B1 * m_ref[...] + (1.0 - ADAM_B1) * gv
            vn = ADAM_B2 * v_ref[...] + (1.0 - ADAM_B2) * (gv * gv)
            m_hat = mn / (1.0 - ADAM_B1 ** ADAM_STEP)
            v_hat = vn / (1.0 - ADAM_B2 ** ADAM_STEP)
            d_ref[...] = -ADAM_LR * (m_hat / (jnp.sqrt(v_hat) + ADAM_EPS) + ADAM_WD * w_ref[...])
            mo_ref[...] = mn
            vo_ref[...] = vn

    blk = pl.BlockSpec((None, tr, c_n), lambda i: (layer, i, 0))
    shp = jax.ShapeDtypeStruct((n_l, r, c_n), f32)
    in_specs = [blk] * (4 * n)
    args = []
    for w, g, m, v in zip(ws, gs, ms, vs):
        args += [w, g, m, v]
    aliases = {}
    if intos is not None:
        in_specs = in_specs + [pl.BlockSpec(memory_space=pl.ANY)] * (4 * n)
        for into in intos:
            args += list(into)
        aliases = {4 * n + j: j for j in range(4 * n)}
    res = pl.pallas_call(
        body, name=name, grid=(r // tr,),
        in_specs=in_specs, out_specs=[blk] * (4 * n), out_shape=[shp] * (4 * n),
        input_output_aliases=aliases,
        compiler_params=_params("parallel"),
    )(*args)
    return [tuple(res[4 * j:4 * j + 4]) for j in range(n)]


def kernel(x, meta, ffn1_wg, ffn1_wu, ffn1_wd, ffn2_wg, ffn2_wu, ffn2_wd, ln_gain, ln_bias, conv_w_in, conv_w, conv_w_out, kv_w, f_bias, attn_w_q, attn_w_o, loss_target, m_meta, m_ffn1_wg, m_ffn1_wu, m_ffn1_wd, m_ffn2_wg, m_ffn2_wu, m_ffn2_wd, m_ln_gain, m_ln_bias, m_conv_w_in, m_conv_w, m_conv_w_out, m_kv_w, m_f_bias, m_attn_w_q, m_attn_w_o, v_meta, v_ffn1_wg, v_ffn1_wu, v_ffn1_wd, v_ffn2_wg, v_ffn2_wu, v_ffn2_wd, v_ln_gain, v_ln_bias, v_conv_w_in, v_conv_w, v_conv_w_out, v_kv_w, v_f_bias, v_attn_w_q, v_attn_w_o):
    seq, d = x.shape[1], x.shape[2]
    t = PAD + N_META + seq
    n_heads = d // HEAD_DIM
    dq = d // N_CHIPS
    n_kv = kv_w.shape[1]
    x2 = x[0]
    target = loss_target[0]

    def rows8(a):
        return jnp.pad(a, ((0, 8 - a.shape[0]), (0, 0)))

    def small_pack(mt, g, b, cw, fb):
        fb_row = jnp.pad(fb, (0, mt.shape[1] - n_heads))[None]
        return jnp.concatenate([mt, rows8(g.reshape(6, -1)), rows8(b.reshape(6, -1)),
                                rows8(cw.reshape(3, -1)), rows8(fb_row)], axis=0)

    w_small = small_pack(meta, ln_gain, ln_bias, conv_w, f_bias)

    cx, cy, c = lax.axis_index("x"), lax.axis_index("y"), lax.axis_index("c")
    idx = jnp.stack([2 * cx + cy, 2 * (1 - cx) + cy, 2 * cx + (1 - cy), 2 * (1 - cx) + (1 - cy), c]
                    ).astype(jnp.int32)

    def tr(a):
        return a.transpose(0, 2, 1)

    w3 = {"wg1": tr(ffn1_wg), "wu1": tr(ffn1_wu), "wd1": ffn1_wd, "wg2": tr(ffn2_wg),
          "wu2": tr(ffn2_wu), "wd2": ffn2_wd, "win": conv_w_in, "wout": conv_w_out, "kv": kv_w[None],
          "wq": attn_w_q, "wo": attn_w_o, "small": w_small[None]}
    transposed = ("wg1", "wu1", "wg2", "wu2")
    buf = {n: _place_shard(w, idx, "place_shard") for n, w in w3.items()}

    def split(item):
        name, layer = item.split(".")
        return name, int(layer)

    def gather_stage(planner, items):
        triples = [(n, l, buf[n].shape[2] // 2) for n, l in map(split, items)]
        plan, n_copies = planner(triples)
        return dict(plan=plan, n=n_copies, bufs={n: buf[n] for n, _, _ in triples})

    def ici(items):
        return gather_stage(_plan_gather_ici, items)

    def d2d(items):
        return gather_stage(_plan_gather_d2d, items)

    def pair_exchange(items):
        r2s = [buf["G_" + it].shape[1] // 2 for it in items]
        plan, n_copies = _plan_pair_exchange(items, r2s)
        news = {"PAIR_" + it: jax.ShapeDtypeStruct((N_CHIPS, r2, buf["G_" + it].shape[2]), f32)
                for it, r2 in zip(items, r2s)}
        return dict(plan=plan, n=n_copies, srcs={"G_" + it: buf["G_" + it] for it in items}, news=news)

    def chip_exchange(items):
        plan, n_copies = _plan_chip_exchange(items)
        srcs = {"SEND_" + it: buf["SEND_" + it] for it in items}
        news = {"RECV_" + it: jax.ShapeDtypeStruct(s.shape, s.dtype)
                for it, s in ((it, buf["SEND_" + it]) for it in items)}
        return dict(plan=plan, n=n_copies, srcs=srcs, news=news)

    def pair_share(items):
        triples = [(n, l, buf["RED_" + n].shape[1] // 2) for n, l in map(split, items)]
        plan, n_copies = _plan_pair_share(triples)
        return dict(plan=plan, n=n_copies, bufs={"RED_" + n: buf["RED_" + n] for n, _, _ in triples})

    def run(fn, *args, stages=(), name=None, **kw):
        comm = _Copies()
        for st in (stages() if callable(stages) else stages):
            comm.add(st["plan"], st["n"], srcs=st.get("srcs"), bufs=st.get("bufs"), news=st.get("news"))
        out = _copy_call(comm, name) if fn is None else fn(*args, comm=comm, **kw)
        buf.update(comm.out_bufs)
        buf.update(comm.out_news)
        return out

    first = ["wg1.0", "wu1.0", "small.0"]
    run(None, stages=[ici(first)], name="gather_first_ici")
    run(None, stages=[d2d(first)], name="gather_first_d2d")
    small = buf["small"].reshape(N_CHIPS, SMALL_ROWS, dq).transpose(1, 0, 2).reshape(SMALL_ROWS, d)
    meta_full = small[:N_META]
    gains = small[16:22].reshape(DEPTH, 3, 1, d)
    biases = small[24:30].reshape(DEPTH, 3, 1, d)
    conv_w_full = small[32:35]
    fb_pad = jnp.pad(f_bias, (0, LANES - n_heads))[None]
    down1 = ["wd1.0"]
    ffn2_l0 = ["win.0", "wout.0", "wg2.0", "wu2.0", "wd2.0", "kv.0"]
    attn_ffn2_l1 = ["wq.0", "wo.0", "wg2.1", "wu2.1", "wd2.1"]
    ffn1_l1 = ["wg1.1", "wu1.1", "wd1.1"]

    meta_pad = jnp.concatenate([jnp.zeros((PAD, d), f32), meta_full], axis=0)
    h0, h0b = run(_embed, meta_pad, x2, "embed", stages=[ici(down1)])
    a1, b1, s1 = run(_ffn_up, h0b, buf["wg1"], buf["wu1"], 0, "ffn_up",
                     stages=lambda: [d2d(down1), ici(ffn2_l0)])
    r1, h1, h1b = run(_down_ln, s1, buf["wd1"], 0, h0, gains[0, 0], biases[0, 0], 0.5, "ffn_down_ln",
                      stages=lambda: [d2d(ffn2_l0), ici(attn_ffn2_l1)])
    n_in = conv_w_in.shape[-1]
    w_in = buf["win"].reshape(N_CHIPS, d, n_in)
    w_out = buf["wout"].reshape(1, 1, d, d)
    p = run(_nn_matmul, h1b, w_in, f32, "conv_in", stages=lambda: [d2d(attn_ffn2_l1), ici(ffn1_l1)])
    z = _conv_fwd(p, conv_w_full, "conv_fwd")
    r2, h2, h2b = run(_down_ln, z[None], w_out, 0, h1, gains[0, 1], biases[0, 1], 1.0, "mix_out_ln",
                      stages=lambda: [d2d(ffn1_l1)])
    wg1, wu1, wd1, wg2, wu2, wd2 = (buf[n] for n in ("wg1", "wu1", "wd1", "wg2", "wu2", "wd2"))
    w_q = buf["wq"].reshape(1, d, d)
    w_o = buf["wo"].reshape(1, 1, d, d)
    kv_full = buf["kv"].reshape(N_CHIPS, d, n_kv).transpose(1, 0, 2).reshape(d, N_CHIPS * n_kv)
    w_k = kv_full[:, :d][None]
    w_v = kv_full[:, d:2 * d][None]
    w_f = jnp.pad(kv_full[:, 2 * d:], ((0, 0), (0, LANES - n_heads)))[None]
    a2, b2, s2 = _ffn_up(h2b, wg2, wu2, 0, "ffn_up")
    r3, h3, h3b = _down_ln(s2, wd2, 0, h2, gains[0, 2], biases[0, 2], 0.5, "ffn_down_ln")
    kk = _nn_matmul(h3b, w_k, bf16, "proj_bf16")
    vv = _nn_matmul(h3b, w_v, bf16, "proj_bf16")
    flog = _nn_matmul(h3b, w_f, f32, "proj_gate")
    cum = _row_scan(flog, "gate_cumsum", fb_pad)
    bk = _tile(t, 640, LANES)
    c_ht = cum[:, :n_heads].T
    c_keys = jnp.where(jnp.arange(t)[None, :] < PAD, 1e30, c_ht)
    cq_rep = jnp.broadcast_to(c_ht[:, :, None], (n_heads, t, LANES))
    ck_rep = jnp.broadcast_to(c_keys[:, :, None], (n_heads, t, LANES))
    ck_rows = c_keys.reshape(n_heads, t // bk, 1, bk)
    a3, b3, s3 = _ffn_up(h3b, wg1, wu1, 1, "ffn_up")
    r4, h4, h4b = _down_ln(s3, wd1, 1, h3, gains[1, 0], biases[1, 0], 0.5, "ffn_down_ln")
    q = _nn_matmul(h4b, w_q, bf16, "proj_q", out_scale=1.0 / math.sqrt(HEAD_DIM))
    o, o32, cl = _attn_fwd(q, kk, vv, cq_rep, ck_rows, "attn_fwd")
    r5, h5, h5b = _down_ln(o[None], w_o, 0, h4, gains[1, 1], biases[1, 1], 1.0, "mix_out_ln")
    a4, b4, s4 = _ffn_up(h5b, wg2, wu2, 1, "ffn_up")
    r6, h6, _ = _down_ln(s4, wd2, 1, h5, gains[1, 2], biases[1, 2], 0.5, "ffn_down_ln")
    dr6, dr6b, dg12, db12, sq = _loss_head(h6, target, r6, gains[1, 2], "loss_head")
    loss_part = 0.5 * sq[0, 0] / d

    m_small = small_pack(m_meta, m_ln_gain, m_ln_bias, m_conv_w, m_f_bias)
    v_small = small_pack(v_meta, v_ln_gain, v_ln_bias, v_conv_w, v_f_bias)
    m3 = {"wg1": tr(m_ffn1_wg), "wu1": tr(m_ffn1_wu), "wd1": m_ffn1_wd, "wg2": tr(m_ffn2_wg),
          "wu2": tr(m_ffn2_wu), "wd2": m_ffn2_wd, "win": m_conv_w_in, "wout": m_conv_w_out,
          "kv": m_kv_w[None], "wq": m_attn_w_q, "wo": m_attn_w_o, "small": m_small[None]}
    v3 = {"wg1": tr(v_ffn1_wg), "wu1": tr(v_ffn1_wu), "wd1": v_ffn1_wd, "wg2": tr(v_ffn2_wg),
          "wu2": tr(v_ffn2_wu), "wd2": v_ffn2_wd, "win": v_conv_w_in, "wout": v_conv_w_out,
          "kv": v_kv_w[None], "wq": v_attn_w_q, "wo": v_attn_w_o, "small": v_small[None]}
    stepped = {}

    def alike(items):
        groups = {}
        for it in items:
            n, l = split(it)
            groups.setdefault((w3[n].shape, l, ("RED_" + n) in buf, n in stepped), []).append(it)
        return groups.values()

    def prepare(items):
        for group in alike(items):
            sends = _rs_prepare([buf["G_" + it] for it in group], [buf["PAIR_" + it] for it in group],
                                idx, "grad_prepare")
            buf.update({"SEND_" + it: s for it, s in zip(group, sends)})

    def finish(items):
        for group in alike(items):
            names = [split(it)[0] for it in group]
            layer = split(group[0])[1]
            intos = [buf["RED_" + n] for n in names] if ("RED_" + names[0]) in buf else None
            reds = _rs_finish([buf["G_" + it] for it in group], [buf["PAIR_" + it] for it in group],
                              [buf["RECV_" + it] for it in group], idx, layer, w3[names[0]].shape[0],
                              intos, "grad_finish")
            buf.update({"RED_" + n: r for n, r in zip(names, reds)})

    def adam(items, grads=None):
        for group in alike(items):
            names = [split(it)[0] for it in group]
            layer = split(group[0])[1]
            gs = [buf["RED_" + n] if grads is None else grads[n] for n in names]
            intos = [stepped[n] for n in names] if names[0] in stepped else None
            res = _adamw([w3[n] for n in names], gs, [m3[n] for n in names], [v3[n] for n in names],
                         layer, intos, "adamw")
            stepped.update(dict(zip(names, res)))

    def ffn_bwd(dr, drb, hb_in, a, b, s, f, layer, on_act=(), after_act=None, on_dwd=(),
                after_dwd=None, on_dx=(), ln=None):
        da, db = run(_ffn_bwd_act, drb, buf["wd" + f], layer, a, b, "ffn_bwd_act", stages=on_act)
        if after_act is not None:
            after_act()
        (buf[f"G_wd{f}.{layer}"],) = run(_tn_matmul, [(s, "stack", drb, "shared")], N_CHIPS, 0.5,
                                         "ffn_dwd", stages=on_dwd)
        if after_dwd is not None:
            after_dwd()
        buf[f"G_wg{f}.{layer}"], buf[f"G_wu{f}.{layer}"] = _tn_matmul(
            [(da, "stack", hb_in, "shared"), (db, "stack", hb_in, "shared")], N_CHIPS, 1.0, "ffn_dwgu")
        return run(_nt_sum, [(da, "stack", buf["wg" + f], layer, True),
                             (db, "stack", buf["wu" + f], layer, True)],
                   dr, ALPHA, N_CHIPS, f32, "ffn_dx", stages=on_dx, ln=ln)

    ffn2_1 = ["wg2.1", "wu2.1", "wd2.1"]
    ffn1_1 = ["wg1.1", "wu1.1", "wd1.1"]
    ffn2_0 = ["wg2.0", "wu2.0", "wd2.0"]
    conv_items = ["wout.0", "win.0"]

    dr5, dr5b, dg11, db11 = ffn_bwd(dr6, dr6b, h5b, a4, b4, s4, "2", 1,
                                    on_dx=lambda: [pair_exchange(ffn2_1)], ln=(r5, gains[1, 1]))
    prepare(ffn2_1)
    (dwo,) = _tn_matmul([(o, "shared", dr5b, "shared")], 1, 1.0, "sq_dw")
    buf["G_wo.0"] = dwo.reshape(N_CHIPS, dq, d)
    do, delta = run(_attn_do, dr5b, w_o[0, 0], o32, bk, "attn_do",
                    stages=lambda: [pair_exchange(["wo.0"])])
    prepare(["wo.0"])
    kt = kk.reshape(t // bk, bk, n_heads, HEAD_DIM).transpose(2, 0, 3, 1)
    dq_att, dc_q, dk, dv, dc_k = run(_attn_bwd, q, kk, vv, kt, do, ck_rep, cl, delta, "attn_bwd",
                                     stages=lambda: [chip_exchange(ffn2_1 + ["wo.0"])])
    finish(ffn2_1 + ["wo.0"])
    dc = (dc_q + dc_k).reshape(n_heads, t)
    (dwq,) = run(_tn_matmul, [(h4b, "shared", dq_att, "shared")], 1, 1.0, "sq_dw",
                 stages=lambda: [pair_share(ffn2_1 + ["wo.0"])])
    buf["G_wq.0"] = dwq.reshape(N_CHIPS, dq, d)
    adam(ffn2_1 + ["wo.0"])
    dr4, dr4b, dg10, db10 = run(_nt_sum, [(dq_att, "cols", w_q, None)], dr5, ALPHA, 1, f32,
                                "sq_dx_res", stages=lambda: [pair_exchange(["wq.0"])],
                                ln=(r4, gains[1, 0]))
    prepare(["wq.0"])
    dh3a = ffn_bwd(dr4, dr4b, h3b, a3, b3, s3, "1", 1,
                   on_act=lambda: [chip_exchange(["wq.0"])],
                   on_dx=lambda: [pair_exchange(ffn1_1)])
    prepare(ffn1_1)
    finish(["wq.0"])
    dc_t = jnp.pad(dc.T, ((0, 0), (0, LANES - n_heads)))
    dlogf = _row_scan(dc_t, "rev_cumsum", reverse=True)
    dfl, dfb_cols = _fgate_bwd(dlogf, flog, fb_pad, "gate_bwd")
    dwk, dwv, dwf = run(_tn_matmul, [(h3b, "shared", g, "shared") for g in (dk, dv, dfl)], 1, 1.0,
                        "kv_dw",
                        stages=lambda: [chip_exchange(ffn1_1), pair_share(["wq.0"])])
    adam(["wq.0"])

    def by_chip(full):
        rows = full.shape[0]
        return full.reshape(rows, N_CHIPS, full.shape[1] // N_CHIPS).transpose(1, 0, 2)

    buf["G_kv.0"] = by_chip(jnp.concatenate([dwk[0], dwv[0], dwf[0][:, :n_heads]], axis=1))
    dr3, dr3b, dg02, db02 = run(
        _nt_sum, [(dk, "cols", w_k, None), (dv, "cols", w_v, None), (dfl, "cols", w_f, None)],
        dh3a, 1.0, 1, f32, "kv_dx", stages=lambda: [pair_exchange(["kv.0"])], ln=(r3, gains[0, 2]))
    prepare(["kv.0"])
    finish(ffn1_1)
    dr2, dr2b, dg01, db01 = ffn_bwd(dr3, dr3b, h2b, a2, b2, s2, "2", 0,
                                    on_act=lambda: [chip_exchange(["kv.0"]), pair_share(ffn1_1)],
                                    after_act=lambda: (adam(ffn1_1), finish(["kv.0"])),
                                    on_dwd=lambda: [pair_share(["kv.0"])],
                                    after_dwd=lambda: adam(["kv.0"]),
                                    on_dx=lambda: [pair_exchange(ffn2_0)], ln=(r2, gains[0, 1]))
    prepare(ffn2_0)
    (dwout,) = _tn_matmul([(z, "shared", dr2b, "shared")], 1, 1.0, "sq_dw")
    buf["G_wout.0"] = dwout.reshape(N_CHIPS, dq, d)
    dz = _nt_sum([(dr2b, "cols", w_out[0], None)], None, 1.0, 1, f32, "sq_dx_f32")
    dp, dconv_w = run(_conv_bwd, dz, p, conv_w_full, "conv_bwd",
                      stages=lambda: [chip_exchange(ffn2_0)])
    (buf["G_win.0"],) = _tn_matmul([(h1b, "shared", dp, "cols")], N_CHIPS, 1.0, "conv_dwin")
    finish(ffn2_0)
    dr1, dr1b, dg00, db00 = run(_nt_sum, [(dp, "cols", w_in, None)], dr2, ALPHA, N_CHIPS, f32,
                                "conv_dx",
                                stages=lambda: [pair_exchange(conv_items), pair_share(ffn2_0)],
                                ln=(r1, gains[0, 0]))
    prepare(conv_items)
    adam(ffn2_0)
    gate_up = ["wg1.0", "wu1.0"]
    da, db = run(_ffn_bwd_act, dr1b, buf["wd1"], 0, a1, b1, "ffn_bwd_act",
                 stages=lambda: [chip_exchange(conv_items)])
    finish(conv_items)
    buf["G_wg1.0"], buf["G_wu1.0"] = run(
        _tn_matmul, [(da, "stack", h0b, "shared"), (db, "stack", h0b, "shared")], N_CHIPS, 1.0,
        "ffn_dwgu", stages=lambda: [pair_share(conv_items)])
    adam(conv_items)
    (buf["G_wd1.0"],) = run(_tn_matmul, [(s1, "stack", dr1b, "shared")], N_CHIPS, 0.5, "ffn_dwd",
                            stages=lambda: [pair_exchange(gate_up)])
    prepare(gate_up)
    dh0 = run(_nt_sum, [(da, "stack", buf["wg1"], 0, True), (db, "stack", buf["wu1"], 0, True)],
              dr1, ALPHA, N_CHIPS, f32, "ffn_dx",
              stages=lambda: [chip_exchange(gate_up), pair_exchange(["wd1.0"])])
    prepare(["wd1.0"])
    finish(gate_up)
    grad_x = dh0[PAD + N_META:][None]
    dmeta = dh0[PAD:PAD + N_META]
    buf["G_small.0"] = by_chip(jnp.concatenate(
        [dmeta, rows8(jnp.concatenate([dg00, dg01, dg02, dg10, dg11, dg12], axis=0)),
         rows8(jnp.concatenate([db00, db01, db02, db10, db11, db12], axis=0)),
         rows8(dconv_w), jnp.zeros((8, d), f32)], axis=0))
    run(None, stages=lambda: [chip_exchange(["wd1.0"]), pair_exchange(["small.0"]), pair_share(gate_up)],
        name="grad_tail_1")
    prepare(["small.0"])
    finish(["wd1.0"])
    adam(gate_up)
    run(None, stages=lambda: [chip_exchange(["small.0"]), pair_share(["wd1.0"])], name="grad_tail_2")
    finish(["small.0"])
    adam(["wd1.0"])
    run(None, stages=lambda: [pair_share(["small.0"])], name="grad_tail_3")

    tail = jnp.zeros((LANES,), f32).at[:n_heads].set(dfb_cols[0, :n_heads]).at[n_heads].set(loss_part)
    tail = lax.psum(tail, ("x", "y", "c"))
    loss = tail[n_heads]
    g_fb = tail[:n_heads]
    g_small = jnp.concatenate([buf["RED_small"][0, :40],
                               rows8(jnp.pad(g_fb, (0, dq - n_heads))[None])], axis=0)
    adam(["small.0"], grads={"small": g_small[None]})

    def unpack(pk):
        return (pk[:16], pk[16:22].reshape(DEPTH, 3, dq), pk[24:30].reshape(DEPTH, 3, dq),
                pk[32:35].reshape(1, 3, dq), pk[40, :n_heads])

    def order(pick):
        mt, g, b, cw, fb = unpack(pick("small")[0])
        big = {n: pick(n) for n in w3 if n != "small"}
        big["kv"] = big["kv"][0]
        for n in transposed:
            big[n] = tr(big[n])
        return [mt, big["wg1"], big["wu1"], big["wd1"], big["wg2"], big["wu2"], big["wd2"], g, b,
                big["win"], cw, big["wout"], big["kv"], fb, big["wq"], big["wo"]]

    return (loss, grad_x, *order(lambda n: stepped[n][3]), *order(lambda n: stepped[n][0]),
            *order(lambda n: stepped[n][1]), *order(lambda n: stepped[n][2]))
```

```python
import functools
import math

import jax
import jax.numpy as jnp
from jax import lax
from jax.experimental import pallas as pl
from jax.experimental.pallas import tpu as pltpu

f32 = jnp.float32
bf16 = jnp.bfloat16

N_META = 16
PAD = 112
HEAD_DIM = 128
DEPTH = 2
LN_EPS = 1e-5
ALPHA = (2 * DEPTH) ** 0.25
NEG_INF = -1e30
N_CHIPS = 4
SMALL_ROWS = 48
LANES = 128

ADAM_LR = 0.001
ADAM_B1 = 0.9
ADAM_B2 = 0.999
ADAM_EPS = 1e-08
ADAM_WD = 0.01
ADAM_STEP = 10

VMEM_LIMIT_BYTES = 56 * 1024 * 1024
ROWS_WIDE = 1664
ROWS_ACC = 1040
ROWS_ACC_LN = 832
STRIP = 16
MESH = pl.DeviceIdType.MESH

NT_DIMS = (((1,), (1,)), ((), ()))
TN_DIMS = (((0,), (0,)), ((), ()))


def _tile(n, target, mult):
    best = None
    for d in range(mult, min(n, target) + 1, mult):
        if n % d == 0:
            best = d
    assert best is not None, (n, target, mult)
    return best


def _params(*sem):
    return pltpu.CompilerParams(dimension_semantics=sem, vmem_limit_bytes=VMEM_LIMIT_BYTES)


class _Place:
    def __init__(self):
        self.cx, self.cy, self.c = lax.axis_index("x"), lax.axis_index("y"), lax.axis_index("c")
        self.chips = [(1 - self.cx, self.cy), (self.cx, 1 - self.cy), (1 - self.cx, 1 - self.cy)]
        self.me = 2 * self.cx + self.cy
        self.slots = [self.me] + [2 * px + py for px, py in self.chips]
        self.sib = (self.cx, self.cy, 1 - self.c)


class _Copies:
    def __init__(self):
        self.srcs, self.bufs, self.news = {}, {}, {}
        self.plans = []
        self.out_bufs, self.out_news = {}, {}

    def add(self, plan, n_copies, srcs=None, bufs=None, news=None):
        for have, more in ((self.srcs, srcs), (self.bufs, bufs), (self.news, news)):
            for key, val in (more or {}).items():
                assert key not in have or have[key] is val, key
                have[key] = val
        self.plans.append((plan, n_copies))

    def empty(self):
        return not self.plans

    def count(self):
        return sum(n for _, n in self.plans)

    def copies(self, src_refs, buf_refs, new_refs, send, recv):
        place = _Place()
        srcs = dict(zip(self.srcs, src_refs))
        bufs = dict(zip(self.bufs, buf_refs))
        news = dict(zip(self.news, new_refs))
        out = []
        for plan, n_copies in self.plans:
            triples = plan(srcs, bufs, news, place)
            assert len(triples) == n_copies
            for src, dst, dev in triples:
                n = len(out)
                out.append(pltpu.make_async_remote_copy(
                    src_ref=src, dst_ref=dst, send_sem=send.at[n], recv_sem=recv.at[n],
                    device_id=dev, device_id_type=MESH))
        return out

    def land(self, results):
        n_b = len(self.bufs)
        self.out_bufs = dict(zip(self.bufs, results[:n_b]))
        self.out_news = dict(zip(self.news, results[n_b:]))


def _pallas(comm, body, *, name, grid, in_specs, out_specs, out_shape, compiler_params,
            scratch_shapes=(), input_output_aliases=None):
    aliases = dict(input_output_aliases or {})
    if comm is None or comm.empty():
        return pl.pallas_call(body, name=name, grid=grid, in_specs=in_specs, out_specs=out_specs,
                              out_shape=out_shape, scratch_shapes=list(scratch_shapes),
                              input_output_aliases=aliases, compiler_params=compiler_params)
    single = not isinstance(out_shape, (list, tuple))
    out_shapes = [out_shape] if single else list(out_shape)
    out_specs_l = [out_specs] if single else list(out_specs)
    n_in, n_out, n_scr = len(in_specs), len(out_shapes), len(scratch_shapes)
    n_s, n_b, n_n = len(comm.srcs), len(comm.bufs), len(comm.news)
    n_copies = comm.count()

    def wrapped(*refs):
        ins = refs[:n_in]
        src_refs = refs[n_in:n_in + n_s]
        o0 = n_in + n_s + n_b
        outs = refs[o0:o0 + n_out]
        buf_refs = refs[o0 + n_out:o0 + n_out + n_b]
        new_refs = refs[o0 + n_out + n_b:o0 + n_out + n_b + n_n]
        rest = refs[o0 + n_out + n_b + n_n:]
        scratch, (send, recv) = rest[:n_scr], rest[n_scr:]
        ids = [pl.program_id(a) for a in range(len(grid))]
        first = functools.reduce(jnp.logical_and, [i == 0 for i in ids])
        last = functools.reduce(jnp.logical_and, [i == g - 1 for i, g in zip(ids, grid)])

        @pl.when(first)
        def _():
            for cp in comm.copies(src_refs, buf_refs, new_refs, send, recv):
                cp.start()

        body(*ins, *outs, *scratch)

        @pl.when(last)
        def _():
            for cp in comm.copies(src_refs, buf_refs, new_refs, send, recv):
                cp.wait()

    hbm = pl.BlockSpec(memory_space=pl.ANY)
    for j in range(n_b):
        aliases[n_in + n_s + j] = n_out + j
    call = pl.pallas_call(
        wrapped, name=name, grid=grid,
        in_specs=[*in_specs, *([hbm] * (n_s + n_b))],
        out_specs=[*out_specs_l, *([hbm] * (n_b + n_n))],
        out_shape=[*out_shapes,
                   *[jax.ShapeDtypeStruct(a.shape, a.dtype) for a in comm.bufs.values()],
                   *comm.news.values()],
        scratch_shapes=[*scratch_shapes, pltpu.SemaphoreType.DMA((n_copies,)),
                        pltpu.SemaphoreType.DMA((n_copies,))],
        input_output_aliases=aliases, compiler_params=compiler_params)

    def run(*args):
        res = call(*args, *comm.srcs.values(), *comm.bufs.values())
        comm.land(res[n_out:])
        return res[0] if single else res[:n_out]

    return run


def _copy_call(comm, name):
    n_s, n_b, n_n = len(comm.srcs), len(comm.bufs), len(comm.news)
    n_copies = comm.count()

    def body(*refs):
        src_refs = refs[:n_s]
        buf_refs = refs[n_s + n_b:n_s + 2 * n_b]
        new_refs = refs[n_s + 2 * n_b:n_s + 2 * n_b + n_n]
        send, recv = refs[n_s + 2 * n_b + n_n:]
        copies = comm.copies(src_refs, buf_refs, new_refs, send, recv)
        for cp in copies:
            cp.start()
        for cp in copies:
            cp.wait()

    hbm = pl.BlockSpec(memory_space=pl.ANY)
    res = pl.pallas_call(
        body, name=name,
        in_specs=[hbm] * (n_s + n_b), out_specs=[hbm] * (n_b + n_n),
        out_shape=[*[jax.ShapeDtypeStruct(a.shape, a.dtype) for a in comm.bufs.values()],
                   *comm.news.values()],
        input_output_aliases={n_s + j: j for j in range(n_b)},
        scratch_shapes=[pltpu.SemaphoreType.DMA((n_copies,)), pltpu.SemaphoreType.DMA((n_copies,))],
    )(*comm.srcs.values(), *comm.bufs.values())
    comm.land(res)


def _embed(meta_pad, x, name, comm=None):
    seq, d = x.shape
    t = seq + LANES

    def body(m_ref, x_ref, h_ref, hb_ref):
        first = pl.program_id(0) == 0
        v = jnp.where(first, m_ref[...], x_ref[...])
        h_ref[...] = v
        hb_ref[...] = v.astype(bf16)

    return _pallas(
        comm, body, name=name, grid=(t // LANES,),
        in_specs=[pl.BlockSpec((LANES, d), lambda i: (0, 0)),
                  pl.BlockSpec((LANES, d), lambda i: (jnp.maximum(i - 1, 0), 0))],
        out_specs=[pl.BlockSpec((LANES, d), lambda i: (i, 0)),
                   pl.BlockSpec((LANES, d), lambda i: (i, 0))],
        out_shape=[jax.ShapeDtypeStruct((t, d), f32), jax.ShapeDtypeStruct((t, d), bf16)],
        compiler_params=_params("parallel"),
    )(meta_pad, x)


def _nn_matmul(x, w, out_dtype, name, comm=None, out_scale=None):
    t, k = x.shape
    s_n, _, n = w.shape
    assert s_n == 1 or n % LANES == 0
    tm = _tile(t, ROWS_WIDE, 16)

    def body(x_ref, w_ref, o_ref):
        res = jnp.dot(x_ref[...].astype(bf16), w_ref[...], preferred_element_type=f32)
        if out_scale is not None:
            res = res * out_scale
        o_ref[...] = res.astype(o_ref.dtype)

    return _pallas(
        comm, body, name=name, grid=(s_n, t // tm),
        in_specs=[pl.BlockSpec((tm, k), lambda s, i: (i, 0)),
                  pl.BlockSpec((None, k, n), lambda s, i: (s, 0, 0))],
        out_specs=pl.BlockSpec((tm, n), lambda s, i: (i, s)),
        out_shape=jax.ShapeDtypeStruct((t, s_n * n), out_dtype),
        compiler_params=_params("parallel", "parallel"),
    )(x, w)


def _ffn_up(hb, wg, wu, layer, name, comm=None):
    t, d = hb.shape
    s_n, _, n, _ = wg.shape
    tm = _tile(t, ROWS_WIDE, 16)

    def body(x_ref, wg_ref, wu_ref, a_ref, b_ref, s_ref):
        x = x_ref[...]
        a = lax.dot_general(x, wg_ref[...], NT_DIMS, preferred_element_type=f32)
        b = lax.dot_general(x, wu_ref[...], NT_DIMS, preferred_element_type=f32)
        a_ref[...] = a.astype(bf16)
        b_ref[...] = b.astype(bf16)
        s_ref[...] = (a * jax.nn.sigmoid(a) * b).astype(bf16)

    wspec = pl.BlockSpec((None, None, n, d), lambda s, i: (s, layer, 0, 0))
    ospec = pl.BlockSpec((None, tm, n), lambda s, i: (s, i, 0))
    return _pallas(
        comm, body, name=name, grid=(s_n, t // tm),
        in_specs=[pl.BlockSpec((tm, d), lambda s, i: (i, 0)), wspec, wspec],
        out_specs=[ospec, ospec, ospec],
        out_shape=[jax.ShapeDtypeStruct((s_n, t, n), bf16)] * 3,
        compiler_params=_params("parallel", "parallel"),
    )(hb, wg, wu)


def _down_ln(x, w, layer, hprev, gain, bias, beta, name, comm=None):
    s_n, t, k = x.shape
    d = w.shape[-1]
    tm = _tile(t, ROWS_ACC, 16)

    def body(x_ref, w_ref, h_ref, g_ref, b_ref, r_out, h_out, hb_out, acc):
        s = pl.program_id(1)

        @pl.when(s == 0)
        def _():
            acc[...] = jnp.zeros_like(acc)

        acc[...] += jnp.dot(x_ref[...], w_ref[...], preferred_element_type=f32)

        @pl.when(s == s_n - 1)
        def _():
            r = ALPHA * h_ref[...] + beta * acc[...]
            mu = jnp.mean(r, axis=-1, keepdims=True)
            xc = r - mu
            var = jnp.mean(xc * xc, axis=-1, keepdims=True)
            y = xc * lax.rsqrt(var + LN_EPS) * g_ref[...] + b_ref[...]
            r_out[...] = r
            h_out[...] = y
            hb_out[...] = y.astype(bf16)

    row = pl.BlockSpec((tm, d), lambda i, s: (i, 0))
    vec = pl.BlockSpec((1, d), lambda i, s: (0, 0))
    return _pallas(
        comm, body, name=name, grid=(t // tm, s_n),
        in_specs=[pl.BlockSpec((None, tm, k), lambda i, s: (s, i, 0)),
                  pl.BlockSpec((None, None, k, d), lambda i, s: (s, layer, 0, 0)),
                  row, vec, vec],
        out_specs=[row, row, row],
        out_shape=[jax.ShapeDtypeStruct((t, d), f32), jax.ShapeDtypeStruct((t, d), f32),
                   jax.ShapeDtypeStruct((t, d), bf16)],
        scratch_shapes=[pltpu.VMEM((tm, d), f32)],
        compiler_params=_params("parallel", "arbitrary"),
    )(x, w, hprev, gain, bias)


def _conv_fwd(p, conv_w, name):
    t, d3 = p.shape
    d = d3 // 3
    tm = _tile(t, 320, 8)
    hb = tm // 8

    def body(p_ref, prev_ref, w_ref, z_ref):
        i = pl.program_id(0)
        rows = i * tm - 8 + lax.broadcasted_iota(jnp.int32, (tm + 8, 1), 0)
        cg = jnp.concatenate([prev_ref[:, d:2 * d], p_ref[:, d:2 * d]], axis=0)
        val = jnp.concatenate([prev_ref[:, 2 * d:], p_ref[:, 2 * d:]], axis=0)
        u = jnp.where(rows >= PAD, cg * val, 0.0)
        y = (w_ref[2:3, :] * u + w_ref[1:2, :] * pltpu.roll(u, 1, 0)
             + w_ref[0:1, :] * pltpu.roll(u, 2, 0))
        z_ref[...] = (p_ref[:, :d] * y[8:]).astype(bf16)

    return pl.pallas_call(
        body, name=name, grid=(t // tm,),
        in_specs=[pl.BlockSpec((tm, d3), lambda i: (i, 0)),
                  pl.BlockSpec((8, d3), lambda i: (jnp.maximum(i * hb - 1, 0), 0)),
                  pl.BlockSpec((3, d), lambda i: (0, 0))],
        out_specs=pl.BlockSpec((tm, d), lambda i: (i, 0)),
        out_shape=jax.ShapeDtypeStruct((t, d), bf16),
        compiler_params=_params("parallel"),
    )(p, p, conv_w)


def _row_scan(x, name, fbias=None, reverse=False):
    t, n = x.shape
    blk = _tile(t, 640, LANES)
    n_blk = t // blk
    gate = fbias is not None

    def body(*refs):
        if gate:
            x_ref, fb_ref, o_ref, carry = refs
        else:
            x_ref, o_ref, carry = refs
        i = pl.program_id(0)

        @pl.when(i == 0)
        def _():
            carry[...] = jnp.zeros_like(carry)

        v = x_ref[...]
        r = lax.broadcasted_iota(jnp.int32, (blk, n), 0)
        if gate:
            v = v + fb_ref[...]
            v = jnp.minimum(v, 0.0) - jnp.log1p(jnp.exp(-jnp.abs(v)))
            v = jnp.where(i * blk + r >= PAD, v, 0.0)
        sh = 1
        while sh < blk:
            if reverse:
                v = v + jnp.where(r < blk - sh, pltpu.roll(v, blk - sh, 0), 0.0)
            else:
                v = v + jnp.where(r >= sh, pltpu.roll(v, sh, 0), 0.0)
            sh *= 2
        v = v + carry[...]
        o_ref[...] = v
        carry[...] = o_ref[0:1, :] if reverse else o_ref[blk - 1:blk, :]

    order = (lambda i: (n_blk - 1 - i, 0)) if reverse else (lambda i: (i, 0))
    in_specs = [pl.BlockSpec((blk, n), order)]
    args = [x]
    if gate:
        in_specs.append(pl.BlockSpec((1, n), lambda i: (0, 0)))
        args.append(fbias)
    return pl.pallas_call(
        body, name=name, grid=(n_blk,),
        in_specs=in_specs,
        out_specs=pl.BlockSpec((blk, n), order),
        out_shape=jax.ShapeDtypeStruct((t, n), f32),
        scratch_shapes=[pltpu.VMEM((1, n), f32)],
        compiler_params=_params("arbitrary"),
    )(*args)


def _lanes(x, n):
    return jnp.concatenate([x] * (n // LANES), axis=1)


def _attn_fwd(q, k, v, cq_rep, ck_rows, name):
    t, d = q.shape
    n_heads = d // HEAD_DIM
    bk = ck_rows.shape[-1]
    bq = bk

    def lane_fold(x, op):
        out = x[:, :LANES]
        for c0 in range(LANES, bk, LANES):
            out = op(out, x[:, c0:c0 + LANES])
        return out

    def body(q_ref, k_ref, v_ref, cq_ref, ck_ref, o_ref, o32_ref, cl_ref,
             s_scr, p_scr, m_scr, l_scr, red_scr, acc_scr):
        i = pl.program_id(1)
        m_scr[...] = jnp.full_like(m_scr, NEG_INF)
        l_scr[...] = jnp.zeros_like(l_scr)
        acc_scr[...] = jnp.zeros_like(acc_scr)
        qb = q_ref[...]
        ahead = (lax.broadcasted_iota(jnp.int32, (STRIP, bk), 1)
                 - lax.broadcasted_iota(jnp.int32, (STRIP, bk), 0))

        def tile(j, diagonal):
            k0 = pl.multiple_of(j * bk, bk)
            s_scr[...] = lax.dot_general(qb, k_ref[pl.ds(k0, bk), :], NT_DIMS,
                                         preferred_element_type=f32)
            ck = ck_ref[j]
            for r in range(0, bq, STRIP):
                rows = slice(r, r + STRIP)
                s = s_scr[rows, :] + _lanes(cq_ref[rows, :], bk) - ck
                if diagonal:
                    s = jnp.where(ahead <= r, s, NEG_INF)
                s_scr[rows, :] = s
                red_scr[rows, :] = lane_fold(s, jnp.maximum)
            m_old = m_scr[...]
            m_new = jnp.maximum(m_old, jnp.broadcast_to(
                jnp.max(red_scr[...], axis=1, keepdims=True), (bq, LANES)))
            a = jnp.exp(m_old - m_new)
            m_scr[...] = m_new
            for r in range(0, bq, STRIP):
                rows = slice(r, r + STRIP)
                pr = jnp.exp(s_scr[rows, :] - _lanes(m_scr[rows, :], bk))
                red_scr[rows, :] = lane_fold(pr, jnp.add)
                p_scr[rows, :] = pr.astype(bf16)
            l_scr[...] = a * l_scr[...] + jnp.broadcast_to(
                jnp.sum(red_scr[...], axis=1, keepdims=True), (bq, LANES))
            acc_scr[...] = a * acc_scr[...] + jnp.dot(
                p_scr[...], v_ref[pl.ds(k0, bk), :], preferred_element_type=f32)

        def full_tile(j, carry):
            tile(j, False)
            return carry

        lax.fori_loop(0, i, full_tile, 0)
        tile(i, True)
        out = acc_scr[...] / l_scr[...]
        o_ref[...] = out.astype(bf16)
        o32_ref[...] = out
        red_scr[...] = cq_ref[...] - (m_scr[...] + jnp.log(l_scr[...]))
        for c0 in range(0, bq, LANES):
            cl_ref[:, c0:c0 + LANES] = red_scr[c0:c0 + LANES, :].T[0:1, :]

    qblk = pl.BlockSpec((bq, HEAD_DIM), lambda h, i: (i, h))
    head_rows = pl.BlockSpec((t, HEAD_DIM), lambda h, i: (0, h))
    rep = pl.BlockSpec((None, bq, LANES), lambda h, i: (h, i, 0))
    col = pltpu.VMEM((bq, LANES), f32)
    return pl.pallas_call(
        body, name=name, grid=(n_heads, t // bq),
        in_specs=[qblk, head_rows, head_rows, rep,
                  pl.BlockSpec((None, t // bk, 1, bk), lambda h, i: (h, 0, 0, 0))],
        out_specs=[qblk, qblk, pl.BlockSpec((None, None, 1, bq), lambda h, i: (h, i, 0, 0))],
        out_shape=[jax.ShapeDtypeStruct((t, d), bf16), jax.ShapeDtypeStruct((t, d), f32),
                   jax.ShapeDtypeStruct((n_heads, t // bq, 1, bq), f32)],
        scratch_shapes=[pltpu.VMEM((bq, bk), f32), pltpu.VMEM((bq, bk), bf16), col, col, col,
                        pltpu.VMEM((bq, HEAD_DIM), f32)],
        compiler_params=_params("parallel", "parallel"),
    )(q, k, v, cq_rep, ck_rows)


def _loss_head(h, target, r, gain, name):
    t, d = h.shape

    def body(h_ref, t_ref, r_ref, g_ref, dr_ref, drb_ref, dg_ref, db_ref, loss_ref):
        i = pl.program_id(0)

        @pl.when(i == 0)
        def _():
            loss_ref[...] = jnp.zeros_like(loss_ref)
            dg_ref[...] = jnp.zeros_like(dg_ref)
            db_ref[...] = jnp.zeros_like(db_ref)

        diff = jnp.where(i >= 1, h_ref[...] - t_ref[...], 0.0)
        loss_ref[...] += jnp.sum(diff * diff)
        dr, dg, db = _ln_bwd_rows(diff * (1.0 / d), r_ref[...], g_ref[...])
        dr_ref[...] = dr
        drb_ref[...] = dr.astype(bf16)
        dg_ref[...] += dg
        db_ref[...] += db

    row = pl.BlockSpec((LANES, d), lambda i: (i, 0))
    vec = pl.BlockSpec((1, d), lambda i: (0, 0))
    return pl.pallas_call(
        body, name=name, grid=(t // LANES,),
        in_specs=[row, pl.BlockSpec((LANES, d), lambda i: (jnp.maximum(i - 1, 0), 0)), row, vec],
        out_specs=[row, row, vec, vec, pl.BlockSpec((1, LANES), lambda i: (0, 0))],
        out_shape=[jax.ShapeDtypeStruct((t, d), f32), jax.ShapeDtypeStruct((t, d), bf16),
                   jax.ShapeDtypeStruct((1, d), f32), jax.ShapeDtypeStruct((1, d), f32),
                   jax.ShapeDtypeStruct((1, LANES), f32)],
        compiler_params=_params("arbitrary"),
    )(h, target, r, gain)


def _ln_bwd_rows(dy, rr, gain):
    mu = jnp.mean(rr, axis=-1, keepdims=True)
    xc = rr - mu
    var = jnp.mean(xc * xc, axis=-1, keepdims=True)
    rstd = lax.rsqrt(var + LN_EPS)
    xhat = xc * rstd
    dxh = dy * gain
    m1 = jnp.mean(dxh, axis=-1, keepdims=True)
    m2 = jnp.mean(dxh * xhat, axis=-1, keepdims=True)
    dr = rstd * (dxh - m1 - xhat * m2)
    return dr, jnp.sum(dy * xhat, axis=0, keepdims=True), jnp.sum(dy, axis=0, keepdims=True)


def _ffn_bwd_act(drb, wd, layer, a, b, name, comm=None):
    t, d = drb.shape
    s_n, _, n = a.shape
    tm = _tile(t, ROWS_WIDE, 16)

    def body(dr_ref, w_ref, a_ref, b_ref, da_ref, db_ref):
        ds = 0.5 * lax.dot_general(dr_ref[...], w_ref[...], NT_DIMS, preferred_element_type=f32)
        da_ref[...], db_ref[...] = _swiglu_bwd(ds, a_ref, b_ref)

    act = pl.BlockSpec((None, tm, n), lambda s, i: (s, i, 0))
    return _pallas(
        comm, body, name=name, grid=(s_n, t // tm),
        in_specs=[pl.BlockSpec((tm, d), lambda s, i: (i, 0)),
                  pl.BlockSpec((None, None, n, d), lambda s, i: (s, layer, 0, 0)), act, act],
        out_specs=[act, act],
        out_shape=[jax.ShapeDtypeStruct((s_n, t, n), bf16), jax.ShapeDtypeStruct((s_n, t, n), bf16)],
        compiler_params=_params("parallel", "parallel"),
    )(drb, wd, a, b)


def _swiglu_bwd(ds, a_ref, b_ref):
    av = a_ref[...].astype(f32)
    sig = jax.nn.sigmoid(av)
    da = ds * b_ref[...].astype(f32) * (sig * (1.0 + av * (1.0 - sig)))
    return da.astype(bf16), (ds * (av * sig)).astype(bf16)


def _act_spec(mode, tt, k, t_first):
    def fix(fn):
        return (lambda i, s: fn(s, i)) if t_first else fn
    if mode == "shared":
        return pl.BlockSpec((tt, k), fix(lambda s, i: (i, 0)))
    if mode == "cols":
        return pl.BlockSpec((tt, k), fix(lambda s, i: (i, s)))
    assert mode == "stack"
    return pl.BlockSpec((None, tt, k), fix(lambda s, i: (s, i, 0)))


def _act_width(arr, mode, s_n):
    return arr.shape[-1] // s_n if mode == "cols" else arr.shape[-1]


def _tn_matmul(pairs, s_n, scale, name, comm=None):
    t = pairs[0][0].shape[-2]
    tt = _tile(t, 2080, 16)
    n_t = t // tt
    arrays, specs, where = [], [], []
    for x, xmode, y, ymode in pairs:
        pos = []
        for arr, mode in ((x, xmode), (y, ymode)):
            hit = [j for j, a in enumerate(arrays) if a is arr]
            if not hit:
                arrays.append(arr)
                specs.append(_act_spec(mode, tt, _act_width(arr, mode, s_n), False))
                hit = [len(arrays) - 1]
            pos.append(hit[0])
        where.append(pos)
    widths = [(_act_width(x, xm, s_n), _act_width(y, ym, s_n)) for x, xm, y, ym in pairs]
    n_a = len(arrays)

    def body(*refs):
        i = pl.program_id(1)
        for (px, py), o_ref in zip(where, refs[n_a:]):
            part = lax.dot_general(refs[px][...].astype(bf16), refs[py][...].astype(bf16), TN_DIMS,
                                   preferred_element_type=f32)

            @pl.when(i == 0)
            def _():
                o_ref[...] = part

            @pl.when(i > 0)
            def _():
                o_ref[...] += part

            if scale != 1.0:
                @pl.when(i == n_t - 1)
                def _():
                    o_ref[...] = o_ref[...] * scale

    return _pallas(
        comm, body, name=name, grid=(s_n, n_t),
        in_specs=specs,
        out_specs=[pl.BlockSpec((None, kx, ky), lambda s, i: (s, 0, 0)) for kx, ky in widths],
        out_shape=[jax.ShapeDtypeStruct((s_n, kx, ky), f32) for kx, ky in widths],
        compiler_params=_params("parallel", "arbitrary"),
    )(*arrays)


def _nt_sum(pairs, base, base_scale, s_n, out_dtype, name, comm=None, ln=None):
    t = pairs[0][0].shape[-2]
    pairs = [(*pr, False)[:5] for pr in pairs]
    d = pairs[0][2].shape[-1] if pairs[0][4] else pairs[0][2].shape[-2]
    tm = _tile(t, ROWS_ACC if ln is None else ROWS_ACC_LN, 16)
    n_p = len(pairs)
    has_base = base is not None
    flipped = [pr[4] for pr in pairs]
    n_out = 1 if ln is None else 4

    def body(*refs):
        dy_refs = refs[0:2 * n_p:2]
        w_refs = refs[1:2 * n_p:2]
        rest = refs[2 * n_p:]
        base_ref = rest[0] if has_base else None
        o_refs, acc = rest[-1 - n_out:-1], rest[-1]
        s = pl.program_id(1)

        if ln is not None:
            @pl.when(jnp.logical_and(pl.program_id(0) == 0, s == 0))
            def _():
                o_refs[2][...] = jnp.zeros_like(o_refs[2])
                o_refs[3][...] = jnp.zeros_like(o_refs[3])

        @pl.when(s == 0)
        def _():
            acc[...] = jnp.zeros_like(acc)

        tot = None
        for dy_ref, w_ref, flip in zip(dy_refs, w_refs, flipped):
            dyv = dy_ref[...].astype(bf16)
            if flip:
                part = jnp.dot(dyv, w_ref[...], preferred_element_type=f32)
            else:
                part = lax.dot_general(dyv, w_ref[...], NT_DIMS, preferred_element_type=f32)
            tot = part if tot is None else tot + part
        acc[...] += tot

        @pl.when(s == s_n - 1)
        def _():
            res = acc[...]
            if has_base:
                res = base_scale * base_ref[...] + res
            if ln is None:
                o_refs[0][...] = res.astype(o_refs[0].dtype)
            else:
                r_ref, g_ref = rest[-7], rest[-6]
                dr, dg, db = _ln_bwd_rows(res, r_ref[...], g_ref[...])
                o_refs[0][...] = dr
                o_refs[1][...] = dr.astype(bf16)
                o_refs[2][...] += dg
                o_refs[3][...] += db

    in_specs, args = [], []
    for dy, mode, w, layer, flip in pairs:
        k = _act_width(dy, mode, s_n)
        in_specs.append(_act_spec(mode, tm, k, True))
        wshape = (k, d) if flip else (d, k)
        if layer is None:
            in_specs.append(pl.BlockSpec((None, *wshape), lambda i, s: (s, 0, 0)))
        else:
            in_specs.append(pl.BlockSpec((None, None, *wshape),
                                         functools.partial(lambda i, s, l: (s, l, 0, 0), l=layer)))
        args += [dy, w]
    row = pl.BlockSpec((tm, d), lambda i, s: (i, 0))
    vec = pl.BlockSpec((1, d), lambda i, s: (0, 0))
    if has_base:
        in_specs.append(row)
        args.append(base)
    if ln is None:
        out_specs = row
        out_shape = jax.ShapeDtypeStruct((t, d), out_dtype)
    else:
        in_specs += [row, vec]
        args += list(ln)
        out_specs = [row, row, vec, vec]
        out_shape = [jax.ShapeDtypeStruct((t, d), f32), jax.ShapeDtypeStruct((t, d), bf16),
                     jax.ShapeDtypeStruct((1, d), f32), jax.ShapeDtypeStruct((1, d), f32)]
    return _pallas(
        comm, body, name=name, grid=(t // tm, s_n),
        in_specs=in_specs, out_specs=out_specs, out_shape=out_shape,
        scratch_shapes=[pltpu.VMEM((tm, d), f32)],
        compiler_params=_params("parallel" if ln is None else "arbitrary", "arbitrary"),
    )(*args)


def _conv_bwd(dz, p, conv_w, name, comm=None):
    t, d3 = p.shape
    d = d3 // 3
    tm = _tile(t, 320, 8)
    hb = tm // 8
    last8 = t // 8 - 1
    n_ext = tm + 8

    def body(dz_ref, dzn_ref, p_ref, pp_ref, pn_ref, w_ref, dp_ref, dw_ref):
        i = pl.program_id(0)

        @pl.when(i == 0)
        def _():
            dw_ref[...] = jnp.zeros_like(dw_ref)

        w0, w1, w2 = w_ref[0:1, :], w_ref[1:2, :], w_ref[2:3, :]
        rows_u = i * tm - 8 + lax.broadcasted_iota(jnp.int32, (n_ext, 1), 0)
        cg = jnp.concatenate([pp_ref[:, d:2 * d], p_ref[:, d:2 * d]], axis=0)
        val = jnp.concatenate([pp_ref[:, 2 * d:], p_ref[:, 2 * d:]], axis=0)
        u = jnp.where(rows_u >= PAD, cg * val, 0.0)
        u1 = pltpu.roll(u, 1, 0)
        u2 = pltpu.roll(u, 2, 0)
        y = (w2 * u + w1 * u1 + w0 * u2)[8:]
        dzv = dz_ref[...]
        bg = p_ref[:, :d]
        rows_n = (i + 1) * tm + lax.broadcasted_iota(jnp.int32, (8, 1), 0)
        dy_main = dzv * bg
        dy_next = jnp.where(rows_n < t, dzn_ref[...] * pn_ref[:, :d], 0.0)
        dye = jnp.concatenate([dy_main, dy_next], axis=0)
        du = (w2 * dye + w1 * pltpu.roll(dye, n_ext - 1, 0)
              + w0 * pltpu.roll(dye, n_ext - 2, 0))[:tm]
        du = jnp.where(rows_u[8:] >= PAD, du, 0.0)
        dp_ref[:, :d] = (dzv * y).astype(bf16)
        dp_ref[:, d:2 * d] = (du * val[8:]).astype(bf16)
        dp_ref[:, 2 * d:] = (du * cg[8:]).astype(bf16)
        dw_ref[0:1, :] += jnp.sum(dy_main * u2[8:], axis=0, keepdims=True)
        dw_ref[1:2, :] += jnp.sum(dy_main * u1[8:], axis=0, keepdims=True)
        dw_ref[2:3, :] += jnp.sum(dy_main * u[8:], axis=0, keepdims=True)

    nxt = lambda i: (jnp.minimum((i + 1) * hb, last8), 0)
    return _pallas(
        comm, body, name=name, grid=(t // tm,),
        in_specs=[pl.BlockSpec((tm, d), lambda i: (i, 0)),
                  pl.BlockSpec((8, d), nxt),
                  pl.BlockSpec((tm, d3), lambda i: (i, 0)),
                  pl.BlockSpec((8, d3), lambda i: (jnp.maximum(i * hb - 1, 0), 0)),
                  pl.BlockSpec((8, d3), nxt),
                  pl.BlockSpec((3, d), lambda i: (0, 0))],
        out_specs=[pl.BlockSpec((tm, d3), lambda i: (i, 0)),
                   pl.BlockSpec((3, d), lambda i: (0, 0))],
        out_shape=[jax.ShapeDtypeStruct((t, d3), bf16), jax.ShapeDtypeStruct((3, d), f32)],
        compiler_params=_params("arbitrary"),
    )(dz, dz, p, p, p, conv_w)


def _attn_do(drb, w_o, o, bq, name, comm=None):
    t, d = drb.shape
    n_heads = d // HEAD_DIM

    def body(dr_ref, w_ref, o_ref, do_ref, delta_ref):
        do = lax.dot_general(dr_ref[...], w_ref[...], NT_DIMS, preferred_element_type=f32).astype(bf16)
        do_ref[...] = do
        prod = o_ref[...] * do.astype(f32)
        for h in range(n_heads):
            for c0 in range(0, bq, LANES):
                blk = prod[c0:c0 + LANES, h * HEAD_DIM:(h + 1) * HEAD_DIM]
                delta_ref[h, :, c0:c0 + LANES] = jnp.sum(blk.T, axis=0, keepdims=True)

    row = pl.BlockSpec((bq, d), lambda i: (i, 0))
    return _pallas(
        comm, body, name=name, grid=(t // bq,),
        in_specs=[row, pl.BlockSpec((d, d), lambda i: (0, 0)), row],
        out_specs=[row, pl.BlockSpec((n_heads, None, 1, bq), lambda i: (0, i, 0, 0))],
        out_shape=[jax.ShapeDtypeStruct((t, d), bf16),
                   jax.ShapeDtypeStruct((n_heads, t // bq, 1, bq), f32)],
        compiler_params=_params("parallel"),
    )(drb, w_o, o)


def _attn_bwd(q, k, v, kt, do, ckey, cl_rows, delta_rows, name, comm=None):
    t, d = q.shape
    n_heads = d // HEAD_DIM
    bk = kt.shape[-1]
    bq = bk
    n_kv = t // bk
    n_q = t // bq
    scale = 1.0 / math.sqrt(HEAD_DIM)

    def body(q_ref, do_ref, cl_ref, dl_ref, k_ref, v_ref, kt_ref, ck_ref,
             dq_ref, dcq_ref, dk_ref, dv_ref, dck_ref,
             st_scr, dp_scr, p_scr, ds_scr, dqt, dk_acc, dv_acc, dck_acc):
        j = pl.program_id(1)

        @pl.when(j == 0)
        def _():
            dqt[...] = jnp.zeros_like(dqt)
            dcq_ref[...] = jnp.zeros_like(dcq_ref)

        dk_acc[...] = jnp.zeros_like(dk_acc)
        dv_acc[...] = jnp.zeros_like(dv_acc)
        dck_acc[...] = jnp.zeros_like(dck_acc)
        kb = k_ref[...]
        vb = v_ref[...]
        behind = (lax.broadcasted_iota(jnp.int32, (STRIP, bq), 1)
                  - lax.broadcasted_iota(jnp.int32, (STRIP, bq), 0))

        def tile(i, diagonal):
            r0 = pl.multiple_of(i * bq, bq)
            qi = q_ref[pl.ds(r0, bq), :]
            doi = do_ref[pl.ds(r0, bq), :]
            st_scr[...] = lax.dot_general(kb, qi, NT_DIMS, preferred_element_type=f32)
            dp_scr[...] = lax.dot_general(vb, doi, NT_DIMS, preferred_element_type=f32)
            cl = cl_ref[i]
            dl = dl_ref[i]
            over_keys = jnp.zeros((STRIP, bq), f32)
            for r in range(0, bk, STRIP):
                keys = slice(r, r + STRIP)
                st = st_scr[keys, :] + cl - _lanes(ck_ref[keys, :], bq)
                if diagonal:
                    st = jnp.where(behind >= r, st, NEG_INF)
                pr = jnp.exp(st)
                ds = pr * (dp_scr[keys, :] - dl)
                over_keys = over_keys + ds
                dck_acc[keys, :] -= jnp.sum(ds, axis=1, keepdims=True)
                p_scr[keys, :] = pr.astype(bf16)
                ds_scr[keys, :] = ds.astype(bf16)
            dcq_ref[i] += jnp.sum(over_keys, axis=0, keepdims=True)
            dv_acc[...] += jnp.dot(p_scr[...], doi, preferred_element_type=f32)
            dk_acc[...] += jnp.dot(ds_scr[...], qi, preferred_element_type=f32)
            dqt[i] += jnp.dot(kt_ref[...], ds_scr[...], preferred_element_type=f32)

        def full_tile(i, carry):
            tile(i, False)
            return carry

        tile(j, True)
        lax.fori_loop(j + 1, n_q, full_tile, 0)
        dk_ref[...] = dk_acc[...].astype(bf16)
        dv_ref[...] = dv_acc[...].astype(bf16)
        for c0 in range(0, bk, LANES):
            keys = slice(c0, c0 + LANES)
            dck_ref[:, keys] = jnp.broadcast_to(dck_acc[keys, :], (LANES, LANES)).T[0:1, :]

        @pl.when(j == n_kv - 1)
        def _():
            def emit(i, carry):
                r0 = pl.multiple_of(i * bq, bq)
                dq_ref[pl.ds(r0, bq), :] = (dqt[i].T * scale).astype(bf16)
                return carry
            lax.fori_loop(0, n_q, emit, 0)

    head_rows = pl.BlockSpec((t, HEAD_DIM), lambda h, j: (0, h))
    head_stat = pl.BlockSpec((None, n_q, 1, bq), lambda h, j: (h, 0, 0, 0))
    kblk = pl.BlockSpec((bk, HEAD_DIM), lambda h, j: (j, h))
    return _pallas(
        comm, body, name=name, grid=(n_heads, n_kv),
        in_specs=[head_rows, head_rows, head_stat, head_stat, kblk, kblk,
                  pl.BlockSpec((None, None, HEAD_DIM, bk), lambda h, j: (h, j, 0, 0)),
                  pl.BlockSpec((None, bk, LANES), lambda h, j: (h, j, 0))],
        out_specs=[head_rows, head_stat, kblk, kblk,
                   pl.BlockSpec((None, None, 1, bk), lambda h, j: (h, j, 0, 0))],
        out_shape=[jax.ShapeDtypeStruct((t, d), bf16),
                   jax.ShapeDtypeStruct((n_heads, n_q, 1, bq), f32),
                   jax.ShapeDtypeStruct((t, d), bf16), jax.ShapeDtypeStruct((t, d), bf16),
                   jax.ShapeDtypeStruct((n_heads, n_kv, 1, bk), f32)],
        scratch_shapes=[pltpu.VMEM((bk, bq), f32), pltpu.VMEM((bk, bq), f32),
                        pltpu.VMEM((bk, bq), bf16), pltpu.VMEM((bk, bq), bf16),
                        pltpu.VMEM((n_q, HEAD_DIM, bq), f32),
                        pltpu.VMEM((bk, HEAD_DIM), f32), pltpu.VMEM((bk, HEAD_DIM), f32),
                        pltpu.VMEM((bk, 1), f32)],
        compiler_params=_params("parallel", "arbitrary"),
    )(q, do, cl_rows, delta_rows, k, v, kt, ckey)


def _fgate_bwd(dlogf, flog, fbias, name):
    t, n = flog.shape
    rows = _tile(t, 640, LANES)

    def body(dl_ref, fl_ref, fb_ref, o_ref, sum_ref):
        i = pl.program_id(0)

        @pl.when(i == 0)
        def _():
            sum_ref[...] = jnp.zeros_like(sum_ref)

        r = i * rows + lax.broadcasted_iota(jnp.int32, (rows, n), 0)
        g = dl_ref[...] * jax.nn.sigmoid(-(fl_ref[...] + fb_ref[...]))
        g = jnp.where(r >= PAD, g, 0.0)
        o_ref[...] = g
        sum_ref[...] += jnp.sum(g, axis=0, keepdims=True)

    blk = pl.BlockSpec((rows, n), lambda i: (i, 0))
    vec = pl.BlockSpec((1, n), lambda i: (0, 0))
    return pl.pallas_call(
        body, name=name, grid=(t // rows,),
        in_specs=[blk, blk, vec], out_specs=[blk, vec],
        out_shape=[jax.ShapeDtypeStruct((t, n), f32), jax.ShapeDtypeStruct((1, n), f32)],
        compiler_params=_params("arbitrary"),
    )(dlogf, flog, fbias)


def _place_shard(ws, idx, name):
    n = len(ws)
    n_l, r, c_n = ws[0].shape
    out_dtype = bf16 if r * c_n > 2 ** 16 else ws[0].dtype
    tr = _tile(r, 512, 16) if r % 16 == 0 else r

    def body(idx_ref, *refs):
        for w_ref, o_ref in zip(refs[:n], refs[n:]):
            o_ref[...] = w_ref[...].astype(out_dtype)

    grid_spec = pltpu.PrefetchScalarGridSpec(
        num_scalar_prefetch=1, grid=(n_l, r // tr),
        in_specs=[pl.BlockSpec((None, tr, c_n), lambda l, i, idx: (l, i, 0))] * n,
        out_specs=[pl.BlockSpec((None, None, tr, c_n), lambda l, i, idx: (idx[0], l, i, 0))] * n)
    return pl.pallas_call(
        body, name=name, grid_spec=grid_spec,
        out_shape=[jax.ShapeDtypeStruct((N_CHIPS, n_l, r, c_n), out_dtype)] * n,
        compiler_params=_params("parallel", "parallel"),
    )(idx, *ws)


def _plan_gather_ici(items):
    def plan(srcs, bufs, news, p):
        out = []
        for name, layer, r2 in items:
            mine = bufs[name].at[p.me, layer, pl.ds(p.c * r2, r2)]
            out += [(mine, mine, (*chip, p.c)) for chip in p.chips]
        return out
    return plan, 3 * len(items)


def _plan_gather_d2d(items):
    def plan(srcs, bufs, news, p):
        out = []
        for name, layer, r2 in items:
            for px, py in p.chips:
                landed = bufs[name].at[2 * px + py, layer, pl.ds(p.c * r2, r2)]
                out.append((landed, landed, p.sib))
        return out
    return plan, 3 * len(items)


def _plan_pair_exchange(names, r2s):
    def plan(srcs, bufs, news, p):
        out = []
        for name, r2 in zip(names, r2s):
            for k, slot in enumerate(p.slots):
                out.append((srcs["G_" + name].at[slot, pl.ds((1 - p.c) * r2, r2)],
                            news["PAIR_" + name].at[k], p.sib))
        return out
    return plan, 4 * len(names)


def _plan_chip_exchange(names):
    def plan(srcs, bufs, news, p):
        out = []
        for name in names:
            for k, chip in enumerate(p.chips):
                out.append((srcs["SEND_" + name].at[k], news["RECV_" + name].at[k], (*chip, p.c)))
        return out
    return plan, 3 * len(names)


def _plan_pair_share(items):
    def plan(srcs, bufs, news, p):
        out = []
        for name, layer, r2 in items:
            mine = bufs["RED_" + name].at[layer, pl.ds(p.c * r2, r2)]
            out.append((mine, mine, p.sib))
        return out
    return plan, len(items)


def _rs_prepare(gs, pairs, idx, name):
    n = len(gs)
    _, r2, c_n = pairs[0].shape
    tr = _tile(r2, 256, 8)
    nb = r2 // tr

    def body(idx_ref, *refs):
        for g_ref, p_ref, o_ref in zip(refs[:n], refs[n:2 * n], refs[2 * n:]):
            o_ref[...] = (g_ref[...] + p_ref[...]).astype(bf16)

    g_spec = pl.BlockSpec((None, tr, c_n), lambda k, i, idx: (idx[k + 1], idx[4] * nb + i, 0))
    p_spec = pl.BlockSpec((None, tr, c_n), lambda k, i, idx: (k + 1, i, 0))
    grid_spec = pltpu.PrefetchScalarGridSpec(
        num_scalar_prefetch=1, grid=(3, nb), in_specs=[g_spec] * n + [p_spec] * n,
        out_specs=[pl.BlockSpec((None, tr, c_n), lambda k, i, idx: (k, i, 0))] * n)
    return pl.pallas_call(
        body, name=name, grid_spec=grid_spec,
        out_shape=[jax.ShapeDtypeStruct((3, r2, c_n), bf16)] * n,
        compiler_params=_params("parallel", "parallel"),
    )(idx, *gs, *pairs)


def _rs_finish(gs, pairs, recvs, idx, layer, n_layers, intos, name):
    n = len(gs)
    _, r2, c_n = pairs[0].shape
    tr = _tile(r2, 256, 8)
    nb = r2 // tr

    def body(idx_ref, *refs):
        for j in range(n):
            g_ref, p_ref, r0_ref, r1_ref, r2_ref = refs[5 * j:5 * j + 5]
            acc = g_ref[...] + p_ref[...]
            acc = acc + r0_ref[...].astype(f32)
            acc = acc + r1_ref[...].astype(f32)
            acc = acc + r2_ref[...].astype(f32)
            refs[len(refs) - n + j][...] = acc

    def rspec(k):
        return pl.BlockSpec((None, tr, c_n), functools.partial(lambda i, idx, kk: (kk, i, 0), kk=k))

    item_specs = [pl.BlockSpec((None, tr, c_n), lambda i, idx: (idx[0], idx[4] * nb + i, 0)),
                  rspec(0), rspec(0), rspec(1), rspec(2)]
    in_specs = item_specs * n
    args = [idx]
    for g, pair, recv in zip(gs, pairs, recvs):
        args += [g, pair, recv, recv, recv]
    aliases = {}
    if intos is not None:
        in_specs = in_specs + [pl.BlockSpec(memory_space=pl.ANY)] * n
        args += list(intos)
        aliases = {1 + 5 * n + j: j for j in range(n)}
    grid_spec = pltpu.PrefetchScalarGridSpec(
        num_scalar_prefetch=1, grid=(nb,), in_specs=in_specs,
        out_specs=[pl.BlockSpec((None, tr, c_n), lambda i, idx: (layer, idx[4] * nb + i, 0))] * n)
    return pl.pallas_call(
        body, name=name, grid_spec=grid_spec,
        out_shape=[jax.ShapeDtypeStruct((n_layers, 2 * r2, c_n), f32)] * n,
        input_output_aliases=aliases,
        compiler_params=_params("parallel"),
    )(*args)


def _adamw(ws, gs, ms, vs, layer, intos, name):
    n = len(ws)
    n_l, r, c_n = ws[0].shape
    tr = _tile(r, 256, 8)

    def body(*refs):
        for j in range(n):
            w_ref, g_ref, m_ref, v_ref = refs[4 * j:4 * j + 4]
            o0 = len(refs) - 4 * n + 4 * j
            d_ref, mo_ref, vo_ref, go_ref = refs[o0:o0 + 4]
            gv = g_ref[...]
            go_ref[...] = gv
            mn = ADAM_B1 * m_ref[...] + (1.0 - ADAM_B1) * gv
            vn = ADAM_B2 * v_ref[...] + (1.0 - ADAM_B2) * (gv * gv)
            m_hat = mn / (1.0 - ADAM_B1 ** ADAM_STEP)
            v_hat = vn / (1.0 - ADAM_B2 ** ADAM_STEP)
            d_ref[...] = -ADAM_LR * (m_hat / (jnp.sqrt(v_hat) + ADAM_EPS) + ADAM_WD * w_ref[...])
            mo_ref[...] = mn
            vo_ref[...] = vn

    blk = pl.BlockSpec((None, tr, c_n), lambda i: (layer, i, 0))
    shp = jax.ShapeDtypeStruct((n_l, r, c_n), f32)
    in_specs = [blk] * (4 * n)
    args = []
    for w, g, m, v in zip(ws, gs, ms, vs):
        args += [w, g, m, v]
    aliases = {}
    if intos is not None:
        in_specs = in_specs + [pl.BlockSpec(memory_space=pl.ANY)] * (4 * n)
        for into in intos:
            args += list(into)
        aliases = {4 * n + j: j for j in range(4 * n)}
    res = pl.pallas_call(
        body, name=name, grid=(r // tr,),
        in_specs=in_specs, out_specs=[blk] * (4 * n), out_shape=[shp] * (4 * n),
        input_output_aliases=aliases,
        compiler_params=_params("parallel"),
    )(*args)
    return [tuple(res[4 * j:4 * j + 4]) for j in range(n)]


def kernel(x, meta, ffn1_wg, ffn1_wu, ffn1_wd, ffn2_wg, ffn2_wu, ffn2_wd, ln_gain, ln_bias, conv_w_in, conv_w, conv_w_out, kv_w, f_bias, attn_w_q, attn_w_o, loss_target, m_meta, m_ffn1_wg, m_ffn1_wu, m_ffn1_wd, m_ffn2_wg, m_ffn2_wu, m_ffn2_wd, m_ln_gain, m_ln_bias, m_conv_w_in, m_conv_w, m_conv_w_out, m_kv_w, m_f_bias, m_attn_w_q, m_attn_w_o, v_meta, v_ffn1_wg, v_ffn1_wu, v_ffn1_wd, v_ffn2_wg, v_ffn2_wu, v_ffn2_wd, v_ln_gain, v_ln_bias, v_conv_w_in, v_conv_w, v_conv_w_out, v_kv_w, v_f_bias, v_attn_w_q, v_attn_w_o):
    seq, d = x.shape[1], x.shape[2]
    t = PAD + N_META + seq
    n_heads = d // HEAD_DIM
    dq = d // N_CHIPS
    n_kv = kv_w.shape[1]
    x2 = x[0]
    target = loss_target[0]

    def rows8(a):
        return jnp.pad(a, ((0, 8 - a.shape[0]), (0, 0)))

    def small_pack(mt, g, b, cw, fb):
        fb_row = jnp.pad(fb, (0, mt.shape[1] - n_heads))[None]
        return jnp.concatenate([mt, rows8(g.reshape(6, -1)), rows8(b.reshape(6, -1)),
                                rows8(cw.reshape(3, -1)), rows8(fb_row)], axis=0)

    w_small = small_pack(meta, ln_gain, ln_bias, conv_w, f_bias)

    cx, cy, c = lax.axis_index("x"), lax.axis_index("y"), lax.axis_index("c")
    idx = jnp.stack([2 * cx + cy, 2 * (1 - cx) + cy, 2 * cx + (1 - cy), 2 * (1 - cx) + (1 - cy), c]
                    ).astype(jnp.int32)

    def tr(a):
        return a.transpose(0, 2, 1)

    w3 = {"wg1": tr(ffn1_wg), "wu1": tr(ffn1_wu), "wd1": ffn1_wd, "wg2": tr(ffn2_wg),
          "wu2": tr(ffn2_wu), "wd2": ffn2_wd, "win": conv_w_in, "wout": conv_w_out, "kv": kv_w[None],
          "wq": attn_w_q, "wo": attn_w_o, "small": w_small[None]}
    transposed = ("wg1", "wu1", "wg2", "wu2")
    buf = {}
    for group in (("wg1", "wu1"), ("small",), ("wd1", "wg2", "wu2", "wd2"), ("win",), ("kv",),
                  ("wout", "wq", "wo")):
        buf.update(zip(group, _place_shard([w3[n] for n in group], idx, "place_shard")))

    def split(item):
        name, layer = item.split(".")
        return name, int(layer)

    def gather_stage(planner, items):
        triples = [(n, l, buf[n].shape[2] // 2) for n, l in map(split, items)]
        plan, n_copies = planner(triples)
        return dict(plan=plan, n=n_copies, bufs={n: buf[n] for n, _, _ in triples})

    def ici(items):
        return gather_stage(_plan_gather_ici, items)

    def d2d(items):
        return gather_stage(_plan_gather_d2d, items)

    def pair_exchange(items):
        r2s = [buf["G_" + it].shape[1] // 2 for it in items]
        plan, n_copies = _plan_pair_exchange(items, r2s)
        news = {"PAIR_" + it: jax.ShapeDtypeStruct((N_CHIPS, r2, buf["G_" + it].shape[2]), f32)
                for it, r2 in zip(items, r2s)}
        return dict(plan=plan, n=n_copies, srcs={"G_" + it: buf["G_" + it] for it in items}, news=news)

    def chip_exchange(items):
        plan, n_copies = _plan_chip_exchange(items)
        srcs = {"SEND_" + it: buf["SEND_" + it] for it in items}
        news = {"RECV_" + it: jax.ShapeDtypeStruct(s.shape, s.dtype)
                for it, s in ((it, buf["SEND_" + it]) for it in items)}
        return dict(plan=plan, n=n_copies, srcs=srcs, news=news)

    def pair_share(items):
        triples = [(n, l, buf["RED_" + n].shape[1] // 2) for n, l in map(split, items)]
        plan, n_copies = _plan_pair_share(triples)
        return dict(plan=plan, n=n_copies, bufs={"RED_" + n: buf["RED_" + n] for n, _, _ in triples})

    def run(fn, *args, stages=(), name=None, **kw):
        comm = _Copies()
        for st in (stages() if callable(stages) else stages):
            comm.add(st["plan"], st["n"], srcs=st.get("srcs"), bufs=st.get("bufs"), news=st.get("news"))
        out = _copy_call(comm, name) if fn is None else fn(*args, comm=comm, **kw)
        buf.update(comm.out_bufs)
        buf.update(comm.out_news)
        return out

    first = ["wg1.0", "wu1.0", "small.0"]
    run(None, stages=[ici(first)], name="gather_first_ici")
    run(None, stages=[d2d(first)], name="gather_first_d2d")
    small = buf["small"].reshape(N_CHIPS, SMALL_ROWS, dq).transpose(1, 0, 2).reshape(SMALL_ROWS, d)
    meta_full = small[:N_META]
    gains = small[16:22].reshape(DEPTH, 3, 1, d)
    biases = small[24:30].reshape(DEPTH, 3, 1, d)
    conv_w_full = small[32:35]
    fb_pad = jnp.pad(f_bias, (0, LANES - n_heads))[None]
    down1 = ["wd1.0"]
    ffn2_l0 = ["win.0", "wout.0", "wg2.0", "wu2.0", "wd2.0", "kv.0"]
    attn_ffn2_l1 = ["wq.0", "wo.0", "wg2.1", "wu2.1", "wd2.1"]
    ffn1_l1 = ["wg1.1", "wu1.1", "wd1.1"]

    meta_pad = jnp.concatenate([jnp.zeros((PAD, d), f32), meta_full], axis=0)
    h0, h0b = run(_embed, meta_pad, x2, "embed", stages=[ici(down1)])
    a1, b1, s1 = run(_ffn_up, h0b, buf["wg1"], buf["wu1"], 0, "ffn_up",
                     stages=lambda: [d2d(down1), ici(ffn2_l0)])
    r1, h1, h1b = run(_down_ln, s1, buf["wd1"], 0, h0, gains[0, 0], biases[0, 0], 0.5, "ffn_down_ln",
                      stages=lambda: [d2d(ffn2_l0), ici(attn_ffn2_l1)])
    n_in = conv_w_in.shape[-1]
    w_in = buf["win"].reshape(N_CHIPS, d, n_in)
    w_out = buf["wout"].reshape(1, 1, d, d)
    p = run(_nn_matmul, h1b, w_in, f32, "conv_in", stages=lambda: [d2d(attn_ffn2_l1), ici(ffn1_l1)])
    z = _conv_fwd(p, conv_w_full, "conv_fwd")
    r2, h2, h2b = run(_down_ln, z[None], w_out, 0, h1, gains[0, 1], biases[0, 1], 1.0, "mix_out_ln",
                      stages=lambda: [d2d(ffn1_l1)])
    wg1, wu1, wd1, wg2, wu2, wd2 = (buf[n] for n in ("wg1", "wu1", "wd1", "wg2", "wu2", "wd2"))
    w_q = buf["wq"].reshape(1, d, d)
    w_o = buf["wo"].reshape(1, 1, d, d)
    kv_full = buf["kv"].reshape(N_CHIPS, d, n_kv).transpose(1, 0, 2).reshape(d, N_CHIPS * n_kv)
    w_k = kv_full[:, :d][None]
    w_v = kv_full[:, d:2 * d][None]
    w_f = jnp.pad(kv_full[:, 2 * d:], ((0, 0), (0, LANES - n_heads)))[None]
    a2, b2, s2 = _ffn_up(h2b, wg2, wu2, 0, "ffn_up")
    r3, h3, h3b = _down_ln(s2, wd2, 0, h2, gains[0, 2], biases[0, 2], 0.5, "ffn_down_ln")
    kk = _nn_matmul(h3b, w_k, bf16, "proj_bf16")
    vv = _nn_matmul(h3b, w_v, bf16, "proj_bf16")
    flog = _nn_matmul(h3b, w_f, f32, "proj_gate")
    cum = _row_scan(flog, "gate_cumsum", fb_pad)
    bk = _tile(t, 640, LANES)
    c_ht = cum[:, :n_heads].T
    c_keys = jnp.where(jnp.arange(t)[None, :] < PAD, 1e30, c_ht)
    cq_rep = jnp.broadcast_to(c_ht[:, :, None], (n_heads, t, LANES))
    ck_rep = jnp.broadcast_to(c_keys[:, :, None], (n_heads, t, LANES))
    ck_rows = c_keys.reshape(n_heads, t // bk, 1, bk)
    a3, b3, s3 = _ffn_up(h3b, wg1, wu1, 1, "ffn_up")
    r4, h4, h4b = _down_ln(s3, wd1, 1, h3, gains[1, 0], biases[1, 0], 0.5, "ffn_down_ln")
    q = _nn_matmul(h4b, w_q, bf16, "proj_q", out_scale=1.0 / math.sqrt(HEAD_DIM))
    o, o32, cl = _attn_fwd(q, kk, vv, cq_rep, ck_rows, "attn_fwd")
    r5, h5, h5b = _down_ln(o[None], w_o, 0, h4, gains[1, 1], biases[1, 1], 1.0, "mix_out_ln")
    a4, b4, s4 = _ffn_up(h5b, wg2, wu2, 1, "ffn_up")
    r6, h6, _ = _down_ln(s4, wd2, 1, h5, gains[1, 2], biases[1, 2], 0.5, "ffn_down_ln")
    dr6, dr6b, dg12, db12, sq = _loss_head(h6, target, r6, gains[1, 2], "loss_head")
    loss_part = 0.5 * sq[0, 0] / d

    m_small = small_pack(m_meta, m_ln_gain, m_ln_bias, m_conv_w, m_f_bias)
    v_small = small_pack(v_meta, v_ln_gain, v_ln_bias, v_conv_w, v_f_bias)
    m3 = {"wg1": tr(m_ffn1_wg), "wu1": tr(m_ffn1_wu), "wd1": m_ffn1_wd, "wg2": tr(m_ffn2_wg),
          "wu2": tr(m_ffn2_wu), "wd2": m_ffn2_wd, "win": m_conv_w_in, "wout": m_conv_w_out,
          "kv": m_kv_w[None], "wq": m_attn_w_q, "wo": m_attn_w_o, "small": m_small[None]}
    v3 = {"wg1": tr(v_ffn1_wg), "wu1": tr(v_ffn1_wu), "wd1": v_ffn1_wd, "wg2": tr(v_ffn2_wg),
          "wu2": tr(v_ffn2_wu), "wd2": v_ffn2_wd, "win": v_conv_w_in, "wout": v_conv_w_out,
          "kv": v_kv_w[None], "wq": v_attn_w_q, "wo": v_attn_w_o, "small": v_small[None]}
    stepped = {}

    def alike(items):
        groups = {}
        for it in items:
            n, l = split(it)
            groups.setdefault((w3[n].shape, l, ("RED_" + n) in buf, n in stepped), []).append(it)
        return groups.values()

    def prepare(items):
        for group in alike(items):
            sends = _rs_prepare([buf["G_" + it] for it in group], [buf["PAIR_" + it] for it in group],
                                idx, "grad_prepare")
            buf.update({"SEND_" + it: s for it, s in zip(group, sends)})

    def finish(items):
        for group in alike(items):
            names = [split(it)[0] for it in group]
            layer = split(group[0])[1]
            intos = [buf["RED_" + n] for n in names] if ("RED_" + names[0]) in buf else None
            reds = _rs_finish([buf["G_" + it] for it in group], [buf["PAIR_" + it] for it in group],
                              [buf["RECV_" + it] for it in group], idx, layer, w3[names[0]].shape[0],
                              intos, "grad_finish")
            buf.update({"RED_" + n: r for n, r in zip(names, reds)})

    def adam(items, grads=None):
        for group in alike(items):
            names = [split(it)[0] for it in group]
            layer = split(group[0])[1]
            gs = [buf["RED_" + n] if grads is None else grads[n] for n in names]
            intos = [stepped[n] for n in names] if names[0] in stepped else None
            res = _adamw([w3[n] for n in names], gs, [m3[n] for n in names], [v3[n] for n in names],
                         layer, intos, "adamw")
            stepped.update(dict(zip(names, res)))

    def ffn_bwd(dr, drb, hb_in, a, b, s, f, layer, on_act=(), after_act=None, on_dwd=(),
                after_dwd=None, on_dx=(), ln=None):
        da, db = run(_ffn_bwd_act, drb, buf["wd" + f], layer, a, b, "ffn_bwd_act", stages=on_act)
        if after_act is not None:
            after_act()
        (buf[f"G_wd{f}.{layer}"],) = run(_tn_matmul, [(s, "stack", drb, "shared")], N_CHIPS, 0.5,
                                         "ffn_dwd", stages=on_dwd)
        if after_dwd is not None:
            after_dwd()
        buf[f"G_wg{f}.{layer}"], buf[f"G_wu{f}.{layer}"] = _tn_matmul(
            [(da, "stack", hb_in, "shared"), (db, "stack", hb_in, "shared")], N_CHIPS, 1.0, "ffn_dwgu")
        return run(_nt_sum, [(da, "stack", buf["wg" + f], layer, True),
                             (db, "stack", buf["wu" + f], layer, True)],
                   dr, ALPHA, N_CHIPS, f32, "ffn_dx", stages=on_dx, ln=ln)

    ffn2_1 = ["wg2.1", "wu2.1", "wd2.1"]
    ffn1_1 = ["wg1.1", "wu1.1", "wd1.1"]
    ffn2_0 = ["wg2.0", "wu2.0", "wd2.0"]
    conv_items = ["wout.0", "win.0"]

    dr5, dr5b, dg11, db11 = ffn_bwd(dr6, dr6b, h5b, a4, b4, s4, "2", 1,
                                    on_dx=lambda: [pair_exchange(ffn2_1)], ln=(r5, gains[1, 1]))
    prepare(ffn2_1)
    (dwo,) = _tn_matmul([(o, "shared", dr5b, "shared")], 1, 1.0, "sq_dw")
    buf["G_wo.0"] = dwo.reshape(N_CHIPS, dq, d)
    do, delta = run(_attn_do, dr5b, w_o[0, 0], o32, bk, "attn_do",
                    stages=lambda: [pair_exchange(["wo.0"])])
    prepare(["wo.0"])
    kt = kk.reshape(t // bk, bk, n_heads, HEAD_DIM).transpose(2, 0, 3, 1)
    dq_att, dc_q, dk, dv, dc_k = run(_attn_bwd, q, kk, vv, kt, do, ck_rep, cl, delta, "attn_bwd",
                                     stages=lambda: [chip_exchange(ffn2_1 + ["wo.0"])])
    finish(ffn2_1 + ["wo.0"])
    dc = (dc_q + dc_k).reshape(n_heads, t)
    (dwq,) = run(_tn_matmul, [(h4b, "shared", dq_att, "shared")], 1, 1.0, "sq_dw",
                 stages=lambda: [pair_share(ffn2_1 + ["wo.0"])])
    buf["G_wq.0"] = dwq.reshape(N_CHIPS, dq, d)
    adam(ffn2_1 + ["wo.0"])
    dr4, dr4b, dg10, db10 = run(_nt_sum, [(dq_att, "cols", w_q, None)], dr5, ALPHA, 1, f32,
                                "sq_dx_res", stages=lambda: [pair_exchange(["wq.0"])],
                                ln=(r4, gains[1, 0]))
    prepare(["wq.0"])
    dh3a = ffn_bwd(dr4, dr4b, h3b, a3, b3, s3, "1", 1,
                   on_act=lambda: [chip_exchange(["wq.0"])],
                   on_dx=lambda: [pair_exchange(ffn1_1)])
    prepare(ffn1_1)
    finish(["wq.0"])
    dc_t = jnp.pad(dc.T, ((0, 0), (0, LANES - n_heads)))
    dlogf = _row_scan(dc_t, "rev_cumsum", reverse=True)
    dfl, dfb_cols = _fgate_bwd(dlogf, flog, fb_pad, "gate_bwd")
    dwk, dwv, dwf = run(_tn_matmul, [(h3b, "shared", g, "shared") for g in (dk, dv, dfl)], 1, 1.0,
                        "kv_dw",
                        stages=lambda: [chip_exchange(ffn1_1), pair_share(["wq.0"])])
    adam(["wq.0"])

    def by_chip(full):
        rows = full.shape[0]
        return full.reshape(rows, N_CHIPS, full.shape[1] // N_CHIPS).transpose(1, 0, 2)

    buf["G_kv.0"] = by_chip(jnp.concatenate([dwk[0], dwv[0], dwf[0][:, :n_heads]], axis=1))
    dr3, dr3b, dg02, db02 = run(
        _nt_sum, [(dk, "cols", w_k, None), (dv, "cols", w_v, None), (dfl, "cols", w_f, None)],
        dh3a, 1.0, 1, f32, "kv_dx", stages=lambda: [pair_exchange(["kv.0"])], ln=(r3, gains[0, 2]))
    prepare(["kv.0"])
    finish(ffn1_1)
    dr2, dr2b, dg01, db01 = ffn_bwd(dr3, dr3b, h2b, a2, b2, s2, "2", 0,
                                    on_act=lambda: [chip_exchange(["kv.0"]), pair_share(ffn1_1)],
                                    after_act=lambda: (adam(ffn1_1), finish(["kv.0"])),
                                    on_dwd=lambda: [pair_share(["kv.0"])],
                                    after_dwd=lambda: adam(["kv.0"]),
                                    on_dx=lambda: [pair_exchange(ffn2_0)], ln=(r2, gains[0, 1]))
    prepare(ffn2_0)
    (dwout,) = _tn_matmul([(z, "shared", dr2b, "shared")], 1, 1.0, "sq_dw")
    buf["G_wout.0"] = dwout.reshape(N_CHIPS, dq, d)
    dz = _nt_sum([(dr2b, "cols", w_out[0], None)], None, 1.0, 1, f32, "sq_dx_f32")
    dp, dconv_w = run(_conv_bwd, dz, p, conv_w_full, "conv_bwd",
                      stages=lambda: [chip_exchange(ffn2_0)])
    (buf["G_win.0"],) = _tn_matmul([(h1b, "shared", dp, "cols")], N_CHIPS, 1.0, "conv_dwin")
    finish(ffn2_0)
    dr1, dr1b, dg00, db00 = run(_nt_sum, [(dp, "cols", w_in, None)], dr2, ALPHA, N_CHIPS, f32,
                                "conv_dx",
                                stages=lambda: [pair_exchange(conv_items), pair_share(ffn2_0)],
                                ln=(r1, gains[0, 0]))
    prepare(conv_items)
    adam(ffn2_0)
    gate_up = ["wg1.0", "wu1.0"]
    da, db = run(_ffn_bwd_act, dr1b, buf["wd1"], 0, a1, b1, "ffn_bwd_act",
                 stages=lambda: [chip_exchange(conv_items)])
    finish(conv_items)
    buf["G_wg1.0"], buf["G_wu1.0"] = run(
        _tn_matmul, [(da, "stack", h0b, "shared"), (db, "stack", h0b, "shared")], N_CHIPS, 1.0,
        "ffn_dwgu", stages=lambda: [pair_share(conv_items)])
    adam(conv_items)
    (buf["G_wd1.0"],) = run(_tn_matmul, [(s1, "stack", dr1b, "shared")], N_CHIPS, 0.5, "ffn_dwd",
                            stages=lambda: [pair_exchange(gate_up)])
    prepare(gate_up)
    dh0 = run(_nt_sum, [(da, "stack", buf["wg1"], 0, True), (db, "stack", buf["wu1"], 0, True)],
              dr1, ALPHA, N_CHIPS, f32, "ffn_dx",
              stages=lambda: [chip_exchange(gate_up), pair_exchange(["wd1.0"])])
    prepare(["wd1.0"])
    finish(gate_up)
    grad_x = dh0[PAD + N_META:][None]
    dmeta = dh0[PAD:PAD + N_META]
    buf["G_small.0"] = by_chip(jnp.concatenate(
        [dmeta, rows8(jnp.concatenate([dg00, dg01, dg02, dg10, dg11, dg12], axis=0)),
         rows8(jnp.concatenate([db00, db01, db02, db10, db11, db12], axis=0)),
         rows8(dconv_w), jnp.zeros((8, d), f32)], axis=0))
    run(None, stages=lambda: [chip_exchange(["wd1.0"]), pair_exchange(["small.0"]), pair_share(gate_up)],
        name="grad_tail_1")
    prepare(["small.0"])
    finish(["wd1.0"])
    adam(gate_up)
    run(None, stages=lambda: [chip_exchange(["small.0"]), pair_share(["wd1.0"])], name="grad_tail_2")
    finish(["small.0"])
    adam(["wd1.0"])
    run(None, stages=lambda: [pair_share(["small.0"])], name="grad_tail_3")

    tail = jnp.zeros((LANES,), f32).at[:n_heads].set(dfb_cols[0, :n_heads]).at[n_heads].set(loss_part)
    tail = lax.psum(tail, ("x", "y", "c"))
    loss = tail[n_heads]
    g_fb = tail[:n_heads]
    g_small = jnp.concatenate([buf["RED_small"][0, :40],
                               rows8(jnp.pad(g_fb, (0, dq - n_heads))[None])], axis=0)
    adam(["small.0"], grads={"small": g_small[None]})

    def unpack(pk):
        return (pk[:16], pk[16:22].reshape(DEPTH, 3, dq), pk[24:30].reshape(DEPTH, 3, dq),
                pk[32:35].reshape(1, 3, dq), pk[40, :n_heads])

    def order(pick):
        mt, g, b, cw, fb = unpack(pick("small")[0])
        big = {n: pick(n) for n in w3 if n != "small"}
        big["kv"] = big["kv"][0]
        for n in transposed:
            big[n] = tr(big[n])
        return [mt, big["wg1"], big["wu1"], big["wd1"], big["wg2"], big["wu2"], big["wd2"], g, b,
                big["win"], cw, big["wout"], big["kv"], fb, big["wq"], big["wo"]]

    return (loss, grad_x, *order(lambda n: stepped[n][3]), *order(lambda n: stepped[n][0]),
            *order(lambda n: stepped[n][1]), *order(lambda n: stepped[n][2]))
```

```python
import functools
import math

import jax
import jax.numpy as jnp
from jax import lax
from jax.experimental import pallas as pl
from jax.experimental.pallas import tpu as pltpu

f32 = jnp.float32
bf16 = jnp.bfloat16

N_META = 16
PAD = 112
HEAD_DIM = 128
DEPTH = 2
LN_EPS = 1e-5
ALPHA = (2 * DEPTH) ** 0.25
NEG_INF = -1e30
N_CHIPS = 4
SMALL_ROWS = 48
LANES = 128

ADAM_LR = 0.001
ADAM_B1 = 0.9
ADAM_B2 = 0.999
ADAM_EPS = 1e-08
ADAM_WD = 0.01
ADAM_STEP = 10

VMEM_LIMIT_BYTES = 56 * 1024 * 1024
ROWS_WIDE = 1664
ROWS_ACC = 1040
ROWS_ACC_LN = 832
ROWS_TN = 2080
ROWS_TN_ONE = 4160
STRIP = 16
MESH = pl.DeviceIdType.MESH

NT_DIMS = (((1,), (1,)), ((), ()))
TN_DIMS = (((0,), (0,)), ((), ()))


def _tile(n, target, mult):
    best = None
    for d in range(mult, min(n, target) + 1, mult):
        if n % d == 0:
            best = d
    assert best is not None, (n, target, mult)
    return best


def _params(*sem):
    return pltpu.CompilerParams(dimension_semantics=sem, vmem_limit_bytes=VMEM_LIMIT_BYTES)


class _Place:
    def __init__(self):
        self.cx, self.cy, self.c = lax.axis_index("x"), lax.axis_index("y"), lax.axis_index("c")
        self.chips = [(1 - self.cx, self.cy), (self.cx, 1 - self.cy), (1 - self.cx, 1 - self.cy)]
        self.me = 2 * self.cx + self.cy
        self.slots = [self.me] + [2 * px + py for px, py in self.chips]
        self.sib = (self.cx, self.cy, 1 - self.c)


class _Copies:
    def __init__(self):
        self.srcs, self.bufs, self.news = {}, {}, {}
        self.plans = []
        self.out_bufs, self.out_news = {}, {}

    def add(self, plan, n_copies, srcs=None, bufs=None, news=None):
        for have, more in ((self.srcs, srcs), (self.bufs, bufs), (self.news, news)):
            for key, val in (more or {}).items():
                assert key not in have or have[key] is val, key
                have[key] = val
        self.plans.append((plan, n_copies))

    def empty(self):
        return not self.plans

    def count(self):
        return sum(n for _, n in self.plans)

    def copies(self, src_refs, buf_refs, new_refs, send, recv):
        place = _Place()
        srcs = dict(zip(self.srcs, src_refs))
        bufs = dict(zip(self.bufs, buf_refs))
        news = dict(zip(self.news, new_refs))
        out = []
        for plan, n_copies in self.plans:
            triples = plan(srcs, bufs, news, place)
            assert len(triples) == n_copies
            for src, dst, dev in triples:
                n = len(out)
                out.append(pltpu.make_async_remote_copy(
                    src_ref=src, dst_ref=dst, send_sem=send.at[n], recv_sem=recv.at[n],
                    device_id=dev, device_id_type=MESH))
        return out

    def land(self, results):
        n_b = len(self.bufs)
        self.out_bufs = dict(zip(self.bufs, results[:n_b]))
        self.out_news = dict(zip(self.news, results[n_b:]))


def _pallas(comm, body, *, name, grid, in_specs, out_specs, out_shape, compiler_params,
            scratch_shapes=(), input_output_aliases=None):
    aliases = dict(input_output_aliases or {})
    if comm is None or comm.empty():
        return pl.pallas_call(body, name=name, grid=grid, in_specs=in_specs, out_specs=out_specs,
                              out_shape=out_shape, scratch_shapes=list(scratch_shapes),
                              input_output_aliases=aliases, compiler_params=compiler_params)
    single = not isinstance(out_shape, (list, tuple))
    out_shapes = [out_shape] if single else list(out_shape)
    out_specs_l = [out_specs] if single else list(out_specs)
    n_in, n_out, n_scr = len(in_specs), len(out_shapes), len(scratch_shapes)
    n_s, n_b, n_n = len(comm.srcs), len(comm.bufs), len(comm.news)
    n_copies = comm.count()

    def wrapped(*refs):
        ins = refs[:n_in]
        src_refs = refs[n_in:n_in + n_s]
        o0 = n_in + n_s + n_b
        outs = refs[o0:o0 + n_out]
        buf_refs = refs[o0 + n_out:o0 + n_out + n_b]
        new_refs = refs[o0 + n_out + n_b:o0 + n_out + n_b + n_n]
        rest = refs[o0 + n_out + n_b + n_n:]
        scratch, (send, recv) = rest[:n_scr], rest[n_scr:]
        ids = [pl.program_id(a) for a in range(len(grid))]
        first = functools.reduce(jnp.logical_and, [i == 0 for i in ids])
        last = functools.reduce(jnp.logical_and, [i == g - 1 for i, g in zip(ids, grid)])

        @pl.when(first)
        def _():
            for cp in comm.copies(src_refs, buf_refs, new_refs, send, recv):
                cp.start()

        body(*ins, *outs, *scratch)

        @pl.when(last)
        def _():
            for cp in comm.copies(src_refs, buf_refs, new_refs, send, recv):
                cp.wait()

    hbm = pl.BlockSpec(memory_space=pl.ANY)
    for j in range(n_b):
        aliases[n_in + n_s + j] = n_out + j
    call = pl.pallas_call(
        wrapped, name=name, grid=grid,
        in_specs=[*in_specs, *([hbm] * (n_s + n_b))],
        out_specs=[*out_specs_l, *([hbm] * (n_b + n_n))],
        out_shape=[*out_shapes,
                   *[jax.ShapeDtypeStruct(a.shape, a.dtype) for a in comm.bufs.values()],
                   *comm.news.values()],
        scratch_shapes=[*scratch_shapes, pltpu.SemaphoreType.DMA((n_copies,)),
                        pltpu.SemaphoreType.DMA((n_copies,))],
        input_output_aliases=aliases, compiler_params=compiler_params)

    def run(*args):
        res = call(*args, *comm.srcs.values(), *comm.bufs.values())
        comm.land(res[n_out:])
        return res[0] if single else res[:n_out]

    return run


def _copy_call(comm, name):
    n_s, n_b, n_n = len(comm.srcs), len(comm.bufs), len(comm.news)
    n_copies = comm.count()

    def body(*refs):
        src_refs = refs[:n_s]
        buf_refs = refs[n_s + n_b:n_s + 2 * n_b]
        new_refs = refs[n_s + 2 * n_b:n_s + 2 * n_b + n_n]
        send, recv = refs[n_s + 2 * n_b + n_n:]
        copies = comm.copies(src_refs, buf_refs, new_refs, send, recv)
        for cp in copies:
            cp.start()
        for cp in copies:
            cp.wait()

    hbm = pl.BlockSpec(memory_space=pl.ANY)
    res = pl.pallas_call(
        body, name=name,
        in_specs=[hbm] * (n_s + n_b), out_specs=[hbm] * (n_b + n_n),
        out_shape=[*[jax.ShapeDtypeStruct(a.shape, a.dtype) for a in comm.bufs.values()],
                   *comm.news.values()],
        input_output_aliases={n_s + j: j for j in range(n_b)},
        scratch_shapes=[pltpu.SemaphoreType.DMA((n_copies,)), pltpu.SemaphoreType.DMA((n_copies,))],
    )(*comm.srcs.values(), *comm.bufs.values())
    comm.land(res)


def _embed(meta_pad, x, name, comm=None):
    seq, d = x.shape
    t = seq + LANES

    def body(m_ref, x_ref, h_ref, hb_ref):
        first = pl.program_id(0) == 0
        v = jnp.where(first, m_ref[...], x_ref[...])
        h_ref[...] = v
        hb_ref[...] = v.astype(bf16)

    return _pallas(
        comm, body, name=name, grid=(t // LANES,),
        in_specs=[pl.BlockSpec((LANES, d), lambda i: (0, 0)),
                  pl.BlockSpec((LANES, d), lambda i: (jnp.maximum(i - 1, 0), 0))],
        out_specs=[pl.BlockSpec((LANES, d), lambda i: (i, 0)),
                   pl.BlockSpec((LANES, d), lambda i: (i, 0))],
        out_shape=[jax.ShapeDtypeStruct((t, d), f32), jax.ShapeDtypeStruct((t, d), bf16)],
        compiler_params=_params("parallel"),
    )(meta_pad, x)


def _nn_matmul(x, w, out_dtype, name, comm=None, out_scale=None):
    t, k = x.shape
    s_n, _, n = w.shape
    assert s_n == 1 or n % LANES == 0
    tm = _tile(t, ROWS_WIDE, 16)

    def body(x_ref, w_ref, o_ref):
        res = jnp.dot(x_ref[...].astype(bf16), w_ref[...], preferred_element_type=f32)
        if out_scale is not None:
            res = res * out_scale
        o_ref[...] = res.astype(o_ref.dtype)

    return _pallas(
        comm, body, name=name, grid=(s_n, t // tm),
        in_specs=[pl.BlockSpec((tm, k), lambda s, i: (i, 0)),
                  pl.BlockSpec((None, k, n), lambda s, i: (s, 0, 0))],
        out_specs=pl.BlockSpec((tm, n), lambda s, i: (i, s)),
        out_shape=jax.ShapeDtypeStruct((t, s_n * n), out_dtype),
        compiler_params=_params("parallel", "parallel"),
    )(x, w)


def _ffn_up(hb, wg, wu, layer, name, comm=None):
    t, d = hb.shape
    s_n, _, n, _ = wg.shape
    tm = _tile(t, ROWS_WIDE, 16)

    def body(x_ref, wg_ref, wu_ref, a_ref, b_ref, s_ref):
        x = x_ref[...]
        a = lax.dot_general(x, wg_ref[...], NT_DIMS, preferred_element_type=f32)
        b = lax.dot_general(x, wu_ref[...], NT_DIMS, preferred_element_type=f32)
        a_ref[...] = a.astype(bf16)
        b_ref[...] = b.astype(bf16)
        s_ref[...] = (a * jax.nn.sigmoid(a) * b).astype(bf16)

    wspec = pl.BlockSpec((None, None, n, d), lambda s, i: (s, layer, 0, 0))
    ospec = pl.BlockSpec((None, tm, n), lambda s, i: (s, i, 0))
    return _pallas(
        comm, body, name=name, grid=(s_n, t // tm),
        in_specs=[pl.BlockSpec((tm, d), lambda s, i: (i, 0)), wspec, wspec],
        out_specs=[ospec, ospec, ospec],
        out_shape=[jax.ShapeDtypeStruct((s_n, t, n), bf16)] * 3,
        compiler_params=_params("parallel", "parallel"),
    )(hb, wg, wu)


def _down_ln(x, w, layer, hprev, gain, bias, beta, name, comm=None):
    s_n, t, k = x.shape
    d = w.shape[-1]
    tm = _tile(t, ROWS_ACC, 16)

    def body(x_ref, w_ref, h_ref, g_ref, b_ref, r_out, h_out, hb_out, acc):
        s = pl.program_id(1)

        @pl.when(s == 0)
        def _():
            acc[...] = jnp.zeros_like(acc)

        acc[...] += jnp.dot(x_ref[...], w_ref[...], preferred_element_type=f32)

        @pl.when(s == s_n - 1)
        def _():
            r = ALPHA * h_ref[...] + beta * acc[...]
            mu = jnp.mean(r, axis=-1, keepdims=True)
            xc = r - mu
            var = jnp.mean(xc * xc, axis=-1, keepdims=True)
            y = xc * lax.rsqrt(var + LN_EPS) * g_ref[...] + b_ref[...]
            r_out[...] = r
            h_out[...] = y
            hb_out[...] = y.astype(bf16)

    row = pl.BlockSpec((tm, d), lambda i, s: (i, 0))
    vec = pl.BlockSpec((1, d), lambda i, s: (0, 0))
    return _pallas(
        comm, body, name=name, grid=(t // tm, s_n),
        in_specs=[pl.BlockSpec((None, tm, k), lambda i, s: (s, i, 0)),
                  pl.BlockSpec((None, None, k, d), lambda i, s: (s, layer, 0, 0)),
                  row, vec, vec],
        out_specs=[row, row, row],
        out_shape=[jax.ShapeDtypeStruct((t, d), f32), jax.ShapeDtypeStruct((t, d), f32),
                   jax.ShapeDtypeStruct((t, d), bf16)],
        scratch_shapes=[pltpu.VMEM((tm, d), f32)],
        compiler_params=_params("parallel", "arbitrary"),
    )(x, w, hprev, gain, bias)


def _conv_fwd(p, conv_w, name):
    t, d3 = p.shape
    d = d3 // 3
    tm = _tile(t, 320, 8)
    hb = tm // 8

    def body(p_ref, prev_ref, w_ref, z_ref):
        i = pl.program_id(0)
        rows = i * tm - 8 + lax.broadcasted_iota(jnp.int32, (tm + 8, 1), 0)
        cg = jnp.concatenate([prev_ref[:, d:2 * d], p_ref[:, d:2 * d]], axis=0)
        val = jnp.concatenate([prev_ref[:, 2 * d:], p_ref[:, 2 * d:]], axis=0)
        u = jnp.where(rows >= PAD, cg * val, 0.0)
        y = (w_ref[2:3, :] * u + w_ref[1:2, :] * pltpu.roll(u, 1, 0)
             + w_ref[0:1, :] * pltpu.roll(u, 2, 0))
        z_ref[...] = (p_ref[:, :d] * y[8:]).astype(bf16)

    return pl.pallas_call(
        body, name=name, grid=(t // tm,),
        in_specs=[pl.BlockSpec((tm, d3), lambda i: (i, 0)),
                  pl.BlockSpec((8, d3), lambda i: (jnp.maximum(i * hb - 1, 0), 0)),
                  pl.BlockSpec((3, d), lambda i: (0, 0))],
        out_specs=pl.BlockSpec((tm, d), lambda i: (i, 0)),
        out_shape=jax.ShapeDtypeStruct((t, d), bf16),
        compiler_params=_params("parallel"),
    )(p, p, conv_w)


def _row_scan(x, name, fbias=None, reverse=False):
    t, n = x.shape
    blk = _tile(t, 640, LANES)
    n_blk = t // blk
    gate = fbias is not None

    def body(*refs):
        if gate:
            x_ref, fb_ref, o_ref, carry = refs
        else:
            x_ref, o_ref, carry = refs
        i = pl.program_id(0)

        @pl.when(i == 0)
        def _():
            carry[...] = jnp.zeros_like(carry)

        v = x_ref[...]
        r = lax.broadcasted_iota(jnp.int32, (blk, n), 0)
        if gate:
            v = v + fb_ref[...]
            v = jnp.minimum(v, 0.0) - jnp.log1p(jnp.exp(-jnp.abs(v)))
            v = jnp.where(i * blk + r >= PAD, v, 0.0)
        sh = 1
        while sh < blk:
            if reverse:
                v = v + jnp.where(r < blk - sh, pltpu.roll(v, blk - sh, 0), 0.0)
            else:
                v = v + jnp.where(r >= sh, pltpu.roll(v, sh, 0), 0.0)
            sh *= 2
        v = v + carry[...]
        o_ref[...] = v
        carry[...] = o_ref[0:1, :] if reverse else o_ref[blk - 1:blk, :]

    order = (lambda i: (n_blk - 1 - i, 0)) if reverse else (lambda i: (i, 0))
    in_specs = [pl.BlockSpec((blk, n), order)]
    args = [x]
    if gate:
        in_specs.append(pl.BlockSpec((1, n), lambda i: (0, 0)))
        args.append(fbias)
    return pl.pallas_call(
        body, name=name, grid=(n_blk,),
        in_specs=in_specs,
        out_specs=pl.BlockSpec((blk, n), order),
        out_shape=jax.ShapeDtypeStruct((t, n), f32),
        scratch_shapes=[pltpu.VMEM((1, n), f32)],
        compiler_params=_params("arbitrary"),
    )(*args)


def _lanes(x, n):
    return jnp.concatenate([x] * (n // LANES), axis=1)


def _attn_fwd(q, k, v, cq_rep, ck_rows, name):
    t, d = q.shape
    n_heads = d // HEAD_DIM
    bk = ck_rows.shape[-1]
    bq = bk

    def lane_fold(x, op):
        out = x[:, :LANES]
        for c0 in range(LANES, bk, LANES):
            out = op(out, x[:, c0:c0 + LANES])
        return out

    def body(q_ref, k_ref, v_ref, cq_ref, ck_ref, o_ref, o32_ref, cl_ref,
             s_scr, p_scr, m_scr, l_scr, red_scr, acc_scr):
        i = pl.program_id(1)
        m_scr[...] = jnp.full_like(m_scr, NEG_INF)
        l_scr[...] = jnp.zeros_like(l_scr)
        acc_scr[...] = jnp.zeros_like(acc_scr)
        qb = q_ref[...]
        ahead = (lax.broadcasted_iota(jnp.int32, (STRIP, bk), 1)
                 - lax.broadcasted_iota(jnp.int32, (STRIP, bk), 0))

        def tile(j, diagonal):
            k0 = pl.multiple_of(j * bk, bk)
            s_scr[...] = lax.dot_general(qb, k_ref[pl.ds(k0, bk), :], NT_DIMS,
                                         preferred_element_type=f32)
            ck = ck_ref[j]
            for r in range(0, bq, STRIP):
                rows = slice(r, r + STRIP)
                s = s_scr[rows, :] + _lanes(cq_ref[rows, :], bk) - ck
                if diagonal:
                    s = jnp.where(ahead <= r, s, NEG_INF)
                s_scr[rows, :] = s
                red_scr[rows, :] = lane_fold(s, jnp.maximum)
            m_old = m_scr[...]
            m_new = jnp.maximum(m_old, jnp.broadcast_to(
                jnp.max(red_scr[...], axis=1, keepdims=True), (bq, LANES)))
            a = jnp.exp(m_old - m_new)
            m_scr[...] = m_new
            for r in range(0, bq, STRIP):
                rows = slice(r, r + STRIP)
                pr = jnp.exp(s_scr[rows, :] - _lanes(m_scr[rows, :], bk))
                red_scr[rows, :] = lane_fold(pr, jnp.add)
                p_scr[rows, :] = pr.astype(bf16)
            l_scr[...] = a * l_scr[...] + jnp.broadcast_to(
                jnp.sum(red_scr[...], axis=1, keepdims=True), (bq, LANES))
            acc_scr[...] = a * acc_scr[...] + jnp.dot(
                p_scr[...], v_ref[pl.ds(k0, bk), :], preferred_element_type=f32)

        def full_tile(j, carry):
            tile(j, False)
            return carry

        lax.fori_loop(0, i, full_tile, 0)
        tile(i, True)
        out = acc_scr[...] / l_scr[...]
        o_ref[...] = out.astype(bf16)
        o32_ref[...] = out
        red_scr[...] = cq_ref[...] - (m_scr[...] + jnp.log(l_scr[...]))
        for c0 in range(0, bq, LANES):
            cl_ref[:, c0:c0 + LANES] = red_scr[c0:c0 + LANES, :].T[0:1, :]

    qblk = pl.BlockSpec((bq, HEAD_DIM), lambda h, i: (i, h))
    head_rows = pl.BlockSpec((t, HEAD_DIM), lambda h, i: (0, h))
    rep = pl.BlockSpec((None, bq, LANES), lambda h, i: (h, i, 0))
    col = pltpu.VMEM((bq, LANES), f32)
    return pl.pallas_call(
        body, name=name, grid=(n_heads, t // bq),
        in_specs=[qblk, head_rows, head_rows, rep,
                  pl.BlockSpec((None, t // bk, 1, bk), lambda h, i: (h, 0, 0, 0))],
        out_specs=[qblk, qblk, pl.BlockSpec((None, None, 1, bq), lambda h, i: (h, i, 0, 0))],
        out_shape=[jax.ShapeDtypeStruct((t, d), bf16), jax.ShapeDtypeStruct((t, d), f32),
                   jax.ShapeDtypeStruct((n_heads, t // bq, 1, bq), f32)],
        scratch_shapes=[pltpu.VMEM((bq, bk), f32), pltpu.VMEM((bq, bk), bf16), col, col, col,
                        pltpu.VMEM((bq, HEAD_DIM), f32)],
        compiler_params=_params("parallel", "parallel"),
    )(q, k, v, cq_rep, ck_rows)


def _loss_head(h, target, r, gain, name):
    t, d = h.shape

    def body(h_ref, t_ref, r_ref, g_ref, dr_ref, drb_ref, dg_ref, db_ref, loss_ref):
        i = pl.program_id(0)

        @pl.when(i == 0)
        def _():
            loss_ref[...] = jnp.zeros_like(loss_ref)
            dg_ref[...] = jnp.zeros_like(dg_ref)
            db_ref[...] = jnp.zeros_like(db_ref)

        diff = jnp.where(i >= 1, h_ref[...] - t_ref[...], 0.0)
        loss_ref[...] += jnp.sum(diff * diff)
        dr, dg, db = _ln_bwd_rows(diff * (1.0 / d), r_ref[...], g_ref[...])
        dr_ref[...] = dr
        drb_ref[...] = dr.astype(bf16)
        dg_ref[...] += dg
        db_ref[...] += db

    row = pl.BlockSpec((LANES, d), lambda i: (i, 0))
    vec = pl.BlockSpec((1, d), lambda i: (0, 0))
    return pl.pallas_call(
        body, name=name, grid=(t // LANES,),
        in_specs=[row, pl.BlockSpec((LANES, d), lambda i: (jnp.maximum(i - 1, 0), 0)), row, vec],
        out_specs=[row, row, vec, vec, pl.BlockSpec((1, LANES), lambda i: (0, 0))],
        out_shape=[jax.ShapeDtypeStruct((t, d), f32), jax.ShapeDtypeStruct((t, d), bf16),
                   jax.ShapeDtypeStruct((1, d), f32), jax.ShapeDtypeStruct((1, d), f32),
                   jax.ShapeDtypeStruct((1, LANES), f32)],
        compiler_params=_params("arbitrary"),
    )(h, target, r, gain)


def _ln_bwd_rows(dy, rr, gain):
    mu = jnp.mean(rr, axis=-1, keepdims=True)
    xc = rr - mu
    var = jnp.mean(xc * xc, axis=-1, keepdims=True)
    rstd = lax.rsqrt(var + LN_EPS)
    xhat = xc * rstd
    dxh = dy * gain
    m1 = jnp.mean(dxh, axis=-1, keepdims=True)
    m2 = jnp.mean(dxh * xhat, axis=-1, keepdims=True)
    dr = rstd * (dxh - m1 - xhat * m2)
    return dr, jnp.sum(dy * xhat, axis=0, keepdims=True), jnp.sum(dy, axis=0, keepdims=True)


def _ffn_bwd_act(drb, wd, layer, a, b, name, comm=None):
    t, d = drb.shape
    s_n, _, n = a.shape
    tm = _tile(t, ROWS_WIDE, 16)

    def body(dr_ref, w_ref, a_ref, b_ref, da_ref, db_ref):
        ds = 0.5 * lax.dot_general(dr_ref[...], w_ref[...], NT_DIMS, preferred_element_type=f32)
        da_ref[...], db_ref[...] = _swiglu_bwd(ds, a_ref, b_ref)

    act = pl.BlockSpec((None, tm, n), lambda s, i: (s, i, 0))
    return _pallas(
        comm, body, name=name, grid=(s_n, t // tm),
        in_specs=[pl.BlockSpec((tm, d), lambda s, i: (i, 0)),
                  pl.BlockSpec((None, None, n, d), lambda s, i: (s, layer, 0, 0)), act, act],
        out_specs=[act, act],
        out_shape=[jax.ShapeDtypeStruct((s_n, t, n), bf16), jax.ShapeDtypeStruct((s_n, t, n), bf16)],
        compiler_params=_params("parallel", "parallel"),
    )(drb, wd, a, b)


def _swiglu_bwd(ds, a_ref, b_ref):
    av = a_ref[...].astype(f32)
    sig = jax.nn.sigmoid(av)
    da = ds * b_ref[...].astype(f32) * (sig * (1.0 + av * (1.0 - sig)))
    return da.astype(bf16), (ds * (av * sig)).astype(bf16)


def _act_spec(mode, tt, k, t_first):
    def fix(fn):
        return (lambda i, s: fn(s, i)) if t_first else fn
    if mode == "shared":
        return pl.BlockSpec((tt, k), fix(lambda s, i: (i, 0)))
    if mode == "cols":
        return pl.BlockSpec((tt, k), fix(lambda s, i: (i, s)))
    assert mode == "stack"
    return pl.BlockSpec((None, tt, k), fix(lambda s, i: (s, i, 0)))


def _act_width(arr, mode, s_n):
    return arr.shape[-1] // s_n if mode == "cols" else arr.shape[-1]


def _tn_matmul(pairs, s_n, scale, name, comm=None):
    t = pairs[0][0].shape[-2]
    tt = _tile(t, ROWS_TN_ONE if len(pairs) == 1 else ROWS_TN, 16)
    n_t = t // tt
    arrays, specs, where = [], [], []
    for x, xmode, y, ymode in pairs:
        pos = []
        for arr, mode in ((x, xmode), (y, ymode)):
            hit = [j for j, a in enumerate(arrays) if a is arr]
            if not hit:
                arrays.append(arr)
                specs.append(_act_spec(mode, tt, _act_width(arr, mode, s_n), False))
                hit = [len(arrays) - 1]
            pos.append(hit[0])
        where.append(pos)
    widths = [(_act_width(x, xm, s_n), _act_width(y, ym, s_n)) for x, xm, y, ym in pairs]
    n_a = len(arrays)

    def body(*refs):
        i = pl.program_id(1)
        for (px, py), o_ref in zip(where, refs[n_a:]):
            part = lax.dot_general(refs[px][...].astype(bf16), refs[py][...].astype(bf16), TN_DIMS,
                                   preferred_element_type=f32)

            @pl.when(i == 0)
            def _():
                o_ref[...] = part

            @pl.when(i > 0)
            def _():
                o_ref[...] += part

            if scale != 1.0:
                @pl.when(i == n_t - 1)
                def _():
                    o_ref[...] = o_ref[...] * scale

    return _pallas(
        comm, body, name=name, grid=(s_n, n_t),
        in_specs=specs,
        out_specs=[pl.BlockSpec((None, kx, ky), lambda s, i: (s, 0, 0)) for kx, ky in widths],
        out_shape=[jax.ShapeDtypeStruct((s_n, kx, ky), f32) for kx, ky in widths],
        compiler_params=_params("parallel", "arbitrary"),
    )(*arrays)


def _nt_sum(pairs, base, base_scale, s_n, out_dtype, name, comm=None, ln=None):
    t = pairs[0][0].shape[-2]
    pairs = [(*pr, False)[:5] for pr in pairs]
    d = pairs[0][2].shape[-1] if pairs[0][4] else pairs[0][2].shape[-2]
    tm = _tile(t, ROWS_ACC if ln is None else ROWS_ACC_LN, 16)
    n_p = len(pairs)
    has_base = base is not None
    flipped = [pr[4] for pr in pairs]
    n_out = 1 if ln is None else 4

    def body(*refs):
        dy_refs = refs[0:2 * n_p:2]
        w_refs = refs[1:2 * n_p:2]
        rest = refs[2 * n_p:]
        base_ref = rest[0] if has_base else None
        o_refs, acc = rest[-1 - n_out:-1], rest[-1]
        s = pl.program_id(1)

        if ln is not None:
            @pl.when(jnp.logical_and(pl.program_id(0) == 0, s == 0))
            def _():
                o_refs[2][...] = jnp.zeros_like(o_refs[2])
                o_refs[3][...] = jnp.zeros_like(o_refs[3])

        @pl.when(s == 0)
        def _():
            acc[...] = jnp.zeros_like(acc)

        tot = None
        for dy_ref, w_ref, flip in zip(dy_refs, w_refs, flipped):
            dyv = dy_ref[...].astype(bf16)
            if flip:
                part = jnp.dot(dyv, w_ref[...], preferred_element_type=f32)
            else:
                part = lax.dot_general(dyv, w_ref[...], NT_DIMS, preferred_element_type=f32)
            tot = part if tot is None else tot + part
        acc[...] += tot

        @pl.when(s == s_n - 1)
        def _():
            res = acc[...]
            if has_base:
                res = base_scale * base_ref[...] + res
            if ln is None:
                o_refs[0][...] = res.astype(o_refs[0].dtype)
            else:
                r_ref, g_ref = rest[-7], rest[-6]
                dr, dg, db = _ln_bwd_rows(res, r_ref[...], g_ref[...])
                o_refs[0][...] = dr
                o_refs[1][...] = dr.astype(bf16)
                o_refs[2][...] += dg
                o_refs[3][...] += db

    in_specs, args = [], []
    for dy, mode, w, layer, flip in pairs:
        k = _act_width(dy, mode, s_n)
        in_specs.append(_act_spec(mode, tm, k, True))
        wshape = (k, d) if flip else (d, k)
        if layer is None:
            in_specs.append(pl.BlockSpec((None, *wshape), lambda i, s: (s, 0, 0)))
        else:
            in_specs.append(pl.BlockSpec((None, None, *wshape),
                                         functools.partial(lambda i, s, l: (s, l, 0, 0), l=layer)))
        args += [dy, w]
    row = pl.BlockSpec((tm, d), lambda i, s: (i, 0))
    vec = pl.BlockSpec((1, d), lambda i, s: (0, 0))
    if has_base:
        in_specs.append(row)
        args.append(base)
    if ln is None:
        out_specs = row
        out_shape = jax.ShapeDtypeStruct((t, d), out_dtype)
    else:
        in_specs += [row, vec]
        args += list(ln)
        out_specs = [row, row, vec, vec]
        out_shape = [jax.ShapeDtypeStruct((t, d), f32), jax.ShapeDtypeStruct((t, d), bf16),
                     jax.ShapeDtypeStruct((1, d), f32), jax.ShapeDtypeStruct((1, d), f32)]
    return _pallas(
        comm, body, name=name, grid=(t // tm, s_n),
        in_specs=in_specs, out_specs=out_specs, out_shape=out_shape,
        scratch_shapes=[pltpu.VMEM((tm, d), f32)],
        compiler_params=_params("parallel" if ln is None else "arbitrary", "arbitrary"),
    )(*args)


def _conv_bwd(dz, p, conv_w, name, comm=None):
    t, d3 = p.shape
    d = d3 // 3
    tm = _tile(t, 320, 8)
    hb = tm // 8
    last8 = t // 8 - 1
    n_ext = tm + 8

    def body(dz_ref, dzn_ref, p_ref, pp_ref, pn_ref, w_ref, dp_ref, dw_ref):
        i = pl.program_id(0)

        @pl.when(i == 0)
        def _():
            dw_ref[...] = jnp.zeros_like(dw_ref)

        w0, w1, w2 = w_ref[0:1, :], w_ref[1:2, :], w_ref[2:3, :]
        rows_u = i * tm - 8 + lax.broadcasted_iota(jnp.int32, (n_ext, 1), 0)
        cg = jnp.concatenate([pp_ref[:, d:2 * d], p_ref[:, d:2 * d]], axis=0)
        val = jnp.concatenate([pp_ref[:, 2 * d:], p_ref[:, 2 * d:]], axis=0)
        u = jnp.where(rows_u >= PAD, cg * val, 0.0)
        u1 = pltpu.roll(u, 1, 0)
        u2 = pltpu.roll(u, 2, 0)
        y = (w2 * u + w1 * u1 + w0 * u2)[8:]
        dzv = dz_ref[...]
        bg = p_ref[:, :d]
        rows_n = (i + 1) * tm + lax.broadcasted_iota(jnp.int32, (8, 1), 0)
        dy_main = dzv * bg
        dy_next = jnp.where(rows_n < t, dzn_ref[...] * pn_ref[:, :d], 0.0)
        dye = jnp.concatenate([dy_main, dy_next], axis=0)
        du = (w2 * dye + w1 * pltpu.roll(dye, n_ext - 1, 0)
              + w0 * pltpu.roll(dye, n_ext - 2, 0))[:tm]
        du = jnp.where(rows_u[8:] >= PAD, du, 0.0)
        dp_ref[:, :d] = (dzv * y).astype(bf16)
        dp_ref[:, d:2 * d] = (du * val[8:]).astype(bf16)
        dp_ref[:, 2 * d:] = (du * cg[8:]).astype(bf16)
        dw_ref[0:1, :] += jnp.sum(dy_main * u2[8:], axis=0, keepdims=True)
        dw_ref[1:2, :] += jnp.sum(dy_main * u1[8:], axis=0, keepdims=True)
        dw_ref[2:3, :] += jnp.sum(dy_main * u[8:], axis=0, keepdims=True)

    nxt = lambda i: (jnp.minimum((i + 1) * hb, last8), 0)
    return _pallas(
        comm, body, name=name, grid=(t // tm,),
        in_specs=[pl.BlockSpec((tm, d), lambda i: (i, 0)),
                  pl.BlockSpec((8, d), nxt),
                  pl.BlockSpec((tm, d3), lambda i: (i, 0)),
                  pl.BlockSpec((8, d3), lambda i: (jnp.maximum(i * hb - 1, 0), 0)),
                  pl.BlockSpec((8, d3), nxt),
                  pl.BlockSpec((3, d), lambda i: (0, 0))],
        out_specs=[pl.BlockSpec((tm, d3), lambda i: (i, 0)),
                   pl.BlockSpec((3, d), lambda i: (0, 0))],
        out_shape=[jax.ShapeDtypeStruct((t, d3), bf16), jax.ShapeDtypeStruct((3, d), f32)],
        compiler_params=_params("arbitrary"),
    )(dz, dz, p, p, p, conv_w)


def _attn_do(drb, w_o, o, bq, name, comm=None):
    t, d = drb.shape
    n_heads = d // HEAD_DIM

    def body(dr_ref, w_ref, o_ref, do_ref, delta_ref):
        do = lax.dot_general(dr_ref[...], w_ref[...], NT_DIMS, preferred_element_type=f32).astype(bf16)
        do_ref[...] = do
        prod = o_ref[...] * do.astype(f32)
        for h in range(n_heads):
            for c0 in range(0, bq, LANES):
                blk = prod[c0:c0 + LANES, h * HEAD_DIM:(h + 1) * HEAD_DIM]
                delta_ref[h, :, c0:c0 + LANES] = jnp.sum(blk.T, axis=0, keepdims=True)

    row = pl.BlockSpec((bq, d), lambda i: (i, 0))
    return _pallas(
        comm, body, name=name, grid=(t // bq,),
        in_specs=[row, pl.BlockSpec((d, d), lambda i: (0, 0)), row],
        out_specs=[row, pl.BlockSpec((n_heads, None, 1, bq), lambda i: (0, i, 0, 0))],
        out_shape=[jax.ShapeDtypeStruct((t, d), bf16),
                   jax.ShapeDtypeStruct((n_heads, t // bq, 1, bq), f32)],
        compiler_params=_params("parallel"),
    )(drb, w_o, o)


def _attn_bwd(q, k, v, kt, do, ckey, cl_rows, delta_rows, name, comm=None):
    t, d = q.shape
    n_heads = d // HEAD_DIM
    bk = kt.shape[-1]
    bq = bk
    n_kv = t // bk
    n_q = t // bq
    scale = 1.0 / math.sqrt(HEAD_DIM)

    def body(q_ref, do_ref, cl_ref, dl_ref, k_ref, v_ref, kt_ref, ck_ref,
             dq_ref, dcq_ref, dk_ref, dv_ref, dck_ref,
             st_scr, dp_scr, p_scr, ds_scr, dqt, dk_acc, dv_acc, dck_acc):
        j = pl.program_id(1)

        @pl.when(j == 0)
        def _():
            dqt[...] = jnp.zeros_like(dqt)
            dcq_ref[...] = jnp.zeros_like(dcq_ref)

        dk_acc[...] = jnp.zeros_like(dk_acc)
        dv_acc[...] = jnp.zeros_like(dv_acc)
        dck_acc[...] = jnp.zeros_like(dck_acc)
        kb = k_ref[...]
        vb = v_ref[...]
        behind = (lax.broadcasted_iota(jnp.int32, (STRIP, bq), 1)
                  - lax.broadcasted_iota(jnp.int32, (STRIP, bq), 0))

        def tile(i, diagonal):
            r0 = pl.multiple_of(i * bq, bq)
            qi = q_ref[pl.ds(r0, bq), :]
            doi = do_ref[pl.ds(r0, bq), :]
            st_scr[...] = lax.dot_general(kb, qi, NT_DIMS, preferred_element_type=f32)
            dp_scr[...] = lax.dot_general(vb, doi, NT_DIMS, preferred_element_type=f32)
            cl = cl_ref[i]
            dl = dl_ref[i]
            over_keys = jnp.zeros((STRIP, bq), f32)
            for r in range(0, bk, STRIP):
                keys = slice(r, r + STRIP)
                st = st_scr[keys, :] + cl - _lanes(ck_ref[keys, :], bq)
                if diagonal:
                    st = jnp.where(behind >= r, st, NEG_INF)
                pr = jnp.exp(st)
                ds = pr * (dp_scr[keys, :] - dl)
                over_keys = over_keys + ds
                dck_acc[keys, :] -= jnp.sum(ds, axis=1, keepdims=True)
                p_scr[keys, :] = pr.astype(bf16)
                ds_scr[keys, :] = ds.astype(bf16)
            dcq_ref[i] += jnp.sum(over_keys, axis=0, keepdims=True)
            dv_acc[...] += jnp.dot(p_scr[...], doi, preferred_element_type=f32)
            dk_acc[...] += jnp.dot(ds_scr[...], qi, preferred_element_type=f32)
            dqt[i] += jnp.dot(kt_ref[...], ds_scr[...], preferred_element_type=f32)

        def full_tile(i, carry):
            tile(i, False)
            return carry

        tile(j, True)
        lax.fori_loop(j + 1, n_q, full_tile, 0)
        dk_ref[...] = dk_acc[...].astype(bf16)
        dv_ref[...] = dv_acc[...].astype(bf16)
        for c0 in range(0, bk, LANES):
            keys = slice(c0, c0 + LANES)
            dck_ref[:, keys] = jnp.broadcast_to(dck_acc[keys, :], (LANES, LANES)).T[0:1, :]

        @pl.when(j == n_kv - 1)
        def _():
            def emit(i, carry):
                r0 = pl.multiple_of(i * bq, bq)
                dq_ref[pl.ds(r0, bq), :] = (dqt[i].T * scale).astype(bf16)
                return carry
            lax.fori_loop(0, n_q, emit, 0)

    head_rows = pl.BlockSpec((t, HEAD_DIM), lambda h, j: (0, h))
    head_stat = pl.BlockSpec((None, n_q, 1, bq), lambda h, j: (h, 0, 0, 0))
    kblk = pl.BlockSpec((bk, HEAD_DIM), lambda h, j: (j, h))
    return _pallas(
        comm, body, name=name, grid=(n_heads, n_kv),
        in_specs=[head_rows, head_rows, head_stat, head_stat, kblk, kblk,
                  pl.BlockSpec((None, None, HEAD_DIM, bk), lambda h, j: (h, j, 0, 0)),
                  pl.BlockSpec((None, bk, LANES), lambda h, j: (h, j, 0))],
        out_specs=[head_rows, head_stat, kblk, kblk,
                   pl.BlockSpec((None, None, 1, bk), lambda h, j: (h, j, 0, 0))],
        out_shape=[jax.ShapeDtypeStruct((t, d), bf16),
                   jax.ShapeDtypeStruct((n_heads, n_q, 1, bq), f32),
                   jax.ShapeDtypeStruct((t, d), bf16), jax.ShapeDtypeStruct((t, d), bf16),
                   jax.ShapeDtypeStruct((n_heads, n_kv, 1, bk), f32)],
        scratch_shapes=[pltpu.VMEM((bk, bq), f32), pltpu.VMEM((bk, bq), f32),
                        pltpu.VMEM((bk, bq), bf16), pltpu.VMEM((bk, bq), bf16),
                        pltpu.VMEM((n_q, HEAD_DIM, bq), f32),
                        pltpu.VMEM((bk, HEAD_DIM), f32), pltpu.VMEM((bk, HEAD_DIM), f32),
                        pltpu.VMEM((bk, 1), f32)],
        compiler_params=_params("parallel", "arbitrary"),
    )(q, do, cl_rows, delta_rows, k, v, kt, ckey)


def _fgate_bwd(dlogf, flog, fbias, name):
    t, n = flog.shape
    rows = _tile(t, 640, LANES)

    def body(dl_ref, fl_ref, fb_ref, o_ref, sum_ref):
        i = pl.program_id(0)

        @pl.when(i == 0)
        def _():
            sum_ref[...] = jnp.zeros_like(sum_ref)

        r = i * rows + lax.broadcasted_iota(jnp.int32, (rows, n), 0)
        g = dl_ref[...] * jax.nn.sigmoid(-(fl_ref[...] + fb_ref[...]))
        g = jnp.where(r >= PAD, g, 0.0)
        o_ref[...] = g
        sum_ref[...] += jnp.sum(g, axis=0, keepdims=True)

    blk = pl.BlockSpec((rows, n), lambda i: (i, 0))
    vec = pl.BlockSpec((1, n), lambda i: (0, 0))
    return pl.pallas_call(
        body, name=name, grid=(t // rows,),
        in_specs=[blk, blk, vec], out_specs=[blk, vec],
        out_shape=[jax.ShapeDtypeStruct((t, n), f32), jax.ShapeDtypeStruct((1, n), f32)],
        compiler_params=_params("arbitrary"),
    )(dlogf, flog, fbias)


def _place_shard(ws, idx, name):
    n = len(ws)
    n_l, r, c_n = ws[0].shape
    out_dtype = bf16 if r * c_n > 2 ** 16 else ws[0].dtype
    tr = _tile(r, 512, 16) if r % 16 == 0 else r

    def body(idx_ref, *refs):
        for w_ref, o_ref in zip(refs[:n], refs[n:]):
            o_ref[...] = w_ref[...].astype(out_dtype)

    grid_spec = pltpu.PrefetchScalarGridSpec(
        num_scalar_prefetch=1, grid=(n_l, r // tr),
        in_specs=[pl.BlockSpec((None, tr, c_n), lambda l, i, idx: (l, i, 0))] * n,
        out_specs=[pl.BlockSpec((None, None, tr, c_n), lambda l, i, idx: (idx[0], l, i, 0))] * n)
    return pl.pallas_call(
        body, name=name, grid_spec=grid_spec,
        out_shape=[jax.ShapeDtypeStruct((N_CHIPS, n_l, r, c_n), out_dtype)] * n,
        compiler_params=_params("parallel", "parallel"),
    )(idx, *ws)


def _plan_gather_ici(items):
    def plan(srcs, bufs, news, p):
        out = []
        for name, layer, r2 in items:
            mine = bufs[name].at[p.me, layer, pl.ds(p.c * r2, r2)]
            out += [(mine, mine, (*chip, p.c)) for chip in p.chips]
        return out
    return plan, 3 * len(items)


def _plan_gather_d2d(items):
    def plan(srcs, bufs, news, p):
        out = []
        for name, layer, r2 in items:
            for px, py in p.chips:
                landed = bufs[name].at[2 * px + py, layer, pl.ds(p.c * r2, r2)]
                out.append((landed, landed, p.sib))
        return out
    return plan, 3 * len(items)


def _plan_pair_exchange(names, r2s):
    def plan(srcs, bufs, news, p):
        out = []
        for name, r2 in zip(names, r2s):
            for k, slot in enumerate(p.slots):
                out.append((srcs["G_" + name].at[slot, pl.ds((1 - p.c) * r2, r2)],
                            news["PAIR_" + name].at[k], p.sib))
        return out
    return plan, 4 * len(names)


def _plan_chip_exchange(names):
    def plan(srcs, bufs, news, p):
        out = []
        for name in names:
            for k, chip in enumerate(p.chips):
                out.append((srcs["SEND_" + name].at[k], news["RECV_" + name].at[k], (*chip, p.c)))
        return out
    return plan, 3 * len(names)


def _plan_pair_share(items):
    def plan(srcs, bufs, news, p):
        out = []
        for name, layer, r2 in items:
            mine = bufs["RED_" + name].at[layer, pl.ds(p.c * r2, r2)]
            out.append((mine, mine, p.sib))
        return out
    return plan, len(items)


def _rs_prepare(gs, pairs, idx, name):
    n = len(gs)
    _, r2, c_n = pairs[0].shape
    tr = _tile(r2, 256, 8)
    nb = r2 // tr

    def body(idx_ref, *refs):
        for g_ref, p_ref, o_ref in zip(refs[:n], refs[n:2 * n], refs[2 * n:]):
            o_ref[...] = (g_ref[...] + p_ref[...]).astype(bf16)

    g_spec = pl.BlockSpec((None, tr, c_n), lambda k, i, idx: (idx[k + 1], idx[4] * nb + i, 0))
    p_spec = pl.BlockSpec((None, tr, c_n), lambda k, i, idx: (k + 1, i, 0))
    grid_spec = pltpu.PrefetchScalarGridSpec(
        num_scalar_prefetch=1, grid=(3, nb), in_specs=[g_spec] * n + [p_spec] * n,
        out_specs=[pl.BlockSpec((None, tr, c_n), lambda k, i, idx: (k, i, 0))] * n)
    return pl.pallas_call(
        body, name=name, grid_spec=grid_spec,
        out_shape=[jax.ShapeDtypeStruct((3, r2, c_n), bf16)] * n,
        compiler_params=_params("parallel", "parallel"),
    )(idx, *gs, *pairs)


def _rs_finish(gs, pairs, recvs, idx, layer, n_layers, intos, name):
    n = len(gs)
    _, r2, c_n = pairs[0].shape
    tr = _tile(r2, 256, 8)
    nb = r2 // tr

    def body(idx_ref, *refs):
        for j in range(n):
            g_ref, p_ref, r0_ref, r1_ref, r2_ref = refs[5 * j:5 * j + 5]
            acc = g_ref[...] + p_ref[...]
            acc = acc + r0_ref[...].astype(f32)
            acc = acc + r1_ref[...].astype(f32)
            acc = acc + r2_ref[...].astype(f32)
            refs[len(refs) - n + j][...] = acc

    def rspec(k):
        return pl.BlockSpec((None, tr, c_n), functools.partial(lambda i, idx, kk: (kk, i, 0), kk=k))

    item_specs = [pl.BlockSpec((None, tr, c_n), lambda i, idx: (idx[0], idx[4] * nb + i, 0)),
                  rspec(0), rspec(0), rspec(1), rspec(2)]
    in_specs = item_specs * n
    args = [idx]
    for g, pair, recv in zip(gs, pairs, recvs):
        args += [g, pair, recv, recv, recv]
    aliases = {}
    if intos is not None:
        in_specs = in_specs + [pl.BlockSpec(memory_space=pl.ANY)] * n
        args += list(intos)
        aliases = {1 + 5 * n + j: j for j in range(n)}
    grid_spec = pltpu.PrefetchScalarGridSpec(
        num_scalar_prefetch=1, grid=(nb,), in_specs=in_specs,
        out_specs=[pl.BlockSpec((None, tr, c_n), lambda i, idx: (layer, idx[4] * nb + i, 0))] * n)
    return pl.pallas_call(
        body, name=name, grid_spec=grid_spec,
        out_shape=[jax.ShapeDtypeStruct((n_layers, 2 * r2, c_n), f32)] * n,
        input_output_aliases=aliases,
        compiler_params=_params("parallel"),
    )(*args)


def _adamw(ws, gs, ms, vs, layer, intos, name):
    n = len(ws)
    n_l, r, c_n = ws[0].shape
    tr = _tile(r, 256, 8)

    def body(*refs):
        for j in range(n):
            w_ref, g_ref, m_ref, v_ref = refs[4 * j:4 * j + 4]
            o0 = len(refs) - 4 * n + 4 * j
            d_ref, mo_ref, vo_ref, go_ref = refs[o0:o0 + 4]
            gv = g_ref[...]
            go_ref[...] = gv
            mn = ADAM_B1 * m_ref[...] + (1.0 - ADAM_B1) * gv
            vn = ADAM_B2 * v_ref[...] + (1.0 - ADAM_B2) * (gv * gv)
            m_hat = mn / (1.0 - ADAM_B1 ** ADAM_STEP)
            v_hat = vn / (1.0 - ADAM_B2 ** ADAM_STEP)
            d_ref[...] = -ADAM_LR * (m_hat / (jnp.sqrt(v_hat) + ADAM_EPS) + ADAM_WD * w_ref[...])
            mo_ref[...] = mn
            vo_ref[...] = vn

    blk = pl.BlockSpec((None, tr, c_n), lambda i: (layer, i, 0))
    shp = jax.ShapeDtypeStruct((n_l, r, c_n), f32)
    in_specs = [blk] * (4 * n)
    args = []
    for w, g, m, v in zip(ws, gs, ms, vs):
        args += [w, g, m, v]
    aliases = {}
    if intos is not None:
        in_specs = in_specs + [pl.BlockSpec(memory_space=pl.ANY)] * (4 * n)
        for into in intos:
            args += list(into)
        aliases = {4 * n + j: j for j in range(4 * n)}
    res = pl.pallas_call(
        body, name=name, grid=(r // tr,),
        in_specs=in_specs, out_specs=[blk] * (4 * n), out_shape=[shp] * (4 * n),
        input_output_aliases=aliases,
        compiler_params=_params("parallel"),
    )(*args)
    return [tuple(res[4 * j:4 * j + 4]) for j in range(n)]


def kernel(x, meta, ffn1_wg, ffn1_wu, ffn1_wd, ffn2_wg, ffn2_wu, ffn2_wd, ln_gain, ln_bias, conv_w_in, conv_w, conv_w_out, kv_w, f_bias, attn_w_q, attn_w_o, loss_target, m_meta, m_ffn1_wg, m_ffn1_wu, m_ffn1_wd, m_ffn2_wg, m_ffn2_wu, m_ffn2_wd, m_ln_gain, m_ln_bias, m_conv_w_in, m_conv_w, m_conv_w_out, m_kv_w, m_f_bias, m_attn_w_q, m_attn_w_o, v_meta, v_ffn1_wg, v_ffn1_wu, v_ffn1_wd, v_ffn2_wg, v_ffn2_wu, v_ffn2_wd, v_ln_gain, v_ln_bias, v_conv_w_in, v_conv_w, v_conv_w_out, v_kv_w, v_f_bias, v_attn_w_q, v_attn_w_o):
    seq, d = x.shape[1], x.shape[2]
    t = PAD + N_META + seq
    n_heads = d // HEAD_DIM
    dq = d // N_CHIPS
    n_kv = kv_w.shape[1]
    x2 = x[0]
    target = loss_target[0]

    def rows8(a):
        return jnp.pad(a, ((0, 8 - a.shape[0]), (0, 0)))

    def small_pack(mt, g, b, cw, fb):
        fb_row = jnp.pad(fb, (0, mt.shape[1] - n_heads))[None]
        return jnp.concatenate([mt, rows8(g.reshape(6, -1)), rows8(b.reshape(6, -1)),
                                rows8(cw.reshape(3, -1)), rows8(fb_row)], axis=0)

    w_small = small_pack(meta, ln_gain, ln_bias, conv_w, f_bias)

    cx, cy, c = lax.axis_index("x"), lax.axis_index("y"), lax.axis_index("c")
    idx = jnp.stack([2 * cx + cy, 2 * (1 - cx) + cy, 2 * cx + (1 - cy), 2 * (1 - cx) + (1 - cy), c]
                    ).astype(jnp.int32)

    def tr(a):
        return a.transpose(0, 2, 1)

    w3 = {"wg1": tr(ffn1_wg), "wu1": tr(ffn1_wu), "wd1": ffn1_wd, "wg2": tr(ffn2_wg),
          "wu2": tr(ffn2_wu), "wd2": ffn2_wd, "win": conv_w_in, "wout": conv_w_out, "kv": kv_w[None],
          "wq": attn_w_q, "wo": attn_w_o, "small": w_small[None]}
    transposed = ("wg1", "wu1", "wg2", "wu2")
    buf = {}
    for group in (("wg1", "wu1"), ("small",), ("wd1", "wg2", "wu2", "wd2"), ("win",), ("kv",),
                  ("wout", "wq", "wo")):
        buf.update(zip(group, _place_shard([w3[n] for n in group], idx, "place_shard")))

    def split(item):
        name, layer = item.split(".")
        return name, int(layer)

    def gather_stage(planner, items):
        triples = [(n, l, buf[n].shape[2] // 2) for n, l in map(split, items)]
        plan, n_copies = planner(triples)
        return dict(plan=plan, n=n_copies, bufs={n: buf[n] for n, _, _ in triples})

    def ici(items):
        return gather_stage(_plan_gather_ici, items)

    def d2d(items):
        return gather_stage(_plan_gather_d2d, items)

    def pair_exchange(items):
        r2s = [buf["G_" + it].shape[1] // 2 for it in items]
        plan, n_copies = _plan_pair_exchange(items, r2s)
        news = {"PAIR_" + it: jax.ShapeDtypeStruct((N_CHIPS, r2, buf["G_" + it].shape[2]), f32)
                for it, r2 in zip(items, r2s)}
        return dict(plan=plan, n=n_copies, srcs={"G_" + it: buf["G_" + it] for it in items}, news=news)

    def chip_exchange(items):
        plan, n_copies = _plan_chip_exchange(items)
        srcs = {"SEND_" + it: buf["SEND_" + it] for it in items}
        news = {"RECV_" + it: jax.ShapeDtypeStruct(s.shape, s.dtype)
                for it, s in ((it, buf["SEND_" + it]) for it in items)}
        return dict(plan=plan, n=n_copies, srcs=srcs, news=news)

    def pair_share(items):
        triples = [(n, l, buf["RED_" + n].shape[1] // 2) for n, l in map(split, items)]
        plan, n_copies = _plan_pair_share(triples)
        return dict(plan=plan, n=n_copies, bufs={"RED_" + n: buf["RED_" + n] for n, _, _ in triples})

    def run(fn, *args, stages=(), name=None, **kw):
        comm = _Copies()
        for st in (stages() if callable(stages) else stages):
            comm.add(st["plan"], st["n"], srcs=st.get("srcs"), bufs=st.get("bufs"), news=st.get("news"))
        out = _copy_call(comm, name) if fn is None else fn(*args, comm=comm, **kw)
        buf.update(comm.out_bufs)
        buf.update(comm.out_news)
        return out

    first = ["wg1.0", "wu1.0", "small.0"]
    run(None, stages=[ici(first)], name="gather_first_ici")
    run(None, stages=[d2d(first)], name="gather_first_d2d")
    small = buf["small"].reshape(N_CHIPS, SMALL_ROWS, dq).transpose(1, 0, 2).reshape(SMALL_ROWS, d)
    meta_full = small[:N_META]
    gains = small[16:22].reshape(DEPTH, 3, 1, d)
    biases = small[24:30].reshape(DEPTH, 3, 1, d)
    conv_w_full = small[32:35]
    fb_pad = jnp.pad(f_bias, (0, LANES - n_heads))[None]
    down1 = ["wd1.0"]
    ffn2_l0 = ["win.0", "wout.0", "wg2.0", "wu2.0", "wd2.0", "kv.0"]
    attn_ffn2_l1 = ["wq.0", "wo.0", "wg2.1", "wu2.1", "wd2.1"]
    ffn1_l1 = ["wg1.1", "wu1.1", "wd1.1"]

    meta_pad = jnp.concatenate([jnp.zeros((PAD, d), f32), meta_full], axis=0)
    h0, h0b = run(_embed, meta_pad, x2, "embed", stages=[ici(down1)])
    a1, b1, s1 = run(_ffn_up, h0b, buf["wg1"], buf["wu1"], 0, "ffn_up",
                     stages=lambda: [d2d(down1), ici(ffn2_l0)])
    r1, h1, h1b = run(_down_ln, s1, buf["wd1"], 0, h0, gains[0, 0], biases[0, 0], 0.5, "ffn_down_ln",
                      stages=lambda: [d2d(ffn2_l0), ici(attn_ffn2_l1)])
    n_in = conv_w_in.shape[-1]
    w_in = buf["win"].reshape(N_CHIPS, d, n_in)
    w_out = buf["wout"].reshape(1, 1, d, d)
    p = run(_nn_matmul, h1b, w_in, f32, "conv_in", stages=lambda: [d2d(attn_ffn2_l1), ici(ffn1_l1)])
    z = _conv_fwd(p, conv_w_full, "conv_fwd")
    r2, h2, h2b = run(_down_ln, z[None], w_out, 0, h1, gains[0, 1], biases[0, 1], 1.0, "mix_out_ln",
                      stages=lambda: [d2d(ffn1_l1)])
    wg1, wu1, wd1, wg2, wu2, wd2 = (buf[n] for n in ("wg1", "wu1", "wd1", "wg2", "wu2", "wd2"))
    w_q = buf["wq"].reshape(1, d, d)
    w_o = buf["wo"].reshape(1, 1, d, d)
    kv_full = buf["kv"].reshape(N_CHIPS, d, n_kv).transpose(1, 0, 2).reshape(d, N_CHIPS * n_kv)
    w_k = kv_full[:, :d][None]
    w_v = kv_full[:, d:2 * d][None]
    w_f = jnp.pad(kv_full[:, 2 * d:], ((0, 0), (0, LANES - n_heads)))[None]
    a2, b2, s2 = _ffn_up(h2b, wg2, wu2, 0, "ffn_up")
    r3, h3, h3b = _down_ln(s2, wd2, 0, h2, gains[0, 2], biases[0, 2], 0.5, "ffn_down_ln")
    kk = _nn_matmul(h3b, w_k, bf16, "proj_bf16")
    vv = _nn_matmul(h3b, w_v, bf16, "proj_bf16")
    flog = _nn_matmul(h3b, w_f, f32, "proj_gate")
    cum = _row_scan(flog, "gate_cumsum", fb_pad)
    bk = _tile(t, 640, LANES)
    c_ht = cum[:, :n_heads].T
    c_keys = jnp.where(jnp.arange(t)[None, :] < PAD, 1e30, c_ht)
    cq_rep = jnp.broadcast_to(c_ht[:, :, None], (n_heads, t, LANES))
    ck_rep = jnp.broadcast_to(c_keys[:, :, None], (n_heads, t, LANES))
    ck_rows = c_keys.reshape(n_heads, t // bk, 1, bk)
    a3, b3, s3 = _ffn_up(h3b, wg1, wu1, 1, "ffn_up")
    r4, h4, h4b = _down_ln(s3, wd1, 1, h3, gains[1, 0], biases[1, 0], 0.5, "ffn_down_ln")
    q = _nn_matmul(h4b, w_q, bf16, "proj_q", out_scale=1.0 / math.sqrt(HEAD_DIM))
    o, o32, cl = _attn_fwd(q, kk, vv, cq_rep, ck_rows, "attn_fwd")
    r5, h5, h5b = _down_ln(o[None], w_o, 0, h4, gains[1, 1], biases[1, 1], 1.0, "mix_out_ln")
    a4, b4, s4 = _ffn_up(h5b, wg2, wu2, 1, "ffn_up")
    r6, h6, _ = _down_ln(s4, wd2, 1, h5, gains[1, 2], biases[1, 2], 0.5, "ffn_down_ln")
    dr6, dr6b, dg12, db12, sq = _loss_head(h6, target, r6, gains[1, 2], "loss_head")
    loss_part = 0.5 * sq[0, 0] / d

    m_small = small_pack(m_meta, m_ln_gain, m_ln_bias, m_conv_w, m_f_bias)
    v_small = small_pack(v_meta, v_ln_gain, v_ln_bias, v_conv_w, v_f_bias)
    m3 = {"wg1": tr(m_ffn1_wg), "wu1": tr(m_ffn1_wu), "wd1": m_ffn1_wd, "wg2": tr(m_ffn2_wg),
          "wu2": tr(m_ffn2_wu), "wd2": m_ffn2_wd, "win": m_conv_w_in, "wout": m_conv_w_out,
          "kv": m_kv_w[None], "wq": m_attn_w_q, "wo": m_attn_w_o, "small": m_small[None]}
    v3 = {"wg1": tr(v_ffn1_wg), "wu1": tr(v_ffn1_wu), "wd1": v_ffn1_wd, "wg2": tr(v_ffn2_wg),
          "wu2": tr(v_ffn2_wu), "wd2": v_ffn2_wd, "win": v_conv_w_in, "wout": v_conv_w_out,
          "kv": v_kv_w[None], "wq": v_attn_w_q, "wo": v_attn_w_o, "small": v_small[None]}
    stepped = {}

    def alike(items):
        groups = {}
        for it in items:
            n, l = split(it)
            groups.setdefault((w3[n].shape, l, ("RED_" + n) in buf, n in stepped), []).append(it)
        return groups.values()

    def prepare(items):
        for group in alike(items):
            sends = _rs_prepare([buf["G_" + it] for it in group], [buf["PAIR_" + it] for it in group],
                                idx, "grad_prepare")
            buf.update({"SEND_" + it: s for it, s in zip(group, sends)})

    def finish(items):
        for group in alike(items):
            names = [split(it)[0] for it in group]
            layer = split(group[0])[1]
            intos = [buf["RED_" + n] for n in names] if ("RED_" + names[0]) in buf else None
            reds = _rs_finish([buf["G_" + it] for it in group], [buf["PAIR_" + it] for it in group],
                              [buf["RECV_" + it] for it in group], idx, layer, w3[names[0]].shape[0],
                              intos, "grad_finish")
            buf.update({"RED_" + n: r for n, r in zip(names, reds)})

    def adam(items, grads=None):
        for group in alike(items):
            names = [split(it)[0] for it in group]
            layer = split(group[0])[1]
            gs = [buf["RED_" + n] if grads is None else grads[n] for n in names]
            intos = [stepped[n] for n in names] if names[0] in stepped else None
            res = _adamw([w3[n] for n in names], gs, [m3[n] for n in names], [v3[n] for n in names],
                         layer, intos, "adamw")
            stepped.update(dict(zip(names, res)))

    def ffn_bwd(dr, drb, hb_in, a, b, s, f, layer, on_act=(), after_act=None, on_dwd=(),
                after_dwd=None, on_dx=(), ln=None):
        da, db = run(_ffn_bwd_act, drb, buf["wd" + f], layer, a, b, "ffn_bwd_act", stages=on_act)
        if after_act is not None:
            after_act()
        (buf[f"G_wd{f}.{layer}"],) = run(_tn_matmul, [(s, "stack", drb, "shared")], N_CHIPS, 0.5,
                                         "ffn_dwd", stages=on_dwd)
        if after_dwd is not None:
            after_dwd()
        buf[f"G_wg{f}.{layer}"], buf[f"G_wu{f}.{layer}"] = _tn_matmul(
            [(da, "stack", hb_in, "shared"), (db, "stack", hb_in, "shared")], N_CHIPS, 1.0, "ffn_dwgu")
        return run(_nt_sum, [(da, "stack", buf["wg" + f], layer, True),
                             (db, "stack", buf["wu" + f], layer, True)],
                   dr, ALPHA, N_CHIPS, f32, "ffn_dx", stages=on_dx, ln=ln)

    ffn2_1 = ["wg2.1", "wu2.1", "wd2.1"]
    ffn1_1 = ["wg1.1", "wu1.1", "wd1.1"]
    ffn2_0 = ["wg2.0", "wu2.0", "wd2.0"]
    conv_items = ["wout.0", "win.0"]

    dr5, dr5b, dg11, db11 = ffn_bwd(dr6, dr6b, h5b, a4, b4, s4, "2", 1,
                                    on_dx=lambda: [pair_exchange(ffn2_1)], ln=(r5, gains[1, 1]))
    prepare(ffn2_1)
    (dwo,) = _tn_matmul([(o, "shared", dr5b, "shared")], 1, 1.0, "sq_dw")
    buf["G_wo.0"] = dwo.reshape(N_CHIPS, dq, d)
    do, delta = run(_attn_do, dr5b, w_o[0, 0], o32, bk, "attn_do",
                    stages=lambda: [pair_exchange(["wo.0"])])
    prepare(["wo.0"])
    kt = kk.reshape(t // bk, bk, n_heads, HEAD_DIM).transpose(2, 0, 3, 1)
    dq_att, dc_q, dk, dv, dc_k = run(_attn_bwd, q, kk, vv, kt, do, ck_rep, cl, delta, "attn_bwd",
                                     stages=lambda: [chip_exchange(ffn2_1 + ["wo.0"])])
    finish(ffn2_1 + ["wo.0"])
    dc = (dc_q + dc_k).reshape(n_heads, t)
    (dwq,) = run(_tn_matmul, [(h4b, "shared", dq_att, "shared")], 1, 1.0, "sq_dw",
                 stages=lambda: [pair_share(ffn2_1 + ["wo.0"])])
    buf["G_wq.0"] = dwq.reshape(N_CHIPS, dq, d)
    adam(ffn2_1 + ["wo.0"])
    dr4, dr4b, dg10, db10 = run(_nt_sum, [(dq_att, "cols", w_q, None)], dr5, ALPHA, 1, f32,
                                "sq_dx_res", stages=lambda: [pair_exchange(["wq.0"])],
                                ln=(r4, gains[1, 0]))
    prepare(["wq.0"])
    dh3a = ffn_bwd(dr4, dr4b, h3b, a3, b3, s3, "1", 1,
                   on_act=lambda: [chip_exchange(["wq.0"])],
                   on_dx=lambda: [pair_exchange(ffn1_1)])
    prepare(ffn1_1)
    finish(["wq.0"])
    dc_t = jnp.pad(dc.T, ((0, 0), (0, LANES - n_heads)))
    dlogf = _row_scan(dc_t, "rev_cumsum", reverse=True)
    dfl, dfb_cols = _fgate_bwd(dlogf, flog, fb_pad, "gate_bwd")
    dwk, dwv, dwf = run(_tn_matmul, [(h3b, "shared", g, "shared") for g in (dk, dv, dfl)], 1, 1.0,
                        "kv_dw",
                        stages=lambda: [chip_exchange(ffn1_1), pair_share(["wq.0"])])
    adam(["wq.0"])

    def by_chip(full):
        rows = full.shape[0]
        return full.reshape(rows, N_CHIPS, full.shape[1] // N_CHIPS).transpose(1, 0, 2)

    buf["G_kv.0"] = by_chip(jnp.concatenate([dwk[0], dwv[0], dwf[0][:, :n_heads]], axis=1))
    dr3, dr3b, dg02, db02 = run(
        _nt_sum, [(dk, "cols", w_k, None), (dv, "cols", w_v, None), (dfl, "cols", w_f, None)],
        dh3a, 1.0, 1, f32, "kv_dx", stages=lambda: [pair_exchange(["kv.0"])], ln=(r3, gains[0, 2]))
    prepare(["kv.0"])
    finish(ffn1_1)
    dr2, dr2b, dg01, db01 = ffn_bwd(dr3, dr3b, h2b, a2, b2, s2, "2", 0,
                                    on_act=lambda: [chip_exchange(["kv.0"]), pair_share(ffn1_1)],
                                    after_act=lambda: (adam(ffn1_1), finish(["kv.0"])),
                                    on_dwd=lambda: [pair_share(["kv.0"])],
                                    after_dwd=lambda: adam(["kv.0"]),
                                    on_dx=lambda: [pair_exchange(ffn2_0)], ln=(r2, gains[0, 1]))
    prepare(ffn2_0)
    (dwout,) = _tn_matmul([(z, "shared", dr2b, "shared")], 1, 1.0, "sq_dw")
    buf["G_wout.0"] = dwout.reshape(N_CHIPS, dq, d)
    dz = _nt_sum([(dr2b, "cols", w_out[0], None)], None, 1.0, 1, f32, "sq_dx_f32")
    dp, dconv_w = run(_conv_bwd, dz, p, conv_w_full, "conv_bwd",
                      stages=lambda: [chip_exchange(ffn2_0)])
    (buf["G_win.0"],) = _tn_matmul([(h1b, "shared", dp, "cols")], N_CHIPS, 1.0, "conv_dwin")
    finish(ffn2_0)
    dr1, dr1b, dg00, db00 = run(_nt_sum, [(dp, "cols", w_in, None)], dr2, ALPHA, N_CHIPS, f32,
                                "conv_dx",
                                stages=lambda: [pair_exchange(conv_items), pair_share(ffn2_0)],
                                ln=(r1, gains[0, 0]))
    prepare(conv_items)
    adam(ffn2_0)
    gate_up = ["wg1.0", "wu1.0"]
    da, db = run(_ffn_bwd_act, dr1b, buf["wd1"], 0, a1, b1, "ffn_bwd_act",
                 stages=lambda: [chip_exchange(conv_items)])
    finish(conv_items)
    buf["G_wg1.0"], buf["G_wu1.0"] = run(
        _tn_matmul, [(da, "stack", h0b, "shared"), (db, "stack", h0b, "shared")], N_CHIPS, 1.0,
        "ffn_dwgu", stages=lambda: [pair_share(conv_items)])
    adam(conv_items)
    (buf["G_wd1.0"],) = run(_tn_matmul, [(s1, "stack", dr1b, "shared")], N_CHIPS, 0.5, "ffn_dwd",
                            stages=lambda: [pair_exchange(gate_up)])
    prepare(gate_up)
    dh0 = run(_nt_sum, [(da, "stack", buf["wg1"], 0, True), (db, "stack", buf["wu1"], 0, True)],
              dr1, ALPHA, N_CHIPS, f32, "ffn_dx",
              stages=lambda: [chip_exchange(gate_up), pair_exchange(["wd1.0"])])
    prepare(["wd1.0"])
    finish(gate_up)
    grad_x = dh0[PAD + N_META:][None]
    dmeta = dh0[PAD:PAD + N_META]
    buf["G_small.0"] = by_chip(jnp.concatenate(
        [dmeta, rows8(jnp.concatenate([dg00, dg01, dg02, dg10, dg11, dg12], axis=0)),
         rows8(jnp.concatenate([db00, db01, db02, db10, db11, db12], axis=0)),
         rows8(dconv_w), jnp.zeros((8, d), f32)], axis=0))
    run(None, stages=lambda: [chip_exchange(["wd1.0"]), pair_exchange(["small.0"]), pair_share(gate_up)],
        name="grad_tail_1")
    prepare(["small.0"])
    finish(["wd1.0"])
    adam(gate_up)
    run(None, stages=lambda: [chip_exchange(["small.0"]), pair_share(["wd1.0"])], name="grad_tail_2")
    finish(["small.0"])
    adam(["wd1.0"])
    run(None, stages=lambda: [pair_share(["small.0"])], name="grad_tail_3")

    tail = jnp.zeros((LANES,), f32).at[:n_heads].set(dfb_cols[0, :n_heads]).at[n_heads].set(loss_part)
    tail = lax.psum(tail, ("x", "y", "c"))
    loss = tail[n_heads]
    g_fb = tail[:n_heads]
    g_small = jnp.concatenate([buf["RED_small"][0, :40],
                               rows8(jnp.pad(g_fb, (0, dq - n_heads))[None])], axis=0)
    adam(["small.0"], grads={"small": g_small[None]})

    def unpack(pk):
        return (pk[:16], pk[16:22].reshape(DEPTH, 3, dq), pk[24:30].reshape(DEPTH, 3, dq),
                pk[32:35].reshape(1, 3, dq), pk[40, :n_heads])

    def order(pick):
        mt, g, b, cw, fb = unpack(pick("small")[0])
        big = {n: pick(n) for n in w3 if n != "small"}
        big["kv"] = big["kv"][0]
        for n in transposed:
            big[n] = tr(big[n])
        return [mt, big["wg1"], big["wu1"], big["wd1"], big["wg2"], big["wu2"], big["wd2"], g, b,
                big["win"], cw, big["wout"], big["kv"], fb, big["wq"], big["wo"]]

    return (loss, grad_x, *order(lambda n: stepped[n][3]), *order(lambda n: stepped[n][0]),
            *order(lambda n: stepped[n][1]), *order(lambda n: stepped[n][2]))
```

```python
import functools
import math

import jax
import jax.numpy as jnp
from jax import lax
from jax.experimental import pallas as pl
from jax.experimental.pallas import tpu as pltpu

f32 = jnp.float32
bf16 = jnp.bfloat16

N_META = 16
PAD = 112
HEAD_DIM = 128
DEPTH = 2
LN_EPS = 1e-5
ALPHA = (2 * DEPTH) ** 0.25
NEG_INF = -1e30
N_CHIPS = 4
SMALL_ROWS = 48
LANES = 128

ADAM_LR = 0.001
ADAM_B1 = 0.9
ADAM_B2 = 0.999
ADAM_EPS = 1e-08
ADAM_WD = 0.01
ADAM_STEP = 10

VMEM_LIMIT_BYTES = 56 * 1024 * 1024
ROWS_WIDE = 1664
ROWS_ACC = 1040
ROWS_ACC_LN = 832
ROWS_RESIDENT = 640
ROWS_TN = 2080
ROWS_TN_ONE = 4160
STRIP = 16
MESH = pl.DeviceIdType.MESH

NT_DIMS = (((1,), (1,)), ((), ()))
TN_DIMS = (((0,), (0,)), ((), ()))


def _tile(n, target, mult):
    best = None
    for d in range(mult, min(n, target) + 1, mult):
        if n % d == 0:
            best = d
    assert best is not None, (n, target, mult)
    return best


def _params(*sem):
    return pltpu.CompilerParams(dimension_semantics=sem, vmem_limit_bytes=VMEM_LIMIT_BYTES)


class _Place:
    def __init__(self):
        self.cx, self.cy, self.c = lax.axis_index("x"), lax.axis_index("y"), lax.axis_index("c")
        self.chips = [(1 - self.cx, self.cy), (self.cx, 1 - self.cy), (1 - self.cx, 1 - self.cy)]
        self.me = 2 * self.cx + self.cy
        self.slots = [self.me] + [2 * px + py for px, py in self.chips]
        self.sib = (self.cx, self.cy, 1 - self.c)


class _Copies:
    def __init__(self):
        self.srcs, self.bufs, self.news = {}, {}, {}
        self.plans = []
        self.out_bufs, self.out_news = {}, {}

    def add(self, plan, n_copies, srcs=None, bufs=None, news=None):
        for have, more in ((self.srcs, srcs), (self.bufs, bufs), (self.news, news)):
            for key, val in (more or {}).items():
                assert key not in have or have[key] is val, key
                have[key] = val
        self.plans.append((plan, n_copies))

    def empty(self):
        return not self.plans

    def count(self):
        return sum(n for _, n in self.plans)

    def copies(self, src_refs, buf_refs, new_refs, send, recv):
        place = _Place()
        srcs = dict(zip(self.srcs, src_refs))
        bufs = dict(zip(self.bufs, buf_refs))
        news = dict(zip(self.news, new_refs))
        out = []
        for plan, n_copies in self.plans:
            triples = plan(srcs, bufs, news, place)
            assert len(triples) == n_copies
            for src, dst, dev in triples:
                n = len(out)
                out.append(pltpu.make_async_remote_copy(
                    src_ref=src, dst_ref=dst, send_sem=send.at[n], recv_sem=recv.at[n],
                    device_id=dev, device_id_type=MESH))
        return out

    def land(self, results):
        n_b = len(self.bufs)
        self.out_bufs = dict(zip(self.bufs, results[:n_b]))
        self.out_news = dict(zip(self.news, results[n_b:]))


def _pallas(comm, body, *, name, grid, in_specs, out_specs, out_shape, compiler_params,
            scratch_shapes=(), input_output_aliases=None):
    aliases = dict(input_output_aliases or {})
    if comm is None or comm.empty():
        return pl.pallas_call(body, name=name, grid=grid, in_specs=in_specs, out_specs=out_specs,
                              out_shape=out_shape, scratch_shapes=list(scratch_shapes),
                              input_output_aliases=aliases, compiler_params=compiler_params)
    single = not isinstance(out_shape, (list, tuple))
    out_shapes = [out_shape] if single else list(out_shape)
    out_specs_l = [out_specs] if single else list(out_specs)
    n_in, n_out, n_scr = len(in_specs), len(out_shapes), len(scratch_shapes)
    n_s, n_b, n_n = len(comm.srcs), len(comm.bufs), len(comm.news)
    n_copies = comm.count()

    def wrapped(*refs):
        ins = refs[:n_in]
        src_refs = refs[n_in:n_in + n_s]
        o0 = n_in + n_s + n_b
        outs = refs[o0:o0 + n_out]
        buf_refs = refs[o0 + n_out:o0 + n_out + n_b]
        new_refs = refs[o0 + n_out + n_b:o0 + n_out + n_b + n_n]
        rest = refs[o0 + n_out + n_b + n_n:]
        scratch, (send, recv) = rest[:n_scr], rest[n_scr:]
        ids = [pl.program_id(a) for a in range(len(grid))]
        first = functools.reduce(jnp.logical_and, [i == 0 for i in ids])
        last = functools.reduce(jnp.logical_and, [i == g - 1 for i, g in zip(ids, grid)])

        @pl.when(first)
        def _():
            for cp in comm.copies(src_refs, buf_refs, new_refs, send, recv):
                cp.start()

        body(*ins, *outs, *scratch)

        @pl.when(last)
        def _():
            for cp in comm.copies(src_refs, buf_refs, new_refs, send, recv):
                cp.wait()

    hbm = pl.BlockSpec(memory_space=pl.ANY)
    for j in range(n_b):
        aliases[n_in + n_s + j] = n_out + j
    call = pl.pallas_call(
        wrapped, name=name, grid=grid,
        in_specs=[*in_specs, *([hbm] * (n_s + n_b))],
        out_specs=[*out_specs_l, *([hbm] * (n_b + n_n))],
        out_shape=[*out_shapes,
                   *[jax.ShapeDtypeStruct(a.shape, a.dtype) for a in comm.bufs.values()],
                   *comm.news.values()],
        scratch_shapes=[*scratch_shapes, pltpu.SemaphoreType.DMA((n_copies,)),
                        pltpu.SemaphoreType.DMA((n_copies,))],
        input_output_aliases=aliases, compiler_params=compiler_params)

    def run(*args):
        res = call(*args, *comm.srcs.values(), *comm.bufs.values())
        comm.land(res[n_out:])
        return res[0] if single else res[:n_out]

    return run


def _copy_call(comm, name):
    n_s, n_b, n_n = len(comm.srcs), len(comm.bufs), len(comm.news)
    n_copies = comm.count()

    def body(*refs):
        src_refs = refs[:n_s]
        buf_refs = refs[n_s + n_b:n_s + 2 * n_b]
        new_refs = refs[n_s + 2 * n_b:n_s + 2 * n_b + n_n]
        send, recv = refs[n_s + 2 * n_b + n_n:]
        copies = comm.copies(src_refs, buf_refs, new_refs, send, recv)
        for cp in copies:
            cp.start()
        for cp in copies:
            cp.wait()

    hbm = pl.BlockSpec(memory_space=pl.ANY)
    res = pl.pallas_call(
        body, name=name,
        in_specs=[hbm] * (n_s + n_b), out_specs=[hbm] * (n_b + n_n),
        out_shape=[*[jax.ShapeDtypeStruct(a.shape, a.dtype) for a in comm.bufs.values()],
                   *comm.news.values()],
        input_output_aliases={n_s + j: j for j in range(n_b)},
        scratch_shapes=[pltpu.SemaphoreType.DMA((n_copies,)), pltpu.SemaphoreType.DMA((n_copies,))],
    )(*comm.srcs.values(), *comm.bufs.values())
    comm.land(res)


def _embed(meta_pad, x, name, comm=None):
    seq, d = x.shape
    t = seq + LANES

    def body(m_ref, x_ref, h_ref, hb_ref):
        first = pl.program_id(0) == 0
        v = jnp.where(first, m_ref[...], x_ref[...])
        h_ref[...] = v
        hb_ref[...] = v.astype(bf16)

    return _pallas(
        comm, body, name=name, grid=(t // LANES,),
        in_specs=[pl.BlockSpec((LANES, d), lambda i: (0, 0)),
                  pl.BlockSpec((LANES, d), lambda i: (jnp.maximum(i - 1, 0), 0))],
        out_specs=[pl.BlockSpec((LANES, d), lambda i: (i, 0)),
                   pl.BlockSpec((LANES, d), lambda i: (i, 0))],
        out_shape=[jax.ShapeDtypeStruct((t, d), f32), jax.ShapeDtypeStruct((t, d), bf16)],
        compiler_params=_params("parallel"),
    )(meta_pad, x)


def _nn_matmul(x, w, out_dtype, name, comm=None, out_scale=None):
    t, k = x.shape
    s_n, _, n = w.shape
    assert s_n == 1 or n % LANES == 0
    tm = _tile(t, ROWS_WIDE, 16)

    def body(x_ref, w_ref, o_ref):
        res = jnp.dot(x_ref[...].astype(bf16), w_ref[...], preferred_element_type=f32)
        if out_scale is not None:
            res = res * out_scale
        o_ref[...] = res.astype(o_ref.dtype)

    return _pallas(
        comm, body, name=name, grid=(s_n, t // tm),
        in_specs=[pl.BlockSpec((tm, k), lambda s, i: (i, 0)),
                  pl.BlockSpec((None, k, n), lambda s, i: (s, 0, 0))],
        out_specs=pl.BlockSpec((tm, n), lambda s, i: (i, s)),
        out_shape=jax.ShapeDtypeStruct((t, s_n * n), out_dtype),
        compiler_params=_params("parallel", "parallel"),
    )(x, w)


def _ffn_up(hb, wg, wu, layer, name, comm=None):
    t, d = hb.shape
    s_n, _, n, _ = wg.shape
    tm = _tile(t, ROWS_WIDE, 16)

    def body(x_ref, wg_ref, wu_ref, a_ref, b_ref, s_ref):
        x = x_ref[...]
        a = lax.dot_general(x, wg_ref[...], NT_DIMS, preferred_element_type=f32)
        b = lax.dot_general(x, wu_ref[...], NT_DIMS, preferred_element_type=f32)
        a_ref[...] = a.astype(bf16)
        b_ref[...] = b.astype(bf16)
        s_ref[...] = (a * jax.nn.sigmoid(a) * b).astype(bf16)

    wspec = pl.BlockSpec((None, None, n, d), lambda s, i: (s, layer, 0, 0))
    ospec = pl.BlockSpec((None, tm, n), lambda s, i: (s, i, 0))
    return _pallas(
        comm, body, name=name, grid=(s_n, t // tm),
        in_specs=[pl.BlockSpec((tm, d), lambda s, i: (i, 0)), wspec, wspec],
        out_specs=[ospec, ospec, ospec],
        out_shape=[jax.ShapeDtypeStruct((s_n, t, n), bf16)] * 3,
        compiler_params=_params("parallel", "parallel"),
    )(hb, wg, wu)


def _down_ln(x, w, layer, hprev, gain, bias, beta, name, comm=None):
    s_n, t, k = x.shape
    d = w.shape[-1]
    tm = _tile(t, ROWS_RESIDENT, 16)

    def body(x_ref, w_ref, h_ref, g_ref, b_ref, r_out, h_out, hb_out):
        acc = jnp.dot(x_ref[0], w_ref[0], preferred_element_type=f32)
        for s in range(1, s_n):
            acc = acc + jnp.dot(x_ref[s], w_ref[s], preferred_element_type=f32)
        r = ALPHA * h_ref[...] + beta * acc
        mu = jnp.mean(r, axis=-1, keepdims=True)
        xc = r - mu
        var = jnp.mean(xc * xc, axis=-1, keepdims=True)
        y = xc * lax.rsqrt(var + LN_EPS) * g_ref[...] + b_ref[...]
        r_out[...] = r
        h_out[...] = y
        hb_out[...] = y.astype(bf16)

    row = pl.BlockSpec((tm, d), lambda i: (i, 0))
    vec = pl.BlockSpec((1, d), lambda i: (0, 0))
    return _pallas(
        comm, body, name=name, grid=(t // tm,),
        in_specs=[pl.BlockSpec((s_n, tm, k), lambda i: (0, i, 0)),
                  pl.BlockSpec((s_n, None, k, d), lambda i: (0, layer, 0, 0),
                               pipeline_mode=pl.Buffered(1)),
                  row, vec, vec],
        out_specs=[row, row, row],
        out_shape=[jax.ShapeDtypeStruct((t, d), f32), jax.ShapeDtypeStruct((t, d), f32),
                   jax.ShapeDtypeStruct((t, d), bf16)],
        compiler_params=_params("parallel"),
    )(x, w, hprev, gain, bias)


def _conv_fwd(p, conv_w, name):
    t, d3 = p.shape
    d = d3 // 3
    tm = _tile(t, 320, 8)
    hb = tm // 8

    def body(p_ref, prev_ref, w_ref, z_ref):
        i = pl.program_id(0)
        rows = i * tm - 8 + lax.broadcasted_iota(jnp.int32, (tm + 8, 1), 0)
        cg = jnp.concatenate([prev_ref[:, d:2 * d], p_ref[:, d:2 * d]], axis=0)
        val = jnp.concatenate([prev_ref[:, 2 * d:], p_ref[:, 2 * d:]], axis=0)
        u = jnp.where(rows >= PAD, cg * val, 0.0)
        y = (w_ref[2:3, :] * u + w_ref[1:2, :] * pltpu.roll(u, 1, 0)
             + w_ref[0:1, :] * pltpu.roll(u, 2, 0))
        z_ref[...] = (p_ref[:, :d] * y[8:]).astype(bf16)

    return pl.pallas_call(
        body, name=name, grid=(t // tm,),
        in_specs=[pl.BlockSpec((tm, d3), lambda i: (i, 0)),
                  pl.BlockSpec((8, d3), lambda i: (jnp.maximum(i * hb - 1, 0), 0)),
                  pl.BlockSpec((3, d), lambda i: (0, 0))],
        out_specs=pl.BlockSpec((tm, d), lambda i: (i, 0)),
        out_shape=jax.ShapeDtypeStruct((t, d), bf16),
        compiler_params=_params("parallel"),
    )(p, p, conv_w)


def _row_scan(x, name, fbias=None, reverse=False):
    t, n = x.shape
    blk = _tile(t, 640, LANES)
    n_blk = t // blk
    gate = fbias is not None

    def body(*refs):
        if gate:
            x_ref, fb_ref, o_ref, carry = refs
        else:
            x_ref, o_ref, carry = refs
        i = pl.program_id(0)

        @pl.when(i == 0)
        def _():
            carry[...] = jnp.zeros_like(carry)

        v = x_ref[...]
        r = lax.broadcasted_iota(jnp.int32, (blk, n), 0)
        if gate:
            v = v + fb_ref[...]
            v = jnp.minimum(v, 0.0) - jnp.log1p(jnp.exp(-jnp.abs(v)))
            v = jnp.where(i * blk + r >= PAD, v, 0.0)
        sh = 1
        while sh < blk:
            if reverse:
                v = v + jnp.where(r < blk - sh, pltpu.roll(v, blk - sh, 0), 0.0)
            else:
                v = v + jnp.where(r >= sh, pltpu.roll(v, sh, 0), 0.0)
            sh *= 2
        v = v + carry[...]
        o_ref[...] = v
        carry[...] = o_ref[0:1, :] if reverse else o_ref[blk - 1:blk, :]

    order = (lambda i: (n_blk - 1 - i, 0)) if reverse else (lambda i: (i, 0))
    in_specs = [pl.BlockSpec((blk, n), order)]
    args = [x]
    if gate:
        in_specs.append(pl.BlockSpec((1, n), lambda i: (0, 0)))
        args.append(fbias)
    return pl.pallas_call(
        body, name=name, grid=(n_blk,),
        in_specs=in_specs,
        out_specs=pl.BlockSpec((blk, n), order),
        out_shape=jax.ShapeDtypeStruct((t, n), f32),
        scratch_shapes=[pltpu.VMEM((1, n), f32)],
        compiler_params=_params("arbitrary"),
    )(*args)


def _lanes(x, n):
    return jnp.concatenate([x] * (n // LANES), axis=1)


def _attn_fwd(q, k, v, cq_rep, ck_rows, name):
    t, d = q.shape
    n_heads = d // HEAD_DIM
    bk = ck_rows.shape[-1]
    bq = bk

    def lane_fold(x, op):
        out = x[:, :LANES]
        for c0 in range(LANES, bk, LANES):
            out = op(out, x[:, c0:c0 + LANES])
        return out

    def body(q_ref, k_ref, v_ref, cq_ref, ck_ref, o_ref, o32_ref, cl_ref,
             s_scr, p_scr, m_scr, l_scr, red_scr, acc_scr):
        i = pl.program_id(1)
        m_scr[...] = jnp.full_like(m_scr, NEG_INF)
        l_scr[...] = jnp.zeros_like(l_scr)
        acc_scr[...] = jnp.zeros_like(acc_scr)
        qb = q_ref[...]
        ahead = (lax.broadcasted_iota(jnp.int32, (STRIP, bk), 1)
                 - lax.broadcasted_iota(jnp.int32, (STRIP, bk), 0))

        def tile(j, diagonal):
            k0 = pl.multiple_of(j * bk, bk)
            s_scr[...] = lax.dot_general(qb, k_ref[pl.ds(k0, bk), :], NT_DIMS,
                                         preferred_element_type=f32)
            ck = ck_ref[j]
            for r in range(0, bq, STRIP):
                rows = slice(r, r + STRIP)
                s = s_scr[rows, :] + _lanes(cq_ref[rows, :], bk) - ck
                if diagonal:
                    s = jnp.where(ahead <= r, s, NEG_INF)
                s_scr[rows, :] = s
                red_scr[rows, :] = lane_fold(s, jnp.maximum)
            m_old = m_scr[...]
            m_new = jnp.maximum(m_old, jnp.broadcast_to(
                jnp.max(red_scr[...], axis=1, keepdims=True), (bq, LANES)))
            a = jnp.exp(m_old - m_new)
            m_scr[...] = m_new
            for r in range(0, bq, STRIP):
                rows = slice(r, r + STRIP)
                pr = jnp.exp(s_scr[rows, :] - _lanes(m_scr[rows, :], bk))
                red_scr[rows, :] = lane_fold(pr, jnp.add)
                p_scr[rows, :] = pr.astype(bf16)
            l_scr[...] = a * l_scr[...] + jnp.broadcast_to(
                jnp.sum(red_scr[...], axis=1, keepdims=True), (bq, LANES))
            acc_scr[...] = a * acc_scr[...] + jnp.dot(
                p_scr[...], v_ref[pl.ds(k0, bk), :], preferred_element_type=f32)

        def full_tile(j, carry):
            tile(j, False)
            return carry

        lax.fori_loop(0, i, full_tile, 0)
        tile(i, True)
        out = acc_scr[...] / l_scr[...]
        o_ref[...] = out.astype(bf16)
        o32_ref[...] = out
        red_scr[...] = cq_ref[...] - (m_scr[...] + jnp.log(l_scr[...]))
        for c0 in range(0, bq, LANES):
            cl_ref[:, c0:c0 + LANES] = red_scr[c0:c0 + LANES, :].T[0:1, :]

    qblk = pl.BlockSpec((bq, HEAD_DIM), lambda h, i: (i, h))
    head_rows = pl.BlockSpec((t, HEAD_DIM), lambda h, i: (0, h))
    rep = pl.BlockSpec((None, bq, LANES), lambda h, i: (h, i, 0))
    col = pltpu.VMEM((bq, LANES), f32)
    return pl.pallas_call(
        body, name=name, grid=(n_heads, t // bq),
        in_specs=[qblk, head_rows, head_rows, rep,
                  pl.BlockSpec((None, t // bk, 1, bk), lambda h, i: (h, 0, 0, 0))],
        out_specs=[qblk, qblk, pl.BlockSpec((None, None, 1, bq), lambda h, i: (h, i, 0, 0))],
        out_shape=[jax.ShapeDtypeStruct((t, d), bf16), jax.ShapeDtypeStruct((t, d), f32),
                   jax.ShapeDtypeStruct((n_heads, t // bq, 1, bq), f32)],
        scratch_shapes=[pltpu.VMEM((bq, bk), f32), pltpu.VMEM((bq, bk), bf16), col, col, col,
                        pltpu.VMEM((bq, HEAD_DIM), f32)],
        compiler_params=_params("parallel", "parallel"),
    )(q, k, v, cq_rep, ck_rows)


def _loss_head(h, target, r, gain, name):
    t, d = h.shape

    def body(h_ref, t_ref, r_ref, g_ref, dr_ref, drb_ref, dg_ref, db_ref, loss_ref):
        i = pl.program_id(0)

        @pl.when(i == 0)
        def _():
            loss_ref[...] = jnp.zeros_like(loss_ref)
            dg_ref[...] = jnp.zeros_like(dg_ref)
            db_ref[...] = jnp.zeros_like(db_ref)

        diff = jnp.where(i >= 1, h_ref[...] - t_ref[...], 0.0)
        loss_ref[...] += jnp.sum(diff * diff)
        dr, dg, db = _ln_bwd_rows(diff * (1.0 / d), r_ref[...], g_ref[...])
        dr_ref[...] = dr
        drb_ref[...] = dr.astype(bf16)
        dg_ref[...] += dg
        db_ref[...] += db

    row = pl.BlockSpec((LANES, d), lambda i: (i, 0))
    vec = pl.BlockSpec((1, d), lambda i: (0, 0))
    return pl.pallas_call(
        body, name=name, grid=(t // LANES,),
        in_specs=[row, pl.BlockSpec((LANES, d), lambda i: (jnp.maximum(i - 1, 0), 0)), row, vec],
        out_specs=[row, row, vec, vec, pl.BlockSpec((1, LANES), lambda i: (0, 0))],
        out_shape=[jax.ShapeDtypeStruct((t, d), f32), jax.ShapeDtypeStruct((t, d), bf16),
                   jax.ShapeDtypeStruct((1, d), f32), jax.ShapeDtypeStruct((1, d), f32),
                   jax.ShapeDtypeStruct((1, LANES), f32)],
        compiler_params=_params("arbitrary"),
    )(h, target, r, gain)


def _ln_bwd_rows(dy, rr, gain):
    mu = jnp.mean(rr, axis=-1, keepdims=True)
    xc = rr - mu
    var = jnp.mean(xc * xc, axis=-1, keepdims=True)
    rstd = lax.rsqrt(var + LN_EPS)
    xhat = xc * rstd
    dxh = dy * gain
    m1 = jnp.mean(dxh, axis=-1, keepdims=True)
    m2 = jnp.mean(dxh * xhat, axis=-1, keepdims=True)
    dr = rstd * (dxh - m1 - xhat * m2)
    return dr, jnp.sum(dy * xhat, axis=0, keepdims=True), jnp.sum(dy, axis=0, keepdims=True)


def _ffn_bwd_act(drb, wd, layer, a, b, name, comm=None):
    t, d = drb.shape
    s_n, _, n = a.shape
    tm = _tile(t, ROWS_WIDE, 16)

    def body(dr_ref, w_ref, a_ref, b_ref, da_ref, db_ref):
        ds = 0.5 * lax.dot_general(dr_ref[...], w_ref[...], NT_DIMS, preferred_element_type=f32)
        da_ref[...], db_ref[...] = _swiglu_bwd(ds, a_ref, b_ref)

    act = pl.BlockSpec((None, tm, n), lambda s, i: (s, i, 0))
    return _pallas(
        comm, body, name=name, grid=(s_n, t // tm),
        in_specs=[pl.BlockSpec((tm, d), lambda s, i: (i, 0)),
                  pl.BlockSpec((None, None, n, d), lambda s, i: (s, layer, 0, 0)), act, act],
        out_specs=[act, act],
        out_shape=[jax.ShapeDtypeStruct((s_n, t, n), bf16), jax.ShapeDtypeStruct((s_n, t, n), bf16)],
        compiler_params=_params("parallel", "parallel"),
    )(drb, wd, a, b)


def _swiglu_bwd(ds, a_ref, b_ref):
    av = a_ref[...].astype(f32)
    sig = jax.nn.sigmoid(av)
    da = ds * b_ref[...].astype(f32) * (sig * (1.0 + av * (1.0 - sig)))
    return da.astype(bf16), (ds * (av * sig)).astype(bf16)


def _act_spec(mode, tt, k, t_first):
    def fix(fn):
        return (lambda i, s: fn(s, i)) if t_first else fn
    if mode == "shared":
        return pl.BlockSpec((tt, k), fix(lambda s, i: (i, 0)))
    if mode == "cols":
        return pl.BlockSpec((tt, k), fix(lambda s, i: (i, s)))
    assert mode == "stack"
    return pl.BlockSpec((None, tt, k), fix(lambda s, i: (s, i, 0)))


def _act_width(arr, mode, s_n):
    return arr.shape[-1] // s_n if mode == "cols" else arr.shape[-1]


def _tn_matmul(pairs, s_n, scale, name, comm=None):
    t = pairs[0][0].shape[-2]
    tt = _tile(t, ROWS_TN_ONE if len(pairs) == 1 else ROWS_TN, 16)
    n_t = t // tt
    arrays, specs, where = [], [], []
    for x, xmode, y, ymode in pairs:
        pos = []
        for arr, mode in ((x, xmode), (y, ymode)):
            hit = [j for j, a in enumerate(arrays) if a is arr]
            if not hit:
                arrays.append(arr)
                specs.append(_act_spec(mode, tt, _act_width(arr, mode, s_n), False))
                hit = [len(arrays) - 1]
            pos.append(hit[0])
        where.append(pos)
    widths = [(_act_width(x, xm, s_n), _act_width(y, ym, s_n)) for x, xm, y, ym in pairs]
    n_a = len(arrays)

    def body(*refs):
        i = pl.program_id(1)
        for (px, py), o_ref in zip(where, refs[n_a:]):
            part = lax.dot_general(refs[px][...].astype(bf16), refs[py][...].astype(bf16), TN_DIMS,
                                   preferred_element_type=f32)

            @pl.when(i == 0)
            def _():
                o_ref[...] = part

            @pl.when(i > 0)
            def _():
                o_ref[...] += part

            if scale != 1.0:
                @pl.when(i == n_t - 1)
                def _():
                    o_ref[...] = o_ref[...] * scale

    return _pallas(
        comm, body, name=name, grid=(s_n, n_t),
        in_specs=specs,
        out_specs=[pl.BlockSpec((None, kx, ky), lambda s, i: (s, 0, 0)) for kx, ky in widths],
        out_shape=[jax.ShapeDtypeStruct((s_n, kx, ky), f32) for kx, ky in widths],
        compiler_params=_params("parallel", "arbitrary"),
    )(*arrays)


def _nt_sum(pairs, base, base_scale, s_n, out_dtype, name, comm=None, ln=None):
    t = pairs[0][0].shape[-2]
    pairs = [(*pr, False)[:5] for pr in pairs]
    d = pairs[0][2].shape[-1] if pairs[0][4] else pairs[0][2].shape[-2]
    tm = _tile(t, ROWS_ACC if ln is None else ROWS_ACC_LN, 16)
    n_p = len(pairs)
    has_base = base is not None
    flipped = [pr[4] for pr in pairs]
    n_out = 1 if ln is None else 4

    def body(*refs):
        dy_refs = refs[0:2 * n_p:2]
        w_refs = refs[1:2 * n_p:2]
        rest = refs[2 * n_p:]
        base_ref = rest[0] if has_base else None
        o_refs, acc = rest[-1 - n_out:-1], rest[-1]
        s = pl.program_id(1)

        if ln is not None:
            @pl.when(jnp.logical_and(pl.program_id(0) == 0, s == 0))
            def _():
                o_refs[2][...] = jnp.zeros_like(o_refs[2])
                o_refs[3][...] = jnp.zeros_like(o_refs[3])

        @pl.when(s == 0)
        def _():
            acc[...] = jnp.zeros_like(acc)

        tot = None
        for dy_ref, w_ref, flip in zip(dy_refs, w_refs, flipped):
            dyv = dy_ref[...].astype(bf16)
            if flip:
                part = jnp.dot(dyv, w_ref[...], preferred_element_type=f32)
            else:
                part = lax.dot_general(dyv, w_ref[...], NT_DIMS, preferred_element_type=f32)
            tot = part if tot is None else tot + part
        acc[...] += tot

        @pl.when(s == s_n - 1)
        def _():
            res = acc[...]
            if has_base:
                res = base_scale * base_ref[...] + res
            if ln is None:
                o_refs[0][...] = res.astype(o_refs[0].dtype)
            else:
                r_ref, g_ref = rest[-7], rest[-6]
                dr, dg, db = _ln_bwd_rows(res, r_ref[...], g_ref[...])
                o_refs[0][...] = dr
                o_refs[1][...] = dr.astype(bf16)
                o_refs[2][...] += dg
                o_refs[3][...] += db

    in_specs, args = [], []
    for dy, mode, w, layer, flip in pairs:
        k = _act_width(dy, mode, s_n)
        in_specs.append(_act_spec(mode, tm, k, True))
        wshape = (k, d) if flip else (d, k)
        if layer is None:
            in_specs.append(pl.BlockSpec((None, *wshape), lambda i, s: (s, 0, 0)))
        else:
            in_specs.append(pl.BlockSpec((None, None, *wshape),
                                         functools.partial(lambda i, s, l: (s, l, 0, 0), l=layer)))
        args += [dy, w]
    row = pl.BlockSpec((tm, d), lambda i, s: (i, 0))
    vec = pl.BlockSpec((1, d), lambda i, s: (0, 0))
    if has_base:
        in_specs.append(row)
        args.append(base)
    if ln is None:
        out_specs = row
        out_shape = jax.ShapeDtypeStruct((t, d), out_dtype)
    else:
        in_specs += [row, vec]
        args += list(ln)
        out_specs = [row, row, vec, vec]
        out_shape = [jax.ShapeDtypeStruct((t, d), f32), jax.ShapeDtypeStruct((t, d), bf16),
                     jax.ShapeDtypeStruct((1, d), f32), jax.ShapeDtypeStruct((1, d), f32)]
    return _pallas(
        comm, body, name=name, grid=(t // tm, s_n),
        in_specs=in_specs, out_specs=out_specs, out_shape=out_shape,
        scratch_shapes=[pltpu.VMEM((tm, d), f32)],
        compiler_params=_params("parallel" if ln is None else "arbitrary", "arbitrary"),
    )(*args)


def _conv_bwd(dz, p, conv_w, name, comm=None):
    t, d3 = p.shape
    d = d3 // 3
    tm = _tile(t, 320, 8)
    hb = tm // 8
    last8 = t // 8 - 1
    n_ext = tm + 8

    def body(dz_ref, dzn_ref, p_ref, pp_ref, pn_ref, w_ref, dp_ref, dw_ref):
        i = pl.program_id(0)

        @pl.when(i == 0)
        def _():
            dw_ref[...] = jnp.zeros_like(dw_ref)

        w0, w1, w2 = w_ref[0:1, :], w_ref[1:2, :], w_ref[2:3, :]
        rows_u = i * tm - 8 + lax.broadcasted_iota(jnp.int32, (n_ext, 1), 0)
        cg = jnp.concatenate([pp_ref[:, d:2 * d], p_ref[:, d:2 * d]], axis=0)
        val = jnp.concatenate([pp_ref[:, 2 * d:], p_ref[:, 2 * d:]], axis=0)
        u = jnp.where(rows_u >= PAD, cg * val, 0.0)
        u1 = pltpu.roll(u, 1, 0)
        u2 = pltpu.roll(u, 2, 0)
        y = (w2 * u + w1 * u1 + w0 * u2)[8:]
        dzv = dz_ref[...]
        bg = p_ref[:, :d]
        rows_n = (i + 1) * tm + lax.broadcasted_iota(jnp.int32, (8, 1), 0)
        dy_main = dzv * bg
        dy_next = jnp.where(rows_n < t, dzn_ref[...] * pn_ref[:, :d], 0.0)
        dye = jnp.concatenate([dy_main, dy_next], axis=0)
        du = (w2 * dye + w1 * pltpu.roll(dye, n_ext - 1, 0)
              + w0 * pltpu.roll(dye, n_ext - 2, 0))[:tm]
        du = jnp.where(rows_u[8:] >= PAD, du, 0.0)
        dp_ref[:, :d] = (dzv * y).astype(bf16)
        dp_ref[:, d:2 * d] = (du * val[8:]).astype(bf16)
        dp_ref[:, 2 * d:] = (du * cg[8:]).astype(bf16)
        dw_ref[0:1, :] += jnp.sum(dy_main * u2[8:], axis=0, keepdims=True)
        dw_ref[1:2, :] += jnp.sum(dy_main * u1[8:], axis=0, keepdims=True)
        dw_ref[2:3, :] += jnp.sum(dy_main * u[8:], axis=0, keepdims=True)

    nxt = lambda i: (jnp.minimum((i + 1) * hb, last8), 0)
    return _pallas(
        comm, body, name=name, grid=(t // tm,),
        in_specs=[pl.BlockSpec((tm, d), lambda i: (i, 0)),
                  pl.BlockSpec((8, d), nxt),
                  pl.BlockSpec((tm, d3), lambda i: (i, 0)),
                  pl.BlockSpec((8, d3), lambda i: (jnp.maximum(i * hb - 1, 0), 0)),
                  pl.BlockSpec((8, d3), nxt),
                  pl.BlockSpec((3, d), lambda i: (0, 0))],
        out_specs=[pl.BlockSpec((tm, d3), lambda i: (i, 0)),
                   pl.BlockSpec((3, d), lambda i: (0, 0))],
        out_shape=[jax.ShapeDtypeStruct((t, d3), bf16), jax.ShapeDtypeStruct((3, d), f32)],
        compiler_params=_params("arbitrary"),
    )(dz, dz, p, p, p, conv_w)


def _attn_do(drb, w_o, o, bq, name, comm=None):
    t, d = drb.shape
    n_heads = d // HEAD_DIM

    def body(dr_ref, w_ref, o_ref, do_ref, delta_ref):
        do = lax.dot_general(dr_ref[...], w_ref[...], NT_DIMS, preferred_element_type=f32).astype(bf16)
        do_ref[...] = do
        prod = o_ref[...] * do.astype(f32)
        for h in range(n_heads):
            for c0 in range(0, bq, LANES):
                blk = prod[c0:c0 + LANES, h * HEAD_DIM:(h + 1) * HEAD_DIM]
                delta_ref[h, :, c0:c0 + LANES] = jnp.sum(blk.T, axis=0, keepdims=True)

    row = pl.BlockSpec((bq, d), lambda i: (i, 0))
    return _pallas(
        comm, body, name=name, grid=(t // bq,),
        in_specs=[row, pl.BlockSpec((d, d), lambda i: (0, 0)), row],
        out_specs=[row, pl.BlockSpec((n_heads, None, 1, bq), lambda i: (0, i, 0, 0))],
        out_shape=[jax.ShapeDtypeStruct((t, d), bf16),
                   jax.ShapeDtypeStruct((n_heads, t // bq, 1, bq), f32)],
        compiler_params=_params("parallel"),
    )(drb, w_o, o)


def _attn_bwd(q, k, v, kt, do, ckey, cl_rows, delta_rows, name, comm=None):
    t, d = q.shape
    n_heads = d // HEAD_DIM
    bk = kt.shape[-1]
    bq = bk
    n_kv = t // bk
    n_q = t // bq
    scale = 1.0 / math.sqrt(HEAD_DIM)

    def body(q_ref, do_ref, cl_ref, dl_ref, k_ref, v_ref, kt_ref, ck_ref,
             dq_ref, dcq_ref, dk_ref, dv_ref, dck_ref,
             st_scr, dp_scr, p_scr, ds_scr, dqt, dk_acc, dv_acc, dck_acc):
        j = pl.program_id(1)

        @pl.when(j == 0)
        def _():
            dqt[...] = jnp.zeros_like(dqt)
            dcq_ref[...] = jnp.zeros_like(dcq_ref)

        dk_acc[...] = jnp.zeros_like(dk_acc)
        dv_acc[...] = jnp.zeros_like(dv_acc)
        dck_acc[...] = jnp.zeros_like(dck_acc)
        kb = k_ref[...]
        vb = v_ref[...]
        behind = (lax.broadcasted_iota(jnp.int32, (STRIP, bq), 1)
                  - lax.broadcasted_iota(jnp.int32, (STRIP, bq), 0))

        def tile(i, diagonal):
            r0 = pl.multiple_of(i * bq, bq)
            qi = q_ref[pl.ds(r0, bq), :]
            doi = do_ref[pl.ds(r0, bq), :]
            st_scr[...] = lax.dot_general(kb, qi, NT_DIMS, preferred_element_type=f32)
            dp_scr[...] = lax.dot_general(vb, doi, NT_DIMS, preferred_element_type=f32)
            cl = cl_ref[i]
            dl = dl_ref[i]
            over_keys = jnp.zeros((STRIP, bq), f32)
            for r in range(0, bk, STRIP):
                keys = slice(r, r + STRIP)
                st = st_scr[keys, :] + cl - _lanes(ck_ref[keys, :], bq)
                if diagonal:
                    st = jnp.where(behind >= r, st, NEG_INF)
                pr = jnp.exp(st)
                ds = pr * (dp_scr[keys, :] - dl)
                over_keys = over_keys + ds
                dck_acc[keys, :] -= jnp.sum(ds, axis=1, keepdims=True)
                p_scr[keys, :] = pr.astype(bf16)
                ds_scr[keys, :] = ds.astype(bf16)
            dcq_ref[i] += jnp.sum(over_keys, axis=0, keepdims=True)
            dv_acc[...] += jnp.dot(p_scr[...], doi, preferred_element_type=f32)
            dk_acc[...] += jnp.dot(ds_scr[...], qi, preferred_element_type=f32)
            dqt[i] += jnp.dot(kt_ref[...], ds_scr[...], preferred_element_type=f32)

        def full_tile(i, carry):
            tile(i, False)
            return carry

        tile(j, True)
        lax.fori_loop(j + 1, n_q, full_tile, 0)
        dk_ref[...] = dk_acc[...].astype(bf16)
        dv_ref[...] = dv_acc[...].astype(bf16)
        for c0 in range(0, bk, LANES):
            keys = slice(c0, c0 + LANES)
            dck_ref[:, keys] = jnp.broadcast_to(dck_acc[keys, :], (LANES, LANES)).T[0:1, :]

        @pl.when(j == n_kv - 1)
        def _():
            def emit(i, carry):
                r0 = pl.multiple_of(i * bq, bq)
                dq_ref[pl.ds(r0, bq), :] = (dqt[i].T * scale).astype(bf16)
                return carry
            lax.fori_loop(0, n_q, emit, 0)

    head_rows = pl.BlockSpec((t, HEAD_DIM), lambda h, j: (0, h))
    head_stat = pl.BlockSpec((None, n_q, 1, bq), lambda h, j: (h, 0, 0, 0))
    kblk = pl.BlockSpec((bk, HEAD_DIM), lambda h, j: (j, h))
    return _pallas(
        comm, body, name=name, grid=(n_heads, n_kv),
        in_specs=[head_rows, head_rows, head_stat, head_stat, kblk, kblk,
                  pl.BlockSpec((None, None, HEAD_DIM, bk), lambda h, j: (h, j, 0, 0)),
                  pl.BlockSpec((None, bk, LANES), lambda h, j: (h, j, 0))],
        out_specs=[head_rows, head_stat, kblk, kblk,
                   pl.BlockSpec((None, None, 1, bk), lambda h, j: (h, j, 0, 0))],
        out_shape=[jax.ShapeDtypeStruct((t, d), bf16),
                   jax.ShapeDtypeStruct((n_heads, n_q, 1, bq), f32),
                   jax.ShapeDtypeStruct((t, d), bf16), jax.ShapeDtypeStruct((t, d), bf16),
                   jax.ShapeDtypeStruct((n_heads, n_kv, 1, bk), f32)],
        scratch_shapes=[pltpu.VMEM((bk, bq), f32), pltpu.VMEM((bk, bq), f32),
                        pltpu.VMEM((bk, bq), bf16), pltpu.VMEM((bk, bq), bf16),
                        pltpu.VMEM((n_q, HEAD_DIM, bq), f32),
                        pltpu.VMEM((bk, HEAD_DIM), f32), pltpu.VMEM((bk, HEAD_DIM), f32),
                        pltpu.VMEM((bk, 1), f32)],
        compiler_params=_params("parallel", "arbitrary"),
    )(q, do, cl_rows, delta_rows, k, v, kt, ckey)


def _fgate_bwd(dlogf, flog, fbias, name):
    t, n = flog.shape
    rows = _tile(t, 640, LANES)

    def body(dl_ref, fl_ref, fb_ref, o_ref, sum_ref):
        i = pl.program_id(0)

        @pl.when(i == 0)
        def _():
            sum_ref[...] = jnp.zeros_like(sum_ref)

        r = i * rows + lax.broadcasted_iota(jnp.int32, (rows, n), 0)
        g = dl_ref[...] * jax.nn.sigmoid(-(fl_ref[...] + fb_ref[...]))
        g = jnp.where(r >= PAD, g, 0.0)
        o_ref[...] = g
        sum_ref[...] += jnp.sum(g, axis=0, keepdims=True)

    blk = pl.BlockSpec((rows, n), lambda i: (i, 0))
    vec = pl.BlockSpec((1, n), lambda i: (0, 0))
    return pl.pallas_call(
        body, name=name, grid=(t // rows,),
        in_specs=[blk, blk, vec], out_specs=[blk, vec],
        out_shape=[jax.ShapeDtypeStruct((t, n), f32), jax.ShapeDtypeStruct((1, n), f32)],
        compiler_params=_params("arbitrary"),
    )(dlogf, flog, fbias)


def _place_shard(ws, idx, name):
    n = len(ws)
    n_l, r, c_n = ws[0].shape
    out_dtype = bf16 if r * c_n > 2 ** 16 else ws[0].dtype
    tr = _tile(r, 512, 16) if r % 16 == 0 else r

    def body(idx_ref, *refs):
        for w_ref, o_ref in zip(refs[:n], refs[n:]):
            o_ref[...] = w_ref[...].astype(out_dtype)

    grid_spec = pltpu.PrefetchScalarGridSpec(
        num_scalar_prefetch=1, grid=(n_l, r // tr),
        in_specs=[pl.BlockSpec((None, tr, c_n), lambda l, i, idx: (l, i, 0))] * n,
        out_specs=[pl.BlockSpec((None, None, tr, c_n), lambda l, i, idx: (idx[0], l, i, 0))] * n)
    return pl.pallas_call(
        body, name=name, grid_spec=grid_spec,
        out_shape=[jax.ShapeDtypeStruct((N_CHIPS, n_l, r, c_n), out_dtype)] * n,
        compiler_params=_params("parallel", "parallel"),
    )(idx, *ws)


def _plan_gather_ici(items):
    def plan(srcs, bufs, news, p):
        out = []
        for name, layer, r2 in items:
            mine = bufs[name].at[p.me, layer, pl.ds(p.c * r2, r2)]
            out += [(mine, mine, (*chip, p.c)) for chip in p.chips]
        return out
    return plan, 3 * len(items)


def _plan_gather_d2d(items):
    def plan(srcs, bufs, news, p):
        out = []
        for name, layer, r2 in items:
            for px, py in p.chips:
                landed = bufs[name].at[2 * px + py, layer, pl.ds(p.c * r2, r2)]
                out.append((landed, landed, p.sib))
        return out
    return plan, 3 * len(items)


def _plan_pair_exchange(names, r2s):
    def plan(srcs, bufs, news, p):
        out = []
        for name, r2 in zip(names, r2s):
            for k, slot in enumerate(p.slots):
                out.append((srcs["G_" + name].at[slot, pl.ds((1 - p.c) * r2, r2)],
                            news["PAIR_" + name].at[k], p.sib))
        return out
    return plan, 4 * len(names)


def _plan_chip_exchange(names):
    def plan(srcs, bufs, news, p):
        out = []
        for name in names:
            for k, chip in enumerate(p.chips):
                out.append((srcs["SEND_" + name].at[k], news["RECV_" + name].at[k], (*chip, p.c)))
        return out
    return plan, 3 * len(names)


def _plan_pair_share(items):
    def plan(srcs, bufs, news, p):
        out = []
        for name, layer, r2 in items:
            mine = bufs["RED_" + name].at[layer, pl.ds(p.c * r2, r2)]
            out.append((mine, mine, p.sib))
        return out
    return plan, len(items)


def _rs_prepare(gs, pairs, idx, name):
    n = len(gs)
    _, r2, c_n = pairs[0].shape
    tr = _tile(r2, 256, 8)
    nb = r2 // tr

    def body(idx_ref, *refs):
        for g_ref, p_ref, o_ref in zip(refs[:n], refs[n:2 * n], refs[2 * n:]):
            o_ref[...] = (g_ref[...] + p_ref[...]).astype(bf16)

    g_spec = pl.BlockSpec((None, tr, c_n), lambda k, i, idx: (idx[k + 1], idx[4] * nb + i, 0))
    p_spec = pl.BlockSpec((None, tr, c_n), lambda k, i, idx: (k + 1, i, 0))
    grid_spec = pltpu.PrefetchScalarGridSpec(
        num_scalar_prefetch=1, grid=(3, nb), in_specs=[g_spec] * n + [p_spec] * n,
        out_specs=[pl.BlockSpec((None, tr, c_n), lambda k, i, idx: (k, i, 0))] * n)
    return pl.pallas_call(
        body, name=name, grid_spec=grid_spec,
        out_shape=[jax.ShapeDtypeStruct((3, r2, c_n), bf16)] * n,
        compiler_params=_params("parallel", "parallel"),
    )(idx, *gs, *pairs)


def _rs_finish(gs, pairs, recvs, idx, layer, n_layers, intos, name):
    n = len(gs)
    _, r2, c_n = pairs[0].shape
    tr = _tile(r2, 256, 8)
    nb = r2 // tr

    def body(idx_ref, *refs):
        for j in range(n):
            g_ref, p_ref, r0_ref, r1_ref, r2_ref = refs[5 * j:5 * j + 5]
            acc = g_ref[...] + p_ref[...]
            acc = acc + r0_ref[...].astype(f32)
            acc = acc + r1_ref[...].astype(f32)
            acc = acc + r2_ref[...].astype(f32)
            refs[len(refs) - n + j][...] = acc

    def rspec(k):
        return pl.BlockSpec((None, tr, c_n), functools.partial(lambda i, idx, kk: (kk, i, 0), kk=k))

    item_specs = [pl.BlockSpec((None, tr, c_n), lambda i, idx: (idx[0], idx[4] * nb + i, 0)),
                  rspec(0), rspec(0), rspec(1), rspec(2)]
    in_specs = item_specs * n
    args = [idx]
    for g, pair, recv in zip(gs, pairs, recvs):
        args += [g, pair, recv, recv, recv]
    aliases = {}
    if intos is not None:
        in_specs = in_specs + [pl.BlockSpec(memory_space=pl.ANY)] * n
        args += list(intos)
        aliases = {1 + 5 * n + j: j for j in range(n)}
    grid_spec = pltpu.PrefetchScalarGridSpec(
        num_scalar_prefetch=1, grid=(nb,), in_specs=in_specs,
        out_specs=[pl.BlockSpec((None, tr, c_n), lambda i, idx: (layer, idx[4] * nb + i, 0))] * n)
    return pl.pallas_call(
        body, name=name, grid_spec=grid_spec,
        out_shape=[jax.ShapeDtypeStruct((n_layers, 2 * r2, c_n), f32)] * n,
        input_output_aliases=aliases,
        compiler_params=_params("parallel"),
    )(*args)


def _adamw(ws, gs, ms, vs, layer, intos, name):
    n = len(ws)
    n_l, r, c_n = ws[0].shape
    tr = _tile(r, 256, 8)

    def body(*refs):
        for j in range(n):
            w_ref, g_ref, m_ref, v_ref = refs[4 * j:4 * j + 4]
            o0 = len(refs) - 4 * n + 4 * j
            d_ref, mo_ref, vo_ref, go_ref = refs[o0:o0 + 4]
            gv = g_ref[...]
            go_ref[...] = gv
            mn = ADAM_B1 * m_ref[...] + (1.0 - ADAM_B1) * gv
            vn = ADAM_B2 * v_ref[...] + (1.0 - ADAM_B2) * (gv * gv)
            m_hat = mn / (1.0 - ADAM_B1 ** ADAM_STEP)
            v_hat = vn / (1.0 - ADAM_B2 ** ADAM_STEP)
            d_ref[...] = -ADAM_LR * (m_hat / (jnp.sqrt(v_hat) + ADAM_EPS) + ADAM_WD * w_ref[...])
            mo_ref[...] = mn
            vo_ref[...] = vn

    blk = pl.BlockSpec((None, tr, c_n), lambda i: (layer, i, 0))
    shp = jax.ShapeDtypeStruct((n_l, r, c_n), f32)
    in_specs = [blk] * (4 * n)
    args = []
    for w, g, m, v in zip(ws, gs, ms, vs):
        args += [w, g, m, v]
    aliases = {}
    if intos is not None:
        in_specs = in_specs + [pl.BlockSpec(memory_space=pl.ANY)] * (4 * n)
        for into in intos:
            args += list(into)
        aliases = {4 * n + j: j for j in range(4 * n)}
    res = pl.pallas_call(
        body, name=name, grid=(r // tr,),
        in_specs=in_specs, out_specs=[blk] * (4 * n), out_shape=[shp] * (4 * n),
        input_output_aliases=aliases,
        compiler_params=_params("parallel"),
    )(*args)
    return [tuple(res[4 * j:4 * j + 4]) for j in range(n)]


def kernel(x, meta, ffn1_wg, ffn1_wu, ffn1_wd, ffn2_wg, ffn2_wu, ffn2_wd, ln_gain, ln_bias, conv_w_in, conv_w, conv_w_out, kv_w, f_bias, attn_w_q, attn_w_o, loss_target, m_meta, m_ffn1_wg, m_ffn1_wu, m_ffn1_wd, m_ffn2_wg, m_ffn2_wu, m_ffn2_wd, m_ln_gain, m_ln_bias, m_conv_w_in, m_conv_w, m_conv_w_out, m_kv_w, m_f_bias, m_attn_w_q, m_attn_w_o, v_meta, v_ffn1_wg, v_ffn1_wu, v_ffn1_wd, v_ffn2_wg, v_ffn2_wu, v_ffn2_wd, v_ln_gain, v_ln_bias, v_conv_w_in, v_conv_w, v_conv_w_out, v_kv_w, v_f_bias, v_attn_w_q, v_attn_w_o):
    seq, d = x.shape[1], x.shape[2]
    t = PAD + N_META + seq
    n_heads = d // HEAD_DIM
    dq = d // N_CHIPS
    n_kv = kv_w.shape[1]
    x2 = x[0]
    target = loss_target[0]

    def rows8(a):
        return jnp.pad(a, ((0, 8 - a.shape[0]), (0, 0)))

    def small_pack(mt, g, b, cw, fb):
        fb_row = jnp.pad(fb, (0, mt.shape[1] - n_heads))[None]
        return jnp.concatenate([mt, rows8(g.reshape(6, -1)), rows8(b.reshape(6, -1)),
                                rows8(cw.reshape(3, -1)), rows8(fb_row)], axis=0)

    w_small = small_pack(meta, ln_gain, ln_bias, conv_w, f_bias)

    cx, cy, c = lax.axis_index("x"), lax.axis_index("y"), lax.axis_index("c")
    idx = jnp.stack([2 * cx + cy, 2 * (1 - cx) + cy, 2 * cx + (1 - cy), 2 * (1 - cx) + (1 - cy), c]
                    ).astype(jnp.int32)

    def tr(a):
        return a.transpose(0, 2, 1)

    w3 = {"wg1": tr(ffn1_wg), "wu1": tr(ffn1_wu), "wd1": ffn1_wd, "wg2": tr(ffn2_wg),
          "wu2": tr(ffn2_wu), "wd2": ffn2_wd, "win": conv_w_in, "wout": conv_w_out, "kv": kv_w[None],
          "wq": attn_w_q, "wo": attn_w_o, "small": w_small[None]}
    transposed = ("wg1", "wu1", "wg2", "wu2")
    buf = {}
    for group in (("wg1", "wu1"), ("small",), ("wd1", "wg2", "wu2", "wd2"), ("win",), ("kv",),
                  ("wout", "wq", "wo")):
        buf.update(zip(group, _place_shard([w3[n] for n in group], idx, "place_shard")))

    def split(item):
        name, layer = item.split(".")
        return name, int(layer)

    def gather_stage(planner, items):
        triples = [(n, l, buf[n].shape[2] // 2) for n, l in map(split, items)]
        plan, n_copies = planner(triples)
        return dict(plan=plan, n=n_copies, bufs={n: buf[n] for n, _, _ in triples})

    def ici(items):
        return gather_stage(_plan_gather_ici, items)

    def d2d(items):
        return gather_stage(_plan_gather_d2d, items)

    def pair_exchange(items):
        r2s = [buf["G_" + it].shape[1] // 2 for it in items]
        plan, n_copies = _plan_pair_exchange(items, r2s)
        news = {"PAIR_" + it: jax.ShapeDtypeStruct((N_CHIPS, r2, buf["G_" + it].shape[2]), f32)
                for it, r2 in zip(items, r2s)}
        return dict(plan=plan, n=n_copies, srcs={"G_" + it: buf["G_" + it] for it in items}, news=news)

    def chip_exchange(items):
        plan, n_copies = _plan_chip_exchange(items)
        srcs = {"SEND_" + it: buf["SEND_" + it] for it in items}
        news = {"RECV_" + it: jax.ShapeDtypeStruct(s.shape, s.dtype)
                for it, s in ((it, buf["SEND_" + it]) for it in items)}
        return dict(plan=plan, n=n_copies, srcs=srcs, news=news)

    def pair_share(items):
        triples = [(n, l, buf["RED_" + n].shape[1] // 2) for n, l in map(split, items)]
        plan, n_copies = _plan_pair_share(triples)
        return dict(plan=plan, n=n_copies, bufs={"RED_" + n: buf["RED_" + n] for n, _, _ in triples})

    def run(fn, *args, stages=(), name=None, **kw):
        comm = _Copies()
        for st in (stages() if callable(stages) else stages):
            comm.add(st["plan"], st["n"], srcs=st.get("srcs"), bufs=st.get("bufs"), news=st.get("news"))
        out = _copy_call(comm, name) if fn is None else fn(*args, comm=comm, **kw)
        buf.update(comm.out_bufs)
        buf.update(comm.out_news)
        return out

    first = ["wg1.0", "wu1.0", "small.0"]
    run(None, stages=[ici(first)], name="gather_first_ici")
    run(None, stages=[d2d(first)], name="gather_first_d2d")
    small = buf["small"].reshape(N_CHIPS, SMALL_ROWS, dq).transpose(1, 0, 2).reshape(SMALL_ROWS, d)
    meta_full = small[:N_META]
    gains = small[16:22].reshape(DEPTH, 3, 1, d)
    biases = small[24:30].reshape(DEPTH, 3, 1, d)
    conv_w_full = small[32:35]
    fb_pad = jnp.pad(f_bias, (0, LANES - n_heads))[None]
    down1 = ["wd1.0"]
    ffn2_l0 = ["win.0", "wout.0", "wg2.0", "wu2.0", "wd2.0", "kv.0"]
    attn_ffn2_l1 = ["wq.0", "wo.0", "wg2.1", "wu2.1", "wd2.1"]
    ffn1_l1 = ["wg1.1", "wu1.1", "wd1.1"]

    meta_pad = jnp.concatenate([jnp.zeros((PAD, d), f32), meta_full], axis=0)
    h0, h0b = run(_embed, meta_pad, x2, "embed", stages=[ici(down1)])
    a1, b1, s1 = run(_ffn_up, h0b, buf["wg1"], buf["wu1"], 0, "ffn_up",
                     stages=lambda: [d2d(down1), ici(ffn2_l0)])
    r1, h1, h1b = run(_down_ln, s1, buf["wd1"], 0, h0, gains[0, 0], biases[0, 0], 0.5, "ffn_down_ln",
                      stages=lambda: [d2d(ffn2_l0), ici(attn_ffn2_l1)])
    n_in = conv_w_in.shape[-1]
    w_in = buf["win"].reshape(N_CHIPS, d, n_in)
    w_out = buf["wout"].reshape(1, 1, d, d)
    p = run(_nn_matmul, h1b, w_in, f32, "conv_in", stages=lambda: [d2d(attn_ffn2_l1), ici(ffn1_l1)])
    z = _conv_fwd(p, conv_w_full, "conv_fwd")
    r2, h2, h2b = run(_down_ln, z[None], w_out, 0, h1, gains[0, 1], biases[0, 1], 1.0, "mix_out_ln",
                      stages=lambda: [d2d(ffn1_l1)])
    wg1, wu1, wd1, wg2, wu2, wd2 = (buf[n] for n in ("wg1", "wu1", "wd1", "wg2", "wu2", "wd2"))
    w_q = buf["wq"].reshape(1, d, d)
    w_o = buf["wo"].reshape(1, 1, d, d)
    kv_full = buf["kv"].reshape(N_CHIPS, d, n_kv).transpose(1, 0, 2).reshape(d, N_CHIPS * n_kv)
    w_k = kv_full[:, :d][None]
    w_v = kv_full[:, d:2 * d][None]
    w_f = jnp.pad(kv_full[:, 2 * d:], ((0, 0), (0, LANES - n_heads)))[None]
    a2, b2, s2 = _ffn_up(h2b, wg2, wu2, 0, "ffn_up")
    r3, h3, h3b = _down_ln(s2, wd2, 0, h2, gains[0, 2], biases[0, 2], 0.5, "ffn_down_ln")
    kk = _nn_matmul(h3b, w_k, bf16, "proj_bf16")
    vv = _nn_matmul(h3b, w_v, bf16, "proj_bf16")
    flog = _nn_matmul(h3b, w_f, f32, "proj_gate")
    cum = _row_scan(flog, "gate_cumsum", fb_pad)
    bk = _tile(t, 640, LANES)
    c_ht = cum[:, :n_heads].T
    c_keys = jnp.where(jnp.arange(t)[None, :] < PAD, 1e30, c_ht)
    cq_rep = jnp.broadcast_to(c_ht[:, :, None], (n_heads, t, LANES))
    ck_rep = jnp.broadcast_to(c_keys[:, :, None], (n_heads, t, LANES))
    ck_rows = c_keys.reshape(n_heads, t // bk, 1, bk)
    a3, b3, s3 = _ffn_up(h3b, wg1, wu1, 1, "ffn_up")
    r4, h4, h4b = _down_ln(s3, wd1, 1, h3, gains[1, 0], biases[1, 0], 0.5, "ffn_down_ln")
    q = _nn_matmul(h4b, w_q, bf16, "proj_q", out_scale=1.0 / math.sqrt(HEAD_DIM))
    o, o32, cl = _attn_fwd(q, kk, vv, cq_rep, ck_rows, "attn_fwd")
    r5, h5, h5b = _down_ln(o[None], w_o, 0, h4, gains[1, 1], biases[1, 1], 1.0, "mix_out_ln")
    a4, b4, s4 = _ffn_up(h5b, wg2, wu2, 1, "ffn_up")
    r6, h6, _ = _down_ln(s4, wd2, 1, h5, gains[1, 2], biases[1, 2], 0.5, "ffn_down_ln")
    dr6, dr6b, dg12, db12, sq = _loss_head(h6, target, r6, gains[1, 2], "loss_head")
    loss_part = 0.5 * sq[0, 0] / d

    m_small = small_pack(m_meta, m_ln_gain, m_ln_bias, m_conv_w, m_f_bias)
    v_small = small_pack(v_meta, v_ln_gain, v_ln_bias, v_conv_w, v_f_bias)
    m3 = {"wg1": tr(m_ffn1_wg), "wu1": tr(m_ffn1_wu), "wd1": m_ffn1_wd, "wg2": tr(m_ffn2_wg),
          "wu2": tr(m_ffn2_wu), "wd2": m_ffn2_wd, "win": m_conv_w_in, "wout": m_conv_w_out,
          "kv": m_kv_w[None], "wq": m_attn_w_q, "wo": m_attn_w_o, "small": m_small[None]}
    v3 = {"wg1": tr(v_ffn1_wg), "wu1": tr(v_ffn1_wu), "wd1": v_ffn1_wd, "wg2": tr(v_ffn2_wg),
          "wu2": tr(v_ffn2_wu), "wd2": v_ffn2_wd, "win": v_conv_w_in, "wout": v_conv_w_out,
          "kv": v_kv_w[None], "wq": v_attn_w_q, "wo": v_attn_w_o, "small": v_small[None]}
    stepped = {}

    def alike(items):
        groups = {}
        for it in items:
            n, l = split(it)
            groups.setdefault((w3[n].shape, l, ("RED_" + n) in buf, n in stepped), []).append(it)
        return groups.values()

    def prepare(items):
        for group in alike(items):
            sends = _rs_prepare([buf["G_" + it] for it in group], [buf["PAIR_" + it] for it in group],
                                idx, "grad_prepare")
            buf.update({"SEND_" + it: s for it, s in zip(group, sends)})

    def finish(items):
        for group in alike(items):
            names = [split(it)[0] for it in group]
            layer = split(group[0])[1]
            intos = [buf["RED_" + n] for n in names] if ("RED_" + names[0]) in buf else None
            reds = _rs_finish([buf["G_" + it] for it in group], [buf["PAIR_" + it] for it in group],
                              [buf["RECV_" + it] for it in group], idx, layer, w3[names[0]].shape[0],
                              intos, "grad_finish")
            buf.update({"RED_" + n: r for n, r in zip(names, reds)})

    def adam(items, grads=None):
        for group in alike(items):
            names = [split(it)[0] for it in group]
            layer = split(group[0])[1]
            gs = [buf["RED_" + n] if grads is None else grads[n] for n in names]
            intos = [stepped[n] for n in names] if names[0] in stepped else None
            res = _adamw([w3[n] for n in names], gs, [m3[n] for n in names], [v3[n] for n in names],
                         layer, intos, "adamw")
            stepped.update(dict(zip(names, res)))

    def ffn_bwd(dr, drb, hb_in, a, b, s, f, layer, on_act=(), after_act=None, on_dwd=(),
                after_dwd=None, on_dx=(), ln=None):
        da, db = run(_ffn_bwd_act, drb, buf["wd" + f], layer, a, b, "ffn_bwd_act", stages=on_act)
        if after_act is not None:
            after_act()
        (buf[f"G_wd{f}.{layer}"],) = run(_tn_matmul, [(s, "stack", drb, "shared")], N_CHIPS, 0.5,
                                         "ffn_dwd", stages=on_dwd)
        if after_dwd is not None:
            after_dwd()
        buf[f"G_wg{f}.{layer}"], buf[f"G_wu{f}.{layer}"] = _tn_matmul(
            [(da, "stack", hb_in, "shared"), (db, "stack", hb_in, "shared")], N_CHIPS, 1.0, "ffn_dwgu")
        return run(_nt_sum, [(da, "stack", buf["wg" + f], layer, True),
                             (db, "stack", buf["wu" + f], layer, True)],
                   dr, ALPHA, N_CHIPS, f32, "ffn_dx", stages=on_dx, ln=ln)

    ffn2_1 = ["wg2.1", "wu2.1", "wd2.1"]
    ffn1_1 = ["wg1.1", "wu1.1", "wd1.1"]
    ffn2_0 = ["wg2.0", "wu2.0", "wd2.0"]
    conv_items = ["wout.0", "win.0"]

    dr5, dr5b, dg11, db11 = ffn_bwd(dr6, dr6b, h5b, a4, b4, s4, "2", 1,
                                    on_dx=lambda: [pair_exchange(ffn2_1)], ln=(r5, gains[1, 1]))
    prepare(ffn2_1)
    (dwo,) = _tn_matmul([(o, "shared", dr5b, "shared")], 1, 1.0, "sq_dw")
    buf["G_wo.0"] = dwo.reshape(N_CHIPS, dq, d)
    do, delta = run(_attn_do, dr5b, w_o[0, 0], o32, bk, "attn_do",
                    stages=lambda: [pair_exchange(["wo.0"])])
    prepare(["wo.0"])
    kt = kk.reshape(t // bk, bk, n_heads, HEAD_DIM).transpose(2, 0, 3, 1)
    dq_att, dc_q, dk, dv, dc_k = run(_attn_bwd, q, kk, vv, kt, do, ck_rep, cl, delta, "attn_bwd",
                                     stages=lambda: [chip_exchange(ffn2_1 + ["wo.0"])])
    finish(ffn2_1 + ["wo.0"])
    dc = (dc_q + dc_k).reshape(n_heads, t)
    (dwq,) = run(_tn_matmul, [(h4b, "shared", dq_att, "shared")], 1, 1.0, "sq_dw",
                 stages=lambda: [pair_share(ffn2_1 + ["wo.0"])])
    buf["G_wq.0"] = dwq.reshape(N_CHIPS, dq, d)
    adam(ffn2_1 + ["wo.0"])
    dr4, dr4b, dg10, db10 = run(_nt_sum, [(dq_att, "cols", w_q, None)], dr5, ALPHA, 1, f32,
                                "sq_dx_res", stages=lambda: [pair_exchange(["wq.0"])],
                                ln=(r4, gains[1, 0]))
    prepare(["wq.0"])
    dh3a = ffn_bwd(dr4, dr4b, h3b, a3, b3, s3, "1", 1,
                   on_act=lambda: [chip_exchange(["wq.0"])],
                   on_dx=lambda: [pair_exchange(ffn1_1)])
    prepare(ffn1_1)
    finish(["wq.0"])
    dc_t = jnp.pad(dc.T, ((0, 0), (0, LANES - n_heads)))
    dlogf = _row_scan(dc_t, "rev_cumsum", reverse=True)
    dfl, dfb_cols = _fgate_bwd(dlogf, flog, fb_pad, "gate_bwd")
    dwk, dwv, dwf = run(_tn_matmul, [(h3b, "shared", g, "shared") for g in (dk, dv, dfl)], 1, 1.0,
                        "kv_dw",
                        stages=lambda: [chip_exchange(ffn1_1), pair_share(["wq.0"])])
    adam(["wq.0"])

    def by_chip(full):
        rows = full.shape[0]
        return full.reshape(rows, N_CHIPS, full.shape[1] // N_CHIPS).transpose(1, 0, 2)

    buf["G_kv.0"] = by_chip(jnp.concatenate([dwk[0], dwv[0], dwf[0][:, :n_heads]], axis=1))
    dr3, dr3b, dg02, db02 = run(
        _nt_sum, [(dk, "cols", w_k, None), (dv, "cols", w_v, None), (dfl, "cols", w_f, None)],
        dh3a, 1.0, 1, f32, "kv_dx", stages=lambda: [pair_exchange(["kv.0"])], ln=(r3, gains[0, 2]))
    prepare(["kv.0"])
    finish(ffn1_1)
    dr2, dr2b, dg01, db01 = ffn_bwd(dr3, dr3b, h2b, a2, b2, s2, "2", 0,
                                    on_act=lambda: [chip_exchange(["kv.0"]), pair_share(ffn1_1)],
                                    after_act=lambda: (adam(ffn1_1), finish(["kv.0"])),
                                    on_dwd=lambda: [pair_share(["kv.0"])],
                                    after_dwd=lambda: adam(["kv.0"]),
                                    on_dx=lambda: [pair_exchange(ffn2_0)], ln=(r2, gains[0, 1]))
    prepare(ffn2_0)
    (dwout,) = _tn_matmul([(z, "shared", dr2b, "shared")], 1, 1.0, "sq_dw")
    buf["G_wout.0"] = dwout.reshape(N_CHIPS, dq, d)
    dz = _nt_sum([(dr2b, "cols", w_out[0], None)], None, 1.0, 1, f32, "sq_dx_f32")
    dp, dconv_w = run(_conv_bwd, dz, p, conv_w_full, "conv_bwd",
                      stages=lambda: [chip_exchange(ffn2_0)])
    (buf["G_win.0"],) = _tn_matmul([(h1b, "shared", dp, "cols")], N_CHIPS, 1.0, "conv_dwin")
    finish(ffn2_0)
    dr1, dr1b, dg00, db00 = run(_nt_sum, [(dp, "cols", w_in, None)], dr2, ALPHA, N_CHIPS, f32,
                                "conv_dx",
                                stages=lambda: [pair_exchange(conv_items), pair_share(ffn2_0)],
                                ln=(r1, gains[0, 0]))
    prepare(conv_items)
    adam(ffn2_0)
    gate_up = ["wg1.0", "wu1.0"]
    da, db = run(_ffn_bwd_act, dr1b, buf["wd1"], 0, a1, b1, "ffn_bwd_act",
                 stages=lambda: [chip_exchange(conv_items)])
    finish(conv_items)
    buf["G_wg1.0"], buf["G_wu1.0"] = run(
        _tn_matmul, [(da, "stack", h0b, "shared"), (db, "stack", h0b, "shared")], N_CHIPS, 1.0,
        "ffn_dwgu", stages=lambda: [pair_share(conv_items)])
    adam(conv_items)
    (buf["G_wd1.0"],) = run(_tn_matmul, [(s1, "stack", dr1b, "shared")], N_CHIPS, 0.5, "ffn_dwd",
                            stages=lambda: [pair_exchange(gate_up)])
    prepare(gate_up)
    dh0 = run(_nt_sum, [(da, "stack", buf["wg1"], 0, True), (db, "stack", buf["wu1"], 0, True)],
              dr1, ALPHA, N_CHIPS, f32, "ffn_dx",
              stages=lambda: [chip_exchange(gate_up), pair_exchange(["wd1.0"])])
    prepare(["wd1.0"])
    finish(gate_up)
    grad_x = dh0[PAD + N_META:][None]
    dmeta = dh0[PAD:PAD + N_META]
    buf["G_small.0"] = by_chip(jnp.concatenate(
        [dmeta, rows8(jnp.concatenate([dg00, dg01, dg02, dg10, dg11, dg12], axis=0)),
         rows8(jnp.concatenate([db00, db01, db02, db10, db11, db12], axis=0)),
         rows8(dconv_w), jnp.zeros((8, d), f32)], axis=0))
    run(None, stages=lambda: [chip_exchange(["wd1.0"]), pair_exchange(["small.0"]), pair_share(gate_up)],
        name="grad_tail_1")
    prepare(["small.0"])
    finish(["wd1.0"])
    adam(gate_up)
    run(None, stages=lambda: [chip_exchange(["small.0"]), pair_share(["wd1.0"])], name="grad_tail_2")
    finish(["small.0"])
    adam(["wd1.0"])
    run(None, stages=lambda: [pair_share(["small.0"])], name="grad_tail_3")

    tail = jnp.zeros((LANES,), f32).at[:n_heads].set(dfb_cols[0, :n_heads]).at[n_heads].set(loss_part)
    tail = lax.psum(tail, ("x", "y", "c"))
    loss = tail[n_heads]
    g_fb = tail[:n_heads]
    g_small = jnp.concatenate([buf["RED_small"][0, :40],
                               rows8(jnp.pad(g_fb, (0, dq - n_heads))[None])], axis=0)
    adam(["small.0"], grads={"small": g_small[None]})

    def unpack(pk):
        return (pk[:16], pk[16:22].reshape(DEPTH, 3, dq), pk[24:30].reshape(DEPTH, 3, dq),
                pk[32:35].reshape(1, 3, dq), pk[40, :n_heads])

    def order(pick):
        mt, g, b, cw, fb = unpack(pick("small")[0])
        big = {n: pick(n) for n in w3 if n != "small"}
        big["kv"] = big["kv"][0]
        for n in transposed:
            big[n] = tr(big[n])
        return [mt, big["wg1"], big["wu1"], big["wd1"], big["wg2"], big["wu2"], big["wd2"], g, b,
                big["win"], cw, big["wout"], big["kv"], fb, big["wq"], big["wo"]]

    return (loss, grad_x, *order(lambda n: stepped[n][3]), *order(lambda n: stepped[n][0]),
            *order(lambda n: stepped[n][1]), *order(lambda n: stepped[n][2]))
```

```python
import functools
import math

import jax
import jax.numpy as jnp
from jax import lax
from jax.experimental import pallas as pl
from jax.experimental.pallas import tpu as pltpu

f32 = jnp.float32
bf16 = jnp.bfloat16

N_META = 16
PAD = 112
HEAD_DIM = 128
DEPTH = 2
LN_EPS = 1e-5
ALPHA = (2 * DEPTH) ** 0.25
NEG_INF = -1e30
N_CHIPS = 4
SMALL_ROWS = 48
LANES = 128

ADAM_LR = 0.001
ADAM_B1 = 0.9
ADAM_B2 = 0.999
ADAM_EPS = 1e-08
ADAM_WD = 0.01
ADAM_STEP = 10

VMEM_LIMIT_BYTES = 56 * 1024 * 1024
ROWS_WIDE = 1664
ROWS_ACC = 1040
ROWS_ACC_LN = 832
ROWS_RESIDENT = 640
ROWS_TN = 2080
ROWS_TN_ONE = 4160
STRIP = 16
MESH = pl.DeviceIdType.MESH

NT_DIMS = (((1,), (1,)), ((), ()))
TN_DIMS = (((0,), (0,)), ((), ()))


def _tile(n, target, mult):
    best = None
    for d in range(mult, min(n, target) + 1, mult):
        if n % d == 0:
            best = d
    assert best is not None, (n, target, mult)
    return best


def _params(*sem):
    return pltpu.CompilerParams(dimension_semantics=sem, vmem_limit_bytes=VMEM_LIMIT_BYTES)


class _Place:
    def __init__(self):
        self.cx, self.cy, self.c = lax.axis_index("x"), lax.axis_index("y"), lax.axis_index("c")
        self.chips = [(1 - self.cx, self.cy), (self.cx, 1 - self.cy), (1 - self.cx, 1 - self.cy)]
        self.me = 2 * self.cx + self.cy
        self.slots = [self.me] + [2 * px + py for px, py in self.chips]
        self.sib = (self.cx, self.cy, 1 - self.c)


class _Copies:
    def __init__(self):
        self.srcs, self.bufs, self.news = {}, {}, {}
        self.plans = []
        self.out_bufs, self.out_news = {}, {}

    def add(self, plan, n_copies, srcs=None, bufs=None, news=None):
        for have, more in ((self.srcs, srcs), (self.bufs, bufs), (self.news, news)):
            for key, val in (more or {}).items():
                assert key not in have or have[key] is val, key
                have[key] = val
        self.plans.append((plan, n_copies))

    def empty(self):
        return not self.plans

    def count(self):
        return sum(n for _, n in self.plans)

    def copies(self, src_refs, buf_refs, new_refs, send, recv):
        place = _Place()
        srcs = dict(zip(self.srcs, src_refs))
        bufs = dict(zip(self.bufs, buf_refs))
        news = dict(zip(self.news, new_refs))
        out = []
        for plan, n_copies in self.plans:
            triples = plan(srcs, bufs, news, place)
            assert len(triples) == n_copies
            for src, dst, dev in triples:
                n = len(out)
                out.append(pltpu.make_async_remote_copy(
                    src_ref=src, dst_ref=dst, send_sem=send.at[n], recv_sem=recv.at[n],
                    device_id=dev, device_id_type=MESH))
        return out

    def land(self, results):
        n_b = len(self.bufs)
        self.out_bufs = dict(zip(self.bufs, results[:n_b]))
        self.out_news = dict(zip(self.news, results[n_b:]))


def _pallas(comm, body, *, name, grid, in_specs, out_specs, out_shape, compiler_params,
            scratch_shapes=(), input_output_aliases=None):
    aliases = dict(input_output_aliases or {})
    if comm is None or comm.empty():
        return pl.pallas_call(body, name=name, grid=grid, in_specs=in_specs, out_specs=out_specs,
                              out_shape=out_shape, scratch_shapes=list(scratch_shapes),
                              input_output_aliases=aliases, compiler_params=compiler_params)
    single = not isinstance(out_shape, (list, tuple))
    out_shapes = [out_shape] if single else list(out_shape)
    out_specs_l = [out_specs] if single else list(out_specs)
    n_in, n_out, n_scr = len(in_specs), len(out_shapes), len(scratch_shapes)
    n_s, n_b, n_n = len(comm.srcs), len(comm.bufs), len(comm.news)
    n_copies = comm.count()

    def wrapped(*refs):
        ins = refs[:n_in]
        src_refs = refs[n_in:n_in + n_s]
        o0 = n_in + n_s + n_b
        outs = refs[o0:o0 + n_out]
        buf_refs = refs[o0 + n_out:o0 + n_out + n_b]
        new_refs = refs[o0 + n_out + n_b:o0 + n_out + n_b + n_n]
        rest = refs[o0 + n_out + n_b + n_n:]
        scratch, (send, recv) = rest[:n_scr], rest[n_scr:]
        ids = [pl.program_id(a) for a in range(len(grid))]
        first = functools.reduce(jnp.logical_and, [i == 0 for i in ids])
        last = functools.reduce(jnp.logical_and, [i == g - 1 for i, g in zip(ids, grid)])

        @pl.when(first)
        def _():
            for cp in comm.copies(src_refs, buf_refs, new_refs, send, recv):
                cp.start()

        body(*ins, *outs, *scratch)

        @pl.when(last)
        def _():
            for cp in comm.copies(src_refs, buf_refs, new_refs, send, recv):
                cp.wait()

    hbm = pl.BlockSpec(memory_space=pl.ANY)
    for j in range(n_b):
        aliases[n_in + n_s + j] = n_out + j
    call = pl.pallas_call(
        wrapped, name=name, grid=grid,
        in_specs=[*in_specs, *([hbm] * (n_s + n_b))],
        out_specs=[*out_specs_l, *([hbm] * (n_b + n_n))],
        out_shape=[*out_shapes,
                   *[jax.ShapeDtypeStruct(a.shape, a.dtype) for a in comm.bufs.values()],
                   *comm.news.values()],
        scratch_shapes=[*scratch_shapes, pltpu.SemaphoreType.DMA((n_copies,)),
                        pltpu.SemaphoreType.DMA((n_copies,))],
        input_output_aliases=aliases, compiler_params=compiler_params)

    def run(*args):
        res = call(*args, *comm.srcs.values(), *comm.bufs.values())
        comm.land(res[n_out:])
        return res[0] if single else res[:n_out]

    return run


def _copy_call(comm, name):
    n_s, n_b, n_n = len(comm.srcs), len(comm.bufs), len(comm.news)
    n_copies = comm.count()

    def body(*refs):
        src_refs = refs[:n_s]
        buf_refs = refs[n_s + n_b:n_s + 2 * n_b]
        new_refs = refs[n_s + 2 * n_b:n_s + 2 * n_b + n_n]
        send, recv = refs[n_s + 2 * n_b + n_n:]
        copies = comm.copies(src_refs, buf_refs, new_refs, send, recv)
        for cp in copies:
            cp.start()
        for cp in copies:
            cp.wait()

    hbm = pl.BlockSpec(memory_space=pl.ANY)
    res = pl.pallas_call(
        body, name=name,
        in_specs=[hbm] * (n_s + n_b), out_specs=[hbm] * (n_b + n_n),
        out_shape=[*[jax.ShapeDtypeStruct(a.shape, a.dtype) for a in comm.bufs.values()],
                   *comm.news.values()],
        input_output_aliases={n_s + j: j for j in range(n_b)},
        scratch_shapes=[pltpu.SemaphoreType.DMA((n_copies,)), pltpu.SemaphoreType.DMA((n_copies,))],
    )(*comm.srcs.values(), *comm.bufs.values())
    comm.land(res)


def _embed(meta_pad, x, name, comm=None):
    seq, d = x.shape
    t = seq + LANES

    def body(m_ref, x_ref, h_ref, hb_ref):
        first = pl.program_id(0) == 0
        v = jnp.where(first, m_ref[...], x_ref[...])
        h_ref[...] = v
        hb_ref[...] = v.astype(bf16)

    return _pallas(
        comm, body, name=name, grid=(t // LANES,),
        in_specs=[pl.BlockSpec((LANES, d), lambda i: (0, 0)),
                  pl.BlockSpec((LANES, d), lambda i: (jnp.maximum(i - 1, 0), 0))],
        out_specs=[pl.BlockSpec((LANES, d), lambda i: (i, 0)),
                   pl.BlockSpec((LANES, d), lambda i: (i, 0))],
        out_shape=[jax.ShapeDtypeStruct((t, d), f32), jax.ShapeDtypeStruct((t, d), bf16)],
        compiler_params=_params("parallel"),
    )(meta_pad, x)


def _nn_matmul(x, w, out_dtype, name, comm=None, out_scale=None):
    t, k = x.shape
    s_n, _, n = w.shape
    assert s_n == 1 or n % LANES == 0
    tm = _tile(t, ROWS_WIDE, 16)

    def body(x_ref, w_ref, o_ref):
        res = jnp.dot(x_ref[...].astype(bf16), w_ref[...], preferred_element_type=f32)
        if out_scale is not None:
            res = res * out_scale
        o_ref[...] = res.astype(o_ref.dtype)

    return _pallas(
        comm, body, name=name, grid=(s_n, t // tm),
        in_specs=[pl.BlockSpec((tm, k), lambda s, i: (i, 0)),
                  pl.BlockSpec((None, k, n), lambda s, i: (s, 0, 0))],
        out_specs=pl.BlockSpec((tm, n), lambda s, i: (i, s)),
        out_shape=jax.ShapeDtypeStruct((t, s_n * n), out_dtype),
        compiler_params=_params("parallel", "parallel"),
    )(x, w)


def _ffn_up(hb, wg, wu, layer, name, comm=None):
    t, d = hb.shape
    s_n, _, n, _ = wg.shape
    tm = _tile(t, ROWS_WIDE, 16)

    def body(x_ref, wg_ref, wu_ref, a_ref, b_ref, s_ref):
        x = x_ref[...]
        a = lax.dot_general(x, wg_ref[...], NT_DIMS, preferred_element_type=f32)
        b = lax.dot_general(x, wu_ref[...], NT_DIMS, preferred_element_type=f32)
        a_ref[...] = a.astype(bf16)
        b_ref[...] = b.astype(bf16)
        s_ref[...] = (a * jax.nn.sigmoid(a) * b).astype(bf16)

    wspec = pl.BlockSpec((None, None, n, d), lambda s, i: (s, layer, 0, 0))
    ospec = pl.BlockSpec((None, tm, n), lambda s, i: (s, i, 0))
    return _pallas(
        comm, body, name=name, grid=(s_n, t // tm),
        in_specs=[pl.BlockSpec((tm, d), lambda s, i: (i, 0)), wspec, wspec],
        out_specs=[ospec, ospec, ospec],
        out_shape=[jax.ShapeDtypeStruct((s_n, t, n), bf16)] * 3,
        compiler_params=_params("parallel", "parallel"),
    )(hb, wg, wu)


def _down_ln(x, w, layer, hprev, gain, bias, beta, name, comm=None):
    s_n, t, k = x.shape
    d = w.shape[-1]
    tm = _tile(t, ROWS_RESIDENT, 16)

    def body(x_ref, w_ref, h_ref, g_ref, b_ref, r_out, h_out, hb_out):
        acc = jnp.dot(x_ref[0], w_ref[0], preferred_element_type=f32)
        for s in range(1, s_n):
            acc = acc + jnp.dot(x_ref[s], w_ref[s], preferred_element_type=f32)
        r = ALPHA * h_ref[...] + beta * acc
        mu = jnp.mean(r, axis=-1, keepdims=True)
        xc = r - mu
        var = jnp.mean(xc * xc, axis=-1, keepdims=True)
        y = xc * lax.rsqrt(var + LN_EPS) * g_ref[...] + b_ref[...]
        r_out[...] = r
        h_out[...] = y
        hb_out[...] = y.astype(bf16)

    row = pl.BlockSpec((tm, d), lambda i: (i, 0))
    vec = pl.BlockSpec((1, d), lambda i: (0, 0))
    return _pallas(
        comm, body, name=name, grid=(t // tm,),
        in_specs=[pl.BlockSpec((s_n, tm, k), lambda i: (0, i, 0)),
                  pl.BlockSpec((s_n, None, k, d), lambda i: (0, layer, 0, 0),
                               pipeline_mode=pl.Buffered(1)),
                  row, vec, vec],
        out_specs=[row, row, row],
        out_shape=[jax.ShapeDtypeStruct((t, d), f32), jax.ShapeDtypeStruct((t, d), f32),
                   jax.ShapeDtypeStruct((t, d), bf16)],
        compiler_params=_params("parallel"),
    )(x, w, hprev, gain, bias)


def _conv_fwd(p, conv_w, name):
    t, d3 = p.shape
    d = d3 // 3
    tm = _tile(t, 320, 8)
    hb = tm // 8

    def body(p_ref, prev_ref, w_ref, z_ref):
        i = pl.program_id(0)
        rows = i * tm - 8 + lax.broadcasted_iota(jnp.int32, (tm + 8, 1), 0)
        cg = jnp.concatenate([prev_ref[:, d:2 * d], p_ref[:, d:2 * d]], axis=0)
        val = jnp.concatenate([prev_ref[:, 2 * d:], p_ref[:, 2 * d:]], axis=0)
        u = jnp.where(rows >= PAD, cg * val, 0.0)
        y = (w_ref[2:3, :] * u + w_ref[1:2, :] * pltpu.roll(u, 1, 0)
             + w_ref[0:1, :] * pltpu.roll(u, 2, 0))
        z_ref[...] = (p_ref[:, :d] * y[8:]).astype(bf16)

    return pl.pallas_call(
        body, name=name, grid=(t // tm,),
        in_specs=[pl.BlockSpec((tm, d3), lambda i: (i, 0)),
                  pl.BlockSpec((8, d3), lambda i: (jnp.maximum(i * hb - 1, 0), 0)),
                  pl.BlockSpec((3, d), lambda i: (0, 0))],
        out_specs=pl.BlockSpec((tm, d), lambda i: (i, 0)),
        out_shape=jax.ShapeDtypeStruct((t, d), bf16),
        compiler_params=_params("parallel"),
    )(p, p, conv_w)


def _row_scan(x, name, fbias=None, reverse=False):
    t, n = x.shape
    blk = _tile(t, 640, LANES)
    n_blk = t // blk
    gate = fbias is not None

    def body(*refs):
        if gate:
            x_ref, fb_ref, o_ref, carry = refs
        else:
            x_ref, o_ref, carry = refs
        i = pl.program_id(0)

        @pl.when(i == 0)
        def _():
            carry[...] = jnp.zeros_like(carry)

        v = x_ref[...]
        r = lax.broadcasted_iota(jnp.int32, (blk, n), 0)
        if gate:
            v = v + fb_ref[...]
            v = jnp.minimum(v, 0.0) - jnp.log1p(jnp.exp(-jnp.abs(v)))
            v = jnp.where(i * blk + r >= PAD, v, 0.0)
        sh = 1
        while sh < blk:
            if reverse:
                v = v + jnp.where(r < blk - sh, pltpu.roll(v, blk - sh, 0), 0.0)
            else:
                v = v + jnp.where(r >= sh, pltpu.roll(v, sh, 0), 0.0)
            sh *= 2
        v = v + carry[...]
        o_ref[...] = v
        carry[...] = o_ref[0:1, :] if reverse else o_ref[blk - 1:blk, :]

    order = (lambda i: (n_blk - 1 - i, 0)) if reverse else (lambda i: (i, 0))
    in_specs = [pl.BlockSpec((blk, n), order)]
    args = [x]
    if gate:
        in_specs.append(pl.BlockSpec((1, n), lambda i: (0, 0)))
        args.append(fbias)
    return pl.pallas_call(
        body, name=name, grid=(n_blk,),
        in_specs=in_specs,
        out_specs=pl.BlockSpec((blk, n), order),
        out_shape=jax.ShapeDtypeStruct((t, n), f32),
        scratch_shapes=[pltpu.VMEM((1, n), f32)],
        compiler_params=_params("arbitrary"),
    )(*args)


def _lanes(x, n):
    return jnp.concatenate([x] * (n // LANES), axis=1)


def _attn_fwd(q, k, v, cq_rep, ck_rows, name):
    t, d = q.shape
    n_heads = d // HEAD_DIM
    bk = ck_rows.shape[-1]
    bq = bk

    def lane_fold(x, op):
        out = x[:, :LANES]
        for c0 in range(LANES, bk, LANES):
            out = op(out, x[:, c0:c0 + LANES])
        return out

    def body(q_ref, k_ref, v_ref, cq_ref, ck_ref, o_ref, o32_ref, cl_ref,
             s_scr, p_scr, m_scr, l_scr, red_scr, acc_scr):
        i = pl.program_id(1)
        m_scr[...] = jnp.full_like(m_scr, NEG_INF)
        l_scr[...] = jnp.zeros_like(l_scr)
        acc_scr[...] = jnp.zeros_like(acc_scr)
        qb = q_ref[...]
        ahead = (lax.broadcasted_iota(jnp.int32, (STRIP, bk), 1)
                 - lax.broadcasted_iota(jnp.int32, (STRIP, bk), 0))

        def tile(j, diagonal):
            k0 = pl.multiple_of(j * bk, bk)
            s_scr[...] = lax.dot_general(qb, k_ref[pl.ds(k0, bk), :], NT_DIMS,
                                         preferred_element_type=f32)
            ck = ck_ref[j]
            for r in range(0, bq, STRIP):
                rows = slice(r, r + STRIP)
                s = s_scr[rows, :] + _lanes(cq_ref[rows, :], bk) - ck
                if diagonal:
                    s = jnp.where(ahead <= r, s, NEG_INF)
                s_scr[rows, :] = s
                red_scr[rows, :] = lane_fold(s, jnp.maximum)
            m_old = m_scr[...]
            m_new = jnp.maximum(m_old, jnp.broadcast_to(
                jnp.max(red_scr[...], axis=1, keepdims=True), (bq, LANES)))
            a = jnp.exp(m_old - m_new)
            m_scr[...] = m_new
            for r in range(0, bq, STRIP):
                rows = slice(r, r + STRIP)
                pr = jnp.exp(s_scr[rows, :] - _lanes(m_scr[rows, :], bk))
                red_scr[rows, :] = lane_fold(pr, jnp.add)
                p_scr[rows, :] = pr.astype(bf16)
            l_scr[...] = a * l_scr[...] + jnp.broadcast_to(
                jnp.sum(red_scr[...], axis=1, keepdims=True), (bq, LANES))
            acc_scr[...] = a * acc_scr[...] + jnp.dot(
                p_scr[...], v_ref[pl.ds(k0, bk), :], preferred_element_type=f32)

        def full_tile(j, carry):
            tile(j, False)
            return carry

        lax.fori_loop(0, i, full_tile, 0)
        tile(i, True)
        out = acc_scr[...] / l_scr[...]
        o_ref[...] = out.astype(bf16)
        o32_ref[...] = out
        red_scr[...] = cq_ref[...] - (m_scr[...] + jnp.log(l_scr[...]))
        for c0 in range(0, bq, LANES):
            cl_ref[:, c0:c0 + LANES] = red_scr[c0:c0 + LANES, :].T[0:1, :]

    qblk = pl.BlockSpec((bq, HEAD_DIM), lambda h, i: (i, h))
    head_rows = pl.BlockSpec((t, HEAD_DIM), lambda h, i: (0, h))
    rep = pl.BlockSpec((None, bq, LANES), lambda h, i: (h, i, 0))
    col = pltpu.VMEM((bq, LANES), f32)
    return pl.pallas_call(
        body, name=name, grid=(n_heads, t // bq),
        in_specs=[qblk, head_rows, head_rows, rep,
                  pl.BlockSpec((None, t // bk, 1, bk), lambda h, i: (h, 0, 0, 0))],
        out_specs=[qblk, qblk, pl.BlockSpec((None, None, 1, bq), lambda h, i: (h, i, 0, 0))],
        out_shape=[jax.ShapeDtypeStruct((t, d), bf16), jax.ShapeDtypeStruct((t, d), f32),
                   jax.ShapeDtypeStruct((n_heads, t // bq, 1, bq), f32)],
        scratch_shapes=[pltpu.VMEM((bq, bk), f32), pltpu.VMEM((bq, bk), bf16), col, col, col,
                        pltpu.VMEM((bq, HEAD_DIM), f32)],
        compiler_params=_params("parallel", "parallel"),
    )(q, k, v, cq_rep, ck_rows)


def _loss_head(h, target, r, gain, name):
    t, d = h.shape

    def body(h_ref, t_ref, r_ref, g_ref, dr_ref, drb_ref, dg_ref, db_ref, loss_ref):
        i = pl.program_id(0)

        @pl.when(i == 0)
        def _():
            loss_ref[...] = jnp.zeros_like(loss_ref)
            dg_ref[...] = jnp.zeros_like(dg_ref)
            db_ref[...] = jnp.zeros_like(db_ref)

        diff = jnp.where(i >= 1, h_ref[...] - t_ref[...], 0.0)
        loss_ref[...] += jnp.sum(diff * diff)
        dr, dg, db = _ln_bwd_rows(diff * (1.0 / d), r_ref[...], g_ref[...])
        dr_ref[...] = dr
        drb_ref[...] = dr.astype(bf16)
        dg_ref[...] += dg
        db_ref[...] += db

    row = pl.BlockSpec((LANES, d), lambda i: (i, 0))
    vec = pl.BlockSpec((1, d), lambda i: (0, 0))
    return pl.pallas_call(
        body, name=name, grid=(t // LANES,),
        in_specs=[row, pl.BlockSpec((LANES, d), lambda i: (jnp.maximum(i - 1, 0), 0)), row, vec],
        out_specs=[row, row, vec, vec, pl.BlockSpec((1, LANES), lambda i: (0, 0))],
        out_shape=[jax.ShapeDtypeStruct((t, d), f32), jax.ShapeDtypeStruct((t, d), bf16),
                   jax.ShapeDtypeStruct((1, d), f32), jax.ShapeDtypeStruct((1, d), f32),
                   jax.ShapeDtypeStruct((1, LANES), f32)],
        compiler_params=_params("arbitrary"),
    )(h, target, r, gain)


def _ln_bwd_rows(dy, rr, gain):
    mu = jnp.mean(rr, axis=-1, keepdims=True)
    xc = rr - mu
    var = jnp.mean(xc * xc, axis=-1, keepdims=True)
    rstd = lax.rsqrt(var + LN_EPS)
    xhat = xc * rstd
    dxh = dy * gain
    m1 = jnp.mean(dxh, axis=-1, keepdims=True)
    m2 = jnp.mean(dxh * xhat, axis=-1, keepdims=True)
    dr = rstd * (dxh - m1 - xhat * m2)
    return dr, jnp.sum(dy * xhat, axis=0, keepdims=True), jnp.sum(dy, axis=0, keepdims=True)


def _ffn_bwd_act(drb, wd, layer, a, b, name, comm=None):
    t, d = drb.shape
    s_n, _, n = a.shape
    tm = _tile(t, ROWS_WIDE, 16)

    def body(dr_ref, w_ref, a_ref, b_ref, da_ref, db_ref):
        ds = 0.5 * lax.dot_general(dr_ref[...], w_ref[...], NT_DIMS, preferred_element_type=f32)
        da_ref[...], db_ref[...] = _swiglu_bwd(ds, a_ref, b_ref)

    act = pl.BlockSpec((None, tm, n), lambda s, i: (s, i, 0))
    return _pallas(
        comm, body, name=name, grid=(s_n, t // tm),
        in_specs=[pl.BlockSpec((tm, d), lambda s, i: (i, 0)),
                  pl.BlockSpec((None, None, n, d), lambda s, i: (s, layer, 0, 0)), act, act],
        out_specs=[act, act],
        out_shape=[jax.ShapeDtypeStruct((s_n, t, n), bf16), jax.ShapeDtypeStruct((s_n, t, n), bf16)],
        compiler_params=_params("parallel", "parallel"),
    )(drb, wd, a, b)


def _swiglu_bwd(ds, a_ref, b_ref):
    av = a_ref[...].astype(f32)
    sig = jax.nn.sigmoid(av)
    da = ds * b_ref[...].astype(f32) * (sig * (1.0 + av * (1.0 - sig)))
    return da.astype(bf16), (ds * (av * sig)).astype(bf16)


def _act_spec(mode, tt, k, t_first):
    def fix(fn):
        return (lambda i, s: fn(s, i)) if t_first else fn
    if mode == "shared":
        return pl.BlockSpec((tt, k), fix(lambda s, i: (i, 0)))
    if mode == "cols":
        return pl.BlockSpec((tt, k), fix(lambda s, i: (i, s)))
    assert mode == "stack"
    return pl.BlockSpec((None, tt, k), fix(lambda s, i: (s, i, 0)))


def _act_width(arr, mode, s_n):
    return arr.shape[-1] // s_n if mode == "cols" else arr.shape[-1]


def _tn_matmul(pairs, s_n, scale, name, comm=None):
    t = pairs[0][0].shape[-2]
    tt = _tile(t, ROWS_TN_ONE if len(pairs) == 1 else ROWS_TN, 16)
    n_t = t // tt
    arrays, specs, where = [], [], []
    for x, xmode, y, ymode in pairs:
        pos = []
        for arr, mode in ((x, xmode), (y, ymode)):
            hit = [j for j, a in enumerate(arrays) if a is arr]
            if not hit:
                arrays.append(arr)
                specs.append(_act_spec(mode, tt, _act_width(arr, mode, s_n), False))
                hit = [len(arrays) - 1]
            pos.append(hit[0])
        where.append(pos)
    widths = [(_act_width(x, xm, s_n), _act_width(y, ym, s_n)) for x, xm, y, ym in pairs]
    n_a = len(arrays)

    def body(*refs):
        i = pl.program_id(1)
        for (px, py), o_ref in zip(where, refs[n_a:]):
            part = lax.dot_general(refs[px][...].astype(bf16), refs[py][...].astype(bf16), TN_DIMS,
                                   preferred_element_type=f32)

            @pl.when(i == 0)
            def _():
                o_ref[...] = part

            @pl.when(i > 0)
            def _():
                o_ref[...] += part

            if scale != 1.0:
                @pl.when(i == n_t - 1)
                def _():
                    o_ref[...] = o_ref[...] * scale

    return _pallas(
        comm, body, name=name, grid=(s_n, n_t),
        in_specs=specs,
        out_specs=[pl.BlockSpec((None, kx, ky), lambda s, i: (s, 0, 0)) for kx, ky in widths],
        out_shape=[jax.ShapeDtypeStruct((s_n, kx, ky), f32) for kx, ky in widths],
        compiler_params=_params("parallel", "arbitrary"),
    )(*arrays)


def _nt_sum(pairs, base, base_scale, s_n, out_dtype, name, comm=None, ln=None):
    t = pairs[0][0].shape[-2]
    pairs = [(*pr, False)[:5] for pr in pairs]
    d = pairs[0][2].shape[-1] if pairs[0][4] else pairs[0][2].shape[-2]
    if s_n > 1:
        tm = _tile(t, ROWS_RESIDENT, 16)
    else:
        tm = _tile(t, ROWS_ACC if ln is None else ROWS_ACC_LN, 16)
    n_p = len(pairs)
    has_base = base is not None
    n_out = 1 if ln is None else 4
    widths = [_act_width(dy, mode, s_n) for dy, mode, _, _, _ in pairs]

    def body(*refs):
        dy_refs = refs[0:2 * n_p:2]
        w_refs = refs[1:2 * n_p:2]
        rest = refs[2 * n_p:]
        base_ref = rest[0] if has_base else None
        o_refs = rest[-n_out:]

        if ln is not None:
            @pl.when(pl.program_id(0) == 0)
            def _():
                o_refs[2][...] = jnp.zeros_like(o_refs[2])
                o_refs[3][...] = jnp.zeros_like(o_refs[3])

        res = None
        for dy_ref, w_ref, (_, mode, _, _, flip), k in zip(dy_refs, w_refs, pairs, widths):
            for s in range(s_n):
                dyv = (dy_ref[s] if mode == "stack" else dy_ref[:, s * k:(s + 1) * k]).astype(bf16)
                if flip:
                    part = jnp.dot(dyv, w_ref[s], preferred_element_type=f32)
                else:
                    part = lax.dot_general(dyv, w_ref[s], NT_DIMS, preferred_element_type=f32)
                res = part if res is None else res + part
        if has_base:
            res = base_scale * base_ref[...] + res
        if ln is None:
            o_refs[0][...] = res.astype(o_refs[0].dtype)
        else:
            r_ref, g_ref = rest[-n_out - 2], rest[-n_out - 1]
            dr, dg, db = _ln_bwd_rows(res, r_ref[...], g_ref[...])
            o_refs[0][...] = dr
            o_refs[1][...] = dr.astype(bf16)
            o_refs[2][...] += dg
            o_refs[3][...] += db

    in_specs, args = [], []
    for (dy, mode, w, layer, flip), k in zip(pairs, widths):
        if mode == "stack":
            in_specs.append(pl.BlockSpec((s_n, tm, k), lambda i: (0, i, 0)))
        else:
            in_specs.append(pl.BlockSpec((tm, s_n * k), lambda i: (i, 0)))
        wshape = (k, d) if flip else (d, k)
        if layer is None:
            in_specs.append(pl.BlockSpec((s_n, *wshape), lambda i: (0, 0, 0),
                                         pipeline_mode=pl.Buffered(1)))
        else:
            in_specs.append(pl.BlockSpec((s_n, None, *wshape),
                                         functools.partial(lambda i, l: (0, l, 0, 0), l=layer),
                                         pipeline_mode=pl.Buffered(1)))
        args += [dy, w]
    row = pl.BlockSpec((tm, d), lambda i: (i, 0))
    vec = pl.BlockSpec((1, d), lambda i: (0, 0))
    if has_base:
        in_specs.append(row)
        args.append(base)
    if ln is None:
        out_specs = row
        out_shape = jax.ShapeDtypeStruct((t, d), out_dtype)
    else:
        in_specs += [row, vec]
        args += list(ln)
        out_specs = [row, row, vec, vec]
        out_shape = [jax.ShapeDtypeStruct((t, d), f32), jax.ShapeDtypeStruct((t, d), bf16),
                     jax.ShapeDtypeStruct((1, d), f32), jax.ShapeDtypeStruct((1, d), f32)]
    return _pallas(
        comm, body, name=name, grid=(t // tm,),
        in_specs=in_specs, out_specs=out_specs, out_shape=out_shape,
        compiler_params=_params("parallel" if ln is None else "arbitrary"),
    )(*args)


def _conv_bwd(dz, p, conv_w, name, comm=None):
    t, d3 = p.shape
    d = d3 // 3
    tm = _tile(t, 320, 8)
    hb = tm // 8
    last8 = t // 8 - 1
    n_ext = tm + 8

    def body(dz_ref, dzn_ref, p_ref, pp_ref, pn_ref, w_ref, dp_ref, dw_ref):
        i = pl.program_id(0)

        @pl.when(i == 0)
        def _():
            dw_ref[...] = jnp.zeros_like(dw_ref)

        w0, w1, w2 = w_ref[0:1, :], w_ref[1:2, :], w_ref[2:3, :]
        rows_u = i * tm - 8 + lax.broadcasted_iota(jnp.int32, (n_ext, 1), 0)
        cg = jnp.concatenate([pp_ref[:, d:2 * d], p_ref[:, d:2 * d]], axis=0)
        val = jnp.concatenate([pp_ref[:, 2 * d:], p_ref[:, 2 * d:]], axis=0)
        u = jnp.where(rows_u >= PAD, cg * val, 0.0)
        u1 = pltpu.roll(u, 1, 0)
        u2 = pltpu.roll(u, 2, 0)
        y = (w2 * u + w1 * u1 + w0 * u2)[8:]
        dzv = dz_ref[...]
        bg = p_ref[:, :d]
        rows_n = (i + 1) * tm + lax.broadcasted_iota(jnp.int32, (8, 1), 0)
        dy_main = dzv * bg
        dy_next = jnp.where(rows_n < t, dzn_ref[...] * pn_ref[:, :d], 0.0)
        dye = jnp.concatenate([dy_main, dy_next], axis=0)
        du = (w2 * dye + w1 * pltpu.roll(dye, n_ext - 1, 0)
              + w0 * pltpu.roll(dye, n_ext - 2, 0))[:tm]
        du = jnp.where(rows_u[8:] >= PAD, du, 0.0)
        dp_ref[:, :d] = (dzv * y).astype(bf16)
        dp_ref[:, d:2 * d] = (du * val[8:]).astype(bf16)
        dp_ref[:, 2 * d:] = (du * cg[8:]).astype(bf16)
        dw_ref[0:1, :] += jnp.sum(dy_main * u2[8:], axis=0, keepdims=True)
        dw_ref[1:2, :] += jnp.sum(dy_main * u1[8:], axis=0, keepdims=True)
        dw_ref[2:3, :] += jnp.sum(dy_main * u[8:], axis=0, keepdims=True)

    nxt = lambda i: (jnp.minimum((i + 1) * hb, last8), 0)
    return _pallas(
        comm, body, name=name, grid=(t // tm,),
        in_specs=[pl.BlockSpec((tm, d), lambda i: (i, 0)),
                  pl.BlockSpec((8, d), nxt),
                  pl.BlockSpec((tm, d3), lambda i: (i, 0)),
                  pl.BlockSpec((8, d3), lambda i: (jnp.maximum(i * hb - 1, 0), 0)),
                  pl.BlockSpec((8, d3), nxt),
                  pl.BlockSpec((3, d), lambda i: (0, 0))],
        out_specs=[pl.BlockSpec((tm, d3), lambda i: (i, 0)),
                   pl.BlockSpec((3, d), lambda i: (0, 0))],
        out_shape=[jax.ShapeDtypeStruct((t, d3), bf16), jax.ShapeDtypeStruct((3, d), f32)],
        compiler_params=_params("arbitrary"),
    )(dz, dz, p, p, p, conv_w)


def _attn_do(drb, w_o, o, bq, name, comm=None):
    t, d = drb.shape
    n_heads = d // HEAD_DIM

    def body(dr_ref, w_ref, o_ref, do_ref, delta_ref):
        do = lax.dot_general(dr_ref[...], w_ref[...], NT_DIMS, preferred_element_type=f32).astype(bf16)
        do_ref[...] = do
        prod = o_ref[...] * do.astype(f32)
        for h in range(n_heads):
            for c0 in range(0, bq, LANES):
                blk = prod[c0:c0 + LANES, h * HEAD_DIM:(h + 1) * HEAD_DIM]
                delta_ref[h, :, c0:c0 + LANES] = jnp.sum(blk.T, axis=0, keepdims=True)

    row = pl.BlockSpec((bq, d), lambda i: (i, 0))
    return _pallas(
        comm, body, name=name, grid=(t // bq,),
        in_specs=[row, pl.BlockSpec((d, d), lambda i: (0, 0)), row],
        out_specs=[row, pl.BlockSpec((n_heads, None, 1, bq), lambda i: (0, i, 0, 0))],
        out_shape=[jax.ShapeDtypeStruct((t, d), bf16),
                   jax.ShapeDtypeStruct((n_heads, t // bq, 1, bq), f32)],
        compiler_params=_params("parallel"),
    )(drb, w_o, o)


def _attn_bwd(q, k, v, kt, do, ckey, cl_rows, delta_rows, name, comm=None):
    t, d = q.shape
    n_heads = d // HEAD_DIM
    bk = kt.shape[-1]
    bq = bk
    n_kv = t // bk
    n_q = t // bq
    scale = 1.0 / math.sqrt(HEAD_DIM)

    def body(q_ref, do_ref, cl_ref, dl_ref, k_ref, v_ref, kt_ref, ck_ref,
             dq_ref, dcq_ref, dk_ref, dv_ref, dck_ref,
             st_scr, dp_scr, p_scr, ds_scr, dqt, dk_acc, dv_acc, dck_acc):
        j = pl.program_id(1)

        @pl.when(j == 0)
        def _():
            dqt[...] = jnp.zeros_like(dqt)
            dcq_ref[...] = jnp.zeros_like(dcq_ref)

        dk_acc[...] = jnp.zeros_like(dk_acc)
        dv_acc[...] = jnp.zeros_like(dv_acc)
        dck_acc[...] = jnp.zeros_like(dck_acc)
        kb = k_ref[...]
        vb = v_ref[...]
        behind = (lax.broadcasted_iota(jnp.int32, (STRIP, bq), 1)
                  - lax.broadcasted_iota(jnp.int32, (STRIP, bq), 0))

        def tile(i, diagonal):
            r0 = pl.multiple_of(i * bq, bq)
            qi = q_ref[pl.ds(r0, bq), :]
            doi = do_ref[pl.ds(r0, bq), :]
            st_scr[...] = lax.dot_general(kb, qi, NT_DIMS, preferred_element_type=f32)
            dp_scr[...] = lax.dot_general(vb, doi, NT_DIMS, preferred_element_type=f32)
            cl = cl_ref[i]
            dl = dl_ref[i]
            over_keys = jnp.zeros((STRIP, bq), f32)
            for r in range(0, bk, STRIP):
                keys = slice(r, r + STRIP)
                st = st_scr[keys, :] + cl - _lanes(ck_ref[keys, :], bq)
                if diagonal:
                    st = jnp.where(behind >= r, st, NEG_INF)
                pr = jnp.exp(st)
                ds = pr * (dp_scr[keys, :] - dl)
                over_keys = over_keys + ds
                dck_acc[keys, :] -= jnp.sum(ds, axis=1, keepdims=True)
                p_scr[keys, :] = pr.astype(bf16)
                ds_scr[keys, :] = ds.astype(bf16)
            dcq_ref[i] += jnp.sum(over_keys, axis=0, keepdims=True)
            dv_acc[...] += jnp.dot(p_scr[...], doi, preferred_element_type=f32)
            dk_acc[...] += jnp.dot(ds_scr[...], qi, preferred_element_type=f32)
            dqt[i] += jnp.dot(kt_ref[...], ds_scr[...], preferred_element_type=f32)

        def full_tile(i, carry):
            tile(i, False)
            return carry

        tile(j, True)
        lax.fori_loop(j + 1, n_q, full_tile, 0)
        dk_ref[...] = dk_acc[...].astype(bf16)
        dv_ref[...] = dv_acc[...].astype(bf16)
        for c0 in range(0, bk, LANES):
            keys = slice(c0, c0 + LANES)
            dck_ref[:, keys] = jnp.broadcast_to(dck_acc[keys, :], (LANES, LANES)).T[0:1, :]

        @pl.when(j == n_kv - 1)
        def _():
            def emit(i, carry):
                r0 = pl.multiple_of(i * bq, bq)
                dq_ref[pl.ds(r0, bq), :] = (dqt[i].T * scale).astype(bf16)
                return carry
            lax.fori_loop(0, n_q, emit, 0)

    head_rows = pl.BlockSpec((t, HEAD_DIM), lambda h, j: (0, h))
    head_stat = pl.BlockSpec((None, n_q, 1, bq), lambda h, j: (h, 0, 0, 0))
    kblk = pl.BlockSpec((bk, HEAD_DIM), lambda h, j: (j, h))
    return _pallas(
        comm, body, name=name, grid=(n_heads, n_kv),
        in_specs=[head_rows, head_rows, head_stat, head_stat, kblk, kblk,
                  pl.BlockSpec((None, None, HEAD_DIM, bk), lambda h, j: (h, j, 0, 0)),
                  pl.BlockSpec((None, bk, LANES), lambda h, j: (h, j, 0))],
        out_specs=[head_rows, head_stat, kblk, kblk,
                   pl.BlockSpec((None, None, 1, bk), lambda h, j: (h, j, 0, 0))],
        out_shape=[jax.ShapeDtypeStruct((t, d), bf16),
                   jax.ShapeDtypeStruct((n_heads, n_q, 1, bq), f32),
                   jax.ShapeDtypeStruct((t, d), bf16), jax.ShapeDtypeStruct((t, d), bf16),
                   jax.ShapeDtypeStruct((n_heads, n_kv, 1, bk), f32)],
        scratch_shapes=[pltpu.VMEM((bk, bq), f32), pltpu.VMEM((bk, bq), f32),
                        pltpu.VMEM((bk, bq), bf16), pltpu.VMEM((bk, bq), bf16),
                        pltpu.VMEM((n_q, HEAD_DIM, bq), f32),
                        pltpu.VMEM((bk, HEAD_DIM), f32), pltpu.VMEM((bk, HEAD_DIM), f32),
                        pltpu.VMEM((bk, 1), f32)],
        compiler_params=_params("parallel", "arbitrary"),
    )(q, do, cl_rows, delta_rows, k, v, kt, ckey)


def _fgate_bwd(dlogf, flog, fbias, name):
    t, n = flog.shape
    rows = _tile(t, 640, LANES)

    def body(dl_ref, fl_ref, fb_ref, o_ref, sum_ref):
        i = pl.program_id(0)

        @pl.when(i == 0)
        def _():
            sum_ref[...] = jnp.zeros_like(sum_ref)

        r = i * rows + lax.broadcasted_iota(jnp.int32, (rows, n), 0)
        g = dl_ref[...] * jax.nn.sigmoid(-(fl_ref[...] + fb_ref[...]))
        g = jnp.where(r >= PAD, g, 0.0)
        o_ref[...] = g
        sum_ref[...] += jnp.sum(g, axis=0, keepdims=True)

    blk = pl.BlockSpec((rows, n), lambda i: (i, 0))
    vec = pl.BlockSpec((1, n), lambda i: (0, 0))
    return pl.pallas_call(
        body, name=name, grid=(t // rows,),
        in_specs=[blk, blk, vec], out_specs=[blk, vec],
        out_shape=[jax.ShapeDtypeStruct((t, n), f32), jax.ShapeDtypeStruct((1, n), f32)],
        compiler_params=_params("arbitrary"),
    )(dlogf, flog, fbias)


def _place_shard(ws, idx, name):
    n = len(ws)
    n_l, r, c_n = ws[0].shape
    out_dtype = bf16 if r * c_n > 2 ** 16 else ws[0].dtype
    tr = _tile(r, 512, 16) if r % 16 == 0 else r

    def body(idx_ref, *refs):
        for w_ref, o_ref in zip(refs[:n], refs[n:]):
            o_ref[...] = w_ref[...].astype(out_dtype)

    grid_spec = pltpu.PrefetchScalarGridSpec(
        num_scalar_prefetch=1, grid=(n_l, r // tr),
        in_specs=[pl.BlockSpec((None, tr, c_n), lambda l, i, idx: (l, i, 0))] * n,
        out_specs=[pl.BlockSpec((None, None, tr, c_n), lambda l, i, idx: (idx[0], l, i, 0))] * n)
    return pl.pallas_call(
        body, name=name, grid_spec=grid_spec,
        out_shape=[jax.ShapeDtypeStruct((N_CHIPS, n_l, r, c_n), out_dtype)] * n,
        compiler_params=_params("parallel", "parallel"),
    )(idx, *ws)


def _plan_gather_ici(items):
    def plan(srcs, bufs, news, p):
        out = []
        for name, layer, r2 in items:
            mine = bufs[name].at[p.me, layer, pl.ds(p.c * r2, r2)]
            out += [(mine, mine, (*chip, p.c)) for chip in p.chips]
        return out
    return plan, 3 * len(items)


def _plan_gather_d2d(items):
    def plan(srcs, bufs, news, p):
        out = []
        for name, layer, r2 in items:
            for px, py in p.chips:
                landed = bufs[name].at[2 * px + py, layer, pl.ds(p.c * r2, r2)]
                out.append((landed, landed, p.sib))
        return out
    return plan, 3 * len(items)


def _plan_pair_exchange(names, r2s):
    def plan(srcs, bufs, news, p):
        out = []
        for name, r2 in zip(names, r2s):
            for k, slot in enumerate(p.slots):
                out.append((srcs["G_" + name].at[slot, pl.ds((1 - p.c) * r2, r2)],
                            news["PAIR_" + name].at[k], p.sib))
        return out
    return plan, 4 * len(names)


def _plan_chip_exchange(names):
    def plan(srcs, bufs, news, p):
        out = []
        for name in names:
            for k, chip in enumerate(p.chips):
                out.append((srcs["SEND_" + name].at[k], news["RECV_" + name].at[k], (*chip, p.c)))
        return out
    return plan, 3 * len(names)


def _plan_pair_share(items):
    def plan(srcs, bufs, news, p):
        out = []
        for name, layer, r2 in items:
            mine = bufs["RED_" + name].at[layer, pl.ds(p.c * r2, r2)]
            out.append((mine, mine, p.sib))
        return out
    return plan, len(items)


def _rs_prepare(gs, pairs, idx, name):
    n = len(gs)
    _, r2, c_n = pairs[0].shape
    tr = _tile(r2, 256, 8)
    nb = r2 // tr

    def body(idx_ref, *refs):
        for g_ref, p_ref, o_ref in zip(refs[:n], refs[n:2 * n], refs[2 * n:]):
            o_ref[...] = (g_ref[...] + p_ref[...]).astype(bf16)

    g_spec = pl.BlockSpec((None, tr, c_n), lambda k, i, idx: (idx[k + 1], idx[4] * nb + i, 0))
    p_spec = pl.BlockSpec((None, tr, c_n), lambda k, i, idx: (k + 1, i, 0))
    grid_spec = pltpu.PrefetchScalarGridSpec(
        num_scalar_prefetch=1, grid=(3, nb), in_specs=[g_spec] * n + [p_spec] * n,
        out_specs=[pl.BlockSpec((None, tr, c_n), lambda k, i, idx: (k, i, 0))] * n)
    return pl.pallas_call(
        body, name=name, grid_spec=grid_spec,
        out_shape=[jax.ShapeDtypeStruct((3, r2, c_n), bf16)] * n,
        compiler_params=_params("parallel", "parallel"),
    )(idx, *gs, *pairs)


def _rs_finish(gs, pairs, recvs, idx, layer, n_layers, intos, name):
    n = len(gs)
    _, r2, c_n = pairs[0].shape
    tr = _tile(r2, 256, 8)
    nb = r2 // tr

    def body(idx_ref, *refs):
        for j in range(n):
            g_ref, p_ref, r0_ref, r1_ref, r2_ref = refs[5 * j:5 * j + 5]
            acc = g_ref[...] + p_ref[...]
            acc = acc + r0_ref[...].astype(f32)
            acc = acc + r1_ref[...].astype(f32)
            acc = acc + r2_ref[...].astype(f32)
            refs[len(refs) - n + j][...] = acc

    def rspec(k):
        return pl.BlockSpec((None, tr, c_n), functools.partial(lambda i, idx, kk: (kk, i, 0), kk=k))

    item_specs = [pl.BlockSpec((None, tr, c_n), lambda i, idx: (idx[0], idx[4] * nb + i, 0)),
                  rspec(0), rspec(0), rspec(1), rspec(2)]
    in_specs = item_specs * n
    args = [idx]
    for g, pair, recv in zip(gs, pairs, recvs):
        args += [g, pair, recv, recv, recv]
    aliases = {}
    if intos is not None:
        in_specs = in_specs + [pl.BlockSpec(memory_space=pl.ANY)] * n
        args += list(intos)
        aliases = {1 + 5 * n + j: j for j in range(n)}
    grid_spec = pltpu.PrefetchScalarGridSpec(
        num_scalar_prefetch=1, grid=(nb,), in_specs=in_specs,
        out_specs=[pl.BlockSpec((None, tr, c_n), lambda i, idx: (layer, idx[4] * nb + i, 0))] * n)
    return pl.pallas_call(
        body, name=name, grid_spec=grid_spec,
        out_shape=[jax.ShapeDtypeStruct((n_layers, 2 * r2, c_n), f32)] * n,
        input_output_aliases=aliases,
        compiler_params=_params("parallel"),
    )(*args)


def _adamw(ws, gs, ms, vs, layer, intos, name):
    n = len(ws)
    n_l, r, c_n = ws[0].shape
    tr = _tile(r, 256, 8)

    def body(*refs):
        for j in range(n):
            w_ref, g_ref, m_ref, v_ref = refs[4 * j:4 * j + 4]
            o0 = len(refs) - 4 * n + 4 * j
            d_ref, mo_ref, vo_ref, go_ref = refs[o0:o0 + 4]
            gv = g_ref[...]
            go_ref[...] = gv
            mn = ADAM_B1 * m_ref[...] + (1.0 - ADAM_B1) * gv
            vn = ADAM_B2 * v_ref[...] + (1.0 - ADAM_B2) * (gv * gv)
            m_hat = mn / (1.0 - ADAM_B1 ** ADAM_STEP)
            v_hat = vn / (1.0 - ADAM_B2 ** ADAM_STEP)
            d_ref[...] = -ADAM_LR * (m_hat / (jnp.sqrt(v_hat) + ADAM_EPS) + ADAM_WD * w_ref[...])
            mo_ref[...] = mn
            vo_ref[...] = vn

    blk = pl.BlockSpec((None, tr, c_n), lambda i: (layer, i, 0))
    shp = jax.ShapeDtypeStruct((n_l, r, c_n), f32)
    in_specs = [blk] * (4 * n)
    args = []
    for w, g, m, v in zip(ws, gs, ms, vs):
        args += [w, g, m, v]
    aliases = {}
    if intos is not None:
        in_specs = in_specs + [pl.BlockSpec(memory_space=pl.ANY)] * (4 * n)
        for into in intos:
            args += list(into)
        aliases = {4 * n + j: j for j in range(4 * n)}
    res = pl.pallas_call(
        body, name=name, grid=(r // tr,),
        in_specs=in_specs, out_specs=[blk] * (4 * n), out_shape=[shp] * (4 * n),
        input_output_aliases=aliases,
        compiler_params=_params("parallel"),
    )(*args)
    return [tuple(res[4 * j:4 * j + 4]) for j in range(n)]


def kernel(x, meta, ffn1_wg, ffn1_wu, ffn1_wd, ffn2_wg, ffn2_wu, ffn2_wd, ln_gain, ln_bias, conv_w_in, conv_w, conv_w_out, kv_w, f_bias, attn_w_q, attn_w_o, loss_target, m_meta, m_ffn1_wg, m_ffn1_wu, m_ffn1_wd, m_ffn2_wg, m_ffn2_wu, m_ffn2_wd, m_ln_gain, m_ln_bias, m_conv_w_in, m_conv_w, m_conv_w_out, m_kv_w, m_f_bias, m_attn_w_q, m_attn_w_o, v_meta, v_ffn1_wg, v_ffn1_wu, v_ffn1_wd, v_ffn2_wg, v_ffn2_wu, v_ffn2_wd, v_ln_gain, v_ln_bias, v_conv_w_in, v_conv_w, v_conv_w_out, v_kv_w, v_f_bias, v_attn_w_q, v_attn_w_o):
    seq, d = x.shape[1], x.shape[2]
    t = PAD + N_META + seq
    n_heads = d // HEAD_DIM
    dq = d // N_CHIPS
    n_kv = kv_w.shape[1]
    x2 = x[0]
    target = loss_target[0]

    def rows8(a):
        return jnp.pad(a, ((0, 8 - a.shape[0]), (0, 0)))

    def small_pack(mt, g, b, cw, fb):
        fb_row = jnp.pad(fb, (0, mt.shape[1] - n_heads))[None]
        return jnp.concatenate([mt, rows8(g.reshape(6, -1)), rows8(b.reshape(6, -1)),
                                rows8(cw.reshape(3, -1)), rows8(fb_row)], axis=0)

    w_small = small_pack(meta, ln_gain, ln_bias, conv_w, f_bias)

    cx, cy, c = lax.axis_index("x"), lax.axis_index("y"), lax.axis_index("c")
    idx = jnp.stack([2 * cx + cy, 2 * (1 - cx) + cy, 2 * cx + (1 - cy), 2 * (1 - cx) + (1 - cy), c]
                    ).astype(jnp.int32)

    def tr(a):
        return a.transpose(0, 2, 1)

    w3 = {"wg1": tr(ffn1_wg), "wu1": tr(ffn1_wu), "wd1": ffn1_wd, "wg2": tr(ffn2_wg),
          "wu2": tr(ffn2_wu), "wd2": ffn2_wd, "win": conv_w_in, "wout": conv_w_out, "kv": kv_w[None],
          "wq": attn_w_q, "wo": attn_w_o, "small": w_small[None]}
    transposed = ("wg1", "wu1", "wg2", "wu2")
    buf = {}
    for group in (("wg1", "wu1"), ("small",), ("wd1", "wg2", "wu2", "wd2"), ("win",), ("kv",),
                  ("wout", "wq", "wo")):
        buf.update(zip(group, _place_shard([w3[n] for n in group], idx, "place_shard")))

    def split(item):
        name, layer = item.split(".")
        return name, int(layer)

    def gather_stage(planner, items):
        triples = [(n, l, buf[n].shape[2] // 2) for n, l in map(split, items)]
        plan, n_copies = planner(triples)
        return dict(plan=plan, n=n_copies, bufs={n: buf[n] for n, _, _ in triples})

    def ici(items):
        return gather_stage(_plan_gather_ici, items)

    def d2d(items):
        return gather_stage(_plan_gather_d2d, items)

    def pair_exchange(items):
        r2s = [buf["G_" + it].shape[1] // 2 for it in items]
        plan, n_copies = _plan_pair_exchange(items, r2s)
        news = {"PAIR_" + it: jax.ShapeDtypeStruct((N_CHIPS, r2, buf["G_" + it].shape[2]), f32)
                for it, r2 in zip(items, r2s)}
        return dict(plan=plan, n=n_copies, srcs={"G_" + it: buf["G_" + it] for it in items}, news=news)

    def chip_exchange(items):
        plan, n_copies = _plan_chip_exchange(items)
        srcs = {"SEND_" + it: buf["SEND_" + it] for it in items}
        news = {"RECV_" + it: jax.ShapeDtypeStruct(s.shape, s.dtype)
                for it, s in ((it, buf["SEND_" + it]) for it in items)}
        return dict(plan=plan, n=n_copies, srcs=srcs, news=news)

    def pair_share(items):
        triples = [(n, l, buf["RED_" + n].shape[1] // 2) for n, l in map(split, items)]
        plan, n_copies = _plan_pair_share(triples)
        return dict(plan=plan, n=n_copies, bufs={"RED_" + n: buf["RED_" + n] for n, _, _ in triples})

    def run(fn, *args, stages=(), name=None, **kw):
        comm = _Copies()
        for st in (stages() if callable(stages) else stages):
            comm.add(st["plan"], st["n"], srcs=st.get("srcs"), bufs=st.get("bufs"), news=st.get("news"))
        out = _copy_call(comm, name) if fn is None else fn(*args, comm=comm, **kw)
        buf.update(comm.out_bufs)
        buf.update(comm.out_news)
        return out

    first = ["wg1.0", "wu1.0", "small.0"]
    run(None, stages=[ici(first)], name="gather_first_ici")
    run(None, stages=[d2d(first)], name="gather_first_d2d")
    small = buf["small"].reshape(N_CHIPS, SMALL_ROWS, dq).transpose(1, 0, 2).reshape(SMALL_ROWS, d)
    meta_full = small[:N_META]
    gains = small[16:22].reshape(DEPTH, 3, 1, d)
    biases = small[24:30].reshape(DEPTH, 3, 1, d)
    conv_w_full = small[32:35]
    fb_pad = jnp.pad(f_bias, (0, LANES - n_heads))[None]
    down1 = ["wd1.0"]
    ffn2_l0 = ["win.0", "wout.0", "wg2.0", "wu2.0", "wd2.0", "kv.0"]
    attn_ffn2_l1 = ["wq.0", "wo.0", "wg2.1", "wu2.1", "wd2.1"]
    ffn1_l1 = ["wg1.1", "wu1.1", "wd1.1"]

    meta_pad = jnp.concatenate([jnp.zeros((PAD, d), f32), meta_full], axis=0)
    h0, h0b = run(_embed, meta_pad, x2, "embed", stages=[ici(down1)])
    a1, b1, s1 = run(_ffn_up, h0b, buf["wg1"], buf["wu1"], 0, "ffn_up",
                     stages=lambda: [d2d(down1), ici(ffn2_l0)])
    r1, h1, h1b = run(_down_ln, s1, buf["wd1"], 0, h0, gains[0, 0], biases[0, 0], 0.5, "ffn_down_ln",
                      stages=lambda: [d2d(ffn2_l0), ici(attn_ffn2_l1)])
    n_in = conv_w_in.shape[-1]
    w_in = buf["win"].reshape(N_CHIPS, d, n_in)
    w_out = buf["wout"].reshape(1, 1, d, d)
    p = run(_nn_matmul, h1b, w_in, f32, "conv_in", stages=lambda: [d2d(attn_ffn2_l1), ici(ffn1_l1)])
    z = _conv_fwd(p, conv_w_full, "conv_fwd")
    r2, h2, h2b = run(_down_ln, z[None], w_out, 0, h1, gains[0, 1], biases[0, 1], 1.0, "mix_out_ln",
                      stages=lambda: [d2d(ffn1_l1)])
    wg1, wu1, wd1, wg2, wu2, wd2 = (buf[n] for n in ("wg1", "wu1", "wd1", "wg2", "wu2", "wd2"))
    w_q = buf["wq"].reshape(1, d, d)
    w_o = buf["wo"].reshape(1, 1, d, d)
    kv_full = buf["kv"].reshape(N_CHIPS, d, n_kv).transpose(1, 0, 2).reshape(d, N_CHIPS * n_kv)
    w_k = kv_full[:, :d][None]
    w_v = kv_full[:, d:2 * d][None]
    w_f = jnp.pad(kv_full[:, 2 * d:], ((0, 0), (0, LANES - n_heads)))[None]
    a2, b2, s2 = _ffn_up(h2b, wg2, wu2, 0, "ffn_up")
    r3, h3, h3b = _down_ln(s2, wd2, 0, h2, gains[0, 2], biases[0, 2], 0.5, "ffn_down_ln")
    kk = _nn_matmul(h3b, w_k, bf16, "proj_bf16")
    vv = _nn_matmul(h3b, w_v, bf16, "proj_bf16")
    flog = _nn_matmul(h3b, w_f, f32, "proj_gate")
    cum = _row_scan(flog, "gate_cumsum", fb_pad)
    bk = _tile(t, 640, LANES)
    c_ht = cum[:, :n_heads].T
    c_keys = jnp.where(jnp.arange(t)[None, :] < PAD, 1e30, c_ht)
    cq_rep = jnp.broadcast_to(c_ht[:, :, None], (n_heads, t, LANES))
    ck_rep = jnp.broadcast_to(c_keys[:, :, None], (n_heads, t, LANES))
    ck_rows = c_keys.reshape(n_heads, t // bk, 1, bk)
    a3, b3, s3 = _ffn_up(h3b, wg1, wu1, 1, "ffn_up")
    r4, h4, h4b = _down_ln(s3, wd1, 1, h3, gains[1, 0], biases[1, 0], 0.5, "ffn_down_ln")
    q = _nn_matmul(h4b, w_q, bf16, "proj_q", out_scale=1.0 / math.sqrt(HEAD_DIM))
    o, o32, cl = _attn_fwd(q, kk, vv, cq_rep, ck_rows, "attn_fwd")
    r5, h5, h5b = _down_ln(o[None], w_o, 0, h4, gains[1, 1], biases[1, 1], 1.0, "mix_out_ln")
    a4, b4, s4 = _ffn_up(h5b, wg2, wu2, 1, "ffn_up")
    r6, h6, _ = _down_ln(s4, wd2, 1, h5, gains[1, 2], biases[1, 2], 0.5, "ffn_down_ln")
    dr6, dr6b, dg12, db12, sq = _loss_head(h6, target, r6, gains[1, 2], "loss_head")
    loss_part = 0.5 * sq[0, 0] / d

    m_small = small_pack(m_meta, m_ln_gain, m_ln_bias, m_conv_w, m_f_bias)
    v_small = small_pack(v_meta, v_ln_gain, v_ln_bias, v_conv_w, v_f_bias)
    m3 = {"wg1": tr(m_ffn1_wg), "wu1": tr(m_ffn1_wu), "wd1": m_ffn1_wd, "wg2": tr(m_ffn2_wg),
          "wu2": tr(m_ffn2_wu), "wd2": m_ffn2_wd, "win": m_conv_w_in, "wout": m_conv_w_out,
          "kv": m_kv_w[None], "wq": m_attn_w_q, "wo": m_attn_w_o, "small": m_small[None]}
    v3 = {"wg1": tr(v_ffn1_wg), "wu1": tr(v_ffn1_wu), "wd1": v_ffn1_wd, "wg2": tr(v_ffn2_wg),
          "wu2": tr(v_ffn2_wu), "wd2": v_ffn2_wd, "win": v_conv_w_in, "wout": v_conv_w_out,
          "kv": v_kv_w[None], "wq": v_attn_w_q, "wo": v_attn_w_o, "small": v_small[None]}
    stepped = {}

    def alike(items):
        groups = {}
        for it in items:
            n, l = split(it)
            groups.setdefault((w3[n].shape, l, ("RED_" + n) in buf, n in stepped), []).append(it)
        return groups.values()

    def prepare(items):
        for group in alike(items):
            sends = _rs_prepare([buf["G_" + it] for it in group], [buf["PAIR_" + it] for it in group],
                                idx, "grad_prepare")
            buf.update({"SEND_" + it: s for it, s in zip(group, sends)})

    def finish(items):
        for group in alike(items):
            names = [split(it)[0] for it in group]
            layer = split(group[0])[1]
            intos = [buf["RED_" + n] for n in names] if ("RED_" + names[0]) in buf else None
            reds = _rs_finish([buf["G_" + it] for it in group], [buf["PAIR_" + it] for it in group],
                              [buf["RECV_" + it] for it in group], idx, layer, w3[names[0]].shape[0],
                              intos, "grad_finish")
            buf.update({"RED_" + n: r for n, r in zip(names, reds)})

    def adam(items, grads=None):
        for group in alike(items):
            names = [split(it)[0] for it in group]
            layer = split(group[0])[1]
            gs = [buf["RED_" + n] if grads is None else grads[n] for n in names]
            intos = [stepped[n] for n in names] if names[0] in stepped else None
            res = _adamw([w3[n] for n in names], gs, [m3[n] for n in names], [v3[n] for n in names],
                         layer, intos, "adamw")
            stepped.update(dict(zip(names, res)))

    def ffn_bwd(dr, drb, hb_in, a, b, s, f, layer, on_act=(), after_act=None, on_dwd=(),
                after_dwd=None, on_dx=(), ln=None):
        da, db = run(_ffn_bwd_act, drb, buf["wd" + f], layer, a, b, "ffn_bwd_act", stages=on_act)
        if after_act is not None:
            after_act()
        (buf[f"G_wd{f}.{layer}"],) = run(_tn_matmul, [(s, "stack", drb, "shared")], N_CHIPS, 0.5,
                                         "ffn_dwd", stages=on_dwd)
        if after_dwd is not None:
            after_dwd()
        buf[f"G_wg{f}.{layer}"], buf[f"G_wu{f}.{layer}"] = _tn_matmul(
            [(da, "stack", hb_in, "shared"), (db, "stack", hb_in, "shared")], N_CHIPS, 1.0, "ffn_dwgu")
        return run(_nt_sum, [(da, "stack", buf["wg" + f], layer, True),
                             (db, "stack", buf["wu" + f], layer, True)],
                   dr, ALPHA, N_CHIPS, f32, "ffn_dx", stages=on_dx, ln=ln)

    ffn2_1 = ["wg2.1", "wu2.1", "wd2.1"]
    ffn1_1 = ["wg1.1", "wu1.1", "wd1.1"]
    ffn2_0 = ["wg2.0", "wu2.0", "wd2.0"]
    conv_items = ["wout.0", "win.0"]

    dr5, dr5b, dg11, db11 = ffn_bwd(dr6, dr6b, h5b, a4, b4, s4, "2", 1,
                                    on_dx=lambda: [pair_exchange(ffn2_1)], ln=(r5, gains[1, 1]))
    prepare(ffn2_1)
    (dwo,) = _tn_matmul([(o, "shared", dr5b, "shared")], 1, 1.0, "sq_dw")
    buf["G_wo.0"] = dwo.reshape(N_CHIPS, dq, d)
    do, delta = run(_attn_do, dr5b, w_o[0, 0], o32, bk, "attn_do",
                    stages=lambda: [pair_exchange(["wo.0"])])
    prepare(["wo.0"])
    kt = kk.reshape(t // bk, bk, n_heads, HEAD_DIM).transpose(2, 0, 3, 1)
    dq_att, dc_q, dk, dv, dc_k = run(_attn_bwd, q, kk, vv, kt, do, ck_rep, cl, delta, "attn_bwd",
                                     stages=lambda: [chip_exchange(ffn2_1 + ["wo.0"])])
    finish(ffn2_1 + ["wo.0"])
    dc = (dc_q + dc_k).reshape(n_heads, t)
    (dwq,) = run(_tn_matmul, [(h4b, "shared", dq_att, "shared")], 1, 1.0, "sq_dw",
                 stages=lambda: [pair_share(ffn2_1 + ["wo.0"])])
    buf["G_wq.0"] = dwq.reshape(N_CHIPS, dq, d)
    adam(ffn2_1 + ["wo.0"])
    dr4, dr4b, dg10, db10 = run(_nt_sum, [(dq_att, "cols", w_q, None)], dr5, ALPHA, 1, f32,
                                "sq_dx_res", stages=lambda: [pair_exchange(["wq.0"])],
                                ln=(r4, gains[1, 0]))
    prepare(["wq.0"])
    dh3a = ffn_bwd(dr4, dr4b, h3b, a3, b3, s3, "1", 1,
                   on_act=lambda: [chip_exchange(["wq.0"])],
                   on_dx=lambda: [pair_exchange(ffn1_1)])
    prepare(ffn1_1)
    finish(["wq.0"])
    dc_t = jnp.pad(dc.T, ((0, 0), (0, LANES - n_heads)))
    dlogf = _row_scan(dc_t, "rev_cumsum", reverse=True)
    dfl, dfb_cols = _fgate_bwd(dlogf, flog, fb_pad, "gate_bwd")
    dwk, dwv, dwf = run(_tn_matmul, [(h3b, "shared", g, "shared") for g in (dk, dv, dfl)], 1, 1.0,
                        "kv_dw",
                        stages=lambda: [chip_exchange(ffn1_1), pair_share(["wq.0"])])
    adam(["wq.0"])

    def by_chip(full):
        rows = full.shape[0]
        return full.reshape(rows, N_CHIPS, full.shape[1] // N_CHIPS).transpose(1, 0, 2)

    buf["G_kv.0"] = by_chip(jnp.concatenate([dwk[0], dwv[0], dwf[0][:, :n_heads]], axis=1))
    dr3, dr3b, dg02, db02 = run(
        _nt_sum, [(dk, "cols", w_k, None), (dv, "cols", w_v, None), (dfl, "cols", w_f, None)],
        dh3a, 1.0, 1, f32, "kv_dx", stages=lambda: [pair_exchange(["kv.0"])], ln=(r3, gains[0, 2]))
    prepare(["kv.0"])
    finish(ffn1_1)
    dr2, dr2b, dg01, db01 = ffn_bwd(dr3, dr3b, h2b, a2, b2, s2, "2", 0,
                                    on_act=lambda: [chip_exchange(["kv.0"]), pair_share(ffn1_1)],
                                    after_act=lambda: (adam(ffn1_1), finish(["kv.0"])),
                                    on_dwd=lambda: [pair_share(["kv.0"])],
                                    after_dwd=lambda: adam(["kv.0"]),
                                    on_dx=lambda: [pair_exchange(ffn2_0)], ln=(r2, gains[0, 1]))
    prepare(ffn2_0)
    (dwout,) = _tn_matmul([(z, "shared", dr2b, "shared")], 1, 1.0, "sq_dw")
    buf["G_wout.0"] = dwout.reshape(N_CHIPS, dq, d)
    dz = _nt_sum([(dr2b, "cols", w_out[0], None)], None, 1.0, 1, f32, "sq_dx_f32")
    dp, dconv_w = run(_conv_bwd, dz, p, conv_w_full, "conv_bwd",
                      stages=lambda: [chip_exchange(ffn2_0)])
    (buf["G_win.0"],) = _tn_matmul([(h1b, "shared", dp, "cols")], N_CHIPS, 1.0, "conv_dwin")
    finish(ffn2_0)
    dr1, dr1b, dg00, db00 = run(_nt_sum, [(dp, "cols", w_in, None)], dr2, ALPHA, N_CHIPS, f32,
                                "conv_dx",
                                stages=lambda: [pair_exchange(conv_items), pair_share(ffn2_0)],
                                ln=(r1, gains[0, 0]))
    prepare(conv_items)
    adam(ffn2_0)
    gate_up = ["wg1.0", "wu1.0"]
    da, db = run(_ffn_bwd_act, dr1b, buf["wd1"], 0, a1, b1, "ffn_bwd_act",
                 stages=lambda: [chip_exchange(conv_items)])
    finish(conv_items)
    buf["G_wg1.0"], buf["G_wu1.0"] = run(
        _tn_matmul, [(da, "stack", h0b, "shared"), (db, "stack", h0b, "shared")], N_CHIPS, 1.0,
        "ffn_dwgu", stages=lambda: [pair_share(conv_items)])
    adam(conv_items)
    (buf["G_wd1.0"],) = run(_tn_matmul, [(s1, "stack", dr1b, "shared")], N_CHIPS, 0.5, "ffn_dwd",
                            stages=lambda: [pair_exchange(gate_up)])
    prepare(gate_up)
    dh0 = run(_nt_sum, [(da, "stack", buf["wg1"], 0, True), (db, "stack", buf["wu1"], 0, True)],
              dr1, ALPHA, N_CHIPS, f32, "ffn_dx",
              stages=lambda: [chip_exchange(gate_up), pair_exchange(["wd1.0"])])
    prepare(["wd1.0"])
    finish(gate_up)
    grad_x = dh0[PAD + N_META:][None]
    dmeta = dh0[PAD:PAD + N_META]
    buf["G_small.0"] = by_chip(jnp.concatenate(
        [dmeta, rows8(jnp.concatenate([dg00, dg01, dg02, dg10, dg11, dg12], axis=0)),
         rows8(jnp.concatenate([db00, db01, db02, db10, db11, db12], axis=0)),
         rows8(dconv_w), jnp.zeros((8, d), f32)], axis=0))
    run(None, stages=lambda: [chip_exchange(["wd1.0"]), pair_exchange(["small.0"]), pair_share(gate_up)],
        name="grad_tail_1")
    prepare(["small.0"])
    finish(["wd1.0"])
    adam(gate_up)
    run(None, stages=lambda: [chip_exchange(["small.0"]), pair_share(["wd1.0"])], name="grad_tail_2")
    finish(["small.0"])
    adam(["wd1.0"])
    run(None, stages=lambda: [pair_share(["small.0"])], name="grad_tail_3")

    tail = jnp.zeros((LANES,), f32).at[:n_heads].set(dfb_cols[0, :n_heads]).at[n_heads].set(loss_part)
    tail = lax.psum(tail, ("x", "y", "c"))
    loss = tail[n_heads]
    g_fb = tail[:n_heads]
    g_small = jnp.concatenate([buf["RED_small"][0, :40],
                               rows8(jnp.pad(g_fb, (0, dq - n_heads))[None])], axis=0)
    adam(["small.0"], grads={"small": g_small[None]})

    def unpack(pk):
        return (pk[:16], pk[16:22].reshape(DEPTH, 3, dq), pk[24:30].reshape(DEPTH, 3, dq),
                pk[32:35].reshape(1, 3, dq), pk[40, :n_heads])

    def order(pick):
        mt, g, b, cw, fb = unpack(pick("small")[0])
        big = {n: pick(n) for n in w3 if n != "small"}
        big["kv"] = big["kv"][0]
        for n in transposed:
            big[n] = tr(big[n])
        return [mt, big["wg1"], big["wu1"], big["wd1"], big["wg2"], big["wu2"], big["wd2"], g, b,
                big["win"], cw, big["wout"], big["kv"], fb, big["wq"], big["wo"]]

    return (loss, grad_x, *order(lambda n: stepped[n][3]), *order(lambda n: stepped[n][0]),
            *order(lambda n: stepped[n][1]), *order(lambda n: stepped[n][2]))
```
